```python
import jax, jax.numpy as jnp
from jax import lax
import numpy as np

D_MODEL = 2048
BATCH = 2
SEQ = 4096
DEPTH = 2
DEC_BATCH = 8
DEC_SEQ = 1
PAST_LEN = 16384
PAGE_SIZE = 128

HEAD_DIM = 128
GLA_HEADS = 4
GLA_DK = 64
GLA_DV = 128
GLA_RANK = 16
GLA_GATE_NORM = 16.0
GLA_CHUNK = 64
GLA_WIDTH = GLA_HEADS * GLA_DV
NSA_HEADS = 8
NSA_KV_HEADS = 2
NSA_REP = NSA_HEADS // NSA_KV_HEADS
NSA_WIDTH = NSA_HEADS * HEAD_DIM
CMP_LEN = 32
CMP_STRIDE = 16
CMP_HIDDEN = 128
SEL_BLOCK = 64
SEL_TOPN = 16
WINDOW = 512
Q_BLOCK = 128
POOL_GROUPS = 4
POOL_GROUP_DIM = 128
POOL_WIDTH = POOL_GROUPS * POOL_GROUP_DIM
POOL_WINDOWS = (2, 4, 8, 16)
POOL_MAX = 16
MIX_WIDTH = GLA_WIDTH + NSA_WIDTH + POOL_WIDTH
ROPE_THETA = 500000.0
ROPE_DIM = HEAD_DIM // 4
D_FF = 5504
N_EXPERTS = 8
TOP_K = 2
D_FF_EXPERT = 7168
MOE_ROW_BLOCK = 128
MOE_MIN_BLOCK = 8
N_DENSE = (DEPTH + 1) // 2
N_MOE = DEPTH // 2
ALPHA = (2 * DEPTH) ** 0.25
BETA = (8 * DEPTH) ** -0.25
LN_EPS = 1e-5
RMS_EPS = 1e-6

PROJ_SIZES = (
    ('gla_q', GLA_HEADS * GLA_DK), ('gla_k', GLA_HEADS * GLA_DK), ('gla_v', GLA_HEADS * GLA_DV),
    ('gla_glr', GLA_RANK), ('gla_r', GLA_HEADS * GLA_DV),
    ('nsa_q', NSA_HEADS * HEAD_DIM),
    ('cmp_k', NSA_KV_HEADS * HEAD_DIM), ('cmp_v', NSA_KV_HEADS * HEAD_DIM),
    ('slc_k', NSA_KV_HEADS * HEAD_DIM), ('slc_v', NSA_KV_HEADS * HEAD_DIM),
    ('win_k', NSA_KV_HEADS * HEAD_DIM), ('win_v', NSA_KV_HEADS * HEAD_DIM),
    ('nsa_gate', 3 * NSA_HEADS),
    ('pool', POOL_WIDTH),
)
PROJ_WIDTH = sum(s for _, s in PROJ_SIZES)

kernel_name = 'hymba_gla_nsa_pool_deepnorm_moe_step'


def split_proj(p):
    out = {}
    off = 0
    for name, size in PROJ_SIZES:
        out[name] = p[..., off:off + size]
        off += size
    return out


def layer_norm(x, g, b):
    xf = x.astype(jnp.float32)
    xc = xf - jnp.mean(xf, -1, keepdims=True)
    var = jnp.mean(xc * xc, -1, keepdims=True)
    return (xc * lax.rsqrt(var + LN_EPS) * g + b).astype(x.dtype)


def rope(x, pos):
    half = ROPE_DIM // 2
    inv_freq = ROPE_THETA ** (-jnp.arange(half, dtype=jnp.float32) / half)
    ang = pos.astype(jnp.float32)[:, None] * inv_freq[None, :]
    cos = jnp.cos(ang)[:, None, :]
    sin = jnp.sin(ang)[:, None, :]
    xf = x.astype(jnp.float32)
    x1, x2 = xf[..., :half], xf[..., half:ROPE_DIM]
    out = jnp.concatenate([x1 * cos - x2 * sin, x2 * cos + x1 * sin, xf[..., ROPE_DIM:]], axis=-1)
    return out.astype(x.dtype)


def masked_softmax(s, mask):
    s = jnp.where(mask, s.astype(jnp.float32), -jnp.inf)
    m = jnp.max(s, axis=-1, keepdims=True)
    m = jnp.where(jnp.isfinite(m), m, 0.0)
    p = jnp.where(mask, jnp.exp(s - m), 0.0)
    return p / jnp.maximum(jnp.sum(p, -1, keepdims=True), 1e-30)


def gla_recurrence(q, k, v, g, s0):
    B, T, H, _ = q.shape
    C = GLA_CHUNK
    n_chunks = -(-T // C)
    pad = n_chunks * C - T

    def prep(a):
        a = jnp.pad(a, ((0, 0), (0, pad), (0, 0), (0, 0)))
        return a.reshape(B, n_chunks, C, H, a.shape[-1]).transpose(1, 0, 3, 2, 4)

    causal = jnp.tril(jnp.ones((C, C), dtype=bool))

    def step(S, inp):
        qi, ki, vi, gi = [a.astype(jnp.float32) for a in inp]
        b = jnp.cumsum(gi, axis=2)
        o_inter = jnp.einsum('bhtk,bhkv->bhtv', qi * jnp.exp(b), S)
        diff = jnp.where(causal[:, :, None], b[:, :, :, None, :] - b[:, :, None, :, :], -jnp.inf)
        attn = jnp.einsum('bhtk,bhsk,bhtsk->bhts', qi, ki, jnp.exp(diff))
        o = o_inter + jnp.einsum('bhts,bhsv->bhtv', attn, vi)
        b_last = b[:, :, -1:, :]
        S = jnp.exp(b_last[:, :, 0, :])[..., None] * S + jnp.einsum('bhsk,bhsv->bhkv', ki * jnp.exp(b_last - b), vi)
        return S, o

    S, o = lax.scan(step, s0.astype(jnp.float32), (prep(q), prep(k), prep(v), prep(g)))
    o = o.transpose(1, 0, 3, 2, 4).reshape(B, n_chunks * C, H, v.shape[-1])[:, :T]
    return o, S.astype(s0.dtype)


def gla_mixer(parts, s0, w2, b2, norm_g):
    B, T = parts['gla_q'].shape[:2]
    q = parts['gla_q'].reshape(B, T, GLA_HEADS, GLA_DK) * (GLA_DK ** -0.5)
    k = parts['gla_k'].reshape(B, T, GLA_HEADS, GLA_DK)
    v = parts['gla_v'].reshape(B, T, GLA_HEADS, GLA_DV)
    g = jax.nn.log_sigmoid((parts['gla_glr'] @ w2 + b2).astype(jnp.float32)) / GLA_GATE_NORM
    g = g.reshape(B, T, GLA_HEADS, GLA_DK)
    o, s_new = gla_recurrence(q, k, v, g, s0)
    o = o * lax.rsqrt(jnp.mean(o * o, -1, keepdims=True) + RMS_EPS)
    out = o.reshape(B, T, GLA_WIDTH) * norm_g * jax.nn.silu(parts['gla_r'].astype(jnp.float32))
    return out.astype(parts['gla_v'].dtype), s_new


def nsa_compress(kx, pos_emb, w1, w2):
    B, T, G, D = kx.shape
    nh = T // CMP_STRIDE
    halves = kx[:, :nh * CMP_STRIDE].reshape(B, nh, CMP_STRIDE, G, D).astype(jnp.float32)
    pe = pos_emb.reshape(2, CMP_STRIDE, D)
    w = w1.reshape(2, CMP_STRIDE, D, CMP_HIDDEN)
    h_lo = jnp.einsum('bnjgd,jdh->bngh', halves + pe[0][None, None, :, None, :], w[0])
    h_hi = jnp.einsum('bnjgd,jdh->bngh', halves + pe[1][None, None, :, None, :], w[1])
    h = jax.nn.gelu(h_lo[:, :-1] + h_hi[:, 1:])
    return jnp.einsum('bngh,hd->bngd', h, w2)


def nsa_global(qn, qr, rows, q_pos, cmp_pos, cmp_w1, cmp_w2):
    B, Tq, G, R, D = qn.shape
    Tk = rows.shape[1]
    scale = HEAD_DIM ** -0.5
    kcmp = nsa_compress(rows[:, :, 0], cmp_pos[0], cmp_w1[0], cmp_w2[0])
    vcmp = nsa_compress(rows[:, :, 1], cmp_pos[1], cmp_w1[1], cmp_w2[1])
    n_cmp = kcmp.shape[1]
    cmp_end = jnp.arange(n_cmp) * CMP_STRIDE + CMP_LEN - 1
    n_sel = -(-Tk // SEL_BLOCK)
    pad = n_sel * SEL_BLOCK - Tk

    def to_blocks(a):
        a = jnp.pad(a, ((0, 0), (0, pad), (0, 0), (0, 0)))
        return a.reshape(B, n_sel, SEL_BLOCK, G, D).transpose(0, 3, 1, 2, 4)

    ksb = to_blocks(rows[:, :, 2])
    vsb = to_blocks(rows[:, :, 3])
    ci = jnp.arange(n_cmp)[:, None]
    sj = jnp.arange(n_sel)[None, :]
    overlap = ((ci * CMP_STRIDE <= sj * SEL_BLOCK + SEL_BLOCK - 1) &
               (ci * CMP_STRIDE + CMP_LEN - 1 >= sj * SEL_BLOCK)).astype(jnp.float32)
    n_top = min(SEL_TOPN, n_sel)
    gather = jax.vmap(jax.vmap(lambda blocks, idx: blocks[idx]))
    blk_ids = jnp.arange(n_sel)

    def block_fn(args):
        qnb, qrb, qp = args
        qb = qp.shape[0]
        s = jnp.einsum('bqgrd,bngd->bgrqn', qnb, kcmp) * scale
        p_c = masked_softmax(s, cmp_end[None, :] <= qp[:, None])
        o_c = jnp.einsum('bgrqn,bngd->bqgrd', p_c, vcmp)
        imp = jnp.einsum('bgqn,nj->bgqj', jnp.sum(p_c, axis=2), overlap)
        cur = qp[:, None] // SEL_BLOCK
        valid = blk_ids[None, :] <= cur
        forced = (blk_ids[None, :] == 0) | (blk_ids[None, :] == cur) | (blk_ids[None, :] == cur - 1)
        imp = jnp.where(valid, jnp.where(forced, jnp.inf, imp), -jnp.inf)
        top_s, top_i = lax.top_k(imp, n_top)
        kg = gather(ksb, top_i)
        vg = gather(vsb, top_i)
        kpos = top_i[..., None] * SEL_BLOCK + jnp.arange(SEL_BLOCK)
        mask = (top_s > -jnp.inf)[..., None] & (kpos <= qp[None, None, :, None, None])
        s2 = jnp.einsum('bqgrd,bgqnjd->bgrqnj', qrb, kg) * scale
        s2 = s2.reshape(B, G, R, qb, n_top * SEL_BLOCK)
        p_s = masked_softmax(s2, mask.reshape(B, G, 1, qb, n_top * SEL_BLOCK))
        p_s = p_s.reshape(B, G, R, qb, n_top, SEL_BLOCK)
        o_s = jnp.einsum('bgrqnj,bgqnjd->bqgrd', p_s, vg.astype(jnp.float32))
        return o_c, o_s

    qb = min(Tq, Q_BLOCK)
    nq = -(-Tq // qb)
    padq = nq * qb - Tq
    qpad = ((0, 0), (0, padq), (0, 0), (0, 0), (0, 0))
    qn_b = jnp.pad(qn, qpad).reshape(B, nq, qb, G, R, D).swapaxes(0, 1)
    qr_b = jnp.pad(qr, qpad).reshape(B, nq, qb, G, R, D).swapaxes(0, 1)
    qp_b = jnp.pad(q_pos, (0, padq), mode='edge').reshape(nq, qb)
    o_c, o_s = lax.map(block_fn, (qn_b, qr_b, qp_b))
    o_c = o_c.swapaxes(0, 1).reshape(B, nq * qb, G, R, D)[:, :Tq]
    o_s = o_s.swapaxes(0, 1).reshape(B, nq * qb, G, R, D)[:, :Tq]
    return o_c, o_s


def band_attend(q, k, v, qpos, kpos):
    s = jnp.einsum('bqgrd,bkgd->bgrqk', q, k) * (HEAD_DIM ** -0.5)
    mask = ((kpos[None, :] <= qpos[:, None]) & (kpos[None, :] > qpos[:, None] - WINDOW) & (kpos[None, :] >= 0))
    p = masked_softmax(s, mask)
    return jnp.einsum('bgrqk,bkgd->bqgrd', p, v.astype(jnp.float32))


def sliding_prompt(q, k, v):
    B, T, G, R, D = q.shape
    kp = jnp.pad(k, ((0, 0), (WINDOW, 0), (0, 0), (0, 0)))
    vp = jnp.pad(v, ((0, 0), (WINDOW, 0), (0, 0), (0, 0)))
    nq = T // Q_BLOCK

    def fn(i):
        q0 = i * Q_BLOCK
        qi = lax.dynamic_slice_in_dim(q, q0, Q_BLOCK, axis=1)
        ki = lax.dynamic_slice_in_dim(kp, q0, WINDOW + Q_BLOCK, axis=1)
        vi = lax.dynamic_slice_in_dim(vp, q0, WINDOW + Q_BLOCK, axis=1)
        qpos = q0 + jnp.arange(Q_BLOCK)
        kpos = q0 - WINDOW + jnp.arange(WINDOW + Q_BLOCK)
        return band_attend(qi, ki, vi, qpos, kpos)

    o = lax.map(fn, jnp.arange(nq))
    return o.swapaxes(0, 1).reshape(B, T, G, R, D)


def nsa_mixer(parts, pos, nsa_past, win_past, cmp_pos, cmp_w1, cmp_w2):
    B, T = parts['nsa_q'].shape[:2]
    dt = parts['nsa_q'].dtype
    q = parts['nsa_q'].reshape(B, T, NSA_HEADS, HEAD_DIM)
    q_rope = rope(q, pos)
    kv = lambda name: parts[name].reshape(B, T, NSA_KV_HEADS, HEAD_DIM)
    k_win, v_win = rope(kv('win_k'), pos), kv('win_v')
    new_rows = jnp.stack([kv('cmp_k'), kv('cmp_v'), rope(kv('slc_k'), pos), kv('slc_v')], axis=2)
    rows = new_rows if nsa_past is None else jnp.concatenate([nsa_past.astype(dt), new_rows], axis=1)
    qg = q.reshape(B, T, NSA_KV_HEADS, NSA_REP, HEAD_DIM)
    qrg = q_rope.reshape(B, T, NSA_KV_HEADS, NSA_REP, HEAD_DIM)
    o_cmp, o_slc = nsa_global(qg, qrg, rows, pos, cmp_pos, cmp_w1, cmp_w2)
    win_rows = jnp.stack([k_win, v_win], axis=2)
    if win_past is None:
        o_win = sliding_prompt(qrg, k_win, v_win)
        new_win = win_rows[:, T - min(WINDOW, T):]
    else:
        n_buf = win_past.shape[1]
        ext = jnp.concatenate([win_past.astype(dt), win_rows], axis=1)
        kpos = pos[0] - n_buf + jnp.arange(n_buf + T)
        o_win = band_attend(qrg, ext[:, :, 0], ext[:, :, 1], pos, kpos)
        new_win = ext[:, T:]
    gates = jax.nn.sigmoid(parts['nsa_gate'].astype(jnp.float32)).reshape(B, T, NSA_KV_HEADS, NSA_REP, 3)
    o = gates[..., 0:1] * o_cmp + gates[..., 1:2] * o_slc + gates[..., 2:3] * o_win
    return o.reshape(B, T, NSA_WIDTH).astype(dt), new_rows, new_win


def pool_mixer(u, prev, pos, w_pool, scale):
    B, T, C = u.shape
    P = POOL_MAX - 1
    ext = jnp.concatenate([prev.astype(u.dtype), u], axis=1).astype(jnp.float32)
    cs = jnp.concatenate([jnp.zeros((B, 1, C), jnp.float32), jnp.cumsum(ext, axis=1)], axis=1)
    end = cs[:, P + 1:]
    means = []
    for gi, w in enumerate(POOL_WINDOWS):
        sl = slice(gi * POOL_GROUP_DIM, (gi + 1) * POOL_GROUP_DIM)
        start = cs[:, P + 1 - w:P + 1 - w + T, sl]
        cnt = jnp.minimum(pos + 1, w).astype(jnp.float32)[None, :, None]
        means.append((end[..., sl] - start) / cnt)
    pooled = (jnp.concatenate(means, axis=-1) - ext[:, P:]).reshape(B, T, POOL_GROUPS, POOL_GROUP_DIM)
    y = jnp.einsum('btgc,gcd->btgd', pooled, w_pool.astype(jnp.float32)).reshape(B, T, C) * scale
    return y.astype(u.dtype), ext[:, -P:].astype(u.dtype)


def mixer_layer(x, pos0, gla_s0, pool_prev, nsa_past, win_past,
                w_in, gla_w2, gla_b, gla_norm_g, cmp_pos, cmp_w1, cmp_w2, pool_w, pool_scale, w_out):
    B, T, _ = x.shape
    pos = pos0 + jnp.arange(T, dtype=jnp.int32)
    parts = split_proj(x @ w_in)
    y_gla, s_gla = gla_mixer(parts, gla_s0, gla_w2, gla_b, gla_norm_g)
    y_nsa, nsa_rows, win_rows = nsa_mixer(parts, pos, nsa_past, win_past, cmp_pos, cmp_w1, cmp_w2)
    y_pool, pool_rows = pool_mixer(parts['pool'], pool_prev, pos, pool_w, pool_scale)
    y = jnp.concatenate([y_gla, y_nsa, y_pool], axis=-1) @ w_out
    return y, nsa_rows, win_rows, s_gla, pool_rows


def swiglu(x, wg, wu, wd):
    return (jax.nn.silu(x @ wg) * (x @ wu)) @ wd


def moe_ffn(x, router, wg, wu, wd):
    B, T, D = x.shape
    n_tok = B * T
    xt = x.reshape(n_tok, D)
    logits = (xt @ router).astype(jnp.float32)
    top_v, top_i = lax.top_k(logits, TOP_K)
    gates = jax.nn.softmax(top_v, axis=-1)
    n_asg = n_tok * TOP_K
    e_flat = top_i.reshape(n_asg)
    tok_flat = jnp.arange(n_asg, dtype=jnp.int32) // TOP_K
    gate_flat = gates.reshape(n_asg)
    blk = MOE_ROW_BLOCK if n_asg >= N_EXPERTS * MOE_ROW_BLOCK else MOE_MIN_BLOCK
    n_blk = -(-(n_asg + N_EXPERTS * (blk - 1)) // blk)
    order = jnp.argsort(e_flat)
    e_sorted = e_flat[order]
    counts = jnp.bincount(e_flat, length=N_EXPERTS)
    padded = (counts + blk - 1) // blk * blk
    pad_end = jnp.cumsum(padded)
    pad_start = pad_end - padded
    start = jnp.cumsum(counts) - counts
    dest = pad_start[e_sorted] + jnp.arange(n_asg) - start[e_sorted]
    row_tok = jnp.zeros((n_blk * blk,), jnp.int32).at[dest].set(tok_flat[order])
    row_gate = jnp.zeros((n_blk * blk,), jnp.float32).at[dest].set(gate_flat[order])
    blk_expert = jnp.minimum(jnp.searchsorted(pad_end, jnp.arange(n_blk) * blk, side='right'), N_EXPERTS - 1)

    def expert_block(args):
        rows, e = args
        xb = xt[rows]
        return (jax.nn.silu(xb @ wg[e]) * (xb @ wu[e])) @ wd[e]

    out = lax.map(expert_block, (row_tok.reshape(n_blk, blk), blk_expert))
    y = jnp.zeros((n_tok, D), jnp.float32).at[row_tok].add(
        out.reshape(n_blk * blk, D).astype(jnp.float32) * row_gate[:, None])
    return y.reshape(B, T, D).astype(x.dtype)


def setup_inputs(seed: int = 0) -> dict:
    key = jax.random.key(seed)
    ks = jax.random.split(key, 40)
    nrm = lambda k, shape, s=1.0: jax.random.normal(k, shape, jnp.float32) * s
    n_pages = PAST_LEN // PAGE_SIZE
    n_used = DEC_BATCH * n_pages
    n_pool = n_used + max(1, n_used // 4)
    sw = min(WINDOW, PAST_LEN)
    page_table = jax.random.permutation(ks[3], n_pool)[:n_used].reshape(DEC_BATCH, n_pages).astype(jnp.int32)
    return {
        'x_prompt': nrm(ks[0], (BATCH, SEQ, D_MODEL)),
        'x_sample': nrm(ks[1], (DEC_BATCH, DEC_SEQ, D_MODEL)),
        'cache_nsa': nrm(ks[2], (DEPTH, n_pool, PAGE_SIZE, 4, NSA_KV_HEADS, HEAD_DIM)),
        'page_table': page_table,
        'state_win': nrm(ks[4], (DEPTH, DEC_BATCH, sw, 2, NSA_KV_HEADS, HEAD_DIM)),
        'state_gla': nrm(ks[5], (DEPTH, DEC_BATCH, GLA_HEADS, GLA_DK, GLA_DV), 0.5),
        'state_pool': nrm(ks[6], (DEPTH, DEC_BATCH, POOL_MAX - 1, POOL_WIDTH)),
        'w_in': nrm(ks[7], (DEPTH, D_MODEL, PROJ_WIDTH), D_MODEL ** -0.5),
        'gla_gate_w2': nrm(ks[8], (DEPTH, GLA_RANK, GLA_HEADS * GLA_DK), GLA_RANK ** -0.5),
        'gla_gate_b': nrm(ks[9], (DEPTH, GLA_HEADS * GLA_DK), 0.1),
        'gla_norm_g': 1.0 + nrm(ks[10], (DEPTH, GLA_WIDTH), 0.02),
        'nsa_cmp_pos': nrm(ks[11], (DEPTH, 2, CMP_LEN, HEAD_DIM), 0.02),
        'nsa_cmp_w1': nrm(ks[12], (DEPTH, 2, CMP_LEN, HEAD_DIM, CMP_HIDDEN), (CMP_LEN * HEAD_DIM) ** -0.5),
        'nsa_cmp_w2': nrm(ks[13], (DEPTH, 2, CMP_HIDDEN, HEAD_DIM), CMP_HIDDEN ** -0.5),
        'pool_w': nrm(ks[14], (DEPTH, POOL_GROUPS, POOL_GROUP_DIM, POOL_GROUP_DIM), POOL_GROUP_DIM ** -0.5),
        'pool_scale': 1.0 + nrm(ks[15], (DEPTH, POOL_WIDTH), 0.02),
        'w_out': nrm(ks[16], (DEPTH, MIX_WIDTH, D_MODEL), BETA * MIX_WIDTH ** -0.5),
        'ln1_g': 1.0 + nrm(ks[17], (DEPTH, D_MODEL), 0.02),
        'ln1_b': nrm(ks[18], (DEPTH, D_MODEL), 0.02),
        'ln2_g': 1.0 + nrm(ks[19], (DEPTH, D_MODEL), 0.02),
        'ln2_b': nrm(ks[20], (DEPTH, D_MODEL), 0.02),
        'ffn_w_gate': nrm(ks[21], (N_DENSE, D_MODEL, D_FF), D_MODEL ** -0.5),
        'ffn_w_up': nrm(ks[22], (N_DENSE, D_MODEL, D_FF), D_MODEL ** -0.5),
        'ffn_w_down': nrm(ks[23], (N_DENSE, D_FF, D_MODEL), BETA * D_FF ** -0.5),
        'moe_router': nrm(ks[24], (N_MOE, D_MODEL, N_EXPERTS), D_MODEL ** -0.5),
        'moe_w_gate': nrm(ks[25], (N_MOE, N_EXPERTS, D_MODEL, D_FF_EXPERT), D_MODEL ** -0.5),
        'moe_w_up': nrm(ks[26], (N_MOE, N_EXPERTS, D_MODEL, D_FF_EXPERT), D_MODEL ** -0.5),
        'moe_w_down': nrm(ks[27], (N_MOE, N_EXPERTS, D_FF_EXPERT, D_MODEL), BETA * D_FF_EXPERT ** -0.5),
    }


def reference(x_prompt, x_sample, cache_nsa, page_table, state_win, state_gla, state_pool,
              w_in, gla_gate_w2, gla_gate_b, gla_norm_g, nsa_cmp_pos, nsa_cmp_w1, nsa_cmp_w2,
              pool_w, pool_scale, w_out, ln1_g, ln1_b, ln2_g, ln2_b,
              ffn_w_gate, ffn_w_up, ffn_w_down, moe_router, moe_w_gate, moe_w_up, moe_w_down):
    n_prompt = x_prompt.shape[0]
    n_dec = x_sample.shape[0]
    xp, xs = x_prompt, x_sample
    nsa_p, nsa_s, win_p, win_s, gla_p, gla_s, pool_p, pool_s = [], [], [], [], [], [], [], []
    for l in range(DEPTH):
        lw = (w_in[l], gla_gate_w2[l], gla_gate_b[l], gla_norm_g[l], nsa_cmp_pos[l], nsa_cmp_w1[l],
              nsa_cmp_w2[l], pool_w[l], pool_scale[l], w_out[l])
        gla0 = jnp.zeros((n_prompt, GLA_HEADS, GLA_DK, GLA_DV), xp.dtype)
        pool0 = jnp.zeros((n_prompt, POOL_MAX - 1, POOL_WIDTH), xp.dtype)
        hp, r_p, w_p, g_p, p_p = mixer_layer(xp, 0, gla0, pool0, None, None, *lw)
        past_rows = cache_nsa[l, page_table].reshape(n_dec, -1, 4, NSA_KV_HEADS, HEAD_DIM)
        hs, r_s, w_s, g_s, p_s = mixer_layer(xs, PAST_LEN, state_gla[l], state_pool[l], past_rows, state_win[l], *lw)
        xp = layer_norm(ALPHA * xp + hp, ln1_g[l], ln1_b[l])
        xs = layer_norm(ALPHA * xs + hs, ln1_g[l], ln1_b[l])
        if l % 2 == 0:
            i = l // 2
            fp = swiglu(xp, ffn_w_gate[i], ffn_w_up[i], ffn_w_down[i])
            fs = swiglu(xs, ffn_w_gate[i], ffn_w_up[i], ffn_w_down[i])
        else:
            i = l // 2
            fp = moe_ffn(xp, moe_router[i], moe_w_gate[i], moe_w_up[i], moe_w_down[i])
            fs = moe_ffn(xs, moe_router[i], moe_w_gate[i], moe_w_up[i], moe_w_down[i])
        xp = layer_norm(ALPHA * xp + fp, ln2_g[l], ln2_b[l])
        xs = layer_norm(ALPHA * xs + fs, ln2_g[l], ln2_b[l])
        nsa_p.append(r_p); nsa_s.append(r_s); win_p.append(w_p); win_s.append(w_s)
        gla_p.append(g_p); gla_s.append(g_s); pool_p.append(p_p); pool_s.append(p_s)
    return (xp, xs, jnp.stack(nsa_p), jnp.stack(nsa_s), jnp.stack(win_p), jnp.stack(win_s),
            jnp.stack(gla_p), jnp.stack(gla_s), jnp.stack(pool_p), jnp.stack(pool_s))
```

```python
import functools

import jax
import jax.numpy as jnp
from jax import lax
from jax.experimental import pallas as pl
from jax.experimental.pallas import tpu as pltpu

D_MODEL = 2048
DEPTH = 2
PAST_LEN = 16384
HEAD_DIM = 128
GLA_HEADS = 4
GLA_DK = 64
GLA_DV = 128
GLA_RANK = 16
GLA_GATE_NORM = 16.0
GLA_CHUNK = 64
GLA_WIDTH = GLA_HEADS * GLA_DV
NSA_HEADS = 8
NSA_KV_HEADS = 2
NSA_REP = NSA_HEADS // NSA_KV_HEADS
NSA_WIDTH = NSA_HEADS * HEAD_DIM
CMP_LEN = 32
CMP_STRIDE = 16
CMP_HIDDEN = 128
SEL_BLOCK = 64
SEL_TOPN = 16
WINDOW = 512
Q_BLOCK = 128
POOL_GROUPS = 4
POOL_GROUP_DIM = 128
POOL_WIDTH = POOL_GROUPS * POOL_GROUP_DIM
POOL_WINDOWS = (2, 4, 8, 16)
POOL_MAX = 16
ROPE_THETA = 500000.0
ROPE_DIM = HEAD_DIM // 4
N_EXPERTS = 8
TOP_K = 2
MOE_ROW_BLOCK = 128
MOE_MIN_BLOCK = 8
ALPHA = (2 * DEPTH) ** 0.25
LN_EPS = 1e-5
RMS_EPS = 1e-6

PROJ_SIZES = (
    ('gla_q', GLA_HEADS * GLA_DK), ('gla_k', GLA_HEADS * GLA_DK), ('gla_v', GLA_HEADS * GLA_DV),
    ('gla_glr', GLA_RANK), ('gla_r', GLA_HEADS * GLA_DV),
    ('nsa_q', NSA_HEADS * HEAD_DIM),
    ('cmp_k', NSA_KV_HEADS * HEAD_DIM), ('cmp_v', NSA_KV_HEADS * HEAD_DIM),
    ('slc_k', NSA_KV_HEADS * HEAD_DIM), ('slc_v', NSA_KV_HEADS * HEAD_DIM),
    ('win_k', NSA_KV_HEADS * HEAD_DIM), ('win_v', NSA_KV_HEADS * HEAD_DIM),
    ('nsa_gate', 3 * NSA_HEADS),
    ('pool', POOL_WIDTH),
)

VMEM_LIMIT_BYTES = 56 * 1024 * 1024


def _mm_kernel(x_ref, w_ref, o_ref, xb_ref):
    @pl.when(pl.program_id(1) == 0)
    def _():
        xb_ref[...] = x_ref[...].astype(jnp.bfloat16)

    o_ref[...] = jnp.dot(xb_ref[...], w_ref[...].astype(jnp.bfloat16), preferred_element_type=jnp.float32)


def _pick_tile(n, pref):
    for t in pref:
        if n % t == 0:
            return t
    return n


def matmul(x, w):
    m, k = x.shape
    n = w.shape[1]
    tm = _pick_tile(m, tuple(t for t in (1024, 512, 256, 128, 64, 32, 16, 8) if t * k <= 2048 * 1024))
    tn = 512 if n >= 512 else n
    return pl.pallas_call(
        _mm_kernel,
        grid=(m // tm, pl.cdiv(n, tn)),
        in_specs=[pl.BlockSpec((tm, k), lambda i, j: (i, 0)),
                  pl.BlockSpec((k, tn), lambda i, j: (0, j))],
        out_specs=pl.BlockSpec((tm, tn), lambda i, j: (i, j)),
        out_shape=jax.ShapeDtypeStruct((m, n), jnp.float32),
        scratch_shapes=[pltpu.VMEM((tm, k), jnp.bfloat16)],
        compiler_params=pltpu.CompilerParams(dimension_semantics=("arbitrary", "arbitrary"),
                                             vmem_limit_bytes=VMEM_LIMIT_BYTES),
        name="matmul",
    )(x, w)


def mm3(x, w):
    lead = x.shape[:-1]
    return matmul(x.reshape(-1, x.shape[-1]), w).reshape(*lead, w.shape[1])


def split_proj(p):
    out = {}
    off = 0
    for name, size in PROJ_SIZES:
        out[name] = p[..., off:off + size]
        off += size
    return out


def layer_norm(x, g, b):
    xf = x.astype(jnp.float32)
    xc = xf - jnp.mean(xf, -1, keepdims=True)
    var = jnp.mean(xc * xc, -1, keepdims=True)
    return (xc * lax.rsqrt(var + LN_EPS) * g + b).astype(x.dtype)


def rope(x, pos):
    half = ROPE_DIM // 2
    inv_freq = ROPE_THETA ** (-jnp.arange(half, dtype=jnp.float32) / half)
    ang = pos.astype(jnp.float32)[:, None] * inv_freq[None, :]
    cos = jnp.cos(ang)[:, None, :]
    sin = jnp.sin(ang)[:, None, :]
    xf = x.astype(jnp.float32)
    x1, x2 = xf[..., :half], xf[..., half:ROPE_DIM]
    out = jnp.concatenate([x1 * cos - x2 * sin, x2 * cos + x1 * sin, xf[..., ROPE_DIM:]], axis=-1)
    return out.astype(x.dtype)


def masked_softmax(s, mask):
    s = jnp.where(mask, s.astype(jnp.float32), -jnp.inf)
    m = jnp.max(s, axis=-1, keepdims=True)
    m = jnp.where(jnp.isfinite(m), m, 0.0)
    p = jnp.where(mask, jnp.exp(s - m), 0.0)
    return p / jnp.maximum(jnp.sum(p, -1, keepdims=True), 1e-30)


def gla_recurrence(q, k, v, g, s0):
    B, T, H, _ = q.shape
    C = GLA_CHUNK
    n_chunks = -(-T // C)
    pad = n_chunks * C - T

    def prep(a):
        a = jnp.pad(a, ((0, 0), (0, pad), (0, 0), (0, 0)))
        return a.reshape(B, n_chunks, C, H, a.shape[-1]).transpose(1, 0, 3, 2, 4)

    causal = jnp.tril(jnp.ones((C, C), dtype=bool))

    def step(S, inp):
        qi, ki, vi, gi = [a.astype(jnp.float32) for a in inp]
        b = jnp.cumsum(gi, axis=2)
        o_inter = jnp.einsum('bhtk,bhkv->bhtv', qi * jnp.exp(b), S)
        diff = jnp.where(causal[:, :, None], b[:, :, :, None, :] - b[:, :, None, :, :], -jnp.inf)
        attn = jnp.einsum('bhtk,bhsk,bhtsk->bhts', qi, ki, jnp.exp(diff))
        o = o_inter + jnp.einsum('bhts,bhsv->bhtv', attn, vi)
        b_last = b[:, :, -1:, :]
        S = jnp.exp(b_last[:, :, 0, :])[..., None] * S + jnp.einsum('bhsk,bhsv->bhkv', ki * jnp.exp(b_last - b), vi)
        return S, o

    S, o = lax.scan(step, s0.astype(jnp.float32), (prep(q), prep(k), prep(v), prep(g)))
    o = o.transpose(1, 0, 3, 2, 4).reshape(B, n_chunks * C, H, v.shape[-1])[:, :T]
    return o, S.astype(s0.dtype)


def gla_mixer(parts, s0, w2, b2, norm_g):
    B, T = parts['gla_q'].shape[:2]
    q = parts['gla_q'].reshape(B, T, GLA_HEADS, GLA_DK) * (GLA_DK ** -0.5)
    k = parts['gla_k'].reshape(B, T, GLA_HEADS, GLA_DK)
    v = parts['gla_v'].reshape(B, T, GLA_HEADS, GLA_DV)
    g = jax.nn.log_sigmoid((parts['gla_glr'] @ w2 + b2).astype(jnp.float32)) / GLA_GATE_NORM
    g = g.reshape(B, T, GLA_HEADS, GLA_DK)
    o, s_new = gla_recurrence(q, k, v, g, s0)
    o = o * lax.rsqrt(jnp.mean(o * o, -1, keepdims=True) + RMS_EPS)
    out = o.reshape(B, T, GLA_WIDTH) * norm_g * jax.nn.silu(parts['gla_r'].astype(jnp.float32))
    return out.astype(parts['gla_v'].dtype), s_new


def nsa_compress(kx, pos_emb, w1, w2):
    B, T, G, D = kx.shape
    nh = T // CMP_STRIDE
    halves = kx[:, :nh * CMP_STRIDE].reshape(B, nh, CMP_STRIDE, G, D).astype(jnp.float32)
    pe = pos_emb.reshape(2, CMP_STRIDE, D)
    w = w1.reshape(2, CMP_STRIDE, D, CMP_HIDDEN)
    h_lo = jnp.einsum('bnjgd,jdh->bngh', halves + pe[0][None, None, :, None, :], w[0])
    h_hi = jnp.einsum('bnjgd,jdh->bngh', halves + pe[1][None, None, :, None, :], w[1])
    h = jax.nn.gelu(h_lo[:, :-1] + h_hi[:, 1:])
    return jnp.einsum('bngh,hd->bngd', h, w2)


def nsa_global(qn, qr, rows, q_pos, cmp_pos, cmp_w1, cmp_w2):
    B, Tq, G, R, D = qn.shape
    Tk = rows.shape[1]
    scale = HEAD_DIM ** -0.5
    kcmp = nsa_compress(rows[:, :, 0], cmp_pos[0], cmp_w1[0], cmp_w2[0])
    vcmp = nsa_compress(rows[:, :, 1], cmp_pos[1], cmp_w1[1], cmp_w2[1])
    n_cmp = kcmp.shape[1]
    cmp_end = jnp.arange(n_cmp) * CMP_STRIDE + CMP_LEN - 1
    n_sel = -(-Tk // SEL_BLOCK)
    pad = n_sel * SEL_BLOCK - Tk

    def to_blocks(a):
        a = jnp.pad(a, ((0, 0), (0, pad), (0, 0), (0, 0)))
        return a.reshape(B, n_sel, SEL_BLOCK, G, D).transpose(0, 3, 1, 2, 4)

    ksb = to_blocks(rows[:, :, 2])
    vsb = to_blocks(rows[:, :, 3])
    ci = jnp.arange(n_cmp)[:, None]
    sj = jnp.arange(n_sel)[None, :]
    overlap = ((ci * CMP_STRIDE <= sj * SEL_BLOCK + SEL_BLOCK - 1) &
               (ci * CMP_STRIDE + CMP_LEN - 1 >= sj * SEL_BLOCK)).astype(jnp.float32)
    n_top = min(SEL_TOPN, n_sel)
    gather = jax.vmap(jax.vmap(lambda blocks, idx: blocks[idx]))
    blk_ids = jnp.arange(n_sel)

    def block_fn(args):
        qnb, qrb, qp = args
        qb = qp.shape[0]
        s = jnp.einsum('bqgrd,bngd->bgrqn', qnb, kcmp) * scale
        p_c = masked_softmax(s, cmp_end[None, :] <= qp[:, None])
        o_c = jnp.einsum('bgrqn,bngd->bqgrd', p_c, vcmp)
        imp = jnp.einsum('bgqn,nj->bgqj', jnp.sum(p_c, axis=2), overlap)
        cur = qp[:, None] // SEL_BLOCK
        valid = blk_ids[None, :] <= cur
        forced = (blk_ids[None, :] == 0) | (blk_ids[None, :] == cur) | (blk_ids[None, :] == cur - 1)
        imp = jnp.where(valid, jnp.where(forced, jnp.inf, imp), -jnp.inf)
        top_s, top_i = lax.top_k(imp, n_top)
        kg = gather(ksb, top_i)
        vg = gather(vsb, top_i)
        kpos = top_i[..., None] * SEL_BLOCK + jnp.arange(SEL_BLOCK)
        mask = (top_s > -jnp.inf)[..., None] & (kpos <= qp[None, None, :, None, None])
        s2 = jnp.einsum('bqgrd,bgqnjd->bgrqnj', qrb, kg) * scale
        s2 = s2.reshape(B, G, R, qb, n_top * SEL_BLOCK)
        p_s = masked_softmax(s2, mask.reshape(B, G, 1, qb, n_top * SEL_BLOCK))
        p_s = p_s.reshape(B, G, R, qb, n_top, SEL_BLOCK)
        o_s = jnp.einsum('bgrqnj,bgqnjd->bqgrd', p_s, vg.astype(jnp.float32))
        return o_c, o_s

    qb = min(Tq, Q_BLOCK)
    nq = -(-Tq // qb)
    padq = nq * qb - Tq
    qpad = ((0, 0), (0, padq), (0, 0), (0, 0), (0, 0))
    qn_b = jnp.pad(qn, qpad).reshape(B, nq, qb, G, R, D).swapaxes(0, 1)
    qr_b = jnp.pad(qr, qpad).reshape(B, nq, qb, G, R, D).swapaxes(0, 1)
    qp_b = jnp.pad(q_pos, (0, padq), mode='edge').reshape(nq, qb)
    o_c, o_s = lax.map(block_fn, (qn_b, qr_b, qp_b))
    o_c = o_c.swapaxes(0, 1).reshape(B, nq * qb, G, R, D)[:, :Tq]
    o_s = o_s.swapaxes(0, 1).reshape(B, nq * qb, G, R, D)[:, :Tq]
    return o_c, o_s


def band_attend(q, k, v, qpos, kpos):
    s = jnp.einsum('bqgrd,bkgd->bgrqk', q, k) * (HEAD_DIM ** -0.5)
    mask = ((kpos[None, :] <= qpos[:, None]) & (kpos[None, :] > qpos[:, None] - WINDOW) & (kpos[None, :] >= 0))
    p = masked_softmax(s, mask)
    return jnp.einsum('bgrqk,bkgd->bqgrd', p, v.astype(jnp.float32))


def sliding_prompt(q, k, v):
    B, T, G, R, D = q.shape
    kp = jnp.pad(k, ((0, 0), (WINDOW, 0), (0, 0), (0, 0)))
    vp = jnp.pad(v, ((0, 0), (WINDOW, 0), (0, 0), (0, 0)))
    nq = T // Q_BLOCK

    def fn(i):
        q0 = i * Q_BLOCK
        qi = lax.dynamic_slice_in_dim(q, q0, Q_BLOCK, axis=1)
        ki = lax.dynamic_slice_in_dim(kp, q0, WINDOW + Q_BLOCK, axis=1)
        vi = lax.dynamic_slice_in_dim(vp, q0, WINDOW + Q_BLOCK, axis=1)
        qpos = q0 + jnp.arange(Q_BLOCK)
        kpos = q0 - WINDOW + jnp.arange(WINDOW + Q_BLOCK)
        return band_attend(qi, ki, vi, qpos, kpos)

    o = lax.map(fn, jnp.arange(nq))
    return o.swapaxes(0, 1).reshape(B, T, G, R, D)


def nsa_mixer(parts, pos, nsa_past, win_past, cmp_pos, cmp_w1, cmp_w2):
    B, T = parts['nsa_q'].shape[:2]
    dt = parts['nsa_q'].dtype
    q = parts['nsa_q'].reshape(B, T, NSA_HEADS, HEAD_DIM)
    q_rope = rope(q, pos)
    kv = lambda name: parts[name].reshape(B, T, NSA_KV_HEADS, HEAD_DIM)
    k_win, v_win = rope(kv('win_k'), pos), kv('win_v')
    new_rows = jnp.stack([kv('cmp_k'), kv('cmp_v'), rope(kv('slc_k'), pos), kv('slc_v')], axis=2)
    rows = new_rows if nsa_past is None else jnp.concatenate([nsa_past.astype(dt), new_rows], axis=1)
    qg = q.reshape(B, T, NSA_KV_HEADS, NSA_REP, HEAD_DIM)
    qrg = q_rope.reshape(B, T, NSA_KV_HEADS, NSA_REP, HEAD_DIM)
    o_cmp, o_slc = nsa_global(qg, qrg, rows, pos, cmp_pos, cmp_w1, cmp_w2)
    win_rows = jnp.stack([k_win, v_win], axis=2)
    if win_past is None:
        o_win = sliding_prompt(qrg, k_win, v_win)
        new_win = win_rows[:, T - min(WINDOW, T):]
    else:
        n_buf = win_past.shape[1]
        ext = jnp.concatenate([win_past.astype(dt), win_rows], axis=1)
        kpos = pos[0] - n_buf + jnp.arange(n_buf + T)
        o_win = band_attend(qrg, ext[:, :, 0], ext[:, :, 1], pos, kpos)
        new_win = ext[:, T:]
    gates = jax.nn.sigmoid(parts['nsa_gate'].astype(jnp.float32)).reshape(B, T, NSA_KV_HEADS, NSA_REP, 3)
    o = gates[..., 0:1] * o_cmp + gates[..., 1:2] * o_slc + gates[..., 2:3] * o_win
    return o.reshape(B, T, NSA_WIDTH).astype(dt), new_rows, new_win


def pool_mixer(u, prev, pos, w_pool, scale):
    B, T, C = u.shape
    P = POOL_MAX - 1
    ext = jnp.concatenate([prev.astype(u.dtype), u], axis=1).astype(jnp.float32)
    cs = jnp.concatenate([jnp.zeros((B, 1, C), jnp.float32), jnp.cumsum(ext, axis=1)], axis=1)
    end = cs[:, P + 1:]
    means = []
    for gi, w in enumerate(POOL_WINDOWS):
        sl = slice(gi * POOL_GROUP_DIM, (gi + 1) * POOL_GROUP_DIM)
        start = cs[:, P + 1 - w:P + 1 - w + T, sl]
        cnt = jnp.minimum(pos + 1, w).astype(jnp.float32)[None, :, None]
        means.append((end[..., sl] - start) / cnt)
    pooled = (jnp.concatenate(means, axis=-1) - ext[:, P:]).reshape(B, T, POOL_GROUPS, POOL_GROUP_DIM)
    y = jnp.einsum('btgc,gcd->btgd', pooled, w_pool.astype(jnp.float32)).reshape(B, T, C) * scale
    return y.astype(u.dtype), ext[:, -P:].astype(u.dtype)


def mixer_layer(x, pos0, gla_s0, pool_prev, nsa_past, win_past,
                w_in, gla_w2, gla_b, gla_norm_g, cmp_pos, cmp_w1, cmp_w2, pool_w, pool_scale, w_out):
    B, T, _ = x.shape
    pos = pos0 + jnp.arange(T, dtype=jnp.int32)
    parts = split_proj(mm3(x, w_in))
    y_gla, s_gla = gla_mixer(parts, gla_s0, gla_w2, gla_b, gla_norm_g)
    y_nsa, nsa_rows, win_rows = nsa_mixer(parts, pos, nsa_past, win_past, cmp_pos, cmp_w1, cmp_w2)
    y_pool, pool_rows = pool_mixer(parts['pool'], pool_prev, pos, pool_w, pool_scale)
    y = mm3(jnp.concatenate([y_gla, y_nsa, y_pool], axis=-1), w_out)
    return y, nsa_rows, win_rows, s_gla, pool_rows


def swiglu(x, wg, wu, wd):
    return mm3(jax.nn.silu(mm3(x, wg)) * mm3(x, wu), wd)


def moe_ffn(x, router, wg, wu, wd):
    B, T, D = x.shape
    n_tok = B * T
    xt = x.reshape(n_tok, D)
    logits = (xt @ router).astype(jnp.float32)
    top_v, top_i = lax.top_k(logits, TOP_K)
    gates = jax.nn.softmax(top_v, axis=-1)
    n_asg = n_tok * TOP_K
    e_flat = top_i.reshape(n_asg)
    tok_flat = jnp.arange(n_asg, dtype=jnp.int32) // TOP_K
    gate_flat = gates.reshape(n_asg)
    blk = MOE_ROW_BLOCK if n_asg >= N_EXPERTS * MOE_ROW_BLOCK else MOE_MIN_BLOCK
    n_blk = -(-(n_asg + N_EXPERTS * (blk - 1)) // blk)
    order = jnp.argsort(e_flat)
    e_sorted = e_flat[order]
    counts = jnp.bincount(e_flat, length=N_EXPERTS)
    padded = (counts + blk - 1) // blk * blk
    pad_end = jnp.cumsum(padded)
    pad_start = pad_end - padded
    start = jnp.cumsum(counts) - counts
    dest = pad_start[e_sorted] + jnp.arange(n_asg) - start[e_sorted]
    row_tok = jnp.zeros((n_blk * blk,), jnp.int32).at[dest].set(tok_flat[order])
    row_gate = jnp.zeros((n_blk * blk,), jnp.float32).at[dest].set(gate_flat[order])
    blk_expert = jnp.minimum(jnp.searchsorted(pad_end, jnp.arange(n_blk) * blk, side='right'), N_EXPERTS - 1)

    def expert_block(args):
        rows, e = args
        xb = xt[rows]
        return (jax.nn.silu(xb @ wg[e]) * (xb @ wu[e])) @ wd[e]

    out = lax.map(expert_block, (row_tok.reshape(n_blk, blk), blk_expert))
    y = jnp.zeros((n_tok, D), jnp.float32).at[row_tok].add(
        out.reshape(n_blk * blk, D).astype(jnp.float32) * row_gate[:, None])
    return y.reshape(B, T, D).astype(x.dtype)


def kernel(x_prompt, x_sample, cache_nsa, page_table, state_win, state_gla, state_pool, w_in, gla_gate_w2, gla_gate_b, gla_norm_g, nsa_cmp_pos, nsa_cmp_w1, nsa_cmp_w2, pool_w, pool_scale, w_out, ln1_g, ln1_b, ln2_g, ln2_b, ffn_w_gate, ffn_w_up, ffn_w_down, moe_router, moe_w_gate, moe_w_up, moe_w_down):
    n_prompt = x_prompt.shape[0]
    n_dec = x_sample.shape[0]
    xp, xs = x_prompt, x_sample
    nsa_p, nsa_s, win_p, win_s, gla_p, gla_s, pool_p, pool_s = [], [], [], [], [], [], [], []
    for l in range(DEPTH):
        lw = (w_in[l], gla_gate_w2[l], gla_gate_b[l], gla_norm_g[l], nsa_cmp_pos[l], nsa_cmp_w1[l],
              nsa_cmp_w2[l], pool_w[l], pool_scale[l], w_out[l])
        gla0 = jnp.zeros((n_prompt, GLA_HEADS, GLA_DK, GLA_DV), xp.dtype)
        pool0 = jnp.zeros((n_prompt, POOL_MAX - 1, POOL_WIDTH), xp.dtype)
        hp, r_p, w_p, g_p, p_p = mixer_layer(xp, 0, gla0, pool0, None, None, *lw)
        past_rows = cache_nsa[l, page_table].reshape(n_dec, -1, 4, NSA_KV_HEADS, HEAD_DIM)
        hs, r_s, w_s, g_s, p_s = mixer_layer(xs, PAST_LEN, state_gla[l], state_pool[l], past_rows, state_win[l], *lw)
        xp = layer_norm(ALPHA * xp + hp, ln1_g[l], ln1_b[l])
        xs = layer_norm(ALPHA * xs + hs, ln1_g[l], ln1_b[l])
        if l % 2 == 0:
            i = l // 2
            fp = swiglu(xp, ffn_w_gate[i], ffn_w_up[i], ffn_w_down[i])
            fs = swiglu(xs, ffn_w_gate[i], ffn_w_up[i], ffn_w_down[i])
        else:
            i = l // 2
            fp = moe_ffn(xp, moe_router[i], moe_w_gate[i], moe_w_up[i], moe_w_down[i])
            fs = moe_ffn(xs, moe_router[i], moe_w_gate[i], moe_w_up[i], moe_w_down[i])
        xp = layer_norm(ALPHA * xp + fp, ln2_g[l], ln2_b[l])
        xs = layer_norm(ALPHA * xs + fs, ln2_g[l], ln2_b[l])
        nsa_p.append(r_p); nsa_s.append(r_s); win_p.append(w_p); win_s.append(w_s)
        gla_p.append(g_p); gla_s.append(g_s); pool_p.append(p_p); pool_s.append(p_s)
    return (xp, xs, jnp.stack(nsa_p), jnp.stack(nsa_s), jnp.stack(win_p), jnp.stack(win_s),
            jnp.stack(gla_p), jnp.stack(gla_s), jnp.stack(pool_p), jnp.stack(pool_s))
```

```python
import functools

import jax
import jax.numpy as jnp
from jax import lax
from jax.experimental import pallas as pl
from jax.experimental.pallas import tpu as pltpu

D_MODEL = 2048
DEPTH = 2
PAST_LEN = 16384
HEAD_DIM = 128
GLA_HEADS = 4
GLA_DK = 64
GLA_DV = 128
GLA_RANK = 16
GLA_GATE_NORM = 16.0
GLA_CHUNK = 64
GLA_WIDTH = GLA_HEADS * GLA_DV
NSA_HEADS = 8
NSA_KV_HEADS = 2
NSA_REP = NSA_HEADS // NSA_KV_HEADS
NSA_WIDTH = NSA_HEADS * HEAD_DIM
CMP_LEN = 32
CMP_STRIDE = 16
CMP_HIDDEN = 128
SEL_BLOCK = 64
SEL_TOPN = 16
WINDOW = 512
Q_BLOCK = 128
POOL_GROUPS = 4
POOL_GROUP_DIM = 128
POOL_WIDTH = POOL_GROUPS * POOL_GROUP_DIM
POOL_WINDOWS = (2, 4, 8, 16)
POOL_MAX = 16
ROPE_THETA = 500000.0
ROPE_DIM = HEAD_DIM // 4
N_EXPERTS = 8
TOP_K = 2
MOE_ROW_BLOCK = 128
MOE_MIN_BLOCK = 8
ALPHA = (2 * DEPTH) ** 0.25
LN_EPS = 1e-5
RMS_EPS = 1e-6

PROJ_SIZES = (
    ('gla_q', GLA_HEADS * GLA_DK), ('gla_k', GLA_HEADS * GLA_DK), ('gla_v', GLA_HEADS * GLA_DV),
    ('gla_glr', GLA_RANK), ('gla_r', GLA_HEADS * GLA_DV),
    ('nsa_q', NSA_HEADS * HEAD_DIM),
    ('cmp_k', NSA_KV_HEADS * HEAD_DIM), ('cmp_v', NSA_KV_HEADS * HEAD_DIM),
    ('slc_k', NSA_KV_HEADS * HEAD_DIM), ('slc_v', NSA_KV_HEADS * HEAD_DIM),
    ('win_k', NSA_KV_HEADS * HEAD_DIM), ('win_v', NSA_KV_HEADS * HEAD_DIM),
    ('nsa_gate', 3 * NSA_HEADS),
    ('pool', POOL_WIDTH),
)

GLA_SUB = 16
SEL_PAD = 128

BF16 = jnp.bfloat16
F32 = jnp.float32
NEG_BIG = -1e30
VMEM_LIMIT_BYTES = 56 * 1024 * 1024

COL_NSA_Q = 0
COL_ROWS = 1024
COL_WIN = 2048
COL_POOL = 2560
COL_GLA_V = 3072
COL_GLA_R = 3584
COL_GLA_Q = 4096
COL_GLA_K = 4352
COL_SMALL = 4608
PACKED_WIDTH = 4736
SMALL_GATE_OFF = GLA_RANK


def _params(*sem):
    return pltpu.CompilerParams(dimension_semantics=sem, vmem_limit_bytes=VMEM_LIMIT_BYTES)


def _proj_offsets():
    out, off = {}, 0
    for name, size in PROJ_SIZES:
        out[name] = (off, size)
        off += size
    return out


def pack_w_in(w):
    offs = _proj_offsets()
    sl = lambda n: w[:, offs[n][0]:offs[n][0] + offs[n][1]]
    pad = jnp.zeros((w.shape[0], 128 - GLA_RANK - 3 * NSA_HEADS), w.dtype)
    cols = [sl('nsa_q'), sl('cmp_k'), sl('cmp_v'), sl('slc_k'), sl('slc_v'), sl('win_k'), sl('win_v'),
            sl('pool'), sl('gla_v'), sl('gla_r'), sl('gla_q'), sl('gla_k'), sl('gla_glr'), sl('nsa_gate'), pad]
    return jnp.concatenate(cols, axis=1).astype(BF16)


def rope_tables(pos):
    half = ROPE_DIM // 2
    inv_freq = ROPE_THETA ** (-jnp.arange(half, dtype=F32) / half)
    ang = pos.astype(F32)[:, None] * inv_freq[None, :]
    cos, sin = jnp.cos(ang), jnp.sin(ang)
    t = pos.shape[0]
    c = jnp.concatenate([cos, cos, jnp.ones((t, HEAD_DIM - ROPE_DIM), F32)], axis=1)
    sa = jnp.concatenate([-sin, jnp.zeros((t, HEAD_DIM - half), F32)], axis=1)
    sb = jnp.concatenate([jnp.zeros((t, half), F32), sin, jnp.zeros((t, HEAD_DIM - ROPE_DIM), F32)], axis=1)
    return c, sa, sb


def _pick_tile(n, pref):
    for t in pref:
        if n % t == 0:
            return t
    return n


def _mm_kernel(x_ref, w_ref, o_ref, xb_ref):
    @pl.when(pl.program_id(1) == 0)
    def _():
        xb_ref[...] = x_ref[...].astype(BF16)

    o_ref[...] = jnp.dot(xb_ref[...], w_ref[...].astype(BF16), preferred_element_type=F32)


def matmul(x, w):
    m, k = x.shape
    n = w.shape[1]
    tm = _pick_tile(m, tuple(t for t in (1024, 512, 256, 128, 64, 32, 16, 8) if t * k <= 2048 * 1024))
    tn = 512 if n >= 512 else n
    return pl.pallas_call(
        _mm_kernel,
        grid=(m // tm, pl.cdiv(n, tn)),
        in_specs=[pl.BlockSpec((tm, k), lambda i, j: (i, 0)),
                  pl.BlockSpec((k, tn), lambda i, j: (0, j))],
        out_specs=pl.BlockSpec((tm, tn), lambda i, j: (i, j)),
        out_shape=jax.ShapeDtypeStruct((m, n), F32),
        scratch_shapes=[pltpu.VMEM((tm, k), BF16)],
        compiler_params=_params("arbitrary", "arbitrary"),
        name="matmul",
    )(x, w)


def mm3(x, w):
    lead = x.shape[:-1]
    return matmul(x.reshape(-1, x.shape[-1]), w).reshape(*lead, w.shape[1])


def _rope(x, c, sa, sb):
    return x * c + pltpu.roll(x, HEAD_DIM - ROPE_DIM // 2, 1) * sa + pltpu.roll(x, ROPE_DIM // 2, 1) * sb


def _nsa_prep_kernel(q_ref, rows_ref, win_ref, small_ref, c_ref, sa_ref, sb_ref,
                     rows_o, win_o, qn_o, qr_o, ks_o, vs_o, kw_o, vw_o, gate_o):
    c, sa, sb = c_ref[...], sa_ref[...], sb_ref[...]
    scale = HEAD_DIM ** -0.5
    hd = HEAD_DIM
    for h in range(NSA_HEADS):
        x = q_ref[:, h * hd:(h + 1) * hd]
        qn_o[0, h] = (x * scale).astype(BF16)
        qr_o[0, h] = (_rope(x, c, sa, sb) * scale).astype(BF16)
    ones = jnp.ones((q_ref.shape[0], hd), BF16)
    rows_o[:, 0:4 * hd] = rows_ref[:, 0:4 * hd]
    for g in range(NSA_KV_HEADS):
        k = _rope(rows_ref[:, (4 + g) * hd:(5 + g) * hd], c, sa, sb)
        rows_o[:, (4 + g) * hd:(5 + g) * hd] = k
        ks_o[0, g] = k.astype(BF16)
        v = rows_ref[:, (6 + g) * hd:(7 + g) * hd]
        rows_o[:, (6 + g) * hd:(7 + g) * hd] = v
        vs_o[0, g, :, 0:hd] = v.astype(BF16)
        vs_o[0, g, :, hd:2 * hd] = ones
        k = _rope(win_ref[:, g * hd:(g + 1) * hd], c, sa, sb)
        win_o[:, g * hd:(g + 1) * hd] = k
        kw_o[0, g] = k.astype(BF16)
        v = win_ref[:, (2 + g) * hd:(3 + g) * hd]
        win_o[:, (2 + g) * hd:(3 + g) * hd] = v
        vw_o[0, g, :, 0:hd] = v.astype(BF16)
        vw_o[0, g, :, hd:2 * hd] = ones
    sig = jax.nn.sigmoid(small_ref[...])
    per_g = 3 * NSA_REP
    for g in range(NSA_KV_HEADS):
        gate_o[0, g] = pltpu.roll(sig, 128 - SMALL_GATE_OFF - g * per_g, 1)


def nsa_prep(p, tables, n_batch, t_len):
    tr = _pick_tile(t_len, (512, 256, 128, 64, 32, 16))
    nt = t_len // tr
    n = n_batch * t_len
    hd = HEAD_DIM
    row = lambda w, cb: pl.BlockSpec((tr, w), lambda b, i: (b * nt + i, cb))
    tab = pl.BlockSpec((tr, hd), lambda b, i: (i, 0))
    head = lambda nh, w: pl.BlockSpec((1, nh, tr, w), lambda b, i: (b, 0, i, 0))
    out_shape = (
        jax.ShapeDtypeStruct((n, 8 * hd), F32),
        jax.ShapeDtypeStruct((n, 4 * hd), F32),
        jax.ShapeDtypeStruct((n_batch, NSA_HEADS, t_len, hd), BF16),
        jax.ShapeDtypeStruct((n_batch, NSA_HEADS, t_len, hd), BF16),
        jax.ShapeDtypeStruct((n_batch, NSA_KV_HEADS, t_len, hd), BF16),
        jax.ShapeDtypeStruct((n_batch, NSA_KV_HEADS, t_len, 2 * hd), BF16),
        jax.ShapeDtypeStruct((n_batch, NSA_KV_HEADS, t_len, hd), BF16),
        jax.ShapeDtypeStruct((n_batch, NSA_KV_HEADS, t_len, 2 * hd), BF16),
        jax.ShapeDtypeStruct((n_batch, NSA_KV_HEADS, t_len, 128), F32),
    )
    return pl.pallas_call(
        _nsa_prep_kernel,
        grid=(n_batch, nt),
        in_specs=[row(8 * hd, COL_NSA_Q // (8 * hd)), row(8 * hd, COL_ROWS // (8 * hd)),
                  row(4 * hd, COL_WIN // (4 * hd)), row(128, COL_SMALL // 128), tab, tab, tab],
        out_specs=(row(8 * hd, 0), row(4 * hd, 0), head(NSA_HEADS, hd), head(NSA_HEADS, hd),
                   head(NSA_KV_HEADS, hd), head(NSA_KV_HEADS, 2 * hd), head(NSA_KV_HEADS, hd),
                   head(NSA_KV_HEADS, 2 * hd), head(NSA_KV_HEADS, 128)),
        out_shape=out_shape,
        compiler_params=_params("arbitrary", "arbitrary"),
        name="nsa_prep",
    )(p, p, p, p, *tables)


def _nsa_cmp_kernel(x_ref, pe_ref, w1_ref, w2_ref, o_ref):
    nh = o_ref.shape[0]
    h_lo = jnp.zeros((nh, CMP_HIDDEN), F32)
    h_hi = jnp.zeros((nh, CMP_HIDDEN), F32)
    for j in range(CMP_STRIDE):
        xj = x_ref[pl.ds(j, nh, stride=CMP_STRIDE), :]
        h_lo += jnp.dot((xj + pe_ref[j:j + 1, :]).astype(BF16), w1_ref[j].astype(BF16), preferred_element_type=F32)
        h_hi += jnp.dot((xj + pe_ref[CMP_STRIDE + j:CMP_STRIDE + j + 1, :]).astype(BF16),
                        w1_ref[CMP_STRIDE + j].astype(BF16), preferred_element_type=F32)
    h = jax.nn.gelu(h_lo + pltpu.roll(h_hi, nh - 1, 0))
    o_ref[...] = jnp.dot(h.astype(BF16), w2_ref[...].astype(BF16), preferred_element_type=F32).astype(BF16)


def nsa_compress_prompt(rows, cmp_pos, cmp_w1, cmp_w2, n_batch, t_len):
    nh = t_len // CMP_STRIDE
    hd = HEAD_DIM
    rows3 = rows.reshape(n_batch, t_len, 8 * hd)
    return pl.pallas_call(
        _nsa_cmp_kernel,
        grid=(n_batch, 2, NSA_KV_HEADS),
        in_specs=[pl.BlockSpec((None, t_len, hd), lambda b, kd, g: (b, 0, kd * NSA_KV_HEADS + g)),
                  pl.BlockSpec((None, CMP_LEN, hd), lambda b, kd, g: (kd, 0, 0)),
                  pl.BlockSpec((None, CMP_LEN, hd, CMP_HIDDEN), lambda b, kd, g: (kd, 0, 0, 0)),
                  pl.BlockSpec((None, CMP_HIDDEN, hd), lambda b, kd, g: (kd, 0, 0))],
        out_specs=pl.BlockSpec((None, None, None, nh, hd), lambda b, kd, g: (b, kd, g, 0, 0)),
        out_shape=jax.ShapeDtypeStruct((n_batch, 2, NSA_KV_HEADS, nh, hd), BF16),
        compiler_params=_params("arbitrary", "arbitrary", "arbitrary"),
        name="nsa_compress",
    )(rows3, cmp_pos, cmp_w1, cmp_w2)


def _nsa_select_kernel(qn_ref, kc_ref, vc_ref, ovt_ref, oc_ref, selb_ref, *, n_cmp, n_top):
    rep, tq, hd = qn_ref.shape[1], qn_ref.shape[2], qn_ref.shape[3]
    n_cmp_pad = kc_ref.shape[0]
    n_sel = ovt_ref.shape[0]
    q0 = pl.program_id(2) * tq
    q = qn_ref[0].reshape(rep * tq, hd)
    s = lax.dot_general(q, kc_ref[...], (((1,), (1,)), ((), ())), preferred_element_type=F32)
    row = lax.broadcasted_iota(jnp.int32, (rep * tq, n_cmp_pad), 0)
    col = lax.broadcasted_iota(jnp.int32, (rep * tq, n_cmp_pad), 1)
    qpos = q0 + (row & (tq - 1))
    mask = (col * CMP_STRIDE + (CMP_LEN - 1) <= qpos) & (col < n_cmp)
    s = jnp.where(mask, s, -jnp.inf)
    m = jnp.max(s, axis=-1, keepdims=True)
    m = jnp.where(m > -jnp.inf, m, 0.0)
    p = jnp.where(mask, jnp.exp(s - m), 0.0)
    p = p / jnp.maximum(jnp.sum(p, axis=-1, keepdims=True), 1e-30)
    oc = jnp.dot(p.astype(BF16), vc_ref[...], preferred_element_type=F32)
    oc_ref[0] = oc.reshape(rep, tq, hd).astype(BF16)
    psum = p[0:tq]
    for r in range(1, rep):
        psum = psum + p[r * tq:(r + 1) * tq]
    imp = lax.dot_general(ovt_ref[...], psum, (((1,), (1,)), ((), ())), preferred_element_type=F32,
                          precision=lax.Precision.HIGHEST)
    blk = lax.broadcasted_iota(jnp.int32, (n_sel, tq), 0)
    cur = (q0 + lax.broadcasted_iota(jnp.int32, (n_sel, tq), 1)) // SEL_BLOCK
    forced = (blk == 0) | (blk == cur) | (blk == cur - 1)
    v = jnp.where(blk <= cur, jnp.where(forced, jnp.inf, imp), -jnp.inf)
    rank = jnp.zeros((n_sel, tq), jnp.int32)
    for i in range(n_sel):
        vi = v[i:i + 1, :]
        ahead = (vi > v) | ((vi == v) & (blk > i))
        rank = rank + ahead.astype(jnp.int32)
    selb_t = jnp.where((rank < n_top) & (v > -jnp.inf), 0.0, NEG_BIG)
    pad = jnp.full((SEL_PAD - n_sel, tq), NEG_BIG, F32)
    selb_ref[0, 0] = jnp.concatenate([selb_t, pad], axis=0).T.astype(BF16)


def nsa_select(qn, cmp_kv, n_cmp, t_k):
    n_batch, _, t_len, hd = qn.shape
    n_cmp_pad = cmp_kv.shape[3]
    n_sel = -(-t_k // SEL_BLOCK)
    tq = _pick_tile(t_len, (256, 128, 64, 32, 16))
    ci = jnp.arange(n_cmp_pad)[None, :]
    sj = jnp.arange(n_sel)[:, None]
    overlap_t = ((ci * CMP_STRIDE <= sj * SEL_BLOCK + SEL_BLOCK - 1) &
                 (ci * CMP_STRIDE + CMP_LEN - 1 >= sj * SEL_BLOCK) & (ci < n_cmp)).astype(F32)
    kern = functools.partial(_nsa_select_kernel, n_cmp=n_cmp, n_top=min(SEL_TOPN, n_sel))
    return pl.pallas_call(
        kern,
        grid=(n_batch, NSA_KV_HEADS, t_len // tq),
        in_specs=[pl.BlockSpec((1, NSA_REP, tq, hd), lambda b, g, i: (b, g, i, 0)),
                  pl.BlockSpec((None, None, None, n_cmp_pad, hd), lambda b, g, i: (b, 0, g, 0, 0)),
                  pl.BlockSpec((None, None, None, n_cmp_pad, hd), lambda b, g, i: (b, 1, g, 0, 0)),
                  pl.BlockSpec((n_sel, n_cmp_pad), lambda b, g, i: (0, 0))],
        out_specs=(pl.BlockSpec((1, NSA_REP, tq, hd), lambda b, g, i: (b, g, i, 0)),
                   pl.BlockSpec((1, 1, tq, SEL_PAD), lambda b, g, i: (b, g, i, 0))),
        out_shape=(jax.ShapeDtypeStruct((n_batch, NSA_HEADS, t_len, hd), BF16),
                   jax.ShapeDtypeStruct((n_batch, NSA_KV_HEADS, t_len, SEL_PAD), BF16)),
        compiler_params=_params("arbitrary", "arbitrary", "arbitrary"),
        name="nsa_select",
    )(qn, cmp_kv, cmp_kv, overlap_t)


def _nsa_attn_kernel(qr_ref, oc_ref, selb_ref, gate_ref, ks_ref, vs_ref, kw_ref, vw_ref, e_ref, o_ref,
                     m_scr, acc_scr, *, tk, wk):
    rep, qb, hd = qr_ref.shape[1], qr_ref.shape[2], qr_ref.shape[3]
    nr = rep * qb
    q0 = pl.program_id(2) * qb
    q = qr_ref[0].reshape(nr, hd)
    selb = selb_ref[0, 0]
    nt = (((1,), (1,)), ((), ()))

    def sel_scores(t):
        k = ks_ref[0, 0, pl.ds(pl.multiple_of(t * tk, tk), tk), :]
        s = lax.dot_general(q, k, nt, preferred_element_type=F32)
        bias = jnp.dot(selb, e_ref[t], preferred_element_type=F32)
        return (s.reshape(rep, qb, tk) + bias[None]).reshape(nr, tk)

    def sel_values(t):
        return vs_ref[0, 0, pl.ds(pl.multiple_of(t * tk, tk), tk), :]

    td = q0 // tk
    qpos = q0 + (lax.broadcasted_iota(jnp.int32, (nr, tk), 0) & (qb - 1))
    kpos = td * tk + lax.broadcasted_iota(jnp.int32, (nr, tk), 1)
    s = jnp.where(kpos <= qpos, sel_scores(td), NEG_BIG)
    m = jnp.max(s, axis=-1, keepdims=True)
    m_scr[...] = m
    acc_scr[...] = jnp.dot(jnp.exp(s - m).astype(BF16), sel_values(td), preferred_element_type=F32)

    def body(t, carry):
        s = sel_scores(t)
        m_old = m_scr[...]
        m_new = jnp.maximum(m_old, jnp.max(s, axis=-1, keepdims=True))
        p = jnp.exp(s - m_new).astype(BF16)
        acc_scr[...] = jnp.exp(m_old - m_new) * acc_scr[...] + jnp.dot(p, sel_values(t), preferred_element_type=F32)
        m_scr[...] = m_new
        return carry

    lax.fori_loop(0, td, body, 0)
    acc = acc_scr[...]
    o_sel = acc[:, 0:hd] / jnp.maximum(acc[:, hd:hd + 1], 1e-30)

    kstart = pl.multiple_of(jnp.maximum(q0 - WINDOW, 0), qb)
    kw = kw_ref[0, 0, pl.ds(kstart, wk), :]
    s = lax.dot_general(q, kw, nt, preferred_element_type=F32)
    qpos = q0 + (lax.broadcasted_iota(jnp.int32, (nr, wk), 0) & (qb - 1))
    kpos = kstart + lax.broadcasted_iota(jnp.int32, (nr, wk), 1)
    s = jnp.where((kpos <= qpos) & (kpos > qpos - WINDOW), s, NEG_BIG)
    m = jnp.max(s, axis=-1, keepdims=True)
    accw = jnp.dot(jnp.exp(s - m).astype(BF16), vw_ref[0, 0, pl.ds(kstart, wk), :], preferred_element_type=F32)
    o_win = accw[:, 0:hd] / jnp.maximum(accw[:, hd:hd + 1], 1e-30)

    gates = gate_ref[0, 0]
    for r in range(rep):
        rows = slice(r * qb, (r + 1) * qb)
        o = (gates[:, 3 * r:3 * r + 1] * oc_ref[0, r].astype(F32)
             + gates[:, 3 * r + 1:3 * r + 2] * o_sel[rows]
             + gates[:, 3 * r + 2:3 * r + 3] * o_win[rows])
        o_ref[:, r * hd:(r + 1) * hd] = o.astype(BF16)


def nsa_attend(qr, o_cmp, selb, gates, ks, vs, kw, vw):
    n_batch, _, t_len, hd = qr.shape
    n_sel = selb.shape[3]
    qb = Q_BLOCK
    tk = min(512, t_len)
    wk = min(WINDOW + qb, t_len)
    nq = t_len // qb
    n_tiles = t_len // tk
    key_blk = (jnp.arange(n_tiles)[:, None, None] * tk + jnp.arange(tk)[None, None, :]) // SEL_BLOCK
    e = (key_blk == jnp.arange(n_sel)[None, :, None]).astype(BF16)
    kern = functools.partial(_nsa_attn_kernel, tk=tk, wk=wk)
    per_q = lambda nh, w: pl.BlockSpec((1, nh, qb, w), lambda b, g, i: (b, g, i, 0))
    full = lambda w: pl.BlockSpec((1, 1, t_len, w), lambda b, g, i: (b, g, 0, 0))
    return pl.pallas_call(
        kern,
        grid=(n_batch, NSA_KV_HEADS, nq),
        in_specs=[per_q(NSA_REP, hd), per_q(NSA_REP, hd), per_q(1, n_sel), per_q(1, 128),
                  full(hd), full(2 * hd), full(hd), full(2 * hd),
                  pl.BlockSpec((n_tiles, n_sel, tk), lambda b, g, i: (0, 0, 0))],
        out_specs=pl.BlockSpec((qb, NSA_REP * hd), lambda b, g, i: (b * nq + i, g)),
        out_shape=jax.ShapeDtypeStruct((n_batch * t_len, NSA_HEADS * hd), BF16),
        scratch_shapes=[pltpu.VMEM((NSA_REP * qb, 1), F32), pltpu.VMEM((NSA_REP * qb, 2 * hd), F32)],
        compiler_params=_params("arbitrary", "arbitrary", "arbitrary"),
        name="nsa_attend",
    )(qr, o_cmp, selb, gates, ks, vs, kw, vw, e)


def _gla_kernel(q_ref, k_ref, v_ref, r_ref, small_ref, w2_ref, b2_ref, ng_ref, s0_ref, y_ref, sf_ref, s_scr):
    tb = q_ref.shape[0]
    c, sub, dk, dv = GLA_CHUNK, GLA_SUB, GLA_DK, GLA_DV
    n_sub = c // sub
    t = pl.program_id(1)

    @pl.when(t == 0)
    def _():
        s_scr[...] = s0_ref[0]

    z = jnp.dot(small_ref[:, 0:GLA_RANK].astype(BF16), w2_ref[...].astype(BF16),
                preferred_element_type=F32) + b2_ref[...]
    g_all = (jnp.minimum(z, 0.0) - jnp.log1p(jnp.exp(-jnp.abs(z)))) / GLA_GATE_NORM
    ri = lax.broadcasted_iota(jnp.int32, (c, c), 0)
    ci = lax.broadcasted_iota(jnp.int32, (c, c), 1)
    tril = ri >= ci
    cum = tril.astype(F32)
    rsub = lax.broadcasted_iota(jnp.int32, (c, dk), 0) // sub
    eye = lax.broadcasted_iota(jnp.int32, (dk, dk), 0) == lax.broadcasted_iota(jnp.int32, (dk, dk), 1)
    for cc in range(tb // c):
        rows = slice(cc * c, (cc + 1) * c)
        b_all = jnp.dot(cum, g_all[rows], preferred_element_type=F32, precision=lax.Precision.HIGHEST)
        for h in range(GLA_HEADS):
            b = b_all[:, h * dk:(h + 1) * dk]
            qh = q_ref[rows, h * dk:(h + 1) * dk] * (dk ** -0.5)
            kh = k_ref[rows, h * dk:(h + 1) * dk]
            vh = v_ref[rows, h * dv:(h + 1) * dv]
            a_rows = []
            for i in range(n_sub):
                ref = b[sub * i - 1:sub * i, :] if i else jnp.zeros((1, dk), F32)
                rs = slice(sub * i, sub * (i + 1))
                qi = (qh[rs] * jnp.exp(b[rs] - ref)).astype(BF16)
                ki = jnp.where(rsub <= i, kh * jnp.exp(ref - b), 0.0).astype(BF16)
                a_rows.append(lax.dot_general(qi, ki, (((1,), (1,)), ((), ())), preferred_element_type=F32))
            a = jnp.where(tril, jnp.concatenate(a_rows, axis=0), 0.0)
            s_old = s_scr[h]
            o = jnp.dot(a.astype(BF16), vh.astype(BF16), preferred_element_type=F32)
            o += jnp.dot((qh * jnp.exp(b)).astype(BF16), s_old.astype(BF16), preferred_element_type=F32)
            b_last = b[c - 1:c, :]
            ke = (kh * jnp.exp(b_last - b)).astype(BF16)
            upd = lax.dot_general(ke, vh.astype(BF16), (((0,), (0,)), ((), ())), preferred_element_type=F32)
            decay = jnp.exp(jnp.sum(jnp.where(eye, jnp.broadcast_to(b_last, (dk, dk)), 0.0), axis=1, keepdims=True))
            s_scr[h] = decay * s_old + upd
            o = o * lax.rsqrt(jnp.mean(o * o, axis=-1, keepdims=True) + RMS_EPS)
            y = o * ng_ref[:, h * dv:(h + 1) * dv] * jax.nn.silu(r_ref[rows, h * dv:(h + 1) * dv])
            y_ref[rows, h * dv:(h + 1) * dv] = y.astype(BF16)

    @pl.when(t == pl.num_programs(1) - 1)
    def _():
        sf_ref[0] = s_scr[...]


def gla_mix(p, s0, w2, b2, norm_g, n_batch, t_len):
    tb = _pick_tile(t_len, (256, 128, 64))
    nt = t_len // tb
    row = lambda w, off: pl.BlockSpec((tb, w), lambda b, i: (b * nt + i, off // w))
    const = lambda shape: pl.BlockSpec(shape, lambda b, i: (0,) * len(shape))
    return pl.pallas_call(
        _gla_kernel,
        grid=(n_batch, nt),
        in_specs=[row(256, COL_GLA_Q), row(256, COL_GLA_K), row(512, COL_GLA_V), row(512, COL_GLA_R),
                  row(128, COL_SMALL), const((GLA_RANK, GLA_HEADS * GLA_DK)), const((1, GLA_HEADS * GLA_DK)),
                  const((1, GLA_WIDTH)),
                  pl.BlockSpec((1, GLA_HEADS, GLA_DK, GLA_DV), lambda b, i: (b, 0, 0, 0))],
        out_specs=(pl.BlockSpec((tb, GLA_WIDTH), lambda b, i: (b * nt + i, 0)),
                   pl.BlockSpec((1, GLA_HEADS, GLA_DK, GLA_DV), lambda b, i: (b, 0, 0, 0))),
        out_shape=(jax.ShapeDtypeStruct((n_batch * t_len, GLA_WIDTH), BF16),
                   jax.ShapeDtypeStruct((n_batch, GLA_HEADS, GLA_DK, GLA_DV), F32)),
        scratch_shapes=[pltpu.VMEM((GLA_HEADS, GLA_DK, GLA_DV), F32)],
        compiler_params=_params("arbitrary", "arbitrary"),
        name="gla_mix",
    )(p, p, p, p, p, w2, b2.reshape(1, -1), norm_g.reshape(1, -1), s0)


def _pool_kernel(u_ref, prev_ref, cnt_ref, w_ref, sc_ref, y_ref, halo):
    tb = u_ref.shape[0]
    gd = POOL_GROUP_DIM

    @pl.when(pl.program_id(1) == 0)
    def _():
        halo[...] = prev_ref[0]

    ext = jnp.concatenate([halo[...], u_ref[...]], axis=0)
    halo[...] = ext[tb:tb + POOL_MAX]
    for gi, w in enumerate(POOL_WINDOWS):
        x = ext[:, gi * gd:(gi + 1) * gd]
        s = x
        shift = 1
        while shift < w:
            s = s + pltpu.roll(s, shift, 0)
            shift *= 2
        pooled = s[POOL_MAX:] / cnt_ref[:, gi:gi + 1] - x[POOL_MAX:]
        y = jnp.dot(pooled.astype(BF16), w_ref[gi].astype(BF16), preferred_element_type=F32)
        y_ref[:, gi * gd:(gi + 1) * gd] = (y * sc_ref[:, gi * gd:(gi + 1) * gd]).astype(BF16)


def pool_mix(p, prev, pos0, w_pool, scale, n_batch, t_len):
    tb = _pick_tile(t_len, (512, 256, 128, 64, 32, 16))
    nt = t_len // tb
    pos = pos0 + jnp.arange(t_len, dtype=jnp.int32)
    cnt = jnp.stack([jnp.minimum(pos + 1, w).astype(F32) for w in POOL_WINDOWS], axis=1)
    cnt = jnp.pad(cnt, ((0, 0), (0, 128 - POOL_GROUPS)), constant_values=1.0)
    prev16 = jnp.pad(prev.astype(F32), ((0, 0), (1, 0), (0, 0)))
    return pl.pallas_call(
        _pool_kernel,
        grid=(n_batch, nt),
        in_specs=[pl.BlockSpec((tb, POOL_WIDTH), lambda b, i: (b * nt + i, COL_POOL // POOL_WIDTH)),
                  pl.BlockSpec((1, POOL_MAX, POOL_WIDTH), lambda b, i: (b, 0, 0)),
                  pl.BlockSpec((tb, 128), lambda b, i: (i, 0)),
                  pl.BlockSpec((POOL_GROUPS, POOL_GROUP_DIM, POOL_GROUP_DIM), lambda b, i: (0, 0, 0)),
                  pl.BlockSpec((1, POOL_WIDTH), lambda b, i: (0, 0))],
        out_specs=pl.BlockSpec((tb, POOL_WIDTH), lambda b, i: (b * nt + i, 0)),
        out_shape=jax.ShapeDtypeStruct((n_batch * t_len, POOL_WIDTH), BF16),
        scratch_shapes=[pltpu.VMEM((POOL_MAX, POOL_WIDTH), F32)],
        compiler_params=_params("arbitrary", "arbitrary"),
        name="pool_mix",
    )(p, prev16, cnt, w_pool, scale.reshape(1, -1))


def _layer_norm_rows(x, g, b):
    xc = x - jnp.mean(x, axis=-1, keepdims=True)
    var = jnp.mean(xc * xc, axis=-1, keepdims=True)
    return xc * lax.rsqrt(var + LN_EPS) * g + b


def _outproj_kernel(x_ref, yg_ref, yn_ref, yp_ref, w_ref, g_ref, b_ref, o_ref):
    h = jnp.dot(yg_ref[...], w_ref[0:GLA_WIDTH, :], preferred_element_type=F32)
    h += jnp.dot(yn_ref[...], w_ref[GLA_WIDTH:GLA_WIDTH + NSA_WIDTH, :], preferred_element_type=F32)
    h += jnp.dot(yp_ref[...], w_ref[GLA_WIDTH + NSA_WIDTH:, :], preferred_element_type=F32)
    o_ref[...] = _layer_norm_rows(ALPHA * x_ref[...] + h, g_ref[...], b_ref[...])


def outproj_ln(x, y_gla, y_nsa, y_pool, w_out_bf16, g, b):
    n, d = x.shape
    tm = _pick_tile(n, (512, 256, 128, 64, 32, 16, 8))
    row = lambda w: pl.BlockSpec((tm, w), lambda i: (i, 0))
    return pl.pallas_call(
        _outproj_kernel,
        grid=(n // tm,),
        in_specs=[row(d), row(GLA_WIDTH), row(NSA_WIDTH), row(POOL_WIDTH),
                  pl.BlockSpec((d, d), lambda i: (0, 0)),
                  pl.BlockSpec((1, d), lambda i: (0, 0)), pl.BlockSpec((1, d), lambda i: (0, 0))],
        out_specs=row(d),
        out_shape=jax.ShapeDtypeStruct((n, d), F32),
        compiler_params=_params("arbitrary"),
        name="outproj_ln",
    )(x, y_gla, y_nsa, y_pool, w_out_bf16, g.reshape(1, -1), b.reshape(1, -1))


def _ffn_kernel(x_ref, wg_ref, wu_ref, wd_ref, g_ref, b_ref, o_ref, xb_ref, *, d_ff):
    j = pl.program_id(1)
    tf = wg_ref.shape[1]

    @pl.when(j == 0)
    def _():
        xb_ref[...] = x_ref[...].astype(BF16)
        o_ref[...] = jnp.zeros_like(o_ref)

    xb = xb_ref[...]
    gate = jnp.dot(xb, wg_ref[...], preferred_element_type=F32)
    up = jnp.dot(xb, wu_ref[...], preferred_element_type=F32)
    col = j * tf + lax.broadcasted_iota(jnp.int32, (1, tf), 1)
    a = jnp.where(col < d_ff, jax.nn.silu(gate) * up, 0.0).astype(BF16)
    rowi = j * tf + lax.broadcasted_iota(jnp.int32, (tf, 1), 0)
    wd = jnp.where(rowi < d_ff, wd_ref[...], jnp.zeros((), BF16))
    o_ref[...] += jnp.dot(a, wd, preferred_element_type=F32)

    @pl.when(j == pl.num_programs(1) - 1)
    def _():
        o_ref[...] = _layer_norm_rows(ALPHA * x_ref[...] + o_ref[...], g_ref[...], b_ref[...])


def ffn_ln(x, wg, wu, wd, g, b):
    n, d = x.shape
    d_ff = wg.shape[1]
    tm = _pick_tile(n, (512, 256, 128, 64, 32, 16, 8))
    tf = 512
    kern = functools.partial(_ffn_kernel, d_ff=d_ff)
    return pl.pallas_call(
        kern,
        grid=(n // tm, pl.cdiv(d_ff, tf)),
        in_specs=[pl.BlockSpec((tm, d), lambda i, j: (i, 0)),
                  pl.BlockSpec((d, tf), lambda i, j: (0, j)), pl.BlockSpec((d, tf), lambda i, j: (0, j)),
                  pl.BlockSpec((tf, d), lambda i, j: (j, 0)),
                  pl.BlockSpec((1, d), lambda i, j: (0, 0)), pl.BlockSpec((1, d), lambda i, j: (0, 0))],
        out_specs=pl.BlockSpec((tm, d), lambda i, j: (i, 0)),
        out_shape=jax.ShapeDtypeStruct((n, d), F32),
        scratch_shapes=[pltpu.VMEM((tm, d), BF16)],
        compiler_params=_params("arbitrary", "arbitrary"),
        name="ffn_ln",
    )(x, wg, wu, wd, g.reshape(1, -1), b.reshape(1, -1))


def prompt_mixer(x2, w_in_packed, w_out_bf16, ln_g, ln_b, gla_w2, gla_b, gla_norm_g, cmp_pos, cmp_w1, cmp_w2,
                 pool_w, pool_scale, n_batch, t_len):
    p = matmul(x2, w_in_packed)
    pos = jnp.arange(t_len, dtype=jnp.int32)
    rows, win, qn, qr, ks, vs, kw, vw, gates = nsa_prep(p, rope_tables(pos), n_batch, t_len)
    cmp_kv = nsa_compress_prompt(rows, cmp_pos, cmp_w1, cmp_w2, n_batch, t_len)
    o_cmp, selb = nsa_select(qn, cmp_kv, t_len // CMP_STRIDE - 1, t_len)
    y_nsa = nsa_attend(qr, o_cmp, selb, gates, ks, vs, kw, vw)
    s0 = jnp.zeros((n_batch, GLA_HEADS, GLA_DK, GLA_DV), F32)
    y_gla, s_gla = gla_mix(p, s0, gla_w2, gla_b, gla_norm_g, n_batch, t_len)
    prev = jnp.zeros((n_batch, POOL_MAX - 1, POOL_WIDTH), F32)
    y_pool = pool_mix(p, prev, 0, pool_w, pool_scale, n_batch, t_len)
    x1 = outproj_ln(x2, y_gla, y_nsa, y_pool, w_out_bf16, ln_g, ln_b)
    nsa_rows = rows.reshape(n_batch, t_len, 4, NSA_KV_HEADS, HEAD_DIM)
    n_win = min(WINDOW, t_len)
    win_rows = win.reshape(n_batch, t_len, 2, NSA_KV_HEADS, HEAD_DIM)[:, t_len - n_win:]
    pool_rows = p.reshape(n_batch, t_len, PACKED_WIDTH)[:, t_len - (POOL_MAX - 1):, COL_POOL:COL_POOL + POOL_WIDTH]
    return x1, nsa_rows, win_rows, s_gla, pool_rows


def split_proj(p):
    out = {}
    off = 0
    for name, size in PROJ_SIZES:
        out[name] = p[..., off:off + size]
        off += size
    return out


def layer_norm(x, g, b):
    xf = x.astype(jnp.float32)
    xc = xf - jnp.mean(xf, -1, keepdims=True)
    var = jnp.mean(xc * xc, -1, keepdims=True)
    return (xc * lax.rsqrt(var + LN_EPS) * g + b).astype(x.dtype)


def rope(x, pos):
    half = ROPE_DIM // 2
    inv_freq = ROPE_THETA ** (-jnp.arange(half, dtype=jnp.float32) / half)
    ang = pos.astype(jnp.float32)[:, None] * inv_freq[None, :]
    cos = jnp.cos(ang)[:, None, :]
    sin = jnp.sin(ang)[:, None, :]
    xf = x.astype(jnp.float32)
    x1, x2 = xf[..., :half], xf[..., half:ROPE_DIM]
    out = jnp.concatenate([x1 * cos - x2 * sin, x2 * cos + x1 * sin, xf[..., ROPE_DIM:]], axis=-1)
    return out.astype(x.dtype)


def masked_softmax(s, mask):
    s = jnp.where(mask, s.astype(jnp.float32), -jnp.inf)
    m = jnp.max(s, axis=-1, keepdims=True)
    m = jnp.where(jnp.isfinite(m), m, 0.0)
    p = jnp.where(mask, jnp.exp(s - m), 0.0)
    return p / jnp.maximum(jnp.sum(p, -1, keepdims=True), 1e-30)


def gla_recurrence(q, k, v, g, s0):
    B, T, H, _ = q.shape
    C = GLA_CHUNK
    n_chunks = -(-T // C)
    pad = n_chunks * C - T

    def prep(a):
        a = jnp.pad(a, ((0, 0), (0, pad), (0, 0), (0, 0)))
        return a.reshape(B, n_chunks, C, H, a.shape[-1]).transpose(1, 0, 3, 2, 4)

    causal = jnp.tril(jnp.ones((C, C), dtype=bool))

    def step(S, inp):
        qi, ki, vi, gi = [a.astype(jnp.float32) for a in inp]
        b = jnp.cumsum(gi, axis=2)
        o_inter = jnp.einsum('bhtk,bhkv->bhtv', qi * jnp.exp(b), S)
        diff = jnp.where(causal[:, :, None], b[:, :, :, None, :] - b[:, :, None, :, :], -jnp.inf)
        attn = jnp.einsum('bhtk,bhsk,bhtsk->bhts', qi, ki, jnp.exp(diff))
        o = o_inter + jnp.einsum('bhts,bhsv->bhtv', attn, vi)
        b_last = b[:, :, -1:, :]
        S = jnp.exp(b_last[:, :, 0, :])[..., None] * S + jnp.einsum('bhsk,bhsv->bhkv', ki * jnp.exp(b_last - b), vi)
        return S, o

    S, o = lax.scan(step, s0.astype(jnp.float32), (prep(q), prep(k), prep(v), prep(g)))
    o = o.transpose(1, 0, 3, 2, 4).reshape(B, n_chunks * C, H, v.shape[-1])[:, :T]
    return o, S.astype(s0.dtype)


def gla_mixer(parts, s0, w2, b2, norm_g):
    B, T = parts['gla_q'].shape[:2]
    q = parts['gla_q'].reshape(B, T, GLA_HEADS, GLA_DK) * (GLA_DK ** -0.5)
    k = parts['gla_k'].reshape(B, T, GLA_HEADS, GLA_DK)
    v = parts['gla_v'].reshape(B, T, GLA_HEADS, GLA_DV)
    g = jax.nn.log_sigmoid((parts['gla_glr'] @ w2 + b2).astype(jnp.float32)) / GLA_GATE_NORM
    g = g.reshape(B, T, GLA_HEADS, GLA_DK)
    o, s_new = gla_recurrence(q, k, v, g, s0)
    o = o * lax.rsqrt(jnp.mean(o * o, -1, keepdims=True) + RMS_EPS)
    out = o.reshape(B, T, GLA_WIDTH) * norm_g * jax.nn.silu(parts['gla_r'].astype(jnp.float32))
    return out.astype(parts['gla_v'].dtype), s_new


def nsa_compress(kx, pos_emb, w1, w2):
    B, T, G, D = kx.shape
    nh = T // CMP_STRIDE
    halves = kx[:, :nh * CMP_STRIDE].reshape(B, nh, CMP_STRIDE, G, D).astype(jnp.float32)
    pe = pos_emb.reshape(2, CMP_STRIDE, D)
    w = w1.reshape(2, CMP_STRIDE, D, CMP_HIDDEN)
    h_lo = jnp.einsum('bnjgd,jdh->bngh', halves + pe[0][None, None, :, None, :], w[0])
    h_hi = jnp.einsum('bnjgd,jdh->bngh', halves + pe[1][None, None, :, None, :], w[1])
    h = jax.nn.gelu(h_lo[:, :-1] + h_hi[:, 1:])
    return jnp.einsum('bngh,hd->bngd', h, w2)


def nsa_global(qn, qr, rows, q_pos, cmp_pos, cmp_w1, cmp_w2):
    B, Tq, G, R, D = qn.shape
    Tk = rows.shape[1]
    scale = HEAD_DIM ** -0.5
    kcmp = nsa_compress(rows[:, :, 0], cmp_pos[0], cmp_w1[0], cmp_w2[0])
    vcmp = nsa_compress(rows[:, :, 1], cmp_pos[1], cmp_w1[1], cmp_w2[1])
    n_cmp = kcmp.shape[1]
    cmp_end = jnp.arange(n_cmp) * CMP_STRIDE + CMP_LEN - 1
    n_sel = -(-Tk // SEL_BLOCK)
    pad = n_sel * SEL_BLOCK - Tk

    def to_blocks(a):
        a = jnp.pad(a, ((0, 0), (0, pad), (0, 0), (0, 0)))
        return a.reshape(B, n_sel, SEL_BLOCK, G, D).transpose(0, 3, 1, 2, 4)

    ksb = to_blocks(rows[:, :, 2])
    vsb = to_blocks(rows[:, :, 3])
    ci = jnp.arange(n_cmp)[:, None]
    sj = jnp.arange(n_sel)[None, :]
    overlap = ((ci * CMP_STRIDE <= sj * SEL_BLOCK + SEL_BLOCK - 1) &
               (ci * CMP_STRIDE + CMP_LEN - 1 >= sj * SEL_BLOCK)).astype(jnp.float32)
    n_top = min(SEL_TOPN, n_sel)
    gather = jax.vmap(jax.vmap(lambda blocks, idx: blocks[idx]))
    blk_ids = jnp.arange(n_sel)

    def block_fn(args):
        qnb, qrb, qp = args
        qb = qp.shape[0]
        s = jnp.einsum('bqgrd,bngd->bgrqn', qnb, kcmp) * scale
        p_c = masked_softmax(s, cmp_end[None, :] <= qp[:, None])
        o_c = jnp.einsum('bgrqn,bngd->bqgrd', p_c, vcmp)
        imp = jnp.einsum('bgqn,nj->bgqj', jnp.sum(p_c, axis=2), overlap)
        cur = qp[:, None] // SEL_BLOCK
        valid = blk_ids[None, :] <= cur
        forced = (blk_ids[None, :] == 0) | (blk_ids[None, :] == cur) | (blk_ids[None, :] == cur - 1)
        imp = jnp.where(valid, jnp.where(forced, jnp.inf, imp), -jnp.inf)
        top_s, top_i = lax.top_k(imp, n_top)
        kg = gather(ksb, top_i)
        vg = gather(vsb, top_i)
        kpos = top_i[..., None] * SEL_BLOCK + jnp.arange(SEL_BLOCK)
        mask = (top_s > -jnp.inf)[..., None] & (kpos <= qp[None, None, :, None, None])
        s2 = jnp.einsum('bqgrd,bgqnjd->bgrqnj', qrb, kg) * scale
        s2 = s2.reshape(B, G, R, qb, n_top * SEL_BLOCK)
        p_s = masked_softmax(s2, mask.reshape(B, G, 1, qb, n_top * SEL_BLOCK))
        p_s = p_s.reshape(B, G, R, qb, n_top, SEL_BLOCK)
        o_s = jnp.einsum('bgrqnj,bgqnjd->bqgrd', p_s, vg.astype(jnp.float32))
        return o_c, o_s

    qb = min(Tq, Q_BLOCK)
    nq = -(-Tq // qb)
    padq = nq * qb - Tq
    qpad = ((0, 0), (0, padq), (0, 0), (0, 0), (0, 0))
    qn_b = jnp.pad(qn, qpad).reshape(B, nq, qb, G, R, D).swapaxes(0, 1)
    qr_b = jnp.pad(qr, qpad).reshape(B, nq, qb, G, R, D).swapaxes(0, 1)
    qp_b = jnp.pad(q_pos, (0, padq), mode='edge').reshape(nq, qb)
    o_c, o_s = lax.map(block_fn, (qn_b, qr_b, qp_b))
    o_c = o_c.swapaxes(0, 1).reshape(B, nq * qb, G, R, D)[:, :Tq]
    o_s = o_s.swapaxes(0, 1).reshape(B, nq * qb, G, R, D)[:, :Tq]
    return o_c, o_s


def band_attend(q, k, v, qpos, kpos):
    s = jnp.einsum('bqgrd,bkgd->bgrqk', q, k) * (HEAD_DIM ** -0.5)
    mask = ((kpos[None, :] <= qpos[:, None]) & (kpos[None, :] > qpos[:, None] - WINDOW) & (kpos[None, :] >= 0))
    p = masked_softmax(s, mask)
    return jnp.einsum('bgrqk,bkgd->bqgrd', p, v.astype(jnp.float32))


def sliding_prompt(q, k, v):
    B, T, G, R, D = q.shape
    kp = jnp.pad(k, ((0, 0), (WINDOW, 0), (0, 0), (0, 0)))
    vp = jnp.pad(v, ((0, 0), (WINDOW, 0), (0, 0), (0, 0)))
    nq = T // Q_BLOCK

    def fn(i):
        q0 = i * Q_BLOCK
        qi = lax.dynamic_slice_in_dim(q, q0, Q_BLOCK, axis=1)
        ki = lax.dynamic_slice_in_dim(kp, q0, WINDOW + Q_BLOCK, axis=1)
        vi = lax.dynamic_slice_in_dim(vp, q0, WINDOW + Q_BLOCK, axis=1)
        qpos = q0 + jnp.arange(Q_BLOCK)
        kpos = q0 - WINDOW + jnp.arange(WINDOW + Q_BLOCK)
        return band_attend(qi, ki, vi, qpos, kpos)

    o = lax.map(fn, jnp.arange(nq))
    return o.swapaxes(0, 1).reshape(B, T, G, R, D)


def nsa_mixer(parts, pos, nsa_past, win_past, cmp_pos, cmp_w1, cmp_w2):
    B, T = parts['nsa_q'].shape[:2]
    dt = parts['nsa_q'].dtype
    q = parts['nsa_q'].reshape(B, T, NSA_HEADS, HEAD_DIM)
    q_rope = rope(q, pos)
    kv = lambda name: parts[name].reshape(B, T, NSA_KV_HEADS, HEAD_DIM)
    k_win, v_win = rope(kv('win_k'), pos), kv('win_v')
    new_rows = jnp.stack([kv('cmp_k'), kv('cmp_v'), rope(kv('slc_k'), pos), kv('slc_v')], axis=2)
    rows = new_rows if nsa_past is None else jnp.concatenate([nsa_past.astype(dt), new_rows], axis=1)
    qg = q.reshape(B, T, NSA_KV_HEADS, NSA_REP, HEAD_DIM)
    qrg = q_rope.reshape(B, T, NSA_KV_HEADS, NSA_REP, HEAD_DIM)
    o_cmp, o_slc = nsa_global(qg, qrg, rows, pos, cmp_pos, cmp_w1, cmp_w2)
    win_rows = jnp.stack([k_win, v_win], axis=2)
    if win_past is None:
        o_win = sliding_prompt(qrg, k_win, v_win)
        new_win = win_rows[:, T - min(WINDOW, T):]
    else:
        n_buf = win_past.shape[1]
        ext = jnp.concatenate([win_past.astype(dt), win_rows], axis=1)
        kpos = pos[0] - n_buf + jnp.arange(n_buf + T)
        o_win = band_attend(qrg, ext[:, :, 0], ext[:, :, 1], pos, kpos)
        new_win = ext[:, T:]
    gates = jax.nn.sigmoid(parts['nsa_gate'].astype(jnp.float32)).reshape(B, T, NSA_KV_HEADS, NSA_REP, 3)
    o = gates[..., 0:1] * o_cmp + gates[..., 1:2] * o_slc + gates[..., 2:3] * o_win
    return o.reshape(B, T, NSA_WIDTH).astype(dt), new_rows, new_win


def pool_mixer(u, prev, pos, w_pool, scale):
    B, T, C = u.shape
    P = POOL_MAX - 1
    ext = jnp.concatenate([prev.astype(u.dtype), u], axis=1).astype(jnp.float32)
    cs = jnp.concatenate([jnp.zeros((B, 1, C), jnp.float32), jnp.cumsum(ext, axis=1)], axis=1)
    end = cs[:, P + 1:]
    means = []
    for gi, w in enumerate(POOL_WINDOWS):
        sl = slice(gi * POOL_GROUP_DIM, (gi + 1) * POOL_GROUP_DIM)
        start = cs[:, P + 1 - w:P + 1 - w + T, sl]
        cnt = jnp.minimum(pos + 1, w).astype(jnp.float32)[None, :, None]
        means.append((end[..., sl] - start) / cnt)
    pooled = (jnp.concatenate(means, axis=-1) - ext[:, P:]).reshape(B, T, POOL_GROUPS, POOL_GROUP_DIM)
    y = jnp.einsum('btgc,gcd->btgd', pooled, w_pool.astype(jnp.float32)).reshape(B, T, C) * scale
    return y.astype(u.dtype), ext[:, -P:].astype(u.dtype)


def mixer_layer(x, pos0, gla_s0, pool_prev, nsa_past, win_past,
                w_in, gla_w2, gla_b, gla_norm_g, cmp_pos, cmp_w1, cmp_w2, pool_w, pool_scale, w_out):
    B, T, _ = x.shape
    pos = pos0 + jnp.arange(T, dtype=jnp.int32)
    parts = split_proj(mm3(x, w_in))
    y_gla, s_gla = gla_mixer(parts, gla_s0, gla_w2, gla_b, gla_norm_g)
    y_nsa, nsa_rows, win_rows = nsa_mixer(parts, pos, nsa_past, win_past, cmp_pos, cmp_w1, cmp_w2)
    y_pool, pool_rows = pool_mixer(parts['pool'], pool_prev, pos, pool_w, pool_scale)
    y = mm3(jnp.concatenate([y_gla, y_nsa, y_pool], axis=-1), w_out)
    return y, nsa_rows, win_rows, s_gla, pool_rows


def swiglu(x, wg, wu, wd):
    return mm3(jax.nn.silu(mm3(x, wg)) * mm3(x, wu), wd)


def moe_ffn(x, router, wg, wu, wd):
    B, T, D = x.shape
    n_tok = B * T
    xt = x.reshape(n_tok, D)
    logits = (xt @ router).astype(jnp.float32)
    top_v, top_i = lax.top_k(logits, TOP_K)
    gates = jax.nn.softmax(top_v, axis=-1)
    n_asg = n_tok * TOP_K
    e_flat = top_i.reshape(n_asg)
    tok_flat = jnp.arange(n_asg, dtype=jnp.int32) // TOP_K
    gate_flat = gates.reshape(n_asg)
    blk = MOE_ROW_BLOCK if n_asg >= N_EXPERTS * MOE_ROW_BLOCK else MOE_MIN_BLOCK
    n_blk = -(-(n_asg + N_EXPERTS * (blk - 1)) // blk)
    order = jnp.argsort(e_flat)
    e_sorted = e_flat[order]
    counts = jnp.bincount(e_flat, length=N_EXPERTS)
    padded = (counts + blk - 1) // blk * blk
    pad_end = jnp.cumsum(padded)
    pad_start = pad_end - padded
    start = jnp.cumsum(counts) - counts
    dest = pad_start[e_sorted] + jnp.arange(n_asg) - start[e_sorted]
    row_tok = jnp.zeros((n_blk * blk,), jnp.int32).at[dest].set(tok_flat[order])
    row_gate = jnp.zeros((n_blk * blk,), jnp.float32).at[dest].set(gate_flat[order])
    blk_expert = jnp.minimum(jnp.searchsorted(pad_end, jnp.arange(n_blk) * blk, side='right'), N_EXPERTS - 1)

    def expert_block(args):
        rows, e = args
        xb = xt[rows]
        return (jax.nn.silu(xb @ wg[e]) * (xb @ wu[e])) @ wd[e]

    out = lax.map(expert_block, (row_tok.reshape(n_blk, blk), blk_expert))
    y = jnp.zeros((n_tok, D), jnp.float32).at[row_tok].add(
        out.reshape(n_blk * blk, D).astype(jnp.float32) * row_gate[:, None])
    return y.reshape(B, T, D).astype(x.dtype)


def kernel(x_prompt, x_sample, cache_nsa, page_table, state_win, state_gla, state_pool, w_in, gla_gate_w2, gla_gate_b, gla_norm_g, nsa_cmp_pos, nsa_cmp_w1, nsa_cmp_w2, pool_w, pool_scale, w_out, ln1_g, ln1_b, ln2_g, ln2_b, ffn_w_gate, ffn_w_up, ffn_w_down, moe_router, moe_w_gate, moe_w_up, moe_w_down):
    n_prompt, t_len, d = x_prompt.shape
    n_dec = x_sample.shape[0]
    xp, xs = x_prompt.reshape(n_prompt * t_len, d), x_sample
    nsa_p, nsa_s, win_p, win_s, gla_p, gla_s, pool_p, pool_s = [], [], [], [], [], [], [], []
    for l in range(DEPTH):
        lw = (w_in[l], gla_gate_w2[l], gla_gate_b[l], gla_norm_g[l], nsa_cmp_pos[l], nsa_cmp_w1[l],
              nsa_cmp_w2[l], pool_w[l], pool_scale[l], w_out[l])
        xp, r_p, w_p, g_p, p_p = prompt_mixer(
            xp, pack_w_in(w_in[l]), w_out[l].astype(BF16), ln1_g[l], ln1_b[l], gla_gate_w2[l], gla_gate_b[l],
            gla_norm_g[l], nsa_cmp_pos[l], nsa_cmp_w1[l], nsa_cmp_w2[l], pool_w[l], pool_scale[l], n_prompt, t_len)
        past_rows = cache_nsa[l, page_table].reshape(n_dec, -1, 4, NSA_KV_HEADS, HEAD_DIM)
        hs, r_s, w_s, g_s, p_s = mixer_layer(xs, PAST_LEN, state_gla[l], state_pool[l], past_rows, state_win[l], *lw)
        xs = layer_norm(ALPHA * xs + hs, ln1_g[l], ln1_b[l])
        i = l // 2
        if l % 2 == 0:
            xp = ffn_ln(xp, ffn_w_gate[i].astype(BF16), ffn_w_up[i].astype(BF16), ffn_w_down[i].astype(BF16),
                        ln2_g[l], ln2_b[l])
            fs = swiglu(xs, ffn_w_gate[i], ffn_w_up[i], ffn_w_down[i])
        else:
            fp = moe_ffn(xp.reshape(n_prompt, t_len, d), moe_router[i], moe_w_gate[i], moe_w_up[i], moe_w_down[i])
            xp = layer_norm(ALPHA * xp + fp.reshape(n_prompt * t_len, d), ln2_g[l], ln2_b[l])
            fs = moe_ffn(xs, moe_router[i], moe_w_gate[i], moe_w_up[i], moe_w_down[i])
        xs = layer_norm(ALPHA * xs + fs, ln2_g[l], ln2_b[l])
        nsa_p.append(r_p); nsa_s.append(r_s); win_p.append(w_p); win_s.append(w_s)
        gla_p.append(g_p); gla_s.append(g_s); pool_p.append(p_p); pool_s.append(p_s)
    return (xp.reshape(n_prompt, t_len, d), xs, jnp.stack(nsa_p), jnp.stack(nsa_s), jnp.stack(win_p),
            jnp.stack(win_s), jnp.stack(gla_p), jnp.stack(gla_s), jnp.stack(pool_p), jnp.stack(pool_s))
```

```python
import functools

import jax
import jax.numpy as jnp
from jax import lax
from jax.experimental import pallas as pl
from jax.experimental.pallas import tpu as pltpu

D_MODEL = 2048
DEPTH = 2
PAST_LEN = 16384
HEAD_DIM = 128
GLA_HEADS = 4
GLA_DK = 64
GLA_DV = 128
GLA_RANK = 16
GLA_GATE_NORM = 16.0
GLA_CHUNK = 64
GLA_WIDTH = GLA_HEADS * GLA_DV
NSA_HEADS = 8
NSA_KV_HEADS = 2
NSA_REP = NSA_HEADS // NSA_KV_HEADS
NSA_WIDTH = NSA_HEADS * HEAD_DIM
CMP_LEN = 32
CMP_STRIDE = 16
CMP_HIDDEN = 128
SEL_BLOCK = 64
SEL_TOPN = 16
WINDOW = 512
Q_BLOCK = 128
POOL_GROUPS = 4
POOL_GROUP_DIM = 128
POOL_WIDTH = POOL_GROUPS * POOL_GROUP_DIM
POOL_WINDOWS = (2, 4, 8, 16)
POOL_MAX = 16
ROPE_THETA = 500000.0
ROPE_DIM = HEAD_DIM // 4
N_EXPERTS = 8
TOP_K = 2
MOE_ROW_BLOCK = 128
MOE_MIN_BLOCK = 8
ALPHA = (2 * DEPTH) ** 0.25
LN_EPS = 1e-5
RMS_EPS = 1e-6

PROJ_SIZES = (
    ('gla_q', GLA_HEADS * GLA_DK), ('gla_k', GLA_HEADS * GLA_DK), ('gla_v', GLA_HEADS * GLA_DV),
    ('gla_glr', GLA_RANK), ('gla_r', GLA_HEADS * GLA_DV),
    ('nsa_q', NSA_HEADS * HEAD_DIM),
    ('cmp_k', NSA_KV_HEADS * HEAD_DIM), ('cmp_v', NSA_KV_HEADS * HEAD_DIM),
    ('slc_k', NSA_KV_HEADS * HEAD_DIM), ('slc_v', NSA_KV_HEADS * HEAD_DIM),
    ('win_k', NSA_KV_HEADS * HEAD_DIM), ('win_v', NSA_KV_HEADS * HEAD_DIM),
    ('nsa_gate', 3 * NSA_HEADS),
    ('pool', POOL_WIDTH),
)

GLA_SUB = 16
SEL_PAD = 128

BF16 = jnp.bfloat16
F32 = jnp.float32
NEG_BIG = -1e30
VMEM_LIMIT_BYTES = 56 * 1024 * 1024

COL_NSA_Q = 0
COL_ROWS = 1024
COL_WIN = 2048
COL_POOL = 2560
COL_GLA_V = 3072
COL_GLA_R = 3584
COL_GLA_Q = 4096
COL_GLA_K = 4352
COL_SMALL = 4608
PACKED_WIDTH = 4736
SMALL_GATE_OFF = GLA_RANK


def _params(*sem):
    return pltpu.CompilerParams(dimension_semantics=sem, vmem_limit_bytes=VMEM_LIMIT_BYTES)


def _proj_offsets():
    out, off = {}, 0
    for name, size in PROJ_SIZES:
        out[name] = (off, size)
        off += size
    return out


def pack_w_in(w):
    offs = _proj_offsets()
    sl = lambda n: w[:, offs[n][0]:offs[n][0] + offs[n][1]]
    pad = jnp.zeros((w.shape[0], 128 - GLA_RANK - 3 * NSA_HEADS), w.dtype)
    cols = [sl('nsa_q'), sl('cmp_k'), sl('cmp_v'), sl('slc_k'), sl('slc_v'), sl('win_k'), sl('win_v'),
            sl('pool'), sl('gla_v'), sl('gla_r'), sl('gla_q'), sl('gla_k'), sl('gla_glr'), sl('nsa_gate'), pad]
    return jnp.concatenate(cols, axis=1).astype(BF16)


def rope_tables(pos):
    half = ROPE_DIM // 2
    inv_freq = ROPE_THETA ** (-jnp.arange(half, dtype=F32) / half)
    ang = pos.astype(F32)[:, None] * inv_freq[None, :]
    cos, sin = jnp.cos(ang), jnp.sin(ang)
    t = pos.shape[0]
    c = jnp.concatenate([cos, cos, jnp.ones((t, HEAD_DIM - ROPE_DIM), F32)], axis=1)
    sa = jnp.concatenate([-sin, jnp.zeros((t, HEAD_DIM - half), F32)], axis=1)
    sb = jnp.concatenate([jnp.zeros((t, half), F32), sin, jnp.zeros((t, HEAD_DIM - ROPE_DIM), F32)], axis=1)
    return c, sa, sb


def _pick_tile(n, pref):
    for t in pref:
        if n % t == 0:
            return t
    return n


def _mm_kernel(x_ref, w_ref, o_ref, xb_ref):
    @pl.when(pl.program_id(1) == 0)
    def _():
        xb_ref[...] = x_ref[...].astype(BF16)

    o_ref[...] = jnp.dot(xb_ref[...], w_ref[...].astype(BF16), preferred_element_type=F32)


def matmul(x, w):
    m, k = x.shape
    n = w.shape[1]
    tm = _pick_tile(m, tuple(t for t in (1024, 512, 256, 128, 64, 32, 16, 8) if t * k <= 2048 * 1024))
    tn = 512 if n >= 512 else n
    return pl.pallas_call(
        _mm_kernel,
        grid=(m // tm, pl.cdiv(n, tn)),
        in_specs=[pl.BlockSpec((tm, k), lambda i, j: (i, 0)),
                  pl.BlockSpec((k, tn), lambda i, j: (0, j))],
        out_specs=pl.BlockSpec((tm, tn), lambda i, j: (i, j)),
        out_shape=jax.ShapeDtypeStruct((m, n), F32),
        scratch_shapes=[pltpu.VMEM((tm, k), BF16)],
        compiler_params=_params("arbitrary", "arbitrary"),
        name="matmul",
    )(x, w)


def mm3(x, w):
    lead = x.shape[:-1]
    return matmul(x.reshape(-1, x.shape[-1]), w).reshape(*lead, w.shape[1])


def _rope(x, c, sa, sb):
    return x * c + pltpu.roll(x, HEAD_DIM - ROPE_DIM // 2, 1) * sa + pltpu.roll(x, ROPE_DIM // 2, 1) * sb


def _nsa_prep_kernel(q_ref, rows_ref, win_ref, small_ref, c_ref, sa_ref, sb_ref,
                     rows_o, win_o, qn_o, qr_o, ks_o, vs_o, kw_o, vw_o, gate_o):
    c, sa, sb = c_ref[...], sa_ref[...], sb_ref[...]
    scale = HEAD_DIM ** -0.5
    hd = HEAD_DIM
    for h in range(NSA_HEADS):
        x = q_ref[:, h * hd:(h + 1) * hd]
        qn_o[0, h] = (x * scale).astype(BF16)
        qr_o[0, h] = (_rope(x, c, sa, sb) * scale).astype(BF16)
    ones = jnp.ones((q_ref.shape[0], hd), BF16)
    rows_o[:, 0:4 * hd] = rows_ref[:, 0:4 * hd]
    for g in range(NSA_KV_HEADS):
        k = _rope(rows_ref[:, (4 + g) * hd:(5 + g) * hd], c, sa, sb)
        rows_o[:, (4 + g) * hd:(5 + g) * hd] = k
        ks_o[0, g] = k.astype(BF16)
        v = rows_ref[:, (6 + g) * hd:(7 + g) * hd]
        rows_o[:, (6 + g) * hd:(7 + g) * hd] = v
        vs_o[0, g, :, 0:hd] = v.astype(BF16)
        vs_o[0, g, :, hd:2 * hd] = ones
        k = _rope(win_ref[:, g * hd:(g + 1) * hd], c, sa, sb)
        win_o[:, g * hd:(g + 1) * hd] = k
        kw_o[0, g] = k.astype(BF16)
        v = win_ref[:, (2 + g) * hd:(3 + g) * hd]
        win_o[:, (2 + g) * hd:(3 + g) * hd] = v
        vw_o[0, g, :, 0:hd] = v.astype(BF16)
        vw_o[0, g, :, hd:2 * hd] = ones
    sig = jax.nn.sigmoid(small_ref[...])
    per_g = 3 * NSA_REP
    for g in range(NSA_KV_HEADS):
        gate_o[0, g] = pltpu.roll(sig, 128 - SMALL_GATE_OFF - g * per_g, 1)


def nsa_prep(p, tables, n_batch, t_len):
    tr = _pick_tile(t_len, (512, 256, 128, 64, 32, 16))
    nt = t_len // tr
    n = n_batch * t_len
    hd = HEAD_DIM
    row = lambda w, cb: pl.BlockSpec((tr, w), lambda b, i: (b * nt + i, cb))
    tab = pl.BlockSpec((tr, hd), lambda b, i: (i, 0))
    head = lambda nh, w: pl.BlockSpec((1, nh, tr, w), lambda b, i: (b, 0, i, 0))
    out_shape = (
        jax.ShapeDtypeStruct((n, 8 * hd), F32),
        jax.ShapeDtypeStruct((n, 4 * hd), F32),
        jax.ShapeDtypeStruct((n_batch, NSA_HEADS, t_len, hd), BF16),
        jax.ShapeDtypeStruct((n_batch, NSA_HEADS, t_len, hd), BF16),
        jax.ShapeDtypeStruct((n_batch, NSA_KV_HEADS, t_len, hd), BF16),
        jax.ShapeDtypeStruct((n_batch, NSA_KV_HEADS, t_len, 2 * hd), BF16),
        jax.ShapeDtypeStruct((n_batch, NSA_KV_HEADS, t_len, hd), BF16),
        jax.ShapeDtypeStruct((n_batch, NSA_KV_HEADS, t_len, 2 * hd), BF16),
        jax.ShapeDtypeStruct((n_batch, NSA_KV_HEADS, t_len, 128), F32),
    )
    return pl.pallas_call(
        _nsa_prep_kernel,
        grid=(n_batch, nt),
        in_specs=[row(8 * hd, COL_NSA_Q // (8 * hd)), row(8 * hd, COL_ROWS // (8 * hd)),
                  row(4 * hd, COL_WIN // (4 * hd)), row(128, COL_SMALL // 128), tab, tab, tab],
        out_specs=(row(8 * hd, 0), row(4 * hd, 0), head(NSA_HEADS, hd), head(NSA_HEADS, hd),
                   head(NSA_KV_HEADS, hd), head(NSA_KV_HEADS, 2 * hd), head(NSA_KV_HEADS, hd),
                   head(NSA_KV_HEADS, 2 * hd), head(NSA_KV_HEADS, 128)),
        out_shape=out_shape,
        compiler_params=_params("arbitrary", "arbitrary"),
        name="nsa_prep",
    )(p, p, p, p, *tables)


def _nsa_cmp_kernel(x_ref, pe_ref, w1_ref, w2_ref, o_ref):
    nh = o_ref.shape[0]
    h_lo = jnp.zeros((nh, CMP_HIDDEN), F32)
    h_hi = jnp.zeros((nh, CMP_HIDDEN), F32)
    for j in range(CMP_STRIDE):
        xj = x_ref[pl.ds(j, nh, stride=CMP_STRIDE), :]
        h_lo += jnp.dot((xj + pe_ref[j:j + 1, :]).astype(BF16), w1_ref[j].astype(BF16), preferred_element_type=F32)
        h_hi += jnp.dot((xj + pe_ref[CMP_STRIDE + j:CMP_STRIDE + j + 1, :]).astype(BF16),
                        w1_ref[CMP_STRIDE + j].astype(BF16), preferred_element_type=F32)
    h = jax.nn.gelu(h_lo + pltpu.roll(h_hi, nh - 1, 0))
    o_ref[...] = jnp.dot(h.astype(BF16), w2_ref[...].astype(BF16), preferred_element_type=F32).astype(BF16)


def nsa_compress_prompt(rows, cmp_pos, cmp_w1, cmp_w2, n_batch, t_len):
    nh = t_len // CMP_STRIDE
    hd = HEAD_DIM
    rows3 = rows.reshape(n_batch, t_len, 8 * hd)
    return pl.pallas_call(
        _nsa_cmp_kernel,
        grid=(n_batch, 2, NSA_KV_HEADS),
        in_specs=[pl.BlockSpec((None, t_len, hd), lambda b, kd, g: (b, 0, kd * NSA_KV_HEADS + g)),
                  pl.BlockSpec((None, CMP_LEN, hd), lambda b, kd, g: (kd, 0, 0)),
                  pl.BlockSpec((None, CMP_LEN, hd, CMP_HIDDEN), lambda b, kd, g: (kd, 0, 0, 0)),
                  pl.BlockSpec((None, CMP_HIDDEN, hd), lambda b, kd, g: (kd, 0, 0))],
        out_specs=pl.BlockSpec((None, None, None, nh, hd), lambda b, kd, g: (b, kd, g, 0, 0)),
        out_shape=jax.ShapeDtypeStruct((n_batch, 2, NSA_KV_HEADS, nh, hd), BF16),
        compiler_params=_params("arbitrary", "arbitrary", "arbitrary"),
        name="nsa_compress",
    )(rows3, cmp_pos, cmp_w1, cmp_w2)


def _nsa_select_kernel(qn_ref, kc_ref, vc_ref, ovt_ref, oc_ref, selb_ref, *, n_cmp, n_top):
    rep, tq, hd = qn_ref.shape[1], qn_ref.shape[2], qn_ref.shape[3]
    n_cmp_pad = kc_ref.shape[0]
    n_sel = ovt_ref.shape[0]
    q0 = pl.program_id(2) * tq
    q = qn_ref[0].reshape(rep * tq, hd)
    s = lax.dot_general(q, kc_ref[...], (((1,), (1,)), ((), ())), preferred_element_type=F32)
    row = lax.broadcasted_iota(jnp.int32, (rep * tq, n_cmp_pad), 0)
    col = lax.broadcasted_iota(jnp.int32, (rep * tq, n_cmp_pad), 1)
    qpos = q0 + (row & (tq - 1))
    mask = (col * CMP_STRIDE + (CMP_LEN - 1) <= qpos) & (col < n_cmp)
    s = jnp.where(mask, s, -jnp.inf)
    m = jnp.max(s, axis=-1, keepdims=True)
    m = jnp.where(m > -jnp.inf, m, 0.0)
    p = jnp.where(mask, jnp.exp(s - m), 0.0)
    p = p / jnp.maximum(jnp.sum(p, axis=-1, keepdims=True), 1e-30)
    oc = jnp.dot(p.astype(BF16), vc_ref[...], preferred_element_type=F32)
    oc_ref[0] = oc.reshape(rep, tq, hd).astype(BF16)
    psum = p[0:tq]
    for r in range(1, rep):
        psum = psum + p[r * tq:(r + 1) * tq]
    imp = lax.dot_general(ovt_ref[...], psum, (((1,), (1,)), ((), ())), preferred_element_type=F32,
                          precision=lax.Precision.HIGHEST)
    blk = lax.broadcasted_iota(jnp.int32, (n_sel, tq), 0)
    cur = (q0 + lax.broadcasted_iota(jnp.int32, (n_sel, tq), 1)) // SEL_BLOCK
    forced = (blk == 0) | (blk == cur) | (blk == cur - 1)
    v = jnp.where(blk <= cur, jnp.where(forced, jnp.inf, imp), -jnp.inf)
    rank = jnp.zeros((n_sel, tq), jnp.int32)
    for i in range(n_sel):
        vi = v[i:i + 1, :]
        ahead = (vi > v) | ((vi == v) & (blk > i))
        rank = rank + ahead.astype(jnp.int32)
    selb_t = jnp.where((rank < n_top) & (v > -jnp.inf), 0.0, NEG_BIG)
    pad = jnp.full((SEL_PAD - n_sel, tq), NEG_BIG, F32)
    selb_ref[0, 0] = jnp.concatenate([selb_t, pad], axis=0).T.astype(BF16)


def nsa_select(qn, cmp_kv, n_cmp, t_k):
    n_batch, _, t_len, hd = qn.shape
    n_cmp_pad = cmp_kv.shape[3]
    n_sel = -(-t_k // SEL_BLOCK)
    tq = _pick_tile(t_len, (256, 128, 64, 32, 16))
    ci = jnp.arange(n_cmp_pad)[None, :]
    sj = jnp.arange(n_sel)[:, None]
    overlap_t = ((ci * CMP_STRIDE <= sj * SEL_BLOCK + SEL_BLOCK - 1) &
                 (ci * CMP_STRIDE + CMP_LEN - 1 >= sj * SEL_BLOCK) & (ci < n_cmp)).astype(F32)
    kern = functools.partial(_nsa_select_kernel, n_cmp=n_cmp, n_top=min(SEL_TOPN, n_sel))
    return pl.pallas_call(
        kern,
        grid=(n_batch, NSA_KV_HEADS, t_len // tq),
        in_specs=[pl.BlockSpec((1, NSA_REP, tq, hd), lambda b, g, i: (b, g, i, 0)),
                  pl.BlockSpec((None, None, None, n_cmp_pad, hd), lambda b, g, i: (b, 0, g, 0, 0)),
                  pl.BlockSpec((None, None, None, n_cmp_pad, hd), lambda b, g, i: (b, 1, g, 0, 0)),
                  pl.BlockSpec((n_sel, n_cmp_pad), lambda b, g, i: (0, 0))],
        out_specs=(pl.BlockSpec((1, NSA_REP, tq, hd), lambda b, g, i: (b, g, i, 0)),
                   pl.BlockSpec((1, 1, tq, SEL_PAD), lambda b, g, i: (b, g, i, 0))),
        out_shape=(jax.ShapeDtypeStruct((n_batch, NSA_HEADS, t_len, hd), BF16),
                   jax.ShapeDtypeStruct((n_batch, NSA_KV_HEADS, t_len, SEL_PAD), BF16)),
        compiler_params=_params("arbitrary", "arbitrary", "arbitrary"),
        name="nsa_select",
    )(qn, cmp_kv, cmp_kv, overlap_t)


def _nsa_attn_kernel(qr_ref, oc_ref, selb_ref, gate_ref, ks_ref, vs_ref, kw_ref, vw_ref, e_ref, o_ref,
                     m_scr, acc_scr, *, tk, wk):
    rep, qb, hd = qr_ref.shape[1], qr_ref.shape[2], qr_ref.shape[3]
    nr = rep * qb
    q0 = pl.program_id(2) * qb
    q = qr_ref[0].reshape(nr, hd)
    selb = selb_ref[0, 0]
    nt = (((1,), (1,)), ((), ()))

    def sel_scores(t):
        k = ks_ref[0, 0, pl.ds(pl.multiple_of(t * tk, tk), tk), :]
        s = lax.dot_general(q, k, nt, preferred_element_type=F32)
        bias = jnp.dot(selb, e_ref[t], preferred_element_type=F32)
        return (s.reshape(rep, qb, tk) + bias[None]).reshape(nr, tk)

    def sel_values(t):
        return vs_ref[0, 0, pl.ds(pl.multiple_of(t * tk, tk), tk), :]

    td = q0 // tk
    qpos = q0 + (lax.broadcasted_iota(jnp.int32, (nr, tk), 0) & (qb - 1))
    kpos = td * tk + lax.broadcasted_iota(jnp.int32, (nr, tk), 1)
    s = jnp.where(kpos <= qpos, sel_scores(td), NEG_BIG)
    m = jnp.max(s, axis=-1, keepdims=True)
    m_scr[...] = m
    acc_scr[...] = jnp.dot(jnp.exp(s - m).astype(BF16), sel_values(td), preferred_element_type=F32)

    def body(t, carry):
        s = sel_scores(t)
        m_old = m_scr[...]
        m_new = jnp.maximum(m_old, jnp.max(s, axis=-1, keepdims=True))
        p = jnp.exp(s - m_new).astype(BF16)
        acc_scr[...] = jnp.exp(m_old - m_new) * acc_scr[...] + jnp.dot(p, sel_values(t), preferred_element_type=F32)
        m_scr[...] = m_new
        return carry

    lax.fori_loop(0, td, body, 0)
    acc = acc_scr[...]
    o_sel = acc[:, 0:hd] / jnp.maximum(acc[:, hd:hd + 1], 1e-30)

    kstart = pl.multiple_of(jnp.maximum(q0 - WINDOW, 0), qb)
    kw = kw_ref[0, 0, pl.ds(kstart, wk), :]
    s = lax.dot_general(q, kw, nt, preferred_element_type=F32)
    qpos = q0 + (lax.broadcasted_iota(jnp.int32, (nr, wk), 0) & (qb - 1))
    kpos = kstart + lax.broadcasted_iota(jnp.int32, (nr, wk), 1)
    s = jnp.where((kpos <= qpos) & (kpos > qpos - WINDOW), s, NEG_BIG)
    m = jnp.max(s, axis=-1, keepdims=True)
    accw = jnp.dot(jnp.exp(s - m).astype(BF16), vw_ref[0, 0, pl.ds(kstart, wk), :], preferred_element_type=F32)
    o_win = accw[:, 0:hd] / jnp.maximum(accw[:, hd:hd + 1], 1e-30)

    gates = gate_ref[0, 0]
    for r in range(rep):
        rows = slice(r * qb, (r + 1) * qb)
        o = (gates[:, 3 * r:3 * r + 1] * oc_ref[0, r].astype(F32)
             + gates[:, 3 * r + 1:3 * r + 2] * o_sel[rows]
             + gates[:, 3 * r + 2:3 * r + 3] * o_win[rows])
        o_ref[:, r * hd:(r + 1) * hd] = o.astype(BF16)


def nsa_attend(qr, o_cmp, selb, gates, ks, vs, kw, vw):
    n_batch, _, t_len, hd = qr.shape
    n_sel = selb.shape[3]
    qb = Q_BLOCK
    tk = min(512, t_len)
    wk = min(WINDOW + qb, t_len)
    nq = t_len // qb
    n_tiles = t_len // tk
    key_blk = (jnp.arange(n_tiles)[:, None, None] * tk + jnp.arange(tk)[None, None, :]) // SEL_BLOCK
    e = (key_blk == jnp.arange(n_sel)[None, :, None]).astype(BF16)
    kern = functools.partial(_nsa_attn_kernel, tk=tk, wk=wk)
    per_q = lambda nh, w: pl.BlockSpec((1, nh, qb, w), lambda b, g, i: (b, g, i, 0))
    full = lambda w: pl.BlockSpec((1, 1, t_len, w), lambda b, g, i: (b, g, 0, 0))
    return pl.pallas_call(
        kern,
        grid=(n_batch, NSA_KV_HEADS, nq),
        in_specs=[per_q(NSA_REP, hd), per_q(NSA_REP, hd), per_q(1, n_sel), per_q(1, 128),
                  full(hd), full(2 * hd), full(hd), full(2 * hd),
                  pl.BlockSpec((n_tiles, n_sel, tk), lambda b, g, i: (0, 0, 0))],
        out_specs=pl.BlockSpec((qb, NSA_REP * hd), lambda b, g, i: (b * nq + i, g)),
        out_shape=jax.ShapeDtypeStruct((n_batch * t_len, NSA_HEADS * hd), BF16),
        scratch_shapes=[pltpu.VMEM((NSA_REP * qb, 1), F32), pltpu.VMEM((NSA_REP * qb, 2 * hd), F32)],
        compiler_params=_params("arbitrary", "arbitrary", "arbitrary"),
        name="nsa_attend",
    )(qr, o_cmp, selb, gates, ks, vs, kw, vw, e)


def _gla_kernel(q_ref, k_ref, v_ref, r_ref, small_ref, w2_ref, b2_ref, ng_ref, s0_ref, y_ref, sf_ref, s_scr):
    tb = q_ref.shape[0]
    c, sub, dk, dv = GLA_CHUNK, GLA_SUB, GLA_DK, GLA_DV
    n_sub = c // sub
    t = pl.program_id(1)

    @pl.when(t == 0)
    def _():
        s_scr[...] = s0_ref[0]

    z = jnp.dot(small_ref[:, 0:GLA_RANK].astype(BF16), w2_ref[...].astype(BF16),
                preferred_element_type=F32) + b2_ref[...]
    g_all = (jnp.minimum(z, 0.0) - jnp.log1p(jnp.exp(-jnp.abs(z)))) / GLA_GATE_NORM
    ri = lax.broadcasted_iota(jnp.int32, (c, c), 0)
    ci = lax.broadcasted_iota(jnp.int32, (c, c), 1)
    tril = ri >= ci
    cum = tril.astype(F32)
    rsub = lax.broadcasted_iota(jnp.int32, (c, dk), 0) // sub
    eye = lax.broadcasted_iota(jnp.int32, (dk, dk), 0) == lax.broadcasted_iota(jnp.int32, (dk, dk), 1)
    for cc in range(tb // c):
        rows = slice(cc * c, (cc + 1) * c)
        b_all = jnp.dot(cum, g_all[rows], preferred_element_type=F32, precision=lax.Precision.HIGHEST)
        for h in range(GLA_HEADS):
            b = b_all[:, h * dk:(h + 1) * dk]
            qh = q_ref[rows, h * dk:(h + 1) * dk] * (dk ** -0.5)
            kh = k_ref[rows, h * dk:(h + 1) * dk]
            vh = v_ref[rows, h * dv:(h + 1) * dv]
            a_rows = []
            for i in range(n_sub):
                ref = b[sub * i - 1:sub * i, :] if i else jnp.zeros((1, dk), F32)
                rs = slice(sub * i, sub * (i + 1))
                qi = (qh[rs] * jnp.exp(b[rs] - ref)).astype(BF16)
                ki = jnp.where(rsub <= i, kh * jnp.exp(ref - b), 0.0).astype(BF16)
                a_rows.append(lax.dot_general(qi, ki, (((1,), (1,)), ((), ())), preferred_element_type=F32))
            a = jnp.where(tril, jnp.concatenate(a_rows, axis=0), 0.0)
            s_old = s_scr[h]
            o = jnp.dot(a.astype(BF16), vh.astype(BF16), preferred_element_type=F32)
            o += jnp.dot((qh * jnp.exp(b)).astype(BF16), s_old.astype(BF16), preferred_element_type=F32)
            b_last = b[c - 1:c, :]
            ke = (kh * jnp.exp(b_last - b)).astype(BF16)
            upd = lax.dot_general(ke, vh.astype(BF16), (((0,), (0,)), ((), ())), preferred_element_type=F32)
            decay = jnp.exp(jnp.sum(jnp.where(eye, jnp.broadcast_to(b_last, (dk, dk)), 0.0), axis=1, keepdims=True))
            s_scr[h] = decay * s_old + upd
            o = o * lax.rsqrt(jnp.mean(o * o, axis=-1, keepdims=True) + RMS_EPS)
            y = o * ng_ref[:, h * dv:(h + 1) * dv] * jax.nn.silu(r_ref[rows, h * dv:(h + 1) * dv])
            y_ref[rows, h * dv:(h + 1) * dv] = y.astype(BF16)

    @pl.when(t == pl.num_programs(1) - 1)
    def _():
        sf_ref[0] = s_scr[...]


def gla_mix(p, s0, w2, b2, norm_g, n_batch, t_len):
    tb = _pick_tile(t_len, (256, 128, 64))
    nt = t_len // tb
    row = lambda w, off: pl.BlockSpec((tb, w), lambda b, i: (b * nt + i, off // w))
    const = lambda shape: pl.BlockSpec(shape, lambda b, i: (0,) * len(shape))
    return pl.pallas_call(
        _gla_kernel,
        grid=(n_batch, nt),
        in_specs=[row(256, COL_GLA_Q), row(256, COL_GLA_K), row(512, COL_GLA_V), row(512, COL_GLA_R),
                  row(128, COL_SMALL), const((GLA_RANK, GLA_HEADS * GLA_DK)), const((1, GLA_HEADS * GLA_DK)),
                  const((1, GLA_WIDTH)),
                  pl.BlockSpec((1, GLA_HEADS, GLA_DK, GLA_DV), lambda b, i: (b, 0, 0, 0))],
        out_specs=(pl.BlockSpec((tb, GLA_WIDTH), lambda b, i: (b * nt + i, 0)),
                   pl.BlockSpec((1, GLA_HEADS, GLA_DK, GLA_DV), lambda b, i: (b, 0, 0, 0))),
        out_shape=(jax.ShapeDtypeStruct((n_batch * t_len, GLA_WIDTH), BF16),
                   jax.ShapeDtypeStruct((n_batch, GLA_HEADS, GLA_DK, GLA_DV), F32)),
        scratch_shapes=[pltpu.VMEM((GLA_HEADS, GLA_DK, GLA_DV), F32)],
        compiler_params=_params("arbitrary", "arbitrary"),
        name="gla_mix",
    )(p, p, p, p, p, w2, b2.reshape(1, -1), norm_g.reshape(1, -1), s0)


def _pool_kernel(u_ref, prev_ref, cnt_ref, w_ref, sc_ref, y_ref, halo):
    tb = u_ref.shape[0]
    gd = POOL_GROUP_DIM

    @pl.when(pl.program_id(1) == 0)
    def _():
        halo[...] = prev_ref[0]

    ext = jnp.concatenate([halo[...], u_ref[...]], axis=0)
    halo[...] = ext[tb:tb + POOL_MAX]
    for gi, w in enumerate(POOL_WINDOWS):
        x = ext[:, gi * gd:(gi + 1) * gd]
        s = x
        shift = 1
        while shift < w:
            s = s + pltpu.roll(s, shift, 0)
            shift *= 2
        pooled = s[POOL_MAX:] / cnt_ref[:, gi:gi + 1] - x[POOL_MAX:]
        y = jnp.dot(pooled.astype(BF16), w_ref[gi].astype(BF16), preferred_element_type=F32)
        y_ref[:, gi * gd:(gi + 1) * gd] = (y * sc_ref[:, gi * gd:(gi + 1) * gd]).astype(BF16)


def pool_mix(p, prev, pos0, w_pool, scale, n_batch, t_len):
    tb = _pick_tile(t_len, (512, 256, 128, 64, 32, 16))
    nt = t_len // tb
    pos = pos0 + jnp.arange(t_len, dtype=jnp.int32)
    cnt = jnp.stack([jnp.minimum(pos + 1, w).astype(F32) for w in POOL_WINDOWS], axis=1)
    cnt = jnp.pad(cnt, ((0, 0), (0, 128 - POOL_GROUPS)), constant_values=1.0)
    prev16 = jnp.pad(prev.astype(F32), ((0, 0), (1, 0), (0, 0)))
    return pl.pallas_call(
        _pool_kernel,
        grid=(n_batch, nt),
        in_specs=[pl.BlockSpec((tb, POOL_WIDTH), lambda b, i: (b * nt + i, COL_POOL // POOL_WIDTH)),
                  pl.BlockSpec((1, POOL_MAX, POOL_WIDTH), lambda b, i: (b, 0, 0)),
                  pl.BlockSpec((tb, 128), lambda b, i: (i, 0)),
                  pl.BlockSpec((POOL_GROUPS, POOL_GROUP_DIM, POOL_GROUP_DIM), lambda b, i: (0, 0, 0)),
                  pl.BlockSpec((1, POOL_WIDTH), lambda b, i: (0, 0))],
        out_specs=pl.BlockSpec((tb, POOL_WIDTH), lambda b, i: (b * nt + i, 0)),
        out_shape=jax.ShapeDtypeStruct((n_batch * t_len, POOL_WIDTH), BF16),
        scratch_shapes=[pltpu.VMEM((POOL_MAX, POOL_WIDTH), F32)],
        compiler_params=_params("arbitrary", "arbitrary"),
        name="pool_mix",
    )(p, prev16, cnt, w_pool, scale.reshape(1, -1))


def _layer_norm_rows(x, g, b):
    xc = x - jnp.mean(x, axis=-1, keepdims=True)
    var = jnp.mean(xc * xc, axis=-1, keepdims=True)
    return xc * lax.rsqrt(var + LN_EPS) * g + b


def _outproj_kernel(x_ref, yg_ref, yn_ref, yp_ref, w_ref, g_ref, b_ref, *rest):
    h = jnp.dot(yg_ref[...], w_ref[0:GLA_WIDTH, :], preferred_element_type=F32)
    h += jnp.dot(yn_ref[...], w_ref[GLA_WIDTH:GLA_WIDTH + NSA_WIDTH, :], preferred_element_type=F32)
    h += jnp.dot(yp_ref[...], w_ref[GLA_WIDTH + NSA_WIDTH:, :], preferred_element_type=F32)
    x1 = _layer_norm_rows(ALPHA * x_ref[...] + h, g_ref[...], b_ref[...])
    if len(rest) == 1:
        rest[0][...] = x1
    else:
        rt_ref, o_ref, lg_ref = rest
        o_ref[...] = x1
        lg_ref[...] = jnp.dot(x1, rt_ref[...], preferred_element_type=F32, precision=lax.Precision.HIGHEST)


def router_pad(router):
    return jnp.pad(router, ((0, 0), (0, 128 - router.shape[1])))


def outproj_ln(x, y_gla, y_nsa, y_pool, w_out_bf16, g, b, router=None):
    n, d = x.shape
    tm = _pick_tile(n, (512, 256, 128, 64, 32, 16, 8))
    row = lambda w: pl.BlockSpec((tm, w), lambda i: (i, 0))
    const = lambda r, c: pl.BlockSpec((r, c), lambda i: (0, 0))
    in_specs = [row(d), row(GLA_WIDTH), row(NSA_WIDTH), row(POOL_WIDTH), const(d, d), const(1, d), const(1, d)]
    args = [x, y_gla, y_nsa, y_pool, w_out_bf16, g.reshape(1, -1), b.reshape(1, -1)]
    out_specs, out_shape = row(d), jax.ShapeDtypeStruct((n, d), F32)
    if router is not None:
        in_specs.append(const(d, 128))
        args.append(router_pad(router))
        out_specs, out_shape = (out_specs, row(128)), (out_shape, jax.ShapeDtypeStruct((n, 128), F32))
    return pl.pallas_call(
        _outproj_kernel,
        grid=(n // tm,),
        in_specs=in_specs,
        out_specs=out_specs,
        out_shape=out_shape,
        compiler_params=_params("arbitrary"),
        name="outproj_ln",
    )(*args)


MOE_TM = 256
MOE_TF = 1024
MOE_TN = 512
ROUTE_TM = 512
PERMUTE_CHUNK = 1024


def _route_kernel(lg_ref, ii_ref, gf_ref, cnt_ref, carry, *, n_valid):
    tm = lg_ref.shape[0]
    i = pl.program_id(0)

    @pl.when(i == 0)
    def _():
        carry[...] = jnp.zeros_like(carry)

    lane = lax.broadcasted_iota(jnp.int32, (tm, 128), 1)
    valid = (i * tm + lax.broadcasted_iota(jnp.int32, (tm, 128), 0)) < n_valid
    lg = jnp.where(lane < N_EXPERTS, lg_ref[...], -jnp.inf)
    m1 = jnp.max(lg, axis=-1, keepdims=True)
    i1 = jnp.min(jnp.where(lg == m1, lane, 128), axis=-1, keepdims=True)
    lg2 = jnp.where(lane == i1, -jnp.inf, lg)
    m2 = jnp.max(lg2, axis=-1, keepdims=True)
    i2 = jnp.min(jnp.where(lg2 == m2, lane, 128), axis=-1, keepdims=True)
    t = jnp.exp(m2 - m1)
    g1 = 1.0 / (1.0 + t)
    g2 = t / (1.0 + t)
    oh1 = jnp.where((lane == i1) & valid, 1.0, 0.0)
    oh2 = jnp.where((lane == i2) & valid, 1.0, 0.0)
    cnt = oh1 + oh2
    strict = (lax.broadcasted_iota(jnp.int32, (tm, tm), 0) > lax.broadcasted_iota(jnp.int32, (tm, tm), 1))
    before = jnp.dot(strict.astype(BF16), cnt.astype(BF16), preferred_element_type=F32) + carry[...]
    r1 = jnp.sum(before * oh1, axis=-1, keepdims=True).astype(jnp.int32)
    r2 = jnp.sum(before * oh2, axis=-1, keepdims=True).astype(jnp.int32)
    carry[...] += jnp.sum(cnt, axis=0, keepdims=True)
    ii_ref[...] = jnp.where(lane == 0, i1, jnp.where(lane == 1, i2, jnp.where(lane == 2, r1, r2)))
    gf_ref[...] = jnp.where(lane == 0, g1, g2)
    cnt_ref[...] = carry[...]


def moe_route(logits, n_valid):
    npad = logits.shape[0]
    tm = ROUTE_TM
    row = pl.BlockSpec((tm, 128), lambda i: (i, 0))
    info, gates, counts = pl.pallas_call(
        functools.partial(_route_kernel, n_valid=n_valid),
        grid=(npad // tm,),
        in_specs=[row],
        out_specs=(row, row, pl.BlockSpec((1, 128), lambda i: (0, 0))),
        out_shape=(jax.ShapeDtypeStruct((npad, 128), jnp.int32), jax.ShapeDtypeStruct((npad, 128), F32),
                   jax.ShapeDtypeStruct((1, 128), F32)),
        scratch_shapes=[pltpu.VMEM((1, 128), F32)],
        compiler_params=_params("arbitrary"),
        name="moe_route",
    )(logits)
    return info[:, 0:2], info[:, 2:4], gates, counts[0, :N_EXPERTS].astype(jnp.int32)


def _gather_rows_kernel(idx_ref, src_ref, dst_ref, sem):
    ch = idx_ref.shape[2]
    base = pl.program_id(0) * ch

    def issue(r, c):
        pltpu.make_async_copy(src_ref.at[pl.ds(idx_ref[0, 0, r], 1)], dst_ref.at[pl.ds(base + r, 1)], sem).start()
        return c

    lax.fori_loop(0, ch, issue, 0, unroll=8)

    def drain(r, c):
        pltpu.make_async_copy(src_ref.at[pl.ds(0, 1)], dst_ref.at[pl.ds(0, 1)], sem).wait()
        return c

    lax.fori_loop(0, ch, drain, 0, unroll=8)


def gather_rows(src, idx):
    n = idx.shape[0]
    ch = PERMUTE_CHUNK
    return pl.pallas_call(
        _gather_rows_kernel,
        grid=(n // ch,),
        in_specs=[pl.BlockSpec((1, 1, ch), lambda i: (i, 0, 0), memory_space=pltpu.SMEM),
                  pl.BlockSpec(memory_space=pl.ANY)],
        out_specs=pl.BlockSpec(memory_space=pl.ANY),
        out_shape=jax.ShapeDtypeStruct((n, src.shape[1]), src.dtype),
        scratch_shapes=[pltpu.SemaphoreType.DMA(())],
        compiler_params=_params("arbitrary"),
        name="gather_rows",
    )(idx.reshape(n // ch, 1, ch), src)


def _moe_up_kernel(te_ref, tfirst_ref, tused_ref, x_ref, wg_ref, wu_ref, h_ref, wgb, wub):
    i = pl.program_id(1)

    @pl.when((i == 0) | (tfirst_ref[i] == 1))
    def _():
        wgb[...] = wg_ref[...].astype(BF16)
        wub[...] = wu_ref[...].astype(BF16)

    @pl.when(tused_ref[i] == 1)
    def _():
        xb = x_ref[...].astype(BF16)
        gate = jnp.dot(xb, wgb[...], preferred_element_type=F32)
        up = jnp.dot(xb, wub[...], preferred_element_type=F32)
        h_ref[...] = (jax.nn.silu(gate) * up).astype(BF16)

    @pl.when(tused_ref[i] == 0)
    def _():
        h_ref[...] = jnp.zeros_like(h_ref)


def _moe_down_kernel(te_ref, tfirst_ref, tused_ref, h_ref, wd_ref, y_ref, wdb):
    i = pl.program_id(1)

    @pl.when((i == 0) | (tfirst_ref[i] == 1))
    def _():
        wdb[...] = wd_ref[...].astype(BF16)

    @pl.when(tused_ref[i] == 1)
    def _():
        y_ref[...] = jnp.dot(h_ref[...], wdb[...], preferred_element_type=F32)

    @pl.when(tused_ref[i] == 0)
    def _():
        y_ref[...] = jnp.zeros_like(y_ref)


def moe_experts(xs, tile_e, tile_first, tile_used, wg, wu, wd):
    r, d = xs.shape
    d_ff = wg.shape[2]
    tm, tf, tn = MOE_TM, MOE_TF, MOE_TN
    n_tiles = r // tm
    h = pl.pallas_call(
        _moe_up_kernel,
        grid_spec=pltpu.PrefetchScalarGridSpec(
            num_scalar_prefetch=3,
            grid=(d_ff // tf, n_tiles),
            in_specs=[pl.BlockSpec((tm, d), lambda j, i, te, t1, tu: (i, 0)),
                      pl.BlockSpec((None, d, tf), lambda j, i, te, t1, tu: (te[i], 0, j)),
                      pl.BlockSpec((None, d, tf), lambda j, i, te, t1, tu: (te[i], 0, j))],
            out_specs=pl.BlockSpec((tm, tf), lambda j, i, te, t1, tu: (i, j)),
            scratch_shapes=[pltpu.VMEM((d, tf), BF16), pltpu.VMEM((d, tf), BF16)]),
        out_shape=jax.ShapeDtypeStruct((r, d_ff), BF16),
        compiler_params=_params("arbitrary", "arbitrary"),
        name="moe_up",
    )(tile_e, tile_first, tile_used, xs, wg, wu)
    return pl.pallas_call(
        _moe_down_kernel,
        grid_spec=pltpu.PrefetchScalarGridSpec(
            num_scalar_prefetch=3,
            grid=(d // tn, n_tiles),
            in_specs=[pl.BlockSpec((tm, d_ff), lambda j, i, te, t1, tu: (i, 0)),
                      pl.BlockSpec((None, d_ff, tn), lambda j, i, te, t1, tu: (te[i], 0, j))],
            out_specs=pl.BlockSpec((tm, tn), lambda j, i, te, t1, tu: (i, j)),
            scratch_shapes=[pltpu.VMEM((d_ff, tn), BF16)]),
        out_shape=jax.ShapeDtypeStruct((r, d), F32),
        compiler_params=_params("arbitrary", "arbitrary"),
        name="moe_down",
    )(tile_e, tile_first, tile_used, h, wd)


def _moe_combine_kernel(x_ref, y0_ref, y1_ref, gt_ref, g_ref, b_ref, o_ref):
    gt = gt_ref[...]
    y = gt[:, 0:1] * y0_ref[...] + gt[:, 1:2] * y1_ref[...]
    o_ref[...] = _layer_norm_rows(ALPHA * x_ref[...] + y, g_ref[...], b_ref[...])


def moe_combine_ln(x, yg, gates, row0, n_tok_pad, g, b):
    n, d = x.shape
    tm = _pick_tile(n, (512, 256, 128, 64, 32, 16, 8))
    o0, o1 = row0 // tm, (n_tok_pad + row0) // tm
    return pl.pallas_call(
        _moe_combine_kernel,
        grid=(n // tm,),
        in_specs=[pl.BlockSpec((tm, d), lambda i: (i, 0)),
                  pl.BlockSpec((tm, d), lambda i: (o0 + i, 0)), pl.BlockSpec((tm, d), lambda i: (o1 + i, 0)),
                  pl.BlockSpec((tm, 128), lambda i: (o0 + i, 0)),
                  pl.BlockSpec((1, d), lambda i: (0, 0)), pl.BlockSpec((1, d), lambda i: (0, 0))],
        out_specs=pl.BlockSpec((tm, d), lambda i: (i, 0)),
        out_shape=jax.ShapeDtypeStruct((n, d), F32),
        compiler_params=_params("arbitrary"),
        name="moe_combine_ln",
    )(x, yg, yg, gates, g.reshape(1, -1), b.reshape(1, -1))


def moe_ln(x_groups, logit_groups, wg, wu, wd, g, b):
    d = x_groups[0].shape[1]
    n_tok = sum(x.shape[0] for x in x_groups)
    n_tok_pad = -(-n_tok // PERMUTE_CHUNK) * PERMUTE_CHUNK
    n_tok_pad = -(-n_tok_pad // ROUTE_TM) * ROUTE_TM
    logits = jnp.concatenate(logit_groups + [jnp.zeros((n_tok_pad - n_tok, 128), F32)], axis=0)
    experts, ranks, gates, counts = moe_route(logits, n_tok)
    tm = MOE_TM
    n_tiles = -(-(n_tok * TOP_K + N_EXPERTS * (tm - 1)) // tm)
    n_tiles = -(-n_tiles * tm // PERMUTE_CHUNK) * PERMUTE_CHUNK // tm
    padded = (counts + tm - 1) // tm * tm
    pad_end = jnp.cumsum(padded)
    pad_start = pad_end - padded
    valid = (jnp.arange(n_tok_pad) < n_tok)[:, None]
    dest = jnp.where(valid, pad_start[experts] + ranks, 0)
    tok = jnp.broadcast_to(jnp.arange(n_tok_pad, dtype=jnp.int32)[:, None], dest.shape)
    row_tok = jnp.zeros((n_tiles * tm,), jnp.int32).at[jnp.where(valid, dest, n_tiles * tm).reshape(-1)].set(
        tok.reshape(-1), mode='drop')
    tile_start = jnp.arange(n_tiles, dtype=jnp.int32) * tm
    tile_e = jnp.minimum(jnp.searchsorted(pad_end, tile_start, side='right'), N_EXPERTS - 1).astype(jnp.int32)
    tile_used = (tile_start < pad_end[-1]).astype(jnp.int32)
    tile_first = jnp.concatenate([jnp.ones((1,), jnp.int32), (tile_e[1:] != tile_e[:-1]).astype(jnp.int32)])
    x_all = jnp.concatenate(x_groups, axis=0)
    xs = gather_rows(x_all, row_tok)
    ys = moe_experts(xs, tile_e, tile_first, tile_used, wg, wu, wd)
    yg = gather_rows(ys, jnp.concatenate([dest[:, 0], dest[:, 1]]).astype(jnp.int32))
    outs, row0 = [], 0
    for x in x_groups:
        outs.append(moe_combine_ln(x, yg, gates, row0, n_tok_pad, g, b))
        row0 += x.shape[0]
    return outs


def _ffn_kernel(x_ref, wg_ref, wu_ref, wd_ref, g_ref, b_ref, o_ref, xb_ref, *, d_ff):
    j = pl.program_id(1)
    tf = wg_ref.shape[1]

    @pl.when(j == 0)
    def _():
        xb_ref[...] = x_ref[...].astype(BF16)
        o_ref[...] = jnp.zeros_like(o_ref)

    xb = xb_ref[...]
    gate = jnp.dot(xb, wg_ref[...], preferred_element_type=F32)
    up = jnp.dot(xb, wu_ref[...], preferred_element_type=F32)
    col = j * tf + lax.broadcasted_iota(jnp.int32, (1, tf), 1)
    a = jnp.where(col < d_ff, jax.nn.silu(gate) * up, 0.0).astype(BF16)
    rowi = j * tf + lax.broadcasted_iota(jnp.int32, (tf, 1), 0)
    wd = jnp.where(rowi < d_ff, wd_ref[...], jnp.zeros((), BF16))
    o_ref[...] += jnp.dot(a, wd, preferred_element_type=F32)

    @pl.when(j == pl.num_programs(1) - 1)
    def _():
        o_ref[...] = _layer_norm_rows(ALPHA * x_ref[...] + o_ref[...], g_ref[...], b_ref[...])


def ffn_ln(x, wg, wu, wd, g, b):
    n, d = x.shape
    d_ff = wg.shape[1]
    tm = _pick_tile(n, (512, 256, 128, 64, 32, 16, 8))
    tf = 512
    kern = functools.partial(_ffn_kernel, d_ff=d_ff)
    return pl.pallas_call(
        kern,
        grid=(n // tm, pl.cdiv(d_ff, tf)),
        in_specs=[pl.BlockSpec((tm, d), lambda i, j: (i, 0)),
                  pl.BlockSpec((d, tf), lambda i, j: (0, j)), pl.BlockSpec((d, tf), lambda i, j: (0, j)),
                  pl.BlockSpec((tf, d), lambda i, j: (j, 0)),
                  pl.BlockSpec((1, d), lambda i, j: (0, 0)), pl.BlockSpec((1, d), lambda i, j: (0, 0))],
        out_specs=pl.BlockSpec((tm, d), lambda i, j: (i, 0)),
        out_shape=jax.ShapeDtypeStruct((n, d), F32),
        scratch_shapes=[pltpu.VMEM((tm, d), BF16)],
        compiler_params=_params("arbitrary", "arbitrary"),
        name="ffn_ln",
    )(x, wg, wu, wd, g.reshape(1, -1), b.reshape(1, -1))


def prompt_mixer(x2, w_in_packed, w_out_bf16, ln_g, ln_b, gla_w2, gla_b, gla_norm_g, cmp_pos, cmp_w1, cmp_w2,
                 pool_w, pool_scale, n_batch, t_len, router=None):
    p = matmul(x2, w_in_packed)
    pos = jnp.arange(t_len, dtype=jnp.int32)
    rows, win, qn, qr, ks, vs, kw, vw, gates = nsa_prep(p, rope_tables(pos), n_batch, t_len)
    cmp_kv = nsa_compress_prompt(rows, cmp_pos, cmp_w1, cmp_w2, n_batch, t_len)
    o_cmp, selb = nsa_select(qn, cmp_kv, t_len // CMP_STRIDE - 1, t_len)
    y_nsa = nsa_attend(qr, o_cmp, selb, gates, ks, vs, kw, vw)
    s0 = jnp.zeros((n_batch, GLA_HEADS, GLA_DK, GLA_DV), F32)
    y_gla, s_gla = gla_mix(p, s0, gla_w2, gla_b, gla_norm_g, n_batch, t_len)
    prev = jnp.zeros((n_batch, POOL_MAX - 1, POOL_WIDTH), F32)
    y_pool = pool_mix(p, prev, 0, pool_w, pool_scale, n_batch, t_len)
    x1 = outproj_ln(x2, y_gla, y_nsa, y_pool, w_out_bf16, ln_g, ln_b, router)
    nsa_rows = rows.reshape(n_batch, t_len, 4, NSA_KV_HEADS, HEAD_DIM)
    n_win = min(WINDOW, t_len)
    win_rows = win.reshape(n_batch, t_len, 2, NSA_KV_HEADS, HEAD_DIM)[:, t_len - n_win:]
    pool_rows = p.reshape(n_batch, t_len, PACKED_WIDTH)[:, t_len - (POOL_MAX - 1):, COL_POOL:COL_POOL + POOL_WIDTH]
    return x1, nsa_rows, win_rows, s_gla, pool_rows


def split_proj(p):
    out = {}
    off = 0
    for name, size in PROJ_SIZES:
        out[name] = p[..., off:off + size]
        off += size
    return out


def layer_norm(x, g, b):
    xf = x.astype(jnp.float32)
    xc = xf - jnp.mean(xf, -1, keepdims=True)
    var = jnp.mean(xc * xc, -1, keepdims=True)
    return (xc * lax.rsqrt(var + LN_EPS) * g + b).astype(x.dtype)


def rope(x, pos):
    half = ROPE_DIM // 2
    inv_freq = ROPE_THETA ** (-jnp.arange(half, dtype=jnp.float32) / half)
    ang = pos.astype(jnp.float32)[:, None] * inv_freq[None, :]
    cos = jnp.cos(ang)[:, None, :]
    sin = jnp.sin(ang)[:, None, :]
    xf = x.astype(jnp.float32)
    x1, x2 = xf[..., :half], xf[..., half:ROPE_DIM]
    out = jnp.concatenate([x1 * cos - x2 * sin, x2 * cos + x1 * sin, xf[..., ROPE_DIM:]], axis=-1)
    return out.astype(x.dtype)


def masked_softmax(s, mask):
    s = jnp.where(mask, s.astype(jnp.float32), -jnp.inf)
    m = jnp.max(s, axis=-1, keepdims=True)
    m = jnp.where(jnp.isfinite(m), m, 0.0)
    p = jnp.where(mask, jnp.exp(s - m), 0.0)
    return p / jnp.maximum(jnp.sum(p, -1, keepdims=True), 1e-30)


def gla_recurrence(q, k, v, g, s0):
    B, T, H, _ = q.shape
    C = GLA_CHUNK
    n_chunks = -(-T // C)
    pad = n_chunks * C - T

    def prep(a):
        a = jnp.pad(a, ((0, 0), (0, pad), (0, 0), (0, 0)))
        return a.reshape(B, n_chunks, C, H, a.shape[-1]).transpose(1, 0, 3, 2, 4)

    causal = jnp.tril(jnp.ones((C, C), dtype=bool))

    def step(S, inp):
        qi, ki, vi, gi = [a.astype(jnp.float32) for a in inp]
        b = jnp.cumsum(gi, axis=2)
        o_inter = jnp.einsum('bhtk,bhkv->bhtv', qi * jnp.exp(b), S)
        diff = jnp.where(causal[:, :, None], b[:, :, :, None, :] - b[:, :, None, :, :], -jnp.inf)
        attn = jnp.einsum('bhtk,bhsk,bhtsk->bhts', qi, ki, jnp.exp(diff))
        o = o_inter + jnp.einsum('bhts,bhsv->bhtv', attn, vi)
        b_last = b[:, :, -1:, :]
        S = jnp.exp(b_last[:, :, 0, :])[..., None] * S + jnp.einsum('bhsk,bhsv->bhkv', ki * jnp.exp(b_last - b), vi)
        return S, o

    S, o = lax.scan(step, s0.astype(jnp.float32), (prep(q), prep(k), prep(v), prep(g)))
    o = o.transpose(1, 0, 3, 2, 4).reshape(B, n_chunks * C, H, v.shape[-1])[:, :T]
    return o, S.astype(s0.dtype)


def gla_mixer(parts, s0, w2, b2, norm_g):
    B, T = parts['gla_q'].shape[:2]
    q = parts['gla_q'].reshape(B, T, GLA_HEADS, GLA_DK) * (GLA_DK ** -0.5)
    k = parts['gla_k'].reshape(B, T, GLA_HEADS, GLA_DK)
    v = parts['gla_v'].reshape(B, T, GLA_HEADS, GLA_DV)
    g = jax.nn.log_sigmoid((parts['gla_glr'] @ w2 + b2).astype(jnp.float32)) / GLA_GATE_NORM
    g = g.reshape(B, T, GLA_HEADS, GLA_DK)
    o, s_new = gla_recurrence(q, k, v, g, s0)
    o = o * lax.rsqrt(jnp.mean(o * o, -1, keepdims=True) + RMS_EPS)
    out = o.reshape(B, T, GLA_WIDTH) * norm_g * jax.nn.silu(parts['gla_r'].astype(jnp.float32))
    return out.astype(parts['gla_v'].dtype), s_new


def nsa_compress(kx, pos_emb, w1, w2):
    B, T, G, D = kx.shape
    nh = T // CMP_STRIDE
    halves = kx[:, :nh * CMP_STRIDE].reshape(B, nh, CMP_STRIDE, G, D).astype(jnp.float32)
    pe = pos_emb.reshape(2, CMP_STRIDE, D)
    w = w1.reshape(2, CMP_STRIDE, D, CMP_HIDDEN)
    h_lo = jnp.einsum('bnjgd,jdh->bngh', halves + pe[0][None, None, :, None, :], w[0])
    h_hi = jnp.einsum('bnjgd,jdh->bngh', halves + pe[1][None, None, :, None, :], w[1])
    h = jax.nn.gelu(h_lo[:, :-1] + h_hi[:, 1:])
    return jnp.einsum('bngh,hd->bngd', h, w2)


def nsa_global(qn, qr, rows, q_pos, cmp_pos, cmp_w1, cmp_w2):
    B, Tq, G, R, D = qn.shape
    Tk = rows.shape[1]
    scale = HEAD_DIM ** -0.5
    kcmp = nsa_compress(rows[:, :, 0], cmp_pos[0], cmp_w1[0], cmp_w2[0])
    vcmp = nsa_compress(rows[:, :, 1], cmp_pos[1], cmp_w1[1], cmp_w2[1])
    n_cmp = kcmp.shape[1]
    cmp_end = jnp.arange(n_cmp) * CMP_STRIDE + CMP_LEN - 1
    n_sel = -(-Tk // SEL_BLOCK)
    pad = n_sel * SEL_BLOCK - Tk

    def to_blocks(a):
        a = jnp.pad(a, ((0, 0), (0, pad), (0, 0), (0, 0)))
        return a.reshape(B, n_sel, SEL_BLOCK, G, D).transpose(0, 3, 1, 2, 4)

    ksb = to_blocks(rows[:, :, 2])
    vsb = to_blocks(rows[:, :, 3])
    ci = jnp.arange(n_cmp)[:, None]
    sj = jnp.arange(n_sel)[None, :]
    overlap = ((ci * CMP_STRIDE <= sj * SEL_BLOCK + SEL_BLOCK - 1) &
               (ci * CMP_STRIDE + CMP_LEN - 1 >= sj * SEL_BLOCK)).astype(jnp.float32)
    n_top = min(SEL_TOPN, n_sel)
    gather = jax.vmap(jax.vmap(lambda blocks, idx: blocks[idx]))
    blk_ids = jnp.arange(n_sel)

    def block_fn(args):
        qnb, qrb, qp = args
        qb = qp.shape[0]
        s = jnp.einsum('bqgrd,bngd->bgrqn', qnb, kcmp) * scale
        p_c = masked_softmax(s, cmp_end[None, :] <= qp[:, None])
        o_c = jnp.einsum('bgrqn,bngd->bqgrd', p_c, vcmp)
        imp = jnp.einsum('bgqn,nj->bgqj', jnp.sum(p_c, axis=2), overlap)
        cur = qp[:, None] // SEL_BLOCK
        valid = blk_ids[None, :] <= cur
        forced = (blk_ids[None, :] == 0) | (blk_ids[None, :] == cur) | (blk_ids[None, :] == cur - 1)
        imp = jnp.where(valid, jnp.where(forced, jnp.inf, imp), -jnp.inf)
        top_s, top_i = lax.top_k(imp, n_top)
        kg = gather(ksb, top_i)
        vg = gather(vsb, top_i)
        kpos = top_i[..., None] * SEL_BLOCK + jnp.arange(SEL_BLOCK)
        mask = (top_s > -jnp.inf)[..., None] & (kpos <= qp[None, None, :, None, None])
        s2 = jnp.einsum('bqgrd,bgqnjd->bgrqnj', qrb, kg) * scale
        s2 = s2.reshape(B, G, R, qb, n_top * SEL_BLOCK)
        p_s = masked_softmax(s2, mask.reshape(B, G, 1, qb, n_top * SEL_BLOCK))
        p_s = p_s.reshape(B, G, R, qb, n_top, SEL_BLOCK)
        o_s = jnp.einsum('bgrqnj,bgqnjd->bqgrd', p_s, vg.astype(jnp.float32))
        return o_c, o_s

    qb = min(Tq, Q_BLOCK)
    nq = -(-Tq // qb)
    padq = nq * qb - Tq
    qpad = ((0, 0), (0, padq), (0, 0), (0, 0), (0, 0))
    qn_b = jnp.pad(qn, qpad).reshape(B, nq, qb, G, R, D).swapaxes(0, 1)
    qr_b = jnp.pad(qr, qpad).reshape(B, nq, qb, G, R, D).swapaxes(0, 1)
    qp_b = jnp.pad(q_pos, (0, padq), mode='edge').reshape(nq, qb)
    o_c, o_s = lax.map(block_fn, (qn_b, qr_b, qp_b))
    o_c = o_c.swapaxes(0, 1).reshape(B, nq * qb, G, R, D)[:, :Tq]
    o_s = o_s.swapaxes(0, 1).reshape(B, nq * qb, G, R, D)[:, :Tq]
    return o_c, o_s


def band_attend(q, k, v, qpos, kpos):
    s = jnp.einsum('bqgrd,bkgd->bgrqk', q, k) * (HEAD_DIM ** -0.5)
    mask = ((kpos[None, :] <= qpos[:, None]) & (kpos[None, :] > qpos[:, None] - WINDOW) & (kpos[None, :] >= 0))
    p = masked_softmax(s, mask)
    return jnp.einsum('bgrqk,bkgd->bqgrd', p, v.astype(jnp.float32))


def sliding_prompt(q, k, v):
    B, T, G, R, D = q.shape
    kp = jnp.pad(k, ((0, 0), (WINDOW, 0), (0, 0), (0, 0)))
    vp = jnp.pad(v, ((0, 0), (WINDOW, 0), (0, 0), (0, 0)))
    nq = T // Q_BLOCK

    def fn(i):
        q0 = i * Q_BLOCK
        qi = lax.dynamic_slice_in_dim(q, q0, Q_BLOCK, axis=1)
        ki = lax.dynamic_slice_in_dim(kp, q0, WINDOW + Q_BLOCK, axis=1)
        vi = lax.dynamic_slice_in_dim(vp, q0, WINDOW + Q_BLOCK, axis=1)
        qpos = q0 + jnp.arange(Q_BLOCK)
        kpos = q0 - WINDOW + jnp.arange(WINDOW + Q_BLOCK)
        return band_attend(qi, ki, vi, qpos, kpos)

    o = lax.map(fn, jnp.arange(nq))
    return o.swapaxes(0, 1).reshape(B, T, G, R, D)


def nsa_mixer(parts, pos, nsa_past, win_past, cmp_pos, cmp_w1, cmp_w2):
    B, T = parts['nsa_q'].shape[:2]
    dt = parts['nsa_q'].dtype
    q = parts['nsa_q'].reshape(B, T, NSA_HEADS, HEAD_DIM)
    q_rope = rope(q, pos)
    kv = lambda name: parts[name].reshape(B, T, NSA_KV_HEADS, HEAD_DIM)
    k_win, v_win = rope(kv('win_k'), pos), kv('win_v')
    new_rows = jnp.stack([kv('cmp_k'), kv('cmp_v'), rope(kv('slc_k'), pos), kv('slc_v')], axis=2)
    rows = new_rows if nsa_past is None else jnp.concatenate([nsa_past.astype(dt), new_rows], axis=1)
    qg = q.reshape(B, T, NSA_KV_HEADS, NSA_REP, HEAD_DIM)
    qrg = q_rope.reshape(B, T, NSA_KV_HEADS, NSA_REP, HEAD_DIM)
    o_cmp, o_slc = nsa_global(qg, qrg, rows, pos, cmp_pos, cmp_w1, cmp_w2)
    win_rows = jnp.stack([k_win, v_win], axis=2)
    if win_past is None:
        o_win = sliding_prompt(qrg, k_win, v_win)
        new_win = win_rows[:, T - min(WINDOW, T):]
    else:
        n_buf = win_past.shape[1]
        ext = jnp.concatenate([win_past.astype(dt), win_rows], axis=1)
        kpos = pos[0] - n_buf + jnp.arange(n_buf + T)
        o_win = band_attend(qrg, ext[:, :, 0], ext[:, :, 1], pos, kpos)
        new_win = ext[:, T:]
    gates = jax.nn.sigmoid(parts['nsa_gate'].astype(jnp.float32)).reshape(B, T, NSA_KV_HEADS, NSA_REP, 3)
    o = gates[..., 0:1] * o_cmp + gates[..., 1:2] * o_slc + gates[..., 2:3] * o_win
    return o.reshape(B, T, NSA_WIDTH).astype(dt), new_rows, new_win


def pool_mixer(u, prev, pos, w_pool, scale):
    B, T, C = u.shape
    P = POOL_MAX - 1
    ext = jnp.concatenate([prev.astype(u.dtype), u], axis=1).astype(jnp.float32)
    cs = jnp.concatenate([jnp.zeros((B, 1, C), jnp.float32), jnp.cumsum(ext, axis=1)], axis=1)
    end = cs[:, P + 1:]
    means = []
    for gi, w in enumerate(POOL_WINDOWS):
        sl = slice(gi * POOL_GROUP_DIM, (gi + 1) * POOL_GROUP_DIM)
        start = cs[:, P + 1 - w:P + 1 - w + T, sl]
        cnt = jnp.minimum(pos + 1, w).astype(jnp.float32)[None, :, None]
        means.append((end[..., sl] - start) / cnt)
    pooled = (jnp.concatenate(means, axis=-1) - ext[:, P:]).reshape(B, T, POOL_GROUPS, POOL_GROUP_DIM)
    y = jnp.einsum('btgc,gcd->btgd', pooled, w_pool.astype(jnp.float32)).reshape(B, T, C) * scale
    return y.astype(u.dtype), ext[:, -P:].astype(u.dtype)


def mixer_layer(x, pos0, gla_s0, pool_prev, nsa_past, win_past,
                w_in, gla_w2, gla_b, gla_norm_g, cmp_pos, cmp_w1, cmp_w2, pool_w, pool_scale, w_out):
    B, T, _ = x.shape
    pos = pos0 + jnp.arange(T, dtype=jnp.int32)
    parts = split_proj(mm3(x, w_in))
    y_gla, s_gla = gla_mixer(parts, gla_s0, gla_w2, gla_b, gla_norm_g)
    y_nsa, nsa_rows, win_rows = nsa_mixer(parts, pos, nsa_past, win_past, cmp_pos, cmp_w1, cmp_w2)
    y_pool, pool_rows = pool_mixer(parts['pool'], pool_prev, pos, pool_w, pool_scale)
    y = mm3(jnp.concatenate([y_gla, y_nsa, y_pool], axis=-1), w_out)
    return y, nsa_rows, win_rows, s_gla, pool_rows


def swiglu(x, wg, wu, wd):
    return mm3(jax.nn.silu(mm3(x, wg)) * mm3(x, wu), wd)


def moe_ffn(x, router, wg, wu, wd):
    B, T, D = x.shape
    n_tok = B * T
    xt = x.reshape(n_tok, D)
    logits = (xt @ router).astype(jnp.float32)
    top_v, top_i = lax.top_k(logits, TOP_K)
    gates = jax.nn.softmax(top_v, axis=-1)
    n_asg = n_tok * TOP_K
    e_flat = top_i.reshape(n_asg)
    tok_flat = jnp.arange(n_asg, dtype=jnp.int32) // TOP_K
    gate_flat = gates.reshape(n_asg)
    blk = MOE_ROW_BLOCK if n_asg >= N_EXPERTS * MOE_ROW_BLOCK else MOE_MIN_BLOCK
    n_blk = -(-(n_asg + N_EXPERTS * (blk - 1)) // blk)
    order = jnp.argsort(e_flat)
    e_sorted = e_flat[order]
    counts = jnp.bincount(e_flat, length=N_EXPERTS)
    padded = (counts + blk - 1) // blk * blk
    pad_end = jnp.cumsum(padded)
    pad_start = pad_end - padded
    start = jnp.cumsum(counts) - counts
    dest = pad_start[e_sorted] + jnp.arange(n_asg) - start[e_sorted]
    row_tok = jnp.zeros((n_blk * blk,), jnp.int32).at[dest].set(tok_flat[order])
    row_gate = jnp.zeros((n_blk * blk,), jnp.float32).at[dest].set(gate_flat[order])
    blk_expert = jnp.minimum(jnp.searchsorted(pad_end, jnp.arange(n_blk) * blk, side='right'), N_EXPERTS - 1)

    def expert_block(args):
        rows, e = args
        xb = xt[rows]
        return (jax.nn.silu(xb @ wg[e]) * (xb @ wu[e])) @ wd[e]

    out = lax.map(expert_block, (row_tok.reshape(n_blk, blk), blk_expert))
    y = jnp.zeros((n_tok, D), jnp.float32).at[row_tok].add(
        out.reshape(n_blk * blk, D).astype(jnp.float32) * row_gate[:, None])
    return y.reshape(B, T, D).astype(x.dtype)


def kernel(x_prompt, x_sample, cache_nsa, page_table, state_win, state_gla, state_pool, w_in, gla_gate_w2, gla_gate_b, gla_norm_g, nsa_cmp_pos, nsa_cmp_w1, nsa_cmp_w2, pool_w, pool_scale, w_out, ln1_g, ln1_b, ln2_g, ln2_b, ffn_w_gate, ffn_w_up, ffn_w_down, moe_router, moe_w_gate, moe_w_up, moe_w_down):
    n_prompt, t_len, d = x_prompt.shape
    n_dec = x_sample.shape[0]
    xp, xs = x_prompt.reshape(n_prompt * t_len, d), x_sample
    nsa_p, nsa_s, win_p, win_s, gla_p, gla_s, pool_p, pool_s = [], [], [], [], [], [], [], []
    for l in range(DEPTH):
        lw = (w_in[l], gla_gate_w2[l], gla_gate_b[l], gla_norm_g[l], nsa_cmp_pos[l], nsa_cmp_w1[l],
              nsa_cmp_w2[l], pool_w[l], pool_scale[l], w_out[l])
        i = l // 2
        router = moe_router[i] if l % 2 else None
        xp, r_p, w_p, g_p, p_p = prompt_mixer(
            xp, pack_w_in(w_in[l]), w_out[l].astype(BF16), ln1_g[l], ln1_b[l], gla_gate_w2[l], gla_gate_b[l],
            gla_norm_g[l], nsa_cmp_pos[l], nsa_cmp_w1[l], nsa_cmp_w2[l], pool_w[l], pool_scale[l], n_prompt, t_len,
            router)
        past_rows = cache_nsa[l, page_table].reshape(n_dec, -1, 4, NSA_KV_HEADS, HEAD_DIM)
        hs, r_s, w_s, g_s, p_s = mixer_layer(xs, PAST_LEN, state_gla[l], state_pool[l], past_rows, state_win[l], *lw)
        xs = layer_norm(ALPHA * xs + hs, ln1_g[l], ln1_b[l])
        if l % 2 == 0:
            xp = ffn_ln(xp, ffn_w_gate[i].astype(BF16), ffn_w_up[i].astype(BF16), ffn_w_down[i].astype(BF16),
                        ln2_g[l], ln2_b[l])
            fs = swiglu(xs, ffn_w_gate[i], ffn_w_up[i], ffn_w_down[i])
            xs = layer_norm(ALPHA * xs + fs, ln2_g[l], ln2_b[l])
        else:
            xp, lg_p = xp
            xs2 = xs.reshape(n_dec, d)
            lg_s = jnp.dot(xs2, router_pad(router), precision=lax.Precision.HIGHEST)
            xp, xs2 = moe_ln([xp, xs2], [lg_p, lg_s], moe_w_gate[i], moe_w_up[i], moe_w_down[i], ln2_g[l], ln2_b[l])
            xs = xs2.reshape(xs.shape)
        nsa_p.append(r_p); nsa_s.append(r_s); win_p.append(w_p); win_s.append(w_s)
        gla_p.append(g_p); gla_s.append(g_s); pool_p.append(p_p); pool_s.append(p_s)
    return (xp.reshape(n_prompt, t_len, d), xs, jnp.stack(nsa_p), jnp.stack(nsa_s), jnp.stack(win_p),
            jnp.stack(win_s), jnp.stack(gla_p), jnp.stack(gla_s), jnp.stack(pool_p), jnp.stack(pool_s))
```

```python
import functools

import jax
import jax.numpy as jnp
from jax import lax
from jax.experimental import pallas as pl
from jax.experimental.pallas import tpu as pltpu

D_MODEL = 2048
DEPTH = 2
PAST_LEN = 16384
HEAD_DIM = 128
GLA_HEADS = 4
GLA_DK = 64
GLA_DV = 128
GLA_RANK = 16
GLA_GATE_NORM = 16.0
GLA_CHUNK = 64
GLA_WIDTH = GLA_HEADS * GLA_DV
NSA_HEADS = 8
NSA_KV_HEADS = 2
NSA_REP = NSA_HEADS // NSA_KV_HEADS
NSA_WIDTH = NSA_HEADS * HEAD_DIM
CMP_LEN = 32
CMP_STRIDE = 16
CMP_HIDDEN = 128
SEL_BLOCK = 64
SEL_TOPN = 16
WINDOW = 512
Q_BLOCK = 128
POOL_GROUPS = 4
POOL_GROUP_DIM = 128
POOL_WIDTH = POOL_GROUPS * POOL_GROUP_DIM
POOL_WINDOWS = (2, 4, 8, 16)
POOL_MAX = 16
ROPE_THETA = 500000.0
ROPE_DIM = HEAD_DIM // 4
N_EXPERTS = 8
TOP_K = 2
MOE_ROW_BLOCK = 128
MOE_MIN_BLOCK = 8
ALPHA = (2 * DEPTH) ** 0.25
LN_EPS = 1e-5
RMS_EPS = 1e-6

PROJ_SIZES = (
    ('gla_q', GLA_HEADS * GLA_DK), ('gla_k', GLA_HEADS * GLA_DK), ('gla_v', GLA_HEADS * GLA_DV),
    ('gla_glr', GLA_RANK), ('gla_r', GLA_HEADS * GLA_DV),
    ('nsa_q', NSA_HEADS * HEAD_DIM),
    ('cmp_k', NSA_KV_HEADS * HEAD_DIM), ('cmp_v', NSA_KV_HEADS * HEAD_DIM),
    ('slc_k', NSA_KV_HEADS * HEAD_DIM), ('slc_v', NSA_KV_HEADS * HEAD_DIM),
    ('win_k', NSA_KV_HEADS * HEAD_DIM), ('win_v', NSA_KV_HEADS * HEAD_DIM),
    ('nsa_gate', 3 * NSA_HEADS),
    ('pool', POOL_WIDTH),
)

GLA_SUB = 16
SEL_PAD = 128

BF16 = jnp.bfloat16
F32 = jnp.float32
NEG_BIG = -1e30
VMEM_LIMIT_BYTES = 56 * 1024 * 1024

COL_NSA_Q = 0
COL_ROWS = 1024
COL_WIN = 2048
COL_POOL = 2560
COL_GLA_V = 3072
COL_GLA_R = 3584
COL_GLA_Q = 4096
COL_GLA_K = 4352
COL_SMALL = 4608
PACKED_WIDTH = 4736
SMALL_GATE_OFF = GLA_RANK


def _params(*sem):
    return pltpu.CompilerParams(dimension_semantics=sem, vmem_limit_bytes=VMEM_LIMIT_BYTES)


def _proj_offsets():
    out, off = {}, 0
    for name, size in PROJ_SIZES:
        out[name] = (off, size)
        off += size
    return out


def pack_w_in(w):
    offs = _proj_offsets()
    sl = lambda n: w[:, offs[n][0]:offs[n][0] + offs[n][1]]
    pad = jnp.zeros((w.shape[0], 128 - GLA_RANK - 3 * NSA_HEADS), w.dtype)
    cols = [sl('nsa_q'), sl('cmp_k'), sl('cmp_v'), sl('slc_k'), sl('slc_v'), sl('win_k'), sl('win_v'),
            sl('pool'), sl('gla_v'), sl('gla_r'), sl('gla_q'), sl('gla_k'), sl('gla_glr'), sl('nsa_gate'), pad]
    return jnp.concatenate(cols, axis=1).astype(BF16)


def rope_tables(pos):
    half = ROPE_DIM // 2
    inv_freq = ROPE_THETA ** (-jnp.arange(half, dtype=F32) / half)
    ang = pos.astype(F32)[:, None] * inv_freq[None, :]
    cos, sin = jnp.cos(ang), jnp.sin(ang)
    t = pos.shape[0]
    c = jnp.concatenate([cos, cos, jnp.ones((t, HEAD_DIM - ROPE_DIM), F32)], axis=1)
    sa = jnp.concatenate([-sin, jnp.zeros((t, HEAD_DIM - half), F32)], axis=1)
    sb = jnp.concatenate([jnp.zeros((t, half), F32), sin, jnp.zeros((t, HEAD_DIM - ROPE_DIM), F32)], axis=1)
    return c, sa, sb


def _pick_tile(n, pref):
    for t in pref:
        if n % t == 0:
            return t
    return n


def _mm_kernel(x_ref, w_ref, o_ref, xb_ref):
    @pl.when(pl.program_id(1) == 0)
    def _():
        xb_ref[...] = x_ref[...].astype(BF16)

    o_ref[...] = jnp.dot(xb_ref[...], w_ref[...].astype(BF16), preferred_element_type=F32)


def matmul(x, w):
    m, k = x.shape
    n = w.shape[1]
    tm = _pick_tile(m, tuple(t for t in (1024, 512, 256, 128, 64, 32, 16, 8) if t * k <= 2048 * 1024))
    tn = 512 if n >= 512 else n
    return pl.pallas_call(
        _mm_kernel,
        grid=(m // tm, pl.cdiv(n, tn)),
        in_specs=[pl.BlockSpec((tm, k), lambda i, j: (i, 0)),
                  pl.BlockSpec((k, tn), lambda i, j: (0, j))],
        out_specs=pl.BlockSpec((tm, tn), lambda i, j: (i, j)),
        out_shape=jax.ShapeDtypeStruct((m, n), F32),
        scratch_shapes=[pltpu.VMEM((tm, k), BF16)],
        compiler_params=_params("arbitrary", "arbitrary"),
        name="matmul",
    )(x, w)


def mm3(x, w):
    lead = x.shape[:-1]
    return matmul(x.reshape(-1, x.shape[-1]), w).reshape(*lead, w.shape[1])


def _rope(x, c, sa, sb):
    return x * c + pltpu.roll(x, HEAD_DIM - ROPE_DIM // 2, 1) * sa + pltpu.roll(x, ROPE_DIM // 2, 1) * sb


def _nsa_prep_kernel(q_ref, rows_ref, win_ref, small_ref, c_ref, sa_ref, sb_ref,
                     rows_o, win_o, qn_o, qr_o, ks_o, vs_o, kw_o, vw_o, gate_o):
    c, sa, sb = c_ref[...], sa_ref[...], sb_ref[...]
    scale = HEAD_DIM ** -0.5
    hd = HEAD_DIM
    for h in range(NSA_HEADS):
        x = q_ref[:, h * hd:(h + 1) * hd]
        qn_o[0, h] = (x * scale).astype(BF16)
        qr_o[0, h] = (_rope(x, c, sa, sb) * scale).astype(BF16)
    ones = jnp.ones((q_ref.shape[0], hd), BF16)
    rows_o[:, 0:4 * hd] = rows_ref[:, 0:4 * hd]
    for g in range(NSA_KV_HEADS):
        k = _rope(rows_ref[:, (4 + g) * hd:(5 + g) * hd], c, sa, sb)
        rows_o[:, (4 + g) * hd:(5 + g) * hd] = k
        ks_o[0, g] = k.astype(BF16)
        v = rows_ref[:, (6 + g) * hd:(7 + g) * hd]
        rows_o[:, (6 + g) * hd:(7 + g) * hd] = v
        vs_o[0, g, :, 0:hd] = v.astype(BF16)
        vs_o[0, g, :, hd:2 * hd] = ones
        k = _rope(win_ref[:, g * hd:(g + 1) * hd], c, sa, sb)
        win_o[:, g * hd:(g + 1) * hd] = k
        kw_o[0, g] = k.astype(BF16)
        v = win_ref[:, (2 + g) * hd:(3 + g) * hd]
        win_o[:, (2 + g) * hd:(3 + g) * hd] = v
        vw_o[0, g, :, 0:hd] = v.astype(BF16)
        vw_o[0, g, :, hd:2 * hd] = ones
    sig = jax.nn.sigmoid(small_ref[...])
    per_g = 3 * NSA_REP
    for g in range(NSA_KV_HEADS):
        gate_o[0, g] = pltpu.roll(sig, 128 - SMALL_GATE_OFF - g * per_g, 1)


def nsa_prep(p, tables, n_batch, t_len):
    tr = _pick_tile(t_len, (512, 256, 128, 64, 32, 16))
    nt = t_len // tr
    n = n_batch * t_len
    hd = HEAD_DIM
    row = lambda w, cb: pl.BlockSpec((tr, w), lambda b, i: (b * nt + i, cb))
    tab = pl.BlockSpec((tr, hd), lambda b, i: (i, 0))
    head = lambda nh, w: pl.BlockSpec((1, nh, tr, w), lambda b, i: (b, 0, i, 0))
    out_shape = (
        jax.ShapeDtypeStruct((n, 8 * hd), F32),
        jax.ShapeDtypeStruct((n, 4 * hd), F32),
        jax.ShapeDtypeStruct((n_batch, NSA_HEADS, t_len, hd), BF16),
        jax.ShapeDtypeStruct((n_batch, NSA_HEADS, t_len, hd), BF16),
        jax.ShapeDtypeStruct((n_batch, NSA_KV_HEADS, t_len, hd), BF16),
        jax.ShapeDtypeStruct((n_batch, NSA_KV_HEADS, t_len, 2 * hd), BF16),
        jax.ShapeDtypeStruct((n_batch, NSA_KV_HEADS, t_len, hd), BF16),
        jax.ShapeDtypeStruct((n_batch, NSA_KV_HEADS, t_len, 2 * hd), BF16),
        jax.ShapeDtypeStruct((n_batch, NSA_KV_HEADS, t_len, 128), F32),
    )
    return pl.pallas_call(
        _nsa_prep_kernel,
        grid=(n_batch, nt),
        in_specs=[row(8 * hd, COL_NSA_Q // (8 * hd)), row(8 * hd, COL_ROWS // (8 * hd)),
                  row(4 * hd, COL_WIN // (4 * hd)), row(128, COL_SMALL // 128), tab, tab, tab],
        out_specs=(row(8 * hd, 0), row(4 * hd, 0), head(NSA_HEADS, hd), head(NSA_HEADS, hd),
                   head(NSA_KV_HEADS, hd), head(NSA_KV_HEADS, 2 * hd), head(NSA_KV_HEADS, hd),
                   head(NSA_KV_HEADS, 2 * hd), head(NSA_KV_HEADS, 128)),
        out_shape=out_shape,
        compiler_params=_params("arbitrary", "arbitrary"),
        name="nsa_prep",
    )(p, p, p, p, *tables)


def _nsa_cmp_kernel(x_ref, pe_ref, w1_ref, w2_ref, o_ref):
    nh = o_ref.shape[0]
    h_lo = jnp.zeros((nh, CMP_HIDDEN), F32)
    h_hi = jnp.zeros((nh, CMP_HIDDEN), F32)
    for j in range(CMP_STRIDE):
        xj = x_ref[pl.ds(j, nh, stride=CMP_STRIDE), :]
        h_lo += jnp.dot((xj + pe_ref[j:j + 1, :]).astype(BF16), w1_ref[j].astype(BF16), preferred_element_type=F32)
        h_hi += jnp.dot((xj + pe_ref[CMP_STRIDE + j:CMP_STRIDE + j + 1, :]).astype(BF16),
                        w1_ref[CMP_STRIDE + j].astype(BF16), preferred_element_type=F32)
    h = jax.nn.gelu(h_lo + pltpu.roll(h_hi, nh - 1, 0))
    o_ref[...] = jnp.dot(h.astype(BF16), w2_ref[...].astype(BF16), preferred_element_type=F32).astype(BF16)


def nsa_compress_prompt(rows, cmp_pos, cmp_w1, cmp_w2, n_batch, t_len):
    nh = t_len // CMP_STRIDE
    hd = HEAD_DIM
    rows3 = rows.reshape(n_batch, t_len, 8 * hd)
    return pl.pallas_call(
        _nsa_cmp_kernel,
        grid=(n_batch, 2, NSA_KV_HEADS),
        in_specs=[pl.BlockSpec((None, t_len, hd), lambda b, kd, g: (b, 0, kd * NSA_KV_HEADS + g)),
                  pl.BlockSpec((None, CMP_LEN, hd), lambda b, kd, g: (kd, 0, 0)),
                  pl.BlockSpec((None, CMP_LEN, hd, CMP_HIDDEN), lambda b, kd, g: (kd, 0, 0, 0)),
                  pl.BlockSpec((None, CMP_HIDDEN, hd), lambda b, kd, g: (kd, 0, 0))],
        out_specs=pl.BlockSpec((None, None, None, nh, hd), lambda b, kd, g: (b, kd, g, 0, 0)),
        out_shape=jax.ShapeDtypeStruct((n_batch, 2, NSA_KV_HEADS, nh, hd), BF16),
        compiler_params=_params("arbitrary", "arbitrary", "arbitrary"),
        name="nsa_compress",
    )(rows3, cmp_pos, cmp_w1, cmp_w2)


def _nsa_select_kernel(qn_ref, kc_ref, vc_ref, ovt_ref, oc_ref, selb_ref, *, n_cmp, n_top):
    rep, tq, hd = qn_ref.shape[1], qn_ref.shape[2], qn_ref.shape[3]
    n_cmp_pad = kc_ref.shape[0]
    n_sel = ovt_ref.shape[0]
    q0 = pl.program_id(2) * tq
    q = qn_ref[0].reshape(rep * tq, hd)
    s = lax.dot_general(q, kc_ref[...], (((1,), (1,)), ((), ())), preferred_element_type=F32)
    row = lax.broadcasted_iota(jnp.int32, (rep * tq, n_cmp_pad), 0)
    col = lax.broadcasted_iota(jnp.int32, (rep * tq, n_cmp_pad), 1)
    qpos = q0 + (row & (tq - 1))
    mask = (col * CMP_STRIDE + (CMP_LEN - 1) <= qpos) & (col < n_cmp)
    s = jnp.where(mask, s, -jnp.inf)
    m = jnp.max(s, axis=-1, keepdims=True)
    m = jnp.where(m > -jnp.inf, m, 0.0)
    p = jnp.where(mask, jnp.exp(s - m), 0.0)
    p = p / jnp.maximum(jnp.sum(p, axis=-1, keepdims=True), 1e-30)
    oc = jnp.dot(p.astype(BF16), vc_ref[...], preferred_element_type=F32)
    oc_ref[0] = oc.reshape(rep, tq, hd).astype(BF16)
    psum = p[0:tq]
    for r in range(1, rep):
        psum = psum + p[r * tq:(r + 1) * tq]
    imp = lax.dot_general(ovt_ref[...], psum, (((1,), (1,)), ((), ())), preferred_element_type=F32,
                          precision=lax.Precision.HIGHEST)
    blk = lax.broadcasted_iota(jnp.int32, (n_sel, tq), 0)
    cur = (q0 + lax.broadcasted_iota(jnp.int32, (n_sel, tq), 1)) // SEL_BLOCK
    forced = (blk == 0) | (blk == cur) | (blk == cur - 1)
    v = jnp.where(blk <= cur, jnp.where(forced, jnp.inf, imp), -jnp.inf)
    rank = jnp.zeros((n_sel, tq), jnp.int32)
    for i in range(n_sel):
        vi = v[i:i + 1, :]
        ahead = (vi > v) | ((vi == v) & (blk > i))
        rank = rank + ahead.astype(jnp.int32)
    selb_t = jnp.where((rank < n_top) & (v > -jnp.inf), 0.0, NEG_BIG)
    pad = jnp.full((SEL_PAD - n_sel, tq), NEG_BIG, F32)
    selb_ref[0, 0] = jnp.concatenate([selb_t, pad], axis=0).T.astype(BF16)


def nsa_select(qn, cmp_kv, n_cmp, t_k):
    n_batch, _, t_len, hd = qn.shape
    n_cmp_pad = cmp_kv.shape[3]
    n_sel = -(-t_k // SEL_BLOCK)
    tq = _pick_tile(t_len, (256, 128, 64, 32, 16))
    ci = jnp.arange(n_cmp_pad)[None, :]
    sj = jnp.arange(n_sel)[:, None]
    overlap_t = ((ci * CMP_STRIDE <= sj * SEL_BLOCK + SEL_BLOCK - 1) &
                 (ci * CMP_STRIDE + CMP_LEN - 1 >= sj * SEL_BLOCK) & (ci < n_cmp)).astype(F32)
    kern = functools.partial(_nsa_select_kernel, n_cmp=n_cmp, n_top=min(SEL_TOPN, n_sel))
    return pl.pallas_call(
        kern,
        grid=(n_batch, NSA_KV_HEADS, t_len // tq),
        in_specs=[pl.BlockSpec((1, NSA_REP, tq, hd), lambda b, g, i: (b, g, i, 0)),
                  pl.BlockSpec((None, None, None, n_cmp_pad, hd), lambda b, g, i: (b, 0, g, 0, 0)),
                  pl.BlockSpec((None, None, None, n_cmp_pad, hd), lambda b, g, i: (b, 1, g, 0, 0)),
                  pl.BlockSpec((n_sel, n_cmp_pad), lambda b, g, i: (0, 0))],
        out_specs=(pl.BlockSpec((1, NSA_REP, tq, hd), lambda b, g, i: (b, g, i, 0)),
                   pl.BlockSpec((1, 1, tq, SEL_PAD), lambda b, g, i: (b, g, i, 0))),
        out_shape=(jax.ShapeDtypeStruct((n_batch, NSA_HEADS, t_len, hd), BF16),
                   jax.ShapeDtypeStruct((n_batch, NSA_KV_HEADS, t_len, SEL_PAD), BF16)),
        compiler_params=_params("arbitrary", "arbitrary", "arbitrary"),
        name="nsa_select",
    )(qn, cmp_kv, cmp_kv, overlap_t)


def _nsa_attn_kernel(qr_ref, oc_ref, selb_ref, gate_ref, ks_ref, vs_ref, kw_ref, vw_ref, e_ref, o_ref,
                     m_scr, acc_scr, *, tk, wk):
    rep, qb, hd = qr_ref.shape[1], qr_ref.shape[2], qr_ref.shape[3]
    nr = rep * qb
    q0 = pl.program_id(2) * qb
    q = qr_ref[0].reshape(nr, hd)
    selb = selb_ref[0, 0]
    nt = (((1,), (1,)), ((), ()))

    def sel_scores(t):
        k = ks_ref[0, 0, pl.ds(pl.multiple_of(t * tk, tk), tk), :]
        s = lax.dot_general(q, k, nt, preferred_element_type=F32)
        bias = jnp.dot(selb, e_ref[t], preferred_element_type=F32)
        return (s.reshape(rep, qb, tk) + bias[None]).reshape(nr, tk)

    def sel_values(t):
        return vs_ref[0, 0, pl.ds(pl.multiple_of(t * tk, tk), tk), :]

    td = q0 // tk
    qpos = q0 + (lax.broadcasted_iota(jnp.int32, (nr, tk), 0) & (qb - 1))
    kpos = td * tk + lax.broadcasted_iota(jnp.int32, (nr, tk), 1)
    s = jnp.where(kpos <= qpos, sel_scores(td), NEG_BIG)
    m = jnp.max(s, axis=-1, keepdims=True)
    m_scr[...] = m
    acc_scr[...] = jnp.dot(jnp.exp(s - m).astype(BF16), sel_values(td), preferred_element_type=F32)

    def body(t, carry):
        s = sel_scores(t)
        m_old = m_scr[...]
        m_new = jnp.maximum(m_old, jnp.max(s, axis=-1, keepdims=True))
        p = jnp.exp(s - m_new).astype(BF16)
        acc_scr[...] = jnp.exp(m_old - m_new) * acc_scr[...] + jnp.dot(p, sel_values(t), preferred_element_type=F32)
        m_scr[...] = m_new
        return carry

    lax.fori_loop(0, td, body, 0)
    acc = acc_scr[...]
    o_sel = acc[:, 0:hd] / jnp.maximum(acc[:, hd:hd + 1], 1e-30)

    kstart = pl.multiple_of(jnp.maximum(q0 - WINDOW, 0), qb)
    kw = kw_ref[0, 0, pl.ds(kstart, wk), :]
    s = lax.dot_general(q, kw, nt, preferred_element_type=F32)
    qpos = q0 + (lax.broadcasted_iota(jnp.int32, (nr, wk), 0) & (qb - 1))
    kpos = kstart + lax.broadcasted_iota(jnp.int32, (nr, wk), 1)
    s = jnp.where((kpos <= qpos) & (kpos > qpos - WINDOW), s, NEG_BIG)
    m = jnp.max(s, axis=-1, keepdims=True)
    accw = jnp.dot(jnp.exp(s - m).astype(BF16), vw_ref[0, 0, pl.ds(kstart, wk), :], preferred_element_type=F32)
    o_win = accw[:, 0:hd] / jnp.maximum(accw[:, hd:hd + 1], 1e-30)

    gates = gate_ref[0, 0]
    for r in range(rep):
        rows = slice(r * qb, (r + 1) * qb)
        o = (gates[:, 3 * r:3 * r + 1] * oc_ref[0, r].astype(F32)
             + gates[:, 3 * r + 1:3 * r + 2] * o_sel[rows]
             + gates[:, 3 * r + 2:3 * r + 3] * o_win[rows])
        o_ref[:, r * hd:(r + 1) * hd] = o.astype(BF16)


def nsa_attend(qr, o_cmp, selb, gates, ks, vs, kw, vw):
    n_batch, _, t_len, hd = qr.shape
    n_sel = selb.shape[3]
    qb = Q_BLOCK
    tk = min(512, t_len)
    wk = min(WINDOW + qb, t_len)
    nq = t_len // qb
    n_tiles = t_len // tk
    key_blk = (jnp.arange(n_tiles)[:, None, None] * tk + jnp.arange(tk)[None, None, :]) // SEL_BLOCK
    e = (key_blk == jnp.arange(n_sel)[None, :, None]).astype(BF16)
    kern = functools.partial(_nsa_attn_kernel, tk=tk, wk=wk)
    per_q = lambda nh, w: pl.BlockSpec((1, nh, qb, w), lambda b, g, i: (b, g, i, 0))
    full = lambda w: pl.BlockSpec((1, 1, t_len, w), lambda b, g, i: (b, g, 0, 0))
    return pl.pallas_call(
        kern,
        grid=(n_batch, NSA_KV_HEADS, nq),
        in_specs=[per_q(NSA_REP, hd), per_q(NSA_REP, hd), per_q(1, n_sel), per_q(1, 128),
                  full(hd), full(2 * hd), full(hd), full(2 * hd),
                  pl.BlockSpec((n_tiles, n_sel, tk), lambda b, g, i: (0, 0, 0))],
        out_specs=pl.BlockSpec((qb, NSA_REP * hd), lambda b, g, i: (b * nq + i, g)),
        out_shape=jax.ShapeDtypeStruct((n_batch * t_len, NSA_HEADS * hd), BF16),
        scratch_shapes=[pltpu.VMEM((NSA_REP * qb, 1), F32), pltpu.VMEM((NSA_REP * qb, 2 * hd), F32)],
        compiler_params=_params("arbitrary", "arbitrary", "arbitrary"),
        name="nsa_attend",
    )(qr, o_cmp, selb, gates, ks, vs, kw, vw, e)


def _gla_kernel(q_ref, k_ref, v_ref, r_ref, small_ref, w2_ref, b2_ref, ng_ref, s0_ref, y_ref, sf_ref, s_scr):
    tb = q_ref.shape[0]
    c, sub, dk, dv = GLA_CHUNK, GLA_SUB, GLA_DK, GLA_DV
    n_sub = c // sub
    t = pl.program_id(1)

    @pl.when(t == 0)
    def _():
        s_scr[...] = s0_ref[0]

    z = jnp.dot(small_ref[:, 0:GLA_RANK].astype(BF16), w2_ref[...].astype(BF16),
                preferred_element_type=F32) + b2_ref[...]
    g_all = (jnp.minimum(z, 0.0) - jnp.log1p(jnp.exp(-jnp.abs(z)))) / GLA_GATE_NORM
    ri = lax.broadcasted_iota(jnp.int32, (c, c), 0)
    ci = lax.broadcasted_iota(jnp.int32, (c, c), 1)
    tril = ri >= ci
    cum = tril.astype(F32)
    rsub = lax.broadcasted_iota(jnp.int32, (c, dk), 0) // sub
    eye = lax.broadcasted_iota(jnp.int32, (dk, dk), 0) == lax.broadcasted_iota(jnp.int32, (dk, dk), 1)
    for cc in range(tb // c):
        rows = slice(cc * c, (cc + 1) * c)
        b_all = jnp.dot(cum, g_all[rows], preferred_element_type=F32, precision=lax.Precision.HIGHEST)
        for h in range(GLA_HEADS):
            b = b_all[:, h * dk:(h + 1) * dk]
            qh = q_ref[rows, h * dk:(h + 1) * dk] * (dk ** -0.5)
            kh = k_ref[rows, h * dk:(h + 1) * dk]
            vh = v_ref[rows, h * dv:(h + 1) * dv]
            a_rows = []
            for i in range(n_sub):
                ref = b[sub * i - 1:sub * i, :] if i else jnp.zeros((1, dk), F32)
                rs = slice(sub * i, sub * (i + 1))
                qi = (qh[rs] * jnp.exp(b[rs] - ref)).astype(BF16)
                ki = jnp.where(rsub <= i, kh * jnp.exp(ref - b), 0.0).astype(BF16)
                a_rows.append(lax.dot_general(qi, ki, (((1,), (1,)), ((), ())), preferred_element_type=F32))
            a = jnp.where(tril, jnp.concatenate(a_rows, axis=0), 0.0)
            s_old = s_scr[h]
            o = jnp.dot(a.astype(BF16), vh.astype(BF16), preferred_element_type=F32)
            o += jnp.dot((qh * jnp.exp(b)).astype(BF16), s_old.astype(BF16), preferred_element_type=F32)
            b_last = b[c - 1:c, :]
            ke = (kh * jnp.exp(b_last - b)).astype(BF16)
            upd = lax.dot_general(ke, vh.astype(BF16), (((0,), (0,)), ((), ())), preferred_element_type=F32)
            decay = jnp.exp(jnp.sum(jnp.where(eye, jnp.broadcast_to(b_last, (dk, dk)), 0.0), axis=1, keepdims=True))
            s_scr[h] = decay * s_old + upd
            o = o * lax.rsqrt(jnp.mean(o * o, axis=-1, keepdims=True) + RMS_EPS)
            y = o * ng_ref[:, h * dv:(h + 1) * dv] * jax.nn.silu(r_ref[rows, h * dv:(h + 1) * dv])
            y_ref[rows, h * dv:(h + 1) * dv] = y.astype(BF16)

    @pl.when(t == pl.num_programs(1) - 1)
    def _():
        sf_ref[0] = s_scr[...]


def gla_mix(p, s0, w2, b2, norm_g, n_batch, t_len):
    tb = _pick_tile(t_len, (256, 128, 64))
    nt = t_len // tb
    row = lambda w, off: pl.BlockSpec((tb, w), lambda b, i: (b * nt + i, off // w))
    const = lambda shape: pl.BlockSpec(shape, lambda b, i: (0,) * len(shape))
    return pl.pallas_call(
        _gla_kernel,
        grid=(n_batch, nt),
        in_specs=[row(256, COL_GLA_Q), row(256, COL_GLA_K), row(512, COL_GLA_V), row(512, COL_GLA_R),
                  row(128, COL_SMALL), const((GLA_RANK, GLA_HEADS * GLA_DK)), const((1, GLA_HEADS * GLA_DK)),
                  const((1, GLA_WIDTH)),
                  pl.BlockSpec((1, GLA_HEADS, GLA_DK, GLA_DV), lambda b, i: (b, 0, 0, 0))],
        out_specs=(pl.BlockSpec((tb, GLA_WIDTH), lambda b, i: (b * nt + i, 0)),
                   pl.BlockSpec((1, GLA_HEADS, GLA_DK, GLA_DV), lambda b, i: (b, 0, 0, 0))),
        out_shape=(jax.ShapeDtypeStruct((n_batch * t_len, GLA_WIDTH), BF16),
                   jax.ShapeDtypeStruct((n_batch, GLA_HEADS, GLA_DK, GLA_DV), F32)),
        scratch_shapes=[pltpu.VMEM((GLA_HEADS, GLA_DK, GLA_DV), F32)],
        compiler_params=_params("arbitrary", "arbitrary"),
        name="gla_mix",
    )(p, p, p, p, p, w2, b2.reshape(1, -1), norm_g.reshape(1, -1), s0)


def _pool_kernel(u_ref, prev_ref, cnt_ref, w_ref, sc_ref, y_ref, halo):
    tb = u_ref.shape[0]
    gd = POOL_GROUP_DIM

    @pl.when(pl.program_id(1) == 0)
    def _():
        halo[...] = prev_ref[0]

    ext = jnp.concatenate([halo[...], u_ref[...]], axis=0)
    halo[...] = ext[tb:tb + POOL_MAX]
    for gi, w in enumerate(POOL_WINDOWS):
        x = ext[:, gi * gd:(gi + 1) * gd]
        s = x
        shift = 1
        while shift < w:
            s = s + pltpu.roll(s, shift, 0)
            shift *= 2
        pooled = s[POOL_MAX:] / cnt_ref[:, gi:gi + 1] - x[POOL_MAX:]
        y = jnp.dot(pooled.astype(BF16), w_ref[gi].astype(BF16), preferred_element_type=F32)
        y_ref[:, gi * gd:(gi + 1) * gd] = (y * sc_ref[:, gi * gd:(gi + 1) * gd]).astype(BF16)


def pool_mix(p, prev, pos0, w_pool, scale, n_batch, t_len):
    tb = _pick_tile(t_len, (512, 256, 128, 64, 32, 16))
    nt = t_len // tb
    pos = pos0 + jnp.arange(t_len, dtype=jnp.int32)
    cnt = jnp.stack([jnp.minimum(pos + 1, w).astype(F32) for w in POOL_WINDOWS], axis=1)
    cnt = jnp.pad(cnt, ((0, 0), (0, 128 - POOL_GROUPS)), constant_values=1.0)
    prev16 = jnp.pad(prev.astype(F32), ((0, 0), (1, 0), (0, 0)))
    return pl.pallas_call(
        _pool_kernel,
        grid=(n_batch, nt),
        in_specs=[pl.BlockSpec((tb, POOL_WIDTH), lambda b, i: (b * nt + i, COL_POOL // POOL_WIDTH)),
                  pl.BlockSpec((1, POOL_MAX, POOL_WIDTH), lambda b, i: (b, 0, 0)),
                  pl.BlockSpec((tb, 128), lambda b, i: (i, 0)),
                  pl.BlockSpec((POOL_GROUPS, POOL_GROUP_DIM, POOL_GROUP_DIM), lambda b, i: (0, 0, 0)),
                  pl.BlockSpec((1, POOL_WIDTH), lambda b, i: (0, 0))],
        out_specs=pl.BlockSpec((tb, POOL_WIDTH), lambda b, i: (b * nt + i, 0)),
        out_shape=jax.ShapeDtypeStruct((n_batch * t_len, POOL_WIDTH), BF16),
        scratch_shapes=[pltpu.VMEM((POOL_MAX, POOL_WIDTH), F32)],
        compiler_params=_params("arbitrary", "arbitrary"),
        name="pool_mix",
    )(p, prev16, cnt, w_pool, scale.reshape(1, -1))


def _layer_norm_rows(x, g, b):
    xc = x - jnp.mean(x, axis=-1, keepdims=True)
    var = jnp.mean(xc * xc, axis=-1, keepdims=True)
    return xc * lax.rsqrt(var + LN_EPS) * g + b


def _outproj_kernel(x_ref, yg_ref, yn_ref, yp_ref, w_ref, g_ref, b_ref, *rest):
    h = jnp.dot(yg_ref[...], w_ref[0:GLA_WIDTH, :], preferred_element_type=F32)
    h += jnp.dot(yn_ref[...], w_ref[GLA_WIDTH:GLA_WIDTH + NSA_WIDTH, :], preferred_element_type=F32)
    h += jnp.dot(yp_ref[...], w_ref[GLA_WIDTH + NSA_WIDTH:, :], preferred_element_type=F32)
    x1 = _layer_norm_rows(ALPHA * x_ref[...] + h, g_ref[...], b_ref[...])
    if len(rest) == 1:
        rest[0][...] = x1
    else:
        rt_ref, o_ref, lg_ref = rest
        o_ref[...] = x1
        lg_ref[...] = jnp.dot(x1, rt_ref[...], preferred_element_type=F32, precision=lax.Precision.HIGHEST)


def router_pad(router):
    return jnp.pad(router, ((0, 0), (0, 128 - router.shape[1])))


def outproj_ln(x, y_gla, y_nsa, y_pool, w_out_bf16, g, b, router=None):
    n, d = x.shape
    tm = _pick_tile(n, (512, 256, 128, 64, 32, 16, 8))
    row = lambda w: pl.BlockSpec((tm, w), lambda i: (i, 0))
    const = lambda r, c: pl.BlockSpec((r, c), lambda i: (0, 0))
    in_specs = [row(d), row(GLA_WIDTH), row(NSA_WIDTH), row(POOL_WIDTH), const(d, d), const(1, d), const(1, d)]
    args = [x, y_gla, y_nsa, y_pool, w_out_bf16, g.reshape(1, -1), b.reshape(1, -1)]
    out_specs, out_shape = row(d), jax.ShapeDtypeStruct((n, d), F32)
    if router is not None:
        in_specs.append(const(d, 128))
        args.append(router_pad(router))
        out_specs, out_shape = (out_specs, row(128)), (out_shape, jax.ShapeDtypeStruct((n, 128), F32))
    return pl.pallas_call(
        _outproj_kernel,
        grid=(n // tm,),
        in_specs=in_specs,
        out_specs=out_specs,
        out_shape=out_shape,
        compiler_params=_params("arbitrary"),
        name="outproj_ln",
    )(*args)


MOE_TM = 256
MOE_TF = 1024
MOE_TN = 512
ROUTE_TM = 512
PERMUTE_CHUNK = 1024


def _route_kernel(lg_ref, ii_ref, gf_ref, cnt_ref, carry, *, n_valid):
    tm = lg_ref.shape[0]
    i = pl.program_id(0)

    @pl.when(i == 0)
    def _():
        carry[...] = jnp.zeros_like(carry)

    lane = lax.broadcasted_iota(jnp.int32, (tm, 128), 1)
    valid = (i * tm + lax.broadcasted_iota(jnp.int32, (tm, 128), 0)) < n_valid
    lg = jnp.where(lane < N_EXPERTS, lg_ref[...], -jnp.inf)
    m1 = jnp.max(lg, axis=-1, keepdims=True)
    i1 = jnp.min(jnp.where(lg == m1, lane, 128), axis=-1, keepdims=True)
    lg2 = jnp.where(lane == i1, -jnp.inf, lg)
    m2 = jnp.max(lg2, axis=-1, keepdims=True)
    i2 = jnp.min(jnp.where(lg2 == m2, lane, 128), axis=-1, keepdims=True)
    t = jnp.exp(m2 - m1)
    g1 = 1.0 / (1.0 + t)
    g2 = t / (1.0 + t)
    oh1 = jnp.where((lane == i1) & valid, 1.0, 0.0)
    oh2 = jnp.where((lane == i2) & valid, 1.0, 0.0)
    cnt = oh1 + oh2
    strict = (lax.broadcasted_iota(jnp.int32, (tm, tm), 0) > lax.broadcasted_iota(jnp.int32, (tm, tm), 1))
    before = jnp.dot(strict.astype(BF16), cnt.astype(BF16), preferred_element_type=F32) + carry[...]
    r1 = jnp.sum(before * oh1, axis=-1, keepdims=True).astype(jnp.int32)
    r2 = jnp.sum(before * oh2, axis=-1, keepdims=True).astype(jnp.int32)
    carry[...] += jnp.sum(cnt, axis=0, keepdims=True)
    ii_ref[...] = jnp.where(lane == 0, i1, jnp.where(lane == 1, i2, jnp.where(lane == 2, r1, r2)))
    gf_ref[...] = jnp.where(lane == 0, g1, g2)
    cnt_ref[...] = carry[...]


def moe_route(logits, n_valid):
    npad = logits.shape[0]
    tm = ROUTE_TM
    row = pl.BlockSpec((tm, 128), lambda i: (i, 0))
    info, gates, counts = pl.pallas_call(
        functools.partial(_route_kernel, n_valid=n_valid),
        grid=(npad // tm,),
        in_specs=[row],
        out_specs=(row, row, pl.BlockSpec((1, 128), lambda i: (0, 0))),
        out_shape=(jax.ShapeDtypeStruct((npad, 128), jnp.int32), jax.ShapeDtypeStruct((npad, 128), F32),
                   jax.ShapeDtypeStruct((1, 128), F32)),
        scratch_shapes=[pltpu.VMEM((1, 128), F32)],
        compiler_params=_params("arbitrary"),
        name="moe_route",
    )(logits)
    return info[:, 0:2], info[:, 2:4], gates, counts[0, :N_EXPERTS].astype(jnp.int32)


SLAB = (16, 128)


def _slabify_kernel(x_ref, o_ref):
    for c in range(SLAB[0]):
        o_ref[:, c, :] = x_ref[:, c * SLAB[1]:(c + 1) * SLAB[1]]


def slabify(x):
    n, d = x.shape
    tm = _pick_tile(n, (512, 256, 128, 64, 32, 16, 8))
    return pl.pallas_call(
        _slabify_kernel,
        grid=(n // tm,),
        in_specs=[pl.BlockSpec((tm, d), lambda i: (i, 0))],
        out_specs=pl.BlockSpec((tm,) + SLAB, lambda i: (i, 0, 0)),
        out_shape=jax.ShapeDtypeStruct((n,) + SLAB, x.dtype),
        compiler_params=_params("arbitrary"),
        name="slabify",
    )(x)


def _unslab(ref):
    return jnp.concatenate([ref[:, c, :] for c in range(SLAB[0])], axis=1)


def _gather_slabs_kernel(idx_ref, a_ref, b_ref, o_ref, sem, *, n_a):
    ch = idx_ref.shape[2]

    def issue(r, c):
        j = idx_ref[0, 0, r]

        @pl.when(j < n_a)
        def _():
            pltpu.make_async_copy(a_ref.at[j], o_ref.at[r], sem).start()

        @pl.when(j >= n_a)
        def _():
            pltpu.make_async_copy(b_ref.at[j - n_a], o_ref.at[r], sem).start()

        return c

    lax.fori_loop(0, ch, issue, 0)

    def drain(r, c):
        pltpu.make_async_copy(a_ref.at[0], o_ref.at[0], sem).wait()
        return c

    lax.fori_loop(0, ch, drain, 0)


def gather_slabs(src_a, src_b, idx):
    n = idx.shape[0]
    ch = PERMUTE_CHUNK
    return pl.pallas_call(
        functools.partial(_gather_slabs_kernel, n_a=src_a.shape[0]),
        grid=(n // ch,),
        in_specs=[pl.BlockSpec((1, 1, ch), lambda i: (i, 0, 0), memory_space=pltpu.SMEM),
                  pl.BlockSpec(memory_space=pl.ANY), pl.BlockSpec(memory_space=pl.ANY)],
        out_specs=pl.BlockSpec((ch,) + SLAB, lambda i: (i, 0, 0)),
        out_shape=jax.ShapeDtypeStruct((n,) + SLAB, src_a.dtype),
        scratch_shapes=[pltpu.SemaphoreType.DMA(())],
        compiler_params=_params("arbitrary"),
        name="gather_slabs",
    )(idx.reshape(n // ch, 1, ch), src_a, src_b)


def _moe_up_kernel(te_ref, tfirst_ref, tused_ref, x_ref, wg_ref, wu_ref, h_ref, wgb, wub):
    i = pl.program_id(1)

    @pl.when((i == 0) | (tfirst_ref[i] == 1))
    def _():
        wgb[...] = wg_ref[...].astype(BF16)
        wub[...] = wu_ref[...].astype(BF16)

    @pl.when(tused_ref[i] == 1)
    def _():
        xb = _unslab(x_ref).astype(BF16)
        gate = jnp.dot(xb, wgb[...], preferred_element_type=F32)
        up = jnp.dot(xb, wub[...], preferred_element_type=F32)
        h_ref[...] = (jax.nn.silu(gate) * up).astype(BF16)

    @pl.when(tused_ref[i] == 0)
    def _():
        h_ref[...] = jnp.zeros_like(h_ref)


def _moe_down_kernel(te_ref, tfirst_ref, tused_ref, h_ref, wd_ref, y_ref, wdb):
    i = pl.program_id(1)

    @pl.when((i == 0) | (tfirst_ref[i] == 1))
    def _():
        wdb[...] = wd_ref[...].astype(BF16)

    @pl.when(tused_ref[i] == 1)
    def _():
        y_ref[...] = jnp.dot(h_ref[...], wdb[...], preferred_element_type=F32)

    @pl.when(tused_ref[i] == 0)
    def _():
        y_ref[...] = jnp.zeros_like(y_ref)


def moe_experts(xs, tile_e, tile_first, tile_used, wg, wu, wd):
    r = xs.shape[0]
    d, d_ff = wg.shape[1], wg.shape[2]
    tm, tf, tn = MOE_TM, MOE_TF, MOE_TN
    n_tiles = r // tm
    h = pl.pallas_call(
        _moe_up_kernel,
        grid_spec=pltpu.PrefetchScalarGridSpec(
            num_scalar_prefetch=3,
            grid=(d_ff // tf, n_tiles),
            in_specs=[pl.BlockSpec((tm,) + SLAB, lambda j, i, te, t1, tu: (i, 0, 0)),
                      pl.BlockSpec((None, d, tf), lambda j, i, te, t1, tu: (te[i], 0, j)),
                      pl.BlockSpec((None, d, tf), lambda j, i, te, t1, tu: (te[i], 0, j))],
            out_specs=pl.BlockSpec((tm, tf), lambda j, i, te, t1, tu: (i, j)),
            scratch_shapes=[pltpu.VMEM((d, tf), BF16), pltpu.VMEM((d, tf), BF16)]),
        out_shape=jax.ShapeDtypeStruct((r, d_ff), BF16),
        compiler_params=_params("arbitrary", "arbitrary"),
        name="moe_up",
    )(tile_e, tile_first, tile_used, xs, wg, wu)
    return pl.pallas_call(
        _moe_down_kernel,
        grid_spec=pltpu.PrefetchScalarGridSpec(
            num_scalar_prefetch=3,
            grid=(d // tn, n_tiles),
            in_specs=[pl.BlockSpec((tm, d_ff), lambda j, i, te, t1, tu: (i, 0)),
                      pl.BlockSpec((None, d_ff, tn), lambda j, i, te, t1, tu: (te[i], 0, j))],
            out_specs=pl.BlockSpec((tm, tn), lambda j, i, te, t1, tu: (i, j)),
            scratch_shapes=[pltpu.VMEM((d_ff, tn), BF16)]),
        out_shape=jax.ShapeDtypeStruct((r, d), F32),
        compiler_params=_params("arbitrary", "arbitrary"),
        name="moe_down",
    )(tile_e, tile_first, tile_used, h, wd)


def _moe_combine_kernel(x_ref, y0_ref, y1_ref, gt_ref, g_ref, b_ref, o_ref):
    gt = gt_ref[...]
    y = gt[:, 0:1] * _unslab(y0_ref) + gt[:, 1:2] * _unslab(y1_ref)
    o_ref[...] = _layer_norm_rows(ALPHA * x_ref[...] + y, g_ref[...], b_ref[...])


def moe_combine_ln(x, yg, gates, row0, n_tok_pad, g, b):
    n, d = x.shape
    tm = _pick_tile(n, (512, 256, 128, 64, 32, 16, 8))
    o0, o1 = row0 // tm, (n_tok_pad + row0) // tm
    return pl.pallas_call(
        _moe_combine_kernel,
        grid=(n // tm,),
        in_specs=[pl.BlockSpec((tm, d), lambda i: (i, 0)),
                  pl.BlockSpec((tm,) + SLAB, lambda i: (o0 + i, 0, 0)),
                  pl.BlockSpec((tm,) + SLAB, lambda i: (o1 + i, 0, 0)),
                  pl.BlockSpec((tm, 128), lambda i: (o0 + i, 0)),
                  pl.BlockSpec((1, d), lambda i: (0, 0)), pl.BlockSpec((1, d), lambda i: (0, 0))],
        out_specs=pl.BlockSpec((tm, d), lambda i: (i, 0)),
        out_shape=jax.ShapeDtypeStruct((n, d), F32),
        compiler_params=_params("arbitrary"),
        name="moe_combine_ln",
    )(x, yg, yg, gates, g.reshape(1, -1), b.reshape(1, -1))


def moe_ln(x_groups, logit_groups, wg, wu, wd, g, b):
    d = x_groups[0].shape[1]
    n_tok = sum(x.shape[0] for x in x_groups)
    n_tok_pad = -(-n_tok // PERMUTE_CHUNK) * PERMUTE_CHUNK
    n_tok_pad = -(-n_tok_pad // ROUTE_TM) * ROUTE_TM
    logits = jnp.concatenate(logit_groups + [jnp.zeros((n_tok_pad - n_tok, 128), F32)], axis=0)
    experts, ranks, gates, counts = moe_route(logits, n_tok)
    tm = MOE_TM
    n_tiles = -(-(n_tok * TOP_K + N_EXPERTS * (tm - 1)) // tm)
    n_tiles = -(-n_tiles * tm // PERMUTE_CHUNK) * PERMUTE_CHUNK // tm
    padded = (counts + tm - 1) // tm * tm
    pad_end = jnp.cumsum(padded)
    pad_start = pad_end - padded
    valid = (jnp.arange(n_tok_pad) < n_tok)[:, None]
    dest = jnp.where(valid, pad_start[experts] + ranks, 0)
    tok = jnp.broadcast_to(jnp.arange(n_tok_pad, dtype=jnp.int32)[:, None], dest.shape)
    row_tok = jnp.zeros((n_tiles * tm,), jnp.int32).at[jnp.where(valid, dest, n_tiles * tm).reshape(-1)].set(
        tok.reshape(-1), mode='drop')
    tile_start = jnp.arange(n_tiles, dtype=jnp.int32) * tm
    tile_e = jnp.minimum(jnp.searchsorted(pad_end, tile_start, side='right'), N_EXPERTS - 1).astype(jnp.int32)
    tile_used = (tile_start < pad_end[-1]).astype(jnp.int32)
    tile_first = jnp.concatenate([jnp.ones((1,), jnp.int32), (tile_e[1:] != tile_e[:-1]).astype(jnp.int32)])
    assert len(x_groups) == 2
    xs = gather_slabs(slabify(x_groups[0]), x_groups[1].reshape((-1,) + SLAB), row_tok)
    ys = slabify(moe_experts(xs, tile_e, tile_first, tile_used, wg, wu, wd))
    yg = gather_slabs(ys, ys, jnp.concatenate([dest[:, 0], dest[:, 1]]).astype(jnp.int32))
    outs, row0 = [], 0
    for x in x_groups:
        outs.append(moe_combine_ln(x, yg, gates, row0, n_tok_pad, g, b))
        row0 += x.shape[0]
    return outs


def _ffn_kernel(x_ref, wg_ref, wu_ref, wd_ref, g_ref, b_ref, o_ref, xb_ref, *, d_ff):
    j = pl.program_id(1)
    tf = wg_ref.shape[1]

    @pl.when(j == 0)
    def _():
        xb_ref[...] = x_ref[...].astype(BF16)
        o_ref[...] = jnp.zeros_like(o_ref)

    xb = xb_ref[...]
    gate = jnp.dot(xb, wg_ref[...], preferred_element_type=F32)
    up = jnp.dot(xb, wu_ref[...], preferred_element_type=F32)
    col = j * tf + lax.broadcasted_iota(jnp.int32, (1, tf), 1)
    a = jnp.where(col < d_ff, jax.nn.silu(gate) * up, 0.0).astype(BF16)
    rowi = j * tf + lax.broadcasted_iota(jnp.int32, (tf, 1), 0)
    wd = jnp.where(rowi < d_ff, wd_ref[...], jnp.zeros((), BF16))
    o_ref[...] += jnp.dot(a, wd, preferred_element_type=F32)

    @pl.when(j == pl.num_programs(1) - 1)
    def _():
        o_ref[...] = _layer_norm_rows(ALPHA * x_ref[...] + o_ref[...], g_ref[...], b_ref[...])


def ffn_ln(x, wg, wu, wd, g, b):
    n, d = x.shape
    d_ff = wg.shape[1]
    tm = _pick_tile(n, (512, 256, 128, 64, 32, 16, 8))
    tf = 512
    kern = functools.partial(_ffn_kernel, d_ff=d_ff)
    return pl.pallas_call(
        kern,
        grid=(n // tm, pl.cdiv(d_ff, tf)),
        in_specs=[pl.BlockSpec((tm, d), lambda i, j: (i, 0)),
                  pl.BlockSpec((d, tf), lambda i, j: (0, j)), pl.BlockSpec((d, tf), lambda i, j: (0, j)),
                  pl.BlockSpec((tf, d), lambda i, j: (j, 0)),
                  pl.BlockSpec((1, d), lambda i, j: (0, 0)), pl.BlockSpec((1, d), lambda i, j: (0, 0))],
        out_specs=pl.BlockSpec((tm, d), lambda i, j: (i, 0)),
        out_shape=jax.ShapeDtypeStruct((n, d), F32),
        scratch_shapes=[pltpu.VMEM((tm, d), BF16)],
        compiler_params=_params("arbitrary", "arbitrary"),
        name="ffn_ln",
    )(x, wg, wu, wd, g.reshape(1, -1), b.reshape(1, -1))


def prompt_mixer(x2, w_in_packed, w_out_bf16, ln_g, ln_b, gla_w2, gla_b, gla_norm_g, cmp_pos, cmp_w1, cmp_w2,
                 pool_w, pool_scale, n_batch, t_len, router=None):
    p = matmul(x2, w_in_packed)
    pos = jnp.arange(t_len, dtype=jnp.int32)
    rows, win, qn, qr, ks, vs, kw, vw, gates = nsa_prep(p, rope_tables(pos), n_batch, t_len)
    cmp_kv = nsa_compress_prompt(rows, cmp_pos, cmp_w1, cmp_w2, n_batch, t_len)
    o_cmp, selb = nsa_select(qn, cmp_kv, t_len // CMP_STRIDE - 1, t_len)
    y_nsa = nsa_attend(qr, o_cmp, selb, gates, ks, vs, kw, vw)
    s0 = jnp.zeros((n_batch, GLA_HEADS, GLA_DK, GLA_DV), F32)
    y_gla, s_gla = gla_mix(p, s0, gla_w2, gla_b, gla_norm_g, n_batch, t_len)
    prev = jnp.zeros((n_batch, POOL_MAX - 1, POOL_WIDTH), F32)
    y_pool = pool_mix(p, prev, 0, pool_w, pool_scale, n_batch, t_len)
    x1 = outproj_ln(x2, y_gla, y_nsa, y_pool, w_out_bf16, ln_g, ln_b, router)
    nsa_rows = rows.reshape(n_batch, t_len, 4, NSA_KV_HEADS, HEAD_DIM)
    n_win = min(WINDOW, t_len)
    win_rows = win.reshape(n_batch, t_len, 2, NSA_KV_HEADS, HEAD_DIM)[:, t_len - n_win:]
    pool_rows = p.reshape(n_batch, t_len, PACKED_WIDTH)[:, t_len - (POOL_MAX - 1):, COL_POOL:COL_POOL + POOL_WIDTH]
    return x1, nsa_rows, win_rows, s_gla, pool_rows


def split_proj(p):
    out = {}
    off = 0
    for name, size in PROJ_SIZES:
        out[name] = p[..., off:off + size]
        off += size
    return out


def layer_norm(x, g, b):
    xf = x.astype(jnp.float32)
    xc = xf - jnp.mean(xf, -1, keepdims=True)
    var = jnp.mean(xc * xc, -1, keepdims=True)
    return (xc * lax.rsqrt(var + LN_EPS) * g + b).astype(x.dtype)


def rope(x, pos):
    half = ROPE_DIM // 2
    inv_freq = ROPE_THETA ** (-jnp.arange(half, dtype=jnp.float32) / half)
    ang = pos.astype(jnp.float32)[:, None] * inv_freq[None, :]
    cos = jnp.cos(ang)[:, None, :]
    sin = jnp.sin(ang)[:, None, :]
    xf = x.astype(jnp.float32)
    x1, x2 = xf[..., :half], xf[..., half:ROPE_DIM]
    out = jnp.concatenate([x1 * cos - x2 * sin, x2 * cos + x1 * sin, xf[..., ROPE_DIM:]], axis=-1)
    return out.astype(x.dtype)


def masked_softmax(s, mask):
    s = jnp.where(mask, s.astype(jnp.float32), -jnp.inf)
    m = jnp.max(s, axis=-1, keepdims=True)
    m = jnp.where(jnp.isfinite(m), m, 0.0)
    p = jnp.where(mask, jnp.exp(s - m), 0.0)
    return p / jnp.maximum(jnp.sum(p, -1, keepdims=True), 1e-30)


def gla_recurrence(q, k, v, g, s0):
    B, T, H, _ = q.shape
    C = GLA_CHUNK
    n_chunks = -(-T // C)
    pad = n_chunks * C - T

    def prep(a):
        a = jnp.pad(a, ((0, 0), (0, pad), (0, 0), (0, 0)))
        return a.reshape(B, n_chunks, C, H, a.shape[-1]).transpose(1, 0, 3, 2, 4)

    causal = jnp.tril(jnp.ones((C, C), dtype=bool))

    def step(S, inp):
        qi, ki, vi, gi = [a.astype(jnp.float32) for a in inp]
        b = jnp.cumsum(gi, axis=2)
        o_inter = jnp.einsum('bhtk,bhkv->bhtv', qi * jnp.exp(b), S)
        diff = jnp.where(causal[:, :, None], b[:, :, :, None, :] - b[:, :, None, :, :], -jnp.inf)
        attn = jnp.einsum('bhtk,bhsk,bhtsk->bhts', qi, ki, jnp.exp(diff))
        o = o_inter + jnp.einsum('bhts,bhsv->bhtv', attn, vi)
        b_last = b[:, :, -1:, :]
        S = jnp.exp(b_last[:, :, 0, :])[..., None] * S + jnp.einsum('bhsk,bhsv->bhkv', ki * jnp.exp(b_last - b), vi)
        return S, o

    S, o = lax.scan(step, s0.astype(jnp.float32), (prep(q), prep(k), prep(v), prep(g)))
    o = o.transpose(1, 0, 3, 2, 4).reshape(B, n_chunks * C, H, v.shape[-1])[:, :T]
    return o, S.astype(s0.dtype)


def gla_mixer(parts, s0, w2, b2, norm_g):
    B, T = parts['gla_q'].shape[:2]
    q = parts['gla_q'].reshape(B, T, GLA_HEADS, GLA_DK) * (GLA_DK ** -0.5)
    k = parts['gla_k'].reshape(B, T, GLA_HEADS, GLA_DK)
    v = parts['gla_v'].reshape(B, T, GLA_HEADS, GLA_DV)
    g = jax.nn.log_sigmoid((parts['gla_glr'] @ w2 + b2).astype(jnp.float32)) / GLA_GATE_NORM
    g = g.reshape(B, T, GLA_HEADS, GLA_DK)
    o, s_new = gla_recurrence(q, k, v, g, s0)
    o = o * lax.rsqrt(jnp.mean(o * o, -1, keepdims=True) + RMS_EPS)
    out = o.reshape(B, T, GLA_WIDTH) * norm_g * jax.nn.silu(parts['gla_r'].astype(jnp.float32))
    return out.astype(parts['gla_v'].dtype), s_new


def nsa_compress(kx, pos_emb, w1, w2):
    B, T, G, D = kx.shape
    nh = T // CMP_STRIDE
    halves = kx[:, :nh * CMP_STRIDE].reshape(B, nh, CMP_STRIDE, G, D).astype(jnp.float32)
    pe = pos_emb.reshape(2, CMP_STRIDE, D)
    w = w1.reshape(2, CMP_STRIDE, D, CMP_HIDDEN)
    h_lo = jnp.einsum('bnjgd,jdh->bngh', halves + pe[0][None, None, :, None, :], w[0])
    h_hi = jnp.einsum('bnjgd,jdh->bngh', halves + pe[1][None, None, :, None, :], w[1])
    h = jax.nn.gelu(h_lo[:, :-1] + h_hi[:, 1:])
    return jnp.einsum('bngh,hd->bngd', h, w2)


def nsa_global(qn, qr, rows, q_pos, cmp_pos, cmp_w1, cmp_w2):
    B, Tq, G, R, D = qn.shape
    Tk = rows.shape[1]
    scale = HEAD_DIM ** -0.5
    kcmp = nsa_compress(rows[:, :, 0], cmp_pos[0], cmp_w1[0], cmp_w2[0])
    vcmp = nsa_compress(rows[:, :, 1], cmp_pos[1], cmp_w1[1], cmp_w2[1])
    n_cmp = kcmp.shape[1]
    cmp_end = jnp.arange(n_cmp) * CMP_STRIDE + CMP_LEN - 1
    n_sel = -(-Tk // SEL_BLOCK)
    pad = n_sel * SEL_BLOCK - Tk

    def to_blocks(a):
        a = jnp.pad(a, ((0, 0), (0, pad), (0, 0), (0, 0)))
        return a.reshape(B, n_sel, SEL_BLOCK, G, D).transpose(0, 3, 1, 2, 4)

    ksb = to_blocks(rows[:, :, 2])
    vsb = to_blocks(rows[:, :, 3])
    ci = jnp.arange(n_cmp)[:, None]
    sj = jnp.arange(n_sel)[None, :]
    overlap = ((ci * CMP_STRIDE <= sj * SEL_BLOCK + SEL_BLOCK - 1) &
               (ci * CMP_STRIDE + CMP_LEN - 1 >= sj * SEL_BLOCK)).astype(jnp.float32)
    n_top = min(SEL_TOPN, n_sel)
    gather = jax.vmap(jax.vmap(lambda blocks, idx: blocks[idx]))
    blk_ids = jnp.arange(n_sel)

    def block_fn(args):
        qnb, qrb, qp = args
        qb = qp.shape[0]
        s = jnp.einsum('bqgrd,bngd->bgrqn', qnb, kcmp) * scale
        p_c = masked_softmax(s, cmp_end[None, :] <= qp[:, None])
        o_c = jnp.einsum('bgrqn,bngd->bqgrd', p_c, vcmp)
        imp = jnp.einsum('bgqn,nj->bgqj', jnp.sum(p_c, axis=2), overlap)
        cur = qp[:, None] // SEL_BLOCK
        valid = blk_ids[None, :] <= cur
        forced = (blk_ids[None, :] == 0) | (blk_ids[None, :] == cur) | (blk_ids[None, :] == cur - 1)
        imp = jnp.where(valid, jnp.where(forced, jnp.inf, imp), -jnp.inf)
        top_s, top_i = lax.top_k(imp, n_top)
        kg = gather(ksb, top_i)
        vg = gather(vsb, top_i)
        kpos = top_i[..., None] * SEL_BLOCK + jnp.arange(SEL_BLOCK)
        mask = (top_s > -jnp.inf)[..., None] & (kpos <= qp[None, None, :, None, None])
        s2 = jnp.einsum('bqgrd,bgqnjd->bgrqnj', qrb, kg) * scale
        s2 = s2.reshape(B, G, R, qb, n_top * SEL_BLOCK)
        p_s = masked_softmax(s2, mask.reshape(B, G, 1, qb, n_top * SEL_BLOCK))
        p_s = p_s.reshape(B, G, R, qb, n_top, SEL_BLOCK)
        o_s = jnp.einsum('bgrqnj,bgqnjd->bqgrd', p_s, vg.astype(jnp.float32))
        return o_c, o_s

    qb = min(Tq, Q_BLOCK)
    nq = -(-Tq // qb)
    padq = nq * qb - Tq
    qpad = ((0, 0), (0, padq), (0, 0), (0, 0), (0, 0))
    qn_b = jnp.pad(qn, qpad).reshape(B, nq, qb, G, R, D).swapaxes(0, 1)
    qr_b = jnp.pad(qr, qpad).reshape(B, nq, qb, G, R, D).swapaxes(0, 1)
    qp_b = jnp.pad(q_pos, (0, padq), mode='edge').reshape(nq, qb)
    o_c, o_s = lax.map(block_fn, (qn_b, qr_b, qp_b))
    o_c = o_c.swapaxes(0, 1).reshape(B, nq * qb, G, R, D)[:, :Tq]
    o_s = o_s.swapaxes(0, 1).reshape(B, nq * qb, G, R, D)[:, :Tq]
    return o_c, o_s


def band_attend(q, k, v, qpos, kpos):
    s = jnp.einsum('bqgrd,bkgd->bgrqk', q, k) * (HEAD_DIM ** -0.5)
    mask = ((kpos[None, :] <= qpos[:, None]) & (kpos[None, :] > qpos[:, None] - WINDOW) & (kpos[None, :] >= 0))
    p = masked_softmax(s, mask)
    return jnp.einsum('bgrqk,bkgd->bqgrd', p, v.astype(jnp.float32))


def sliding_prompt(q, k, v):
    B, T, G, R, D = q.shape
    kp = jnp.pad(k, ((0, 0), (WINDOW, 0), (0, 0), (0, 0)))
    vp = jnp.pad(v, ((0, 0), (WINDOW, 0), (0, 0), (0, 0)))
    nq = T // Q_BLOCK

    def fn(i):
        q0 = i * Q_BLOCK
        qi = lax.dynamic_slice_in_dim(q, q0, Q_BLOCK, axis=1)
        ki = lax.dynamic_slice_in_dim(kp, q0, WINDOW + Q_BLOCK, axis=1)
        vi = lax.dynamic_slice_in_dim(vp, q0, WINDOW + Q_BLOCK, axis=1)
        qpos = q0 + jnp.arange(Q_BLOCK)
        kpos = q0 - WINDOW + jnp.arange(WINDOW + Q_BLOCK)
        return band_attend(qi, ki, vi, qpos, kpos)

    o = lax.map(fn, jnp.arange(nq))
    return o.swapaxes(0, 1).reshape(B, T, G, R, D)


def nsa_mixer(parts, pos, nsa_past, win_past, cmp_pos, cmp_w1, cmp_w2):
    B, T = parts['nsa_q'].shape[:2]
    dt = parts['nsa_q'].dtype
    q = parts['nsa_q'].reshape(B, T, NSA_HEADS, HEAD_DIM)
    q_rope = rope(q, pos)
    kv = lambda name: parts[name].reshape(B, T, NSA_KV_HEADS, HEAD_DIM)
    k_win, v_win = rope(kv('win_k'), pos), kv('win_v')
    new_rows = jnp.stack([kv('cmp_k'), kv('cmp_v'), rope(kv('slc_k'), pos), kv('slc_v')], axis=2)
    rows = new_rows if nsa_past is None else jnp.concatenate([nsa_past.astype(dt), new_rows], axis=1)
    qg = q.reshape(B, T, NSA_KV_HEADS, NSA_REP, HEAD_DIM)
    qrg = q_rope.reshape(B, T, NSA_KV_HEADS, NSA_REP, HEAD_DIM)
    o_cmp, o_slc = nsa_global(qg, qrg, rows, pos, cmp_pos, cmp_w1, cmp_w2)
    win_rows = jnp.stack([k_win, v_win], axis=2)
    if win_past is None:
        o_win = sliding_prompt(qrg, k_win, v_win)
        new_win = win_rows[:, T - min(WINDOW, T):]
    else:
        n_buf = win_past.shape[1]
        ext = jnp.concatenate([win_past.astype(dt), win_rows], axis=1)
        kpos = pos[0] - n_buf + jnp.arange(n_buf + T)
        o_win = band_attend(qrg, ext[:, :, 0], ext[:, :, 1], pos, kpos)
        new_win = ext[:, T:]
    gates = jax.nn.sigmoid(parts['nsa_gate'].astype(jnp.float32)).reshape(B, T, NSA_KV_HEADS, NSA_REP, 3)
    o = gates[..., 0:1] * o_cmp + gates[..., 1:2] * o_slc + gates[..., 2:3] * o_win
    return o.reshape(B, T, NSA_WIDTH).astype(dt), new_rows, new_win


def pool_mixer(u, prev, pos, w_pool, scale):
    B, T, C = u.shape
    P = POOL_MAX - 1
    ext = jnp.concatenate([prev.astype(u.dtype), u], axis=1).astype(jnp.float32)
    cs = jnp.concatenate([jnp.zeros((B, 1, C), jnp.float32), jnp.cumsum(ext, axis=1)], axis=1)
    end = cs[:, P + 1:]
    means = []
    for gi, w in enumerate(POOL_WINDOWS):
        sl = slice(gi * POOL_GROUP_DIM, (gi + 1) * POOL_GROUP_DIM)
        start = cs[:, P + 1 - w:P + 1 - w + T, sl]
        cnt = jnp.minimum(pos + 1, w).astype(jnp.float32)[None, :, None]
        means.append((end[..., sl] - start) / cnt)
    pooled = (jnp.concatenate(means, axis=-1) - ext[:, P:]).reshape(B, T, POOL_GROUPS, POOL_GROUP_DIM)
    y = jnp.einsum('btgc,gcd->btgd', pooled, w_pool.astype(jnp.float32)).reshape(B, T, C) * scale
    return y.astype(u.dtype), ext[:, -P:].astype(u.dtype)


def mixer_layer(x, pos0, gla_s0, pool_prev, nsa_past, win_past,
                w_in, gla_w2, gla_b, gla_norm_g, cmp_pos, cmp_w1, cmp_w2, pool_w, pool_scale, w_out):
    B, T, _ = x.shape
    pos = pos0 + jnp.arange(T, dtype=jnp.int32)
    parts = split_proj(mm3(x, w_in))
    y_gla, s_gla = gla_mixer(parts, gla_s0, gla_w2, gla_b, gla_norm_g)
    y_nsa, nsa_rows, win_rows = nsa_mixer(parts, pos, nsa_past, win_past, cmp_pos, cmp_w1, cmp_w2)
    y_pool, pool_rows = pool_mixer(parts['pool'], pool_prev, pos, pool_w, pool_scale)
    y = mm3(jnp.concatenate([y_gla, y_nsa, y_pool], axis=-1), w_out)
    return y, nsa_rows, win_rows, s_gla, pool_rows


def swiglu(x, wg, wu, wd):
    return mm3(jax.nn.silu(mm3(x, wg)) * mm3(x, wu), wd)


def moe_ffn(x, router, wg, wu, wd):
    B, T, D = x.shape
    n_tok = B * T
    xt = x.reshape(n_tok, D)
    logits = (xt @ router).astype(jnp.float32)
    top_v, top_i = lax.top_k(logits, TOP_K)
    gates = jax.nn.softmax(top_v, axis=-1)
    n_asg = n_tok * TOP_K
    e_flat = top_i.reshape(n_asg)
    tok_flat = jnp.arange(n_asg, dtype=jnp.int32) // TOP_K
    gate_flat = gates.reshape(n_asg)
    blk = MOE_ROW_BLOCK if n_asg >= N_EXPERTS * MOE_ROW_BLOCK else MOE_MIN_BLOCK
    n_blk = -(-(n_asg + N_EXPERTS * (blk - 1)) // blk)
    order = jnp.argsort(e_flat)
    e_sorted = e_flat[order]
    counts = jnp.bincount(e_flat, length=N_EXPERTS)
    padded = (counts + blk - 1) // blk * blk
    pad_end = jnp.cumsum(padded)
    pad_start = pad_end - padded
    start = jnp.cumsum(counts) - counts
    dest = pad_start[e_sorted] + jnp.arange(n_asg) - start[e_sorted]
    row_tok = jnp.zeros((n_blk * blk,), jnp.int32).at[dest].set(tok_flat[order])
    row_gate = jnp.zeros((n_blk * blk,), jnp.float32).at[dest].set(gate_flat[order])
    blk_expert = jnp.minimum(jnp.searchsorted(pad_end, jnp.arange(n_blk) * blk, side='right'), N_EXPERTS - 1)

    def expert_block(args):
        rows, e = args
        xb = xt[rows]
        return (jax.nn.silu(xb @ wg[e]) * (xb @ wu[e])) @ wd[e]

    out = lax.map(expert_block, (row_tok.reshape(n_blk, blk), blk_expert))
    y = jnp.zeros((n_tok, D), jnp.float32).at[row_tok].add(
        out.reshape(n_blk * blk, D).astype(jnp.float32) * row_gate[:, None])
    return y.reshape(B, T, D).astype(x.dtype)


def kernel(x_prompt, x_sample, cache_nsa, page_table, state_win, state_gla, state_pool, w_in, gla_gate_w2, gla_gate_b, gla_norm_g, nsa_cmp_pos, nsa_cmp_w1, nsa_cmp_w2, pool_w, pool_scale, w_out, ln1_g, ln1_b, ln2_g, ln2_b, ffn_w_gate, ffn_w_up, ffn_w_down, moe_router, moe_w_gate, moe_w_up, moe_w_down):
    n_prompt, t_len, d = x_prompt.shape
    n_dec = x_sample.shape[0]
    xp, xs = x_prompt.reshape(n_prompt * t_len, d), x_sample
    nsa_p, nsa_s, win_p, win_s, gla_p, gla_s, pool_p, pool_s = [], [], [], [], [], [], [], []
    for l in range(DEPTH):
        lw = (w_in[l], gla_gate_w2[l], gla_gate_b[l], gla_norm_g[l], nsa_cmp_pos[l], nsa_cmp_w1[l],
              nsa_cmp_w2[l], pool_w[l], pool_scale[l], w_out[l])
        i = l // 2
        router = moe_router[i] if l % 2 else None
        xp, r_p, w_p, g_p, p_p = prompt_mixer(
            xp, pack_w_in(w_in[l]), w_out[l].astype(BF16), ln1_g[l], ln1_b[l], gla_gate_w2[l], gla_gate_b[l],
            gla_norm_g[l], nsa_cmp_pos[l], nsa_cmp_w1[l], nsa_cmp_w2[l], pool_w[l], pool_scale[l], n_prompt, t_len,
            router)
        past_rows = cache_nsa[l, page_table].reshape(n_dec, -1, 4, NSA_KV_HEADS, HEAD_DIM)
        hs, r_s, w_s, g_s, p_s = mixer_layer(xs, PAST_LEN, state_gla[l], state_pool[l], past_rows, state_win[l], *lw)
        xs = layer_norm(ALPHA * xs + hs, ln1_g[l], ln1_b[l])
        if l % 2 == 0:
            xp = ffn_ln(xp, ffn_w_gate[i].astype(BF16), ffn_w_up[i].astype(BF16), ffn_w_down[i].astype(BF16),
                        ln2_g[l], ln2_b[l])
            fs = swiglu(xs, ffn_w_gate[i], ffn_w_up[i], ffn_w_down[i])
            xs = layer_norm(ALPHA * xs + fs, ln2_g[l], ln2_b[l])
        else:
            xp, lg_p = xp
            xs2 = xs.reshape(n_dec, d)
            lg_s = jnp.dot(xs2, router_pad(router), precision=lax.Precision.HIGHEST)
            xp, xs2 = moe_ln([xp, xs2], [lg_p, lg_s], moe_w_gate[i], moe_w_up[i], moe_w_down[i], ln2_g[l], ln2_b[l])
            xs = xs2.reshape(xs.shape)
        nsa_p.append(r_p); nsa_s.append(r_s); win_p.append(w_p); win_s.append(w_s)
        gla_p.append(g_p); gla_s.append(g_s); pool_p.append(p_p); pool_s.append(p_s)
    return (xp.reshape(n_prompt, t_len, d), xs, jnp.stack(nsa_p), jnp.stack(nsa_s), jnp.stack(win_p),
            jnp.stack(win_s), jnp.stack(gla_p), jnp.stack(gla_s), jnp.stack(pool_p), jnp.stack(pool_s))
```

```python
import functools

import jax
import jax.numpy as jnp
from jax import lax
from jax.experimental import pallas as pl
from jax.experimental.pallas import tpu as pltpu

D_MODEL = 2048
DEPTH = 2
PAST_LEN = 16384
HEAD_DIM = 128
GLA_HEADS = 4
GLA_DK = 64
GLA_DV = 128
GLA_RANK = 16
GLA_GATE_NORM = 16.0
GLA_CHUNK = 64
GLA_WIDTH = GLA_HEADS * GLA_DV
NSA_HEADS = 8
NSA_KV_HEADS = 2
NSA_REP = NSA_HEADS // NSA_KV_HEADS
NSA_WIDTH = NSA_HEADS * HEAD_DIM
CMP_LEN = 32
CMP_STRIDE = 16
CMP_HIDDEN = 128
SEL_BLOCK = 64
SEL_TOPN = 16
WINDOW = 512
Q_BLOCK = 128
POOL_GROUPS = 4
POOL_GROUP_DIM = 128
POOL_WIDTH = POOL_GROUPS * POOL_GROUP_DIM
POOL_WINDOWS = (2, 4, 8, 16)
POOL_MAX = 16
ROPE_THETA = 500000.0
ROPE_DIM = HEAD_DIM // 4
N_EXPERTS = 8
TOP_K = 2
MOE_ROW_BLOCK = 128
MOE_MIN_BLOCK = 8
ALPHA = (2 * DEPTH) ** 0.25
LN_EPS = 1e-5
RMS_EPS = 1e-6

PROJ_SIZES = (
    ('gla_q', GLA_HEADS * GLA_DK), ('gla_k', GLA_HEADS * GLA_DK), ('gla_v', GLA_HEADS * GLA_DV),
    ('gla_glr', GLA_RANK), ('gla_r', GLA_HEADS * GLA_DV),
    ('nsa_q', NSA_HEADS * HEAD_DIM),
    ('cmp_k', NSA_KV_HEADS * HEAD_DIM), ('cmp_v', NSA_KV_HEADS * HEAD_DIM),
    ('slc_k', NSA_KV_HEADS * HEAD_DIM), ('slc_v', NSA_KV_HEADS * HEAD_DIM),
    ('win_k', NSA_KV_HEADS * HEAD_DIM), ('win_v', NSA_KV_HEADS * HEAD_DIM),
    ('nsa_gate', 3 * NSA_HEADS),
    ('pool', POOL_WIDTH),
)

GLA_SUB = 16
SEL_PAD = 128

BF16 = jnp.bfloat16
F32 = jnp.float32
NEG_BIG = -1e30
VMEM_LIMIT_BYTES = 56 * 1024 * 1024

COL_NSA_Q = 0
COL_ROWS = 1024
COL_WIN = 2048
COL_POOL = 2560
COL_GLA_V = 3072
COL_GLA_R = 3584
COL_GLA_Q = 4096
COL_GLA_K = 4352
COL_SMALL = 4608
PACKED_WIDTH = 4736
SMALL_GATE_OFF = GLA_RANK


def _params(*sem):
    return pltpu.CompilerParams(dimension_semantics=sem, vmem_limit_bytes=VMEM_LIMIT_BYTES)


def _proj_offsets():
    out, off = {}, 0
    for name, size in PROJ_SIZES:
        out[name] = (off, size)
        off += size
    return out


def pack_w_in(w):
    offs = _proj_offsets()
    sl = lambda n: w[:, offs[n][0]:offs[n][0] + offs[n][1]]
    pad = jnp.zeros((w.shape[0], 128 - GLA_RANK - 3 * NSA_HEADS), w.dtype)
    cols = [sl('nsa_q'), sl('cmp_k'), sl('cmp_v'), sl('slc_k'), sl('slc_v'), sl('win_k'), sl('win_v'),
            sl('pool'), sl('gla_v'), sl('gla_r'), sl('gla_q'), sl('gla_k'), sl('gla_glr'), sl('nsa_gate'), pad]
    return jnp.concatenate(cols, axis=1).astype(BF16)


def rope_tables(pos):
    half = ROPE_DIM // 2
    inv_freq = ROPE_THETA ** (-jnp.arange(half, dtype=F32) / half)
    ang = pos.astype(F32)[:, None] * inv_freq[None, :]
    cos, sin = jnp.cos(ang), jnp.sin(ang)
    t = pos.shape[0]
    c = jnp.concatenate([cos, cos, jnp.ones((t, HEAD_DIM - ROPE_DIM), F32)], axis=1)
    sa = jnp.concatenate([-sin, jnp.zeros((t, HEAD_DIM - half), F32)], axis=1)
    sb = jnp.concatenate([jnp.zeros((t, half), F32), sin, jnp.zeros((t, HEAD_DIM - ROPE_DIM), F32)], axis=1)
    return c, sa, sb


def _pick_tile(n, pref):
    for t in pref:
        if n % t == 0:
            return t
    return n


def _mm_kernel(x_ref, w_ref, o_ref, xb_ref):
    @pl.when(pl.program_id(1) == 0)
    def _():
        xb_ref[...] = x_ref[...].astype(BF16)

    o_ref[...] = jnp.dot(xb_ref[...], w_ref[...].astype(BF16), preferred_element_type=F32)


def matmul(x, w):
    m, k = x.shape
    n = w.shape[1]
    tm = _pick_tile(m, tuple(t for t in (1024, 512, 256, 128, 64, 32, 16, 8) if t * k <= 2048 * 1024))
    tn = 512 if n >= 512 else n
    return pl.pallas_call(
        _mm_kernel,
        grid=(m // tm, pl.cdiv(n, tn)),
        in_specs=[pl.BlockSpec((tm, k), lambda i, j: (i, 0)),
                  pl.BlockSpec((k, tn), lambda i, j: (0, j))],
        out_specs=pl.BlockSpec((tm, tn), lambda i, j: (i, j)),
        out_shape=jax.ShapeDtypeStruct((m, n), F32),
        scratch_shapes=[pltpu.VMEM((tm, k), BF16)],
        compiler_params=_params("arbitrary", "arbitrary"),
        name="matmul",
    )(x, w)


def mm3(x, w):
    lead = x.shape[:-1]
    return matmul(x.reshape(-1, x.shape[-1]), w).reshape(*lead, w.shape[1])


def _rope(x, c, sa, sb):
    return x * c + pltpu.roll(x, HEAD_DIM - ROPE_DIM // 2, 1) * sa + pltpu.roll(x, ROPE_DIM // 2, 1) * sb


def _nsa_prep_kernel(q_ref, rows_ref, win_ref, small_ref, c_ref, sa_ref, sb_ref,
                     rows_o, win_o, qn_o, qr_o, ks_o, vs_o, kw_o, vw_o, gate_o):
    c, sa, sb = c_ref[...], sa_ref[...], sb_ref[...]
    scale = HEAD_DIM ** -0.5
    hd = HEAD_DIM
    for h in range(NSA_HEADS):
        x = q_ref[:, h * hd:(h + 1) * hd]
        qn_o[0, h] = (x * scale).astype(BF16)
        qr_o[0, h] = (_rope(x, c, sa, sb) * scale).astype(BF16)
    ones = jnp.ones((q_ref.shape[0], hd), BF16)
    rows_o[:, 0:4 * hd] = rows_ref[:, 0:4 * hd]
    for g in range(NSA_KV_HEADS):
        k = _rope(rows_ref[:, (4 + g) * hd:(5 + g) * hd], c, sa, sb)
        rows_o[:, (4 + g) * hd:(5 + g) * hd] = k
        ks_o[0, g] = k.astype(BF16)
        v = rows_ref[:, (6 + g) * hd:(7 + g) * hd]
        rows_o[:, (6 + g) * hd:(7 + g) * hd] = v
        vs_o[0, g, :, 0:hd] = v.astype(BF16)
        vs_o[0, g, :, hd:2 * hd] = ones
        k = _rope(win_ref[:, g * hd:(g + 1) * hd], c, sa, sb)
        win_o[:, g * hd:(g + 1) * hd] = k
        kw_o[0, g] = k.astype(BF16)
        v = win_ref[:, (2 + g) * hd:(3 + g) * hd]
        win_o[:, (2 + g) * hd:(3 + g) * hd] = v
        vw_o[0, g, :, 0:hd] = v.astype(BF16)
        vw_o[0, g, :, hd:2 * hd] = ones
    sig = jax.nn.sigmoid(small_ref[...])
    per_g = 3 * NSA_REP
    for g in range(NSA_KV_HEADS):
        gate_o[0, g] = pltpu.roll(sig, 128 - SMALL_GATE_OFF - g * per_g, 1)


def nsa_prep(p, tables, n_batch, t_len):
    tr = _pick_tile(t_len, (512, 256, 128, 64, 32, 16))
    nt = t_len // tr
    n = n_batch * t_len
    hd = HEAD_DIM
    row = lambda w, cb: pl.BlockSpec((tr, w), lambda b, i: (b * nt + i, cb))
    tab = pl.BlockSpec((tr, hd), lambda b, i: (i, 0))
    head = lambda nh, w: pl.BlockSpec((1, nh, tr, w), lambda b, i: (b, 0, i, 0))
    out_shape = (
        jax.ShapeDtypeStruct((n, 8 * hd), F32),
        jax.ShapeDtypeStruct((n, 4 * hd), F32),
        jax.ShapeDtypeStruct((n_batch, NSA_HEADS, t_len, hd), BF16),
        jax.ShapeDtypeStruct((n_batch, NSA_HEADS, t_len, hd), BF16),
        jax.ShapeDtypeStruct((n_batch, NSA_KV_HEADS, t_len, hd), BF16),
        jax.ShapeDtypeStruct((n_batch, NSA_KV_HEADS, t_len, 2 * hd), BF16),
        jax.ShapeDtypeStruct((n_batch, NSA_KV_HEADS, t_len, hd), BF16),
        jax.ShapeDtypeStruct((n_batch, NSA_KV_HEADS, t_len, 2 * hd), BF16),
        jax.ShapeDtypeStruct((n_batch, NSA_KV_HEADS, t_len, 128), F32),
    )
    return pl.pallas_call(
        _nsa_prep_kernel,
        grid=(n_batch, nt),
        in_specs=[row(8 * hd, COL_NSA_Q // (8 * hd)), row(8 * hd, COL_ROWS // (8 * hd)),
                  row(4 * hd, COL_WIN // (4 * hd)), row(128, COL_SMALL // 128), tab, tab, tab],
        out_specs=(row(8 * hd, 0), row(4 * hd, 0), head(NSA_HEADS, hd), head(NSA_HEADS, hd),
                   head(NSA_KV_HEADS, hd), head(NSA_KV_HEADS, 2 * hd), head(NSA_KV_HEADS, hd),
                   head(NSA_KV_HEADS, 2 * hd), head(NSA_KV_HEADS, 128)),
        out_shape=out_shape,
        compiler_params=_params("arbitrary", "arbitrary"),
        name="nsa_prep",
    )(p, p, p, p, *tables)


def _nsa_cmp_kernel(x_ref, pe_ref, w1_ref, w2_ref, o_ref):
    nh = o_ref.shape[0]
    h_lo = jnp.zeros((nh, CMP_HIDDEN), F32)
    h_hi = jnp.zeros((nh, CMP_HIDDEN), F32)
    for j in range(CMP_STRIDE):
        xj = x_ref[pl.ds(j, nh, stride=CMP_STRIDE), :]
        h_lo += jnp.dot((xj + pe_ref[j:j + 1, :]).astype(BF16), w1_ref[j].astype(BF16), preferred_element_type=F32)
        h_hi += jnp.dot((xj + pe_ref[CMP_STRIDE + j:CMP_STRIDE + j + 1, :]).astype(BF16),
                        w1_ref[CMP_STRIDE + j].astype(BF16), preferred_element_type=F32)
    h = jax.nn.gelu(h_lo + pltpu.roll(h_hi, nh - 1, 0))
    o_ref[...] = jnp.dot(h.astype(BF16), w2_ref[...].astype(BF16), preferred_element_type=F32).astype(BF16)


def nsa_compress_prompt(rows, cmp_pos, cmp_w1, cmp_w2, n_batch, t_len):
    nh = t_len // CMP_STRIDE
    hd = HEAD_DIM
    rows3 = rows.reshape(n_batch, t_len, 8 * hd)
    return pl.pallas_call(
        _nsa_cmp_kernel,
        grid=(n_batch, 2, NSA_KV_HEADS),
        in_specs=[pl.BlockSpec((None, t_len, hd), lambda b, kd, g: (b, 0, kd * NSA_KV_HEADS + g)),
                  pl.BlockSpec((None, CMP_LEN, hd), lambda b, kd, g: (kd, 0, 0)),
                  pl.BlockSpec((None, CMP_LEN, hd, CMP_HIDDEN), lambda b, kd, g: (kd, 0, 0, 0)),
                  pl.BlockSpec((None, CMP_HIDDEN, hd), lambda b, kd, g: (kd, 0, 0))],
        out_specs=pl.BlockSpec((None, None, None, nh, hd), lambda b, kd, g: (b, kd, g, 0, 0)),
        out_shape=jax.ShapeDtypeStruct((n_batch, 2, NSA_KV_HEADS, nh, hd), BF16),
        compiler_params=_params("arbitrary", "arbitrary", "arbitrary"),
        name="nsa_compress",
    )(rows3, cmp_pos, cmp_w1, cmp_w2)


def _nsa_select_kernel(qn_ref, kc_ref, vc_ref, ovt_ref, oc_ref, selb_ref, *, n_cmp, n_top):
    rep, tq, hd = qn_ref.shape[1], qn_ref.shape[2], qn_ref.shape[3]
    n_cmp_pad = kc_ref.shape[0]
    n_sel = ovt_ref.shape[0]
    q0 = pl.program_id(2) * tq
    q = qn_ref[0].reshape(rep * tq, hd)
    s = lax.dot_general(q, kc_ref[...], (((1,), (1,)), ((), ())), preferred_element_type=F32)
    row = lax.broadcasted_iota(jnp.int32, (rep * tq, n_cmp_pad), 0)
    col = lax.broadcasted_iota(jnp.int32, (rep * tq, n_cmp_pad), 1)
    qpos = q0 + (row & (tq - 1))
    mask = (col * CMP_STRIDE + (CMP_LEN - 1) <= qpos) & (col < n_cmp)
    s = jnp.where(mask, s, -jnp.inf)
    m = jnp.max(s, axis=-1, keepdims=True)
    m = jnp.where(m > -jnp.inf, m, 0.0)
    p = jnp.where(mask, jnp.exp(s - m), 0.0)
    p = p / jnp.maximum(jnp.sum(p, axis=-1, keepdims=True), 1e-30)
    oc = jnp.dot(p.astype(BF16), vc_ref[...], preferred_element_type=F32)
    oc_ref[0] = oc.reshape(rep, tq, hd).astype(BF16)
    psum = p[0:tq]
    for r in range(1, rep):
        psum = psum + p[r * tq:(r + 1) * tq]
    imp = lax.dot_general(ovt_ref[...], psum, (((1,), (1,)), ((), ())), preferred_element_type=F32,
                          precision=lax.Precision.HIGHEST)
    blk = lax.broadcasted_iota(jnp.int32, (n_sel, tq), 0)
    cur = (q0 + lax.broadcasted_iota(jnp.int32, (n_sel, tq), 1)) // SEL_BLOCK
    forced = (blk == 0) | (blk == cur) | (blk == cur - 1)
    v = jnp.where(blk <= cur, jnp.where(forced, jnp.inf, imp), -jnp.inf)
    rank = jnp.zeros((n_sel, tq), jnp.int32)
    for i in range(n_sel):
        vi = v[i:i + 1, :]
        ahead = (vi > v) | ((vi == v) & (blk > i))
        rank = rank + ahead.astype(jnp.int32)
    selb_t = jnp.where((rank < n_top) & (v > -jnp.inf), 0.0, NEG_BIG)
    pad = jnp.full((SEL_PAD - n_sel, tq), NEG_BIG, F32)
    selb_ref[0, 0] = jnp.concatenate([selb_t, pad], axis=0).T.astype(BF16)


def nsa_select(qn, cmp_kv, n_cmp, t_k):
    n_batch, _, t_len, hd = qn.shape
    n_cmp_pad = cmp_kv.shape[3]
    n_sel = -(-t_k // SEL_BLOCK)
    tq = _pick_tile(t_len, (256, 128, 64, 32, 16))
    ci = jnp.arange(n_cmp_pad)[None, :]
    sj = jnp.arange(n_sel)[:, None]
    overlap_t = ((ci * CMP_STRIDE <= sj * SEL_BLOCK + SEL_BLOCK - 1) &
                 (ci * CMP_STRIDE + CMP_LEN - 1 >= sj * SEL_BLOCK) & (ci < n_cmp)).astype(F32)
    kern = functools.partial(_nsa_select_kernel, n_cmp=n_cmp, n_top=min(SEL_TOPN, n_sel))
    return pl.pallas_call(
        kern,
        grid=(n_batch, NSA_KV_HEADS, t_len // tq),
        in_specs=[pl.BlockSpec((1, NSA_REP, tq, hd), lambda b, g, i: (b, g, i, 0)),
                  pl.BlockSpec((None, None, None, n_cmp_pad, hd), lambda b, g, i: (b, 0, g, 0, 0)),
                  pl.BlockSpec((None, None, None, n_cmp_pad, hd), lambda b, g, i: (b, 1, g, 0, 0)),
                  pl.BlockSpec((n_sel, n_cmp_pad), lambda b, g, i: (0, 0))],
        out_specs=(pl.BlockSpec((1, NSA_REP, tq, hd), lambda b, g, i: (b, g, i, 0)),
                   pl.BlockSpec((1, 1, tq, SEL_PAD), lambda b, g, i: (b, g, i, 0))),
        out_shape=(jax.ShapeDtypeStruct((n_batch, NSA_HEADS, t_len, hd), BF16),
                   jax.ShapeDtypeStruct((n_batch, NSA_KV_HEADS, t_len, SEL_PAD), BF16)),
        compiler_params=_params("arbitrary", "arbitrary", "arbitrary"),
        name="nsa_select",
    )(qn, cmp_kv, cmp_kv, overlap_t)


def _nsa_attn_kernel(qr_ref, oc_ref, selb_ref, gate_ref, ks_ref, vs_ref, kw_ref, vw_ref, e_ref, o_ref,
                     m_scr, acc_scr, *, tk, wk):
    rep, qb, hd = qr_ref.shape[1], qr_ref.shape[2], qr_ref.shape[3]
    nr = rep * qb
    q0 = pl.program_id(2) * qb
    q = qr_ref[0].reshape(nr, hd)
    selb = selb_ref[0, 0]
    nt = (((1,), (1,)), ((), ()))

    def sel_scores(t):
        k = ks_ref[0, 0, pl.ds(pl.multiple_of(t * tk, tk), tk), :]
        s = lax.dot_general(q, k, nt, preferred_element_type=F32)
        bias = jnp.dot(selb, e_ref[t], preferred_element_type=F32)
        return (s.reshape(rep, qb, tk) + bias[None]).reshape(nr, tk)

    def sel_values(t):
        return vs_ref[0, 0, pl.ds(pl.multiple_of(t * tk, tk), tk), :]

    td = q0 // tk
    qpos = q0 + (lax.broadcasted_iota(jnp.int32, (nr, tk), 0) & (qb - 1))
    kpos = td * tk + lax.broadcasted_iota(jnp.int32, (nr, tk), 1)
    s = jnp.where(kpos <= qpos, sel_scores(td), NEG_BIG)
    m = jnp.max(s, axis=-1, keepdims=True)
    m_scr[...] = m
    acc_scr[...] = jnp.dot(jnp.exp(s - m).astype(BF16), sel_values(td), preferred_element_type=F32)

    def body(t, carry):
        s = sel_scores(t)
        m_old = m_scr[...]
        m_new = jnp.maximum(m_old, jnp.max(s, axis=-1, keepdims=True))
        p = jnp.exp(s - m_new).astype(BF16)
        acc_scr[...] = jnp.exp(m_old - m_new) * acc_scr[...] + jnp.dot(p, sel_values(t), preferred_element_type=F32)
        m_scr[...] = m_new
        return carry

    lax.fori_loop(0, td, body, 0)
    acc = acc_scr[...]
    o_sel = acc[:, 0:hd] / jnp.maximum(acc[:, hd:hd + 1], 1e-30)

    kstart = pl.multiple_of(jnp.maximum(q0 - WINDOW, 0), qb)
    kw = kw_ref[0, 0, pl.ds(kstart, wk), :]
    s = lax.dot_general(q, kw, nt, preferred_element_type=F32)
    qpos = q0 + (lax.broadcasted_iota(jnp.int32, (nr, wk), 0) & (qb - 1))
    kpos = kstart + lax.broadcasted_iota(jnp.int32, (nr, wk), 1)
    s = jnp.where((kpos <= qpos) & (kpos > qpos - WINDOW), s, NEG_BIG)
    m = jnp.max(s, axis=-1, keepdims=True)
    accw = jnp.dot(jnp.exp(s - m).astype(BF16), vw_ref[0, 0, pl.ds(kstart, wk), :], preferred_element_type=F32)
    o_win = accw[:, 0:hd] / jnp.maximum(accw[:, hd:hd + 1], 1e-30)

    gates = gate_ref[0, 0]
    for r in range(rep):
        rows = slice(r * qb, (r + 1) * qb)
        o = (gates[:, 3 * r:3 * r + 1] * oc_ref[0, r].astype(F32)
             + gates[:, 3 * r + 1:3 * r + 2] * o_sel[rows]
             + gates[:, 3 * r + 2:3 * r + 3] * o_win[rows])
        o_ref[:, r * hd:(r + 1) * hd] = o.astype(BF16)


def nsa_attend(qr, o_cmp, selb, gates, ks, vs, kw, vw):
    n_batch, _, t_len, hd = qr.shape
    n_sel = selb.shape[3]
    qb = Q_BLOCK
    tk = min(512, t_len)
    wk = min(WINDOW + qb, t_len)
    nq = t_len // qb
    n_tiles = t_len // tk
    key_blk = (jnp.arange(n_tiles)[:, None, None] * tk + jnp.arange(tk)[None, None, :]) // SEL_BLOCK
    e = (key_blk == jnp.arange(n_sel)[None, :, None]).astype(BF16)
    kern = functools.partial(_nsa_attn_kernel, tk=tk, wk=wk)
    per_q = lambda nh, w: pl.BlockSpec((1, nh, qb, w), lambda b, g, i: (b, g, i, 0))
    full = lambda w: pl.BlockSpec((1, 1, t_len, w), lambda b, g, i: (b, g, 0, 0))
    return pl.pallas_call(
        kern,
        grid=(n_batch, NSA_KV_HEADS, nq),
        in_specs=[per_q(NSA_REP, hd), per_q(NSA_REP, hd), per_q(1, n_sel), per_q(1, 128),
                  full(hd), full(2 * hd), full(hd), full(2 * hd),
                  pl.BlockSpec((n_tiles, n_sel, tk), lambda b, g, i: (0, 0, 0))],
        out_specs=pl.BlockSpec((qb, NSA_REP * hd), lambda b, g, i: (b * nq + i, g)),
        out_shape=jax.ShapeDtypeStruct((n_batch * t_len, NSA_HEADS * hd), BF16),
        scratch_shapes=[pltpu.VMEM((NSA_REP * qb, 1), F32), pltpu.VMEM((NSA_REP * qb, 2 * hd), F32)],
        compiler_params=_params("arbitrary", "arbitrary", "arbitrary"),
        name="nsa_attend",
    )(qr, o_cmp, selb, gates, ks, vs, kw, vw, e)


def _gla_kernel(q_ref, k_ref, v_ref, r_ref, small_ref, w2_ref, b2_ref, ng_ref, s0_ref, y_ref, sf_ref, s_scr):
    tb = q_ref.shape[0]
    c, sub, dk, dv = GLA_CHUNK, GLA_SUB, GLA_DK, GLA_DV
    n_sub = c // sub
    t = pl.program_id(1)

    @pl.when(t == 0)
    def _():
        s_scr[...] = s0_ref[0]

    z = jnp.dot(small_ref[:, 0:GLA_RANK].astype(BF16), w2_ref[...].astype(BF16),
                preferred_element_type=F32) + b2_ref[...]
    g_all = (jnp.minimum(z, 0.0) - jnp.log1p(jnp.exp(-jnp.abs(z)))) / GLA_GATE_NORM
    ri = lax.broadcasted_iota(jnp.int32, (c, c), 0)
    ci = lax.broadcasted_iota(jnp.int32, (c, c), 1)
    tril = ri >= ci
    cum = tril.astype(F32)
    rsub = lax.broadcasted_iota(jnp.int32, (c, dk), 0) // sub
    eye = lax.broadcasted_iota(jnp.int32, (dk, dk), 0) == lax.broadcasted_iota(jnp.int32, (dk, dk), 1)
    for cc in range(tb // c):
        rows = slice(cc * c, (cc + 1) * c)
        b_all = jnp.dot(cum, g_all[rows], preferred_element_type=F32, precision=lax.Precision.HIGHEST)
        for h in range(GLA_HEADS):
            b = b_all[:, h * dk:(h + 1) * dk]
            qh = q_ref[rows, h * dk:(h + 1) * dk] * (dk ** -0.5)
            kh = k_ref[rows, h * dk:(h + 1) * dk]
            vh = v_ref[rows, h * dv:(h + 1) * dv]
            a_rows = []
            for i in range(n_sub):
                ref = b[sub * i - 1:sub * i, :] if i else jnp.zeros((1, dk), F32)
                rs = slice(sub * i, sub * (i + 1))
                qi = (qh[rs] * jnp.exp(b[rs] - ref)).astype(BF16)
                ki = jnp.where(rsub <= i, kh * jnp.exp(ref - b), 0.0).astype(BF16)
                a_rows.append(lax.dot_general(qi, ki, (((1,), (1,)), ((), ())), preferred_element_type=F32))
            a = jnp.where(tril, jnp.concatenate(a_rows, axis=0), 0.0)
            s_old = s_scr[h]
            o = jnp.dot(a.astype(BF16), vh.astype(BF16), preferred_element_type=F32)
            o += jnp.dot((qh * jnp.exp(b)).astype(BF16), s_old.astype(BF16), preferred_element_type=F32)
            b_last = b[c - 1:c, :]
            ke = (kh * jnp.exp(b_last - b)).astype(BF16)
            upd = lax.dot_general(ke, vh.astype(BF16), (((0,), (0,)), ((), ())), preferred_element_type=F32)
            decay = jnp.exp(jnp.sum(jnp.where(eye, jnp.broadcast_to(b_last, (dk, dk)), 0.0), axis=1, keepdims=True))
            s_scr[h] = decay * s_old + upd
            o = o * lax.rsqrt(jnp.mean(o * o, axis=-1, keepdims=True) + RMS_EPS)
            y = o * ng_ref[:, h * dv:(h + 1) * dv] * jax.nn.silu(r_ref[rows, h * dv:(h + 1) * dv])
            y_ref[rows, h * dv:(h + 1) * dv] = y.astype(BF16)

    @pl.when(t == pl.num_programs(1) - 1)
    def _():
        sf_ref[0] = s_scr[...]


def gla_mix(p, s0, w2, b2, norm_g, n_batch, t_len):
    tb = _pick_tile(t_len, (256, 128, 64))
    nt = t_len // tb
    row = lambda w, off: pl.BlockSpec((tb, w), lambda b, i: (b * nt + i, off // w))
    const = lambda shape: pl.BlockSpec(shape, lambda b, i: (0,) * len(shape))
    return pl.pallas_call(
        _gla_kernel,
        grid=(n_batch, nt),
        in_specs=[row(256, COL_GLA_Q), row(256, COL_GLA_K), row(512, COL_GLA_V), row(512, COL_GLA_R),
                  row(128, COL_SMALL), const((GLA_RANK, GLA_HEADS * GLA_DK)), const((1, GLA_HEADS * GLA_DK)),
                  const((1, GLA_WIDTH)),
                  pl.BlockSpec((1, GLA_HEADS, GLA_DK, GLA_DV), lambda b, i: (b, 0, 0, 0))],
        out_specs=(pl.BlockSpec((tb, GLA_WIDTH), lambda b, i: (b * nt + i, 0)),
                   pl.BlockSpec((1, GLA_HEADS, GLA_DK, GLA_DV), lambda b, i: (b, 0, 0, 0))),
        out_shape=(jax.ShapeDtypeStruct((n_batch * t_len, GLA_WIDTH), BF16),
                   jax.ShapeDtypeStruct((n_batch, GLA_HEADS, GLA_DK, GLA_DV), F32)),
        scratch_shapes=[pltpu.VMEM((GLA_HEADS, GLA_DK, GLA_DV), F32)],
        compiler_params=_params("arbitrary", "arbitrary"),
        name="gla_mix",
    )(p, p, p, p, p, w2, b2.reshape(1, -1), norm_g.reshape(1, -1), s0)


def _pool_kernel(u_ref, prev_ref, cnt_ref, w_ref, sc_ref, y_ref, halo):
    tb = u_ref.shape[0]
    gd = POOL_GROUP_DIM

    @pl.when(pl.program_id(1) == 0)
    def _():
        halo[...] = prev_ref[0]

    ext = jnp.concatenate([halo[...], u_ref[...]], axis=0)
    halo[...] = ext[tb:tb + POOL_MAX]
    for gi, w in enumerate(POOL_WINDOWS):
        x = ext[:, gi * gd:(gi + 1) * gd]
        s = x
        shift = 1
        while shift < w:
            s = s + pltpu.roll(s, shift, 0)
            shift *= 2
        pooled = s[POOL_MAX:] / cnt_ref[:, gi:gi + 1] - x[POOL_MAX:]
        y = jnp.dot(pooled.astype(BF16), w_ref[gi].astype(BF16), preferred_element_type=F32)
        y_ref[:, gi * gd:(gi + 1) * gd] = (y * sc_ref[:, gi * gd:(gi + 1) * gd]).astype(BF16)


def pool_mix(p, prev, pos0, w_pool, scale, n_batch, t_len):
    tb = _pick_tile(t_len, (512, 256, 128, 64, 32, 16))
    nt = t_len // tb
    pos = pos0 + jnp.arange(t_len, dtype=jnp.int32)
    cnt = jnp.stack([jnp.minimum(pos + 1, w).astype(F32) for w in POOL_WINDOWS], axis=1)
    cnt = jnp.pad(cnt, ((0, 0), (0, 128 - POOL_GROUPS)), constant_values=1.0)
    prev16 = jnp.pad(prev.astype(F32), ((0, 0), (1, 0), (0, 0)))
    return pl.pallas_call(
        _pool_kernel,
        grid=(n_batch, nt),
        in_specs=[pl.BlockSpec((tb, POOL_WIDTH), lambda b, i: (b * nt + i, COL_POOL // POOL_WIDTH)),
                  pl.BlockSpec((1, POOL_MAX, POOL_WIDTH), lambda b, i: (b, 0, 0)),
                  pl.BlockSpec((tb, 128), lambda b, i: (i, 0)),
                  pl.BlockSpec((POOL_GROUPS, POOL_GROUP_DIM, POOL_GROUP_DIM), lambda b, i: (0, 0, 0)),
                  pl.BlockSpec((1, POOL_WIDTH), lambda b, i: (0, 0))],
        out_specs=pl.BlockSpec((tb, POOL_WIDTH), lambda b, i: (b * nt + i, 0)),
        out_shape=jax.ShapeDtypeStruct((n_batch * t_len, POOL_WIDTH), BF16),
        scratch_shapes=[pltpu.VMEM((POOL_MAX, POOL_WIDTH), F32)],
        compiler_params=_params("arbitrary", "arbitrary"),
        name="pool_mix",
    )(p, prev16, cnt, w_pool, scale.reshape(1, -1))


def _layer_norm_rows(x, g, b):
    xc = x - jnp.mean(x, axis=-1, keepdims=True)
    var = jnp.mean(xc * xc, axis=-1, keepdims=True)
    return xc * lax.rsqrt(var + LN_EPS) * g + b


def _outproj_kernel(x_ref, yg_ref, yn_ref, yp_ref, w_ref, g_ref, b_ref, *rest):
    h = jnp.dot(yg_ref[...], w_ref[0:GLA_WIDTH, :], preferred_element_type=F32)
    h += jnp.dot(yn_ref[...], w_ref[GLA_WIDTH:GLA_WIDTH + NSA_WIDTH, :], preferred_element_type=F32)
    h += jnp.dot(yp_ref[...], w_ref[GLA_WIDTH + NSA_WIDTH:, :], preferred_element_type=F32)
    x1 = _layer_norm_rows(ALPHA * x_ref[...] + h, g_ref[...], b_ref[...])
    if len(rest) == 1:
        rest[0][...] = x1
    else:
        rt_ref, o_ref, lg_ref = rest
        o_ref[...] = x1
        lg_ref[...] = jnp.dot(x1, rt_ref[...], preferred_element_type=F32, precision=lax.Precision.HIGHEST)


def router_pad(router):
    return jnp.pad(router, ((0, 0), (0, 128 - router.shape[1])))


def outproj_ln(x, y_gla, y_nsa, y_pool, w_out_bf16, g, b, router=None):
    n, d = x.shape
    tm = _pick_tile(n, (512, 256, 128, 64, 32, 16, 8))
    row = lambda w: pl.BlockSpec((tm, w), lambda i: (i, 0))
    const = lambda r, c: pl.BlockSpec((r, c), lambda i: (0, 0))
    in_specs = [row(d), row(GLA_WIDTH), row(NSA_WIDTH), row(POOL_WIDTH), const(d, d), const(1, d), const(1, d)]
    args = [x, y_gla, y_nsa, y_pool, w_out_bf16, g.reshape(1, -1), b.reshape(1, -1)]
    out_specs, out_shape = row(d), jax.ShapeDtypeStruct((n, d), F32)
    if router is not None:
        in_specs.append(const(d, 128))
        args.append(router_pad(router))
        out_specs, out_shape = (out_specs, row(128)), (out_shape, jax.ShapeDtypeStruct((n, 128), F32))
    return pl.pallas_call(
        _outproj_kernel,
        grid=(n // tm,),
        in_specs=in_specs,
        out_specs=out_specs,
        out_shape=out_shape,
        compiler_params=_params("arbitrary"),
        name="outproj_ln",
    )(*args)


MOE_TM = 256
MOE_TF = 1024
MOE_TN = 512
ROUTE_TM = 512
PERMUTE_CHUNK = 1024


def _route_kernel(lg_ref, ii_ref, gf_ref, cnt_ref, carry, *, n_valid):
    tm = lg_ref.shape[0]
    i = pl.program_id(0)

    @pl.when(i == 0)
    def _():
        carry[...] = jnp.zeros_like(carry)

    lane = lax.broadcasted_iota(jnp.int32, (tm, 128), 1)
    valid = (i * tm + lax.broadcasted_iota(jnp.int32, (tm, 128), 0)) < n_valid
    lg = jnp.where(lane < N_EXPERTS, lg_ref[...], -jnp.inf)
    m1 = jnp.max(lg, axis=-1, keepdims=True)
    i1 = jnp.min(jnp.where(lg == m1, lane, 128), axis=-1, keepdims=True)
    lg2 = jnp.where(lane == i1, -jnp.inf, lg)
    m2 = jnp.max(lg2, axis=-1, keepdims=True)
    i2 = jnp.min(jnp.where(lg2 == m2, lane, 128), axis=-1, keepdims=True)
    t = jnp.exp(m2 - m1)
    g1 = 1.0 / (1.0 + t)
    g2 = t / (1.0 + t)
    oh1 = jnp.where((lane == i1) & valid, 1.0, 0.0)
    oh2 = jnp.where((lane == i2) & valid, 1.0, 0.0)
    cnt = oh1 + oh2
    strict = (lax.broadcasted_iota(jnp.int32, (tm, tm), 0) > lax.broadcasted_iota(jnp.int32, (tm, tm), 1))
    before = jnp.dot(strict.astype(BF16), cnt.astype(BF16), preferred_element_type=F32) + carry[...]
    r1 = jnp.sum(before * oh1, axis=-1, keepdims=True).astype(jnp.int32)
    r2 = jnp.sum(before * oh2, axis=-1, keepdims=True).astype(jnp.int32)
    carry[...] += jnp.sum(cnt, axis=0, keepdims=True)
    ii_ref[...] = jnp.where(lane == 0, i1, jnp.where(lane == 1, i2, jnp.where(lane == 2, r1, r2)))
    gf_ref[...] = jnp.where(lane == 0, g1, g2)
    cnt_ref[...] = carry[...]


def moe_route(logits, n_valid):
    npad = logits.shape[0]
    tm = ROUTE_TM
    row = pl.BlockSpec((tm, 128), lambda i: (i, 0))
    info, gates, counts = pl.pallas_call(
        functools.partial(_route_kernel, n_valid=n_valid),
        grid=(npad // tm,),
        in_specs=[row],
        out_specs=(row, row, pl.BlockSpec((1, 128), lambda i: (0, 0))),
        out_shape=(jax.ShapeDtypeStruct((npad, 128), jnp.int32), jax.ShapeDtypeStruct((npad, 128), F32),
                   jax.ShapeDtypeStruct((1, 128), F32)),
        scratch_shapes=[pltpu.VMEM((1, 128), F32)],
        compiler_params=_params("arbitrary"),
        name="moe_route",
    )(logits)
    return info[:, 0:2], info[:, 2:4], gates, counts[0, :N_EXPERTS].astype(jnp.int32)


SLAB = (16, 128)


def _slabify_kernel(x_ref, o_ref):
    for c in range(SLAB[0]):
        o_ref[:, c, :] = x_ref[:, c * SLAB[1]:(c + 1) * SLAB[1]]


def slabify(x):
    n, d = x.shape
    tm = _pick_tile(n, (512, 256, 128, 64, 32, 16, 8))
    return pl.pallas_call(
        _slabify_kernel,
        grid=(n // tm,),
        in_specs=[pl.BlockSpec((tm, d), lambda i: (i, 0))],
        out_specs=pl.BlockSpec((tm,) + SLAB, lambda i: (i, 0, 0)),
        out_shape=jax.ShapeDtypeStruct((n,) + SLAB, x.dtype),
        compiler_params=_params("arbitrary"),
        name="slabify",
    )(x)


def _unslab(ref):
    return jnp.concatenate([ref[:, c, :] for c in range(SLAB[0])], axis=1)


def _gather_slabs_kernel(idx_ref, a_ref, b_ref, o_ref, sem, *, n_a):
    ch = idx_ref.shape[2]

    def issue(r, c):
        j = idx_ref[0, 0, r]

        @pl.when(j < n_a)
        def _():
            pltpu.make_async_copy(a_ref.at[j], o_ref.at[r], sem).start()

        @pl.when(j >= n_a)
        def _():
            pltpu.make_async_copy(b_ref.at[j - n_a], o_ref.at[r], sem).start()

        return c

    lax.fori_loop(0, ch, issue, 0)

    def drain(r, c):
        pltpu.make_async_copy(a_ref.at[0], o_ref.at[0], sem).wait()
        return c

    lax.fori_loop(0, ch, drain, 0)


def gather_slabs(src_a, src_b, idx):
    n = idx.shape[0]
    ch = PERMUTE_CHUNK
    return pl.pallas_call(
        functools.partial(_gather_slabs_kernel, n_a=src_a.shape[0]),
        grid=(n // ch,),
        in_specs=[pl.BlockSpec((1, 1, ch), lambda i: (i, 0, 0), memory_space=pltpu.SMEM),
                  pl.BlockSpec(memory_space=pl.ANY), pl.BlockSpec(memory_space=pl.ANY)],
        out_specs=pl.BlockSpec((ch,) + SLAB, lambda i: (i, 0, 0)),
        out_shape=jax.ShapeDtypeStruct((n,) + SLAB, src_a.dtype),
        scratch_shapes=[pltpu.SemaphoreType.DMA(())],
        compiler_params=_params("arbitrary"),
        name="gather_slabs",
    )(idx.reshape(n // ch, 1, ch), src_a, src_b)


def _moe_up_kernel(te_ref, tfirst_ref, tused_ref, x_ref, wg_ref, wu_ref, h_ref, wgb, wub):
    i = pl.program_id(1)

    @pl.when((i == 0) | (tfirst_ref[i] == 1))
    def _():
        wgb[...] = wg_ref[...].astype(BF16)
        wub[...] = wu_ref[...].astype(BF16)

    @pl.when(tused_ref[i] == 1)
    def _():
        xb = _unslab(x_ref).astype(BF16)
        gate = jnp.dot(xb, wgb[...], preferred_element_type=F32)
        up = jnp.dot(xb, wub[...], preferred_element_type=F32)
        h_ref[...] = (jax.nn.silu(gate) * up).astype(BF16)

    @pl.when(tused_ref[i] == 0)
    def _():
        h_ref[...] = jnp.zeros_like(h_ref)


def _moe_down_kernel(te_ref, tfirst_ref, tused_ref, h_ref, wd_ref, y_ref, wdb):
    i = pl.program_id(1)

    @pl.when((i == 0) | (tfirst_ref[i] == 1))
    def _():
        wdb[...] = wd_ref[...].astype(BF16)

    @pl.when(tused_ref[i] == 1)
    def _():
        y_ref[...] = jnp.dot(h_ref[...], wdb[...], preferred_element_type=F32)

    @pl.when(tused_ref[i] == 0)
    def _():
        y_ref[...] = jnp.zeros_like(y_ref)


def moe_experts(xs, tile_e, tile_first, tile_used, wg, wu, wd):
    r = xs.shape[0]
    d, d_ff = wg.shape[1], wg.shape[2]
    tm, tf, tn = MOE_TM, MOE_TF, MOE_TN
    n_tiles = r // tm
    h = pl.pallas_call(
        _moe_up_kernel,
        grid_spec=pltpu.PrefetchScalarGridSpec(
            num_scalar_prefetch=3,
            grid=(d_ff // tf, n_tiles),
            in_specs=[pl.BlockSpec((tm,) + SLAB, lambda j, i, te, t1, tu: (i, 0, 0)),
                      pl.BlockSpec((None, d, tf), lambda j, i, te, t1, tu: (te[i], 0, j)),
                      pl.BlockSpec((None, d, tf), lambda j, i, te, t1, tu: (te[i], 0, j))],
            out_specs=pl.BlockSpec((tm, tf), lambda j, i, te, t1, tu: (i, j)),
            scratch_shapes=[pltpu.VMEM((d, tf), BF16), pltpu.VMEM((d, tf), BF16)]),
        out_shape=jax.ShapeDtypeStruct((r, d_ff), BF16),
        compiler_params=_params("arbitrary", "arbitrary"),
        name="moe_up",
    )(tile_e, tile_first, tile_used, xs, wg, wu)
    return pl.pallas_call(
        _moe_down_kernel,
        grid_spec=pltpu.PrefetchScalarGridSpec(
            num_scalar_prefetch=3,
            grid=(d // tn, n_tiles),
            in_specs=[pl.BlockSpec((tm, d_ff), lambda j, i, te, t1, tu: (i, 0)),
                      pl.BlockSpec((None, d_ff, tn), lambda j, i, te, t1, tu: (te[i], 0, j))],
            out_specs=pl.BlockSpec((tm, tn), lambda j, i, te, t1, tu: (i, j)),
            scratch_shapes=[pltpu.VMEM((d_ff, tn), BF16)]),
        out_shape=jax.ShapeDtypeStruct((r, d), F32),
        compiler_params=_params("arbitrary", "arbitrary"),
        name="moe_down",
    )(tile_e, tile_first, tile_used, h, wd)


def _moe_combine_kernel(x_ref, y0_ref, y1_ref, gt_ref, g_ref, b_ref, o_ref):
    gt = gt_ref[...]
    y = gt[:, 0:1] * _unslab(y0_ref) + gt[:, 1:2] * _unslab(y1_ref)
    o_ref[...] = _layer_norm_rows(ALPHA * x_ref[...] + y, g_ref[...], b_ref[...])


def moe_combine_ln(x, yg, gates, row0, n_tok_pad, g, b):
    n, d = x.shape
    tm = _pick_tile(n, (512, 256, 128, 64, 32, 16, 8))
    o0, o1 = row0 // tm, (n_tok_pad + row0) // tm
    return pl.pallas_call(
        _moe_combine_kernel,
        grid=(n // tm,),
        in_specs=[pl.BlockSpec((tm, d), lambda i: (i, 0)),
                  pl.BlockSpec((tm,) + SLAB, lambda i: (o0 + i, 0, 0)),
                  pl.BlockSpec((tm,) + SLAB, lambda i: (o1 + i, 0, 0)),
                  pl.BlockSpec((tm, 128), lambda i: (o0 + i, 0)),
                  pl.BlockSpec((1, d), lambda i: (0, 0)), pl.BlockSpec((1, d), lambda i: (0, 0))],
        out_specs=pl.BlockSpec((tm, d), lambda i: (i, 0)),
        out_shape=jax.ShapeDtypeStruct((n, d), F32),
        compiler_params=_params("arbitrary"),
        name="moe_combine_ln",
    )(x, yg, yg, gates, g.reshape(1, -1), b.reshape(1, -1))


def moe_ln(x_groups, logit_groups, wg, wu, wd, g, b):
    d = x_groups[0].shape[1]
    n_tok = sum(x.shape[0] for x in x_groups)
    n_tok_pad = -(-n_tok // PERMUTE_CHUNK) * PERMUTE_CHUNK
    n_tok_pad = -(-n_tok_pad // ROUTE_TM) * ROUTE_TM
    logits = jnp.concatenate(logit_groups + [jnp.zeros((n_tok_pad - n_tok, 128), F32)], axis=0)
    experts, ranks, gates, counts = moe_route(logits, n_tok)
    tm = MOE_TM
    n_tiles = -(-(n_tok * TOP_K + N_EXPERTS * (tm - 1)) // tm)
    n_tiles = -(-n_tiles * tm // PERMUTE_CHUNK) * PERMUTE_CHUNK // tm
    padded = (counts + tm - 1) // tm * tm
    pad_end = jnp.cumsum(padded)
    pad_start = pad_end - padded
    valid = (jnp.arange(n_tok_pad) < n_tok)[:, None]
    dest = jnp.where(valid, pad_start[experts] + ranks, 0)
    tok = jnp.broadcast_to(jnp.arange(n_tok_pad, dtype=jnp.int32)[:, None], dest.shape)
    row_tok = jnp.zeros((n_tiles * tm,), jnp.int32).at[jnp.where(valid, dest, n_tiles * tm).reshape(-1)].set(
        tok.reshape(-1), mode='drop')
    tile_start = jnp.arange(n_tiles, dtype=jnp.int32) * tm
    tile_e = jnp.minimum(jnp.searchsorted(pad_end, tile_start, side='right'), N_EXPERTS - 1).astype(jnp.int32)
    tile_used = (tile_start < pad_end[-1]).astype(jnp.int32)
    tile_first = jnp.concatenate([jnp.ones((1,), jnp.int32), (tile_e[1:] != tile_e[:-1]).astype(jnp.int32)])
    assert len(x_groups) == 2
    xs = gather_slabs(slabify(x_groups[0]), x_groups[1].reshape((-1,) + SLAB), row_tok)
    ys = slabify(moe_experts(xs, tile_e, tile_first, tile_used, wg, wu, wd))
    yg = gather_slabs(ys, ys, jnp.concatenate([dest[:, 0], dest[:, 1]]).astype(jnp.int32))
    outs, row0 = [], 0
    for x in x_groups:
        outs.append(moe_combine_ln(x, yg, gates, row0, n_tok_pad, g, b))
        row0 += x.shape[0]
    return outs


def _ffn_kernel(x_ref, wg_ref, wu_ref, wd_ref, g_ref, b_ref, o_ref, xb_ref, *, d_ff):
    j = pl.program_id(1)
    tf = wg_ref.shape[1]

    @pl.when(j == 0)
    def _():
        xb_ref[...] = x_ref[...].astype(BF16)
        o_ref[...] = jnp.zeros_like(o_ref)

    xb = xb_ref[...]
    gate = jnp.dot(xb, wg_ref[...], preferred_element_type=F32)
    up = jnp.dot(xb, wu_ref[...], preferred_element_type=F32)
    col = j * tf + lax.broadcasted_iota(jnp.int32, (1, tf), 1)
    a = jnp.where(col < d_ff, jax.nn.silu(gate) * up, 0.0).astype(BF16)
    rowi = j * tf + lax.broadcasted_iota(jnp.int32, (tf, 1), 0)
    wd = jnp.where(rowi < d_ff, wd_ref[...], jnp.zeros((), BF16))
    o_ref[...] += jnp.dot(a, wd, preferred_element_type=F32)

    @pl.when(j == pl.num_programs(1) - 1)
    def _():
        o_ref[...] = _layer_norm_rows(ALPHA * x_ref[...] + o_ref[...], g_ref[...], b_ref[...])


def ffn_ln(x, wg, wu, wd, g, b):
    n, d = x.shape
    d_ff = wg.shape[1]
    tm = _pick_tile(n, (512, 256, 128, 64, 32, 16, 8))
    tf = 512
    kern = functools.partial(_ffn_kernel, d_ff=d_ff)
    return pl.pallas_call(
        kern,
        grid=(n // tm, pl.cdiv(d_ff, tf)),
        in_specs=[pl.BlockSpec((tm, d), lambda i, j: (i, 0)),
                  pl.BlockSpec((d, tf), lambda i, j: (0, j)), pl.BlockSpec((d, tf), lambda i, j: (0, j)),
                  pl.BlockSpec((tf, d), lambda i, j: (j, 0)),
                  pl.BlockSpec((1, d), lambda i, j: (0, 0)), pl.BlockSpec((1, d), lambda i, j: (0, 0))],
        out_specs=pl.BlockSpec((tm, d), lambda i, j: (i, 0)),
        out_shape=jax.ShapeDtypeStruct((n, d), F32),
        scratch_shapes=[pltpu.VMEM((tm, d), BF16)],
        compiler_params=_params("arbitrary", "arbitrary"),
        name="ffn_ln",
    )(x, wg, wu, wd, g.reshape(1, -1), b.reshape(1, -1))


CMP_PAGES = 32
SEL_PAD_DEC = 384


def _cmp_paged_kernel(pt_ref, cache_ref, pe_ref, w1_ref, w2_ref, o_ref, buf, hlo, hhi, sem, *, layer, n_pages, page):
    b, ch = pl.program_id(0), pl.program_id(1)
    hd = HEAD_DIM
    nh = CMP_PAGES * page // CMP_STRIDE

    def copy(pg, c):
        src = cache_ref.at[layer, pt_ref[b * n_pages + ch * CMP_PAGES + pg], :, c // NSA_KV_HEADS, c % NSA_KV_HEADS, :]
        return pltpu.make_async_copy(src, buf.at[c, pl.ds(pg * page, page), :], sem)

    for pg in range(CMP_PAGES):
        for c in range(2 * NSA_KV_HEADS):
            copy(pg, c).start()
    for pg in range(CMP_PAGES):
        for c in range(2 * NSA_KV_HEADS):
            copy(pg, c).wait()

    for c in range(2 * NSA_KV_HEADS):
        kd, g = c // NSA_KV_HEADS, c % NSA_KV_HEADS
        xs = [buf[c, pl.ds(j, nh, stride=CMP_STRIDE), :] for j in range(CMP_STRIDE)]
        lo = jnp.concatenate([(xs[j] + pe_ref[kd, j:j + 1, :]).astype(BF16) for j in range(CMP_STRIDE)], axis=1)
        hi = jnp.concatenate([(xs[j] + pe_ref[kd, CMP_STRIDE + j:CMP_STRIDE + j + 1, :]).astype(BF16)
                              for j in range(CMP_STRIDE)], axis=1)
        w_lo = w1_ref[kd, 0:CMP_STRIDE].reshape(CMP_STRIDE * hd, CMP_HIDDEN).astype(BF16)
        w_hi = w1_ref[kd, CMP_STRIDE:CMP_LEN].reshape(CMP_STRIDE * hd, CMP_HIDDEN).astype(BF16)
        rows = pl.ds(pl.multiple_of(ch * nh, nh), nh)
        hlo[c, rows, :] = jnp.dot(lo, w_lo, preferred_element_type=F32)
        hhi[c, rows, :] = jnp.dot(hi, w_hi, preferred_element_type=F32)

    @pl.when(ch == pl.num_programs(1) - 1)
    def _():
        n_all = hlo.shape[1]
        for c in range(2 * NSA_KV_HEADS):
            kd, g = c // NSA_KV_HEADS, c % NSA_KV_HEADS
            h = jax.nn.gelu(hlo[c] + pltpu.roll(hhi[c], n_all - 1, 0))
            o_ref[kd, g] = jnp.dot(h.astype(BF16), w2_ref[kd].astype(BF16), preferred_element_type=F32).astype(BF16)


def nsa_compress_paged(cache, layer, page_table, cmp_pos, cmp_w1, cmp_w2):
    page = cache.shape[2]
    n_batch, n_pages = page_table.shape
    hd = HEAD_DIM
    n_all = n_pages * page // CMP_STRIDE
    kern = functools.partial(_cmp_paged_kernel, layer=layer, n_pages=n_pages, page=page)
    const = lambda shape: pl.BlockSpec(shape, lambda b, c, pt: (0,) * len(shape))
    return pl.pallas_call(
        kern,
        grid_spec=pltpu.PrefetchScalarGridSpec(
            num_scalar_prefetch=1,
            grid=(n_batch, n_pages // CMP_PAGES),
            in_specs=[pl.BlockSpec(memory_space=pl.ANY), const((2, CMP_LEN, hd)),
                      const((2, CMP_LEN, hd, CMP_HIDDEN)), const((2, CMP_HIDDEN, hd))],
            out_specs=pl.BlockSpec((None, 2, NSA_KV_HEADS, n_all, hd), lambda b, c, pt: (b, 0, 0, 0, 0)),
            scratch_shapes=[pltpu.VMEM((2 * NSA_KV_HEADS, CMP_PAGES * page, hd), F32),
                            pltpu.VMEM((2 * NSA_KV_HEADS, n_all, CMP_HIDDEN), F32),
                            pltpu.VMEM((2 * NSA_KV_HEADS, n_all, CMP_HIDDEN), F32),
                            pltpu.SemaphoreType.DMA(())]),
        out_shape=jax.ShapeDtypeStruct((n_batch, 2, NSA_KV_HEADS, n_all, hd), BF16),
        compiler_params=_params("arbitrary", "arbitrary"),
        name="nsa_compress_paged",
    )(page_table.reshape(-1).astype(jnp.int32), cache, cmp_pos, cmp_w1, cmp_w2)


def _sel_decode_kernel(qn_ref, kc_ref, vc_ref, ovt_ref, oc_ref, idx_ref, v_scr, *, n_cmp, n_sel, n_top, q_pos):
    hd = HEAD_DIM
    n_cmp_pad = kc_ref.shape[2]
    nsp = ovt_ref.shape[0]
    n_rows = qn_ref.shape[0]
    col = lax.broadcasted_iota(jnp.int32, (n_rows, n_cmp_pad), 1)
    mask = (col * CMP_STRIDE + (CMP_LEN - 1) <= q_pos) & (col < n_cmp)
    psums = []
    for g in range(NSA_KV_HEADS):
        own = slice(g * NSA_REP, (g + 1) * NSA_REP)
        s = lax.dot_general(qn_ref[...], kc_ref[0, g], (((1,), (1,)), ((), ())), preferred_element_type=F32)
        s = jnp.where(mask, s, -jnp.inf)
        m = jnp.max(s, axis=-1, keepdims=True)
        m = jnp.where(m > -jnp.inf, m, 0.0)
        p = jnp.where(mask, jnp.exp(s - m), 0.0)
        p = p / jnp.maximum(jnp.sum(p, axis=-1, keepdims=True), 1e-30)
        oc = jnp.dot(p.astype(BF16), vc_ref[0, g], preferred_element_type=F32)
        oc_ref[own, :] = oc[own]
        psums.append(jnp.sum(p[own], axis=0, keepdims=True))
    psum = jnp.concatenate(psums + [jnp.zeros((8 - NSA_KV_HEADS, n_cmp_pad), F32)], axis=0)
    imp = lax.dot_general(ovt_ref[...], psum, (((1,), (1,)), ((), ())), preferred_element_type=F32,
                          precision=lax.Precision.HIGHEST)
    blk = lax.broadcasted_iota(jnp.int32, (nsp, 8), 0)
    cur = q_pos // SEL_BLOCK
    forced = (blk == 0) | (blk == cur) | (blk == cur - 1)
    v = jnp.where((blk <= cur) & (blk < n_sel), jnp.where(forced, jnp.inf, imp), -jnp.inf)
    v_scr[...] = v

    def count(i, rank):
        vi = v_scr[pl.ds(i, 1), :]
        ahead = (vi > v) | ((vi == v) & (blk > i))
        return rank + ahead.astype(jnp.int32)

    rank = lax.fori_loop(0, n_sel, count, jnp.zeros((nsp, 8), jnp.int32))
    chosen = (rank < n_top) & (v > -jnp.inf)
    blk_f = blk.astype(F32)
    rows = [jnp.sum(jnp.where(chosen & (rank == t), blk_f, 0.0), axis=0, keepdims=True) for t in range(n_top)]
    idx_ref[...] = jnp.concatenate(rows, axis=0).astype(jnp.int32)


def nsa_select_decode(qn, cmp_kv, n_cmp, n_sel, q_pos):
    n_batch, n_heads, hd = qn.shape
    n_cmp_pad = cmp_kv.shape[3]
    nsp = SEL_PAD_DEC
    n_top = min(SEL_TOPN, n_sel)
    ci = jnp.arange(n_cmp_pad)[None, :]
    sj = jnp.arange(nsp)[:, None]
    overlap_t = ((ci * CMP_STRIDE <= sj * SEL_BLOCK + SEL_BLOCK - 1) &
                 (ci * CMP_STRIDE + CMP_LEN - 1 >= sj * SEL_BLOCK) & (ci < n_cmp) & (sj < n_sel)).astype(F32)
    kern = functools.partial(_sel_decode_kernel, n_cmp=n_cmp, n_sel=n_sel, n_top=n_top, q_pos=q_pos)
    kv = lambda kd: pl.BlockSpec((None, 1, NSA_KV_HEADS, n_cmp_pad, hd), lambda b: (b, kd, 0, 0, 0))
    return pl.pallas_call(
        kern,
        grid=(n_batch,),
        in_specs=[pl.BlockSpec((None, 2 * n_heads, hd), lambda b: (b, 0, 0)), kv(0), kv(1),
                  pl.BlockSpec((nsp, n_cmp_pad), lambda b: (0, 0))],
        out_specs=(pl.BlockSpec((None, n_heads, hd), lambda b: (b, 0, 0)),
                   pl.BlockSpec((None, n_top, 8), lambda b: (b, 0, 0))),
        out_shape=(jax.ShapeDtypeStruct((n_batch, n_heads, hd), F32),
                   jax.ShapeDtypeStruct((n_batch, n_top, 8), jnp.int32)),
        scratch_shapes=[pltpu.VMEM((nsp, 8), F32)],
        compiler_params=_params("arbitrary"),
        name="nsa_select_decode",
    )(jnp.pad(qn, ((0, 0), (0, n_heads), (0, 0))), cmp_kv, cmp_kv, overlap_t)


def _attn_decode_kernel(pt_ref, sel_ref, q_ref, k0_ref, v0_ref, k1_ref, v1_ref, new_ref, kw_ref, vw_ref, wnew_ref,
                        oc_ref, gate_ref, o_ref, m_scr, l_scr, acc_scr, *, n_past_blocks, n_top):
    b, slot = pl.program_id(0), pl.program_id(1)
    G = NSA_KV_HEADS
    nt = (((1,), (1,)), ((), ()))
    kv_refs = ((k0_ref, v0_ref), (k1_ref, v1_ref))
    for g in range(G):
        qb = q_ref[g]
        q = qb.astype(F32)
        kc_ref, vc_ref = kv_refs[g]

        @pl.when(slot == 0)
        def _():
            k_new = new_ref[2 * G + g:2 * G + g + 1, :]
            v_new = new_ref[3 * G + g:3 * G + g + 1, :]
            m_scr[g] = jnp.sum(q * k_new, axis=-1, keepdims=True)
            l_scr[g] = jnp.ones(l_scr.shape[1:], F32)
            acc_scr[g] = jnp.broadcast_to(v_new, acc_scr.shape[1:])

        @pl.when(sel_ref[(b * G + g) * n_top + slot] < n_past_blocks)
        def _():
            s = lax.dot_general(qb, kc_ref[:, g, :].astype(BF16), nt, preferred_element_type=F32)
            m_old = m_scr[g]
            m_new = jnp.maximum(m_old, jnp.max(s, axis=-1, keepdims=True))
            alpha = jnp.exp(m_old - m_new)
            p = jnp.exp(s - m_new)
            l_scr[g] = alpha * l_scr[g] + jnp.sum(p, axis=-1, keepdims=True)
            acc_scr[g] = alpha * acc_scr[g] + jnp.dot(p.astype(BF16), vc_ref[:, g, :].astype(BF16),
                                                      preferred_element_type=F32)
            m_scr[g] = m_new

        @pl.when(slot == n_top - 1)
        def _():
            o_sel = acc_scr[g] / jnp.maximum(l_scr[g], 1e-30)
            n_buf = kw_ref.shape[0]
            s = lax.dot_general(qb, kw_ref[:, g, :].astype(BF16), nt, preferred_element_type=F32)
            keep = lax.broadcasted_iota(jnp.int32, s.shape, 1) > n_buf - WINDOW
            s = jnp.where(keep, s, NEG_BIG)
            kw_new = wnew_ref[g:g + 1, :]
            vw_new = wnew_ref[G + g:G + g + 1, :]
            s_new = jnp.sum(q * kw_new, axis=-1, keepdims=True)
            m = jnp.maximum(jnp.max(s, axis=-1, keepdims=True), s_new)
            p = jnp.exp(s - m)
            p_new = jnp.exp(s_new - m)
            l = jnp.sum(p, axis=-1, keepdims=True) + p_new
            o_win = (jnp.dot(p.astype(BF16), vw_ref[:, g, :].astype(BF16), preferred_element_type=F32)
                     + p_new * vw_new) / l
            n_rows = qb.shape[0]
            gates = jnp.broadcast_to(gate_ref[g:g + 1, :], (n_rows, 128))
            lane = lax.broadcasted_iota(jnp.int32, (n_rows, 128), 1)
            head = lax.broadcasted_iota(jnp.int32, (n_rows, 128), 0)
            pick = lambda c: jnp.sum(jnp.where(lane == head * 3 + c, gates, 0.0), axis=-1, keepdims=True)
            o_ref[g] = (pick(0) * oc_ref[g] + pick(1) * o_sel + pick(2) * o_win).astype(o_ref.dtype)


def nsa_attend_decode(qr, o_cmp, sel_idx, gates, cache, layer, page_table, new_rows, state_win, new_win):
    n_batch, n_heads, hd = qr.shape
    page = cache.shape[2]
    n_pages = page_table.shape[1]
    n_top = sel_idx.shape[2]
    per_page = page // SEL_BLOCK
    n_past_blocks = n_pages * per_page
    n_buf = state_win.shape[2]
    G, R = NSA_KV_HEADS, NSA_REP

    def cache_spec(kind, g):
        def index(b, s, pt, sel):
            j = jnp.minimum(sel[(b * G + g) * n_top + s], n_past_blocks - 1)
            return layer, pt[b * n_pages + j // per_page], j % per_page, kind, 0, 0
        return pl.BlockSpec((None, None, SEL_BLOCK, None, G, hd), index)

    rp = 16
    pad_heads = lambda a: jnp.pad(a.reshape(n_batch, G, R, hd), ((0, 0), (0, 0), (0, rp - R), (0, 0)))
    per_bg = lambda: pl.BlockSpec((None, G, rp, hd), lambda b, s, pt, sel: (b, 0, 0, 0))
    per_b = lambda rows: pl.BlockSpec((None, rows, hd), lambda b, s, pt, sel: (b, 0, 0))
    win_spec = lambda kv: pl.BlockSpec((None, None, n_buf, None, G, hd),
                                       lambda b, s, pt, sel: (layer, b, 0, kv, 0, 0))
    kern = functools.partial(_attn_decode_kernel, n_past_blocks=n_past_blocks, n_top=n_top)
    out = pl.pallas_call(
        kern,
        grid_spec=pltpu.PrefetchScalarGridSpec(
            num_scalar_prefetch=2,
            grid=(n_batch, n_top),
            in_specs=[per_bg(), cache_spec(2, 0), cache_spec(3, 0), cache_spec(2, 1), cache_spec(3, 1),
                      per_b(4 * G), win_spec(0), win_spec(1), per_b(2 * G), per_bg(), per_b(G)],
            out_specs=per_bg(),
            scratch_shapes=[pltpu.VMEM((G, rp, 1), F32), pltpu.VMEM((G, rp, 1), F32), pltpu.VMEM((G, rp, hd), F32)]),
        out_shape=jax.ShapeDtypeStruct((n_batch, G, rp, hd), BF16),
        compiler_params=_params("arbitrary", "arbitrary"),
        name="nsa_attend_decode",
    )(page_table.reshape(-1).astype(jnp.int32), sel_idx.reshape(-1).astype(jnp.int32),
      pad_heads(qr), cache, cache, cache, cache, new_rows, state_win, state_win, new_win, pad_heads(o_cmp), gates)
    return out[:, :, :R].reshape(n_batch, n_heads * hd)


def _gla_decode_kernel(q_ref, k_ref, v_ref, r_ref, small_ref, w2_ref, b2_ref, ng_ref, s0_ref, y_ref, sf_ref):
    dk, dv = GLA_DK, GLA_DV
    z = jnp.dot(small_ref[:, 0:GLA_RANK].astype(BF16), w2_ref[...].astype(BF16),
                preferred_element_type=F32) + b2_ref[...]
    g_all = (jnp.minimum(z, 0.0) - jnp.log1p(jnp.exp(-jnp.abs(z)))) / GLA_GATE_NORM
    eye = lax.broadcasted_iota(jnp.int32, (dk, dk), 0) == lax.broadcasted_iota(jnp.int32, (dk, dk), 1)
    column = lambda row: jnp.sum(jnp.where(eye, jnp.broadcast_to(row, (dk, dk)), 0.0), axis=1, keepdims=True)
    for b in range(q_ref.shape[0]):
        for h in range(GLA_HEADS):
            ks = slice(h * dk, (h + 1) * dk)
            vs = slice(h * dv, (h + 1) * dv)
            s_new = (jnp.exp(column(g_all[b:b + 1, ks])) * s0_ref[b, h]
                     + column(k_ref[b:b + 1, ks]) * v_ref[b:b + 1, vs])
            sf_ref[b, h] = s_new
            o = jnp.sum(column(q_ref[b:b + 1, ks] * (dk ** -0.5)) * s_new, axis=0, keepdims=True)
            o = o * lax.rsqrt(jnp.mean(o * o, axis=-1, keepdims=True) + RMS_EPS)
            y_ref[b:b + 1, vs] = (o * ng_ref[:, vs] * jax.nn.silu(r_ref[b:b + 1, vs])).astype(BF16)


def gla_decode(p, s0, w2, b2, norm_g):
    n = p.shape[0]
    row = lambda w, off: pl.BlockSpec((n, w), lambda i: (0, off // w))
    const = lambda shape: pl.BlockSpec(shape, lambda i: (0,) * len(shape))
    return pl.pallas_call(
        _gla_decode_kernel,
        grid=(1,),
        in_specs=[row(256, COL_GLA_Q), row(256, COL_GLA_K), row(512, COL_GLA_V), row(512, COL_GLA_R),
                  row(128, COL_SMALL), const((GLA_RANK, GLA_HEADS * GLA_DK)), const((1, GLA_HEADS * GLA_DK)),
                  const((1, GLA_WIDTH)), const(s0.shape)],
        out_specs=(const((n, GLA_WIDTH)), const(s0.shape)),
        out_shape=(jax.ShapeDtypeStruct((n, GLA_WIDTH), BF16), jax.ShapeDtypeStruct(s0.shape, F32)),
        compiler_params=_params("arbitrary"),
        name="gla_decode",
    )(p, p, p, p, p, w2, b2.reshape(1, -1), norm_g.reshape(1, -1), s0)


def decode_mixer(xs2, w_in_packed, w_out_bf16, ln_g, ln_b, gla_w2, gla_b, gla_norm_g, cmp_pos, cmp_w1, cmp_w2,
                 pool_w, pool_scale, cache, layer, page_table, state_win, state_gla, state_pool, past_len,
                 router=None):
    n_dec = xs2.shape[0]
    hd, G = HEAD_DIM, NSA_KV_HEADS
    p = matmul(xs2, w_in_packed)
    pos = jnp.full((n_dec,), past_len, jnp.int32)
    rows, win, qn, qr, _, _, _, _, gates = nsa_prep(p, rope_tables(pos), 1, n_dec)
    cmp_kv = nsa_compress_paged(cache, layer, page_table, cmp_pos, cmp_w1, cmp_w2)
    t_k = past_len + 1
    n_sel = -(-t_k // SEL_BLOCK)
    o_cmp, sel = nsa_select_decode(qn[0].transpose(1, 0, 2), cmp_kv, past_len // CMP_STRIDE - 1, n_sel, past_len)
    sel_idx = sel[:, :, :G].transpose(0, 2, 1)
    y_nsa = nsa_attend_decode(qr[0].transpose(1, 0, 2), o_cmp, sel_idx, gates[0].transpose(1, 0, 2), cache, layer,
                              page_table, rows.reshape(n_dec, 4 * G, hd), state_win, win.reshape(n_dec, 2 * G, hd))
    y_gla, s_gla = gla_decode(p, state_gla, gla_w2, gla_b, gla_norm_g)
    y_pool, pool_rows = pool_mixer(p[:, None, COL_POOL:COL_POOL + POOL_WIDTH], state_pool, pos[:1], pool_w, pool_scale)
    x1 = outproj_ln(xs2, y_gla, y_nsa, y_pool.reshape(n_dec, -1).astype(BF16), w_out_bf16, ln_g, ln_b, router)
    nsa_rows = rows.reshape(n_dec, 1, 4, G, hd)
    new_win = jnp.concatenate([state_win[layer, :, 1:], win.reshape(n_dec, 1, 2, G, hd)], axis=1)
    return x1, nsa_rows, new_win, s_gla, pool_rows


def prompt_mixer(x2, w_in_packed, w_out_bf16, ln_g, ln_b, gla_w2, gla_b, gla_norm_g, cmp_pos, cmp_w1, cmp_w2,
                 pool_w, pool_scale, n_batch, t_len, router=None):
    p = matmul(x2, w_in_packed)
    pos = jnp.arange(t_len, dtype=jnp.int32)
    rows, win, qn, qr, ks, vs, kw, vw, gates = nsa_prep(p, rope_tables(pos), n_batch, t_len)
    cmp_kv = nsa_compress_prompt(rows, cmp_pos, cmp_w1, cmp_w2, n_batch, t_len)
    o_cmp, selb = nsa_select(qn, cmp_kv, t_len // CMP_STRIDE - 1, t_len)
    y_nsa = nsa_attend(qr, o_cmp, selb, gates, ks, vs, kw, vw)
    s0 = jnp.zeros((n_batch, GLA_HEADS, GLA_DK, GLA_DV), F32)
    y_gla, s_gla = gla_mix(p, s0, gla_w2, gla_b, gla_norm_g, n_batch, t_len)
    prev = jnp.zeros((n_batch, POOL_MAX - 1, POOL_WIDTH), F32)
    y_pool = pool_mix(p, prev, 0, pool_w, pool_scale, n_batch, t_len)
    x1 = outproj_ln(x2, y_gla, y_nsa, y_pool, w_out_bf16, ln_g, ln_b, router)
    nsa_rows = rows.reshape(n_batch, t_len, 4, NSA_KV_HEADS, HEAD_DIM)
    n_win = min(WINDOW, t_len)
    win_rows = win.reshape(n_batch, t_len, 2, NSA_KV_HEADS, HEAD_DIM)[:, t_len - n_win:]
    pool_rows = p.reshape(n_batch, t_len, PACKED_WIDTH)[:, t_len - (POOL_MAX - 1):, COL_POOL:COL_POOL + POOL_WIDTH]
    return x1, nsa_rows, win_rows, s_gla, pool_rows


def split_proj(p):
    out = {}
    off = 0
    for name, size in PROJ_SIZES:
        out[name] = p[..., off:off + size]
        off += size
    return out


def layer_norm(x, g, b):
    xf = x.astype(jnp.float32)
    xc = xf - jnp.mean(xf, -1, keepdims=True)
    var = jnp.mean(xc * xc, -1, keepdims=True)
    return (xc * lax.rsqrt(var + LN_EPS) * g + b).astype(x.dtype)


def rope(x, pos):
    half = ROPE_DIM // 2
    inv_freq = ROPE_THETA ** (-jnp.arange(half, dtype=jnp.float32) / half)
    ang = pos.astype(jnp.float32)[:, None] * inv_freq[None, :]
    cos = jnp.cos(ang)[:, None, :]
    sin = jnp.sin(ang)[:, None, :]
    xf = x.astype(jnp.float32)
    x1, x2 = xf[..., :half], xf[..., half:ROPE_DIM]
    out = jnp.concatenate([x1 * cos - x2 * sin, x2 * cos + x1 * sin, xf[..., ROPE_DIM:]], axis=-1)
    return out.astype(x.dtype)


def masked_softmax(s, mask):
    s = jnp.where(mask, s.astype(jnp.float32), -jnp.inf)
    m = jnp.max(s, axis=-1, keepdims=True)
    m = jnp.where(jnp.isfinite(m), m, 0.0)
    p = jnp.where(mask, jnp.exp(s - m), 0.0)
    return p / jnp.maximum(jnp.sum(p, -1, keepdims=True), 1e-30)


def gla_recurrence(q, k, v, g, s0):
    B, T, H, _ = q.shape
    C = GLA_CHUNK
    n_chunks = -(-T // C)
    pad = n_chunks * C - T

    def prep(a):
        a = jnp.pad(a, ((0, 0), (0, pad), (0, 0), (0, 0)))
        return a.reshape(B, n_chunks, C, H, a.shape[-1]).transpose(1, 0, 3, 2, 4)

    causal = jnp.tril(jnp.ones((C, C), dtype=bool))

    def step(S, inp):
        qi, ki, vi, gi = [a.astype(jnp.float32) for a in inp]
        b = jnp.cumsum(gi, axis=2)
        o_inter = jnp.einsum('bhtk,bhkv->bhtv', qi * jnp.exp(b), S)
        diff = jnp.where(causal[:, :, None], b[:, :, :, None, :] - b[:, :, None, :, :], -jnp.inf)
        attn = jnp.einsum('bhtk,bhsk,bhtsk->bhts', qi, ki, jnp.exp(diff))
        o = o_inter + jnp.einsum('bhts,bhsv->bhtv', attn, vi)
        b_last = b[:, :, -1:, :]
        S = jnp.exp(b_last[:, :, 0, :])[..., None] * S + jnp.einsum('bhsk,bhsv->bhkv', ki * jnp.exp(b_last - b), vi)
        return S, o

    S, o = lax.scan(step, s0.astype(jnp.float32), (prep(q), prep(k), prep(v), prep(g)))
    o = o.transpose(1, 0, 3, 2, 4).reshape(B, n_chunks * C, H, v.shape[-1])[:, :T]
    return o, S.astype(s0.dtype)


def gla_mixer(parts, s0, w2, b2, norm_g):
    B, T = parts['gla_q'].shape[:2]
    q = parts['gla_q'].reshape(B, T, GLA_HEADS, GLA_DK) * (GLA_DK ** -0.5)
    k = parts['gla_k'].reshape(B, T, GLA_HEADS, GLA_DK)
    v = parts['gla_v'].reshape(B, T, GLA_HEADS, GLA_DV)
    g = jax.nn.log_sigmoid((parts['gla_glr'] @ w2 + b2).astype(jnp.float32)) / GLA_GATE_NORM
    g = g.reshape(B, T, GLA_HEADS, GLA_DK)
    o, s_new = gla_recurrence(q, k, v, g, s0)
    o = o * lax.rsqrt(jnp.mean(o * o, -1, keepdims=True) + RMS_EPS)
    out = o.reshape(B, T, GLA_WIDTH) * norm_g * jax.nn.silu(parts['gla_r'].astype(jnp.float32))
    return out.astype(parts['gla_v'].dtype), s_new


def nsa_compress(kx, pos_emb, w1, w2):
    B, T, G, D = kx.shape
    nh = T // CMP_STRIDE
    halves = kx[:, :nh * CMP_STRIDE].reshape(B, nh, CMP_STRIDE, G, D).astype(jnp.float32)
    pe = pos_emb.reshape(2, CMP_STRIDE, D)
    w = w1.reshape(2, CMP_STRIDE, D, CMP_HIDDEN)
    h_lo = jnp.einsum('bnjgd,jdh->bngh', halves + pe[0][None, None, :, None, :], w[0])
    h_hi = jnp.einsum('bnjgd,jdh->bngh', halves + pe[1][None, None, :, None, :], w[1])
    h = jax.nn.gelu(h_lo[:, :-1] + h_hi[:, 1:])
    return jnp.einsum('bngh,hd->bngd', h, w2)


def nsa_global(qn, qr, rows, q_pos, cmp_pos, cmp_w1, cmp_w2):
    B, Tq, G, R, D = qn.shape
    Tk = rows.shape[1]
    scale = HEAD_DIM ** -0.5
    kcmp = nsa_compress(rows[:, :, 0], cmp_pos[0], cmp_w1[0], cmp_w2[0])
    vcmp = nsa_compress(rows[:, :, 1], cmp_pos[1], cmp_w1[1], cmp_w2[1])
    n_cmp = kcmp.shape[1]
    cmp_end = jnp.arange(n_cmp) * CMP_STRIDE + CMP_LEN - 1
    n_sel = -(-Tk // SEL_BLOCK)
    pad = n_sel * SEL_BLOCK - Tk

    def to_blocks(a):
        a = jnp.pad(a, ((0, 0), (0, pad), (0, 0), (0, 0)))
        return a.reshape(B, n_sel, SEL_BLOCK, G, D).transpose(0, 3, 1, 2, 4)

    ksb = to_blocks(rows[:, :, 2])
    vsb = to_blocks(rows[:, :, 3])
    ci = jnp.arange(n_cmp)[:, None]
    sj = jnp.arange(n_sel)[None, :]
    overlap = ((ci * CMP_STRIDE <= sj * SEL_BLOCK + SEL_BLOCK - 1) &
               (ci * CMP_STRIDE + CMP_LEN - 1 >= sj * SEL_BLOCK)).astype(jnp.float32)
    n_top = min(SEL_TOPN, n_sel)
    gather = jax.vmap(jax.vmap(lambda blocks, idx: blocks[idx]))
    blk_ids = jnp.arange(n_sel)

    def block_fn(args):
        qnb, qrb, qp = args
        qb = qp.shape[0]
        s = jnp.einsum('bqgrd,bngd->bgrqn', qnb, kcmp) * scale
        p_c = masked_softmax(s, cmp_end[None, :] <= qp[:, None])
        o_c = jnp.einsum('bgrqn,bngd->bqgrd', p_c, vcmp)
        imp = jnp.einsum('bgqn,nj->bgqj', jnp.sum(p_c, axis=2), overlap)
        cur = qp[:, None] // SEL_BLOCK
        valid = blk_ids[None, :] <= cur
        forced = (blk_ids[None, :] == 0) | (blk_ids[None, :] == cur) | (blk_ids[None, :] == cur - 1)
        imp = jnp.where(valid, jnp.where(forced, jnp.inf, imp), -jnp.inf)
        top_s, top_i = lax.top_k(imp, n_top)
        kg = gather(ksb, top_i)
        vg = gather(vsb, top_i)
        kpos = top_i[..., None] * SEL_BLOCK + jnp.arange(SEL_BLOCK)
        mask = (top_s > -jnp.inf)[..., None] & (kpos <= qp[None, None, :, None, None])
        s2 = jnp.einsum('bqgrd,bgqnjd->bgrqnj', qrb, kg) * scale
        s2 = s2.reshape(B, G, R, qb, n_top * SEL_BLOCK)
        p_s = masked_softmax(s2, mask.reshape(B, G, 1, qb, n_top * SEL_BLOCK))
        p_s = p_s.reshape(B, G, R, qb, n_top, SEL_BLOCK)
        o_s = jnp.einsum('bgrqnj,bgqnjd->bqgrd', p_s, vg.astype(jnp.float32))
        return o_c, o_s

    qb = min(Tq, Q_BLOCK)
    nq = -(-Tq // qb)
    padq = nq * qb - Tq
    qpad = ((0, 0), (0, padq), (0, 0), (0, 0), (0, 0))
    qn_b = jnp.pad(qn, qpad).reshape(B, nq, qb, G, R, D).swapaxes(0, 1)
    qr_b = jnp.pad(qr, qpad).reshape(B, nq, qb, G, R, D).swapaxes(0, 1)
    qp_b = jnp.pad(q_pos, (0, padq), mode='edge').reshape(nq, qb)
    o_c, o_s = lax.map(block_fn, (qn_b, qr_b, qp_b))
    o_c = o_c.swapaxes(0, 1).reshape(B, nq * qb, G, R, D)[:, :Tq]
    o_s = o_s.swapaxes(0, 1).reshape(B, nq * qb, G, R, D)[:, :Tq]
    return o_c, o_s


def band_attend(q, k, v, qpos, kpos):
    s = jnp.einsum('bqgrd,bkgd->bgrqk', q, k) * (HEAD_DIM ** -0.5)
    mask = ((kpos[None, :] <= qpos[:, None]) & (kpos[None, :] > qpos[:, None] - WINDOW) & (kpos[None, :] >= 0))
    p = masked_softmax(s, mask)
    return jnp.einsum('bgrqk,bkgd->bqgrd', p, v.astype(jnp.float32))


def sliding_prompt(q, k, v):
    B, T, G, R, D = q.shape
    kp = jnp.pad(k, ((0, 0), (WINDOW, 0), (0, 0), (0, 0)))
    vp = jnp.pad(v, ((0, 0), (WINDOW, 0), (0, 0), (0, 0)))
    nq = T // Q_BLOCK

    def fn(i):
        q0 = i * Q_BLOCK
        qi = lax.dynamic_slice_in_dim(q, q0, Q_BLOCK, axis=1)
        ki = lax.dynamic_slice_in_dim(kp, q0, WINDOW + Q_BLOCK, axis=1)
        vi = lax.dynamic_slice_in_dim(vp, q0, WINDOW + Q_BLOCK, axis=1)
        qpos = q0 + jnp.arange(Q_BLOCK)
        kpos = q0 - WINDOW + jnp.arange(WINDOW + Q_BLOCK)
        return band_attend(qi, ki, vi, qpos, kpos)

    o = lax.map(fn, jnp.arange(nq))
    return o.swapaxes(0, 1).reshape(B, T, G, R, D)


def nsa_mixer(parts, pos, nsa_past, win_past, cmp_pos, cmp_w1, cmp_w2):
    B, T = parts['nsa_q'].shape[:2]
    dt = parts['nsa_q'].dtype
    q = parts['nsa_q'].reshape(B, T, NSA_HEADS, HEAD_DIM)
    q_rope = rope(q, pos)
    kv = lambda name: parts[name].reshape(B, T, NSA_KV_HEADS, HEAD_DIM)
    k_win, v_win = rope(kv('win_k'), pos), kv('win_v')
    new_rows = jnp.stack([kv('cmp_k'), kv('cmp_v'), rope(kv('slc_k'), pos), kv('slc_v')], axis=2)
    rows = new_rows if nsa_past is None else jnp.concatenate([nsa_past.astype(dt), new_rows], axis=1)
    qg = q.reshape(B, T, NSA_KV_HEADS, NSA_REP, HEAD_DIM)
    qrg = q_rope.reshape(B, T, NSA_KV_HEADS, NSA_REP, HEAD_DIM)
    o_cmp, o_slc = nsa_global(qg, qrg, rows, pos, cmp_pos, cmp_w1, cmp_w2)
    win_rows = jnp.stack([k_win, v_win], axis=2)
    if win_past is None:
        o_win = sliding_prompt(qrg, k_win, v_win)
        new_win = win_rows[:, T - min(WINDOW, T):]
    else:
        n_buf = win_past.shape[1]
        ext = jnp.concatenate([win_past.astype(dt), win_rows], axis=1)
        kpos = pos[0] - n_buf + jnp.arange(n_buf + T)
        o_win = band_attend(qrg, ext[:, :, 0], ext[:, :, 1], pos, kpos)
        new_win = ext[:, T:]
    gates = jax.nn.sigmoid(parts['nsa_gate'].astype(jnp.float32)).reshape(B, T, NSA_KV_HEADS, NSA_REP, 3)
    o = gates[..., 0:1] * o_cmp + gates[..., 1:2] * o_slc + gates[..., 2:3] * o_win
    return o.reshape(B, T, NSA_WIDTH).astype(dt), new_rows, new_win


def pool_mixer(u, prev, pos, w_pool, scale):
    B, T, C = u.shape
    P = POOL_MAX - 1
    ext = jnp.concatenate([prev.astype(u.dtype), u], axis=1).astype(jnp.float32)
    cs = jnp.concatenate([jnp.zeros((B, 1, C), jnp.float32), jnp.cumsum(ext, axis=1)], axis=1)
    end = cs[:, P + 1:]
    means = []
    for gi, w in enumerate(POOL_WINDOWS):
        sl = slice(gi * POOL_GROUP_DIM, (gi + 1) * POOL_GROUP_DIM)
        start = cs[:, P + 1 - w:P + 1 - w + T, sl]
        cnt = jnp.minimum(pos + 1, w).astype(jnp.float32)[None, :, None]
        means.append((end[..., sl] - start) / cnt)
    pooled = (jnp.concatenate(means, axis=-1) - ext[:, P:]).reshape(B, T, POOL_GROUPS, POOL_GROUP_DIM)
    y = jnp.einsum('btgc,gcd->btgd', pooled, w_pool.astype(jnp.float32)).reshape(B, T, C) * scale
    return y.astype(u.dtype), ext[:, -P:].astype(u.dtype)


def mixer_layer(x, pos0, gla_s0, pool_prev, nsa_past, win_past,
                w_in, gla_w2, gla_b, gla_norm_g, cmp_pos, cmp_w1, cmp_w2, pool_w, pool_scale, w_out):
    B, T, _ = x.shape
    pos = pos0 + jnp.arange(T, dtype=jnp.int32)
    parts = split_proj(mm3(x, w_in))
    y_gla, s_gla = gla_mixer(parts, gla_s0, gla_w2, gla_b, gla_norm_g)
    y_nsa, nsa_rows, win_rows = nsa_mixer(parts, pos, nsa_past, win_past, cmp_pos, cmp_w1, cmp_w2)
    y_pool, pool_rows = pool_mixer(parts['pool'], pool_prev, pos, pool_w, pool_scale)
    y = mm3(jnp.concatenate([y_gla, y_nsa, y_pool], axis=-1), w_out)
    return y, nsa_rows, win_rows, s_gla, pool_rows


def swiglu(x, wg, wu, wd):
    return mm3(jax.nn.silu(mm3(x, wg)) * mm3(x, wu), wd)


def moe_ffn(x, router, wg, wu, wd):
    B, T, D = x.shape
    n_tok = B * T
    xt = x.reshape(n_tok, D)
    logits = (xt @ router).astype(jnp.float32)
    top_v, top_i = lax.top_k(logits, TOP_K)
    gates = jax.nn.softmax(top_v, axis=-1)
    n_asg = n_tok * TOP_K
    e_flat = top_i.reshape(n_asg)
    tok_flat = jnp.arange(n_asg, dtype=jnp.int32) // TOP_K
    gate_flat = gates.reshape(n_asg)
    blk = MOE_ROW_BLOCK if n_asg >= N_EXPERTS * MOE_ROW_BLOCK else MOE_MIN_BLOCK
    n_blk = -(-(n_asg + N_EXPERTS * (blk - 1)) // blk)
    order = jnp.argsort(e_flat)
    e_sorted = e_flat[order]
    counts = jnp.bincount(e_flat, length=N_EXPERTS)
    padded = (counts + blk - 1) // blk * blk
    pad_end = jnp.cumsum(padded)
    pad_start = pad_end - padded
    start = jnp.cumsum(counts) - counts
    dest = pad_start[e_sorted] + jnp.arange(n_asg) - start[e_sorted]
    row_tok = jnp.zeros((n_blk * blk,), jnp.int32).at[dest].set(tok_flat[order])
    row_gate = jnp.zeros((n_blk * blk,), jnp.float32).at[dest].set(gate_flat[order])
    blk_expert = jnp.minimum(jnp.searchsorted(pad_end, jnp.arange(n_blk) * blk, side='right'), N_EXPERTS - 1)

    def expert_block(args):
        rows, e = args
        xb = xt[rows]
        return (jax.nn.silu(xb @ wg[e]) * (xb @ wu[e])) @ wd[e]

    out = lax.map(expert_block, (row_tok.reshape(n_blk, blk), blk_expert))
    y = jnp.zeros((n_tok, D), jnp.float32).at[row_tok].add(
        out.reshape(n_blk * blk, D).astype(jnp.float32) * row_gate[:, None])
    return y.reshape(B, T, D).astype(x.dtype)


def kernel(x_prompt, x_sample, cache_nsa, page_table, state_win, state_gla, state_pool, w_in, gla_gate_w2, gla_gate_b, gla_norm_g, nsa_cmp_pos, nsa_cmp_w1, nsa_cmp_w2, pool_w, pool_scale, w_out, ln1_g, ln1_b, ln2_g, ln2_b, ffn_w_gate, ffn_w_up, ffn_w_down, moe_router, moe_w_gate, moe_w_up, moe_w_down):
    n_prompt, t_len, d = x_prompt.shape
    n_dec = x_sample.shape[0]
    xp, xs = x_prompt.reshape(n_prompt * t_len, d), x_sample.reshape(n_dec, d)
    nsa_p, nsa_s, win_p, win_s, gla_p, gla_s, pool_p, pool_s = [], [], [], [], [], [], [], []
    for l in range(DEPTH):
        i = l // 2
        router = moe_router[i] if l % 2 else None
        w_in_packed, w_out_bf16 = pack_w_in(w_in[l]), w_out[l].astype(BF16)
        lw = (w_in_packed, w_out_bf16, ln1_g[l], ln1_b[l], gla_gate_w2[l], gla_gate_b[l], gla_norm_g[l],
              nsa_cmp_pos[l], nsa_cmp_w1[l], nsa_cmp_w2[l], pool_w[l], pool_scale[l])
        xp, r_p, w_p, g_p, p_p = prompt_mixer(xp, *lw, n_prompt, t_len, router)
        xs, r_s, w_s, g_s, p_s = decode_mixer(xs, *lw, cache_nsa, l, page_table, state_win, state_gla[l],
                                              state_pool[l], PAST_LEN, router)
        if l % 2 == 0:
            wg, wu, wd = ffn_w_gate[i].astype(BF16), ffn_w_up[i].astype(BF16), ffn_w_down[i].astype(BF16)
            xp = ffn_ln(xp, wg, wu, wd, ln2_g[l], ln2_b[l])
            xs = ffn_ln(xs, wg, wu, wd, ln2_g[l], ln2_b[l])
        else:
            (xp, lg_p), (xs, lg_s) = xp, xs
            xp, xs = moe_ln([xp, xs], [lg_p, lg_s], moe_w_gate[i], moe_w_up[i], moe_w_down[i], ln2_g[l], ln2_b[l])
        nsa_p.append(r_p); nsa_s.append(r_s); win_p.append(w_p); win_s.append(w_s)
        gla_p.append(g_p); gla_s.append(g_s); pool_p.append(p_p); pool_s.append(p_s)
    return (xp.reshape(n_prompt, t_len, d), xs.reshape(x_sample.shape), jnp.stack(nsa_p), jnp.stack(nsa_s),
            jnp.stack(win_p), jnp.stack(win_s), jnp.stack(gla_p), jnp.stack(gla_s), jnp.stack(pool_p),
            jnp.stack(pool_s))
```

```python
import functools

import jax
import jax.numpy as jnp
from jax import lax
from jax.experimental import pallas as pl
from jax.experimental.pallas import tpu as pltpu

D_MODEL = 2048
DEPTH = 2
PAST_LEN = 16384
HEAD_DIM = 128
GLA_HEADS = 4
GLA_DK = 64
GLA_DV = 128
GLA_RANK = 16
GLA_GATE_NORM = 16.0
GLA_CHUNK = 64
GLA_WIDTH = GLA_HEADS * GLA_DV
NSA_HEADS = 8
NSA_KV_HEADS = 2
NSA_REP = NSA_HEADS // NSA_KV_HEADS
NSA_WIDTH = NSA_HEADS * HEAD_DIM
CMP_LEN = 32
CMP_STRIDE = 16
CMP_HIDDEN = 128
SEL_BLOCK = 64
SEL_TOPN = 16
WINDOW = 512
Q_BLOCK = 128
POOL_GROUPS = 4
POOL_GROUP_DIM = 128
POOL_WIDTH = POOL_GROUPS * POOL_GROUP_DIM
POOL_WINDOWS = (2, 4, 8, 16)
POOL_MAX = 16
ROPE_THETA = 500000.0
ROPE_DIM = HEAD_DIM // 4
N_EXPERTS = 8
TOP_K = 2
MOE_ROW_BLOCK = 128
MOE_MIN_BLOCK = 8
ALPHA = (2 * DEPTH) ** 0.25
LN_EPS = 1e-5
RMS_EPS = 1e-6

PROJ_SIZES = (
    ('gla_q', GLA_HEADS * GLA_DK), ('gla_k', GLA_HEADS * GLA_DK), ('gla_v', GLA_HEADS * GLA_DV),
    ('gla_glr', GLA_RANK), ('gla_r', GLA_HEADS * GLA_DV),
    ('nsa_q', NSA_HEADS * HEAD_DIM),
    ('cmp_k', NSA_KV_HEADS * HEAD_DIM), ('cmp_v', NSA_KV_HEADS * HEAD_DIM),
    ('slc_k', NSA_KV_HEADS * HEAD_DIM), ('slc_v', NSA_KV_HEADS * HEAD_DIM),
    ('win_k', NSA_KV_HEADS * HEAD_DIM), ('win_v', NSA_KV_HEADS * HEAD_DIM),
    ('nsa_gate', 3 * NSA_HEADS),
    ('pool', POOL_WIDTH),
)

GLA_SUB = 16
SEL_PAD = 128

BF16 = jnp.bfloat16
F32 = jnp.float32
NEG_BIG = -1e30
VMEM_LIMIT_BYTES = 56 * 1024 * 1024

COL_NSA_Q = 0
COL_ROWS = 1024
COL_WIN = 2048
COL_POOL = 2560
COL_GLA_V = 3072
COL_GLA_R = 3584
COL_GLA_Q = 4096
COL_GLA_K = 4352
COL_SMALL = 4608
PACKED_WIDTH = 4736
SMALL_GATE_OFF = GLA_RANK


def _params(*sem):
    return pltpu.CompilerParams(dimension_semantics=sem, vmem_limit_bytes=VMEM_LIMIT_BYTES)


def _proj_offsets():
    out, off = {}, 0
    for name, size in PROJ_SIZES:
        out[name] = (off, size)
        off += size
    return out


def pack_w_in(w):
    offs = _proj_offsets()
    sl = lambda n: w[:, offs[n][0]:offs[n][0] + offs[n][1]]
    pad = jnp.zeros((w.shape[0], 128 - GLA_RANK - 3 * NSA_HEADS), w.dtype)
    cols = [sl('nsa_q'), sl('cmp_k'), sl('cmp_v'), sl('slc_k'), sl('slc_v'), sl('win_k'), sl('win_v'),
            sl('pool'), sl('gla_v'), sl('gla_r'), sl('gla_q'), sl('gla_k'), sl('gla_glr'), sl('nsa_gate'), pad]
    return jnp.concatenate(cols, axis=1).astype(BF16)


def rope_tables(pos):
    half = ROPE_DIM // 2
    inv_freq = ROPE_THETA ** (-jnp.arange(half, dtype=F32) / half)
    ang = pos.astype(F32)[:, None] * inv_freq[None, :]
    cos, sin = jnp.cos(ang), jnp.sin(ang)
    t = pos.shape[0]
    c = jnp.concatenate([cos, cos, jnp.ones((t, HEAD_DIM - ROPE_DIM), F32)], axis=1)
    sa = jnp.concatenate([-sin, jnp.zeros((t, HEAD_DIM - half), F32)], axis=1)
    sb = jnp.concatenate([jnp.zeros((t, half), F32), sin, jnp.zeros((t, HEAD_DIM - ROPE_DIM), F32)], axis=1)
    return c, sa, sb


def _pick_tile(n, pref):
    for t in pref:
        if n % t == 0:
            return t
    return n


def _mm_kernel(x_ref, w_ref, o_ref, xb_ref):
    @pl.when(pl.program_id(1) == 0)
    def _():
        xb_ref[...] = x_ref[...].astype(BF16)

    o_ref[...] = jnp.dot(xb_ref[...], w_ref[...].astype(BF16), preferred_element_type=F32)


def matmul(x, w):
    m, k = x.shape
    n = w.shape[1]
    tm = _pick_tile(m, tuple(t for t in (1024, 512, 256, 128, 64, 32, 16, 8) if t * k <= 2048 * 1024))
    tn = 512 if n >= 512 else n
    return pl.pallas_call(
        _mm_kernel,
        grid=(m // tm, pl.cdiv(n, tn)),
        in_specs=[pl.BlockSpec((tm, k), lambda i, j: (i, 0)),
                  pl.BlockSpec((k, tn), lambda i, j: (0, j))],
        out_specs=pl.BlockSpec((tm, tn), lambda i, j: (i, j)),
        out_shape=jax.ShapeDtypeStruct((m, n), F32),
        scratch_shapes=[pltpu.VMEM((tm, k), BF16)],
        compiler_params=_params("arbitrary", "arbitrary"),
        name="matmul",
    )(x, w)


def mm3(x, w):
    lead = x.shape[:-1]
    return matmul(x.reshape(-1, x.shape[-1]), w).reshape(*lead, w.shape[1])


def _rope(x, c, sa, sb):
    return x * c + pltpu.roll(x, HEAD_DIM - ROPE_DIM // 2, 1) * sa + pltpu.roll(x, ROPE_DIM // 2, 1) * sb


def _nsa_prep_kernel(q_ref, rows_ref, win_ref, small_ref, c_ref, sa_ref, sb_ref,
                     rows_o, win_o, qn_o, qr_o, ks_o, vs_o, kw_o, vw_o, gate_o):
    c, sa, sb = c_ref[...], sa_ref[...], sb_ref[...]
    scale = HEAD_DIM ** -0.5
    hd = HEAD_DIM
    for h in range(NSA_HEADS):
        x = q_ref[:, h * hd:(h + 1) * hd]
        qn_o[0, h] = (x * scale).astype(BF16)
        qr_o[0, h] = (_rope(x, c, sa, sb) * scale).astype(BF16)
    ones = jnp.ones((q_ref.shape[0], hd), BF16)
    rows_o[:, 0:4 * hd] = rows_ref[:, 0:4 * hd]
    for g in range(NSA_KV_HEADS):
        k = _rope(rows_ref[:, (4 + g) * hd:(5 + g) * hd], c, sa, sb)
        rows_o[:, (4 + g) * hd:(5 + g) * hd] = k
        ks_o[0, g] = k.astype(BF16)
        v = rows_ref[:, (6 + g) * hd:(7 + g) * hd]
        rows_o[:, (6 + g) * hd:(7 + g) * hd] = v
        vs_o[0, g, :, 0:hd] = v.astype(BF16)
        vs_o[0, g, :, hd:2 * hd] = ones
        k = _rope(win_ref[:, g * hd:(g + 1) * hd], c, sa, sb)
        win_o[:, g * hd:(g + 1) * hd] = k
        kw_o[0, g] = k.astype(BF16)
        v = win_ref[:, (2 + g) * hd:(3 + g) * hd]
        win_o[:, (2 + g) * hd:(3 + g) * hd] = v
        vw_o[0, g, :, 0:hd] = v.astype(BF16)
        vw_o[0, g, :, hd:2 * hd] = ones
    sig = jax.nn.sigmoid(small_ref[...])
    per_g = 3 * NSA_REP
    for g in range(NSA_KV_HEADS):
        gate_o[0, g] = pltpu.roll(sig, 128 - SMALL_GATE_OFF - g * per_g, 1)


def nsa_prep(p, tables, n_batch, t_len):
    tr = _pick_tile(t_len, (512, 256, 128, 64, 32, 16))
    nt = t_len // tr
    n = n_batch * t_len
    hd = HEAD_DIM
    row = lambda w, cb: pl.BlockSpec((tr, w), lambda b, i: (b * nt + i, cb))
    tab = pl.BlockSpec((tr, hd), lambda b, i: (i, 0))
    head = lambda nh, w: pl.BlockSpec((1, nh, tr, w), lambda b, i: (b, 0, i, 0))
    out_shape = (
        jax.ShapeDtypeStruct((n, 8 * hd), F32),
        jax.ShapeDtypeStruct((n, 4 * hd), F32),
        jax.ShapeDtypeStruct((n_batch, NSA_HEADS, t_len, hd), BF16),
        jax.ShapeDtypeStruct((n_batch, NSA_HEADS, t_len, hd), BF16),
        jax.ShapeDtypeStruct((n_batch, NSA_KV_HEADS, t_len, hd), BF16),
        jax.ShapeDtypeStruct((n_batch, NSA_KV_HEADS, t_len, 2 * hd), BF16),
        jax.ShapeDtypeStruct((n_batch, NSA_KV_HEADS, t_len, hd), BF16),
        jax.ShapeDtypeStruct((n_batch, NSA_KV_HEADS, t_len, 2 * hd), BF16),
        jax.ShapeDtypeStruct((n_batch, NSA_KV_HEADS, t_len, 128), F32),
    )
    return pl.pallas_call(
        _nsa_prep_kernel,
        grid=(n_batch, nt),
        in_specs=[row(8 * hd, COL_NSA_Q // (8 * hd)), row(8 * hd, COL_ROWS // (8 * hd)),
                  row(4 * hd, COL_WIN // (4 * hd)), row(128, COL_SMALL // 128), tab, tab, tab],
        out_specs=(row(8 * hd, 0), row(4 * hd, 0), head(NSA_HEADS, hd), head(NSA_HEADS, hd),
                   head(NSA_KV_HEADS, hd), head(NSA_KV_HEADS, 2 * hd), head(NSA_KV_HEADS, hd),
                   head(NSA_KV_HEADS, 2 * hd), head(NSA_KV_HEADS, 128)),
        out_shape=out_shape,
        compiler_params=_params("arbitrary", "arbitrary"),
        name="nsa_prep",
    )(p, p, p, p, *tables)


def _nsa_cmp_kernel(x_ref, pe_ref, w1_ref, w2_ref, o_ref):
    nh = o_ref.shape[0]
    h_lo = jnp.zeros((nh, CMP_HIDDEN), F32)
    h_hi = jnp.zeros((nh, CMP_HIDDEN), F32)
    for j in range(CMP_STRIDE):
        xj = x_ref[pl.ds(j, nh, stride=CMP_STRIDE), :]
        h_lo += jnp.dot((xj + pe_ref[j:j + 1, :]).astype(BF16), w1_ref[j].astype(BF16), preferred_element_type=F32)
        h_hi += jnp.dot((xj + pe_ref[CMP_STRIDE + j:CMP_STRIDE + j + 1, :]).astype(BF16),
                        w1_ref[CMP_STRIDE + j].astype(BF16), preferred_element_type=F32)
    h = jax.nn.gelu(h_lo + pltpu.roll(h_hi, nh - 1, 0))
    o_ref[...] = jnp.dot(h.astype(BF16), w2_ref[...].astype(BF16), preferred_element_type=F32).astype(BF16)


def nsa_compress_prompt(rows, cmp_pos, cmp_w1, cmp_w2, n_batch, t_len):
    nh = t_len // CMP_STRIDE
    hd = HEAD_DIM
    rows3 = rows.reshape(n_batch, t_len, 8 * hd)
    return pl.pallas_call(
        _nsa_cmp_kernel,
        grid=(n_batch, 2, NSA_KV_HEADS),
        in_specs=[pl.BlockSpec((None, t_len, hd), lambda b, kd, g: (b, 0, kd * NSA_KV_HEADS + g)),
                  pl.BlockSpec((None, CMP_LEN, hd), lambda b, kd, g: (kd, 0, 0)),
                  pl.BlockSpec((None, CMP_LEN, hd, CMP_HIDDEN), lambda b, kd, g: (kd, 0, 0, 0)),
                  pl.BlockSpec((None, CMP_HIDDEN, hd), lambda b, kd, g: (kd, 0, 0))],
        out_specs=pl.BlockSpec((None, None, None, nh, hd), lambda b, kd, g: (b, kd, g, 0, 0)),
        out_shape=jax.ShapeDtypeStruct((n_batch, 2, NSA_KV_HEADS, nh, hd), BF16),
        compiler_params=_params("arbitrary", "arbitrary", "arbitrary"),
        name="nsa_compress",
    )(rows3, cmp_pos, cmp_w1, cmp_w2)


def _nsa_select_kernel(qn_ref, kc_ref, vc_ref, ovt_ref, oc_ref, selb_ref, *, n_cmp, n_top):
    rep, tq, hd = qn_ref.shape[1], qn_ref.shape[2], qn_ref.shape[3]
    n_cmp_pad = kc_ref.shape[0]
    n_sel = ovt_ref.shape[0]
    q0 = pl.program_id(2) * tq
    q = qn_ref[0].reshape(rep * tq, hd)
    s = lax.dot_general(q, kc_ref[...], (((1,), (1,)), ((), ())), preferred_element_type=F32)
    row = lax.broadcasted_iota(jnp.int32, (rep * tq, n_cmp_pad), 0)
    col = lax.broadcasted_iota(jnp.int32, (rep * tq, n_cmp_pad), 1)
    qpos = q0 + (row & (tq - 1))
    mask = (col * CMP_STRIDE + (CMP_LEN - 1) <= qpos) & (col < n_cmp)
    s = jnp.where(mask, s, -jnp.inf)
    m = jnp.max(s, axis=-1, keepdims=True)
    m = jnp.where(m > -jnp.inf, m, 0.0)
    p = jnp.where(mask, jnp.exp(s - m), 0.0)
    p = p / jnp.maximum(jnp.sum(p, axis=-1, keepdims=True), 1e-30)
    oc = jnp.dot(p.astype(BF16), vc_ref[...], preferred_element_type=F32)
    oc_ref[0] = oc.reshape(rep, tq, hd).astype(BF16)
    psum = p[0:tq]
    for r in range(1, rep):
        psum = psum + p[r * tq:(r + 1) * tq]
    imp = lax.dot_general(ovt_ref[...], psum, (((1,), (1,)), ((), ())), preferred_element_type=F32,
                          precision=lax.Precision.HIGHEST)
    blk = lax.broadcasted_iota(jnp.int32, (n_sel, tq), 0)
    cur = (q0 + lax.broadcasted_iota(jnp.int32, (n_sel, tq), 1)) // SEL_BLOCK
    forced = (blk == 0) | (blk == cur) | (blk == cur - 1)
    v = jnp.where(blk <= cur, jnp.where(forced, jnp.inf, imp), -jnp.inf)
    rank = jnp.zeros((n_sel, tq), jnp.int32)
    for i in range(n_sel):
        vi = v[i:i + 1, :]
        ahead = (vi > v) | ((vi == v) & (blk > i))
        rank = rank + ahead.astype(jnp.int32)
    selb_t = jnp.where((rank < n_top) & (v > -jnp.inf), 0.0, NEG_BIG)
    pad = jnp.full((SEL_PAD - n_sel, tq), NEG_BIG, F32)
    selb_ref[0, 0] = jnp.concatenate([selb_t, pad], axis=0).T.astype(BF16)


def nsa_select(qn, cmp_kv, n_cmp, t_k):
    n_batch, _, t_len, hd = qn.shape
    n_cmp_pad = cmp_kv.shape[3]
    n_sel = -(-t_k // SEL_BLOCK)
    tq = _pick_tile(t_len, (256, 128, 64, 32, 16))
    ci = jnp.arange(n_cmp_pad)[None, :]
    sj = jnp.arange(n_sel)[:, None]
    overlap_t = ((ci * CMP_STRIDE <= sj * SEL_BLOCK + SEL_BLOCK - 1) &
                 (ci * CMP_STRIDE + CMP_LEN - 1 >= sj * SEL_BLOCK) & (ci < n_cmp)).astype(F32)
    kern = functools.partial(_nsa_select_kernel, n_cmp=n_cmp, n_top=min(SEL_TOPN, n_sel))
    return pl.pallas_call(
        kern,
        grid=(n_batch, NSA_KV_HEADS, t_len // tq),
        in_specs=[pl.BlockSpec((1, NSA_REP, tq, hd), lambda b, g, i: (b, g, i, 0)),
                  pl.BlockSpec((None, None, None, n_cmp_pad, hd), lambda b, g, i: (b, 0, g, 0, 0)),
                  pl.BlockSpec((None, None, None, n_cmp_pad, hd), lambda b, g, i: (b, 1, g, 0, 0)),
                  pl.BlockSpec((n_sel, n_cmp_pad), lambda b, g, i: (0, 0))],
        out_specs=(pl.BlockSpec((1, NSA_REP, tq, hd), lambda b, g, i: (b, g, i, 0)),
                   pl.BlockSpec((1, 1, tq, SEL_PAD), lambda b, g, i: (b, g, i, 0))),
        out_shape=(jax.ShapeDtypeStruct((n_batch, NSA_HEADS, t_len, hd), BF16),
                   jax.ShapeDtypeStruct((n_batch, NSA_KV_HEADS, t_len, SEL_PAD), BF16)),
        compiler_params=_params("arbitrary", "arbitrary", "arbitrary"),
        name="nsa_select",
    )(qn, cmp_kv, cmp_kv, overlap_t)


ATTN_Q_BLOCK = 256
ATTN_K_TILE = 1024


def _nsa_attn_kernel(qr_ref, oc_ref, selb_ref, gate_ref, ks_ref, vs_ref, kw_ref, vw_ref, e_ref, o_ref,
                     m_scr, acc_scr, *, tk, wk):
    rep, qb, hd = qr_ref.shape[1], qr_ref.shape[2], qr_ref.shape[3]
    nr = rep * qb
    q0 = pl.program_id(2) * qb
    q = qr_ref[0].reshape(nr, hd)
    selb = selb_ref[0, 0]
    nt = (((1,), (1,)), ((), ()))

    def sel_scores(t):
        k = ks_ref[0, 0, pl.ds(pl.multiple_of(t * tk, tk), tk), :]
        s = lax.dot_general(q, k, nt, preferred_element_type=F32)
        bias = jnp.dot(selb, e_ref[t], preferred_element_type=F32)
        return (s.reshape(rep, qb, tk) + bias[None]).reshape(nr, tk)

    def sel_values(t):
        return vs_ref[0, 0, pl.ds(pl.multiple_of(t * tk, tk), tk), :]

    td = q0 // tk
    qpos = q0 + (lax.broadcasted_iota(jnp.int32, (nr, tk), 0) & (qb - 1))
    kpos = td * tk + lax.broadcasted_iota(jnp.int32, (nr, tk), 1)
    s = jnp.where(kpos <= qpos, sel_scores(td), NEG_BIG)
    m = jnp.max(s, axis=-1, keepdims=True)
    m_scr[...] = m
    acc_scr[...] = jnp.dot(jnp.exp(s - m).astype(BF16), sel_values(td), preferred_element_type=F32)

    def body(t, carry):
        s = sel_scores(t)
        m_old = m_scr[...]
        m_new = jnp.maximum(m_old, jnp.max(s, axis=-1, keepdims=True))
        p = jnp.exp(s - m_new).astype(BF16)
        acc_scr[...] = jnp.exp(m_old - m_new) * acc_scr[...] + jnp.dot(p, sel_values(t), preferred_element_type=F32)
        m_scr[...] = m_new
        return carry

    lax.fori_loop(0, td, body, 0)
    acc = acc_scr[...]
    o_sel = acc[:, 0:hd] / jnp.maximum(acc[:, hd:hd + 1], 1e-30)

    kstart = pl.multiple_of(jnp.maximum(q0 - WINDOW, 0), qb)
    kw = kw_ref[0, 0, pl.ds(kstart, wk), :]
    s = lax.dot_general(q, kw, nt, preferred_element_type=F32)
    qpos = q0 + (lax.broadcasted_iota(jnp.int32, (nr, wk), 0) & (qb - 1))
    kpos = kstart + lax.broadcasted_iota(jnp.int32, (nr, wk), 1)
    s = jnp.where((kpos <= qpos) & (kpos > qpos - WINDOW), s, NEG_BIG)
    m = jnp.max(s, axis=-1, keepdims=True)
    accw = jnp.dot(jnp.exp(s - m).astype(BF16), vw_ref[0, 0, pl.ds(kstart, wk), :], preferred_element_type=F32)
    o_win = accw[:, 0:hd] / jnp.maximum(accw[:, hd:hd + 1], 1e-30)

    gates = gate_ref[0, 0]
    for r in range(rep):
        rows = slice(r * qb, (r + 1) * qb)
        o = (gates[:, 3 * r:3 * r + 1] * oc_ref[0, r].astype(F32)
             + gates[:, 3 * r + 1:3 * r + 2] * o_sel[rows]
             + gates[:, 3 * r + 2:3 * r + 3] * o_win[rows])
        o_ref[:, r * hd:(r + 1) * hd] = o.astype(BF16)


def nsa_attend(qr, o_cmp, selb, gates, ks, vs, kw, vw):
    n_batch, _, t_len, hd = qr.shape
    n_sel = selb.shape[3]
    qb = min(ATTN_Q_BLOCK, t_len)
    tk = min(ATTN_K_TILE, t_len)
    wk = min(WINDOW + qb, t_len)
    nq = t_len // qb
    n_tiles = t_len // tk
    key_blk = (jnp.arange(n_tiles)[:, None, None] * tk + jnp.arange(tk)[None, None, :]) // SEL_BLOCK
    e = (key_blk == jnp.arange(n_sel)[None, :, None]).astype(BF16)
    kern = functools.partial(_nsa_attn_kernel, tk=tk, wk=wk)
    per_q = lambda nh, w: pl.BlockSpec((1, nh, qb, w), lambda b, g, i: (b, g, i, 0))
    full = lambda w: pl.BlockSpec((1, 1, t_len, w), lambda b, g, i: (b, g, 0, 0))
    return pl.pallas_call(
        kern,
        grid=(n_batch, NSA_KV_HEADS, nq),
        in_specs=[per_q(NSA_REP, hd), per_q(NSA_REP, hd), per_q(1, n_sel), per_q(1, 128),
                  full(hd), full(2 * hd), full(hd), full(2 * hd),
                  pl.BlockSpec((n_tiles, n_sel, tk), lambda b, g, i: (0, 0, 0))],
        out_specs=pl.BlockSpec((qb, NSA_REP * hd), lambda b, g, i: (b * nq + i, g)),
        out_shape=jax.ShapeDtypeStruct((n_batch * t_len, NSA_HEADS * hd), BF16),
        scratch_shapes=[pltpu.VMEM((NSA_REP * qb, 1), F32), pltpu.VMEM((NSA_REP * qb, 2 * hd), F32)],
        compiler_params=_params("arbitrary", "arbitrary", "arbitrary"),
        name="nsa_attend",
    )(qr, o_cmp, selb, gates, ks, vs, kw, vw, e)


def _gla_kernel(q_ref, k_ref, v_ref, r_ref, small_ref, w2_ref, b2_ref, ng_ref, s0_ref, y_ref, sf_ref, s_scr):
    tb = q_ref.shape[0]
    c, sub, dk, dv = GLA_CHUNK, GLA_SUB, GLA_DK, GLA_DV
    n_sub = c // sub
    t = pl.program_id(1)

    @pl.when(t == 0)
    def _():
        s_scr[...] = s0_ref[0]

    z = jnp.dot(small_ref[:, 0:GLA_RANK].astype(BF16), w2_ref[...].astype(BF16),
                preferred_element_type=F32) + b2_ref[...]
    g_all = (jnp.minimum(z, 0.0) - jnp.log1p(jnp.exp(-jnp.abs(z)))) / GLA_GATE_NORM
    ri = lax.broadcasted_iota(jnp.int32, (c, c), 0)
    ci = lax.broadcasted_iota(jnp.int32, (c, c), 1)
    tril = ri >= ci
    cum = tril.astype(F32)
    rsub = lax.broadcasted_iota(jnp.int32, (c, dk), 0) // sub
    eye = lax.broadcasted_iota(jnp.int32, (dk, dk), 0) == lax.broadcasted_iota(jnp.int32, (dk, dk), 1)
    for cc in range(tb // c):
        rows = slice(cc * c, (cc + 1) * c)
        b_all = jnp.dot(cum, g_all[rows], preferred_element_type=F32, precision=lax.Precision.HIGHEST)
        for h in range(GLA_HEADS):
            b = b_all[:, h * dk:(h + 1) * dk]
            qh = q_ref[rows, h * dk:(h + 1) * dk] * (dk ** -0.5)
            kh = k_ref[rows, h * dk:(h + 1) * dk]
            vh = v_ref[rows, h * dv:(h + 1) * dv]
            a_rows = []
            for i in range(n_sub):
                ref = b[sub * i - 1:sub * i, :] if i else jnp.zeros((1, dk), F32)
                rs = slice(sub * i, sub * (i + 1))
                qi = (qh[rs] * jnp.exp(b[rs] - ref)).astype(BF16)
                ki = jnp.where(rsub <= i, kh * jnp.exp(ref - b), 0.0).astype(BF16)
                a_rows.append(lax.dot_general(qi, ki, (((1,), (1,)), ((), ())), preferred_element_type=F32))
            a = jnp.where(tril, jnp.concatenate(a_rows, axis=0), 0.0)
            s_old = s_scr[h]
            o = jnp.dot(a.astype(BF16), vh.astype(BF16), preferred_element_type=F32)
            o += jnp.dot((qh * jnp.exp(b)).astype(BF16), s_old.astype(BF16), preferred_element_type=F32)
            b_last = b[c - 1:c, :]
            ke = (kh * jnp.exp(b_last - b)).astype(BF16)
            upd = lax.dot_general(ke, vh.astype(BF16), (((0,), (0,)), ((), ())), preferred_element_type=F32)
            decay = jnp.exp(jnp.sum(jnp.where(eye, jnp.broadcast_to(b_last, (dk, dk)), 0.0), axis=1, keepdims=True))
            s_scr[h] = decay * s_old + upd
            o = o * lax.rsqrt(jnp.mean(o * o, axis=-1, keepdims=True) + RMS_EPS)
            y = o * ng_ref[:, h * dv:(h + 1) * dv] * jax.nn.silu(r_ref[rows, h * dv:(h + 1) * dv])
            y_ref[rows, h * dv:(h + 1) * dv] = y.astype(BF16)

    @pl.when(t == pl.num_programs(1) - 1)
    def _():
        sf_ref[0] = s_scr[...]


def gla_mix(p, s0, w2, b2, norm_g, n_batch, t_len):
    tb = _pick_tile(t_len, (256, 128, 64))
    nt = t_len // tb
    row = lambda w, off: pl.BlockSpec((tb, w), lambda b, i: (b * nt + i, off // w))
    const = lambda shape: pl.BlockSpec(shape, lambda b, i: (0,) * len(shape))
    return pl.pallas_call(
        _gla_kernel,
        grid=(n_batch, nt),
        in_specs=[row(256, COL_GLA_Q), row(256, COL_GLA_K), row(512, COL_GLA_V), row(512, COL_GLA_R),
                  row(128, COL_SMALL), const((GLA_RANK, GLA_HEADS * GLA_DK)), const((1, GLA_HEADS * GLA_DK)),
                  const((1, GLA_WIDTH)),
                  pl.BlockSpec((1, GLA_HEADS, GLA_DK, GLA_DV), lambda b, i: (b, 0, 0, 0))],
        out_specs=(pl.BlockSpec((tb, GLA_WIDTH), lambda b, i: (b * nt + i, 0)),
                   pl.BlockSpec((1, GLA_HEADS, GLA_DK, GLA_DV), lambda b, i: (b, 0, 0, 0))),
        out_shape=(jax.ShapeDtypeStruct((n_batch * t_len, GLA_WIDTH), BF16),
                   jax.ShapeDtypeStruct((n_batch, GLA_HEADS, GLA_DK, GLA_DV), F32)),
        scratch_shapes=[pltpu.VMEM((GLA_HEADS, GLA_DK, GLA_DV), F32)],
        compiler_params=_params("arbitrary", "arbitrary"),
        name="gla_mix",
    )(p, p, p, p, p, w2, b2.reshape(1, -1), norm_g.reshape(1, -1), s0)


def _pool_kernel(u_ref, prev_ref, cnt_ref, w_ref, sc_ref, y_ref, halo):
    tb = u_ref.shape[0]
    gd = POOL_GROUP_DIM

    @pl.when(pl.program_id(1) == 0)
    def _():
        halo[...] = prev_ref[0]

    ext = jnp.concatenate([halo[...], u_ref[...]], axis=0)
    halo[...] = ext[tb:tb + POOL_MAX]
    for gi, w in enumerate(POOL_WINDOWS):
        x = ext[:, gi * gd:(gi + 1) * gd]
        s = x
        shift = 1
        while shift < w:
            s = s + pltpu.roll(s, shift, 0)
            shift *= 2
        pooled = s[POOL_MAX:] / cnt_ref[:, gi:gi + 1] - x[POOL_MAX:]
        y = jnp.dot(pooled.astype(BF16), w_ref[gi].astype(BF16), preferred_element_type=F32)
        y_ref[:, gi * gd:(gi + 1) * gd] = (y * sc_ref[:, gi * gd:(gi + 1) * gd]).astype(BF16)


def pool_mix(p, prev, pos0, w_pool, scale, n_batch, t_len):
    tb = _pick_tile(t_len, (512, 256, 128, 64, 32, 16))
    nt = t_len // tb
    pos = pos0 + jnp.arange(t_len, dtype=jnp.int32)
    cnt = jnp.stack([jnp.minimum(pos + 1, w).astype(F32) for w in POOL_WINDOWS], axis=1)
    cnt = jnp.pad(cnt, ((0, 0), (0, 128 - POOL_GROUPS)), constant_values=1.0)
    prev16 = jnp.pad(prev.astype(F32), ((0, 0), (1, 0), (0, 0)))
    return pl.pallas_call(
        _pool_kernel,
        grid=(n_batch, nt),
        in_specs=[pl.BlockSpec((tb, POOL_WIDTH), lambda b, i: (b * nt + i, COL_POOL // POOL_WIDTH)),
                  pl.BlockSpec((1, POOL_MAX, POOL_WIDTH), lambda b, i: (b, 0, 0)),
                  pl.BlockSpec((tb, 128), lambda b, i: (i, 0)),
                  pl.BlockSpec((POOL_GROUPS, POOL_GROUP_DIM, POOL_GROUP_DIM), lambda b, i: (0, 0, 0)),
                  pl.BlockSpec((1, POOL_WIDTH), lambda b, i: (0, 0))],
        out_specs=pl.BlockSpec((tb, POOL_WIDTH), lambda b, i: (b * nt + i, 0)),
        out_shape=jax.ShapeDtypeStruct((n_batch * t_len, POOL_WIDTH), BF16),
        scratch_shapes=[pltpu.VMEM((POOL_MAX, POOL_WIDTH), F32)],
        compiler_params=_params("arbitrary", "arbitrary"),
        name="pool_mix",
    )(p, prev16, cnt, w_pool, scale.reshape(1, -1))


def _layer_norm_rows(x, g, b):
    xc = x - jnp.mean(x, axis=-1, keepdims=True)
    var = jnp.mean(xc * xc, axis=-1, keepdims=True)
    return xc * lax.rsqrt(var + LN_EPS) * g + b


def _outproj_kernel(x_ref, yg_ref, yn_ref, yp_ref, w_ref, g_ref, b_ref, *rest):
    h = jnp.dot(yg_ref[...], w_ref[0:GLA_WIDTH, :], preferred_element_type=F32)
    h += jnp.dot(yn_ref[...], w_ref[GLA_WIDTH:GLA_WIDTH + NSA_WIDTH, :], preferred_element_type=F32)
    h += jnp.dot(yp_ref[...], w_ref[GLA_WIDTH + NSA_WIDTH:, :], preferred_element_type=F32)
    x1 = _layer_norm_rows(ALPHA * x_ref[...] + h, g_ref[...], b_ref[...])
    if len(rest) == 1:
        rest[0][...] = x1
    else:
        rh_ref, rl_ref, o_ref, lg_ref = rest
        o_ref[...] = x1
        xh = x1.astype(BF16)
        xl = (x1 - xh.astype(F32)).astype(BF16)
        lg_ref[...] = (jnp.dot(xh, rh_ref[...], preferred_element_type=F32)
                       + jnp.dot(xl, rh_ref[...], preferred_element_type=F32)
                       + jnp.dot(xh, rl_ref[...], preferred_element_type=F32))


def router_split(router):
    r = jnp.pad(router, ((0, 0), (0, 128 - router.shape[1])))
    hi = r.astype(BF16)
    return hi, (r - hi.astype(F32)).astype(BF16)


def outproj_ln(x, y_gla, y_nsa, y_pool, w_out_bf16, g, b, router=None):
    n, d = x.shape
    tm = _pick_tile(n, (512, 256, 128, 64, 32, 16, 8))
    row = lambda w: pl.BlockSpec((tm, w), lambda i: (i, 0))
    const = lambda r, c: pl.BlockSpec((r, c), lambda i: (0, 0))
    in_specs = [row(d), row(GLA_WIDTH), row(NSA_WIDTH), row(POOL_WIDTH), const(d, d), const(1, d), const(1, d)]
    args = [x, y_gla, y_nsa, y_pool, w_out_bf16, g.reshape(1, -1), b.reshape(1, -1)]
    out_specs, out_shape = row(d), jax.ShapeDtypeStruct((n, d), F32)
    if router is not None:
        in_specs += [const(d, 128), const(d, 128)]
        args += list(router_split(router))
        out_specs, out_shape = (out_specs, row(128)), (out_shape, jax.ShapeDtypeStruct((n, 128), F32))
    return pl.pallas_call(
        _outproj_kernel,
        grid=(n // tm,),
        in_specs=in_specs,
        out_specs=out_specs,
        out_shape=out_shape,
        compiler_params=_params("arbitrary"),
        name="outproj_ln",
    )(*args)


MOE_TM = 256
MOE_TF = 1024
MOE_TN = 512
ROUTE_TM = 512
PERMUTE_CHUNK = 1024


def _route_kernel(lg_ref, ii_ref, gf_ref, cnt_ref, carry, *, n_valid):
    tm = lg_ref.shape[0]
    i = pl.program_id(0)

    @pl.when(i == 0)
    def _():
        carry[...] = jnp.zeros_like(carry)

    lane = lax.broadcasted_iota(jnp.int32, (tm, 128), 1)
    valid = (i * tm + lax.broadcasted_iota(jnp.int32, (tm, 128), 0)) < n_valid
    lg = jnp.where(lane < N_EXPERTS, lg_ref[...], -jnp.inf)
    m1 = jnp.max(lg, axis=-1, keepdims=True)
    i1 = jnp.min(jnp.where(lg == m1, lane, 128), axis=-1, keepdims=True)
    lg2 = jnp.where(lane == i1, -jnp.inf, lg)
    m2 = jnp.max(lg2, axis=-1, keepdims=True)
    i2 = jnp.min(jnp.where(lg2 == m2, lane, 128), axis=-1, keepdims=True)
    t = jnp.exp(m2 - m1)
    g1 = 1.0 / (1.0 + t)
    g2 = t / (1.0 + t)
    oh1 = jnp.where((lane == i1) & valid, 1.0, 0.0)
    oh2 = jnp.where((lane == i2) & valid, 1.0, 0.0)
    cnt = oh1 + oh2
    strict = (lax.broadcasted_iota(jnp.int32, (tm, tm), 0) > lax.broadcasted_iota(jnp.int32, (tm, tm), 1))
    before = jnp.dot(strict.astype(BF16), cnt.astype(BF16), preferred_element_type=F32) + carry[...]
    r1 = jnp.sum(before * oh1, axis=-1, keepdims=True).astype(jnp.int32)
    r2 = jnp.sum(before * oh2, axis=-1, keepdims=True).astype(jnp.int32)
    carry[...] += jnp.sum(cnt, axis=0, keepdims=True)
    ii_ref[...] = jnp.where(lane == 0, i1, jnp.where(lane == 1, i2, jnp.where(lane == 2, r1, r2)))
    gf_ref[...] = jnp.where(lane == 0, g1, g2)
    cnt_ref[...] = carry[...]


def moe_route(logits, n_valid):
    npad = logits.shape[0]
    tm = ROUTE_TM
    row = pl.BlockSpec((tm, 128), lambda i: (i, 0))
    info, gates, counts = pl.pallas_call(
        functools.partial(_route_kernel, n_valid=n_valid),
        grid=(npad // tm,),
        in_specs=[row],
        out_specs=(row, row, pl.BlockSpec((1, 128), lambda i: (0, 0))),
        out_shape=(jax.ShapeDtypeStruct((npad, 128), jnp.int32), jax.ShapeDtypeStruct((npad, 128), F32),
                   jax.ShapeDtypeStruct((1, 128), F32)),
        scratch_shapes=[pltpu.VMEM((1, 128), F32)],
        compiler_params=_params("arbitrary"),
        name="moe_route",
    )(logits)
    return info[:, 0:2], info[:, 2:4], gates, counts[0, :N_EXPERTS].astype(jnp.int32)


SLAB = (16, 128)


def _slabify_kernel(x_ref, *rest):
    o_ref = rest[-1]

    def slab_rows(src_ref, n_rows):
        for c in range(SLAB[0]):
            o_ref[0:n_rows, c, :] = src_ref[:, c * SLAB[1]:(c + 1) * SLAB[1]]

    if len(rest) == 1:
        slab_rows(x_ref, x_ref.shape[0])
        return
    t_ref = rest[0]
    last = pl.program_id(0) == pl.num_programs(0) - 1

    @pl.when(jnp.logical_not(last))
    def _():
        slab_rows(x_ref, x_ref.shape[0])

    @pl.when(last)
    def _():
        o_ref[...] = jnp.zeros_like(o_ref)
        slab_rows(t_ref, t_ref.shape[0])


def slabify(x, tail=None):
    n, d = x.shape
    tm = _pick_tile(n, (512, 256, 128, 64, 32, 16, 8))
    nt = n // tm
    in_specs = [pl.BlockSpec((tm, d), lambda i: (jnp.minimum(i, nt - 1), 0))]
    args = [x]
    if tail is not None:
        assert tail.shape[0] <= tm
        in_specs.append(pl.BlockSpec(tail.shape, lambda i: (0, 0)))
        args.append(tail)
    steps = nt + (tail is not None)
    return pl.pallas_call(
        _slabify_kernel,
        grid=(steps,),
        in_specs=in_specs,
        out_specs=pl.BlockSpec((tm,) + SLAB, lambda i: (i, 0, 0)),
        out_shape=jax.ShapeDtypeStruct((steps * tm,) + SLAB, x.dtype),
        compiler_params=_params("arbitrary"),
        name="slabify",
    )(*args)


def _unslab(ref):
    return jnp.concatenate([ref[:, c, :] for c in range(SLAB[0])], axis=1)


GATHER_UNROLL = 8


def _gather_slabs_kernel(idx_ref, src_ref, o_ref, sem):
    ch = idx_ref.shape[2]

    def row_copy(r):
        return pltpu.make_async_copy(src_ref.at[idx_ref[0, 0, r]], o_ref.at[r], sem)

    def issue(t, c):
        for u in range(GATHER_UNROLL):
            row_copy(t * GATHER_UNROLL + u).start(priority=u % 2)
        return c

    lax.fori_loop(0, ch // GATHER_UNROLL, issue, 0)

    def drain(t, c):
        for u in range(GATHER_UNROLL):
            row_copy(t * GATHER_UNROLL + u).wait()
        return c

    lax.fori_loop(0, ch // GATHER_UNROLL, drain, 0)


def gather_slabs(src, idx):
    n = idx.shape[0]
    ch = PERMUTE_CHUNK
    return pl.pallas_call(
        _gather_slabs_kernel,
        grid=(n // ch,),
        in_specs=[pl.BlockSpec((1, 1, ch), lambda i: (i, 0, 0), memory_space=pltpu.SMEM),
                  pl.BlockSpec(memory_space=pl.ANY)],
        out_specs=pl.BlockSpec((ch,) + SLAB, lambda i: (i, 0, 0)),
        out_shape=jax.ShapeDtypeStruct((n,) + SLAB, src.dtype),
        scratch_shapes=[pltpu.SemaphoreType.DMA(())],
        compiler_params=_params("arbitrary"),
        name="gather_slabs",
    )(idx.reshape(n // ch, 1, ch), src)


def _moe_up_kernel(te_ref, tfirst_ref, tused_ref, x_ref, wg_ref, wu_ref, h_ref, wgb, wub):
    i = pl.program_id(1)

    @pl.when((i == 0) | (tfirst_ref[i] == 1))
    def _():
        wgb[...] = wg_ref[...].astype(BF16)
        wub[...] = wu_ref[...].astype(BF16)

    @pl.when(tused_ref[i] == 1)
    def _():
        xb = _unslab(x_ref).astype(BF16)
        gate = jnp.dot(xb, wgb[...], preferred_element_type=F32)
        up = jnp.dot(xb, wub[...], preferred_element_type=F32)
        h_ref[...] = (jax.nn.silu(gate) * up).astype(BF16)

    @pl.when(tused_ref[i] == 0)
    def _():
        h_ref[...] = jnp.zeros_like(h_ref)


def _moe_down_kernel(te_ref, tfirst_ref, tused_ref, h_ref, wd_ref, y_ref, wdb):
    i = pl.program_id(1)

    @pl.when((i == 0) | (tfirst_ref[i] == 1))
    def _():
        wdb[...] = wd_ref[...].astype(BF16)

    @pl.when(tused_ref[i] == 1)
    def _():
        y_ref[...] = jnp.dot(h_ref[...], wdb[...], preferred_element_type=F32)

    @pl.when(tused_ref[i] == 0)
    def _():
        y_ref[...] = jnp.zeros_like(y_ref)


def moe_experts(xs, tile_e, tile_first, tile_used, wg, wu, wd):
    r = xs.shape[0]
    d, d_ff = wg.shape[1], wg.shape[2]
    tm, tf, tn = MOE_TM, MOE_TF, MOE_TN
    n_tiles = r // tm
    h = pl.pallas_call(
        _moe_up_kernel,
        grid_spec=pltpu.PrefetchScalarGridSpec(
            num_scalar_prefetch=3,
            grid=(d_ff // tf, n_tiles),
            in_specs=[pl.BlockSpec((tm,) + SLAB, lambda j, i, te, t1, tu: (i, 0, 0)),
                      pl.BlockSpec((None, d, tf), lambda j, i, te, t1, tu: (te[i], 0, j)),
                      pl.BlockSpec((None, d, tf), lambda j, i, te, t1, tu: (te[i], 0, j))],
            out_specs=pl.BlockSpec((tm, tf), lambda j, i, te, t1, tu: (i, j)),
            scratch_shapes=[pltpu.VMEM((d, tf), BF16), pltpu.VMEM((d, tf), BF16)]),
        out_shape=jax.ShapeDtypeStruct((r, d_ff), BF16),
        compiler_params=_params("arbitrary", "arbitrary"),
        name="moe_up",
    )(tile_e, tile_first, tile_used, xs, wg, wu)
    return pl.pallas_call(
        _moe_down_kernel,
        grid_spec=pltpu.PrefetchScalarGridSpec(
            num_scalar_prefetch=3,
            grid=(d // tn, n_tiles),
            in_specs=[pl.BlockSpec((tm, d_ff), lambda j, i, te, t1, tu: (i, 0)),
                      pl.BlockSpec((None, d_ff, tn), lambda j, i, te, t1, tu: (te[i], 0, j))],
            out_specs=pl.BlockSpec((tm, tn), lambda j, i, te, t1, tu: (i, j)),
            scratch_shapes=[pltpu.VMEM((d_ff, tn), BF16)]),
        out_shape=jax.ShapeDtypeStruct((r, d), F32),
        compiler_params=_params("arbitrary", "arbitrary"),
        name="moe_down",
    )(tile_e, tile_first, tile_used, h, wd)


def _moe_combine_kernel(x_ref, y0_ref, y1_ref, gt_ref, g_ref, b_ref, o_ref):
    gt = gt_ref[...]
    y = gt[:, 0:1] * _unslab(y0_ref) + gt[:, 1:2] * _unslab(y1_ref)
    o_ref[...] = _layer_norm_rows(ALPHA * x_ref[...] + y, g_ref[...], b_ref[...])


def moe_combine_ln(x, yg, gates, row0, n_tok_pad, g, b):
    n, d = x.shape
    tm = _pick_tile(n, (512, 256, 128, 64, 32, 16, 8))
    o0, o1 = row0 // tm, (n_tok_pad + row0) // tm
    return pl.pallas_call(
        _moe_combine_kernel,
        grid=(n // tm,),
        in_specs=[pl.BlockSpec((tm, d), lambda i: (i, 0)),
                  pl.BlockSpec((tm,) + SLAB, lambda i: (o0 + i, 0, 0)),
                  pl.BlockSpec((tm,) + SLAB, lambda i: (o1 + i, 0, 0)),
                  pl.BlockSpec((tm, 128), lambda i: (o0 + i, 0)),
                  pl.BlockSpec((1, d), lambda i: (0, 0)), pl.BlockSpec((1, d), lambda i: (0, 0))],
        out_specs=pl.BlockSpec((tm, d), lambda i: (i, 0)),
        out_shape=jax.ShapeDtypeStruct((n, d), F32),
        compiler_params=_params("arbitrary"),
        name="moe_combine_ln",
    )(x, yg, yg, gates, g.reshape(1, -1), b.reshape(1, -1))


def moe_ln(x_groups, logit_groups, wg, wu, wd, g, b):
    d = x_groups[0].shape[1]
    n_tok = sum(x.shape[0] for x in x_groups)
    n_tok_pad = -(-n_tok // PERMUTE_CHUNK) * PERMUTE_CHUNK
    n_tok_pad = -(-n_tok_pad // ROUTE_TM) * ROUTE_TM
    logits = jnp.concatenate(logit_groups + [jnp.zeros((n_tok_pad - n_tok, 128), F32)], axis=0)
    experts, ranks, gates, counts = moe_route(logits, n_tok)
    tm = MOE_TM
    n_tiles = -(-(n_tok * TOP_K + N_EXPERTS * (tm - 1)) // tm)
    n_tiles = -(-n_tiles * tm // PERMUTE_CHUNK) * PERMUTE_CHUNK // tm
    padded = (counts + tm - 1) // tm * tm
    pad_end = jnp.cumsum(padded)
    pad_start = pad_end - padded
    valid = (jnp.arange(n_tok_pad) < n_tok)[:, None]
    dest = jnp.where(valid, pad_start[experts] + ranks, 0)
    tok = jnp.broadcast_to(jnp.arange(n_tok_pad, dtype=jnp.int32)[:, None], dest.shape)
    row_tok = jnp.zeros((n_tiles * tm,), jnp.int32).at[jnp.where(valid, dest, n_tiles * tm).reshape(-1)].set(
        tok.reshape(-1), mode='drop')
    tile_start = jnp.arange(n_tiles, dtype=jnp.int32) * tm
    tile_e = jnp.minimum(jnp.searchsorted(pad_end, tile_start, side='right'), N_EXPERTS - 1).astype(jnp.int32)
    tile_used = (tile_start < pad_end[-1]).astype(jnp.int32)
    tile_first = jnp.concatenate([jnp.ones((1,), jnp.int32), (tile_e[1:] != tile_e[:-1]).astype(jnp.int32)])
    assert len(x_groups) == 2
    xs = gather_slabs(slabify(x_groups[0], x_groups[1]), row_tok)
    ys = slabify(moe_experts(xs, tile_e, tile_first, tile_used, wg, wu, wd))
    yg = gather_slabs(ys, jnp.concatenate([dest[:, 0], dest[:, 1]]).astype(jnp.int32))
    outs, row0 = [], 0
    for x in x_groups:
        outs.append(moe_combine_ln(x, yg, gates, row0, n_tok_pad, g, b))
        row0 += x.shape[0]
    return outs


def _ffn_kernel(x_ref, wg_ref, wu_ref, wd_ref, g_ref, b_ref, o_ref, xb_ref, *, d_ff):
    j = pl.program_id(1)
    tf = wg_ref.shape[1]

    @pl.when(j == 0)
    def _():
        xb_ref[...] = x_ref[...].astype(BF16)
        o_ref[...] = jnp.zeros_like(o_ref)

    xb = xb_ref[...]
    gate = jnp.dot(xb, wg_ref[...], preferred_element_type=F32)
    up = jnp.dot(xb, wu_ref[...], preferred_element_type=F32)
    col = j * tf + lax.broadcasted_iota(jnp.int32, (1, tf), 1)
    a = jnp.where(col < d_ff, jax.nn.silu(gate) * up, 0.0).astype(BF16)
    rowi = j * tf + lax.broadcasted_iota(jnp.int32, (tf, 1), 0)
    wd = jnp.where(rowi < d_ff, wd_ref[...], jnp.zeros((), BF16))
    o_ref[...] += jnp.dot(a, wd, preferred_element_type=F32)

    @pl.when(j == pl.num_programs(1) - 1)
    def _():
        o_ref[...] = _layer_norm_rows(ALPHA * x_ref[...] + o_ref[...], g_ref[...], b_ref[...])


def ffn_ln(x, wg, wu, wd, g, b):
    n, d = x.shape
    d_ff = wg.shape[1]
    tm = _pick_tile(n, (512, 256, 128, 64, 32, 16, 8))
    tf = 512
    kern = functools.partial(_ffn_kernel, d_ff=d_ff)
    return pl.pallas_call(
        kern,
        grid=(n // tm, pl.cdiv(d_ff, tf)),
        in_specs=[pl.BlockSpec((tm, d), lambda i, j: (i, 0)),
                  pl.BlockSpec((d, tf), lambda i, j: (0, j)), pl.BlockSpec((d, tf), lambda i, j: (0, j)),
                  pl.BlockSpec((tf, d), lambda i, j: (j, 0)),
                  pl.BlockSpec((1, d), lambda i, j: (0, 0)), pl.BlockSpec((1, d), lambda i, j: (0, 0))],
        out_specs=pl.BlockSpec((tm, d), lambda i, j: (i, 0)),
        out_shape=jax.ShapeDtypeStruct((n, d), F32),
        scratch_shapes=[pltpu.VMEM((tm, d), BF16)],
        compiler_params=_params("arbitrary", "arbitrary"),
        name="ffn_ln",
    )(x, wg, wu, wd, g.reshape(1, -1), b.reshape(1, -1))


CMP_PAGES = 32
SEL_PAD_DEC = 384


def _cmp_paged_kernel(pt_ref, cache_ref, pe_ref, w1_ref, w2_ref, o_ref, buf, hlo, hhi, sems, *, layer, n_pages, page):
    b, ch = pl.program_id(0), pl.program_id(1)
    n_ch = pl.num_programs(1)
    hd = HEAD_DIM
    nh = CMP_PAGES * page // CMP_STRIDE
    step = b * n_ch + ch

    def copies(s, slot):
        first = s * CMP_PAGES
        return [pltpu.make_async_copy(
            cache_ref.at[layer, pt_ref[first + pg], :, c // NSA_KV_HEADS, c % NSA_KV_HEADS, :],
            buf.at[slot, c, pl.ds(pg * page, page), :], sems.at[slot])
            for pg in range(CMP_PAGES) for c in range(2 * NSA_KV_HEADS)]

    def for_slot(s, fn):
        for slot in range(2):
            @pl.when(s % 2 == slot)
            def _():
                fn(slot)

    @pl.when(step == 0)
    def _():
        for cp in copies(0, 0):
            cp.start()

    @pl.when(step + 1 < pl.num_programs(0) * n_ch)
    def _():
        for_slot(step + 1, lambda slot: [cp.start() for cp in copies(step + 1, slot)])

    for_slot(step, lambda slot: [cp.wait() for cp in copies(step, slot)])
    cur = step % 2

    for c in range(2 * NSA_KV_HEADS):
        kd, g = c // NSA_KV_HEADS, c % NSA_KV_HEADS
        xs = [buf[cur, c, pl.ds(j, nh, stride=CMP_STRIDE), :] for j in range(CMP_STRIDE)]
        lo = jnp.concatenate([(xs[j] + pe_ref[kd, j:j + 1, :]).astype(BF16) for j in range(CMP_STRIDE)], axis=1)
        hi = jnp.concatenate([(xs[j] + pe_ref[kd, CMP_STRIDE + j:CMP_STRIDE + j + 1, :]).astype(BF16)
                              for j in range(CMP_STRIDE)], axis=1)
        w_lo = w1_ref[kd, 0:CMP_STRIDE].reshape(CMP_STRIDE * hd, CMP_HIDDEN).astype(BF16)
        w_hi = w1_ref[kd, CMP_STRIDE:CMP_LEN].reshape(CMP_STRIDE * hd, CMP_HIDDEN).astype(BF16)
        rows = pl.ds(pl.multiple_of(ch * nh, nh), nh)
        hlo[c, rows, :] = jnp.dot(lo, w_lo, preferred_element_type=F32)
        hhi[c, rows, :] = jnp.dot(hi, w_hi, preferred_element_type=F32)

    @pl.when(ch == pl.num_programs(1) - 1)
    def _():
        n_all = hlo.shape[1]
        for c in range(2 * NSA_KV_HEADS):
            kd, g = c // NSA_KV_HEADS, c % NSA_KV_HEADS
            h = jax.nn.gelu(hlo[c] + pltpu.roll(hhi[c], n_all - 1, 0))
            o_ref[kd, g] = jnp.dot(h.astype(BF16), w2_ref[kd].astype(BF16), preferred_element_type=F32).astype(BF16)


def nsa_compress_paged(cache, layer, page_table, cmp_pos, cmp_w1, cmp_w2):
    page = cache.shape[2]
    n_batch, n_pages = page_table.shape
    hd = HEAD_DIM
    n_all = n_pages * page // CMP_STRIDE
    kern = functools.partial(_cmp_paged_kernel, layer=layer, n_pages=n_pages, page=page)
    const = lambda shape: pl.BlockSpec(shape, lambda b, c, pt: (0,) * len(shape))
    return pl.pallas_call(
        kern,
        grid_spec=pltpu.PrefetchScalarGridSpec(
            num_scalar_prefetch=1,
            grid=(n_batch, n_pages // CMP_PAGES),
            in_specs=[pl.BlockSpec(memory_space=pl.ANY), const((2, CMP_LEN, hd)),
                      const((2, CMP_LEN, hd, CMP_HIDDEN)), const((2, CMP_HIDDEN, hd))],
            out_specs=pl.BlockSpec((None, 2, NSA_KV_HEADS, n_all, hd), lambda b, c, pt: (b, 0, 0, 0, 0)),
            scratch_shapes=[pltpu.VMEM((2, 2 * NSA_KV_HEADS, CMP_PAGES * page, hd), F32),
                            pltpu.VMEM((2 * NSA_KV_HEADS, n_all, CMP_HIDDEN), F32),
                            pltpu.VMEM((2 * NSA_KV_HEADS, n_all, CMP_HIDDEN), F32),
                            pltpu.SemaphoreType.DMA((2,))]),
        out_shape=jax.ShapeDtypeStruct((n_batch, 2, NSA_KV_HEADS, n_all, hd), BF16),
        compiler_params=_params("arbitrary", "arbitrary"),
        name="nsa_compress_paged",
    )(page_table.reshape(-1).astype(jnp.int32), cache, cmp_pos, cmp_w1, cmp_w2)


def _sel_decode_kernel(qn_ref, kc_ref, vc_ref, ovt_ref, oc_ref, idx_ref, v_scr, psum_scr, *, n_cmp, n_sel, n_top,
                       q_pos):
    b = pl.program_id(0)
    n_cmp_pad = kc_ref.shape[2]
    nsp = ovt_ref.shape[0]
    n_rows = qn_ref.shape[0]
    col = lax.broadcasted_iota(jnp.int32, (n_rows, n_cmp_pad), 1)
    mask = (col * CMP_STRIDE + (CMP_LEN - 1) <= q_pos) & (col < n_cmp)
    psums = []
    for g in range(NSA_KV_HEADS):
        own = slice(g * NSA_REP, (g + 1) * NSA_REP)
        s = lax.dot_general(qn_ref[...], kc_ref[0, g], (((1,), (1,)), ((), ())), preferred_element_type=F32)
        s = jnp.where(mask, s, -jnp.inf)
        m = jnp.max(s, axis=-1, keepdims=True)
        m = jnp.where(m > -jnp.inf, m, 0.0)
        p = jnp.where(mask, jnp.exp(s - m), 0.0)
        p = p / jnp.maximum(jnp.sum(p, axis=-1, keepdims=True), 1e-30)
        oc = jnp.dot(p.astype(BF16), vc_ref[0, g], preferred_element_type=F32)
        oc_ref[own, :] = oc[own]
        psum_scr[pl.ds(b * NSA_KV_HEADS + g, 1), :] = jnp.sum(p[own], axis=0, keepdims=True)

    @pl.when(b == pl.num_programs(0) - 1)
    def _():
        n_col = psum_scr.shape[0]
        imp = lax.dot_general(ovt_ref[...], psum_scr[...], (((1,), (1,)), ((), ())), preferred_element_type=F32,
                              precision=lax.Precision.HIGHEST)
        blk = lax.broadcasted_iota(jnp.int32, (nsp, n_col), 0)
        cur = q_pos // SEL_BLOCK
        forced = (blk == 0) | (blk == cur) | (blk == cur - 1)
        v = jnp.where((blk <= cur) & (blk < n_sel), jnp.where(forced, jnp.inf, imp), -jnp.inf)
        v_scr[...] = v

        def count(i, rank):
            vi = v_scr[pl.ds(i, 1), :]
            ahead = (vi > v) | ((vi == v) & (blk > i))
            return rank + ahead.astype(jnp.int32)

        rank = lax.fori_loop(0, n_sel, count, jnp.zeros((nsp, n_col), jnp.int32))
        chosen = (rank < n_top) & (v > -jnp.inf)
        blk_f = blk.astype(F32)
        rows = [jnp.sum(jnp.where(chosen & (rank == t), blk_f, 0.0), axis=0, keepdims=True) for t in range(n_top)]
        idx_ref[...] = jnp.concatenate(rows, axis=0).astype(jnp.int32)


def nsa_select_decode(qn, cmp_kv, n_cmp, n_sel, q_pos):
    n_batch, n_heads, hd = qn.shape
    n_cmp_pad = cmp_kv.shape[3]
    nsp = SEL_PAD_DEC
    n_top = min(SEL_TOPN, n_sel)
    ci = jnp.arange(n_cmp_pad)[None, :]
    sj = jnp.arange(nsp)[:, None]
    overlap_t = ((ci * CMP_STRIDE <= sj * SEL_BLOCK + SEL_BLOCK - 1) &
                 (ci * CMP_STRIDE + CMP_LEN - 1 >= sj * SEL_BLOCK) & (ci < n_cmp) & (sj < n_sel)).astype(F32)
    kern = functools.partial(_sel_decode_kernel, n_cmp=n_cmp, n_sel=n_sel, n_top=n_top, q_pos=q_pos)
    kv = lambda kd: pl.BlockSpec((None, 1, NSA_KV_HEADS, n_cmp_pad, hd), lambda b: (b, kd, 0, 0, 0))
    n_col = n_batch * NSA_KV_HEADS
    o_cmp, idx = pl.pallas_call(
        kern,
        grid=(n_batch,),
        in_specs=[pl.BlockSpec((None, 2 * n_heads, hd), lambda b: (b, 0, 0)), kv(0), kv(1),
                  pl.BlockSpec((nsp, n_cmp_pad), lambda b: (0, 0))],
        out_specs=(pl.BlockSpec((None, n_heads, hd), lambda b: (b, 0, 0)),
                   pl.BlockSpec((n_top, n_col), lambda b: (0, 0))),
        out_shape=(jax.ShapeDtypeStruct((n_batch, n_heads, hd), F32),
                   jax.ShapeDtypeStruct((n_top, n_col), jnp.int32)),
        scratch_shapes=[pltpu.VMEM((nsp, n_col), F32), pltpu.VMEM((n_col, n_cmp_pad), F32)],
        compiler_params=_params("arbitrary"),
        name="nsa_select_decode",
    )(jnp.pad(qn, ((0, 0), (0, n_heads), (0, 0))), cmp_kv, cmp_kv, overlap_t)
    return o_cmp, idx.T.reshape(n_batch, NSA_KV_HEADS, n_top)


def _attn_decode_kernel(pt_ref, sel_ref, q_ref, k0_ref, v0_ref, k1_ref, v1_ref, new_ref, kw_ref, vw_ref, wnew_ref,
                        oc_ref, gate_ref, o_ref, m_scr, l_scr, acc_scr, *, n_past_blocks, n_top):
    b, slot = pl.program_id(0), pl.program_id(1)
    G = NSA_KV_HEADS
    nt = (((1,), (1,)), ((), ()))
    kv_refs = ((k0_ref, v0_ref), (k1_ref, v1_ref))
    for g in range(G):
        qb = q_ref[g]
        q = qb.astype(F32)
        kc_ref, vc_ref = kv_refs[g]

        @pl.when(slot == 0)
        def _():
            k_new = new_ref[2 * G + g:2 * G + g + 1, :]
            v_new = new_ref[3 * G + g:3 * G + g + 1, :]
            m_scr[g] = jnp.sum(q * k_new, axis=-1, keepdims=True)
            l_scr[g] = jnp.ones(l_scr.shape[1:], F32)
            acc_scr[g] = jnp.broadcast_to(v_new, acc_scr.shape[1:])

        @pl.when(sel_ref[(b * G + g) * n_top + slot] < n_past_blocks)
        def _():
            s = lax.dot_general(qb, kc_ref[:, g, :].astype(BF16), nt, preferred_element_type=F32)
            m_old = m_scr[g]
            m_new = jnp.maximum(m_old, jnp.max(s, axis=-1, keepdims=True))
            alpha = jnp.exp(m_old - m_new)
            p = jnp.exp(s - m_new)
            l_scr[g] = alpha * l_scr[g] + jnp.sum(p, axis=-1, keepdims=True)
            acc_scr[g] = alpha * acc_scr[g] + jnp.dot(p.astype(BF16), vc_ref[:, g, :].astype(BF16),
                                                      preferred_element_type=F32)
            m_scr[g] = m_new

        @pl.when(slot == n_top - 1)
        def _():
            o_sel = acc_scr[g] / jnp.maximum(l_scr[g], 1e-30)
            n_buf = kw_ref.shape[0]
            s = lax.dot_general(qb, kw_ref[:, g, :].astype(BF16), nt, preferred_element_type=F32)
            keep = lax.broadcasted_iota(jnp.int32, s.shape, 1) > n_buf - WINDOW
            s = jnp.where(keep, s, NEG_BIG)
            kw_new = wnew_ref[g:g + 1, :]
            vw_new = wnew_ref[G + g:G + g + 1, :]
            s_new = jnp.sum(q * kw_new, axis=-1, keepdims=True)
            m = jnp.maximum(jnp.max(s, axis=-1, keepdims=True), s_new)
            p = jnp.exp(s - m)
            p_new = jnp.exp(s_new - m)
            l = jnp.sum(p, axis=-1, keepdims=True) + p_new
            o_win = (jnp.dot(p.astype(BF16), vw_ref[:, g, :].astype(BF16), preferred_element_type=F32)
                     + p_new * vw_new) / l
            n_rows = qb.shape[0]
            gates = jnp.broadcast_to(gate_ref[g:g + 1, :], (n_rows, 128))
            lane = lax.broadcasted_iota(jnp.int32, (n_rows, 128), 1)
            head = lax.broadcasted_iota(jnp.int32, (n_rows, 128), 0)
            pick = lambda c: jnp.sum(jnp.where(lane == head * 3 + c, gates, 0.0), axis=-1, keepdims=True)
            o_ref[g] = (pick(0) * oc_ref[g] + pick(1) * o_sel + pick(2) * o_win).astype(o_ref.dtype)


def nsa_attend_decode(qr, o_cmp, sel_idx, gates, cache, layer, page_table, new_rows, state_win, new_win):
    n_batch, n_heads, hd = qr.shape
    page = cache.shape[2]
    n_pages = page_table.shape[1]
    n_top = sel_idx.shape[2]
    per_page = page // SEL_BLOCK
    n_past_blocks = n_pages * per_page
    n_buf = state_win.shape[2]
    G, R = NSA_KV_HEADS, NSA_REP

    def cache_spec(kind, g):
        def index(b, s, pt, sel):
            j = jnp.minimum(sel[(b * G + g) * n_top + s], n_past_blocks - 1)
            return layer, pt[b * n_pages + j // per_page], j % per_page, kind, 0, 0
        return pl.BlockSpec((None, None, SEL_BLOCK, None, G, hd), index)

    rp = 16
    pad_heads = lambda a: jnp.pad(a.reshape(n_batch, G, R, hd), ((0, 0), (0, 0), (0, rp - R), (0, 0)))
    per_bg = lambda: pl.BlockSpec((None, G, rp, hd), lambda b, s, pt, sel: (b, 0, 0, 0))
    per_b = lambda rows: pl.BlockSpec((None, rows, hd), lambda b, s, pt, sel: (b, 0, 0))
    win_spec = lambda kv: pl.BlockSpec((None, None, n_buf, None, G, hd),
                                       lambda b, s, pt, sel: (layer, b, 0, kv, 0, 0))
    kern = functools.partial(_attn_decode_kernel, n_past_blocks=n_past_blocks, n_top=n_top)
    out = pl.pallas_call(
        kern,
        grid_spec=pltpu.PrefetchScalarGridSpec(
            num_scalar_prefetch=2,
            grid=(n_batch, n_top),
            in_specs=[per_bg(), cache_spec(2, 0), cache_spec(3, 0), cache_spec(2, 1), cache_spec(3, 1),
                      per_b(4 * G), win_spec(0), win_spec(1), per_b(2 * G), per_bg(), per_b(G)],
            out_specs=per_bg(),
            scratch_shapes=[pltpu.VMEM((G, rp, 1), F32), pltpu.VMEM((G, rp, 1), F32), pltpu.VMEM((G, rp, hd), F32)]),
        out_shape=jax.ShapeDtypeStruct((n_batch, G, rp, hd), BF16),
        compiler_params=_params("arbitrary", "arbitrary"),
        name="nsa_attend_decode",
    )(page_table.reshape(-1).astype(jnp.int32), sel_idx.reshape(-1).astype(jnp.int32),
      pad_heads(qr), cache, cache, cache, cache, new_rows, state_win, state_win, new_win, pad_heads(o_cmp), gates)
    return out[:, :, :R].reshape(n_batch, n_heads * hd)


def _gla_decode_kernel(q_ref, k_ref, v_ref, r_ref, small_ref, w2_ref, b2_ref, ng_ref, s0_ref, y_ref, sf_ref):
    dk, dv = GLA_DK, GLA_DV
    z = jnp.dot(small_ref[:, 0:GLA_RANK].astype(BF16), w2_ref[...].astype(BF16),
                preferred_element_type=F32) + b2_ref[...]
    g_all = (jnp.minimum(z, 0.0) - jnp.log1p(jnp.exp(-jnp.abs(z)))) / GLA_GATE_NORM
    eye = lax.broadcasted_iota(jnp.int32, (dk, dk), 0) == lax.broadcasted_iota(jnp.int32, (dk, dk), 1)
    column = lambda row: jnp.sum(jnp.where(eye, jnp.broadcast_to(row, (dk, dk)), 0.0), axis=1, keepdims=True)
    for b in range(q_ref.shape[0]):
        for h in range(GLA_HEADS):
            ks = slice(h * dk, (h + 1) * dk)
            vs = slice(h * dv, (h + 1) * dv)
            s_new = (jnp.exp(column(g_all[b:b + 1, ks])) * s0_ref[b, h]
                     + column(k_ref[b:b + 1, ks]) * v_ref[b:b + 1, vs])
            sf_ref[b, h] = s_new
            o = jnp.sum(column(q_ref[b:b + 1, ks] * (dk ** -0.5)) * s_new, axis=0, keepdims=True)
            o = o * lax.rsqrt(jnp.mean(o * o, axis=-1, keepdims=True) + RMS_EPS)
            y_ref[b:b + 1, vs] = (o * ng_ref[:, vs] * jax.nn.silu(r_ref[b:b + 1, vs])).astype(BF16)


def gla_decode(p, s0, w2, b2, norm_g):
    n = p.shape[0]
    row = lambda w, off: pl.BlockSpec((n, w), lambda i: (0, off // w))
    const = lambda shape: pl.BlockSpec(shape, lambda i: (0,) * len(shape))
    return pl.pallas_call(
        _gla_decode_kernel,
        grid=(1,),
        in_specs=[row(256, COL_GLA_Q), row(256, COL_GLA_K), row(512, COL_GLA_V), row(512, COL_GLA_R),
                  row(128, COL_SMALL), const((GLA_RANK, GLA_HEADS * GLA_DK)), const((1, GLA_HEADS * GLA_DK)),
                  const((1, GLA_WIDTH)), const(s0.shape)],
        out_specs=(const((n, GLA_WIDTH)), const(s0.shape)),
        out_shape=(jax.ShapeDtypeStruct((n, GLA_WIDTH), BF16), jax.ShapeDtypeStruct(s0.shape, F32)),
        compiler_params=_params("arbitrary"),
        name="gla_decode",
    )(p, p, p, p, p, w2, b2.reshape(1, -1), norm_g.reshape(1, -1), s0)


def decode_mixer(xs2, w_in_packed, w_out_bf16, ln_g, ln_b, gla_w2, gla_b, gla_norm_g, cmp_pos, cmp_w1, cmp_w2,
                 pool_w, pool_scale, cache, layer, page_table, state_win, state_gla, state_pool, past_len,
                 router=None):
    n_dec = xs2.shape[0]
    hd, G = HEAD_DIM, NSA_KV_HEADS
    p = matmul(xs2, w_in_packed)
    pos = jnp.full((n_dec,), past_len, jnp.int32)
    rows, win, qn, qr, _, _, _, _, gates = nsa_prep(p, rope_tables(pos), 1, n_dec)
    cmp_kv = nsa_compress_paged(cache, layer, page_table, cmp_pos, cmp_w1, cmp_w2)
    t_k = past_len + 1
    n_sel = -(-t_k // SEL_BLOCK)
    o_cmp, sel_idx = nsa_select_decode(qn[0].transpose(1, 0, 2), cmp_kv, past_len // CMP_STRIDE - 1, n_sel, past_len)
    y_nsa = nsa_attend_decode(qr[0].transpose(1, 0, 2), o_cmp, sel_idx, gates[0].transpose(1, 0, 2), cache, layer,
                              page_table, rows.reshape(n_dec, 4 * G, hd), state_win, win.reshape(n_dec, 2 * G, hd))
    y_gla, s_gla = gla_decode(p, state_gla, gla_w2, gla_b, gla_norm_g)
    y_pool, pool_rows = pool_mixer(p[:, None, COL_POOL:COL_POOL + POOL_WIDTH], state_pool, pos[:1], pool_w, pool_scale)
    x1 = outproj_ln(xs2, y_gla, y_nsa, y_pool.reshape(n_dec, -1).astype(BF16), w_out_bf16, ln_g, ln_b, router)
    nsa_rows = rows.reshape(n_dec, 1, 4, G, hd)
    new_win = jnp.concatenate([state_win[layer, :, 1:], win.reshape(n_dec, 1, 2, G, hd)], axis=1)
    return x1, nsa_rows, new_win, s_gla, pool_rows


def prompt_mixer(x2, w_in_packed, w_out_bf16, ln_g, ln_b, gla_w2, gla_b, gla_norm_g, cmp_pos, cmp_w1, cmp_w2,
                 pool_w, pool_scale, n_batch, t_len, router=None):
    p = matmul(x2, w_in_packed)
    pos = jnp.arange(t_len, dtype=jnp.int32)
    rows, win, qn, qr, ks, vs, kw, vw, gates = nsa_prep(p, rope_tables(pos), n_batch, t_len)
    cmp_kv = nsa_compress_prompt(rows, cmp_pos, cmp_w1, cmp_w2, n_batch, t_len)
    o_cmp, selb = nsa_select(qn, cmp_kv, t_len // CMP_STRIDE - 1, t_len)
    y_nsa = nsa_attend(qr, o_cmp, selb, gates, ks, vs, kw, vw)
    s0 = jnp.zeros((n_batch, GLA_HEADS, GLA_DK, GLA_DV), F32)
    y_gla, s_gla = gla_mix(p, s0, gla_w2, gla_b, gla_norm_g, n_batch, t_len)
    prev = jnp.zeros((n_batch, POOL_MAX - 1, POOL_WIDTH), F32)
    y_pool = pool_mix(p, prev, 0, pool_w, pool_scale, n_batch, t_len)
    x1 = outproj_ln(x2, y_gla, y_nsa, y_pool, w_out_bf16, ln_g, ln_b, router)
    nsa_rows = rows.reshape(n_batch, t_len, 4, NSA_KV_HEADS, HEAD_DIM)
    n_win = min(WINDOW, t_len)
    win_rows = win.reshape(n_batch, t_len, 2, NSA_KV_HEADS, HEAD_DIM)[:, t_len - n_win:]
    pool_rows = p.reshape(n_batch, t_len, PACKED_WIDTH)[:, t_len - (POOL_MAX - 1):, COL_POOL:COL_POOL + POOL_WIDTH]
    return x1, nsa_rows, win_rows, s_gla, pool_rows


def split_proj(p):
    out = {}
    off = 0
    for name, size in PROJ_SIZES:
        out[name] = p[..., off:off + size]
        off += size
    return out


def layer_norm(x, g, b):
    xf = x.astype(jnp.float32)
    xc = xf - jnp.mean(xf, -1, keepdims=True)
    var = jnp.mean(xc * xc, -1, keepdims=True)
    return (xc * lax.rsqrt(var + LN_EPS) * g + b).astype(x.dtype)


def rope(x, pos):
    half = ROPE_DIM // 2
    inv_freq = ROPE_THETA ** (-jnp.arange(half, dtype=jnp.float32) / half)
    ang = pos.astype(jnp.float32)[:, None] * inv_freq[None, :]
    cos = jnp.cos(ang)[:, None, :]
    sin = jnp.sin(ang)[:, None, :]
    xf = x.astype(jnp.float32)
    x1, x2 = xf[..., :half], xf[..., half:ROPE_DIM]
    out = jnp.concatenate([x1 * cos - x2 * sin, x2 * cos + x1 * sin, xf[..., ROPE_DIM:]], axis=-1)
    return out.astype(x.dtype)


def masked_softmax(s, mask):
    s = jnp.where(mask, s.astype(jnp.float32), -jnp.inf)
    m = jnp.max(s, axis=-1, keepdims=True)
    m = jnp.where(jnp.isfinite(m), m, 0.0)
    p = jnp.where(mask, jnp.exp(s - m), 0.0)
    return p / jnp.maximum(jnp.sum(p, -1, keepdims=True), 1e-30)


def gla_recurrence(q, k, v, g, s0):
    B, T, H, _ = q.shape
    C = GLA_CHUNK
    n_chunks = -(-T // C)
    pad = n_chunks * C - T

    def prep(a):
        a = jnp.pad(a, ((0, 0), (0, pad), (0, 0), (0, 0)))
        return a.reshape(B, n_chunks, C, H, a.shape[-1]).transpose(1, 0, 3, 2, 4)

    causal = jnp.tril(jnp.ones((C, C), dtype=bool))

    def step(S, inp):
        qi, ki, vi, gi = [a.astype(jnp.float32) for a in inp]
        b = jnp.cumsum(gi, axis=2)
        o_inter = jnp.einsum('bhtk,bhkv->bhtv', qi * jnp.exp(b), S)
        diff = jnp.where(causal[:, :, None], b[:, :, :, None, :] - b[:, :, None, :, :], -jnp.inf)
        attn = jnp.einsum('bhtk,bhsk,bhtsk->bhts', qi, ki, jnp.exp(diff))
        o = o_inter + jnp.einsum('bhts,bhsv->bhtv', attn, vi)
        b_last = b[:, :, -1:, :]
        S = jnp.exp(b_last[:, :, 0, :])[..., None] * S + jnp.einsum('bhsk,bhsv->bhkv', ki * jnp.exp(b_last - b), vi)
        return S, o

    S, o = lax.scan(step, s0.astype(jnp.float32), (prep(q), prep(k), prep(v), prep(g)))
    o = o.transpose(1, 0, 3, 2, 4).reshape(B, n_chunks * C, H, v.shape[-1])[:, :T]
    return o, S.astype(s0.dtype)


def gla_mixer(parts, s0, w2, b2, norm_g):
    B, T = parts['gla_q'].shape[:2]
    q = parts['gla_q'].reshape(B, T, GLA_HEADS, GLA_DK) * (GLA_DK ** -0.5)
    k = parts['gla_k'].reshape(B, T, GLA_HEADS, GLA_DK)
    v = parts['gla_v'].reshape(B, T, GLA_HEADS, GLA_DV)
    g = jax.nn.log_sigmoid((parts['gla_glr'] @ w2 + b2).astype(jnp.float32)) / GLA_GATE_NORM
    g = g.reshape(B, T, GLA_HEADS, GLA_DK)
    o, s_new = gla_recurrence(q, k, v, g, s0)
    o = o * lax.rsqrt(jnp.mean(o * o, -1, keepdims=True) + RMS_EPS)
    out = o.reshape(B, T, GLA_WIDTH) * norm_g * jax.nn.silu(parts['gla_r'].astype(jnp.float32))
    return out.astype(parts['gla_v'].dtype), s_new


def nsa_compress(kx, pos_emb, w1, w2):
    B, T, G, D = kx.shape
    nh = T // CMP_STRIDE
    halves = kx[:, :nh * CMP_STRIDE].reshape(B, nh, CMP_STRIDE, G, D).astype(jnp.float32)
    pe = pos_emb.reshape(2, CMP_STRIDE, D)
    w = w1.reshape(2, CMP_STRIDE, D, CMP_HIDDEN)
    h_lo = jnp.einsum('bnjgd,jdh->bngh', halves + pe[0][None, None, :, None, :], w[0])
    h_hi = jnp.einsum('bnjgd,jdh->bngh', halves + pe[1][None, None, :, None, :], w[1])
    h = jax.nn.gelu(h_lo[:, :-1] + h_hi[:, 1:])
    return jnp.einsum('bngh,hd->bngd', h, w2)


def nsa_global(qn, qr, rows, q_pos, cmp_pos, cmp_w1, cmp_w2):
    B, Tq, G, R, D = qn.shape
    Tk = rows.shape[1]
    scale = HEAD_DIM ** -0.5
    kcmp = nsa_compress(rows[:, :, 0], cmp_pos[0], cmp_w1[0], cmp_w2[0])
    vcmp = nsa_compress(rows[:, :, 1], cmp_pos[1], cmp_w1[1], cmp_w2[1])
    n_cmp = kcmp.shape[1]
    cmp_end = jnp.arange(n_cmp) * CMP_STRIDE + CMP_LEN - 1
    n_sel = -(-Tk // SEL_BLOCK)
    pad = n_sel * SEL_BLOCK - Tk

    def to_blocks(a):
        a = jnp.pad(a, ((0, 0), (0, pad), (0, 0), (0, 0)))
        return a.reshape(B, n_sel, SEL_BLOCK, G, D).transpose(0, 3, 1, 2, 4)

    ksb = to_blocks(rows[:, :, 2])
    vsb = to_blocks(rows[:, :, 3])
    ci = jnp.arange(n_cmp)[:, None]
    sj = jnp.arange(n_sel)[None, :]
    overlap = ((ci * CMP_STRIDE <= sj * SEL_BLOCK + SEL_BLOCK - 1) &
               (ci * CMP_STRIDE + CMP_LEN - 1 >= sj * SEL_BLOCK)).astype(jnp.float32)
    n_top = min(SEL_TOPN, n_sel)
    gather = jax.vmap(jax.vmap(lambda blocks, idx: blocks[idx]))
    blk_ids = jnp.arange(n_sel)

    def block_fn(args):
        qnb, qrb, qp = args
        qb = qp.shape[0]
        s = jnp.einsum('bqgrd,bngd->bgrqn', qnb, kcmp) * scale
        p_c = masked_softmax(s, cmp_end[None, :] <= qp[:, None])
        o_c = jnp.einsum('bgrqn,bngd->bqgrd', p_c, vcmp)
        imp = jnp.einsum('bgqn,nj->bgqj', jnp.sum(p_c, axis=2), overlap)
        cur = qp[:, None] // SEL_BLOCK
        valid = blk_ids[None, :] <= cur
        forced = (blk_ids[None, :] == 0) | (blk_ids[None, :] == cur) | (blk_ids[None, :] == cur - 1)
        imp = jnp.where(valid, jnp.where(forced, jnp.inf, imp), -jnp.inf)
        top_s, top_i = lax.top_k(imp, n_top)
        kg = gather(ksb, top_i)
        vg = gather(vsb, top_i)
        kpos = top_i[..., None] * SEL_BLOCK + jnp.arange(SEL_BLOCK)
        mask = (top_s > -jnp.inf)[..., None] & (kpos <= qp[None, None, :, None, None])
        s2 = jnp.einsum('bqgrd,bgqnjd->bgrqnj', qrb, kg) * scale
        s2 = s2.reshape(B, G, R, qb, n_top * SEL_BLOCK)
        p_s = masked_softmax(s2, mask.reshape(B, G, 1, qb, n_top * SEL_BLOCK))
        p_s = p_s.reshape(B, G, R, qb, n_top, SEL_BLOCK)
        o_s = jnp.einsum('bgrqnj,bgqnjd->bqgrd', p_s, vg.astype(jnp.float32))
        return o_c, o_s

    qb = min(Tq, Q_BLOCK)
    nq = -(-Tq // qb)
    padq = nq * qb - Tq
    qpad = ((0, 0), (0, padq), (0, 0), (0, 0), (0, 0))
    qn_b = jnp.pad(qn, qpad).reshape(B, nq, qb, G, R, D).swapaxes(0, 1)
    qr_b = jnp.pad(qr, qpad).reshape(B, nq, qb, G, R, D).swapaxes(0, 1)
    qp_b = jnp.pad(q_pos, (0, padq), mode='edge').reshape(nq, qb)
    o_c, o_s = lax.map(block_fn, (qn_b, qr_b, qp_b))
    o_c = o_c.swapaxes(0, 1).reshape(B, nq * qb, G, R, D)[:, :Tq]
    o_s = o_s.swapaxes(0, 1).reshape(B, nq * qb, G, R, D)[:, :Tq]
    return o_c, o_s


def band_attend(q, k, v, qpos, kpos):
    s = jnp.einsum('bqgrd,bkgd->bgrqk', q, k) * (HEAD_DIM ** -0.5)
    mask = ((kpos[None, :] <= qpos[:, None]) & (kpos[None, :] > qpos[:, None] - WINDOW) & (kpos[None, :] >= 0))
    p = masked_softmax(s, mask)
    return jnp.einsum('bgrqk,bkgd->bqgrd', p, v.astype(jnp.float32))


def sliding_prompt(q, k, v):
    B, T, G, R, D = q.shape
    kp = jnp.pad(k, ((0, 0), (WINDOW, 0), (0, 0), (0, 0)))
    vp = jnp.pad(v, ((0, 0), (WINDOW, 0), (0, 0), (0, 0)))
    nq = T // Q_BLOCK

    def fn(i):
        q0 = i * Q_BLOCK
        qi = lax.dynamic_slice_in_dim(q, q0, Q_BLOCK, axis=1)
        ki = lax.dynamic_slice_in_dim(kp, q0, WINDOW + Q_BLOCK, axis=1)
        vi = lax.dynamic_slice_in_dim(vp, q0, WINDOW + Q_BLOCK, axis=1)
        qpos = q0 + jnp.arange(Q_BLOCK)
        kpos = q0 - WINDOW + jnp.arange(WINDOW + Q_BLOCK)
        return band_attend(qi, ki, vi, qpos, kpos)

    o = lax.map(fn, jnp.arange(nq))
    return o.swapaxes(0, 1).reshape(B, T, G, R, D)


def nsa_mixer(parts, pos, nsa_past, win_past, cmp_pos, cmp_w1, cmp_w2):
    B, T = parts['nsa_q'].shape[:2]
    dt = parts['nsa_q'].dtype
    q = parts['nsa_q'].reshape(B, T, NSA_HEADS, HEAD_DIM)
    q_rope = rope(q, pos)
    kv = lambda name: parts[name].reshape(B, T, NSA_KV_HEADS, HEAD_DIM)
    k_win, v_win = rope(kv('win_k'), pos), kv('win_v')
    new_rows = jnp.stack([kv('cmp_k'), kv('cmp_v'), rope(kv('slc_k'), pos), kv('slc_v')], axis=2)
    rows = new_rows if nsa_past is None else jnp.concatenate([nsa_past.astype(dt), new_rows], axis=1)
    qg = q.reshape(B, T, NSA_KV_HEADS, NSA_REP, HEAD_DIM)
    qrg = q_rope.reshape(B, T, NSA_KV_HEADS, NSA_REP, HEAD_DIM)
    o_cmp, o_slc = nsa_global(qg, qrg, rows, pos, cmp_pos, cmp_w1, cmp_w2)
    win_rows = jnp.stack([k_win, v_win], axis=2)
    if win_past is None:
        o_win = sliding_prompt(qrg, k_win, v_win)
        new_win = win_rows[:, T - min(WINDOW, T):]
    else:
        n_buf = win_past.shape[1]
        ext = jnp.concatenate([win_past.astype(dt), win_rows], axis=1)
        kpos = pos[0] - n_buf + jnp.arange(n_buf + T)
        o_win = band_attend(qrg, ext[:, :, 0], ext[:, :, 1], pos, kpos)
        new_win = ext[:, T:]
    gates = jax.nn.sigmoid(parts['nsa_gate'].astype(jnp.float32)).reshape(B, T, NSA_KV_HEADS, NSA_REP, 3)
    o = gates[..., 0:1] * o_cmp + gates[..., 1:2] * o_slc + gates[..., 2:3] * o_win
    return o.reshape(B, T, NSA_WIDTH).astype(dt), new_rows, new_win


def pool_mixer(u, prev, pos, w_pool, scale):
    B, T, C = u.shape
    P = POOL_MAX - 1
    ext = jnp.concatenate([prev.astype(u.dtype), u], axis=1).astype(jnp.float32)
    cs = jnp.concatenate([jnp.zeros((B, 1, C), jnp.float32), jnp.cumsum(ext, axis=1)], axis=1)
    end = cs[:, P + 1:]
    means = []
    for gi, w in enumerate(POOL_WINDOWS):
        sl = slice(gi * POOL_GROUP_DIM, (gi + 1) * POOL_GROUP_DIM)
        start = cs[:, P + 1 - w:P + 1 - w + T, sl]
        cnt = jnp.minimum(pos + 1, w).astype(jnp.float32)[None, :, None]
        means.append((end[..., sl] - start) / cnt)
    pooled = (jnp.concatenate(means, axis=-1) - ext[:, P:]).reshape(B, T, POOL_GROUPS, POOL_GROUP_DIM)
    y = jnp.einsum('btgc,gcd->btgd', pooled, w_pool.astype(jnp.float32)).reshape(B, T, C) * scale
    return y.astype(u.dtype), ext[:, -P:].astype(u.dtype)


def mixer_layer(x, pos0, gla_s0, pool_prev, nsa_past, win_past,
                w_in, gla_w2, gla_b, gla_norm_g, cmp_pos, cmp_w1, cmp_w2, pool_w, pool_scale, w_out):
    B, T, _ = x.shape
    pos = pos0 + jnp.arange(T, dtype=jnp.int32)
    parts = split_proj(mm3(x, w_in))
    y_gla, s_gla = gla_mixer(parts, gla_s0, gla_w2, gla_b, gla_norm_g)
    y_nsa, nsa_rows, win_rows = nsa_mixer(parts, pos, nsa_past, win_past, cmp_pos, cmp_w1, cmp_w2)
    y_pool, pool_rows = pool_mixer(parts['pool'], pool_prev, pos, pool_w, pool_scale)
    y = mm3(jnp.concatenate([y_gla, y_nsa, y_pool], axis=-1), w_out)
    return y, nsa_rows, win_rows, s_gla, pool_rows


def swiglu(x, wg, wu, wd):
    return mm3(jax.nn.silu(mm3(x, wg)) * mm3(x, wu), wd)


def moe_ffn(x, router, wg, wu, wd):
    B, T, D = x.shape
    n_tok = B * T
    xt = x.reshape(n_tok, D)
    logits = (xt @ router).astype(jnp.float32)
    top_v, top_i = lax.top_k(logits, TOP_K)
    gates = jax.nn.softmax(top_v, axis=-1)
    n_asg = n_tok * TOP_K
    e_flat = top_i.reshape(n_asg)
    tok_flat = jnp.arange(n_asg, dtype=jnp.int32) // TOP_K
    gate_flat = gates.reshape(n_asg)
    blk = MOE_ROW_BLOCK if n_asg >= N_EXPERTS * MOE_ROW_BLOCK else MOE_MIN_BLOCK
    n_blk = -(-(n_asg + N_EXPERTS * (blk - 1)) // blk)
    order = jnp.argsort(e_flat)
    e_sorted = e_flat[order]
    counts = jnp.bincount(e_flat, length=N_EXPERTS)
    padded = (counts + blk - 1) // blk * blk
    pad_end = jnp.cumsum(padded)
    pad_start = pad_end - padded
    start = jnp.cumsum(counts) - counts
    dest = pad_start[e_sorted] + jnp.arange(n_asg) - start[e_sorted]
    row_tok = jnp.zeros((n_blk * blk,), jnp.int32).at[dest].set(tok_flat[order])
    row_gate = jnp.zeros((n_blk * blk,), jnp.float32).at[dest].set(gate_flat[order])
    blk_expert = jnp.minimum(jnp.searchsorted(pad_end, jnp.arange(n_blk) * blk, side='right'), N_EXPERTS - 1)

    def expert_block(args):
        rows, e = args
        xb = xt[rows]
        return (jax.nn.silu(xb @ wg[e]) * (xb @ wu[e])) @ wd[e]

    out = lax.map(expert_block, (row_tok.reshape(n_blk, blk), blk_expert))
    y = jnp.zeros((n_tok, D), jnp.float32).at[row_tok].add(
        out.reshape(n_blk * blk, D).astype(jnp.float32) * row_gate[:, None])
    return y.reshape(B, T, D).astype(x.dtype)


def kernel(x_prompt, x_sample, cache_nsa, page_table, state_win, state_gla, state_pool, w_in, gla_gate_w2, gla_gate_b, gla_norm_g, nsa_cmp_pos, nsa_cmp_w1, nsa_cmp_w2, pool_w, pool_scale, w_out, ln1_g, ln1_b, ln2_g, ln2_b, ffn_w_gate, ffn_w_up, ffn_w_down, moe_router, moe_w_gate, moe_w_up, moe_w_down):
    n_prompt, t_len, d = x_prompt.shape
    n_dec = x_sample.shape[0]
    xp, xs = x_prompt.reshape(n_prompt * t_len, d), x_sample.reshape(n_dec, d)
    nsa_p, nsa_s, win_p, win_s, gla_p, gla_s, pool_p, pool_s = [], [], [], [], [], [], [], []
    for l in range(DEPTH):
        i = l // 2
        router = moe_router[i] if l % 2 else None
        w_in_packed, w_out_bf16 = pack_w_in(w_in[l]), w_out[l].astype(BF16)
        lw = (w_in_packed, w_out_bf16, ln1_g[l], ln1_b[l], gla_gate_w2[l], gla_gate_b[l], gla_norm_g[l],
              nsa_cmp_pos[l], nsa_cmp_w1[l], nsa_cmp_w2[l], pool_w[l], pool_scale[l])
        xp, r_p, w_p, g_p, p_p = prompt_mixer(xp, *lw, n_prompt, t_len, router)
        xs, r_s, w_s, g_s, p_s = decode_mixer(xs, *lw, cache_nsa, l, page_table, state_win, state_gla[l],
                                              state_pool[l], PAST_LEN, router)
        if l % 2 == 0:
            wg, wu, wd = ffn_w_gate[i].astype(BF16), ffn_w_up[i].astype(BF16), ffn_w_down[i].astype(BF16)
            xp = ffn_ln(xp, wg, wu, wd, ln2_g[l], ln2_b[l])
            xs = ffn_ln(xs, wg, wu, wd, ln2_g[l], ln2_b[l])
        else:
            (xp, lg_p), (xs, lg_s) = xp, xs
            xp, xs = moe_ln([xp, xs], [lg_p, lg_s], moe_w_gate[i], moe_w_up[i], moe_w_down[i], ln2_g[l], ln2_b[l])
        nsa_p.append(r_p); nsa_s.append(r_s); win_p.append(w_p); win_s.append(w_s)
        gla_p.append(g_p); gla_s.append(g_s); pool_p.append(p_p); pool_s.append(p_s)
    return (xp.reshape(n_prompt, t_len, d), xs.reshape(x_sample.shape), jnp.stack(nsa_p), jnp.stack(nsa_s),
            jnp.stack(win_p), jnp.stack(win_s), jnp.stack(gla_p), jnp.stack(gla_s), jnp.stack(pool_p),
            jnp.stack(pool_s))
```

```python
import functools

import jax
import jax.numpy as jnp
from jax import lax
from jax.experimental import pallas as pl
from jax.experimental.pallas import tpu as pltpu

D_MODEL = 2048
DEPTH = 2
PAST_LEN = 16384
HEAD_DIM = 128
GLA_HEADS = 4
GLA_DK = 64
GLA_DV = 128
GLA_RANK = 16
GLA_GATE_NORM = 16.0
GLA_CHUNK = 64
GLA_WIDTH = GLA_HEADS * GLA_DV
NSA_HEADS = 8
NSA_KV_HEADS = 2
NSA_REP = NSA_HEADS // NSA_KV_HEADS
NSA_WIDTH = NSA_HEADS * HEAD_DIM
CMP_LEN = 32
CMP_STRIDE = 16
CMP_HIDDEN = 128
SEL_BLOCK = 64
SEL_TOPN = 16
WINDOW = 512
Q_BLOCK = 128
POOL_GROUPS = 4
POOL_GROUP_DIM = 128
POOL_WIDTH = POOL_GROUPS * POOL_GROUP_DIM
POOL_WINDOWS = (2, 4, 8, 16)
POOL_MAX = 16
ROPE_THETA = 500000.0
ROPE_DIM = HEAD_DIM // 4
N_EXPERTS = 8
TOP_K = 2
MOE_ROW_BLOCK = 128
MOE_MIN_BLOCK = 8
ALPHA = (2 * DEPTH) ** 0.25
LN_EPS = 1e-5
RMS_EPS = 1e-6

PROJ_SIZES = (
    ('gla_q', GLA_HEADS * GLA_DK), ('gla_k', GLA_HEADS * GLA_DK), ('gla_v', GLA_HEADS * GLA_DV),
    ('gla_glr', GLA_RANK), ('gla_r', GLA_HEADS * GLA_DV),
    ('nsa_q', NSA_HEADS * HEAD_DIM),
    ('cmp_k', NSA_KV_HEADS * HEAD_DIM), ('cmp_v', NSA_KV_HEADS * HEAD_DIM),
    ('slc_k', NSA_KV_HEADS * HEAD_DIM), ('slc_v', NSA_KV_HEADS * HEAD_DIM),
    ('win_k', NSA_KV_HEADS * HEAD_DIM), ('win_v', NSA_KV_HEADS * HEAD_DIM),
    ('nsa_gate', 3 * NSA_HEADS),
    ('pool', POOL_WIDTH),
)

GLA_SUB = 16
SEL_PAD = 128

BF16 = jnp.bfloat16
F32 = jnp.float32
NEG_BIG = -1e30
VMEM_LIMIT_BYTES = 56 * 1024 * 1024

COL_NSA_Q = 0
COL_ROWS = 1024
COL_WIN = 2048
COL_POOL = 2560
COL_GLA_V = 3072
COL_GLA_R = 3584
COL_GLA_Q = 4096
COL_GLA_K = 4352
COL_SMALL = 4608
PACKED_WIDTH = 4736
SMALL_GATE_OFF = GLA_RANK


def _params(*sem, vmem_limit_bytes=VMEM_LIMIT_BYTES):
    return pltpu.CompilerParams(dimension_semantics=sem, vmem_limit_bytes=vmem_limit_bytes)


def _proj_offsets():
    out, off = {}, 0
    for name, size in PROJ_SIZES:
        out[name] = (off, size)
        off += size
    return out


def pack_w_in(w):
    offs = _proj_offsets()
    sl = lambda n: w[:, offs[n][0]:offs[n][0] + offs[n][1]]
    pad = jnp.zeros((w.shape[0], 128 - GLA_RANK - 3 * NSA_HEADS), w.dtype)
    cols = [sl('nsa_q'), sl('cmp_k'), sl('cmp_v'), sl('slc_k'), sl('slc_v'), sl('win_k'), sl('win_v'),
            sl('pool'), sl('gla_v'), sl('gla_r'), sl('gla_q'), sl('gla_k'), sl('gla_glr'), sl('nsa_gate'), pad]
    return jnp.concatenate(cols, axis=1).astype(BF16)


def rope_tables(pos):
    half = ROPE_DIM // 2
    inv_freq = ROPE_THETA ** (-jnp.arange(half, dtype=F32) / half)
    ang = pos.astype(F32)[:, None] * inv_freq[None, :]
    cos, sin = jnp.cos(ang), jnp.sin(ang)
    t = pos.shape[0]
    c = jnp.concatenate([cos, cos, jnp.ones((t, HEAD_DIM - ROPE_DIM), F32)], axis=1)
    sa = jnp.concatenate([-sin, jnp.zeros((t, HEAD_DIM - half), F32)], axis=1)
    sb = jnp.concatenate([jnp.zeros((t, half), F32), sin, jnp.zeros((t, HEAD_DIM - ROPE_DIM), F32)], axis=1)
    return c, sa, sb


def _pick_tile(n, pref):
    for t in pref:
        if n % t == 0:
            return t
    return n


def _mm_kernel(x_ref, w_ref, o_ref, xb_ref):
    @pl.when(pl.program_id(1) == 0)
    def _():
        xb_ref[...] = x_ref[...].astype(BF16)

    o_ref[...] = jnp.dot(xb_ref[...], w_ref[...].astype(BF16), preferred_element_type=F32)


def matmul(x, w):
    m, k = x.shape
    n = w.shape[1]
    tm = _pick_tile(m, tuple(t for t in (1024, 512, 256, 128, 64, 32, 16, 8) if t * k <= 2048 * 1024))
    tn = 512 if n >= 512 else n
    return pl.pallas_call(
        _mm_kernel,
        grid=(m // tm, pl.cdiv(n, tn)),
        in_specs=[pl.BlockSpec((tm, k), lambda i, j: (i, 0)),
                  pl.BlockSpec((k, tn), lambda i, j: (0, j))],
        out_specs=pl.BlockSpec((tm, tn), lambda i, j: (i, j)),
        out_shape=jax.ShapeDtypeStruct((m, n), F32),
        scratch_shapes=[pltpu.VMEM((tm, k), BF16)],
        compiler_params=_params("arbitrary", "arbitrary"),
        name="matmul",
    )(x, w)


def mm3(x, w):
    lead = x.shape[:-1]
    return matmul(x.reshape(-1, x.shape[-1]), w).reshape(*lead, w.shape[1])


def _rope(x, c, sa, sb):
    return x * c + pltpu.roll(x, HEAD_DIM - ROPE_DIM // 2, 1) * sa + pltpu.roll(x, ROPE_DIM // 2, 1) * sb


def _nsa_prep_kernel(q_ref, rows_ref, win_ref, small_ref, c_ref, sa_ref, sb_ref,
                     rows_o, win_o, qn_o, qr_o, ks_o, vs_o, kw_o, vw_o, gate_o):
    c, sa, sb = c_ref[...], sa_ref[...], sb_ref[...]
    scale = HEAD_DIM ** -0.5
    hd = HEAD_DIM
    for h in range(NSA_HEADS):
        x = q_ref[:, h * hd:(h + 1) * hd]
        qn_o[0, h] = (x * scale).astype(BF16)
        qr_o[0, h] = (_rope(x, c, sa, sb) * scale).astype(BF16)
    ones = jnp.ones((q_ref.shape[0], hd), BF16)
    rows_o[:, 0:4 * hd] = rows_ref[:, 0:4 * hd]
    for g in range(NSA_KV_HEADS):
        k = _rope(rows_ref[:, (4 + g) * hd:(5 + g) * hd], c, sa, sb)
        rows_o[:, (4 + g) * hd:(5 + g) * hd] = k
        ks_o[0, g] = k.astype(BF16)
        v = rows_ref[:, (6 + g) * hd:(7 + g) * hd]
        rows_o[:, (6 + g) * hd:(7 + g) * hd] = v
        vs_o[0, g, :, 0:hd] = v.astype(BF16)
        vs_o[0, g, :, hd:2 * hd] = ones
        k = _rope(win_ref[:, g * hd:(g + 1) * hd], c, sa, sb)
        win_o[:, g * hd:(g + 1) * hd] = k
        kw_o[0, g] = k.astype(BF16)
        v = win_ref[:, (2 + g) * hd:(3 + g) * hd]
        win_o[:, (2 + g) * hd:(3 + g) * hd] = v
        vw_o[0, g, :, 0:hd] = v.astype(BF16)
        vw_o[0, g, :, hd:2 * hd] = ones
    sig = jax.nn.sigmoid(small_ref[...])
    per_g = 3 * NSA_REP
    for g in range(NSA_KV_HEADS):
        gate_o[0, g] = pltpu.roll(sig, 128 - SMALL_GATE_OFF - g * per_g, 1)


def nsa_prep(p, tables, n_batch, t_len):
    tr = _pick_tile(t_len, (512, 256, 128, 64, 32, 16))
    nt = t_len // tr
    n = n_batch * t_len
    hd = HEAD_DIM
    row = lambda w, cb: pl.BlockSpec((tr, w), lambda b, i: (b * nt + i, cb))
    tab = pl.BlockSpec((tr, hd), lambda b, i: (i, 0))
    head = lambda nh, w: pl.BlockSpec((1, nh, tr, w), lambda b, i: (b, 0, i, 0))
    out_shape = (
        jax.ShapeDtypeStruct((n, 8 * hd), F32),
        jax.ShapeDtypeStruct((n, 4 * hd), F32),
        jax.ShapeDtypeStruct((n_batch, NSA_HEADS, t_len, hd), BF16),
        jax.ShapeDtypeStruct((n_batch, NSA_HEADS, t_len, hd), BF16),
        jax.ShapeDtypeStruct((n_batch, NSA_KV_HEADS, t_len, hd), BF16),
        jax.ShapeDtypeStruct((n_batch, NSA_KV_HEADS, t_len, 2 * hd), BF16),
        jax.ShapeDtypeStruct((n_batch, NSA_KV_HEADS, t_len, hd), BF16),
        jax.ShapeDtypeStruct((n_batch, NSA_KV_HEADS, t_len, 2 * hd), BF16),
        jax.ShapeDtypeStruct((n_batch, NSA_KV_HEADS, t_len, 128), F32),
    )
    return pl.pallas_call(
        _nsa_prep_kernel,
        grid=(n_batch, nt),
        in_specs=[row(8 * hd, COL_NSA_Q // (8 * hd)), row(8 * hd, COL_ROWS // (8 * hd)),
                  row(4 * hd, COL_WIN // (4 * hd)), row(128, COL_SMALL // 128), tab, tab, tab],
        out_specs=(row(8 * hd, 0), row(4 * hd, 0), head(NSA_HEADS, hd), head(NSA_HEADS, hd),
                   head(NSA_KV_HEADS, hd), head(NSA_KV_HEADS, 2 * hd), head(NSA_KV_HEADS, hd),
                   head(NSA_KV_HEADS, 2 * hd), head(NSA_KV_HEADS, 128)),
        out_shape=out_shape,
        compiler_params=_params("arbitrary", "arbitrary"),
        name="nsa_prep",
    )(p, p, p, p, *tables)


def _nsa_cmp_kernel(x_ref, pe_ref, w1_ref, w2_ref, o_ref):
    nh = o_ref.shape[0]
    h_lo = jnp.zeros((nh, CMP_HIDDEN), F32)
    h_hi = jnp.zeros((nh, CMP_HIDDEN), F32)
    for j in range(CMP_STRIDE):
        xj = x_ref[pl.ds(j, nh, stride=CMP_STRIDE), :]
        h_lo += jnp.dot((xj + pe_ref[j:j + 1, :]).astype(BF16), w1_ref[j].astype(BF16), preferred_element_type=F32)
        h_hi += jnp.dot((xj + pe_ref[CMP_STRIDE + j:CMP_STRIDE + j + 1, :]).astype(BF16),
                        w1_ref[CMP_STRIDE + j].astype(BF16), preferred_element_type=F32)
    h = jax.nn.gelu(h_lo + pltpu.roll(h_hi, nh - 1, 0))
    o_ref[...] = jnp.dot(h.astype(BF16), w2_ref[...].astype(BF16), preferred_element_type=F32).astype(BF16)


def nsa_compress_prompt(rows, cmp_pos, cmp_w1, cmp_w2, n_batch, t_len):
    nh = t_len // CMP_STRIDE
    hd = HEAD_DIM
    rows3 = rows.reshape(n_batch, t_len, 8 * hd)
    return pl.pallas_call(
        _nsa_cmp_kernel,
        grid=(n_batch, 2, NSA_KV_HEADS),
        in_specs=[pl.BlockSpec((None, t_len, hd), lambda b, kd, g: (b, 0, kd * NSA_KV_HEADS + g)),
                  pl.BlockSpec((None, CMP_LEN, hd), lambda b, kd, g: (kd, 0, 0)),
                  pl.BlockSpec((None, CMP_LEN, hd, CMP_HIDDEN), lambda b, kd, g: (kd, 0, 0, 0)),
                  pl.BlockSpec((None, CMP_HIDDEN, hd), lambda b, kd, g: (kd, 0, 0))],
        out_specs=pl.BlockSpec((None, None, None, nh, hd), lambda b, kd, g: (b, kd, g, 0, 0)),
        out_shape=jax.ShapeDtypeStruct((n_batch, 2, NSA_KV_HEADS, nh, hd), BF16),
        compiler_params=_params("arbitrary", "arbitrary", "arbitrary"),
        name="nsa_compress",
    )(rows3, cmp_pos, cmp_w1, cmp_w2)


def _nsa_select_kernel(qn_ref, kc_ref, vc_ref, ovt_ref, oc_ref, selb_ref, *, n_cmp, n_top):
    rep, tq, hd = qn_ref.shape[1], qn_ref.shape[2], qn_ref.shape[3]
    n_cmp_pad = kc_ref.shape[0]
    n_sel = ovt_ref.shape[0]
    q0 = pl.program_id(2) * tq
    q = qn_ref[0].reshape(rep * tq, hd)
    s = lax.dot_general(q, kc_ref[...], (((1,), (1,)), ((), ())), preferred_element_type=F32)
    row = lax.broadcasted_iota(jnp.int32, (rep * tq, n_cmp_pad), 0)
    col = lax.broadcasted_iota(jnp.int32, (rep * tq, n_cmp_pad), 1)
    qpos = q0 + (row & (tq - 1))
    mask = (col * CMP_STRIDE + (CMP_LEN - 1) <= qpos) & (col < n_cmp)
    s = jnp.where(mask, s, -jnp.inf)
    m = jnp.max(s, axis=-1, keepdims=True)
    m = jnp.where(m > -jnp.inf, m, 0.0)
    p = jnp.where(mask, jnp.exp(s - m), 0.0)
    p = p / jnp.maximum(jnp.sum(p, axis=-1, keepdims=True), 1e-30)
    oc = jnp.dot(p.astype(BF16), vc_ref[...], preferred_element_type=F32)
    oc_ref[0] = oc.reshape(rep, tq, hd).astype(BF16)
    psum = p[0:tq]
    for r in range(1, rep):
        psum = psum + p[r * tq:(r + 1) * tq]
    imp = lax.dot_general(ovt_ref[...], psum, (((1,), (1,)), ((), ())), preferred_element_type=F32,
                          precision=lax.Precision.HIGHEST)
    blk = lax.broadcasted_iota(jnp.int32, (n_sel, tq), 0)
    cur = (q0 + lax.broadcasted_iota(jnp.int32, (n_sel, tq), 1)) // SEL_BLOCK
    forced = (blk == 0) | (blk == cur) | (blk == cur - 1)
    v = jnp.where(blk <= cur, jnp.where(forced, jnp.inf, imp), -jnp.inf)
    rank = jnp.zeros((n_sel, tq), jnp.int32)
    for i in range(n_sel):
        vi = v[i:i + 1, :]
        ahead = (vi > v) | ((vi == v) & (blk > i))
        rank = rank + ahead.astype(jnp.int32)
    selb_t = jnp.where((rank < n_top) & (v > -jnp.inf), 0.0, NEG_BIG)
    pad = jnp.full((SEL_PAD - n_sel, tq), NEG_BIG, F32)
    selb_ref[0, 0] = jnp.concatenate([selb_t, pad], axis=0).T.astype(BF16)


def nsa_select(qn, cmp_kv, n_cmp, t_k):
    n_batch, _, t_len, hd = qn.shape
    n_cmp_pad = cmp_kv.shape[3]
    n_sel = -(-t_k // SEL_BLOCK)
    tq = _pick_tile(t_len, (256, 128, 64, 32, 16))
    ci = jnp.arange(n_cmp_pad)[None, :]
    sj = jnp.arange(n_sel)[:, None]
    overlap_t = ((ci * CMP_STRIDE <= sj * SEL_BLOCK + SEL_BLOCK - 1) &
                 (ci * CMP_STRIDE + CMP_LEN - 1 >= sj * SEL_BLOCK) & (ci < n_cmp)).astype(F32)
    kern = functools.partial(_nsa_select_kernel, n_cmp=n_cmp, n_top=min(SEL_TOPN, n_sel))
    return pl.pallas_call(
        kern,
        grid=(n_batch, NSA_KV_HEADS, t_len // tq),
        in_specs=[pl.BlockSpec((1, NSA_REP, tq, hd), lambda b, g, i: (b, g, i, 0)),
                  pl.BlockSpec((None, None, None, n_cmp_pad, hd), lambda b, g, i: (b, 0, g, 0, 0)),
                  pl.BlockSpec((None, None, None, n_cmp_pad, hd), lambda b, g, i: (b, 1, g, 0, 0)),
                  pl.BlockSpec((n_sel, n_cmp_pad), lambda b, g, i: (0, 0))],
        out_specs=(pl.BlockSpec((1, NSA_REP, tq, hd), lambda b, g, i: (b, g, i, 0)),
                   pl.BlockSpec((1, 1, tq, SEL_PAD), lambda b, g, i: (b, g, i, 0))),
        out_shape=(jax.ShapeDtypeStruct((n_batch, NSA_HEADS, t_len, hd), BF16),
                   jax.ShapeDtypeStruct((n_batch, NSA_KV_HEADS, t_len, SEL_PAD), BF16)),
        compiler_params=_params("arbitrary", "arbitrary", "arbitrary"),
        name="nsa_select",
    )(qn, cmp_kv, cmp_kv, overlap_t)


ATTN_Q_BLOCK = 256
ATTN_K_TILE = 1024


def _nsa_attn_kernel(qr_ref, oc_ref, selb_ref, gate_ref, ks_ref, vs_ref, kw_ref, vw_ref, e_ref, o_ref,
                     m_scr, acc_scr, *, tk, wk):
    rep, qb, hd = qr_ref.shape[1], qr_ref.shape[2], qr_ref.shape[3]
    nr = rep * qb
    q0 = pl.program_id(2) * qb
    q = qr_ref[0].reshape(nr, hd)
    selb = selb_ref[0, 0]
    nt = (((1,), (1,)), ((), ()))

    def sel_scores(t):
        k = ks_ref[0, 0, pl.ds(pl.multiple_of(t * tk, tk), tk), :]
        s = lax.dot_general(q, k, nt, preferred_element_type=F32)
        bias = jnp.dot(selb, e_ref[t], preferred_element_type=F32)
        return (s.reshape(rep, qb, tk) + bias[None]).reshape(nr, tk)

    def sel_values(t):
        return vs_ref[0, 0, pl.ds(pl.multiple_of(t * tk, tk), tk), :]

    td = q0 // tk
    qpos = q0 + (lax.broadcasted_iota(jnp.int32, (nr, tk), 0) & (qb - 1))
    kpos = td * tk + lax.broadcasted_iota(jnp.int32, (nr, tk), 1)
    s = jnp.where(kpos <= qpos, sel_scores(td), NEG_BIG)
    m = jnp.max(s, axis=-1, keepdims=True)
    m_scr[...] = m
    acc_scr[...] = jnp.dot(jnp.exp(s - m).astype(BF16), sel_values(td), preferred_element_type=F32)

    def body(t, carry):
        s = sel_scores(t)
        m_old = m_scr[...]
        m_new = jnp.maximum(m_old, jnp.max(s, axis=-1, keepdims=True))
        p = jnp.exp(s - m_new).astype(BF16)
        acc_scr[...] = jnp.exp(m_old - m_new) * acc_scr[...] + jnp.dot(p, sel_values(t), preferred_element_type=F32)
        m_scr[...] = m_new
        return carry

    lax.fori_loop(0, td, body, 0)
    acc = acc_scr[...]
    o_sel = acc[:, 0:hd] / jnp.maximum(acc[:, hd:hd + 1], 1e-30)

    kstart = pl.multiple_of(jnp.maximum(q0 - WINDOW, 0), qb)
    kw = kw_ref[0, 0, pl.ds(kstart, wk), :]
    s = lax.dot_general(q, kw, nt, preferred_element_type=F32)
    qpos = q0 + (lax.broadcasted_iota(jnp.int32, (nr, wk), 0) & (qb - 1))
    kpos = kstart + lax.broadcasted_iota(jnp.int32, (nr, wk), 1)
    s = jnp.where((kpos <= qpos) & (kpos > qpos - WINDOW), s, NEG_BIG)
    m = jnp.max(s, axis=-1, keepdims=True)
    accw = jnp.dot(jnp.exp(s - m).astype(BF16), vw_ref[0, 0, pl.ds(kstart, wk), :], preferred_element_type=F32)
    o_win = accw[:, 0:hd] / jnp.maximum(accw[:, hd:hd + 1], 1e-30)

    gates = gate_ref[0, 0]
    for r in range(rep):
        rows = slice(r * qb, (r + 1) * qb)
        o = (gates[:, 3 * r:3 * r + 1] * oc_ref[0, r].astype(F32)
             + gates[:, 3 * r + 1:3 * r + 2] * o_sel[rows]
             + gates[:, 3 * r + 2:3 * r + 3] * o_win[rows])
        o_ref[:, r * hd:(r + 1) * hd] = o.astype(BF16)


def nsa_attend(qr, o_cmp, selb, gates, ks, vs, kw, vw):
    n_batch, _, t_len, hd = qr.shape
    n_sel = selb.shape[3]
    qb = min(ATTN_Q_BLOCK, t_len)
    tk = min(ATTN_K_TILE, t_len)
    wk = min(WINDOW + qb, t_len)
    nq = t_len // qb
    n_tiles = t_len // tk
    key_blk = (jnp.arange(n_tiles)[:, None, None] * tk + jnp.arange(tk)[None, None, :]) // SEL_BLOCK
    e = (key_blk == jnp.arange(n_sel)[None, :, None]).astype(BF16)
    kern = functools.partial(_nsa_attn_kernel, tk=tk, wk=wk)
    per_q = lambda nh, w: pl.BlockSpec((1, nh, qb, w), lambda b, g, i: (b, g, i, 0))
    full = lambda w: pl.BlockSpec((1, 1, t_len, w), lambda b, g, i: (b, g, 0, 0))
    return pl.pallas_call(
        kern,
        grid=(n_batch, NSA_KV_HEADS, nq),
        in_specs=[per_q(NSA_REP, hd), per_q(NSA_REP, hd), per_q(1, n_sel), per_q(1, 128),
                  full(hd), full(2 * hd), full(hd), full(2 * hd),
                  pl.BlockSpec((n_tiles, n_sel, tk), lambda b, g, i: (0, 0, 0))],
        out_specs=pl.BlockSpec((qb, NSA_REP * hd), lambda b, g, i: (b * nq + i, g)),
        out_shape=jax.ShapeDtypeStruct((n_batch * t_len, NSA_HEADS * hd), BF16),
        scratch_shapes=[pltpu.VMEM((NSA_REP * qb, 1), F32), pltpu.VMEM((NSA_REP * qb, 2 * hd), F32)],
        compiler_params=_params("arbitrary", "arbitrary", "arbitrary"),
        name="nsa_attend",
    )(qr, o_cmp, selb, gates, ks, vs, kw, vw, e)


def _gla_kernel(q_ref, k_ref, v_ref, r_ref, small_ref, w2_ref, b2_ref, ng_ref, s0_ref, y_ref, sf_ref, s_scr):
    tb = q_ref.shape[0]
    c, sub, dk, dv = GLA_CHUNK, GLA_SUB, GLA_DK, GLA_DV
    n_sub = c // sub
    t = pl.program_id(1)

    @pl.when(t == 0)
    def _():
        s_scr[...] = s0_ref[0]

    z = jnp.dot(small_ref[:, 0:GLA_RANK].astype(BF16), w2_ref[...].astype(BF16),
                preferred_element_type=F32) + b2_ref[...]
    g_all = (jnp.minimum(z, 0.0) - jnp.log1p(jnp.exp(-jnp.abs(z)))) / GLA_GATE_NORM
    ri = lax.broadcasted_iota(jnp.int32, (c, c), 0)
    ci = lax.broadcasted_iota(jnp.int32, (c, c), 1)
    tril = ri >= ci
    cum = tril.astype(F32)
    rsub = lax.broadcasted_iota(jnp.int32, (c, dk), 0) // sub
    eye = lax.broadcasted_iota(jnp.int32, (dk, dk), 0) == lax.broadcasted_iota(jnp.int32, (dk, dk), 1)
    for cc in range(tb // c):
        rows = slice(cc * c, (cc + 1) * c)
        b_all = jnp.dot(cum, g_all[rows], preferred_element_type=F32, precision=lax.Precision.HIGHEST)
        for h in range(GLA_HEADS):
            b = b_all[:, h * dk:(h + 1) * dk]
            qh = q_ref[rows, h * dk:(h + 1) * dk] * (dk ** -0.5)
            kh = k_ref[rows, h * dk:(h + 1) * dk]
            vh = v_ref[rows, h * dv:(h + 1) * dv]
            a_rows = []
            for i in range(n_sub):
                ref = b[sub * i - 1:sub * i, :] if i else jnp.zeros((1, dk), F32)
                rs = slice(sub * i, sub * (i + 1))
                qi = (qh[rs] * jnp.exp(b[rs] - ref)).astype(BF16)
                ki = jnp.where(rsub <= i, kh * jnp.exp(ref - b), 0.0).astype(BF16)
                a_rows.append(lax.dot_general(qi, ki, (((1,), (1,)), ((), ())), preferred_element_type=F32))
            a = jnp.where(tril, jnp.concatenate(a_rows, axis=0), 0.0)
            s_old = s_scr[h]
            o = jnp.dot(a.astype(BF16), vh.astype(BF16), preferred_element_type=F32)
            o += jnp.dot((qh * jnp.exp(b)).astype(BF16), s_old.astype(BF16), preferred_element_type=F32)
            b_last = b[c - 1:c, :]
            ke = (kh * jnp.exp(b_last - b)).astype(BF16)
            upd = lax.dot_general(ke, vh.astype(BF16), (((0,), (0,)), ((), ())), preferred_element_type=F32)
            decay = jnp.exp(jnp.sum(jnp.where(eye, jnp.broadcast_to(b_last, (dk, dk)), 0.0), axis=1, keepdims=True))
            s_scr[h] = decay * s_old + upd
            o = o * lax.rsqrt(jnp.mean(o * o, axis=-1, keepdims=True) + RMS_EPS)
            y = o * ng_ref[:, h * dv:(h + 1) * dv] * jax.nn.silu(r_ref[rows, h * dv:(h + 1) * dv])
            y_ref[rows, h * dv:(h + 1) * dv] = y.astype(BF16)

    @pl.when(t == pl.num_programs(1) - 1)
    def _():
        sf_ref[0] = s_scr[...]


def gla_mix(p, s0, w2, b2, norm_g, n_batch, t_len):
    tb = _pick_tile(t_len, (256, 128, 64))
    nt = t_len // tb
    row = lambda w, off: pl.BlockSpec((tb, w), lambda b, i: (b * nt + i, off // w))
    const = lambda shape: pl.BlockSpec(shape, lambda b, i: (0,) * len(shape))
    return pl.pallas_call(
        _gla_kernel,
        grid=(n_batch, nt),
        in_specs=[row(256, COL_GLA_Q), row(256, COL_GLA_K), row(512, COL_GLA_V), row(512, COL_GLA_R),
                  row(128, COL_SMALL), const((GLA_RANK, GLA_HEADS * GLA_DK)), const((1, GLA_HEADS * GLA_DK)),
                  const((1, GLA_WIDTH)),
                  pl.BlockSpec((1, GLA_HEADS, GLA_DK, GLA_DV), lambda b, i: (b, 0, 0, 0))],
        out_specs=(pl.BlockSpec((tb, GLA_WIDTH), lambda b, i: (b * nt + i, 0)),
                   pl.BlockSpec((1, GLA_HEADS, GLA_DK, GLA_DV), lambda b, i: (b, 0, 0, 0))),
        out_shape=(jax.ShapeDtypeStruct((n_batch * t_len, GLA_WIDTH), BF16),
                   jax.ShapeDtypeStruct((n_batch, GLA_HEADS, GLA_DK, GLA_DV), F32)),
        scratch_shapes=[pltpu.VMEM((GLA_HEADS, GLA_DK, GLA_DV), F32)],
        compiler_params=_params("arbitrary", "arbitrary"),
        name="gla_mix",
    )(p, p, p, p, p, w2, b2.reshape(1, -1), norm_g.reshape(1, -1), s0)


def _pool_kernel(u_ref, prev_ref, cnt_ref, w_ref, sc_ref, y_ref, halo):
    tb = u_ref.shape[0]
    gd = POOL_GROUP_DIM

    @pl.when(pl.program_id(1) == 0)
    def _():
        halo[...] = prev_ref[0]

    ext = jnp.concatenate([halo[...], u_ref[...]], axis=0)
    halo[...] = ext[tb:tb + POOL_MAX]
    for gi, w in enumerate(POOL_WINDOWS):
        x = ext[:, gi * gd:(gi + 1) * gd]
        s = x
        shift = 1
        while shift < w:
            s = s + pltpu.roll(s, shift, 0)
            shift *= 2
        pooled = s[POOL_MAX:] / cnt_ref[:, gi:gi + 1] - x[POOL_MAX:]
        y = jnp.dot(pooled.astype(BF16), w_ref[gi].astype(BF16), preferred_element_type=F32)
        y_ref[:, gi * gd:(gi + 1) * gd] = (y * sc_ref[:, gi * gd:(gi + 1) * gd]).astype(BF16)


def pool_mix(p, prev, pos0, w_pool, scale, n_batch, t_len):
    tb = _pick_tile(t_len, (512, 256, 128, 64, 32, 16))
    nt = t_len // tb
    pos = pos0 + jnp.arange(t_len, dtype=jnp.int32)
    cnt = jnp.stack([jnp.minimum(pos + 1, w).astype(F32) for w in POOL_WINDOWS], axis=1)
    cnt = jnp.pad(cnt, ((0, 0), (0, 128 - POOL_GROUPS)), constant_values=1.0)
    prev16 = jnp.pad(prev.astype(F32), ((0, 0), (1, 0), (0, 0)))
    return pl.pallas_call(
        _pool_kernel,
        grid=(n_batch, nt),
        in_specs=[pl.BlockSpec((tb, POOL_WIDTH), lambda b, i: (b * nt + i, COL_POOL // POOL_WIDTH)),
                  pl.BlockSpec((1, POOL_MAX, POOL_WIDTH), lambda b, i: (b, 0, 0)),
                  pl.BlockSpec((tb, 128), lambda b, i: (i, 0)),
                  pl.BlockSpec((POOL_GROUPS, POOL_GROUP_DIM, POOL_GROUP_DIM), lambda b, i: (0, 0, 0)),
                  pl.BlockSpec((1, POOL_WIDTH), lambda b, i: (0, 0))],
        out_specs=pl.BlockSpec((tb, POOL_WIDTH), lambda b, i: (b * nt + i, 0)),
        out_shape=jax.ShapeDtypeStruct((n_batch * t_len, POOL_WIDTH), BF16),
        scratch_shapes=[pltpu.VMEM((POOL_MAX, POOL_WIDTH), F32)],
        compiler_params=_params("arbitrary", "arbitrary"),
        name="pool_mix",
    )(p, prev16, cnt, w_pool, scale.reshape(1, -1))


def _layer_norm_rows(x, g, b):
    xc = x - jnp.mean(x, axis=-1, keepdims=True)
    var = jnp.mean(xc * xc, axis=-1, keepdims=True)
    return xc * lax.rsqrt(var + LN_EPS) * g + b


def _outproj_kernel(x_ref, yg_ref, yn_ref, yp_ref, w_ref, g_ref, b_ref, *rest):
    h = jnp.dot(yg_ref[...], w_ref[0:GLA_WIDTH, :], preferred_element_type=F32)
    h += jnp.dot(yn_ref[...], w_ref[GLA_WIDTH:GLA_WIDTH + NSA_WIDTH, :], preferred_element_type=F32)
    h += jnp.dot(yp_ref[...], w_ref[GLA_WIDTH + NSA_WIDTH:, :], preferred_element_type=F32)
    x1 = _layer_norm_rows(ALPHA * x_ref[...] + h, g_ref[...], b_ref[...])
    if len(rest) == 1:
        rest[0][...] = x1
    else:
        rh_ref, rl_ref, o_ref, lg_ref = rest
        o_ref[...] = x1
        xh = x1.astype(BF16)
        xl = (x1 - xh.astype(F32)).astype(BF16)
        lg_ref[...] = (jnp.dot(xh, rh_ref[...], preferred_element_type=F32)
                       + jnp.dot(xl, rh_ref[...], preferred_element_type=F32)
                       + jnp.dot(xh, rl_ref[...], preferred_element_type=F32))


def router_split(router):
    r = jnp.pad(router, ((0, 0), (0, 128 - router.shape[1])))
    hi = r.astype(BF16)
    return hi, (r - hi.astype(F32)).astype(BF16)


def outproj_ln(x, y_gla, y_nsa, y_pool, w_out_bf16, g, b, router=None):
    n, d = x.shape
    tm = _pick_tile(n, (512, 256, 128, 64, 32, 16, 8))
    row = lambda w: pl.BlockSpec((tm, w), lambda i: (i, 0))
    const = lambda r, c: pl.BlockSpec((r, c), lambda i: (0, 0))
    in_specs = [row(d), row(GLA_WIDTH), row(NSA_WIDTH), row(POOL_WIDTH), const(d, d), const(1, d), const(1, d)]
    args = [x, y_gla, y_nsa, y_pool, w_out_bf16, g.reshape(1, -1), b.reshape(1, -1)]
    out_specs, out_shape = row(d), jax.ShapeDtypeStruct((n, d), F32)
    if router is not None:
        in_specs += [const(d, 128), const(d, 128)]
        args += list(router_split(router))
        out_specs, out_shape = (out_specs, row(128)), (out_shape, jax.ShapeDtypeStruct((n, 128), F32))
    return pl.pallas_call(
        _outproj_kernel,
        grid=(n // tm,),
        in_specs=in_specs,
        out_specs=out_specs,
        out_shape=out_shape,
        compiler_params=_params("arbitrary"),
        name="outproj_ln",
    )(*args)


MOE_TM = 512
MOE_TM_DOWN = 512
MOE_VMEM_BYTES = 60 * 1024 * 1024
MOE_TF = 1024
MOE_TN = 512
ROUTE_TM = 512
PERMUTE_CHUNK = 1024


def _route_kernel(lg_ref, ii_ref, gf_ref, cnt_ref, carry, *, n_valid):
    tm = lg_ref.shape[0]
    i = pl.program_id(0)

    @pl.when(i == 0)
    def _():
        carry[...] = jnp.zeros_like(carry)

    lane = lax.broadcasted_iota(jnp.int32, (tm, 128), 1)
    valid = (i * tm + lax.broadcasted_iota(jnp.int32, (tm, 128), 0)) < n_valid
    lg = jnp.where(lane < N_EXPERTS, lg_ref[...], -jnp.inf)
    m1 = jnp.max(lg, axis=-1, keepdims=True)
    i1 = jnp.min(jnp.where(lg == m1, lane, 128), axis=-1, keepdims=True)
    lg2 = jnp.where(lane == i1, -jnp.inf, lg)
    m2 = jnp.max(lg2, axis=-1, keepdims=True)
    i2 = jnp.min(jnp.where(lg2 == m2, lane, 128), axis=-1, keepdims=True)
    t = jnp.exp(m2 - m1)
    g1 = 1.0 / (1.0 + t)
    g2 = t / (1.0 + t)
    oh1 = jnp.where((lane == i1) & valid, 1.0, 0.0)
    oh2 = jnp.where((lane == i2) & valid, 1.0, 0.0)
    cnt = oh1 + oh2
    strict = (lax.broadcasted_iota(jnp.int32, (tm, tm), 0) > lax.broadcasted_iota(jnp.int32, (tm, tm), 1))
    before = jnp.dot(strict.astype(BF16), cnt.astype(BF16), preferred_element_type=F32) + carry[...]
    r1 = jnp.sum(before * oh1, axis=-1, keepdims=True).astype(jnp.int32)
    r2 = jnp.sum(before * oh2, axis=-1, keepdims=True).astype(jnp.int32)
    carry[...] += jnp.sum(cnt, axis=0, keepdims=True)
    ii_ref[...] = jnp.where(lane == 0, i1, jnp.where(lane == 1, i2, jnp.where(lane == 2, r1, r2)))
    gf_ref[...] = jnp.where(lane == 0, g1, g2)
    cnt_ref[...] = carry[...]


def moe_route(logits, n_valid):
    npad = logits.shape[0]
    tm = ROUTE_TM
    row = pl.BlockSpec((tm, 128), lambda i: (i, 0))
    info, gates, counts = pl.pallas_call(
        functools.partial(_route_kernel, n_valid=n_valid),
        grid=(npad // tm,),
        in_specs=[row],
        out_specs=(row, row, pl.BlockSpec((1, 128), lambda i: (0, 0))),
        out_shape=(jax.ShapeDtypeStruct((npad, 128), jnp.int32), jax.ShapeDtypeStruct((npad, 128), F32),
                   jax.ShapeDtypeStruct((1, 128), F32)),
        scratch_shapes=[pltpu.VMEM((1, 128), F32)],
        compiler_params=_params("arbitrary"),
        name="moe_route",
    )(logits)
    return info[:, 0:2], info[:, 2:4], gates, counts[0, :N_EXPERTS].astype(jnp.int32)


SLAB = (16, 128)


def _slabify_kernel(x_ref, *rest):
    o_ref = rest[-1]

    def slab_rows(src_ref, n_rows):
        for c in range(SLAB[0]):
            o_ref[0:n_rows, c, :] = src_ref[:, c * SLAB[1]:(c + 1) * SLAB[1]]

    if len(rest) == 1:
        slab_rows(x_ref, x_ref.shape[0])
        return
    t_ref = rest[0]
    last = pl.program_id(0) == pl.num_programs(0) - 1

    @pl.when(jnp.logical_not(last))
    def _():
        slab_rows(x_ref, x_ref.shape[0])

    @pl.when(last)
    def _():
        o_ref[...] = jnp.zeros_like(o_ref)
        slab_rows(t_ref, t_ref.shape[0])


def slabify(x, tail=None):
    n, d = x.shape
    tm = _pick_tile(n, (512, 256, 128, 64, 32, 16, 8))
    nt = n // tm
    in_specs = [pl.BlockSpec((tm, d), lambda i: (jnp.minimum(i, nt - 1), 0))]
    args = [x]
    if tail is not None:
        assert tail.shape[0] <= tm
        in_specs.append(pl.BlockSpec(tail.shape, lambda i: (0, 0)))
        args.append(tail)
    steps = nt + (tail is not None)
    return pl.pallas_call(
        _slabify_kernel,
        grid=(steps,),
        in_specs=in_specs,
        out_specs=pl.BlockSpec((tm,) + SLAB, lambda i: (i, 0, 0)),
        out_shape=jax.ShapeDtypeStruct((steps * tm,) + SLAB, x.dtype),
        compiler_params=_params("arbitrary"),
        name="slabify",
    )(*args)


def _unslab(ref):
    return jnp.concatenate([ref[:, c, :] for c in range(SLAB[0])], axis=1)


GATHER_UNROLL = 8


def _gather_slabs_kernel(idx_ref, src_ref, o_ref, sem, *stage):
    ch = idx_ref.shape[2]
    dst = stage[0] if stage else o_ref

    def row_copy(r):
        return pltpu.make_async_copy(src_ref.at[idx_ref[0, 0, r]], dst.at[r], sem)

    def issue(t, c):
        for u in range(GATHER_UNROLL):
            row_copy(t * GATHER_UNROLL + u).start(priority=u % 2)
        return c

    lax.fori_loop(0, ch // GATHER_UNROLL, issue, 0)

    def drain(t, c):
        for u in range(GATHER_UNROLL):
            row_copy(t * GATHER_UNROLL + u).wait()
        return c

    lax.fori_loop(0, ch // GATHER_UNROLL, drain, 0)
    if stage:
        o_ref[...] = _unslab(dst).astype(o_ref.dtype)


def gather_slabs(src, idx, as_bf16_rows=False):
    n = idx.shape[0]
    ch = PERMUTE_CHUNK
    scratch = [pltpu.SemaphoreType.DMA(())]
    if as_bf16_rows:
        d = SLAB[0] * SLAB[1]
        out_specs, out_shape = pl.BlockSpec((ch, d), lambda i: (i, 0)), jax.ShapeDtypeStruct((n, d), BF16)
        scratch.append(pltpu.VMEM((ch,) + SLAB, src.dtype))
    else:
        out_specs = pl.BlockSpec((ch,) + SLAB, lambda i: (i, 0, 0))
        out_shape = jax.ShapeDtypeStruct((n,) + SLAB, src.dtype)
    return pl.pallas_call(
        _gather_slabs_kernel,
        grid=(n // ch,),
        in_specs=[pl.BlockSpec((1, 1, ch), lambda i: (i, 0, 0), memory_space=pltpu.SMEM),
                  pl.BlockSpec(memory_space=pl.ANY)],
        out_specs=out_specs,
        out_shape=out_shape,
        scratch_shapes=scratch,
        compiler_params=_params("arbitrary"),
        name="gather_slabs",
    )(idx.reshape(n // ch, 1, ch), src)


def _moe_up_kernel(te_ref, tfirst_ref, tused_ref, x_ref, wg_ref, wu_ref, h_ref, wgb, wub):
    i = pl.program_id(1)

    @pl.when((i == 0) | (tfirst_ref[i] == 1))
    def _():
        wgb[...] = wg_ref[...].astype(BF16)
        wub[...] = wu_ref[...].astype(BF16)

    @pl.when(tused_ref[i] == 1)
    def _():
        xb = x_ref[...]
        gate = jnp.dot(xb, wgb[...], preferred_element_type=F32)
        up = jnp.dot(xb, wub[...], preferred_element_type=F32)
        h_ref[...] = (jax.nn.silu(gate) * up).astype(BF16)

    @pl.when(tused_ref[i] == 0)
    def _():
        h_ref[...] = jnp.zeros_like(h_ref)


def _moe_down_kernel(te_ref, tfirst_ref, tused_ref, h_ref, wd_ref, y_ref, wdb):
    i = pl.program_id(1)

    @pl.when((i == 0) | (tfirst_ref[i] == 1))
    def _():
        wdb[...] = wd_ref[...].astype(BF16)

    @pl.when(tused_ref[i] == 1)
    def _():
        y_ref[...] = jnp.dot(h_ref[...], wdb[...], preferred_element_type=F32)

    @pl.when(tused_ref[i] == 0)
    def _():
        y_ref[...] = jnp.zeros_like(y_ref)


def _tile_meta(tile_e, tile_used, split):
    te = jnp.repeat(tile_e, split)
    first = jnp.concatenate([jnp.ones((1,), jnp.int32), (te[1:] != te[:-1]).astype(jnp.int32)])
    return te, first, jnp.repeat(tile_used, split)


def moe_experts(xs, tile_e, tile_used, wg, wu, wd):
    r, d = xs.shape
    d_ff = wg.shape[2]
    tm, tf, tn = MOE_TM, MOE_TF, MOE_TN
    h = pl.pallas_call(
        _moe_up_kernel,
        grid_spec=pltpu.PrefetchScalarGridSpec(
            num_scalar_prefetch=3,
            grid=(d_ff // tf, r // tm),
            in_specs=[pl.BlockSpec((tm, d), lambda j, i, te, t1, tu: (i, 0)),
                      pl.BlockSpec((None, d, tf), lambda j, i, te, t1, tu: (te[i], 0, j)),
                      pl.BlockSpec((None, d, tf), lambda j, i, te, t1, tu: (te[i], 0, j))],
            out_specs=pl.BlockSpec((tm, tf), lambda j, i, te, t1, tu: (i, j)),
            scratch_shapes=[pltpu.VMEM((d, tf), BF16), pltpu.VMEM((d, tf), BF16)]),
        out_shape=jax.ShapeDtypeStruct((r, d_ff), BF16),
        compiler_params=_params("arbitrary", "arbitrary", vmem_limit_bytes=MOE_VMEM_BYTES),
        name="moe_up",
    )(*_tile_meta(tile_e, tile_used, 1), xs, wg, wu)
    tmd = MOE_TM_DOWN
    return pl.pallas_call(
        _moe_down_kernel,
        grid_spec=pltpu.PrefetchScalarGridSpec(
            num_scalar_prefetch=3,
            grid=(d // tn, r // tmd),
            in_specs=[pl.BlockSpec((tmd, d_ff), lambda j, i, te, t1, tu: (i, 0)),
                      pl.BlockSpec((None, d_ff, tn), lambda j, i, te, t1, tu: (te[i], 0, j))],
            out_specs=pl.BlockSpec((tmd, tn), lambda j, i, te, t1, tu: (i, j)),
            scratch_shapes=[pltpu.VMEM((d_ff, tn), BF16)]),
        out_shape=jax.ShapeDtypeStruct((r, d), F32),
        compiler_params=_params("arbitrary", "arbitrary", vmem_limit_bytes=MOE_VMEM_BYTES),
        name="moe_down",
    )(*_tile_meta(tile_e, tile_used, tm // tmd), h, wd)


def _moe_combine_kernel(x_ref, y0_ref, y1_ref, gt_ref, g_ref, b_ref, o_ref):
    gt = gt_ref[...]
    y = gt[:, 0:1] * _unslab(y0_ref) + gt[:, 1:2] * _unslab(y1_ref)
    o_ref[...] = _layer_norm_rows(ALPHA * x_ref[...] + y, g_ref[...], b_ref[...])


def moe_combine_ln(x, yg, gates, row0, n_tok_pad, g, b):
    n, d = x.shape
    tm = _pick_tile(n, (512, 256, 128, 64, 32, 16, 8))
    o0, o1 = row0 // tm, (n_tok_pad + row0) // tm
    return pl.pallas_call(
        _moe_combine_kernel,
        grid=(n // tm,),
        in_specs=[pl.BlockSpec((tm, d), lambda i: (i, 0)),
                  pl.BlockSpec((tm,) + SLAB, lambda i: (o0 + i, 0, 0)),
                  pl.BlockSpec((tm,) + SLAB, lambda i: (o1 + i, 0, 0)),
                  pl.BlockSpec((tm, 128), lambda i: (o0 + i, 0)),
                  pl.BlockSpec((1, d), lambda i: (0, 0)), pl.BlockSpec((1, d), lambda i: (0, 0))],
        out_specs=pl.BlockSpec((tm, d), lambda i: (i, 0)),
        out_shape=jax.ShapeDtypeStruct((n, d), F32),
        compiler_params=_params("arbitrary"),
        name="moe_combine_ln",
    )(x, yg, yg, gates, g.reshape(1, -1), b.reshape(1, -1))


def moe_ln(x_groups, logit_groups, wg, wu, wd, g, b):
    d = x_groups[0].shape[1]
    n_tok = sum(x.shape[0] for x in x_groups)
    n_tok_pad = -(-n_tok // PERMUTE_CHUNK) * PERMUTE_CHUNK
    n_tok_pad = -(-n_tok_pad // ROUTE_TM) * ROUTE_TM
    logits = jnp.concatenate(logit_groups + [jnp.zeros((n_tok_pad - n_tok, 128), F32)], axis=0)
    experts, ranks, gates, counts = moe_route(logits, n_tok)
    tm = MOE_TM
    n_tiles = -(-(n_tok * TOP_K + N_EXPERTS * (tm - 1)) // tm)
    n_tiles = -(-n_tiles * tm // PERMUTE_CHUNK) * PERMUTE_CHUNK // tm
    padded = (counts + tm - 1) // tm * tm
    pad_end = jnp.cumsum(padded)
    pad_start = pad_end - padded
    valid = (jnp.arange(n_tok_pad) < n_tok)[:, None]
    dest = jnp.where(valid, pad_start[experts] + ranks, 0)
    tok = jnp.broadcast_to(jnp.arange(n_tok_pad, dtype=jnp.int32)[:, None], dest.shape)
    row_tok = jnp.zeros((n_tiles * tm,), jnp.int32).at[jnp.where(valid, dest, n_tiles * tm).reshape(-1)].set(
        tok.reshape(-1), mode='drop')
    tile_start = jnp.arange(n_tiles, dtype=jnp.int32) * tm
    tile_e = jnp.minimum(jnp.searchsorted(pad_end, tile_start, side='right'), N_EXPERTS - 1).astype(jnp.int32)
    tile_used = (tile_start < pad_end[-1]).astype(jnp.int32)
    assert len(x_groups) == 2
    xs = gather_slabs(slabify(x_groups[0], x_groups[1]), row_tok, as_bf16_rows=True)
    ys = slabify(moe_experts(xs, tile_e, tile_used, wg, wu, wd))
    yg = gather_slabs(ys, jnp.concatenate([dest[:, 0], dest[:, 1]]).astype(jnp.int32))
    outs, row0 = [], 0
    for x in x_groups:
        outs.append(moe_combine_ln(x, yg, gates, row0, n_tok_pad, g, b))
        row0 += x.shape[0]
    return outs


def _ffn_kernel(x_ref, wg_ref, wu_ref, wd_ref, g_ref, b_ref, o_ref, xb_ref, *, d_ff):
    j = pl.program_id(1)
    tf = wg_ref.shape[1]

    @pl.when(j == 0)
    def _():
        xb_ref[...] = x_ref[...].astype(BF16)
        o_ref[...] = jnp.zeros_like(o_ref)

    xb = xb_ref[...]
    gate = jnp.dot(xb, wg_ref[...], preferred_element_type=F32)
    up = jnp.dot(xb, wu_ref[...], preferred_element_type=F32)
    col = j * tf + lax.broadcasted_iota(jnp.int32, (1, tf), 1)
    a = jnp.where(col < d_ff, jax.nn.silu(gate) * up, 0.0).astype(BF16)
    rowi = j * tf + lax.broadcasted_iota(jnp.int32, (tf, 1), 0)
    wd = jnp.where(rowi < d_ff, wd_ref[...], jnp.zeros((), BF16))
    o_ref[...] += jnp.dot(a, wd, preferred_element_type=F32)

    @pl.when(j == pl.num_programs(1) - 1)
    def _():
        o_ref[...] = _layer_norm_rows(ALPHA * x_ref[...] + o_ref[...], g_ref[...], b_ref[...])


def ffn_ln(x, wg, wu, wd, g, b):
    n, d = x.shape
    d_ff = wg.shape[1]
    tm = _pick_tile(n, (512, 256, 128, 64, 32, 16, 8))
    tf = 512
    kern = functools.partial(_ffn_kernel, d_ff=d_ff)
    return pl.pallas_call(
        kern,
        grid=(n // tm, pl.cdiv(d_ff, tf)),
        in_specs=[pl.BlockSpec((tm, d), lambda i, j: (i, 0)),
                  pl.BlockSpec((d, tf), lambda i, j: (0, j)), pl.BlockSpec((d, tf), lambda i, j: (0, j)),
                  pl.BlockSpec((tf, d), lambda i, j: (j, 0)),
                  pl.BlockSpec((1, d), lambda i, j: (0, 0)), pl.BlockSpec((1, d), lambda i, j: (0, 0))],
        out_specs=pl.BlockSpec((tm, d), lambda i, j: (i, 0)),
        out_shape=jax.ShapeDtypeStruct((n, d), F32),
        scratch_shapes=[pltpu.VMEM((tm, d), BF16)],
        compiler_params=_params("arbitrary", "arbitrary"),
        name="ffn_ln",
    )(x, wg, wu, wd, g.reshape(1, -1), b.reshape(1, -1))


CMP_PAGES = 32
SEL_PAD_DEC = 384


def _cmp_paged_kernel(pt_ref, cache_ref, pe_ref, w1_ref, w2_ref, o_ref, buf, hlo, hhi, sems, *, layer, n_pages, page):
    b, ch = pl.program_id(0), pl.program_id(1)
    n_ch = pl.num_programs(1)
    hd = HEAD_DIM
    nh = CMP_PAGES * page // CMP_STRIDE
    step = b * n_ch + ch

    def copies(s, slot):
        first = s * CMP_PAGES
        return [pltpu.make_async_copy(
            cache_ref.at[layer, pt_ref[first + pg], :, c // NSA_KV_HEADS, c % NSA_KV_HEADS, :],
            buf.at[slot, c, pl.ds(pg * page, page), :], sems.at[slot])
            for pg in range(CMP_PAGES) for c in range(2 * NSA_KV_HEADS)]

    def for_slot(s, fn):
        for slot in range(2):
            @pl.when(s % 2 == slot)
            def _():
                fn(slot)

    @pl.when(step == 0)
    def _():
        for cp in copies(0, 0):
            cp.start()

    @pl.when(step + 1 < pl.num_programs(0) * n_ch)
    def _():
        for_slot(step + 1, lambda slot: [cp.start() for cp in copies(step + 1, slot)])

    for_slot(step, lambda slot: [cp.wait() for cp in copies(step, slot)])
    cur = step % 2

    for c in range(2 * NSA_KV_HEADS):
        kd, g = c // NSA_KV_HEADS, c % NSA_KV_HEADS
        xs = [buf[cur, c, pl.ds(j, nh, stride=CMP_STRIDE), :] for j in range(CMP_STRIDE)]
        lo = jnp.concatenate([(xs[j] + pe_ref[kd, j:j + 1, :]).astype(BF16) for j in range(CMP_STRIDE)], axis=1)
        hi = jnp.concatenate([(xs[j] + pe_ref[kd, CMP_STRIDE + j:CMP_STRIDE + j + 1, :]).astype(BF16)
                              for j in range(CMP_STRIDE)], axis=1)
        w_lo = w1_ref[kd, 0:CMP_STRIDE].reshape(CMP_STRIDE * hd, CMP_HIDDEN).astype(BF16)
        w_hi = w1_ref[kd, CMP_STRIDE:CMP_LEN].reshape(CMP_STRIDE * hd, CMP_HIDDEN).astype(BF16)
        rows = pl.ds(pl.multiple_of(ch * nh, nh), nh)
        hlo[c, rows, :] = jnp.dot(lo, w_lo, preferred_element_type=F32)
        hhi[c, rows, :] = jnp.dot(hi, w_hi, preferred_element_type=F32)

    @pl.when(ch == pl.num_programs(1) - 1)
    def _():
        n_all = hlo.shape[1]
        for c in range(2 * NSA_KV_HEADS):
            kd, g = c // NSA_KV_HEADS, c % NSA_KV_HEADS
            h = jax.nn.gelu(hlo[c] + pltpu.roll(hhi[c], n_all - 1, 0))
            o_ref[kd, g] = jnp.dot(h.astype(BF16), w2_ref[kd].astype(BF16), preferred_element_type=F32).astype(BF16)


def nsa_compress_paged(cache, layer, page_table, cmp_pos, cmp_w1, cmp_w2):
    page = cache.shape[2]
    n_batch, n_pages = page_table.shape
    hd = HEAD_DIM
    n_all = n_pages * page // CMP_STRIDE
    kern = functools.partial(_cmp_paged_kernel, layer=layer, n_pages=n_pages, page=page)
    const = lambda shape: pl.BlockSpec(shape, lambda b, c, pt: (0,) * len(shape))
    return pl.pallas_call(
        kern,
        grid_spec=pltpu.PrefetchScalarGridSpec(
            num_scalar_prefetch=1,
            grid=(n_batch, n_pages // CMP_PAGES),
            in_specs=[pl.BlockSpec(memory_space=pl.ANY), const((2, CMP_LEN, hd)),
                      const((2, CMP_LEN, hd, CMP_HIDDEN)), const((2, CMP_HIDDEN, hd))],
            out_specs=pl.BlockSpec((None, 2, NSA_KV_HEADS, n_all, hd), lambda b, c, pt: (b, 0, 0, 0, 0)),
            scratch_shapes=[pltpu.VMEM((2, 2 * NSA_KV_HEADS, CMP_PAGES * page, hd), F32),
                            pltpu.VMEM((2 * NSA_KV_HEADS, n_all, CMP_HIDDEN), F32),
                            pltpu.VMEM((2 * NSA_KV_HEADS, n_all, CMP_HIDDEN), F32),
                            pltpu.SemaphoreType.DMA((2,))]),
        out_shape=jax.ShapeDtypeStruct((n_batch, 2, NSA_KV_HEADS, n_all, hd), BF16),
        compiler_params=_params("arbitrary", "arbitrary"),
        name="nsa_compress_paged",
    )(page_table.reshape(-1).astype(jnp.int32), cache, cmp_pos, cmp_w1, cmp_w2)


def _sel_decode_kernel(qn_ref, kc_ref, vc_ref, ovt_ref, oc_ref, idx_ref, v_scr, psum_scr, *, n_cmp, n_sel, n_top,
                       q_pos):
    b = pl.program_id(0)
    n_cmp_pad = kc_ref.shape[2]
    nsp = ovt_ref.shape[0]
    n_rows = qn_ref.shape[0]
    col = lax.broadcasted_iota(jnp.int32, (n_rows, n_cmp_pad), 1)
    mask = (col * CMP_STRIDE + (CMP_LEN - 1) <= q_pos) & (col < n_cmp)
    psums = []
    for g in range(NSA_KV_HEADS):
        own = slice(g * NSA_REP, (g + 1) * NSA_REP)
        s = lax.dot_general(qn_ref[...], kc_ref[0, g], (((1,), (1,)), ((), ())), preferred_element_type=F32)
        s = jnp.where(mask, s, -jnp.inf)
        m = jnp.max(s, axis=-1, keepdims=True)
        m = jnp.where(m > -jnp.inf, m, 0.0)
        p = jnp.where(mask, jnp.exp(s - m), 0.0)
        p = p / jnp.maximum(jnp.sum(p, axis=-1, keepdims=True), 1e-30)
        oc = jnp.dot(p.astype(BF16), vc_ref[0, g], preferred_element_type=F32)
        oc_ref[own, :] = oc[own]
        psum_scr[pl.ds(b * NSA_KV_HEADS + g, 1), :] = jnp.sum(p[own], axis=0, keepdims=True)

    @pl.when(b == pl.num_programs(0) - 1)
    def _():
        n_col = psum_scr.shape[0]
        imp = lax.dot_general(ovt_ref[...], psum_scr[...], (((1,), (1,)), ((), ())), preferred_element_type=F32,
                              precision=lax.Precision.HIGHEST)
        blk = lax.broadcasted_iota(jnp.int32, (nsp, n_col), 0)
        cur = q_pos // SEL_BLOCK
        forced = (blk == 0) | (blk == cur) | (blk == cur - 1)
        v = jnp.where((blk <= cur) & (blk < n_sel), jnp.where(forced, jnp.inf, imp), -jnp.inf)
        v_scr[...] = v

        def count(i, rank):
            vi = v_scr[pl.ds(i, 1), :]
            ahead = (vi > v) | ((vi == v) & (blk > i))
            return rank + ahead.astype(jnp.int32)

        rank = lax.fori_loop(0, n_sel, count, jnp.zeros((nsp, n_col), jnp.int32))
        chosen = (rank < n_top) & (v > -jnp.inf)
        blk_f = blk.astype(F32)
        rows = [jnp.sum(jnp.where(chosen & (rank == t), blk_f, 0.0), axis=0, keepdims=True) for t in range(n_top)]
        idx_ref[...] = jnp.concatenate(rows, axis=0).astype(jnp.int32)


def nsa_select_decode(qn, cmp_kv, n_cmp, n_sel, q_pos):
    n_batch, n_heads, hd = qn.shape
    n_cmp_pad = cmp_kv.shape[3]
    nsp = SEL_PAD_DEC
    n_top = min(SEL_TOPN, n_sel)
    ci = jnp.arange(n_cmp_pad)[None, :]
    sj = jnp.arange(nsp)[:, None]
    overlap_t = ((ci * CMP_STRIDE <= sj * SEL_BLOCK + SEL_BLOCK - 1) &
                 (ci * CMP_STRIDE + CMP_LEN - 1 >= sj * SEL_BLOCK) & (ci < n_cmp) & (sj < n_sel)).astype(F32)
    kern = functools.partial(_sel_decode_kernel, n_cmp=n_cmp, n_sel=n_sel, n_top=n_top, q_pos=q_pos)
    kv = lambda kd: pl.BlockSpec((None, 1, NSA_KV_HEADS, n_cmp_pad, hd), lambda b: (b, kd, 0, 0, 0))
    n_col = n_batch * NSA_KV_HEADS
    o_cmp, idx = pl.pallas_call(
        kern,
        grid=(n_batch,),
        in_specs=[pl.BlockSpec((None, 2 * n_heads, hd), lambda b: (b, 0, 0)), kv(0), kv(1),
                  pl.BlockSpec((nsp, n_cmp_pad), lambda b: (0, 0))],
        out_specs=(pl.BlockSpec((None, n_heads, hd), lambda b: (b, 0, 0)),
                   pl.BlockSpec((n_top, n_col), lambda b: (0, 0))),
        out_shape=(jax.ShapeDtypeStruct((n_batch, n_heads, hd), F32),
                   jax.ShapeDtypeStruct((n_top, n_col), jnp.int32)),
        scratch_shapes=[pltpu.VMEM((nsp, n_col), F32), pltpu.VMEM((n_col, n_cmp_pad), F32)],
        compiler_params=_params("arbitrary"),
        name="nsa_select_decode",
    )(jnp.pad(qn, ((0, 0), (0, n_heads), (0, 0))), cmp_kv, cmp_kv, overlap_t)
    return o_cmp, idx.T.reshape(n_batch, NSA_KV_HEADS, n_top)


def _attn_decode_kernel(pt_ref, sel_ref, q_ref, k0_ref, v0_ref, k1_ref, v1_ref, new_ref, kw_ref, vw_ref, wnew_ref,
                        oc_ref, gate_ref, o_ref, m_scr, l_scr, acc_scr, *, n_past_blocks, n_top):
    b, slot = pl.program_id(0), pl.program_id(1)
    G = NSA_KV_HEADS
    nt = (((1,), (1,)), ((), ()))
    kv_refs = ((k0_ref, v0_ref), (k1_ref, v1_ref))
    for g in range(G):
        qb = q_ref[g]
        q = qb.astype(F32)
        kc_ref, vc_ref = kv_refs[g]

        @pl.when(slot == 0)
        def _():
            k_new = new_ref[2 * G + g:2 * G + g + 1, :]
            v_new = new_ref[3 * G + g:3 * G + g + 1, :]
            m_scr[g] = jnp.sum(q * k_new, axis=-1, keepdims=True)
            l_scr[g] = jnp.ones(l_scr.shape[1:], F32)
            acc_scr[g] = jnp.broadcast_to(v_new, acc_scr.shape[1:])

        @pl.when(sel_ref[(b * G + g) * n_top + slot] < n_past_blocks)
        def _():
            s = lax.dot_general(qb, kc_ref[:, g, :].astype(BF16), nt, preferred_element_type=F32)
            m_old = m_scr[g]
            m_new = jnp.maximum(m_old, jnp.max(s, axis=-1, keepdims=True))
            alpha = jnp.exp(m_old - m_new)
            p = jnp.exp(s - m_new)
            l_scr[g] = alpha * l_scr[g] + jnp.sum(p, axis=-1, keepdims=True)
            acc_scr[g] = alpha * acc_scr[g] + jnp.dot(p.astype(BF16), vc_ref[:, g, :].astype(BF16),
                                                      preferred_element_type=F32)
            m_scr[g] = m_new

        @pl.when(slot == n_top - 1)
        def _():
            o_sel = acc_scr[g] / jnp.maximum(l_scr[g], 1e-30)
            n_buf = kw_ref.shape[0]
            s = lax.dot_general(qb, kw_ref[:, g, :].astype(BF16), nt, preferred_element_type=F32)
            keep = lax.broadcasted_iota(jnp.int32, s.shape, 1) > n_buf - WINDOW
            s = jnp.where(keep, s, NEG_BIG)
            kw_new = wnew_ref[g:g + 1, :]
            vw_new = wnew_ref[G + g:G + g + 1, :]
            s_new = jnp.sum(q * kw_new, axis=-1, keepdims=True)
            m = jnp.maximum(jnp.max(s, axis=-1, keepdims=True), s_new)
            p = jnp.exp(s - m)
            p_new = jnp.exp(s_new - m)
            l = jnp.sum(p, axis=-1, keepdims=True) + p_new
            o_win = (jnp.dot(p.astype(BF16), vw_ref[:, g, :].astype(BF16), preferred_element_type=F32)
                     + p_new * vw_new) / l
            n_rows = qb.shape[0]
            gates = jnp.broadcast_to(gate_ref[g:g + 1, :], (n_rows, 128))
            lane = lax.broadcasted_iota(jnp.int32, (n_rows, 128), 1)
            head = lax.broadcasted_iota(jnp.int32, (n_rows, 128), 0)
            pick = lambda c: jnp.sum(jnp.where(lane == head * 3 + c, gates, 0.0), axis=-1, keepdims=True)
            o_ref[g] = (pick(0) * oc_ref[g] + pick(1) * o_sel + pick(2) * o_win).astype(o_ref.dtype)


def nsa_attend_decode(qr, o_cmp, sel_idx, gates, cache, layer, page_table, new_rows, state_win, new_win):
    n_batch, n_heads, hd = qr.shape
    page = cache.shape[2]
    n_pages = page_table.shape[1]
    n_top = sel_idx.shape[2]
    per_page = page // SEL_BLOCK
    n_past_blocks = n_pages * per_page
    n_buf = state_win.shape[2]
    G, R = NSA_KV_HEADS, NSA_REP

    def cache_spec(kind, g):
        def index(b, s, pt, sel):
            j = jnp.minimum(sel[(b * G + g) * n_top + s], n_past_blocks - 1)
            return layer, pt[b * n_pages + j // per_page], j % per_page, kind, 0, 0
        return pl.BlockSpec((None, None, SEL_BLOCK, None, G, hd), index)

    rp = 16
    pad_heads = lambda a: jnp.pad(a.reshape(n_batch, G, R, hd), ((0, 0), (0, 0), (0, rp - R), (0, 0)))
    per_bg = lambda: pl.BlockSpec((None, G, rp, hd), lambda b, s, pt, sel: (b, 0, 0, 0))
    per_b = lambda rows: pl.BlockSpec((None, rows, hd), lambda b, s, pt, sel: (b, 0, 0))
    win_spec = lambda kv: pl.BlockSpec((None, None, n_buf, None, G, hd),
                                       lambda b, s, pt, sel: (layer, b, 0, kv, 0, 0))
    kern = functools.partial(_attn_decode_kernel, n_past_blocks=n_past_blocks, n_top=n_top)
    out = pl.pallas_call(
        kern,
        grid_spec=pltpu.PrefetchScalarGridSpec(
            num_scalar_prefetch=2,
            grid=(n_batch, n_top),
            in_specs=[per_bg(), cache_spec(2, 0), cache_spec(3, 0), cache_spec(2, 1), cache_spec(3, 1),
                      per_b(4 * G), win_spec(0), win_spec(1), per_b(2 * G), per_bg(), per_b(G)],
            out_specs=per_bg(),
            scratch_shapes=[pltpu.VMEM((G, rp, 1), F32), pltpu.VMEM((G, rp, 1), F32), pltpu.VMEM((G, rp, hd), F32)]),
        out_shape=jax.ShapeDtypeStruct((n_batch, G, rp, hd), BF16),
        compiler_params=_params("arbitrary", "arbitrary"),
        name="nsa_attend_decode",
    )(page_table.reshape(-1).astype(jnp.int32), sel_idx.reshape(-1).astype(jnp.int32),
      pad_heads(qr), cache, cache, cache, cache, new_rows, state_win, state_win, new_win, pad_heads(o_cmp), gates)
    return out[:, :, :R].reshape(n_batch, n_heads * hd)


def _gla_decode_kernel(q_ref, k_ref, v_ref, r_ref, small_ref, w2_ref, b2_ref, ng_ref, s0_ref, y_ref, sf_ref):
    dk, dv = GLA_DK, GLA_DV
    z = jnp.dot(small_ref[:, 0:GLA_RANK].astype(BF16), w2_ref[...].astype(BF16),
                preferred_element_type=F32) + b2_ref[...]
    g_all = (jnp.minimum(z, 0.0) - jnp.log1p(jnp.exp(-jnp.abs(z)))) / GLA_GATE_NORM
    eye = lax.broadcasted_iota(jnp.int32, (dk, dk), 0) == lax.broadcasted_iota(jnp.int32, (dk, dk), 1)
    column = lambda row: jnp.sum(jnp.where(eye, jnp.broadcast_to(row, (dk, dk)), 0.0), axis=1, keepdims=True)
    for b in range(q_ref.shape[0]):
        for h in range(GLA_HEADS):
            ks = slice(h * dk, (h + 1) * dk)
            vs = slice(h * dv, (h + 1) * dv)
            s_new = (jnp.exp(column(g_all[b:b + 1, ks])) * s0_ref[b, h]
                     + column(k_ref[b:b + 1, ks]) * v_ref[b:b + 1, vs])
            sf_ref[b, h] = s_new
            o = jnp.sum(column(q_ref[b:b + 1, ks] * (dk ** -0.5)) * s_new, axis=0, keepdims=True)
            o = o * lax.rsqrt(jnp.mean(o * o, axis=-1, keepdims=True) + RMS_EPS)
            y_ref[b:b + 1, vs] = (o * ng_ref[:, vs] * jax.nn.silu(r_ref[b:b + 1, vs])).astype(BF16)


def gla_decode(p, s0, w2, b2, norm_g):
    n = p.shape[0]
    row = lambda w, off: pl.BlockSpec((n, w), lambda i: (0, off // w))
    const = lambda shape: pl.BlockSpec(shape, lambda i: (0,) * len(shape))
    return pl.pallas_call(
        _gla_decode_kernel,
        grid=(1,),
        in_specs=[row(256, COL_GLA_Q), row(256, COL_GLA_K), row(512, COL_GLA_V), row(512, COL_GLA_R),
                  row(128, COL_SMALL), const((GLA_RANK, GLA_HEADS * GLA_DK)), const((1, GLA_HEADS * GLA_DK)),
                  const((1, GLA_WIDTH)), const(s0.shape)],
        out_specs=(const((n, GLA_WIDTH)), const(s0.shape)),
        out_shape=(jax.ShapeDtypeStruct((n, GLA_WIDTH), BF16), jax.ShapeDtypeStruct(s0.shape, F32)),
        compiler_params=_params("arbitrary"),
        name="gla_decode",
    )(p, p, p, p, p, w2, b2.reshape(1, -1), norm_g.reshape(1, -1), s0)


def decode_mixer(xs2, w_in_packed, w_out_bf16, ln_g, ln_b, gla_w2, gla_b, gla_norm_g, cmp_pos, cmp_w1, cmp_w2,
                 pool_w, pool_scale, cache, layer, page_table, state_win, state_gla, state_pool, past_len,
                 router=None):
    n_dec = xs2.shape[0]
    hd, G = HEAD_DIM, NSA_KV_HEADS
    p = matmul(xs2, w_in_packed)
    pos = jnp.full((n_dec,), past_len, jnp.int32)
    rows, win, qn, qr, _, _, _, _, gates = nsa_prep(p, rope_tables(pos), 1, n_dec)
    cmp_kv = nsa_compress_paged(cache, layer, page_table, cmp_pos, cmp_w1, cmp_w2)
    t_k = past_len + 1
    n_sel = -(-t_k // SEL_BLOCK)
    o_cmp, sel_idx = nsa_select_decode(qn[0].transpose(1, 0, 2), cmp_kv, past_len // CMP_STRIDE - 1, n_sel, past_len)
    y_nsa = nsa_attend_decode(qr[0].transpose(1, 0, 2), o_cmp, sel_idx, gates[0].transpose(1, 0, 2), cache, layer,
                              page_table, rows.reshape(n_dec, 4 * G, hd), state_win, win.reshape(n_dec, 2 * G, hd))
    y_gla, s_gla = gla_decode(p, state_gla, gla_w2, gla_b, gla_norm_g)
    y_pool, pool_rows = pool_mixer(p[:, None, COL_POOL:COL_POOL + POOL_WIDTH], state_pool, pos[:1], pool_w, pool_scale)
    x1 = outproj_ln(xs2, y_gla, y_nsa, y_pool.reshape(n_dec, -1).astype(BF16), w_out_bf16, ln_g, ln_b, router)
    nsa_rows = rows.reshape(n_dec, 1, 4, G, hd)
    new_win = jnp.concatenate([state_win[layer, :, 1:], win.reshape(n_dec, 1, 2, G, hd)], axis=1)
    return x1, nsa_rows, new_win, s_gla, pool_rows


def prompt_mixer(x2, w_in_packed, w_out_bf16, ln_g, ln_b, gla_w2, gla_b, gla_norm_g, cmp_pos, cmp_w1, cmp_w2,
                 pool_w, pool_scale, n_batch, t_len, router=None):
    p = matmul(x2, w_in_packed)
    pos = jnp.arange(t_len, dtype=jnp.int32)
    rows, win, qn, qr, ks, vs, kw, vw, gates = nsa_prep(p, rope_tables(pos), n_batch, t_len)
    cmp_kv = nsa_compress_prompt(rows, cmp_pos, cmp_w1, cmp_w2, n_batch, t_len)
    o_cmp, selb = nsa_select(qn, cmp_kv, t_len // CMP_STRIDE - 1, t_len)
    y_nsa = nsa_attend(qr, o_cmp, selb, gates, ks, vs, kw, vw)
    s0 = jnp.zeros((n_batch, GLA_HEADS, GLA_DK, GLA_DV), F32)
    y_gla, s_gla = gla_mix(p, s0, gla_w2, gla_b, gla_norm_g, n_batch, t_len)
    prev = jnp.zeros((n_batch, POOL_MAX - 1, POOL_WIDTH), F32)
    y_pool = pool_mix(p, prev, 0, pool_w, pool_scale, n_batch, t_len)
    x1 = outproj_ln(x2, y_gla, y_nsa, y_pool, w_out_bf16, ln_g, ln_b, router)
    nsa_rows = rows.reshape(n_batch, t_len, 4, NSA_KV_HEADS, HEAD_DIM)
    n_win = min(WINDOW, t_len)
    win_rows = win.reshape(n_batch, t_len, 2, NSA_KV_HEADS, HEAD_DIM)[:, t_len - n_win:]
    pool_rows = p.reshape(n_batch, t_len, PACKED_WIDTH)[:, t_len - (POOL_MAX - 1):, COL_POOL:COL_POOL + POOL_WIDTH]
    return x1, nsa_rows, win_rows, s_gla, pool_rows


def split_proj(p):
    out = {}
    off = 0
    for name, size in PROJ_SIZES:
        out[name] = p[..., off:off + size]
        off += size
    return out


def layer_norm(x, g, b):
    xf = x.astype(jnp.float32)
    xc = xf - jnp.mean(xf, -1, keepdims=True)
    var = jnp.mean(xc * xc, -1, keepdims=True)
    return (xc * lax.rsqrt(var + LN_EPS) * g + b).astype(x.dtype)


def rope(x, pos):
    half = ROPE_DIM // 2
    inv_freq = ROPE_THETA ** (-jnp.arange(half, dtype=jnp.float32) / half)
    ang = pos.astype(jnp.float32)[:, None] * inv_freq[None, :]
    cos = jnp.cos(ang)[:, None, :]
    sin = jnp.sin(ang)[:, None, :]
    xf = x.astype(jnp.float32)
    x1, x2 = xf[..., :half], xf[..., half:ROPE_DIM]
    out = jnp.concatenate([x1 * cos - x2 * sin, x2 * cos + x1 * sin, xf[..., ROPE_DIM:]], axis=-1)
    return out.astype(x.dtype)


def masked_softmax(s, mask):
    s = jnp.where(mask, s.astype(jnp.float32), -jnp.inf)
    m = jnp.max(s, axis=-1, keepdims=True)
    m = jnp.where(jnp.isfinite(m), m, 0.0)
    p = jnp.where(mask, jnp.exp(s - m), 0.0)
    return p / jnp.maximum(jnp.sum(p, -1, keepdims=True), 1e-30)


def gla_recurrence(q, k, v, g, s0):
    B, T, H, _ = q.shape
    C = GLA_CHUNK
    n_chunks = -(-T // C)
    pad = n_chunks * C - T

    def prep(a):
        a = jnp.pad(a, ((0, 0), (0, pad), (0, 0), (0, 0)))
        return a.reshape(B, n_chunks, C, H, a.shape[-1]).transpose(1, 0, 3, 2, 4)

    causal = jnp.tril(jnp.ones((C, C), dtype=bool))

    def step(S, inp):
        qi, ki, vi, gi = [a.astype(jnp.float32) for a in inp]
        b = jnp.cumsum(gi, axis=2)
        o_inter = jnp.einsum('bhtk,bhkv->bhtv', qi * jnp.exp(b), S)
        diff = jnp.where(causal[:, :, None], b[:, :, :, None, :] - b[:, :, None, :, :], -jnp.inf)
        attn = jnp.einsum('bhtk,bhsk,bhtsk->bhts', qi, ki, jnp.exp(diff))
        o = o_inter + jnp.einsum('bhts,bhsv->bhtv', attn, vi)
        b_last = b[:, :, -1:, :]
        S = jnp.exp(b_last[:, :, 0, :])[..., None] * S + jnp.einsum('bhsk,bhsv->bhkv', ki * jnp.exp(b_last - b), vi)
        return S, o

    S, o = lax.scan(step, s0.astype(jnp.float32), (prep(q), prep(k), prep(v), prep(g)))
    o = o.transpose(1, 0, 3, 2, 4).reshape(B, n_chunks * C, H, v.shape[-1])[:, :T]
    return o, S.astype(s0.dtype)


def gla_mixer(parts, s0, w2, b2, norm_g):
    B, T = parts['gla_q'].shape[:2]
    q = parts['gla_q'].reshape(B, T, GLA_HEADS, GLA_DK) * (GLA_DK ** -0.5)
    k = parts['gla_k'].reshape(B, T, GLA_HEADS, GLA_DK)
    v = parts['gla_v'].reshape(B, T, GLA_HEADS, GLA_DV)
    g = jax.nn.log_sigmoid((parts['gla_glr'] @ w2 + b2).astype(jnp.float32)) / GLA_GATE_NORM
    g = g.reshape(B, T, GLA_HEADS, GLA_DK)
    o, s_new = gla_recurrence(q, k, v, g, s0)
    o = o * lax.rsqrt(jnp.mean(o * o, -1, keepdims=True) + RMS_EPS)
    out = o.reshape(B, T, GLA_WIDTH) * norm_g * jax.nn.silu(parts['gla_r'].astype(jnp.float32))
    return out.astype(parts['gla_v'].dtype), s_new


def nsa_compress(kx, pos_emb, w1, w2):
    B, T, G, D = kx.shape
    nh = T // CMP_STRIDE
    halves = kx[:, :nh * CMP_STRIDE].reshape(B, nh, CMP_STRIDE, G, D).astype(jnp.float32)
    pe = pos_emb.reshape(2, CMP_STRIDE, D)
    w = w1.reshape(2, CMP_STRIDE, D, CMP_HIDDEN)
    h_lo = jnp.einsum('bnjgd,jdh->bngh', halves + pe[0][None, None, :, None, :], w[0])
    h_hi = jnp.einsum('bnjgd,jdh->bngh', halves + pe[1][None, None, :, None, :], w[1])
    h = jax.nn.gelu(h_lo[:, :-1] + h_hi[:, 1:])
    return jnp.einsum('bngh,hd->bngd', h, w2)


def nsa_global(qn, qr, rows, q_pos, cmp_pos, cmp_w1, cmp_w2):
    B, Tq, G, R, D = qn.shape
    Tk = rows.shape[1]
    scale = HEAD_DIM ** -0.5
    kcmp = nsa_compress(rows[:, :, 0], cmp_pos[0], cmp_w1[0], cmp_w2[0])
    vcmp = nsa_compress(rows[:, :, 1], cmp_pos[1], cmp_w1[1], cmp_w2[1])
    n_cmp = kcmp.shape[1]
    cmp_end = jnp.arange(n_cmp) * CMP_STRIDE + CMP_LEN - 1
    n_sel = -(-Tk // SEL_BLOCK)
    pad = n_sel * SEL_BLOCK - Tk

    def to_blocks(a):
        a = jnp.pad(a, ((0, 0), (0, pad), (0, 0), (0, 0)))
        return a.reshape(B, n_sel, SEL_BLOCK, G, D).transpose(0, 3, 1, 2, 4)

    ksb = to_blocks(rows[:, :, 2])
    vsb = to_blocks(rows[:, :, 3])
    ci = jnp.arange(n_cmp)[:, None]
    sj = jnp.arange(n_sel)[None, :]
    overlap = ((ci * CMP_STRIDE <= sj * SEL_BLOCK + SEL_BLOCK - 1) &
               (ci * CMP_STRIDE + CMP_LEN - 1 >= sj * SEL_BLOCK)).astype(jnp.float32)
    n_top = min(SEL_TOPN, n_sel)
    gather = jax.vmap(jax.vmap(lambda blocks, idx: blocks[idx]))
    blk_ids = jnp.arange(n_sel)

    def block_fn(args):
        qnb, qrb, qp = args
        qb = qp.shape[0]
        s = jnp.einsum('bqgrd,bngd->bgrqn', qnb, kcmp) * scale
        p_c = masked_softmax(s, cmp_end[None, :] <= qp[:, None])
        o_c = jnp.einsum('bgrqn,bngd->bqgrd', p_c, vcmp)
        imp = jnp.einsum('bgqn,nj->bgqj', jnp.sum(p_c, axis=2), overlap)
        cur = qp[:, None] // SEL_BLOCK
        valid = blk_ids[None, :] <= cur
        forced = (blk_ids[None, :] == 0) | (blk_ids[None, :] == cur) | (blk_ids[None, :] == cur - 1)
        imp = jnp.where(valid, jnp.where(forced, jnp.inf, imp), -jnp.inf)
        top_s, top_i = lax.top_k(imp, n_top)
        kg = gather(ksb, top_i)
        vg = gather(vsb, top_i)
        kpos = top_i[..., None] * SEL_BLOCK + jnp.arange(SEL_BLOCK)
        mask = (top_s > -jnp.inf)[..., None] & (kpos <= qp[None, None, :, None, None])
        s2 = jnp.einsum('bqgrd,bgqnjd->bgrqnj', qrb, kg) * scale
        s2 = s2.reshape(B, G, R, qb, n_top * SEL_BLOCK)
        p_s = masked_softmax(s2, mask.reshape(B, G, 1, qb, n_top * SEL_BLOCK))
        p_s = p_s.reshape(B, G, R, qb, n_top, SEL_BLOCK)
        o_s = jnp.einsum('bgrqnj,bgqnjd->bqgrd', p_s, vg.astype(jnp.float32))
        return o_c, o_s

    qb = min(Tq, Q_BLOCK)
    nq = -(-Tq // qb)
    padq = nq * qb - Tq
    qpad = ((0, 0), (0, padq), (0, 0), (0, 0), (0, 0))
    qn_b = jnp.pad(qn, qpad).reshape(B, nq, qb, G, R, D).swapaxes(0, 1)
    qr_b = jnp.pad(qr, qpad).reshape(B, nq, qb, G, R, D).swapaxes(0, 1)
    qp_b = jnp.pad(q_pos, (0, padq), mode='edge').reshape(nq, qb)
    o_c, o_s = lax.map(block_fn, (qn_b, qr_b, qp_b))
    o_c = o_c.swapaxes(0, 1).reshape(B, nq * qb, G, R, D)[:, :Tq]
    o_s = o_s.swapaxes(0, 1).reshape(B, nq * qb, G, R, D)[:, :Tq]
    return o_c, o_s


def band_attend(q, k, v, qpos, kpos):
    s = jnp.einsum('bqgrd,bkgd->bgrqk', q, k) * (HEAD_DIM ** -0.5)
    mask = ((kpos[None, :] <= qpos[:, None]) & (kpos[None, :] > qpos[:, None] - WINDOW) & (kpos[None, :] >= 0))
    p = masked_softmax(s, mask)
    return jnp.einsum('bgrqk,bkgd->bqgrd', p, v.astype(jnp.float32))


def sliding_prompt(q, k, v):
    B, T, G, R, D = q.shape
    kp = jnp.pad(k, ((0, 0), (WINDOW, 0), (0, 0), (0, 0)))
    vp = jnp.pad(v, ((0, 0), (WINDOW, 0), (0, 0), (0, 0)))
    nq = T // Q_BLOCK

    def fn(i):
        q0 = i * Q_BLOCK
        qi = lax.dynamic_slice_in_dim(q, q0, Q_BLOCK, axis=1)
        ki = lax.dynamic_slice_in_dim(kp, q0, WINDOW + Q_BLOCK, axis=1)
        vi = lax.dynamic_slice_in_dim(vp, q0, WINDOW + Q_BLOCK, axis=1)
        qpos = q0 + jnp.arange(Q_BLOCK)
        kpos = q0 - WINDOW + jnp.arange(WINDOW + Q_BLOCK)
        return band_attend(qi, ki, vi, qpos, kpos)

    o = lax.map(fn, jnp.arange(nq))
    return o.swapaxes(0, 1).reshape(B, T, G, R, D)


def nsa_mixer(parts, pos, nsa_past, win_past, cmp_pos, cmp_w1, cmp_w2):
    B, T = parts['nsa_q'].shape[:2]
    dt = parts['nsa_q'].dtype
    q = parts['nsa_q'].reshape(B, T, NSA_HEADS, HEAD_DIM)
    q_rope = rope(q, pos)
    kv = lambda name: parts[name].reshape(B, T, NSA_KV_HEADS, HEAD_DIM)
    k_win, v_win = rope(kv('win_k'), pos), kv('win_v')
    new_rows = jnp.stack([kv('cmp_k'), kv('cmp_v'), rope(kv('slc_k'), pos), kv('slc_v')], axis=2)
    rows = new_rows if nsa_past is None else jnp.concatenate([nsa_past.astype(dt), new_rows], axis=1)
    qg = q.reshape(B, T, NSA_KV_HEADS, NSA_REP, HEAD_DIM)
    qrg = q_rope.reshape(B, T, NSA_KV_HEADS, NSA_REP, HEAD_DIM)
    o_cmp, o_slc = nsa_global(qg, qrg, rows, pos, cmp_pos, cmp_w1, cmp_w2)
    win_rows = jnp.stack([k_win, v_win], axis=2)
    if win_past is None:
        o_win = sliding_prompt(qrg, k_win, v_win)
        new_win = win_rows[:, T - min(WINDOW, T):]
    else:
        n_buf = win_past.shape[1]
        ext = jnp.concatenate([win_past.astype(dt), win_rows], axis=1)
        kpos = pos[0] - n_buf + jnp.arange(n_buf + T)
        o_win = band_attend(qrg, ext[:, :, 0], ext[:, :, 1], pos, kpos)
        new_win = ext[:, T:]
    gates = jax.nn.sigmoid(parts['nsa_gate'].astype(jnp.float32)).reshape(B, T, NSA_KV_HEADS, NSA_REP, 3)
    o = gates[..., 0:1] * o_cmp + gates[..., 1:2] * o_slc + gates[..., 2:3] * o_win
    return o.reshape(B, T, NSA_WIDTH).astype(dt), new_rows, new_win


def pool_mixer(u, prev, pos, w_pool, scale):
    B, T, C = u.shape
    P = POOL_MAX - 1
    ext = jnp.concatenate([prev.astype(u.dtype), u], axis=1).astype(jnp.float32)
    cs = jnp.concatenate([jnp.zeros((B, 1, C), jnp.float32), jnp.cumsum(ext, axis=1)], axis=1)
    end = cs[:, P + 1:]
    means = []
    for gi, w in enumerate(POOL_WINDOWS):
        sl = slice(gi * POOL_GROUP_DIM, (gi + 1) * POOL_GROUP_DIM)
        start = cs[:, P + 1 - w:P + 1 - w + T, sl]
        cnt = jnp.minimum(pos + 1, w).astype(jnp.float32)[None, :, None]
        means.append((end[..., sl] - start) / cnt)
    pooled = (jnp.concatenate(means, axis=-1) - ext[:, P:]).reshape(B, T, POOL_GROUPS, POOL_GROUP_DIM)
    y = jnp.einsum('btgc,gcd->btgd', pooled, w_pool.astype(jnp.float32)).reshape(B, T, C) * scale
    return y.astype(u.dtype), ext[:, -P:].astype(u.dtype)


def mixer_layer(x, pos0, gla_s0, pool_prev, nsa_past, win_past,
                w_in, gla_w2, gla_b, gla_norm_g, cmp_pos, cmp_w1, cmp_w2, pool_w, pool_scale, w_out):
    B, T, _ = x.shape
    pos = pos0 + jnp.arange(T, dtype=jnp.int32)
    parts = split_proj(mm3(x, w_in))
    y_gla, s_gla = gla_mixer(parts, gla_s0, gla_w2, gla_b, gla_norm_g)
    y_nsa, nsa_rows, win_rows = nsa_mixer(parts, pos, nsa_past, win_past, cmp_pos, cmp_w1, cmp_w2)
    y_pool, pool_rows = pool_mixer(parts['pool'], pool_prev, pos, pool_w, pool_scale)
    y = mm3(jnp.concatenate([y_gla, y_nsa, y_pool], axis=-1), w_out)
    return y, nsa_rows, win_rows, s_gla, pool_rows


def swiglu(x, wg, wu, wd):
    return mm3(jax.nn.silu(mm3(x, wg)) * mm3(x, wu), wd)


def moe_ffn(x, router, wg, wu, wd):
    B, T, D = x.shape
    n_tok = B * T
    xt = x.reshape(n_tok, D)
    logits = (xt @ router).astype(jnp.float32)
    top_v, top_i = lax.top_k(logits, TOP_K)
    gates = jax.nn.softmax(top_v, axis=-1)
    n_asg = n_tok * TOP_K
    e_flat = top_i.reshape(n_asg)
    tok_flat = jnp.arange(n_asg, dtype=jnp.int32) // TOP_K
    gate_flat = gates.reshape(n_asg)
    blk = MOE_ROW_BLOCK if n_asg >= N_EXPERTS * MOE_ROW_BLOCK else MOE_MIN_BLOCK
    n_blk = -(-(n_asg + N_EXPERTS * (blk - 1)) // blk)
    order = jnp.argsort(e_flat)
    e_sorted = e_flat[order]
    counts = jnp.bincount(e_flat, length=N_EXPERTS)
    padded = (counts + blk - 1) // blk * blk
    pad_end = jnp.cumsum(padded)
    pad_start = pad_end - padded
    start = jnp.cumsum(counts) - counts
    dest = pad_start[e_sorted] + jnp.arange(n_asg) - start[e_sorted]
    row_tok = jnp.zeros((n_blk * blk,), jnp.int32).at[dest].set(tok_flat[order])
    row_gate = jnp.zeros((n_blk * blk,), jnp.float32).at[dest].set(gate_flat[order])
    blk_expert = jnp.minimum(jnp.searchsorted(pad_end, jnp.arange(n_blk) * blk, side='right'), N_EXPERTS - 1)

    def expert_block(args):
        rows, e = args
        xb = xt[rows]
        return (jax.nn.silu(xb @ wg[e]) * (xb @ wu[e])) @ wd[e]

    out = lax.map(expert_block, (row_tok.reshape(n_blk, blk), blk_expert))
    y = jnp.zeros((n_tok, D), jnp.float32).at[row_tok].add(
        out.reshape(n_blk * blk, D).astype(jnp.float32) * row_gate[:, None])
    return y.reshape(B, T, D).astype(x.dtype)


def kernel(x_prompt, x_sample, cache_nsa, page_table, state_win, state_gla, state_pool, w_in, gla_gate_w2, gla_gate_b, gla_norm_g, nsa_cmp_pos, nsa_cmp_w1, nsa_cmp_w2, pool_w, pool_scale, w_out, ln1_g, ln1_b, ln2_g, ln2_b, ffn_w_gate, ffn_w_up, ffn_w_down, moe_router, moe_w_gate, moe_w_up, moe_w_down):
    n_prompt, t_len, d = x_prompt.shape
    n_dec = x_sample.shape[0]
    xp, xs = x_prompt.reshape(n_prompt * t_len, d), x_sample.reshape(n_dec, d)
    nsa_p, nsa_s, win_p, win_s, gla_p, gla_s, pool_p, pool_s = [], [], [], [], [], [], [], []
    for l in range(DEPTH):
        i = l // 2
        router = moe_router[i] if l % 2 else None
        w_in_packed, w_out_bf16 = pack_w_in(w_in[l]), w_out[l].astype(BF16)
        lw = (w_in_packed, w_out_bf16, ln1_g[l], ln1_b[l], gla_gate_w2[l], gla_gate_b[l], gla_norm_g[l],
              nsa_cmp_pos[l], nsa_cmp_w1[l], nsa_cmp_w2[l], pool_w[l], pool_scale[l])
        xp, r_p, w_p, g_p, p_p = prompt_mixer(xp, *lw, n_prompt, t_len, router)
        xs, r_s, w_s, g_s, p_s = decode_mixer(xs, *lw, cache_nsa, l, page_table, state_win, state_gla[l],
                                              state_pool[l], PAST_LEN, router)
        if l % 2 == 0:
            wg, wu, wd = ffn_w_gate[i].astype(BF16), ffn_w_up[i].astype(BF16), ffn_w_down[i].astype(BF16)
            xp = ffn_ln(xp, wg, wu, wd, ln2_g[l], ln2_b[l])
            xs = ffn_ln(xs, wg, wu, wd, ln2_g[l], ln2_b[l])
        else:
            (xp, lg_p), (xs, lg_s) = xp, xs
            xp, xs = moe_ln([xp, xs], [lg_p, lg_s], moe_w_gate[i], moe_w_up[i], moe_w_down[i], ln2_g[l], ln2_b[l])
        nsa_p.append(r_p); nsa_s.append(r_s); win_p.append(w_p); win_s.append(w_s)
        gla_p.append(g_p); gla_s.append(g_s); pool_p.append(p_p); pool_s.append(p_s)
    return (xp.reshape(n_prompt, t_len, d), xs.reshape(x_sample.shape), jnp.stack(nsa_p), jnp.stack(nsa_s),
            jnp.stack(win_p), jnp.stack(win_s), jnp.stack(gla_p), jnp.stack(gla_s), jnp.stack(pool_p),
            jnp.stack(pool_s))
```

```python
import functools

import jax
import jax.numpy as jnp
from jax import lax
from jax.experimental import pallas as pl
from jax.experimental.pallas import tpu as pltpu

D_MODEL = 2048
DEPTH = 2
PAST_LEN = 16384
HEAD_DIM = 128
GLA_HEADS = 4
GLA_DK = 64
GLA_DV = 128
GLA_RANK = 16
GLA_GATE_NORM = 16.0
GLA_CHUNK = 64
GLA_WIDTH = GLA_HEADS * GLA_DV
NSA_HEADS = 8
NSA_KV_HEADS = 2
NSA_REP = NSA_HEADS // NSA_KV_HEADS
NSA_WIDTH = NSA_HEADS * HEAD_DIM
CMP_LEN = 32
CMP_STRIDE = 16
CMP_HIDDEN = 128
SEL_BLOCK = 64
SEL_TOPN = 16
WINDOW = 512
Q_BLOCK = 128
POOL_GROUPS = 4
POOL_GROUP_DIM = 128
POOL_WIDTH = POOL_GROUPS * POOL_GROUP_DIM
POOL_WINDOWS = (2, 4, 8, 16)
POOL_MAX = 16
ROPE_THETA = 500000.0
ROPE_DIM = HEAD_DIM // 4
N_EXPERTS = 8
TOP_K = 2
MOE_ROW_BLOCK = 128
MOE_MIN_BLOCK = 8
ALPHA = (2 * DEPTH) ** 0.25
LN_EPS = 1e-5
RMS_EPS = 1e-6

PROJ_SIZES = (
    ('gla_q', GLA_HEADS * GLA_DK), ('gla_k', GLA_HEADS * GLA_DK), ('gla_v', GLA_HEADS * GLA_DV),
    ('gla_glr', GLA_RANK), ('gla_r', GLA_HEADS * GLA_DV),
    ('nsa_q', NSA_HEADS * HEAD_DIM),
    ('cmp_k', NSA_KV_HEADS * HEAD_DIM), ('cmp_v', NSA_KV_HEADS * HEAD_DIM),
    ('slc_k', NSA_KV_HEADS * HEAD_DIM), ('slc_v', NSA_KV_HEADS * HEAD_DIM),
    ('win_k', NSA_KV_HEADS * HEAD_DIM), ('win_v', NSA_KV_HEADS * HEAD_DIM),
    ('nsa_gate', 3 * NSA_HEADS),
    ('pool', POOL_WIDTH),
)

GLA_SUB = 16
SEL_PAD = 128

BF16 = jnp.bfloat16
F32 = jnp.float32
NEG_BIG = -1e30
VMEM_LIMIT_BYTES = 56 * 1024 * 1024

COL_NSA_Q = 0
COL_ROWS = 1024
COL_WIN = 2048
COL_POOL = 2560
COL_GLA_V = 3072
COL_GLA_R = 3584
COL_GLA_Q = 4096
COL_GLA_K = 4352
COL_SMALL = 4608
PACKED_WIDTH = 4736
SMALL_GATE_OFF = GLA_RANK


def _params(*sem, vmem_limit_bytes=VMEM_LIMIT_BYTES):
    return pltpu.CompilerParams(dimension_semantics=sem, vmem_limit_bytes=vmem_limit_bytes)


def _proj_offsets():
    out, off = {}, 0
    for name, size in PROJ_SIZES:
        out[name] = (off, size)
        off += size
    return out


def pack_w_in(w):
    offs = _proj_offsets()
    sl = lambda n: w[:, offs[n][0]:offs[n][0] + offs[n][1]]
    pad = jnp.zeros((w.shape[0], 128 - GLA_RANK - 3 * NSA_HEADS), w.dtype)
    cols = [sl('nsa_q'), sl('cmp_k'), sl('cmp_v'), sl('slc_k'), sl('slc_v'), sl('win_k'), sl('win_v'),
            sl('pool'), sl('gla_v'), sl('gla_r'), sl('gla_q'), sl('gla_k'), sl('gla_glr'), sl('nsa_gate'), pad]
    return jnp.concatenate(cols, axis=1).astype(BF16)


def rope_tables(pos):
    half = ROPE_DIM // 2
    inv_freq = ROPE_THETA ** (-jnp.arange(half, dtype=F32) / half)
    ang = pos.astype(F32)[:, None] * inv_freq[None, :]
    cos, sin = jnp.cos(ang), jnp.sin(ang)
    t = pos.shape[0]
    c = jnp.concatenate([cos, cos, jnp.ones((t, HEAD_DIM - ROPE_DIM), F32)], axis=1)
    sa = jnp.concatenate([-sin, jnp.zeros((t, HEAD_DIM - half), F32)], axis=1)
    sb = jnp.concatenate([jnp.zeros((t, half), F32), sin, jnp.zeros((t, HEAD_DIM - ROPE_DIM), F32)], axis=1)
    return c, sa, sb


def _pick_tile(n, pref):
    for t in pref:
        if n % t == 0:
            return t
    return n


def _mm_kernel(x_ref, w_ref, o_ref, xb_ref):
    @pl.when(pl.program_id(1) == 0)
    def _():
        xb_ref[...] = x_ref[...].astype(BF16)

    o_ref[...] = jnp.dot(xb_ref[...], w_ref[...].astype(BF16), preferred_element_type=F32)


def matmul(x, w):
    m, k = x.shape
    n = w.shape[1]
    tm = _pick_tile(m, tuple(t for t in (1024, 512, 256, 128, 64, 32, 16, 8) if t * k <= 2048 * 1024))
    tn = 512 if n >= 512 else n
    return pl.pallas_call(
        _mm_kernel,
        grid=(m // tm, pl.cdiv(n, tn)),
        in_specs=[pl.BlockSpec((tm, k), lambda i, j: (i, 0)),
                  pl.BlockSpec((k, tn), lambda i, j: (0, j))],
        out_specs=pl.BlockSpec((tm, tn), lambda i, j: (i, j)),
        out_shape=jax.ShapeDtypeStruct((m, n), F32),
        scratch_shapes=[pltpu.VMEM((tm, k), BF16)],
        compiler_params=_params("arbitrary", "arbitrary"),
        name="matmul",
    )(x, w)


def mm3(x, w):
    lead = x.shape[:-1]
    return matmul(x.reshape(-1, x.shape[-1]), w).reshape(*lead, w.shape[1])


def _rope(x, c, sa, sb):
    return x * c + pltpu.roll(x, HEAD_DIM - ROPE_DIM // 2, 1) * sa + pltpu.roll(x, ROPE_DIM // 2, 1) * sb


def _nsa_prep_kernel(q_ref, rows_ref, win_ref, small_ref, c_ref, sa_ref, sb_ref,
                     rows_o, win_o, qn_o, qr_o, ks_o, vs_o, kw_o, vw_o, gate_o):
    c, sa, sb = c_ref[...], sa_ref[...], sb_ref[...]
    scale = HEAD_DIM ** -0.5
    hd = HEAD_DIM
    for h in range(NSA_HEADS):
        x = q_ref[:, h * hd:(h + 1) * hd]
        qn_o[0, h] = (x * scale).astype(BF16)
        qr_o[0, h] = (_rope(x, c, sa, sb) * scale).astype(BF16)
    ones = jnp.ones((q_ref.shape[0], hd), BF16)
    rows_o[:, 0:4 * hd] = rows_ref[:, 0:4 * hd]
    for g in range(NSA_KV_HEADS):
        k = _rope(rows_ref[:, (4 + g) * hd:(5 + g) * hd], c, sa, sb)
        rows_o[:, (4 + g) * hd:(5 + g) * hd] = k
        ks_o[0, g] = k.astype(BF16)
        v = rows_ref[:, (6 + g) * hd:(7 + g) * hd]
        rows_o[:, (6 + g) * hd:(7 + g) * hd] = v
        vs_o[0, g, :, 0:hd] = v.astype(BF16)
        vs_o[0, g, :, hd:2 * hd] = ones
        k = _rope(win_ref[:, g * hd:(g + 1) * hd], c, sa, sb)
        win_o[:, g * hd:(g + 1) * hd] = k
        kw_o[0, g] = k.astype(BF16)
        v = win_ref[:, (2 + g) * hd:(3 + g) * hd]
        win_o[:, (2 + g) * hd:(3 + g) * hd] = v
        vw_o[0, g, :, 0:hd] = v.astype(BF16)
        vw_o[0, g, :, hd:2 * hd] = ones
    sig = jax.nn.sigmoid(small_ref[...])
    per_g = 3 * NSA_REP
    for g in range(NSA_KV_HEADS):
        gate_o[0, g] = pltpu.roll(sig, 128 - SMALL_GATE_OFF - g * per_g, 1)


def nsa_prep(p, tables, n_batch, t_len):
    tr = _pick_tile(t_len, (512, 256, 128, 64, 32, 16))
    nt = t_len // tr
    n = n_batch * t_len
    hd = HEAD_DIM
    row = lambda w, cb: pl.BlockSpec((tr, w), lambda b, i: (b * nt + i, cb))
    tab = pl.BlockSpec((tr, hd), lambda b, i: (i, 0))
    head = lambda nh, w: pl.BlockSpec((1, nh, tr, w), lambda b, i: (b, 0, i, 0))
    out_shape = (
        jax.ShapeDtypeStruct((n, 8 * hd), F32),
        jax.ShapeDtypeStruct((n, 4 * hd), F32),
        jax.ShapeDtypeStruct((n_batch, NSA_HEADS, t_len, hd), BF16),
        jax.ShapeDtypeStruct((n_batch, NSA_HEADS, t_len, hd), BF16),
        jax.ShapeDtypeStruct((n_batch, NSA_KV_HEADS, t_len, hd), BF16),
        jax.ShapeDtypeStruct((n_batch, NSA_KV_HEADS, t_len, 2 * hd), BF16),
        jax.ShapeDtypeStruct((n_batch, NSA_KV_HEADS, t_len, hd), BF16),
        jax.ShapeDtypeStruct((n_batch, NSA_KV_HEADS, t_len, 2 * hd), BF16),
        jax.ShapeDtypeStruct((n_batch, NSA_KV_HEADS, t_len, 128), F32),
    )
    return pl.pallas_call(
        _nsa_prep_kernel,
        grid=(n_batch, nt),
        in_specs=[row(8 * hd, COL_NSA_Q // (8 * hd)), row(8 * hd, COL_ROWS // (8 * hd)),
                  row(4 * hd, COL_WIN // (4 * hd)), row(128, COL_SMALL // 128), tab, tab, tab],
        out_specs=(row(8 * hd, 0), row(4 * hd, 0), head(NSA_HEADS, hd), head(NSA_HEADS, hd),
                   head(NSA_KV_HEADS, hd), head(NSA_KV_HEADS, 2 * hd), head(NSA_KV_HEADS, hd),
                   head(NSA_KV_HEADS, 2 * hd), head(NSA_KV_HEADS, 128)),
        out_shape=out_shape,
        compiler_params=_params("arbitrary", "arbitrary"),
        name="nsa_prep",
    )(p, p, p, p, *tables)


def _nsa_cmp_kernel(x_ref, pe_ref, w1_ref, w2_ref, o_ref):
    nh = o_ref.shape[0]
    h_lo = jnp.zeros((nh, CMP_HIDDEN), F32)
    h_hi = jnp.zeros((nh, CMP_HIDDEN), F32)
    for j in range(CMP_STRIDE):
        xj = x_ref[pl.ds(j, nh, stride=CMP_STRIDE), :]
        h_lo += jnp.dot((xj + pe_ref[j:j + 1, :]).astype(BF16), w1_ref[j].astype(BF16), preferred_element_type=F32)
        h_hi += jnp.dot((xj + pe_ref[CMP_STRIDE + j:CMP_STRIDE + j + 1, :]).astype(BF16),
                        w1_ref[CMP_STRIDE + j].astype(BF16), preferred_element_type=F32)
    h = jax.nn.gelu(h_lo + pltpu.roll(h_hi, nh - 1, 0))
    o_ref[...] = jnp.dot(h.astype(BF16), w2_ref[...].astype(BF16), preferred_element_type=F32).astype(BF16)


def nsa_compress_prompt(rows, cmp_pos, cmp_w1, cmp_w2, n_batch, t_len):
    nh = t_len // CMP_STRIDE
    hd = HEAD_DIM
    rows3 = rows.reshape(n_batch, t_len, 8 * hd)
    return pl.pallas_call(
        _nsa_cmp_kernel,
        grid=(n_batch, 2, NSA_KV_HEADS),
        in_specs=[pl.BlockSpec((None, t_len, hd), lambda b, kd, g: (b, 0, kd * NSA_KV_HEADS + g)),
                  pl.BlockSpec((None, CMP_LEN, hd), lambda b, kd, g: (kd, 0, 0)),
                  pl.BlockSpec((None, CMP_LEN, hd, CMP_HIDDEN), lambda b, kd, g: (kd, 0, 0, 0)),
                  pl.BlockSpec((None, CMP_HIDDEN, hd), lambda b, kd, g: (kd, 0, 0))],
        out_specs=pl.BlockSpec((None, None, None, nh, hd), lambda b, kd, g: (b, kd, g, 0, 0)),
        out_shape=jax.ShapeDtypeStruct((n_batch, 2, NSA_KV_HEADS, nh, hd), BF16),
        compiler_params=_params("arbitrary", "arbitrary", "arbitrary"),
        name="nsa_compress",
    )(rows3, cmp_pos, cmp_w1, cmp_w2)


def _nsa_select_kernel(qn_ref, kc_ref, vc_ref, ovt_ref, oc_ref, selb_ref, *, n_cmp, n_top):
    rep, tq, hd = qn_ref.shape[1], qn_ref.shape[2], qn_ref.shape[3]
    n_cmp_pad = kc_ref.shape[0]
    n_sel = ovt_ref.shape[0]
    q0 = pl.program_id(2) * tq
    q = qn_ref[0].reshape(rep * tq, hd)
    s = lax.dot_general(q, kc_ref[...], (((1,), (1,)), ((), ())), preferred_element_type=F32)
    row = lax.broadcasted_iota(jnp.int32, (rep * tq, n_cmp_pad), 0)
    col = lax.broadcasted_iota(jnp.int32, (rep * tq, n_cmp_pad), 1)
    qpos = q0 + (row & (tq - 1))
    mask = (col * CMP_STRIDE + (CMP_LEN - 1) <= qpos) & (col < n_cmp)
    s = jnp.where(mask, s, -jnp.inf)
    m = jnp.max(s, axis=-1, keepdims=True)
    m = jnp.where(m > -jnp.inf, m, 0.0)
    p = jnp.where(mask, jnp.exp(s - m), 0.0)
    p = p / jnp.maximum(jnp.sum(p, axis=-1, keepdims=True), 1e-30)
    oc = jnp.dot(p.astype(BF16), vc_ref[...], preferred_element_type=F32)
    oc_ref[0] = oc.reshape(rep, tq, hd).astype(BF16)
    psum = p[0:tq]
    for r in range(1, rep):
        psum = psum + p[r * tq:(r + 1) * tq]
    imp = lax.dot_general(ovt_ref[...], psum, (((1,), (1,)), ((), ())), preferred_element_type=F32,
                          precision=lax.Precision.HIGHEST)
    blk = lax.broadcasted_iota(jnp.int32, (n_sel, tq), 0)
    cur = (q0 + lax.broadcasted_iota(jnp.int32, (n_sel, tq), 1)) // SEL_BLOCK
    forced = (blk == 0) | (blk == cur) | (blk == cur - 1)
    v = jnp.where(blk <= cur, jnp.where(forced, jnp.inf, imp), -jnp.inf)
    rank = jnp.zeros((n_sel, tq), jnp.int32)
    for i in range(n_sel):
        vi = v[i:i + 1, :]
        ahead = (vi > v) | ((vi == v) & (blk > i))
        rank = rank + ahead.astype(jnp.int32)
    selb_t = jnp.where((rank < n_top) & (v > -jnp.inf), 0.0, NEG_BIG)
    pad = jnp.full((SEL_PAD - n_sel, tq), NEG_BIG, F32)
    selb_ref[0, 0] = jnp.concatenate([selb_t, pad], axis=0).T.astype(BF16)


def nsa_select(qn, cmp_kv, n_cmp, t_k):
    n_batch, _, t_len, hd = qn.shape
    n_cmp_pad = cmp_kv.shape[3]
    n_sel = -(-t_k // SEL_BLOCK)
    tq = _pick_tile(t_len, (256, 128, 64, 32, 16))
    ci = jnp.arange(n_cmp_pad)[None, :]
    sj = jnp.arange(n_sel)[:, None]
    overlap_t = ((ci * CMP_STRIDE <= sj * SEL_BLOCK + SEL_BLOCK - 1) &
                 (ci * CMP_STRIDE + CMP_LEN - 1 >= sj * SEL_BLOCK) & (ci < n_cmp)).astype(F32)
    kern = functools.partial(_nsa_select_kernel, n_cmp=n_cmp, n_top=min(SEL_TOPN, n_sel))
    return pl.pallas_call(
        kern,
        grid=(n_batch, NSA_KV_HEADS, t_len // tq),
        in_specs=[pl.BlockSpec((1, NSA_REP, tq, hd), lambda b, g, i: (b, g, i, 0)),
                  pl.BlockSpec((None, None, None, n_cmp_pad, hd), lambda b, g, i: (b, 0, g, 0, 0)),
                  pl.BlockSpec((None, None, None, n_cmp_pad, hd), lambda b, g, i: (b, 1, g, 0, 0)),
                  pl.BlockSpec((n_sel, n_cmp_pad), lambda b, g, i: (0, 0))],
        out_specs=(pl.BlockSpec((1, NSA_REP, tq, hd), lambda b, g, i: (b, g, i, 0)),
                   pl.BlockSpec((1, 1, tq, SEL_PAD), lambda b, g, i: (b, g, i, 0))),
        out_shape=(jax.ShapeDtypeStruct((n_batch, NSA_HEADS, t_len, hd), BF16),
                   jax.ShapeDtypeStruct((n_batch, NSA_KV_HEADS, t_len, SEL_PAD), BF16)),
        compiler_params=_params("arbitrary", "arbitrary", "arbitrary"),
        name="nsa_select",
    )(qn, cmp_kv, cmp_kv, overlap_t)


ATTN_Q_BLOCK = 256
ATTN_K_TILE = 1024


def _nsa_attn_kernel(qr_ref, oc_ref, selb_ref, gate_ref, ks_ref, vs_ref, kw_ref, vw_ref, e_ref, cb_ref, wb_ref, o_ref,
                     m_scr, acc_scr, *, tk, wk):
    rep, qb, hd = qr_ref.shape[1], qr_ref.shape[2], qr_ref.shape[3]
    nr = rep * qb
    q0 = pl.program_id(2) * qb
    q = qr_ref[0].reshape(nr, hd)
    selb = selb_ref[0, 0]
    nt = (((1,), (1,)), ((), ()))

    def sel_scores(t, extra_bias=None):
        k = ks_ref[0, 0, pl.ds(pl.multiple_of(t * tk, tk), tk), :]
        s = lax.dot_general(q, k, nt, preferred_element_type=F32)
        bias = jnp.dot(selb, e_ref[t], preferred_element_type=F32)
        if extra_bias is not None:
            bias = bias + extra_bias
        return (s.reshape(rep, qb, tk) + bias[None]).reshape(nr, tk)

    def sel_values(t):
        return vs_ref[0, 0, pl.ds(pl.multiple_of(t * tk, tk), tk), :]

    td = q0 // tk
    s = sel_scores(td, cb_ref[0])
    m = jnp.max(s, axis=-1, keepdims=True)
    m_scr[...] = m
    acc_scr[...] = jnp.dot(jnp.exp(s - m).astype(BF16), sel_values(td), preferred_element_type=F32)

    def body(t, carry):
        s = sel_scores(t)
        m_old = m_scr[...]
        m_new = jnp.maximum(m_old, jnp.max(s, axis=-1, keepdims=True))
        p = jnp.exp(s - m_new).astype(BF16)
        acc_scr[...] = jnp.exp(m_old - m_new) * acc_scr[...] + jnp.dot(p, sel_values(t), preferred_element_type=F32)
        m_scr[...] = m_new
        return carry

    kstart = pl.multiple_of(jnp.maximum(q0 - WINDOW, 0), qb)
    kw = kw_ref[0, 0, pl.ds(kstart, wk), :]
    s = lax.dot_general(q, kw, nt, preferred_element_type=F32)
    s = (s.reshape(rep, qb, wk) + wb_ref[0][None]).reshape(nr, wk)
    m = jnp.max(s, axis=-1, keepdims=True)
    accw = jnp.dot(jnp.exp(s - m).astype(BF16), vw_ref[0, 0, pl.ds(kstart, wk), :], preferred_element_type=F32)
    o_win = accw[:, 0:hd] / jnp.maximum(accw[:, hd:hd + 1], 1e-30)

    lax.fori_loop(0, td, body, 0)
    acc = acc_scr[...]
    o_sel = acc[:, 0:hd] / jnp.maximum(acc[:, hd:hd + 1], 1e-30)

    gates = gate_ref[0, 0]
    for r in range(rep):
        rows = slice(r * qb, (r + 1) * qb)
        o = (gates[:, 3 * r:3 * r + 1] * oc_ref[0, r].astype(F32)
             + gates[:, 3 * r + 1:3 * r + 2] * o_sel[rows]
             + gates[:, 3 * r + 2:3 * r + 3] * o_win[rows])
        o_ref[:, r * hd:(r + 1) * hd] = o.astype(BF16)


def nsa_attend(qr, o_cmp, selb, gates, ks, vs, kw, vw):
    n_batch, _, t_len, hd = qr.shape
    n_sel = selb.shape[3]
    qb = min(ATTN_Q_BLOCK, t_len)
    tk = min(ATTN_K_TILE, t_len)
    wk = min(WINDOW + qb, t_len)
    nq = t_len // qb
    n_tiles = t_len // tk
    key_blk = (jnp.arange(n_tiles)[:, None, None] * tk + jnp.arange(tk)[None, None, :]) // SEL_BLOCK
    e = (key_blk == jnp.arange(n_sel)[None, :, None]).astype(BF16)
    n_cv = tk // qb
    qi = jnp.arange(qb)[None, :, None]
    causal = jnp.where(jnp.arange(tk)[None, None, :] <= jnp.arange(n_cv)[:, None, None] * qb + qi, 0.0, NEG_BIG)
    n_wv = min(WINDOW // qb, nq - 1) + 1
    q0v = jnp.arange(n_wv)[:, None, None] * qb
    kpos = jnp.maximum(q0v - WINDOW, 0) + jnp.arange(wk)[None, None, :]
    qpos = q0v + qi
    window = jnp.where((kpos <= qpos) & (kpos > qpos - WINDOW), 0.0, NEG_BIG).astype(F32)
    kern = functools.partial(_nsa_attn_kernel, tk=tk, wk=wk)
    per_q = lambda nh, w: pl.BlockSpec((1, nh, qb, w), lambda b, g, i: (b, g, i, 0))
    full = lambda w: pl.BlockSpec((1, 1, t_len, w), lambda b, g, i: (b, g, 0, 0))
    return pl.pallas_call(
        kern,
        grid=(n_batch, NSA_KV_HEADS, nq),
        in_specs=[per_q(NSA_REP, hd), per_q(NSA_REP, hd), per_q(1, n_sel), per_q(1, 128),
                  full(hd), full(2 * hd), full(hd), full(2 * hd),
                  pl.BlockSpec((n_tiles, n_sel, tk), lambda b, g, i: (0, 0, 0)),
                  pl.BlockSpec((1, qb, tk), lambda b, g, i: (i % n_cv, 0, 0)),
                  pl.BlockSpec((1, qb, wk), lambda b, g, i: (jnp.minimum(i, n_wv - 1), 0, 0))],
        out_specs=pl.BlockSpec((qb, NSA_REP * hd), lambda b, g, i: (b * nq + i, g)),
        out_shape=jax.ShapeDtypeStruct((n_batch * t_len, NSA_HEADS * hd), BF16),
        scratch_shapes=[pltpu.VMEM((NSA_REP * qb, 1), F32), pltpu.VMEM((NSA_REP * qb, 2 * hd), F32)],
        compiler_params=_params("arbitrary", "arbitrary", "arbitrary"),
        name="nsa_attend",
    )(qr, o_cmp, selb, gates, ks, vs, kw, vw, e, causal.astype(F32), window)


def _gla_kernel(q_ref, k_ref, v_ref, r_ref, small_ref, w2_ref, b2_ref, ng_ref, s0_ref, y_ref, sf_ref, s_scr):
    tb = q_ref.shape[0]
    c, sub, dk, dv = GLA_CHUNK, GLA_SUB, GLA_DK, GLA_DV
    n_sub = c // sub
    t = pl.program_id(1)

    @pl.when(t == 0)
    def _():
        s_scr[...] = s0_ref[0]

    z = jnp.dot(small_ref[:, 0:GLA_RANK].astype(BF16), w2_ref[...].astype(BF16),
                preferred_element_type=F32) + b2_ref[...]
    g_all = (jnp.minimum(z, 0.0) - jnp.log1p(jnp.exp(-jnp.abs(z)))) / GLA_GATE_NORM
    ri = lax.broadcasted_iota(jnp.int32, (c, c), 0)
    ci = lax.broadcasted_iota(jnp.int32, (c, c), 1)
    tril = ri >= ci
    cum = tril.astype(F32)
    rsub = lax.broadcasted_iota(jnp.int32, (c, dk), 0) // sub
    eye = lax.broadcasted_iota(jnp.int32, (dk, dk), 0) == lax.broadcasted_iota(jnp.int32, (dk, dk), 1)
    for cc in range(tb // c):
        rows = slice(cc * c, (cc + 1) * c)
        b_all = jnp.dot(cum, g_all[rows], preferred_element_type=F32, precision=lax.Precision.HIGHEST)
        for h in range(GLA_HEADS):
            b = b_all[:, h * dk:(h + 1) * dk]
            qh = q_ref[rows, h * dk:(h + 1) * dk] * (dk ** -0.5)
            kh = k_ref[rows, h * dk:(h + 1) * dk]
            vh = v_ref[rows, h * dv:(h + 1) * dv]
            a_rows = []
            for i in range(n_sub):
                ref = b[sub * i - 1:sub * i, :] if i else jnp.zeros((1, dk), F32)
                rs = slice(sub * i, sub * (i + 1))
                qi = (qh[rs] * jnp.exp(b[rs] - ref)).astype(BF16)
                ki = jnp.where(rsub <= i, kh * jnp.exp(ref - b), 0.0).astype(BF16)
                a_rows.append(lax.dot_general(qi, ki, (((1,), (1,)), ((), ())), preferred_element_type=F32))
            a = jnp.where(tril, jnp.concatenate(a_rows, axis=0), 0.0)
            s_old = s_scr[h]
            o = jnp.dot(a.astype(BF16), vh.astype(BF16), preferred_element_type=F32)
            o += jnp.dot((qh * jnp.exp(b)).astype(BF16), s_old.astype(BF16), preferred_element_type=F32)
            b_last = b[c - 1:c, :]
            ke = (kh * jnp.exp(b_last - b)).astype(BF16)
            upd = lax.dot_general(ke, vh.astype(BF16), (((0,), (0,)), ((), ())), preferred_element_type=F32)
            decay = jnp.exp(jnp.sum(jnp.where(eye, jnp.broadcast_to(b_last, (dk, dk)), 0.0), axis=1, keepdims=True))
            s_scr[h] = decay * s_old + upd
            o = o * lax.rsqrt(jnp.mean(o * o, axis=-1, keepdims=True) + RMS_EPS)
            y = o * ng_ref[:, h * dv:(h + 1) * dv] * jax.nn.silu(r_ref[rows, h * dv:(h + 1) * dv])
            y_ref[rows, h * dv:(h + 1) * dv] = y.astype(BF16)

    @pl.when(t == pl.num_programs(1) - 1)
    def _():
        sf_ref[0] = s_scr[...]


def gla_mix(p, s0, w2, b2, norm_g, n_batch, t_len):
    tb = _pick_tile(t_len, (256, 128, 64))
    nt = t_len // tb
    row = lambda w, off: pl.BlockSpec((tb, w), lambda b, i: (b * nt + i, off // w))
    const = lambda shape: pl.BlockSpec(shape, lambda b, i: (0,) * len(shape))
    return pl.pallas_call(
        _gla_kernel,
        grid=(n_batch, nt),
        in_specs=[row(256, COL_GLA_Q), row(256, COL_GLA_K), row(512, COL_GLA_V), row(512, COL_GLA_R),
                  row(128, COL_SMALL), const((GLA_RANK, GLA_HEADS * GLA_DK)), const((1, GLA_HEADS * GLA_DK)),
                  const((1, GLA_WIDTH)),
                  pl.BlockSpec((1, GLA_HEADS, GLA_DK, GLA_DV), lambda b, i: (b, 0, 0, 0))],
        out_specs=(pl.BlockSpec((tb, GLA_WIDTH), lambda b, i: (b * nt + i, 0)),
                   pl.BlockSpec((1, GLA_HEADS, GLA_DK, GLA_DV), lambda b, i: (b, 0, 0, 0))),
        out_shape=(jax.ShapeDtypeStruct((n_batch * t_len, GLA_WIDTH), BF16),
                   jax.ShapeDtypeStruct((n_batch, GLA_HEADS, GLA_DK, GLA_DV), F32)),
        scratch_shapes=[pltpu.VMEM((GLA_HEADS, GLA_DK, GLA_DV), F32)],
        compiler_params=_params("arbitrary", "arbitrary"),
        name="gla_mix",
    )(p, p, p, p, p, w2, b2.reshape(1, -1), norm_g.reshape(1, -1), s0)


def _pool_kernel(u_ref, prev_ref, cnt_ref, w_ref, sc_ref, y_ref, halo):
    tb = u_ref.shape[0]
    gd = POOL_GROUP_DIM

    @pl.when(pl.program_id(1) == 0)
    def _():
        halo[...] = prev_ref[0]

    ext = jnp.concatenate([halo[...], u_ref[...]], axis=0)
    halo[...] = ext[tb:tb + POOL_MAX]
    for gi, w in enumerate(POOL_WINDOWS):
        x = ext[:, gi * gd:(gi + 1) * gd]
        s = x
        shift = 1
        while shift < w:
            s = s + pltpu.roll(s, shift, 0)
            shift *= 2
        pooled = s[POOL_MAX:] / cnt_ref[:, gi:gi + 1] - x[POOL_MAX:]
        y = jnp.dot(pooled.astype(BF16), w_ref[gi].astype(BF16), preferred_element_type=F32)
        y_ref[:, gi * gd:(gi + 1) * gd] = (y * sc_ref[:, gi * gd:(gi + 1) * gd]).astype(BF16)


def pool_mix(p, prev, pos0, w_pool, scale, n_batch, t_len):
    tb = _pick_tile(t_len, (512, 256, 128, 64, 32, 16))
    nt = t_len // tb
    pos = pos0 + jnp.arange(t_len, dtype=jnp.int32)
    cnt = jnp.stack([jnp.minimum(pos + 1, w).astype(F32) for w in POOL_WINDOWS], axis=1)
    cnt = jnp.pad(cnt, ((0, 0), (0, 128 - POOL_GROUPS)), constant_values=1.0)
    prev16 = jnp.pad(prev.astype(F32), ((0, 0), (1, 0), (0, 0)))
    return pl.pallas_call(
        _pool_kernel,
        grid=(n_batch, nt),
        in_specs=[pl.BlockSpec((tb, POOL_WIDTH), lambda b, i: (b * nt + i, COL_POOL // POOL_WIDTH)),
                  pl.BlockSpec((1, POOL_MAX, POOL_WIDTH), lambda b, i: (b, 0, 0)),
                  pl.BlockSpec((tb, 128), lambda b, i: (i, 0)),
                  pl.BlockSpec((POOL_GROUPS, POOL_GROUP_DIM, POOL_GROUP_DIM), lambda b, i: (0, 0, 0)),
                  pl.BlockSpec((1, POOL_WIDTH), lambda b, i: (0, 0))],
        out_specs=pl.BlockSpec((tb, POOL_WIDTH), lambda b, i: (b * nt + i, 0)),
        out_shape=jax.ShapeDtypeStruct((n_batch * t_len, POOL_WIDTH), BF16),
        scratch_shapes=[pltpu.VMEM((POOL_MAX, POOL_WIDTH), F32)],
        compiler_params=_params("arbitrary", "arbitrary"),
        name="pool_mix",
    )(p, prev16, cnt, w_pool, scale.reshape(1, -1))


def _layer_norm_rows(x, g, b):
    xc = x - jnp.mean(x, axis=-1, keepdims=True)
    var = jnp.mean(xc * xc, axis=-1, keepdims=True)
    return xc * lax.rsqrt(var + LN_EPS) * g + b


def _outproj_kernel(x_ref, yg_ref, yn_ref, yp_ref, w_ref, g_ref, b_ref, *rest):
    h = jnp.dot(yg_ref[...], w_ref[0:GLA_WIDTH, :], preferred_element_type=F32)
    h += jnp.dot(yn_ref[...], w_ref[GLA_WIDTH:GLA_WIDTH + NSA_WIDTH, :], preferred_element_type=F32)
    h += jnp.dot(yp_ref[...], w_ref[GLA_WIDTH + NSA_WIDTH:, :], preferred_element_type=F32)
    x1 = _layer_norm_rows(ALPHA * x_ref[...] + h, g_ref[...], b_ref[...])
    if len(rest) == 1:
        rest[0][...] = x1
    else:
        rh_ref, rl_ref, o_ref, lg_ref = rest
        o_ref[...] = x1
        xh = x1.astype(BF16)
        xl = (x1 - xh.astype(F32)).astype(BF16)
        lg_ref[...] = (jnp.dot(xh, rh_ref[...], preferred_element_type=F32)
                       + jnp.dot(xl, rh_ref[...], preferred_element_type=F32)
                       + jnp.dot(xh, rl_ref[...], preferred_element_type=F32))


def router_split(router):
    r = jnp.pad(router, ((0, 0), (0, 128 - router.shape[1])))
    hi = r.astype(BF16)
    return hi, (r - hi.astype(F32)).astype(BF16)


def outproj_ln(x, y_gla, y_nsa, y_pool, w_out_bf16, g, b, router=None):
    n, d = x.shape
    tm = _pick_tile(n, (512, 256, 128, 64, 32, 16, 8))
    row = lambda w: pl.BlockSpec((tm, w), lambda i: (i, 0))
    const = lambda r, c: pl.BlockSpec((r, c), lambda i: (0, 0))
    in_specs = [row(d), row(GLA_WIDTH), row(NSA_WIDTH), row(POOL_WIDTH), const(d, d), const(1, d), const(1, d)]
    args = [x, y_gla, y_nsa, y_pool, w_out_bf16, g.reshape(1, -1), b.reshape(1, -1)]
    out_specs, out_shape = row(d), jax.ShapeDtypeStruct((n, d), F32)
    if router is not None:
        in_specs += [const(d, 128), const(d, 128)]
        args += list(router_split(router))
        out_specs, out_shape = (out_specs, row(128)), (out_shape, jax.ShapeDtypeStruct((n, 128), F32))
    return pl.pallas_call(
        _outproj_kernel,
        grid=(n // tm,),
        in_specs=in_specs,
        out_specs=out_specs,
        out_shape=out_shape,
        compiler_params=_params("arbitrary"),
        name="outproj_ln",
    )(*args)


MOE_TM = 512
MOE_TM_DOWN = 512
MOE_VMEM_BYTES = 60 * 1024 * 1024
MOE_TF = 1024
MOE_TN = 512
ROUTE_TM = 512
PERMUTE_CHUNK = 1024


def _route_kernel(lg_ref, ii_ref, gf_ref, cnt_ref, carry, *, n_valid):
    tm = lg_ref.shape[0]
    i = pl.program_id(0)

    @pl.when(i == 0)
    def _():
        carry[...] = jnp.zeros_like(carry)

    lane = lax.broadcasted_iota(jnp.int32, (tm, 128), 1)
    valid = (i * tm + lax.broadcasted_iota(jnp.int32, (tm, 128), 0)) < n_valid
    lg = jnp.where(lane < N_EXPERTS, lg_ref[...], -jnp.inf)
    m1 = jnp.max(lg, axis=-1, keepdims=True)
    i1 = jnp.min(jnp.where(lg == m1, lane, 128), axis=-1, keepdims=True)
    lg2 = jnp.where(lane == i1, -jnp.inf, lg)
    m2 = jnp.max(lg2, axis=-1, keepdims=True)
    i2 = jnp.min(jnp.where(lg2 == m2, lane, 128), axis=-1, keepdims=True)
    t = jnp.exp(m2 - m1)
    g1 = 1.0 / (1.0 + t)
    g2 = t / (1.0 + t)
    oh1 = jnp.where((lane == i1) & valid, 1.0, 0.0)
    oh2 = jnp.where((lane == i2) & valid, 1.0, 0.0)
    cnt = oh1 + oh2
    strict = (lax.broadcasted_iota(jnp.int32, (tm, tm), 0) > lax.broadcasted_iota(jnp.int32, (tm, tm), 1))
    before = jnp.dot(strict.astype(BF16), cnt.astype(BF16), preferred_element_type=F32) + carry[...]
    r1 = jnp.sum(before * oh1, axis=-1, keepdims=True).astype(jnp.int32)
    r2 = jnp.sum(before * oh2, axis=-1, keepdims=True).astype(jnp.int32)
    carry[...] += jnp.sum(cnt, axis=0, keepdims=True)
    ii_ref[...] = jnp.where(lane == 0, i1, jnp.where(lane == 1, i2, jnp.where(lane == 2, r1, r2)))
    gf_ref[...] = jnp.where(lane == 0, g1, g2)
    cnt_ref[...] = carry[...]


def moe_route(logits, n_valid):
    npad = logits.shape[0]
    tm = ROUTE_TM
    row = pl.BlockSpec((tm, 128), lambda i: (i, 0))
    info, gates, counts = pl.pallas_call(
        functools.partial(_route_kernel, n_valid=n_valid),
        grid=(npad // tm,),
        in_specs=[row],
        out_specs=(row, row, pl.BlockSpec((1, 128), lambda i: (0, 0))),
        out_shape=(jax.ShapeDtypeStruct((npad, 128), jnp.int32), jax.ShapeDtypeStruct((npad, 128), F32),
                   jax.ShapeDtypeStruct((1, 128), F32)),
        scratch_shapes=[pltpu.VMEM((1, 128), F32)],
        compiler_params=_params("arbitrary"),
        name="moe_route",
    )(logits)
    return info[:, 0:2], info[:, 2:4], gates, counts[0, :N_EXPERTS].astype(jnp.int32)


SLAB = (16, 128)


def _slabify_kernel(x_ref, *rest):
    o_ref = rest[-1]

    def slab_rows(src_ref, n_rows):
        for c in range(SLAB[0]):
            o_ref[0:n_rows, c, :] = src_ref[:, c * SLAB[1]:(c + 1) * SLAB[1]]

    if len(rest) == 1:
        slab_rows(x_ref, x_ref.shape[0])
        return
    t_ref = rest[0]
    last = pl.program_id(0) == pl.num_programs(0) - 1

    @pl.when(jnp.logical_not(last))
    def _():
        slab_rows(x_ref, x_ref.shape[0])

    @pl.when(last)
    def _():
        o_ref[...] = jnp.zeros_like(o_ref)
        slab_rows(t_ref, t_ref.shape[0])


def slabify(x, tail=None):
    n, d = x.shape
    tm = _pick_tile(n, (512, 256, 128, 64, 32, 16, 8))
    nt = n // tm
    in_specs = [pl.BlockSpec((tm, d), lambda i: (jnp.minimum(i, nt - 1), 0))]
    args = [x]
    if tail is not None:
        assert tail.shape[0] <= tm
        in_specs.append(pl.BlockSpec(tail.shape, lambda i: (0, 0)))
        args.append(tail)
    steps = nt + (tail is not None)
    return pl.pallas_call(
        _slabify_kernel,
        grid=(steps,),
        in_specs=in_specs,
        out_specs=pl.BlockSpec((tm,) + SLAB, lambda i: (i, 0, 0)),
        out_shape=jax.ShapeDtypeStruct((steps * tm,) + SLAB, x.dtype),
        compiler_params=_params("arbitrary"),
        name="slabify",
    )(*args)


def _unslab(ref):
    return jnp.concatenate([ref[:, c, :] for c in range(SLAB[0])], axis=1)


GATHER_UNROLL = 8


def _gather_slabs_kernel(idx_ref, src_ref, o_ref, sem, *stage):
    ch = idx_ref.shape[2]

    def row_copy(r):
        if stage:
            dst = stage[0].at[r // 8, :, r % 8, :]
        else:
            dst = o_ref.at[r]
        return pltpu.make_async_copy(src_ref.at[idx_ref[0, 0, r]], dst, sem)

    def issue(t, c):
        for u in range(GATHER_UNROLL):
            row_copy(t * GATHER_UNROLL + u).start(priority=u % 2)
        return c

    lax.fori_loop(0, ch // GATHER_UNROLL, issue, 0)

    def drain(t, c):
        for u in range(GATHER_UNROLL):
            row_copy(t * GATHER_UNROLL + u).wait()
        return c

    lax.fori_loop(0, ch // GATHER_UNROLL, drain, 0)
    if stage:
        rows = jnp.concatenate([stage[0][:, c].reshape(ch, SLAB[1]) for c in range(SLAB[0])], axis=1)
        o_ref[...] = rows.astype(o_ref.dtype)


def gather_slabs(src, idx, as_bf16_rows=False):
    n = idx.shape[0]
    ch = PERMUTE_CHUNK
    scratch = [pltpu.SemaphoreType.DMA(())]
    if as_bf16_rows:
        d = SLAB[0] * SLAB[1]
        out_specs, out_shape = pl.BlockSpec((ch, d), lambda i: (i, 0)), jax.ShapeDtypeStruct((n, d), BF16)
        scratch.append(pltpu.VMEM((ch // 8, SLAB[0], 8, SLAB[1]), src.dtype))
    else:
        out_specs = pl.BlockSpec((ch,) + SLAB, lambda i: (i, 0, 0))
        out_shape = jax.ShapeDtypeStruct((n,) + SLAB, src.dtype)
    return pl.pallas_call(
        _gather_slabs_kernel,
        grid=(n // ch,),
        in_specs=[pl.BlockSpec((1, 1, ch), lambda i: (i, 0, 0), memory_space=pltpu.SMEM),
                  pl.BlockSpec(memory_space=pl.ANY)],
        out_specs=out_specs,
        out_shape=out_shape,
        scratch_shapes=scratch,
        compiler_params=_params("arbitrary"),
        name="gather_slabs",
    )(idx.reshape(n // ch, 1, ch), src)


def _moe_up_kernel(te_ref, tfirst_ref, tused_ref, x_ref, wg_ref, wu_ref, h_ref, wgb, wub):
    i = pl.program_id(1)

    @pl.when((i == 0) | (tfirst_ref[i] == 1))
    def _():
        wgb[...] = wg_ref[...].astype(BF16)
        wub[...] = wu_ref[...].astype(BF16)

    @pl.when(tused_ref[i] == 1)
    def _():
        xb = x_ref[...]
        gate = jnp.dot(xb, wgb[...], preferred_element_type=F32)
        up = jnp.dot(xb, wub[...], preferred_element_type=F32)
        h_ref[...] = (jax.nn.silu(gate) * up).astype(BF16)

    @pl.when(tused_ref[i] == 0)
    def _():
        h_ref[...] = jnp.zeros_like(h_ref)


def _moe_down_kernel(te_ref, tfirst_ref, tused_ref, h_ref, wd_ref, y_ref, wdb):
    i = pl.program_id(1)

    @pl.when((i == 0) | (tfirst_ref[i] == 1))
    def _():
        wdb[...] = wd_ref[...].astype(BF16)

    @pl.when(tused_ref[i] == 1)
    def _():
        y_ref[...] = jnp.dot(h_ref[...], wdb[...], preferred_element_type=F32)

    @pl.when(tused_ref[i] == 0)
    def _():
        y_ref[...] = jnp.zeros_like(y_ref)


def _tile_meta(tile_e, tile_used, split):
    te = jnp.repeat(tile_e, split)
    first = jnp.concatenate([jnp.ones((1,), jnp.int32), (te[1:] != te[:-1]).astype(jnp.int32)])
    return te, first, jnp.repeat(tile_used, split)


def moe_experts(xs, tile_e, tile_used, wg, wu, wd):
    r, d = xs.shape
    d_ff = wg.shape[2]
    tm, tf, tn = MOE_TM, MOE_TF, MOE_TN
    h = pl.pallas_call(
        _moe_up_kernel,
        grid_spec=pltpu.PrefetchScalarGridSpec(
            num_scalar_prefetch=3,
            grid=(d_ff // tf, r // tm),
            in_specs=[pl.BlockSpec((tm, d), lambda j, i, te, t1, tu: (i, 0)),
                      pl.BlockSpec((None, d, tf), lambda j, i, te, t1, tu: (te[i], 0, j)),
                      pl.BlockSpec((None, d, tf), lambda j, i, te, t1, tu: (te[i], 0, j))],
            out_specs=pl.BlockSpec((tm, tf), lambda j, i, te, t1, tu: (i, j)),
            scratch_shapes=[pltpu.VMEM((d, tf), BF16), pltpu.VMEM((d, tf), BF16)]),
        out_shape=jax.ShapeDtypeStruct((r, d_ff), BF16),
        compiler_params=_params("arbitrary", "arbitrary", vmem_limit_bytes=MOE_VMEM_BYTES),
        name="moe_up",
    )(*_tile_meta(tile_e, tile_used, 1), xs, wg, wu)
    tmd = MOE_TM_DOWN
    return pl.pallas_call(
        _moe_down_kernel,
        grid_spec=pltpu.PrefetchScalarGridSpec(
            num_scalar_prefetch=3,
            grid=(d // tn, r // tmd),
            in_specs=[pl.BlockSpec((tmd, d_ff), lambda j, i, te, t1, tu: (i, 0)),
                      pl.BlockSpec((None, d_ff, tn), lambda j, i, te, t1, tu: (te[i], 0, j))],
            out_specs=pl.BlockSpec((tmd, tn), lambda j, i, te, t1, tu: (i, j)),
            scratch_shapes=[pltpu.VMEM((d_ff, tn), BF16)]),
        out_shape=jax.ShapeDtypeStruct((r, d), F32),
        compiler_params=_params("arbitrary", "arbitrary", vmem_limit_bytes=MOE_VMEM_BYTES),
        name="moe_down",
    )(*_tile_meta(tile_e, tile_used, tm // tmd), h, wd)


def _moe_combine_kernel(x_ref, y0_ref, y1_ref, gt_ref, g_ref, b_ref, o_ref):
    gt = gt_ref[...]
    y = gt[:, 0:1] * _unslab(y0_ref) + gt[:, 1:2] * _unslab(y1_ref)
    o_ref[...] = _layer_norm_rows(ALPHA * x_ref[...] + y, g_ref[...], b_ref[...])


def moe_combine_ln(x, yg, gates, row0, n_tok_pad, g, b):
    n, d = x.shape
    tm = _pick_tile(n, (512, 256, 128, 64, 32, 16, 8))
    o0, o1 = row0 // tm, (n_tok_pad + row0) // tm
    return pl.pallas_call(
        _moe_combine_kernel,
        grid=(n // tm,),
        in_specs=[pl.BlockSpec((tm, d), lambda i: (i, 0)),
                  pl.BlockSpec((tm,) + SLAB, lambda i: (o0 + i, 0, 0)),
                  pl.BlockSpec((tm,) + SLAB, lambda i: (o1 + i, 0, 0)),
                  pl.BlockSpec((tm, 128), lambda i: (o0 + i, 0)),
                  pl.BlockSpec((1, d), lambda i: (0, 0)), pl.BlockSpec((1, d), lambda i: (0, 0))],
        out_specs=pl.BlockSpec((tm, d), lambda i: (i, 0)),
        out_shape=jax.ShapeDtypeStruct((n, d), F32),
        compiler_params=_params("arbitrary"),
        name="moe_combine_ln",
    )(x, yg, yg, gates, g.reshape(1, -1), b.reshape(1, -1))


def moe_ln(x_groups, logit_groups, wg, wu, wd, g, b):
    d = x_groups[0].shape[1]
    n_tok = sum(x.shape[0] for x in x_groups)
    n_tok_pad = -(-n_tok // PERMUTE_CHUNK) * PERMUTE_CHUNK
    n_tok_pad = -(-n_tok_pad // ROUTE_TM) * ROUTE_TM
    logits = jnp.concatenate(logit_groups + [jnp.zeros((n_tok_pad - n_tok, 128), F32)], axis=0)
    experts, ranks, gates, counts = moe_route(logits, n_tok)
    tm = MOE_TM
    n_tiles = -(-(n_tok * TOP_K + N_EXPERTS * (tm - 1)) // tm)
    n_tiles = -(-n_tiles * tm // PERMUTE_CHUNK) * PERMUTE_CHUNK // tm
    padded = (counts + tm - 1) // tm * tm
    pad_end = jnp.cumsum(padded)
    pad_start = pad_end - padded
    valid = (jnp.arange(n_tok_pad) < n_tok)[:, None]
    dest = jnp.where(valid, pad_start[experts] + ranks, 0)
    tok = jnp.broadcast_to(jnp.arange(n_tok_pad, dtype=jnp.int32)[:, None], dest.shape)
    row_tok = jnp.zeros((n_tiles * tm,), jnp.int32).at[jnp.where(valid, dest, n_tiles * tm).reshape(-1)].set(
        tok.reshape(-1), mode='drop')
    tile_start = jnp.arange(n_tiles, dtype=jnp.int32) * tm
    tile_e = jnp.minimum(jnp.searchsorted(pad_end, tile_start, side='right'), N_EXPERTS - 1).astype(jnp.int32)
    tile_used = (tile_start < pad_end[-1]).astype(jnp.int32)
    assert len(x_groups) == 2
    xs = gather_slabs(slabify(x_groups[0], x_groups[1]), row_tok, as_bf16_rows=True)
    ys = slabify(moe_experts(xs, tile_e, tile_used, wg, wu, wd))
    yg = gather_slabs(ys, jnp.concatenate([dest[:, 0], dest[:, 1]]).astype(jnp.int32))
    outs, row0 = [], 0
    for x in x_groups:
        outs.append(moe_combine_ln(x, yg, gates, row0, n_tok_pad, g, b))
        row0 += x.shape[0]
    return outs


def _ffn_kernel(x_ref, wg_ref, wu_ref, wd_ref, g_ref, b_ref, o_ref, xb_ref, *, d_ff):
    j = pl.program_id(1)
    tf = wg_ref.shape[1]

    @pl.when(j == 0)
    def _():
        xb_ref[...] = x_ref[...].astype(BF16)
        o_ref[...] = jnp.zeros_like(o_ref)

    xb = xb_ref[...]
    gate = jnp.dot(xb, wg_ref[...], preferred_element_type=F32)
    up = jnp.dot(xb, wu_ref[...], preferred_element_type=F32)
    col = j * tf + lax.broadcasted_iota(jnp.int32, (1, tf), 1)
    a = jnp.where(col < d_ff, jax.nn.silu(gate) * up, 0.0).astype(BF16)
    rowi = j * tf + lax.broadcasted_iota(jnp.int32, (tf, 1), 0)
    wd = jnp.where(rowi < d_ff, wd_ref[...], jnp.zeros((), BF16))
    o_ref[...] += jnp.dot(a, wd, preferred_element_type=F32)

    @pl.when(j == pl.num_programs(1) - 1)
    def _():
        o_ref[...] = _layer_norm_rows(ALPHA * x_ref[...] + o_ref[...], g_ref[...], b_ref[...])


def ffn_ln(x, wg, wu, wd, g, b):
    n, d = x.shape
    d_ff = wg.shape[1]
    tm = _pick_tile(n, (512, 256, 128, 64, 32, 16, 8))
    tf = 512
    kern = functools.partial(_ffn_kernel, d_ff=d_ff)
    return pl.pallas_call(
        kern,
        grid=(n // tm, pl.cdiv(d_ff, tf)),
        in_specs=[pl.BlockSpec((tm, d), lambda i, j: (i, 0)),
                  pl.BlockSpec((d, tf), lambda i, j: (0, j)), pl.BlockSpec((d, tf), lambda i, j: (0, j)),
                  pl.BlockSpec((tf, d), lambda i, j: (j, 0)),
                  pl.BlockSpec((1, d), lambda i, j: (0, 0)), pl.BlockSpec((1, d), lambda i, j: (0, 0))],
        out_specs=pl.BlockSpec((tm, d), lambda i, j: (i, 0)),
        out_shape=jax.ShapeDtypeStruct((n, d), F32),
        scratch_shapes=[pltpu.VMEM((tm, d), BF16)],
        compiler_params=_params("arbitrary", "arbitrary"),
        name="ffn_ln",
    )(x, wg, wu, wd, g.reshape(1, -1), b.reshape(1, -1))


CMP_PAGES = 32
SEL_PAD_DEC = 384


def _cmp_paged_kernel(pt_ref, cache_ref, pe_ref, w1_ref, w2_ref, o_ref, buf, hlo, hhi, sems, *, layer, n_pages, page):
    b, ch = pl.program_id(0), pl.program_id(1)
    n_ch = pl.num_programs(1)
    hd = HEAD_DIM
    nh = CMP_PAGES * page // CMP_STRIDE
    step = b * n_ch + ch

    def copies(s, slot):
        first = s * CMP_PAGES
        return [pltpu.make_async_copy(
            cache_ref.at[layer, pt_ref[first + pg], :, c // NSA_KV_HEADS, c % NSA_KV_HEADS, :],
            buf.at[slot, c, pl.ds(pg * page, page), :], sems.at[slot])
            for pg in range(CMP_PAGES) for c in range(2 * NSA_KV_HEADS)]

    def for_slot(s, fn):
        for slot in range(2):
            @pl.when(s % 2 == slot)
            def _():
                fn(slot)

    @pl.when(step == 0)
    def _():
        for cp in copies(0, 0):
            cp.start()

    @pl.when(step + 1 < pl.num_programs(0) * n_ch)
    def _():
        for_slot(step + 1, lambda slot: [cp.start() for cp in copies(step + 1, slot)])

    for_slot(step, lambda slot: [cp.wait() for cp in copies(step, slot)])
    cur = step % 2

    for c in range(2 * NSA_KV_HEADS):
        kd, g = c // NSA_KV_HEADS, c % NSA_KV_HEADS
        xs = [buf[cur, c, pl.ds(j, nh, stride=CMP_STRIDE), :] for j in range(CMP_STRIDE)]
        lo = jnp.concatenate([(xs[j] + pe_ref[kd, j:j + 1, :]).astype(BF16) for j in range(CMP_STRIDE)], axis=1)
        hi = jnp.concatenate([(xs[j] + pe_ref[kd, CMP_STRIDE + j:CMP_STRIDE + j + 1, :]).astype(BF16)
                              for j in range(CMP_STRIDE)], axis=1)
        w_lo = w1_ref[kd, 0:CMP_STRIDE].reshape(CMP_STRIDE * hd, CMP_HIDDEN).astype(BF16)
        w_hi = w1_ref[kd, CMP_STRIDE:CMP_LEN].reshape(CMP_STRIDE * hd, CMP_HIDDEN).astype(BF16)
        rows = pl.ds(pl.multiple_of(ch * nh, nh), nh)
        hlo[c, rows, :] = jnp.dot(lo, w_lo, preferred_element_type=F32)
        hhi[c, rows, :] = jnp.dot(hi, w_hi, preferred_element_type=F32)

    @pl.when(ch == pl.num_programs(1) - 1)
    def _():
        n_all = hlo.shape[1]
        for c in range(2 * NSA_KV_HEADS):
            kd, g = c // NSA_KV_HEADS, c % NSA_KV_HEADS
            h = jax.nn.gelu(hlo[c] + pltpu.roll(hhi[c], n_all - 1, 0))
            o_ref[kd, g] = jnp.dot(h.astype(BF16), w2_ref[kd].astype(BF16), preferred_element_type=F32).astype(BF16)


def nsa_compress_paged(cache, layer, page_table, cmp_pos, cmp_w1, cmp_w2):
    page = cache.shape[2]
    n_batch, n_pages = page_table.shape
    hd = HEAD_DIM
    n_all = n_pages * page // CMP_STRIDE
    kern = functools.partial(_cmp_paged_kernel, layer=layer, n_pages=n_pages, page=page)
    const = lambda shape: pl.BlockSpec(shape, lambda b, c, pt: (0,) * len(shape))
    return pl.pallas_call(
        kern,
        grid_spec=pltpu.PrefetchScalarGridSpec(
            num_scalar_prefetch=1,
            grid=(n_batch, n_pages // CMP_PAGES),
            in_specs=[pl.BlockSpec(memory_space=pl.ANY), const((2, CMP_LEN, hd)),
                      const((2, CMP_LEN, hd, CMP_HIDDEN)), const((2, CMP_HIDDEN, hd))],
            out_specs=pl.BlockSpec((None, 2, NSA_KV_HEADS, n_all, hd), lambda b, c, pt: (b, 0, 0, 0, 0)),
            scratch_shapes=[pltpu.VMEM((2, 2 * NSA_KV_HEADS, CMP_PAGES * page, hd), F32),
                            pltpu.VMEM((2 * NSA_KV_HEADS, n_all, CMP_HIDDEN), F32),
                            pltpu.VMEM((2 * NSA_KV_HEADS, n_all, CMP_HIDDEN), F32),
                            pltpu.SemaphoreType.DMA((2,))]),
        out_shape=jax.ShapeDtypeStruct((n_batch, 2, NSA_KV_HEADS, n_all, hd), BF16),
        compiler_params=_params("arbitrary", "arbitrary"),
        name="nsa_compress_paged",
    )(page_table.reshape(-1).astype(jnp.int32), cache, cmp_pos, cmp_w1, cmp_w2)


def _sel_decode_kernel(qn_ref, kc_ref, vc_ref, ovt_ref, oc_ref, idx_ref, v_scr, psum_scr, *, n_cmp, n_sel, n_top,
                       q_pos):
    b = pl.program_id(0)
    n_cmp_pad = kc_ref.shape[2]
    nsp = ovt_ref.shape[0]
    n_rows = qn_ref.shape[0]
    col = lax.broadcasted_iota(jnp.int32, (n_rows, n_cmp_pad), 1)
    mask = (col * CMP_STRIDE + (CMP_LEN - 1) <= q_pos) & (col < n_cmp)
    psums = []
    for g in range(NSA_KV_HEADS):
        own = slice(g * NSA_REP, (g + 1) * NSA_REP)
        s = lax.dot_general(qn_ref[...], kc_ref[0, g], (((1,), (1,)), ((), ())), preferred_element_type=F32)
        s = jnp.where(mask, s, -jnp.inf)
        m = jnp.max(s, axis=-1, keepdims=True)
        m = jnp.where(m > -jnp.inf, m, 0.0)
        p = jnp.where(mask, jnp.exp(s - m), 0.0)
        p = p / jnp.maximum(jnp.sum(p, axis=-1, keepdims=True), 1e-30)
        oc = jnp.dot(p.astype(BF16), vc_ref[0, g], preferred_element_type=F32)
        oc_ref[own, :] = oc[own]
        psum_scr[pl.ds(b * NSA_KV_HEADS + g, 1), :] = jnp.sum(p[own], axis=0, keepdims=True)

    @pl.when(b == pl.num_programs(0) - 1)
    def _():
        n_col = psum_scr.shape[0]
        imp = lax.dot_general(ovt_ref[...], psum_scr[...], (((1,), (1,)), ((), ())), preferred_element_type=F32,
                              precision=lax.Precision.HIGHEST)
        blk = lax.broadcasted_iota(jnp.int32, (nsp, n_col), 0)
        cur = q_pos // SEL_BLOCK
        forced = (blk == 0) | (blk == cur) | (blk == cur - 1)
        v = jnp.where((blk <= cur) & (blk < n_sel), jnp.where(forced, jnp.inf, imp), -jnp.inf)
        v_scr[...] = v

        def count(i, rank):
            vi = v_scr[pl.ds(i, 1), :]
            ahead = (vi > v) | ((vi == v) & (blk > i))
            return rank + ahead.astype(jnp.int32)

        rank = lax.fori_loop(0, n_sel, count, jnp.zeros((nsp, n_col), jnp.int32))
        chosen = (rank < n_top) & (v > -jnp.inf)
        blk_f = blk.astype(F32)
        rows = [jnp.sum(jnp.where(chosen & (rank == t), blk_f, 0.0), axis=0, keepdims=True) for t in range(n_top)]
        idx_ref[...] = jnp.concatenate(rows, axis=0).astype(jnp.int32)


def nsa_select_decode(qn, cmp_kv, n_cmp, n_sel, q_pos):
    n_batch, n_heads, hd = qn.shape
    n_cmp_pad = cmp_kv.shape[3]
    nsp = SEL_PAD_DEC
    n_top = min(SEL_TOPN, n_sel)
    ci = jnp.arange(n_cmp_pad)[None, :]
    sj = jnp.arange(nsp)[:, None]
    overlap_t = ((ci * CMP_STRIDE <= sj * SEL_BLOCK + SEL_BLOCK - 1) &
                 (ci * CMP_STRIDE + CMP_LEN - 1 >= sj * SEL_BLOCK) & (ci < n_cmp) & (sj < n_sel)).astype(F32)
    kern = functools.partial(_sel_decode_kernel, n_cmp=n_cmp, n_sel=n_sel, n_top=n_top, q_pos=q_pos)
    kv = lambda kd: pl.BlockSpec((None, 1, NSA_KV_HEADS, n_cmp_pad, hd), lambda b: (b, kd, 0, 0, 0))
    n_col = n_batch * NSA_KV_HEADS
    o_cmp, idx = pl.pallas_call(
        kern,
        grid=(n_batch,),
        in_specs=[pl.BlockSpec((None, 2 * n_heads, hd), lambda b: (b, 0, 0)), kv(0), kv(1),
                  pl.BlockSpec((nsp, n_cmp_pad), lambda b: (0, 0))],
        out_specs=(pl.BlockSpec((None, n_heads, hd), lambda b: (b, 0, 0)),
                   pl.BlockSpec((n_top, n_col), lambda b: (0, 0))),
        out_shape=(jax.ShapeDtypeStruct((n_batch, n_heads, hd), F32),
                   jax.ShapeDtypeStruct((n_top, n_col), jnp.int32)),
        scratch_shapes=[pltpu.VMEM((nsp, n_col), F32), pltpu.VMEM((n_col, n_cmp_pad), F32)],
        compiler_params=_params("arbitrary"),
        name="nsa_select_decode",
    )(jnp.pad(qn, ((0, 0), (0, n_heads), (0, 0))), cmp_kv, cmp_kv, overlap_t)
    return o_cmp, idx.T.reshape(n_batch, NSA_KV_HEADS, n_top)


def _attn_decode_kernel(pt_ref, sel_ref, q_ref, cache_ref, new_ref, kw_ref, vw_ref, wnew_ref, oc_ref, gate_ref,
                        o_ref, kbuf, vbuf, sem, *, layer, n_pages, per_page, n_top):
    b = pl.program_id(0)
    G = NSA_KV_HEADS
    n_past_blocks = n_pages * per_page
    nt = (((1,), (1,)), ((), ()))

    def block_id(g, slot):
        return sel_ref[(b * G + g) * n_top + slot]

    def copies():
        out = []
        for g in range(G):
            for slot in range(n_top):
                j = jnp.minimum(block_id(g, slot), n_past_blocks - 1)
                rows = pl.ds((j % per_page) * SEL_BLOCK, SEL_BLOCK)
                page = pt_ref[b * n_pages + j // per_page]
                for kind, buf in ((2, kbuf), (3, vbuf)):
                    out.append(pltpu.make_async_copy(cache_ref.at[layer, page, rows, kind, g, :],
                                                     buf.at[g, pl.ds(slot * SEL_BLOCK, SEL_BLOCK), :], sem))
        return out

    for cp in copies():
        cp.start()
    for cp in copies():
        cp.wait()

    for g in range(G):
        qb = q_ref[g]
        q = qb.astype(F32)
        n_rows = qb.shape[0]
        s = lax.dot_general(qb, kbuf[g].astype(BF16), nt, preferred_element_type=F32)
        slot_of = lax.broadcasted_iota(jnp.int32, s.shape, 1) // SEL_BLOCK
        for slot in range(n_top):
            s = jnp.where((slot_of == slot) & (block_id(g, slot) >= n_past_blocks), NEG_BIG, s)
        k_new = new_ref[2 * G + g:2 * G + g + 1, :]
        v_new = new_ref[3 * G + g:3 * G + g + 1, :]
        s_new = jnp.sum(q * k_new, axis=-1, keepdims=True)
        m = jnp.maximum(jnp.max(s, axis=-1, keepdims=True), s_new)
        p = jnp.exp(s - m)
        p_new = jnp.exp(s_new - m)
        l = jnp.sum(p, axis=-1, keepdims=True) + p_new
        o_sel = (jnp.dot(p.astype(BF16), vbuf[g].astype(BF16), preferred_element_type=F32) + p_new * v_new) / l

        n_buf = kw_ref.shape[0]
        s = lax.dot_general(qb, kw_ref[:, g, :].astype(BF16), nt, preferred_element_type=F32)
        keep = lax.broadcasted_iota(jnp.int32, s.shape, 1) > n_buf - WINDOW
        s = jnp.where(keep, s, NEG_BIG)
        kw_new = wnew_ref[g:g + 1, :]
        vw_new = wnew_ref[G + g:G + g + 1, :]
        s_new = jnp.sum(q * kw_new, axis=-1, keepdims=True)
        m = jnp.maximum(jnp.max(s, axis=-1, keepdims=True), s_new)
        p = jnp.exp(s - m)
        p_new = jnp.exp(s_new - m)
        l = jnp.sum(p, axis=-1, keepdims=True) + p_new
        o_win = (jnp.dot(p.astype(BF16), vw_ref[:, g, :].astype(BF16), preferred_element_type=F32)
                 + p_new * vw_new) / l
        gates = jnp.broadcast_to(gate_ref[g:g + 1, :], (n_rows, 128))
        lane = lax.broadcasted_iota(jnp.int32, (n_rows, 128), 1)
        head = lax.broadcasted_iota(jnp.int32, (n_rows, 128), 0)
        pick = lambda c: jnp.sum(jnp.where(lane == head * 3 + c, gates, 0.0), axis=-1, keepdims=True)
        o_ref[g] = (pick(0) * oc_ref[g] + pick(1) * o_sel + pick(2) * o_win).astype(o_ref.dtype)


def nsa_attend_decode(qr, o_cmp, sel_idx, gates, cache, layer, page_table, new_rows, state_win, new_win):
    n_batch, n_heads, hd = qr.shape
    page = cache.shape[2]
    n_pages = page_table.shape[1]
    n_top = sel_idx.shape[2]
    per_page = page // SEL_BLOCK
    n_buf = state_win.shape[2]
    G, R = NSA_KV_HEADS, NSA_REP
    rp = 16
    pad_heads = lambda a: jnp.pad(a.reshape(n_batch, G, R, hd), ((0, 0), (0, 0), (0, rp - R), (0, 0)))
    per_bg = lambda: pl.BlockSpec((None, G, rp, hd), lambda b, pt, sel: (b, 0, 0, 0))
    per_b = lambda rows: pl.BlockSpec((None, rows, hd), lambda b, pt, sel: (b, 0, 0))
    win_spec = lambda kv: pl.BlockSpec((None, None, n_buf, None, G, hd), lambda b, pt, sel: (layer, b, 0, kv, 0, 0))
    kern = functools.partial(_attn_decode_kernel, layer=layer, n_pages=n_pages, per_page=per_page, n_top=n_top)
    out = pl.pallas_call(
        kern,
        grid_spec=pltpu.PrefetchScalarGridSpec(
            num_scalar_prefetch=2,
            grid=(n_batch,),
            in_specs=[per_bg(), pl.BlockSpec(memory_space=pl.ANY), per_b(4 * G), win_spec(0), win_spec(1),
                      per_b(2 * G), per_bg(), per_b(G)],
            out_specs=per_bg(),
            scratch_shapes=[pltpu.VMEM((G, n_top * SEL_BLOCK, hd), F32), pltpu.VMEM((G, n_top * SEL_BLOCK, hd), F32),
                            pltpu.SemaphoreType.DMA(())]),
        out_shape=jax.ShapeDtypeStruct((n_batch, G, rp, hd), BF16),
        compiler_params=_params("arbitrary"),
        name="nsa_attend_decode",
    )(page_table.reshape(-1).astype(jnp.int32), sel_idx.reshape(-1).astype(jnp.int32),
      pad_heads(qr), cache, new_rows, state_win, state_win, new_win, pad_heads(o_cmp), gates)
    return out[:, :, :R].reshape(n_batch, n_heads * hd)


def _gla_decode_kernel(q_ref, k_ref, v_ref, r_ref, small_ref, w2_ref, b2_ref, ng_ref, s0_ref, y_ref, sf_ref):
    dk, dv = GLA_DK, GLA_DV
    z = jnp.dot(small_ref[:, 0:GLA_RANK].astype(BF16), w2_ref[...].astype(BF16),
                preferred_element_type=F32) + b2_ref[...]
    g_all = (jnp.minimum(z, 0.0) - jnp.log1p(jnp.exp(-jnp.abs(z)))) / GLA_GATE_NORM
    eye = lax.broadcasted_iota(jnp.int32, (dk, dk), 0) == lax.broadcasted_iota(jnp.int32, (dk, dk), 1)
    column = lambda row: jnp.sum(jnp.where(eye, jnp.broadcast_to(row, (dk, dk)), 0.0), axis=1, keepdims=True)
    for b in range(q_ref.shape[0]):
        for h in range(GLA_HEADS):
            ks = slice(h * dk, (h + 1) * dk)
            vs = slice(h * dv, (h + 1) * dv)
            s_new = (jnp.exp(column(g_all[b:b + 1, ks])) * s0_ref[b, h]
                     + column(k_ref[b:b + 1, ks]) * v_ref[b:b + 1, vs])
            sf_ref[b, h] = s_new
            o = jnp.sum(column(q_ref[b:b + 1, ks] * (dk ** -0.5)) * s_new, axis=0, keepdims=True)
            o = o * lax.rsqrt(jnp.mean(o * o, axis=-1, keepdims=True) + RMS_EPS)
            y_ref[b:b + 1, vs] = (o * ng_ref[:, vs] * jax.nn.silu(r_ref[b:b + 1, vs])).astype(BF16)


def gla_decode(p, s0, w2, b2, norm_g):
    n = p.shape[0]
    row = lambda w, off: pl.BlockSpec((n, w), lambda i: (0, off // w))
    const = lambda shape: pl.BlockSpec(shape, lambda i: (0,) * len(shape))
    return pl.pallas_call(
        _gla_decode_kernel,
        grid=(1,),
        in_specs=[row(256, COL_GLA_Q), row(256, COL_GLA_K), row(512, COL_GLA_V), row(512, COL_GLA_R),
                  row(128, COL_SMALL), const((GLA_RANK, GLA_HEADS * GLA_DK)), const((1, GLA_HEADS * GLA_DK)),
                  const((1, GLA_WIDTH)), const(s0.shape)],
        out_specs=(const((n, GLA_WIDTH)), const(s0.shape)),
        out_shape=(jax.ShapeDtypeStruct((n, GLA_WIDTH), BF16), jax.ShapeDtypeStruct(s0.shape, F32)),
        compiler_params=_params("arbitrary"),
        name="gla_decode",
    )(p, p, p, p, p, w2, b2.reshape(1, -1), norm_g.reshape(1, -1), s0)


def decode_mixer(xs2, w_in_packed, w_out_bf16, ln_g, ln_b, gla_w2, gla_b, gla_norm_g, cmp_pos, cmp_w1, cmp_w2,
                 pool_w, pool_scale, cache, layer, page_table, state_win, state_gla, state_pool, past_len,
                 router=None):
    n_dec = xs2.shape[0]
    hd, G = HEAD_DIM, NSA_KV_HEADS
    p = matmul(xs2, w_in_packed)
    pos = jnp.full((n_dec,), past_len, jnp.int32)
    rows, win, qn, qr, _, _, _, _, gates = nsa_prep(p, rope_tables(pos), 1, n_dec)
    cmp_kv = nsa_compress_paged(cache, layer, page_table, cmp_pos, cmp_w1, cmp_w2)
    t_k = past_len + 1
    n_sel = -(-t_k // SEL_BLOCK)
    o_cmp, sel_idx = nsa_select_decode(qn[0].transpose(1, 0, 2), cmp_kv, past_len // CMP_STRIDE - 1, n_sel, past_len)
    y_nsa = nsa_attend_decode(qr[0].transpose(1, 0, 2), o_cmp, sel_idx, gates[0].transpose(1, 0, 2), cache, layer,
                              page_table, rows.reshape(n_dec, 4 * G, hd), state_win, win.reshape(n_dec, 2 * G, hd))
    y_gla, s_gla = gla_decode(p, state_gla, gla_w2, gla_b, gla_norm_g)
    y_pool, pool_rows = pool_mixer(p[:, None, COL_POOL:COL_POOL + POOL_WIDTH], state_pool, pos[:1], pool_w, pool_scale)
    x1 = outproj_ln(xs2, y_gla, y_nsa, y_pool.reshape(n_dec, -1).astype(BF16), w_out_bf16, ln_g, ln_b, router)
    nsa_rows = rows.reshape(n_dec, 1, 4, G, hd)
    new_win = jnp.concatenate([state_win[layer, :, 1:], win.reshape(n_dec, 1, 2, G, hd)], axis=1)
    return x1, nsa_rows, new_win, s_gla, pool_rows


def prompt_mixer(x2, w_in_packed, w_out_bf16, ln_g, ln_b, gla_w2, gla_b, gla_norm_g, cmp_pos, cmp_w1, cmp_w2,
                 pool_w, pool_scale, n_batch, t_len, router=None):
    p = matmul(x2, w_in_packed)
    pos = jnp.arange(t_len, dtype=jnp.int32)
    rows, win, qn, qr, ks, vs, kw, vw, gates = nsa_prep(p, rope_tables(pos), n_batch, t_len)
    cmp_kv = nsa_compress_prompt(rows, cmp_pos, cmp_w1, cmp_w2, n_batch, t_len)
    o_cmp, selb = nsa_select(qn, cmp_kv, t_len // CMP_STRIDE - 1, t_len)
    y_nsa = nsa_attend(qr, o_cmp, selb, gates, ks, vs, kw, vw)
    s0 = jnp.zeros((n_batch, GLA_HEADS, GLA_DK, GLA_DV), F32)
    y_gla, s_gla = gla_mix(p, s0, gla_w2, gla_b, gla_norm_g, n_batch, t_len)
    prev = jnp.zeros((n_batch, POOL_MAX - 1, POOL_WIDTH), F32)
    y_pool = pool_mix(p, prev, 0, pool_w, pool_scale, n_batch, t_len)
    x1 = outproj_ln(x2, y_gla, y_nsa, y_pool, w_out_bf16, ln_g, ln_b, router)
    nsa_rows = rows.reshape(n_batch, t_len, 4, NSA_KV_HEADS, HEAD_DIM)
    n_win = min(WINDOW, t_len)
    win_rows = win.reshape(n_batch, t_len, 2, NSA_KV_HEADS, HEAD_DIM)[:, t_len - n_win:]
    pool_rows = p.reshape(n_batch, t_len, PACKED_WIDTH)[:, t_len - (POOL_MAX - 1):, COL_POOL:COL_POOL + POOL_WIDTH]
    return x1, nsa_rows, win_rows, s_gla, pool_rows


def split_proj(p):
    out = {}
    off = 0
    for name, size in PROJ_SIZES:
        out[name] = p[..., off:off + size]
        off += size
    return out


def layer_norm(x, g, b):
    xf = x.astype(jnp.float32)
    xc = xf - jnp.mean(xf, -1, keepdims=True)
    var = jnp.mean(xc * xc, -1, keepdims=True)
    return (xc * lax.rsqrt(var + LN_EPS) * g + b).astype(x.dtype)


def rope(x, pos):
    half = ROPE_DIM // 2
    inv_freq = ROPE_THETA ** (-jnp.arange(half, dtype=jnp.float32) / half)
    ang = pos.astype(jnp.float32)[:, None] * inv_freq[None, :]
    cos = jnp.cos(ang)[:, None, :]
    sin = jnp.sin(ang)[:, None, :]
    xf = x.astype(jnp.float32)
    x1, x2 = xf[..., :half], xf[..., half:ROPE_DIM]
    out = jnp.concatenate([x1 * cos - x2 * sin, x2 * cos + x1 * sin, xf[..., ROPE_DIM:]], axis=-1)
    return out.astype(x.dtype)


def masked_softmax(s, mask):
    s = jnp.where(mask, s.astype(jnp.float32), -jnp.inf)
    m = jnp.max(s, axis=-1, keepdims=True)
    m = jnp.where(jnp.isfinite(m), m, 0.0)
    p = jnp.where(mask, jnp.exp(s - m), 0.0)
    return p / jnp.maximum(jnp.sum(p, -1, keepdims=True), 1e-30)


def gla_recurrence(q, k, v, g, s0):
    B, T, H, _ = q.shape
    C = GLA_CHUNK
    n_chunks = -(-T // C)
    pad = n_chunks * C - T

    def prep(a):
        a = jnp.pad(a, ((0, 0), (0, pad), (0, 0), (0, 0)))
        return a.reshape(B, n_chunks, C, H, a.shape[-1]).transpose(1, 0, 3, 2, 4)

    causal = jnp.tril(jnp.ones((C, C), dtype=bool))

    def step(S, inp):
        qi, ki, vi, gi = [a.astype(jnp.float32) for a in inp]
        b = jnp.cumsum(gi, axis=2)
        o_inter = jnp.einsum('bhtk,bhkv->bhtv', qi * jnp.exp(b), S)
        diff = jnp.where(causal[:, :, None], b[:, :, :, None, :] - b[:, :, None, :, :], -jnp.inf)
        attn = jnp.einsum('bhtk,bhsk,bhtsk->bhts', qi, ki, jnp.exp(diff))
        o = o_inter + jnp.einsum('bhts,bhsv->bhtv', attn, vi)
        b_last = b[:, :, -1:, :]
        S = jnp.exp(b_last[:, :, 0, :])[..., None] * S + jnp.einsum('bhsk,bhsv->bhkv', ki * jnp.exp(b_last - b), vi)
        return S, o

    S, o = lax.scan(step, s0.astype(jnp.float32), (prep(q), prep(k), prep(v), prep(g)))
    o = o.transpose(1, 0, 3, 2, 4).reshape(B, n_chunks * C, H, v.shape[-1])[:, :T]
    return o, S.astype(s0.dtype)


def gla_mixer(parts, s0, w2, b2, norm_g):
    B, T = parts['gla_q'].shape[:2]
    q = parts['gla_q'].reshape(B, T, GLA_HEADS, GLA_DK) * (GLA_DK ** -0.5)
    k = parts['gla_k'].reshape(B, T, GLA_HEADS, GLA_DK)
    v = parts['gla_v'].reshape(B, T, GLA_HEADS, GLA_DV)
    g = jax.nn.log_sigmoid((parts['gla_glr'] @ w2 + b2).astype(jnp.float32)) / GLA_GATE_NORM
    g = g.reshape(B, T, GLA_HEADS, GLA_DK)
    o, s_new = gla_recurrence(q, k, v, g, s0)
    o = o * lax.rsqrt(jnp.mean(o * o, -1, keepdims=True) + RMS_EPS)
    out = o.reshape(B, T, GLA_WIDTH) * norm_g * jax.nn.silu(parts['gla_r'].astype(jnp.float32))
    return out.astype(parts['gla_v'].dtype), s_new


def nsa_compress(kx, pos_emb, w1, w2):
    B, T, G, D = kx.shape
    nh = T // CMP_STRIDE
    halves = kx[:, :nh * CMP_STRIDE].reshape(B, nh, CMP_STRIDE, G, D).astype(jnp.float32)
    pe = pos_emb.reshape(2, CMP_STRIDE, D)
    w = w1.reshape(2, CMP_STRIDE, D, CMP_HIDDEN)
    h_lo = jnp.einsum('bnjgd,jdh->bngh', halves + pe[0][None, None, :, None, :], w[0])
    h_hi = jnp.einsum('bnjgd,jdh->bngh', halves + pe[1][None, None, :, None, :], w[1])
    h = jax.nn.gelu(h_lo[:, :-1] + h_hi[:, 1:])
    return jnp.einsum('bngh,hd->bngd', h, w2)


def nsa_global(qn, qr, rows, q_pos, cmp_pos, cmp_w1, cmp_w2):
    B, Tq, G, R, D = qn.shape
    Tk = rows.shape[1]
    scale = HEAD_DIM ** -0.5
    kcmp = nsa_compress(rows[:, :, 0], cmp_pos[0], cmp_w1[0], cmp_w2[0])
    vcmp = nsa_compress(rows[:, :, 1], cmp_pos[1], cmp_w1[1], cmp_w2[1])
    n_cmp = kcmp.shape[1]
    cmp_end = jnp.arange(n_cmp) * CMP_STRIDE + CMP_LEN - 1
    n_sel = -(-Tk // SEL_BLOCK)
    pad = n_sel * SEL_BLOCK - Tk

    def to_blocks(a):
        a = jnp.pad(a, ((0, 0), (0, pad), (0, 0), (0, 0)))
        return a.reshape(B, n_sel, SEL_BLOCK, G, D).transpose(0, 3, 1, 2, 4)

    ksb = to_blocks(rows[:, :, 2])
    vsb = to_blocks(rows[:, :, 3])
    ci = jnp.arange(n_cmp)[:, None]
    sj = jnp.arange(n_sel)[None, :]
    overlap = ((ci * CMP_STRIDE <= sj * SEL_BLOCK + SEL_BLOCK - 1) &
               (ci * CMP_STRIDE + CMP_LEN - 1 >= sj * SEL_BLOCK)).astype(jnp.float32)
    n_top = min(SEL_TOPN, n_sel)
    gather = jax.vmap(jax.vmap(lambda blocks, idx: blocks[idx]))
    blk_ids = jnp.arange(n_sel)

    def block_fn(args):
        qnb, qrb, qp = args
        qb = qp.shape[0]
        s = jnp.einsum('bqgrd,bngd->bgrqn', qnb, kcmp) * scale
        p_c = masked_softmax(s, cmp_end[None, :] <= qp[:, None])
        o_c = jnp.einsum('bgrqn,bngd->bqgrd', p_c, vcmp)
        imp = jnp.einsum('bgqn,nj->bgqj', jnp.sum(p_c, axis=2), overlap)
        cur = qp[:, None] // SEL_BLOCK
        valid = blk_ids[None, :] <= cur
        forced = (blk_ids[None, :] == 0) | (blk_ids[None, :] == cur) | (blk_ids[None, :] == cur - 1)
        imp = jnp.where(valid, jnp.where(forced, jnp.inf, imp), -jnp.inf)
        top_s, top_i = lax.top_k(imp, n_top)
        kg = gather(ksb, top_i)
        vg = gather(vsb, top_i)
        kpos = top_i[..., None] * SEL_BLOCK + jnp.arange(SEL_BLOCK)
        mask = (top_s > -jnp.inf)[..., None] & (kpos <= qp[None, None, :, None, None])
        s2 = jnp.einsum('bqgrd,bgqnjd->bgrqnj', qrb, kg) * scale
        s2 = s2.reshape(B, G, R, qb, n_top * SEL_BLOCK)
        p_s = masked_softmax(s2, mask.reshape(B, G, 1, qb, n_top * SEL_BLOCK))
        p_s = p_s.reshape(B, G, R, qb, n_top, SEL_BLOCK)
        o_s = jnp.einsum('bgrqnj,bgqnjd->bqgrd', p_s, vg.astype(jnp.float32))
        return o_c, o_s

    qb = min(Tq, Q_BLOCK)
    nq = -(-Tq // qb)
    padq = nq * qb - Tq
    qpad = ((0, 0), (0, padq), (0, 0), (0, 0), (0, 0))
    qn_b = jnp.pad(qn, qpad).reshape(B, nq, qb, G, R, D).swapaxes(0, 1)
    qr_b = jnp.pad(qr, qpad).reshape(B, nq, qb, G, R, D).swapaxes(0, 1)
    qp_b = jnp.pad(q_pos, (0, padq), mode='edge').reshape(nq, qb)
    o_c, o_s = lax.map(block_fn, (qn_b, qr_b, qp_b))
    o_c = o_c.swapaxes(0, 1).reshape(B, nq * qb, G, R, D)[:, :Tq]
    o_s = o_s.swapaxes(0, 1).reshape(B, nq * qb, G, R, D)[:, :Tq]
    return o_c, o_s


def band_attend(q, k, v, qpos, kpos):
    s = jnp.einsum('bqgrd,bkgd->bgrqk', q, k) * (HEAD_DIM ** -0.5)
    mask = ((kpos[None, :] <= qpos[:, None]) & (kpos[None, :] > qpos[:, None] - WINDOW) & (kpos[None, :] >= 0))
    p = masked_softmax(s, mask)
    return jnp.einsum('bgrqk,bkgd->bqgrd', p, v.astype(jnp.float32))


def sliding_prompt(q, k, v):
    B, T, G, R, D = q.shape
    kp = jnp.pad(k, ((0, 0), (WINDOW, 0), (0, 0), (0, 0)))
    vp = jnp.pad(v, ((0, 0), (WINDOW, 0), (0, 0), (0, 0)))
    nq = T // Q_BLOCK

    def fn(i):
        q0 = i * Q_BLOCK
        qi = lax.dynamic_slice_in_dim(q, q0, Q_BLOCK, axis=1)
        ki = lax.dynamic_slice_in_dim(kp, q0, WINDOW + Q_BLOCK, axis=1)
        vi = lax.dynamic_slice_in_dim(vp, q0, WINDOW + Q_BLOCK, axis=1)
        qpos = q0 + jnp.arange(Q_BLOCK)
        kpos = q0 - WINDOW + jnp.arange(WINDOW + Q_BLOCK)
        return band_attend(qi, ki, vi, qpos, kpos)

    o = lax.map(fn, jnp.arange(nq))
    return o.swapaxes(0, 1).reshape(B, T, G, R, D)


def nsa_mixer(parts, pos, nsa_past, win_past, cmp_pos, cmp_w1, cmp_w2):
    B, T = parts['nsa_q'].shape[:2]
    dt = parts['nsa_q'].dtype
    q = parts['nsa_q'].reshape(B, T, NSA_HEADS, HEAD_DIM)
    q_rope = rope(q, pos)
    kv = lambda name: parts[name].reshape(B, T, NSA_KV_HEADS, HEAD_DIM)
    k_win, v_win = rope(kv('win_k'), pos), kv('win_v')
    new_rows = jnp.stack([kv('cmp_k'), kv('cmp_v'), rope(kv('slc_k'), pos), kv('slc_v')], axis=2)
    rows = new_rows if nsa_past is None else jnp.concatenate([nsa_past.astype(dt), new_rows], axis=1)
    qg = q.reshape(B, T, NSA_KV_HEADS, NSA_REP, HEAD_DIM)
    qrg = q_rope.reshape(B, T, NSA_KV_HEADS, NSA_REP, HEAD_DIM)
    o_cmp, o_slc = nsa_global(qg, qrg, rows, pos, cmp_pos, cmp_w1, cmp_w2)
    win_rows = jnp.stack([k_win, v_win], axis=2)
    if win_past is None:
        o_win = sliding_prompt(qrg, k_win, v_win)
        new_win = win_rows[:, T - min(WINDOW, T):]
    else:
        n_buf = win_past.shape[1]
        ext = jnp.concatenate([win_past.astype(dt), win_rows], axis=1)
        kpos = pos[0] - n_buf + jnp.arange(n_buf + T)
        o_win = band_attend(qrg, ext[:, :, 0], ext[:, :, 1], pos, kpos)
        new_win = ext[:, T:]
    gates = jax.nn.sigmoid(parts['nsa_gate'].astype(jnp.float32)).reshape(B, T, NSA_KV_HEADS, NSA_REP, 3)
    o = gates[..., 0:1] * o_cmp + gates[..., 1:2] * o_slc + gates[..., 2:3] * o_win
    return o.reshape(B, T, NSA_WIDTH).astype(dt), new_rows, new_win


def pool_mixer(u, prev, pos, w_pool, scale):
    B, T, C = u.shape
    P = POOL_MAX - 1
    ext = jnp.concatenate([prev.astype(u.dtype), u], axis=1).astype(jnp.float32)
    cs = jnp.concatenate([jnp.zeros((B, 1, C), jnp.float32), jnp.cumsum(ext, axis=1)], axis=1)
    end = cs[:, P + 1:]
    means = []
    for gi, w in enumerate(POOL_WINDOWS):
        sl = slice(gi * POOL_GROUP_DIM, (gi + 1) * POOL_GROUP_DIM)
        start = cs[:, P + 1 - w:P + 1 - w + T, sl]
        cnt = jnp.minimum(pos + 1, w).astype(jnp.float32)[None, :, None]
        means.append((end[..., sl] - start) / cnt)
    pooled = (jnp.concatenate(means, axis=-1) - ext[:, P:]).reshape(B, T, POOL_GROUPS, POOL_GROUP_DIM)
    y = jnp.einsum('btgc,gcd->btgd', pooled, w_pool.astype(jnp.float32)).reshape(B, T, C) * scale
    return y.astype(u.dtype), ext[:, -P:].astype(u.dtype)


def mixer_layer(x, pos0, gla_s0, pool_prev, nsa_past, win_past,
                w_in, gla_w2, gla_b, gla_norm_g, cmp_pos, cmp_w1, cmp_w2, pool_w, pool_scale, w_out):
    B, T, _ = x.shape
    pos = pos0 + jnp.arange(T, dtype=jnp.int32)
    parts = split_proj(mm3(x, w_in))
    y_gla, s_gla = gla_mixer(parts, gla_s0, gla_w2, gla_b, gla_norm_g)
    y_nsa, nsa_rows, win_rows = nsa_mixer(parts, pos, nsa_past, win_past, cmp_pos, cmp_w1, cmp_w2)
    y_pool, pool_rows = pool_mixer(parts['pool'], pool_prev, pos, pool_w, pool_scale)
    y = mm3(jnp.concatenate([y_gla, y_nsa, y_pool], axis=-1), w_out)
    return y, nsa_rows, win_rows, s_gla, pool_rows


def swiglu(x, wg, wu, wd):
    return mm3(jax.nn.silu(mm3(x, wg)) * mm3(x, wu), wd)


def moe_ffn(x, router, wg, wu, wd):
    B, T, D = x.shape
    n_tok = B * T
    xt = x.reshape(n_tok, D)
    logits = (xt @ router).astype(jnp.float32)
    top_v, top_i = lax.top_k(logits, TOP_K)
    gates = jax.nn.softmax(top_v, axis=-1)
    n_asg = n_tok * TOP_K
    e_flat = top_i.reshape(n_asg)
    tok_flat = jnp.arange(n_asg, dtype=jnp.int32) // TOP_K
    gate_flat = gates.reshape(n_asg)
    blk = MOE_ROW_BLOCK if n_asg >= N_EXPERTS * MOE_ROW_BLOCK else MOE_MIN_BLOCK
    n_blk = -(-(n_asg + N_EXPERTS * (blk - 1)) // blk)
    order = jnp.argsort(e_flat)
    e_sorted = e_flat[order]
    counts = jnp.bincount(e_flat, length=N_EXPERTS)
    padded = (counts + blk - 1) // blk * blk
    pad_end = jnp.cumsum(padded)
    pad_start = pad_end - padded
    start = jnp.cumsum(counts) - counts
    dest = pad_start[e_sorted] + jnp.arange(n_asg) - start[e_sorted]
    row_tok = jnp.zeros((n_blk * blk,), jnp.int32).at[dest].set(tok_flat[order])
    row_gate = jnp.zeros((n_blk * blk,), jnp.float32).at[dest].set(gate_flat[order])
    blk_expert = jnp.minimum(jnp.searchsorted(pad_end, jnp.arange(n_blk) * blk, side='right'), N_EXPERTS - 1)

    def expert_block(args):
        rows, e = args
        xb = xt[rows]
        return (jax.nn.silu(xb @ wg[e]) * (xb @ wu[e])) @ wd[e]

    out = lax.map(expert_block, (row_tok.reshape(n_blk, blk), blk_expert))
    y = jnp.zeros((n_tok, D), jnp.float32).at[row_tok].add(
        out.reshape(n_blk * blk, D).astype(jnp.float32) * row_gate[:, None])
    return y.reshape(B, T, D).astype(x.dtype)


def kernel(x_prompt, x_sample, cache_nsa, page_table, state_win, state_gla, state_pool, w_in, gla_gate_w2, gla_gate_b, gla_norm_g, nsa_cmp_pos, nsa_cmp_w1, nsa_cmp_w2, pool_w, pool_scale, w_out, ln1_g, ln1_b, ln2_g, ln2_b, ffn_w_gate, ffn_w_up, ffn_w_down, moe_router, moe_w_gate, moe_w_up, moe_w_down):
    n_prompt, t_len, d = x_prompt.shape
    n_dec = x_sample.shape[0]
    xp, xs = x_prompt.reshape(n_prompt * t_len, d), x_sample.reshape(n_dec, d)
    nsa_p, nsa_s, win_p, win_s, gla_p, gla_s, pool_p, pool_s = [], [], [], [], [], [], [], []
    for l in range(DEPTH):
        i = l // 2
        router = moe_router[i] if l % 2 else None
        w_in_packed, w_out_bf16 = pack_w_in(w_in[l]), w_out[l].astype(BF16)
        lw = (w_in_packed, w_out_bf16, ln1_g[l], ln1_b[l], gla_gate_w2[l], gla_gate_b[l], gla_norm_g[l],
              nsa_cmp_pos[l], nsa_cmp_w1[l], nsa_cmp_w2[l], pool_w[l], pool_scale[l])
        xp, r_p, w_p, g_p, p_p = prompt_mixer(xp, *lw, n_prompt, t_len, router)
        xs, r_s, w_s, g_s, p_s = decode_mixer(xs, *lw, cache_nsa, l, page_table, state_win, state_gla[l],
                                              state_pool[l], PAST_LEN, router)
        if l % 2 == 0:
            wg, wu, wd = ffn_w_gate[i].astype(BF16), ffn_w_up[i].astype(BF16), ffn_w_down[i].astype(BF16)
            xp = ffn_ln(xp, wg, wu, wd, ln2_g[l], ln2_b[l])
            xs = ffn_ln(xs, wg, wu, wd, ln2_g[l], ln2_b[l])
        else:
            (xp, lg_p), (xs, lg_s) = xp, xs
            xp, xs = moe_ln([xp, xs], [lg_p, lg_s], moe_w_gate[i], moe_w_up[i], moe_w_down[i], ln2_g[l], ln2_b[l])
        nsa_p.append(r_p); nsa_s.append(r_s); win_p.append(w_p); win_s.append(w_s)
        gla_p.append(g_p); gla_s.append(g_s); pool_p.append(p_p); pool_s.append(p_s)
    return (xp.reshape(n_prompt, t_len, d), xs.reshape(x_sample.shape), jnp.stack(nsa_p), jnp.stack(nsa_s),
            jnp.stack(win_p), jnp.stack(win_s), jnp.stack(gla_p), jnp.stack(gla_s), jnp.stack(pool_p),
            jnp.stack(pool_s))
```

```python
import functools

import jax
import jax.numpy as jnp
from jax import lax
from jax.experimental import pallas as pl
from jax.experimental.pallas import tpu as pltpu

D_MODEL = 2048
DEPTH = 2
PAST_LEN = 16384
HEAD_DIM = 128
GLA_HEADS = 4
GLA_DK = 64
GLA_DV = 128
GLA_RANK = 16
GLA_GATE_NORM = 16.0
GLA_CHUNK = 64
GLA_WIDTH = GLA_HEADS * GLA_DV
NSA_HEADS = 8
NSA_KV_HEADS = 2
NSA_REP = NSA_HEADS // NSA_KV_HEADS
NSA_WIDTH = NSA_HEADS * HEAD_DIM
CMP_LEN = 32
CMP_STRIDE = 16
CMP_HIDDEN = 128
SEL_BLOCK = 64
SEL_TOPN = 16
WINDOW = 512
POOL_GROUPS = 4
POOL_GROUP_DIM = 128
POOL_WIDTH = POOL_GROUPS * POOL_GROUP_DIM
POOL_WINDOWS = (2, 4, 8, 16)
POOL_MAX = 16
ROPE_THETA = 500000.0
ROPE_DIM = HEAD_DIM // 4
N_EXPERTS = 8
TOP_K = 2
ALPHA = (2 * DEPTH) ** 0.25
LN_EPS = 1e-5
RMS_EPS = 1e-6

PROJ_SIZES = (
    ('gla_q', GLA_HEADS * GLA_DK), ('gla_k', GLA_HEADS * GLA_DK), ('gla_v', GLA_HEADS * GLA_DV),
    ('gla_glr', GLA_RANK), ('gla_r', GLA_HEADS * GLA_DV),
    ('nsa_q', NSA_HEADS * HEAD_DIM),
    ('cmp_k', NSA_KV_HEADS * HEAD_DIM), ('cmp_v', NSA_KV_HEADS * HEAD_DIM),
    ('slc_k', NSA_KV_HEADS * HEAD_DIM), ('slc_v', NSA_KV_HEADS * HEAD_DIM),
    ('win_k', NSA_KV_HEADS * HEAD_DIM), ('win_v', NSA_KV_HEADS * HEAD_DIM),
    ('nsa_gate', 3 * NSA_HEADS),
    ('pool', POOL_WIDTH),
)

GLA_SUB = 16
SEL_PAD = 128

BF16 = jnp.bfloat16
F32 = jnp.float32
NEG_BIG = -1e30
VMEM_LIMIT_BYTES = 56 * 1024 * 1024

COL_NSA_Q = 0
COL_ROWS = 1024
COL_WIN = 2048
COL_POOL = 2560
COL_GLA_V = 3072
COL_GLA_R = 3584
COL_GLA_Q = 4096
COL_GLA_K = 4352
COL_SMALL = 4608
PACKED_WIDTH = 4736
SMALL_GATE_OFF = GLA_RANK


def _params(*sem, vmem_limit_bytes=VMEM_LIMIT_BYTES):
    return pltpu.CompilerParams(dimension_semantics=sem, vmem_limit_bytes=vmem_limit_bytes)


def _proj_offsets():
    out, off = {}, 0
    for name, size in PROJ_SIZES:
        out[name] = (off, size)
        off += size
    return out


def pack_w_in(w):
    offs = _proj_offsets()
    sl = lambda n: w[:, offs[n][0]:offs[n][0] + offs[n][1]]
    pad = jnp.zeros((w.shape[0], 128 - GLA_RANK - 3 * NSA_HEADS), w.dtype)
    cols = [sl('nsa_q'), sl('cmp_k'), sl('cmp_v'), sl('slc_k'), sl('slc_v'), sl('win_k'), sl('win_v'),
            sl('pool'), sl('gla_v'), sl('gla_r'), sl('gla_q'), sl('gla_k'), sl('gla_glr'), sl('nsa_gate'), pad]
    return jnp.concatenate(cols, axis=1).astype(BF16)


def rope_tables(pos):
    half = ROPE_DIM // 2
    inv_freq = ROPE_THETA ** (-jnp.arange(half, dtype=F32) / half)
    ang = pos.astype(F32)[:, None] * inv_freq[None, :]
    cos, sin = jnp.cos(ang), jnp.sin(ang)
    t = pos.shape[0]
    c = jnp.concatenate([cos, cos, jnp.ones((t, HEAD_DIM - ROPE_DIM), F32)], axis=1)
    sa = jnp.concatenate([-sin, jnp.zeros((t, HEAD_DIM - half), F32)], axis=1)
    sb = jnp.concatenate([jnp.zeros((t, half), F32), sin, jnp.zeros((t, HEAD_DIM - ROPE_DIM), F32)], axis=1)
    return c, sa, sb


def _pick_tile(n, pref):
    for t in pref:
        if n % t == 0:
            return t
    return n


def _mm_kernel(x_ref, w_ref, o_ref, xb_ref):
    @pl.when(pl.program_id(1) == 0)
    def _():
        xb_ref[...] = x_ref[...].astype(BF16)

    o_ref[...] = jnp.dot(xb_ref[...], w_ref[...].astype(BF16), preferred_element_type=F32)


def matmul(x, w):
    m, k = x.shape
    n = w.shape[1]
    tm = _pick_tile(m, tuple(t for t in (1024, 512, 256, 128, 64, 32, 16, 8) if t * k <= 2048 * 1024))
    tn = 512 if n >= 512 else n
    return pl.pallas_call(
        _mm_kernel,
        grid=(m // tm, pl.cdiv(n, tn)),
        in_specs=[pl.BlockSpec((tm, k), lambda i, j: (i, 0)),
                  pl.BlockSpec((k, tn), lambda i, j: (0, j))],
        out_specs=pl.BlockSpec((tm, tn), lambda i, j: (i, j)),
        out_shape=jax.ShapeDtypeStruct((m, n), F32),
        scratch_shapes=[pltpu.VMEM((tm, k), BF16)],
        compiler_params=_params("arbitrary", "arbitrary"),
        name="matmul",
    )(x, w)


def _rope(x, c, sa, sb):
    return x * c + pltpu.roll(x, HEAD_DIM - ROPE_DIM // 2, 1) * sa + pltpu.roll(x, ROPE_DIM // 2, 1) * sb


def _nsa_prep_kernel(q_ref, rows_ref, win_ref, small_ref, c_ref, sa_ref, sb_ref,
                     rows_o, win_o, qn_o, qr_o, ks_o, vs_o, kw_o, vw_o, gate_o):
    c, sa, sb = c_ref[...], sa_ref[...], sb_ref[...]
    scale = HEAD_DIM ** -0.5
    hd = HEAD_DIM
    for h in range(NSA_HEADS):
        x = q_ref[:, h * hd:(h + 1) * hd]
        qn_o[0, h] = (x * scale).astype(BF16)
        qr_o[0, h] = (_rope(x, c, sa, sb) * scale).astype(BF16)
    ones = jnp.ones((q_ref.shape[0], hd), BF16)
    rows_o[:, 0:4 * hd] = rows_ref[:, 0:4 * hd]
    for g in range(NSA_KV_HEADS):
        k = _rope(rows_ref[:, (4 + g) * hd:(5 + g) * hd], c, sa, sb)
        rows_o[:, (4 + g) * hd:(5 + g) * hd] = k
        ks_o[0, g] = k.astype(BF16)
        v = rows_ref[:, (6 + g) * hd:(7 + g) * hd]
        rows_o[:, (6 + g) * hd:(7 + g) * hd] = v
        vs_o[0, g, :, 0:hd] = v.astype(BF16)
        vs_o[0, g, :, hd:2 * hd] = ones
        k = _rope(win_ref[:, g * hd:(g + 1) * hd], c, sa, sb)
        win_o[:, g * hd:(g + 1) * hd] = k
        kw_o[0, g] = k.astype(BF16)
        v = win_ref[:, (2 + g) * hd:(3 + g) * hd]
        win_o[:, (2 + g) * hd:(3 + g) * hd] = v
        vw_o[0, g, :, 0:hd] = v.astype(BF16)
        vw_o[0, g, :, hd:2 * hd] = ones
    sig = jax.nn.sigmoid(small_ref[...])
    per_g = 3 * NSA_REP
    for g in range(NSA_KV_HEADS):
        gate_o[0, g] = pltpu.roll(sig, 128 - SMALL_GATE_OFF - g * per_g, 1)


def nsa_prep(p, tables, n_batch, t_len):
    tr = _pick_tile(t_len, (512, 256, 128, 64, 32, 16))
    nt = t_len // tr
    n = n_batch * t_len
    hd = HEAD_DIM
    row = lambda w, cb: pl.BlockSpec((tr, w), lambda b, i: (b * nt + i, cb))
    tab = pl.BlockSpec((tr, hd), lambda b, i: (i, 0))
    head = lambda nh, w: pl.BlockSpec((1, nh, tr, w), lambda b, i: (b, 0, i, 0))
    out_shape = (
        jax.ShapeDtypeStruct((n, 8 * hd), F32),
        jax.ShapeDtypeStruct((n, 4 * hd), F32),
        jax.ShapeDtypeStruct((n_batch, NSA_HEADS, t_len, hd), BF16),
        jax.ShapeDtypeStruct((n_batch, NSA_HEADS, t_len, hd), BF16),
        jax.ShapeDtypeStruct((n_batch, NSA_KV_HEADS, t_len, hd), BF16),
        jax.ShapeDtypeStruct((n_batch, NSA_KV_HEADS, t_len, 2 * hd), BF16),
        jax.ShapeDtypeStruct((n_batch, NSA_KV_HEADS, t_len, hd), BF16),
        jax.ShapeDtypeStruct((n_batch, NSA_KV_HEADS, t_len, 2 * hd), BF16),
        jax.ShapeDtypeStruct((n_batch, NSA_KV_HEADS, t_len, 128), F32),
    )
    return pl.pallas_call(
        _nsa_prep_kernel,
        grid=(n_batch, nt),
        in_specs=[row(8 * hd, COL_NSA_Q // (8 * hd)), row(8 * hd, COL_ROWS // (8 * hd)),
                  row(4 * hd, COL_WIN // (4 * hd)), row(128, COL_SMALL // 128), tab, tab, tab],
        out_specs=(row(8 * hd, 0), row(4 * hd, 0), head(NSA_HEADS, hd), head(NSA_HEADS, hd),
                   head(NSA_KV_HEADS, hd), head(NSA_KV_HEADS, 2 * hd), head(NSA_KV_HEADS, hd),
                   head(NSA_KV_HEADS, 2 * hd), head(NSA_KV_HEADS, 128)),
        out_shape=out_shape,
        compiler_params=_params("arbitrary", "arbitrary"),
        name="nsa_prep",
    )(p, p, p, p, *tables)


def _nsa_cmp_kernel(x_ref, pe_ref, w1_ref, w2_ref, o_ref):
    nh = o_ref.shape[0]
    h_lo = jnp.zeros((nh, CMP_HIDDEN), F32)
    h_hi = jnp.zeros((nh, CMP_HIDDEN), F32)
    for j in range(CMP_STRIDE):
        xj = x_ref[pl.ds(j, nh, stride=CMP_STRIDE), :]
        h_lo += jnp.dot((xj + pe_ref[j:j + 1, :]).astype(BF16), w1_ref[j].astype(BF16), preferred_element_type=F32)
        h_hi += jnp.dot((xj + pe_ref[CMP_STRIDE + j:CMP_STRIDE + j + 1, :]).astype(BF16),
                        w1_ref[CMP_STRIDE + j].astype(BF16), preferred_element_type=F32)
    h = jax.nn.gelu(h_lo + pltpu.roll(h_hi, nh - 1, 0))
    o_ref[...] = jnp.dot(h.astype(BF16), w2_ref[...].astype(BF16), preferred_element_type=F32).astype(BF16)


def nsa_compress_prompt(rows, cmp_pos, cmp_w1, cmp_w2, n_batch, t_len):
    nh = t_len // CMP_STRIDE
    hd = HEAD_DIM
    rows3 = rows.reshape(n_batch, t_len, 8 * hd)
    return pl.pallas_call(
        _nsa_cmp_kernel,
        grid=(n_batch, 2, NSA_KV_HEADS),
        in_specs=[pl.BlockSpec((None, t_len, hd), lambda b, kd, g: (b, 0, kd * NSA_KV_HEADS + g)),
                  pl.BlockSpec((None, CMP_LEN, hd), lambda b, kd, g: (kd, 0, 0)),
                  pl.BlockSpec((None, CMP_LEN, hd, CMP_HIDDEN), lambda b, kd, g: (kd, 0, 0, 0)),
                  pl.BlockSpec((None, CMP_HIDDEN, hd), lambda b, kd, g: (kd, 0, 0))],
        out_specs=pl.BlockSpec((None, None, None, nh, hd), lambda b, kd, g: (b, kd, g, 0, 0)),
        out_shape=jax.ShapeDtypeStruct((n_batch, 2, NSA_KV_HEADS, nh, hd), BF16),
        compiler_params=_params("arbitrary", "arbitrary", "arbitrary"),
        name="nsa_compress",
    )(rows3, cmp_pos, cmp_w1, cmp_w2)


def _nsa_select_kernel(qn_ref, kc_ref, vc_ref, ovt_ref, oc_ref, selb_ref, *, n_cmp, n_top):
    rep, tq, hd = qn_ref.shape[1], qn_ref.shape[2], qn_ref.shape[3]
    n_cmp_pad = kc_ref.shape[0]
    n_sel = ovt_ref.shape[0]
    q0 = pl.program_id(2) * tq
    q = qn_ref[0].reshape(rep * tq, hd)
    s = lax.dot_general(q, kc_ref[...], (((1,), (1,)), ((), ())), preferred_element_type=F32)
    row = lax.broadcasted_iota(jnp.int32, (rep * tq, n_cmp_pad), 0)
    col = lax.broadcasted_iota(jnp.int32, (rep * tq, n_cmp_pad), 1)
    qpos = q0 + (row & (tq - 1))
    mask = (col * CMP_STRIDE + (CMP_LEN - 1) <= qpos) & (col < n_cmp)
    s = jnp.where(mask, s, -jnp.inf)
    m = jnp.max(s, axis=-1, keepdims=True)
    m = jnp.where(m > -jnp.inf, m, 0.0)
    p = jnp.where(mask, jnp.exp(s - m), 0.0)
    p = p / jnp.maximum(jnp.sum(p, axis=-1, keepdims=True), 1e-30)
    oc = jnp.dot(p.astype(BF16), vc_ref[...], preferred_element_type=F32)
    oc_ref[0] = oc.reshape(rep, tq, hd).astype(BF16)
    psum = p[0:tq]
    for r in range(1, rep):
        psum = psum + p[r * tq:(r + 1) * tq]
    imp = lax.dot_general(ovt_ref[...], psum, (((1,), (1,)), ((), ())), preferred_element_type=F32,
                          precision=lax.Precision.HIGHEST)
    blk = lax.broadcasted_iota(jnp.int32, (n_sel, tq), 0)
    cur = (q0 + lax.broadcasted_iota(jnp.int32, (n_sel, tq), 1)) // SEL_BLOCK
    forced = (blk == 0) | (blk == cur) | (blk == cur - 1)
    v = jnp.where(blk <= cur, jnp.where(forced, jnp.inf, imp), -jnp.inf)
    rank = jnp.zeros((n_sel, tq), jnp.int32)
    for i in range(n_sel):
        vi = v[i:i + 1, :]
        ahead = (vi > v) | ((vi == v) & (blk > i))
        rank = rank + ahead.astype(jnp.int32)
    selb_t = jnp.where((rank < n_top) & (v > -jnp.inf), 0.0, NEG_BIG)
    pad = jnp.full((SEL_PAD - n_sel, tq), NEG_BIG, F32)
    selb_ref[0, 0] = jnp.concatenate([selb_t, pad], axis=0).T.astype(BF16)


def nsa_select(qn, cmp_kv, n_cmp, t_k):
    n_batch, _, t_len, hd = qn.shape
    n_cmp_pad = cmp_kv.shape[3]
    n_sel = -(-t_k // SEL_BLOCK)
    tq = _pick_tile(t_len, (256, 128, 64, 32, 16))
    ci = jnp.arange(n_cmp_pad)[None, :]
    sj = jnp.arange(n_sel)[:, None]
    overlap_t = ((ci * CMP_STRIDE <= sj * SEL_BLOCK + SEL_BLOCK - 1) &
                 (ci * CMP_STRIDE + CMP_LEN - 1 >= sj * SEL_BLOCK) & (ci < n_cmp)).astype(F32)
    kern = functools.partial(_nsa_select_kernel, n_cmp=n_cmp, n_top=min(SEL_TOPN, n_sel))
    return pl.pallas_call(
        kern,
        grid=(n_batch, NSA_KV_HEADS, t_len // tq),
        in_specs=[pl.BlockSpec((1, NSA_REP, tq, hd), lambda b, g, i: (b, g, i, 0)),
                  pl.BlockSpec((None, None, None, n_cmp_pad, hd), lambda b, g, i: (b, 0, g, 0, 0)),
                  pl.BlockSpec((None, None, None, n_cmp_pad, hd), lambda b, g, i: (b, 1, g, 0, 0)),
                  pl.BlockSpec((n_sel, n_cmp_pad), lambda b, g, i: (0, 0))],
        out_specs=(pl.BlockSpec((1, NSA_REP, tq, hd), lambda b, g, i: (b, g, i, 0)),
                   pl.BlockSpec((1, 1, tq, SEL_PAD), lambda b, g, i: (b, g, i, 0))),
        out_shape=(jax.ShapeDtypeStruct((n_batch, NSA_HEADS, t_len, hd), BF16),
                   jax.ShapeDtypeStruct((n_batch, NSA_KV_HEADS, t_len, SEL_PAD), BF16)),
        compiler_params=_params("arbitrary", "arbitrary", "arbitrary"),
        name="nsa_select",
    )(qn, cmp_kv, cmp_kv, overlap_t)


ATTN_Q_BLOCK = 256
ATTN_K_TILE = 1024


def _nsa_attn_kernel(qr_ref, oc_ref, selb_ref, gate_ref, ks_ref, vs_ref, kw_ref, vw_ref, e_ref, cb_ref, wb_ref, o_ref,
                     m_scr, acc_scr, *, tk, wk):
    rep, qb, hd = qr_ref.shape[1], qr_ref.shape[2], qr_ref.shape[3]
    nr = rep * qb
    q0 = pl.program_id(2) * qb
    q = qr_ref[0].reshape(nr, hd)
    selb = selb_ref[0, 0]
    nt = (((1,), (1,)), ((), ()))

    def sel_scores(t, extra_bias=None):
        k = ks_ref[0, 0, pl.ds(pl.multiple_of(t * tk, tk), tk), :]
        s = lax.dot_general(q, k, nt, preferred_element_type=F32)
        bias = jnp.dot(selb, e_ref[t], preferred_element_type=F32)
        if extra_bias is not None:
            bias = bias + extra_bias
        return (s.reshape(rep, qb, tk) + bias[None]).reshape(nr, tk)

    def sel_values(t):
        return vs_ref[0, 0, pl.ds(pl.multiple_of(t * tk, tk), tk), :]

    td = q0 // tk
    s = sel_scores(td, cb_ref[0])
    m = jnp.max(s, axis=-1, keepdims=True)
    m_scr[...] = m
    acc_scr[...] = jnp.dot(jnp.exp(s - m).astype(BF16), sel_values(td), preferred_element_type=F32)

    def body(t, carry):
        s = sel_scores(t)
        m_old = m_scr[...]
        m_new = jnp.maximum(m_old, jnp.max(s, axis=-1, keepdims=True))
        p = jnp.exp(s - m_new).astype(BF16)
        acc_scr[...] = jnp.exp(m_old - m_new) * acc_scr[...] + jnp.dot(p, sel_values(t), preferred_element_type=F32)
        m_scr[...] = m_new
        return carry

    kstart = pl.multiple_of(jnp.maximum(q0 - WINDOW, 0), qb)
    kw = kw_ref[0, 0, pl.ds(kstart, wk), :]
    s = lax.dot_general(q, kw, nt, preferred_element_type=F32)
    s = (s.reshape(rep, qb, wk) + wb_ref[0][None]).reshape(nr, wk)
    m = jnp.max(s, axis=-1, keepdims=True)
    accw = jnp.dot(jnp.exp(s - m).astype(BF16), vw_ref[0, 0, pl.ds(kstart, wk), :], preferred_element_type=F32)
    o_win = accw[:, 0:hd] / jnp.maximum(accw[:, hd:hd + 1], 1e-30)

    lax.fori_loop(0, td, body, 0)
    acc = acc_scr[...]
    o_sel = acc[:, 0:hd] / jnp.maximum(acc[:, hd:hd + 1], 1e-30)

    gates = gate_ref[0, 0]
    for r in range(rep):
        rows = slice(r * qb, (r + 1) * qb)
        o = (gates[:, 3 * r:3 * r + 1] * oc_ref[0, r].astype(F32)
             + gates[:, 3 * r + 1:3 * r + 2] * o_sel[rows]
             + gates[:, 3 * r + 2:3 * r + 3] * o_win[rows])
        o_ref[:, r * hd:(r + 1) * hd] = o.astype(BF16)


def nsa_attend(qr, o_cmp, selb, gates, ks, vs, kw, vw):
    n_batch, _, t_len, hd = qr.shape
    n_sel = selb.shape[3]
    qb = min(ATTN_Q_BLOCK, t_len)
    tk = min(ATTN_K_TILE, t_len)
    wk = min(WINDOW + qb, t_len)
    nq = t_len // qb
    n_tiles = t_len // tk
    key_blk = (jnp.arange(n_tiles)[:, None, None] * tk + jnp.arange(tk)[None, None, :]) // SEL_BLOCK
    e = (key_blk == jnp.arange(n_sel)[None, :, None]).astype(BF16)
    n_cv = tk // qb
    qi = jnp.arange(qb)[None, :, None]
    causal = jnp.where(jnp.arange(tk)[None, None, :] <= jnp.arange(n_cv)[:, None, None] * qb + qi, 0.0, NEG_BIG)
    n_wv = min(WINDOW // qb, nq - 1) + 1
    q0v = jnp.arange(n_wv)[:, None, None] * qb
    kpos = jnp.maximum(q0v - WINDOW, 0) + jnp.arange(wk)[None, None, :]
    qpos = q0v + qi
    window = jnp.where((kpos <= qpos) & (kpos > qpos - WINDOW), 0.0, NEG_BIG).astype(F32)
    kern = functools.partial(_nsa_attn_kernel, tk=tk, wk=wk)
    per_q = lambda nh, w: pl.BlockSpec((1, nh, qb, w), lambda b, g, i: (b, g, i, 0))
    full = lambda w: pl.BlockSpec((1, 1, t_len, w), lambda b, g, i: (b, g, 0, 0))
    return pl.pallas_call(
        kern,
        grid=(n_batch, NSA_KV_HEADS, nq),
        in_specs=[per_q(NSA_REP, hd), per_q(NSA_REP, hd), per_q(1, n_sel), per_q(1, 128),
                  full(hd), full(2 * hd), full(hd), full(2 * hd),
                  pl.BlockSpec((n_tiles, n_sel, tk), lambda b, g, i: (0, 0, 0)),
                  pl.BlockSpec((1, qb, tk), lambda b, g, i: (i % n_cv, 0, 0)),
                  pl.BlockSpec((1, qb, wk), lambda b, g, i: (jnp.minimum(i, n_wv - 1), 0, 0))],
        out_specs=pl.BlockSpec((qb, NSA_REP * hd), lambda b, g, i: (b * nq + i, g)),
        out_shape=jax.ShapeDtypeStruct((n_batch * t_len, NSA_HEADS * hd), BF16),
        scratch_shapes=[pltpu.VMEM((NSA_REP * qb, 1), F32), pltpu.VMEM((NSA_REP * qb, 2 * hd), F32)],
        compiler_params=_params("arbitrary", "arbitrary", "arbitrary"),
        name="nsa_attend",
    )(qr, o_cmp, selb, gates, ks, vs, kw, vw, e, causal.astype(F32), window)


def _gla_kernel(q_ref, k_ref, v_ref, r_ref, small_ref, w2_ref, b2_ref, ng_ref, s0_ref, y_ref, sf_ref, s_scr):
    tb = q_ref.shape[0]
    c, sub, dk, dv = GLA_CHUNK, GLA_SUB, GLA_DK, GLA_DV
    n_sub = c // sub
    t = pl.program_id(1)

    @pl.when(t == 0)
    def _():
        s_scr[...] = s0_ref[0]

    z = jnp.dot(small_ref[:, 0:GLA_RANK].astype(BF16), w2_ref[...].astype(BF16),
                preferred_element_type=F32) + b2_ref[...]
    g_all = (jnp.minimum(z, 0.0) - jnp.log1p(jnp.exp(-jnp.abs(z)))) / GLA_GATE_NORM
    ri = lax.broadcasted_iota(jnp.int32, (c, c), 0)
    ci = lax.broadcasted_iota(jnp.int32, (c, c), 1)
    tril = ri >= ci
    cum = tril.astype(F32)
    rsub = lax.broadcasted_iota(jnp.int32, (c, dk), 0) // sub
    eye = lax.broadcasted_iota(jnp.int32, (dk, dk), 0) == lax.broadcasted_iota(jnp.int32, (dk, dk), 1)
    for cc in range(tb // c):
        rows = slice(cc * c, (cc + 1) * c)
        b_all = jnp.dot(cum, g_all[rows], preferred_element_type=F32, precision=lax.Precision.HIGHEST)
        for h in range(GLA_HEADS):
            b = b_all[:, h * dk:(h + 1) * dk]
            qh = q_ref[rows, h * dk:(h + 1) * dk] * (dk ** -0.5)
            kh = k_ref[rows, h * dk:(h + 1) * dk]
            vh = v_ref[rows, h * dv:(h + 1) * dv]
            a_rows = []
            for i in range(n_sub):
                ref = b[sub * i - 1:sub * i, :] if i else jnp.zeros((1, dk), F32)
                rs = slice(sub * i, sub * (i + 1))
                qi = (qh[rs] * jnp.exp(b[rs] - ref)).astype(BF16)
                ki = jnp.where(rsub <= i, kh * jnp.exp(ref - b), 0.0).astype(BF16)
                a_rows.append(lax.dot_general(qi, ki, (((1,), (1,)), ((), ())), preferred_element_type=F32))
            a = jnp.where(tril, jnp.concatenate(a_rows, axis=0), 0.0)
            s_old = s_scr[h]
            o = jnp.dot(a.astype(BF16), vh.astype(BF16), preferred_element_type=F32)
            o += jnp.dot((qh * jnp.exp(b)).astype(BF16), s_old.astype(BF16), preferred_element_type=F32)
            b_last = b[c - 1:c, :]
            ke = (kh * jnp.exp(b_last - b)).astype(BF16)
            upd = lax.dot_general(ke, vh.astype(BF16), (((0,), (0,)), ((), ())), preferred_element_type=F32)
            decay = jnp.exp(jnp.sum(jnp.where(eye, jnp.broadcast_to(b_last, (dk, dk)), 0.0), axis=1, keepdims=True))
            s_scr[h] = decay * s_old + upd
            o = o * lax.rsqrt(jnp.mean(o * o, axis=-1, keepdims=True) + RMS_EPS)
            y = o * ng_ref[:, h * dv:(h + 1) * dv] * jax.nn.silu(r_ref[rows, h * dv:(h + 1) * dv])
            y_ref[rows, h * dv:(h + 1) * dv] = y.astype(BF16)

    @pl.when(t == pl.num_programs(1) - 1)
    def _():
        sf_ref[0] = s_scr[...]


def gla_mix(p, s0, w2, b2, norm_g, n_batch, t_len):
    tb = _pick_tile(t_len, (256, 128, 64))
    nt = t_len // tb
    row = lambda w, off: pl.BlockSpec((tb, w), lambda b, i: (b * nt + i, off // w))
    const = lambda shape: pl.BlockSpec(shape, lambda b, i: (0,) * len(shape))
    return pl.pallas_call(
        _gla_kernel,
        grid=(n_batch, nt),
        in_specs=[row(256, COL_GLA_Q), row(256, COL_GLA_K), row(512, COL_GLA_V), row(512, COL_GLA_R),
                  row(128, COL_SMALL), const((GLA_RANK, GLA_HEADS * GLA_DK)), const((1, GLA_HEADS * GLA_DK)),
                  const((1, GLA_WIDTH)),
                  pl.BlockSpec((1, GLA_HEADS, GLA_DK, GLA_DV), lambda b, i: (b, 0, 0, 0))],
        out_specs=(pl.BlockSpec((tb, GLA_WIDTH), lambda b, i: (b * nt + i, 0)),
                   pl.BlockSpec((1, GLA_HEADS, GLA_DK, GLA_DV), lambda b, i: (b, 0, 0, 0))),
        out_shape=(jax.ShapeDtypeStruct((n_batch * t_len, GLA_WIDTH), BF16),
                   jax.ShapeDtypeStruct((n_batch, GLA_HEADS, GLA_DK, GLA_DV), F32)),
        scratch_shapes=[pltpu.VMEM((GLA_HEADS, GLA_DK, GLA_DV), F32)],
        compiler_params=_params("arbitrary", "arbitrary"),
        name="gla_mix",
    )(p, p, p, p, p, w2, b2.reshape(1, -1), norm_g.reshape(1, -1), s0)


def _pool_kernel(u_ref, prev_ref, cnt_ref, w_ref, sc_ref, y_ref, halo):
    tb = u_ref.shape[0]
    gd = POOL_GROUP_DIM

    @pl.when(pl.program_id(1) == 0)
    def _():
        halo[...] = prev_ref[0]

    ext = jnp.concatenate([halo[...], u_ref[...]], axis=0)
    halo[...] = ext[tb:tb + POOL_MAX]
    for gi, w in enumerate(POOL_WINDOWS):
        x = ext[:, gi * gd:(gi + 1) * gd]
        s = x
        shift = 1
        while shift < w:
            s = s + pltpu.roll(s, shift, 0)
            shift *= 2
        pooled = s[POOL_MAX:] / cnt_ref[:, gi:gi + 1] - x[POOL_MAX:]
        y = jnp.dot(pooled.astype(BF16), w_ref[gi].astype(BF16), preferred_element_type=F32)
        y_ref[:, gi * gd:(gi + 1) * gd] = (y * sc_ref[:, gi * gd:(gi + 1) * gd]).astype(BF16)


def pool_mix(p, prev, pos0, w_pool, scale, n_batch, t_len):
    tb = _pick_tile(t_len, (512, 256, 128, 64, 32, 16))
    nt = t_len // tb
    pos = pos0 + jnp.arange(t_len, dtype=jnp.int32)
    cnt = jnp.stack([jnp.minimum(pos + 1, w).astype(F32) for w in POOL_WINDOWS], axis=1)
    cnt = jnp.pad(cnt, ((0, 0), (0, 128 - POOL_GROUPS)), constant_values=1.0)
    prev16 = jnp.pad(prev.astype(F32), ((0, 0), (1, 0), (0, 0)))
    return pl.pallas_call(
        _pool_kernel,
        grid=(n_batch, nt),
        in_specs=[pl.BlockSpec((tb, POOL_WIDTH), lambda b, i: (b * nt + i, COL_POOL // POOL_WIDTH)),
                  pl.BlockSpec((1, POOL_MAX, POOL_WIDTH), lambda b, i: (b, 0, 0)),
                  pl.BlockSpec((tb, 128), lambda b, i: (i, 0)),
                  pl.BlockSpec((POOL_GROUPS, POOL_GROUP_DIM, POOL_GROUP_DIM), lambda b, i: (0, 0, 0)),
                  pl.BlockSpec((1, POOL_WIDTH), lambda b, i: (0, 0))],
        out_specs=pl.BlockSpec((tb, POOL_WIDTH), lambda b, i: (b * nt + i, 0)),
        out_shape=jax.ShapeDtypeStruct((n_batch * t_len, POOL_WIDTH), BF16),
        scratch_shapes=[pltpu.VMEM((POOL_MAX, POOL_WIDTH), F32)],
        compiler_params=_params("arbitrary", "arbitrary"),
        name="pool_mix",
    )(p, prev16, cnt, w_pool, scale.reshape(1, -1))


def _layer_norm_rows(x, g, b):
    xc = x - jnp.mean(x, axis=-1, keepdims=True)
    var = jnp.mean(xc * xc, axis=-1, keepdims=True)
    return xc * lax.rsqrt(var + LN_EPS) * g + b


def _outproj_kernel(x_ref, yg_ref, yn_ref, yp_ref, w_ref, g_ref, b_ref, *rest):
    h = jnp.dot(yg_ref[...], w_ref[0:GLA_WIDTH, :], preferred_element_type=F32)
    h += jnp.dot(yn_ref[...], w_ref[GLA_WIDTH:GLA_WIDTH + NSA_WIDTH, :], preferred_element_type=F32)
    h += jnp.dot(yp_ref[...], w_ref[GLA_WIDTH + NSA_WIDTH:, :], preferred_element_type=F32)
    x1 = _layer_norm_rows(ALPHA * x_ref[...] + h, g_ref[...], b_ref[...])
    if len(rest) == 1:
        rest[0][...] = x1
    else:
        rh_ref, rl_ref, o_ref, lg_ref = rest
        o_ref[...] = x1
        xh = x1.astype(BF16)
        xl = (x1 - xh.astype(F32)).astype(BF16)
        lg_ref[...] = (jnp.dot(xh, rh_ref[...], preferred_element_type=F32)
                       + jnp.dot(xl, rh_ref[...], preferred_element_type=F32)
                       + jnp.dot(xh, rl_ref[...], preferred_element_type=F32))


def router_split(router):
    r = jnp.pad(router, ((0, 0), (0, 128 - router.shape[1])))
    hi = r.astype(BF16)
    return hi, (r - hi.astype(F32)).astype(BF16)


def outproj_ln(x, y_gla, y_nsa, y_pool, w_out_bf16, g, b, router=None):
    n, d = x.shape
    tm = _pick_tile(n, (512, 256, 128, 64, 32, 16, 8))
    row = lambda w: pl.BlockSpec((tm, w), lambda i: (i, 0))
    const = lambda r, c: pl.BlockSpec((r, c), lambda i: (0, 0))
    in_specs = [row(d), row(GLA_WIDTH), row(NSA_WIDTH), row(POOL_WIDTH), const(d, d), const(1, d), const(1, d)]
    args = [x, y_gla, y_nsa, y_pool, w_out_bf16, g.reshape(1, -1), b.reshape(1, -1)]
    out_specs, out_shape = row(d), jax.ShapeDtypeStruct((n, d), F32)
    if router is not None:
        in_specs += [const(d, 128), const(d, 128)]
        args += list(router_split(router))
        out_specs, out_shape = (out_specs, row(128)), (out_shape, jax.ShapeDtypeStruct((n, 128), F32))
    return pl.pallas_call(
        _outproj_kernel,
        grid=(n // tm,),
        in_specs=in_specs,
        out_specs=out_specs,
        out_shape=out_shape,
        compiler_params=_params("arbitrary"),
        name="outproj_ln",
    )(*args)


MOE_TM = 512
MOE_TM_DOWN = 512
MOE_VMEM_BYTES = 60 * 1024 * 1024
MOE_TF = 1024
MOE_TN = 512
ROUTE_TM = 512
PERMUTE_CHUNK = 1024


def _route_kernel(lg_ref, ii_ref, gf_ref, cnt_ref, carry, *, n_valid):
    tm = lg_ref.shape[0]
    i = pl.program_id(0)

    @pl.when(i == 0)
    def _():
        carry[...] = jnp.zeros_like(carry)

    lane = lax.broadcasted_iota(jnp.int32, (tm, 128), 1)
    valid = (i * tm + lax.broadcasted_iota(jnp.int32, (tm, 128), 0)) < n_valid
    lg = jnp.where(lane < N_EXPERTS, lg_ref[...], -jnp.inf)
    m1 = jnp.max(lg, axis=-1, keepdims=True)
    i1 = jnp.min(jnp.where(lg == m1, lane, 128), axis=-1, keepdims=True)
    lg2 = jnp.where(lane == i1, -jnp.inf, lg)
    m2 = jnp.max(lg2, axis=-1, keepdims=True)
    i2 = jnp.min(jnp.where(lg2 == m2, lane, 128), axis=-1, keepdims=True)
    t = jnp.exp(m2 - m1)
    g1 = 1.0 / (1.0 + t)
    g2 = t / (1.0 + t)
    oh1 = jnp.where((lane == i1) & valid, 1.0, 0.0)
    oh2 = jnp.where((lane == i2) & valid, 1.0, 0.0)
    cnt = oh1 + oh2
    strict = (lax.broadcasted_iota(jnp.int32, (tm, tm), 0) > lax.broadcasted_iota(jnp.int32, (tm, tm), 1))
    before = jnp.dot(strict.astype(BF16), cnt.astype(BF16), preferred_element_type=F32) + carry[...]
    r1 = jnp.sum(before * oh1, axis=-1, keepdims=True).astype(jnp.int32)
    r2 = jnp.sum(before * oh2, axis=-1, keepdims=True).astype(jnp.int32)
    carry[...] += jnp.sum(cnt, axis=0, keepdims=True)
    ii_ref[...] = jnp.where(lane == 0, i1, jnp.where(lane == 1, i2, jnp.where(lane == 2, r1, r2)))
    gf_ref[...] = jnp.where(lane == 0, g1, g2)
    cnt_ref[...] = carry[...]


def moe_route(logits, n_valid):
    npad = logits.shape[0]
    tm = ROUTE_TM
    row = pl.BlockSpec((tm, 128), lambda i: (i, 0))
    info, gates, counts = pl.pallas_call(
        functools.partial(_route_kernel, n_valid=n_valid),
        grid=(npad // tm,),
        in_specs=[row],
        out_specs=(row, row, pl.BlockSpec((1, 128), lambda i: (0, 0))),
        out_shape=(jax.ShapeDtypeStruct((npad, 128), jnp.int32), jax.ShapeDtypeStruct((npad, 128), F32),
                   jax.ShapeDtypeStruct((1, 128), F32)),
        scratch_shapes=[pltpu.VMEM((1, 128), F32)],
        compiler_params=_params("arbitrary"),
        name="moe_route",
    )(logits)
    return info[:, 0:2], info[:, 2:4], gates, counts[0, :N_EXPERTS].astype(jnp.int32)


SLAB = (16, 128)


def _slabify_kernel(x_ref, *rest):
    o_ref = rest[-1]

    def slab_rows(src_ref, n_rows):
        for c in range(SLAB[0]):
            o_ref[0:n_rows, c, :] = src_ref[:, c * SLAB[1]:(c + 1) * SLAB[1]]

    if len(rest) == 1:
        slab_rows(x_ref, x_ref.shape[0])
        return
    t_ref = rest[0]
    last = pl.program_id(0) == pl.num_programs(0) - 1

    @pl.when(jnp.logical_not(last))
    def _():
        slab_rows(x_ref, x_ref.shape[0])

    @pl.when(last)
    def _():
        o_ref[...] = jnp.zeros_like(o_ref)
        slab_rows(t_ref, t_ref.shape[0])


def slabify(x, tail=None):
    n, d = x.shape
    tm = _pick_tile(n, (512, 256, 128, 64, 32, 16, 8))
    nt = n // tm
    in_specs = [pl.BlockSpec((tm, d), lambda i: (jnp.minimum(i, nt - 1), 0))]
    args = [x]
    if tail is not None:
        assert tail.shape[0] <= tm
        in_specs.append(pl.BlockSpec(tail.shape, lambda i: (0, 0)))
        args.append(tail)
    steps = nt + (tail is not None)
    return pl.pallas_call(
        _slabify_kernel,
        grid=(steps,),
        in_specs=in_specs,
        out_specs=pl.BlockSpec((tm,) + SLAB, lambda i: (i, 0, 0)),
        out_shape=jax.ShapeDtypeStruct((steps * tm,) + SLAB, x.dtype),
        compiler_params=_params("arbitrary"),
        name="slabify",
    )(*args)


def _unslab(ref):
    return jnp.concatenate([ref[:, c, :] for c in range(SLAB[0])], axis=1)


GATHER_UNROLL = 8


def _gather_slabs_kernel(idx_ref, src_ref, o_ref, sem, *stage):
    ch = idx_ref.shape[2]

    def row_copy(r):
        if stage:
            dst = stage[0].at[r // 8, :, r % 8, :]
        else:
            dst = o_ref.at[r]
        return pltpu.make_async_copy(src_ref.at[idx_ref[0, 0, r]], dst, sem)

    def issue(t, c):
        for u in range(GATHER_UNROLL):
            row_copy(t * GATHER_UNROLL + u).start(priority=u % 2)
        return c

    lax.fori_loop(0, ch // GATHER_UNROLL, issue, 0)

    def drain(t, c):
        for u in range(GATHER_UNROLL):
            row_copy(t * GATHER_UNROLL + u).wait()
        return c

    lax.fori_loop(0, ch // GATHER_UNROLL, drain, 0)
    if stage:
        rows = jnp.concatenate([stage[0][:, c].reshape(ch, SLAB[1]) for c in range(SLAB[0])], axis=1)
        o_ref[...] = rows.astype(o_ref.dtype)


def gather_slabs(src, idx, as_bf16_rows=False):
    n = idx.shape[0]
    ch = PERMUTE_CHUNK
    scratch = [pltpu.SemaphoreType.DMA(())]
    if as_bf16_rows:
        d = SLAB[0] * SLAB[1]
        out_specs, out_shape = pl.BlockSpec((ch, d), lambda i: (i, 0)), jax.ShapeDtypeStruct((n, d), BF16)
        scratch.append(pltpu.VMEM((ch // 8, SLAB[0], 8, SLAB[1]), src.dtype))
    else:
        out_specs = pl.BlockSpec((ch,) + SLAB, lambda i: (i, 0, 0))
        out_shape = jax.ShapeDtypeStruct((n,) + SLAB, src.dtype)
    return pl.pallas_call(
        _gather_slabs_kernel,
        grid=(n // ch,),
        in_specs=[pl.BlockSpec((1, 1, ch), lambda i: (i, 0, 0), memory_space=pltpu.SMEM),
                  pl.BlockSpec(memory_space=pl.ANY)],
        out_specs=out_specs,
        out_shape=out_shape,
        scratch_shapes=scratch,
        compiler_params=_params("arbitrary"),
        name="gather_slabs",
    )(idx.reshape(n // ch, 1, ch), src)


def _moe_up_kernel(te_ref, tfirst_ref, tused_ref, x_ref, wg_ref, wu_ref, h_ref, wgb, wub):
    i = pl.program_id(1)

    @pl.when((i == 0) | (tfirst_ref[i] == 1))
    def _():
        wgb[...] = wg_ref[...].astype(BF16)
        wub[...] = wu_ref[...].astype(BF16)

    @pl.when(tused_ref[i] == 1)
    def _():
        xb = x_ref[...]
        gate = jnp.dot(xb, wgb[...], preferred_element_type=F32)
        up = jnp.dot(xb, wub[...], preferred_element_type=F32)
        h_ref[...] = (jax.nn.silu(gate) * up).astype(BF16)

    @pl.when(tused_ref[i] == 0)
    def _():
        h_ref[...] = jnp.zeros_like(h_ref)


def _moe_down_kernel(te_ref, tfirst_ref, tused_ref, h_ref, wd_ref, y_ref, wdb):
    i = pl.program_id(1)

    @pl.when((i == 0) | (tfirst_ref[i] == 1))
    def _():
        wdb[...] = wd_ref[...].astype(BF16)

    @pl.when(tused_ref[i] == 1)
    def _():
        y_ref[...] = jnp.dot(h_ref[...], wdb[...], preferred_element_type=F32)

    @pl.when(tused_ref[i] == 0)
    def _():
        y_ref[...] = jnp.zeros_like(y_ref)


def _tile_meta(tile_e, tile_used, split):
    te = jnp.repeat(tile_e, split)
    first = jnp.concatenate([jnp.ones((1,), jnp.int32), (te[1:] != te[:-1]).astype(jnp.int32)])
    return te, first, jnp.repeat(tile_used, split)


def moe_experts(xs, tile_e, tile_used, wg, wu, wd):
    r, d = xs.shape
    d_ff = wg.shape[2]
    tm, tf, tn = MOE_TM, MOE_TF, MOE_TN
    h = pl.pallas_call(
        _moe_up_kernel,
        grid_spec=pltpu.PrefetchScalarGridSpec(
            num_scalar_prefetch=3,
            grid=(d_ff // tf, r // tm),
            in_specs=[pl.BlockSpec((tm, d), lambda j, i, te, t1, tu: (i, 0)),
                      pl.BlockSpec((None, d, tf), lambda j, i, te, t1, tu: (te[i], 0, j)),
                      pl.BlockSpec((None, d, tf), lambda j, i, te, t1, tu: (te[i], 0, j))],
            out_specs=pl.BlockSpec((tm, tf), lambda j, i, te, t1, tu: (i, j)),
            scratch_shapes=[pltpu.VMEM((d, tf), BF16), pltpu.VMEM((d, tf), BF16)]),
        out_shape=jax.ShapeDtypeStruct((r, d_ff), BF16),
        compiler_params=_params("arbitrary", "arbitrary", vmem_limit_bytes=MOE_VMEM_BYTES),
        name="moe_up",
    )(*_tile_meta(tile_e, tile_used, 1), xs, wg, wu)
    tmd = MOE_TM_DOWN
    return pl.pallas_call(
        _moe_down_kernel,
        grid_spec=pltpu.PrefetchScalarGridSpec(
            num_scalar_prefetch=3,
            grid=(d // tn, r // tmd),
            in_specs=[pl.BlockSpec((tmd, d_ff), lambda j, i, te, t1, tu: (i, 0)),
                      pl.BlockSpec((None, d_ff, tn), lambda j, i, te, t1, tu: (te[i], 0, j))],
            out_specs=pl.BlockSpec((tmd, tn), lambda j, i, te, t1, tu: (i, j)),
            scratch_shapes=[pltpu.VMEM((d_ff, tn), BF16)]),
        out_shape=jax.ShapeDtypeStruct((r, d), F32),
        compiler_params=_params("arbitrary", "arbitrary", vmem_limit_bytes=MOE_VMEM_BYTES),
        name="moe_down",
    )(*_tile_meta(tile_e, tile_used, tm // tmd), h, wd)


def _moe_combine_kernel(x_ref, y0_ref, y1_ref, gt_ref, g_ref, b_ref, o_ref):
    gt = gt_ref[...]
    y = gt[:, 0:1] * _unslab(y0_ref) + gt[:, 1:2] * _unslab(y1_ref)
    o_ref[...] = _layer_norm_rows(ALPHA * x_ref[...] + y, g_ref[...], b_ref[...])


def moe_combine_ln(x, yg, gates, row0, n_tok_pad, g, b):
    n, d = x.shape
    tm = _pick_tile(n, (512, 256, 128, 64, 32, 16, 8))
    o0, o1 = row0 // tm, (n_tok_pad + row0) // tm
    return pl.pallas_call(
        _moe_combine_kernel,
        grid=(n // tm,),
        in_specs=[pl.BlockSpec((tm, d), lambda i: (i, 0)),
                  pl.BlockSpec((tm,) + SLAB, lambda i: (o0 + i, 0, 0)),
                  pl.BlockSpec((tm,) + SLAB, lambda i: (o1 + i, 0, 0)),
                  pl.BlockSpec((tm, 128), lambda i: (o0 + i, 0)),
                  pl.BlockSpec((1, d), lambda i: (0, 0)), pl.BlockSpec((1, d), lambda i: (0, 0))],
        out_specs=pl.BlockSpec((tm, d), lambda i: (i, 0)),
        out_shape=jax.ShapeDtypeStruct((n, d), F32),
        compiler_params=_params("arbitrary"),
        name="moe_combine_ln",
    )(x, yg, yg, gates, g.reshape(1, -1), b.reshape(1, -1))


def moe_ln(x_groups, logit_groups, wg, wu, wd, g, b):
    d = x_groups[0].shape[1]
    n_tok = sum(x.shape[0] for x in x_groups)
    n_tok_pad = -(-n_tok // PERMUTE_CHUNK) * PERMUTE_CHUNK
    n_tok_pad = -(-n_tok_pad // ROUTE_TM) * ROUTE_TM
    logits = jnp.concatenate(logit_groups + [jnp.zeros((n_tok_pad - n_tok, 128), F32)], axis=0)
    experts, ranks, gates, counts = moe_route(logits, n_tok)
    tm = MOE_TM
    n_tiles = -(-(n_tok * TOP_K + N_EXPERTS * (tm - 1)) // tm)
    n_tiles = -(-n_tiles * tm // PERMUTE_CHUNK) * PERMUTE_CHUNK // tm
    padded = (counts + tm - 1) // tm * tm
    pad_end = jnp.cumsum(padded)
    pad_start = pad_end - padded
    valid = (jnp.arange(n_tok_pad) < n_tok)[:, None]
    dest = jnp.where(valid, pad_start[experts] + ranks, 0)
    tok = jnp.broadcast_to(jnp.arange(n_tok_pad, dtype=jnp.int32)[:, None], dest.shape)
    row_tok = jnp.zeros((n_tiles * tm,), jnp.int32).at[jnp.where(valid, dest, n_tiles * tm).reshape(-1)].set(
        tok.reshape(-1), mode='drop')
    tile_start = jnp.arange(n_tiles, dtype=jnp.int32) * tm
    tile_e = jnp.minimum(jnp.searchsorted(pad_end, tile_start, side='right'), N_EXPERTS - 1).astype(jnp.int32)
    tile_used = (tile_start < pad_end[-1]).astype(jnp.int32)
    assert len(x_groups) == 2
    xs = gather_slabs(slabify(x_groups[0], x_groups[1]), row_tok, as_bf16_rows=True)
    ys = slabify(moe_experts(xs, tile_e, tile_used, wg, wu, wd))
    yg = gather_slabs(ys, jnp.concatenate([dest[:, 0], dest[:, 1]]).astype(jnp.int32))
    outs, row0 = [], 0
    for x in x_groups:
        outs.append(moe_combine_ln(x, yg, gates, row0, n_tok_pad, g, b))
        row0 += x.shape[0]
    return outs


def _ffn_kernel(x_ref, wg_ref, wu_ref, wd_ref, g_ref, b_ref, o_ref, xb_ref, *, d_ff):
    j = pl.program_id(1)
    tf = wg_ref.shape[1]

    @pl.when(j == 0)
    def _():
        xb_ref[...] = x_ref[...].astype(BF16)
        o_ref[...] = jnp.zeros_like(o_ref)

    xb = xb_ref[...]
    gate = jnp.dot(xb, wg_ref[...], preferred_element_type=F32)
    up = jnp.dot(xb, wu_ref[...], preferred_element_type=F32)
    col = j * tf + lax.broadcasted_iota(jnp.int32, (1, tf), 1)
    a = jnp.where(col < d_ff, jax.nn.silu(gate) * up, 0.0).astype(BF16)
    rowi = j * tf + lax.broadcasted_iota(jnp.int32, (tf, 1), 0)
    wd = jnp.where(rowi < d_ff, wd_ref[...], jnp.zeros((), BF16))
    o_ref[...] += jnp.dot(a, wd, preferred_element_type=F32)

    @pl.when(j == pl.num_programs(1) - 1)
    def _():
        o_ref[...] = _layer_norm_rows(ALPHA * x_ref[...] + o_ref[...], g_ref[...], b_ref[...])


def ffn_ln(x, wg, wu, wd, g, b):
    n, d = x.shape
    d_ff = wg.shape[1]
    tm = _pick_tile(n, (512, 256, 128, 64, 32, 16, 8))
    tf = 512
    kern = functools.partial(_ffn_kernel, d_ff=d_ff)
    return pl.pallas_call(
        kern,
        grid=(n // tm, pl.cdiv(d_ff, tf)),
        in_specs=[pl.BlockSpec((tm, d), lambda i, j: (i, 0)),
                  pl.BlockSpec((d, tf), lambda i, j: (0, j)), pl.BlockSpec((d, tf), lambda i, j: (0, j)),
                  pl.BlockSpec((tf, d), lambda i, j: (j, 0)),
                  pl.BlockSpec((1, d), lambda i, j: (0, 0)), pl.BlockSpec((1, d), lambda i, j: (0, 0))],
        out_specs=pl.BlockSpec((tm, d), lambda i, j: (i, 0)),
        out_shape=jax.ShapeDtypeStruct((n, d), F32),
        scratch_shapes=[pltpu.VMEM((tm, d), BF16)],
        compiler_params=_params("arbitrary", "arbitrary"),
        name="ffn_ln",
    )(x, wg, wu, wd, g.reshape(1, -1), b.reshape(1, -1))


CMP_PAGES = 32
SEL_PAD_DEC = 384


def _cmp_paged_kernel(pt_ref, cache_ref, pe_ref, w1_ref, w2_ref, o_ref, buf, hlo, hhi, sems, *, layer, n_pages, page):
    b, ch = pl.program_id(0), pl.program_id(1)
    n_ch = pl.num_programs(1)
    hd = HEAD_DIM
    nh = CMP_PAGES * page // CMP_STRIDE
    step = b * n_ch + ch

    def copies(s, slot):
        first = s * CMP_PAGES
        return [pltpu.make_async_copy(
            cache_ref.at[layer, pt_ref[first + pg], :, c // NSA_KV_HEADS, c % NSA_KV_HEADS, :],
            buf.at[slot, c, pl.ds(pg * page, page), :], sems.at[slot])
            for pg in range(CMP_PAGES) for c in range(2 * NSA_KV_HEADS)]

    def for_slot(s, fn):
        for slot in range(2):
            @pl.when(s % 2 == slot)
            def _():
                fn(slot)

    @pl.when(step == 0)
    def _():
        for cp in copies(0, 0):
            cp.start()

    @pl.when(step + 1 < pl.num_programs(0) * n_ch)
    def _():
        for_slot(step + 1, lambda slot: [cp.start() for cp in copies(step + 1, slot)])

    for_slot(step, lambda slot: [cp.wait() for cp in copies(step, slot)])
    cur = step % 2

    for c in range(2 * NSA_KV_HEADS):
        kd, g = c // NSA_KV_HEADS, c % NSA_KV_HEADS
        xs = [buf[cur, c, pl.ds(j, nh, stride=CMP_STRIDE), :] for j in range(CMP_STRIDE)]
        lo = jnp.concatenate([(xs[j] + pe_ref[kd, j:j + 1, :]).astype(BF16) for j in range(CMP_STRIDE)], axis=1)
        hi = jnp.concatenate([(xs[j] + pe_ref[kd, CMP_STRIDE + j:CMP_STRIDE + j + 1, :]).astype(BF16)
                              for j in range(CMP_STRIDE)], axis=1)
        w_lo = w1_ref[kd, 0:CMP_STRIDE].reshape(CMP_STRIDE * hd, CMP_HIDDEN).astype(BF16)
        w_hi = w1_ref[kd, CMP_STRIDE:CMP_LEN].reshape(CMP_STRIDE * hd, CMP_HIDDEN).astype(BF16)
        rows = pl.ds(pl.multiple_of(ch * nh, nh), nh)
        hlo[c, rows, :] = jnp.dot(lo, w_lo, preferred_element_type=F32)
        hhi[c, rows, :] = jnp.dot(hi, w_hi, preferred_element_type=F32)

    @pl.when(ch == pl.num_programs(1) - 1)
    def _():
        n_all = hlo.shape[1]
        for c in range(2 * NSA_KV_HEADS):
            kd, g = c // NSA_KV_HEADS, c % NSA_KV_HEADS
            h = jax.nn.gelu(hlo[c] + pltpu.roll(hhi[c], n_all - 1, 0))
            o_ref[kd, g] = jnp.dot(h.astype(BF16), w2_ref[kd].astype(BF16), preferred_element_type=F32).astype(BF16)


def nsa_compress_paged(cache, layer, page_table, cmp_pos, cmp_w1, cmp_w2):
    page = cache.shape[2]
    n_batch, n_pages = page_table.shape
    hd = HEAD_DIM
    n_all = n_pages * page // CMP_STRIDE
    kern = functools.partial(_cmp_paged_kernel, layer=layer, n_pages=n_pages, page=page)
    const = lambda shape: pl.BlockSpec(shape, lambda b, c, pt: (0,) * len(shape))
    return pl.pallas_call(
        kern,
        grid_spec=pltpu.PrefetchScalarGridSpec(
            num_scalar_prefetch=1,
            grid=(n_batch, n_pages // CMP_PAGES),
            in_specs=[pl.BlockSpec(memory_space=pl.ANY), const((2, CMP_LEN, hd)),
                      const((2, CMP_LEN, hd, CMP_HIDDEN)), const((2, CMP_HIDDEN, hd))],
            out_specs=pl.BlockSpec((None, 2, NSA_KV_HEADS, n_all, hd), lambda b, c, pt: (b, 0, 0, 0, 0)),
            scratch_shapes=[pltpu.VMEM((2, 2 * NSA_KV_HEADS, CMP_PAGES * page, hd), F32),
                            pltpu.VMEM((2 * NSA_KV_HEADS, n_all, CMP_HIDDEN), F32),
                            pltpu.VMEM((2 * NSA_KV_HEADS, n_all, CMP_HIDDEN), F32),
                            pltpu.SemaphoreType.DMA((2,))]),
        out_shape=jax.ShapeDtypeStruct((n_batch, 2, NSA_KV_HEADS, n_all, hd), BF16),
        compiler_params=_params("arbitrary", "arbitrary"),
        name="nsa_compress_paged",
    )(page_table.reshape(-1).astype(jnp.int32), cache, cmp_pos, cmp_w1, cmp_w2)


def _sel_decode_kernel(qn_ref, kc_ref, vc_ref, ovt_ref, oc_ref, idx_ref, v_scr, psum_scr, *, n_cmp, n_sel, n_top,
                       q_pos):
    b = pl.program_id(0)
    n_cmp_pad = kc_ref.shape[2]
    nsp = ovt_ref.shape[0]
    n_rows = qn_ref.shape[0]
    col = lax.broadcasted_iota(jnp.int32, (n_rows, n_cmp_pad), 1)
    mask = (col * CMP_STRIDE + (CMP_LEN - 1) <= q_pos) & (col < n_cmp)
    psums = []
    for g in range(NSA_KV_HEADS):
        own = slice(g * NSA_REP, (g + 1) * NSA_REP)
        s = lax.dot_general(qn_ref[...], kc_ref[0, g], (((1,), (1,)), ((), ())), preferred_element_type=F32)
        s = jnp.where(mask, s, -jnp.inf)
        m = jnp.max(s, axis=-1, keepdims=True)
        m = jnp.where(m > -jnp.inf, m, 0.0)
        p = jnp.where(mask, jnp.exp(s - m), 0.0)
        p = p / jnp.maximum(jnp.sum(p, axis=-1, keepdims=True), 1e-30)
        oc = jnp.dot(p.astype(BF16), vc_ref[0, g], preferred_element_type=F32)
        oc_ref[own, :] = oc[own]
        psum_scr[pl.ds(b * NSA_KV_HEADS + g, 1), :] = jnp.sum(p[own], axis=0, keepdims=True)

    @pl.when(b == pl.num_programs(0) - 1)
    def _():
        n_col = psum_scr.shape[0]
        imp = lax.dot_general(ovt_ref[...], psum_scr[...], (((1,), (1,)), ((), ())), preferred_element_type=F32,
                              precision=lax.Precision.HIGHEST)
        blk = lax.broadcasted_iota(jnp.int32, (nsp, n_col), 0)
        cur = q_pos // SEL_BLOCK
        forced = (blk == 0) | (blk == cur) | (blk == cur - 1)
        v = jnp.where((blk <= cur) & (blk < n_sel), jnp.where(forced, jnp.inf, imp), -jnp.inf)
        v_scr[...] = v

        def count(i, rank):
            vi = v_scr[pl.ds(i, 1), :]
            ahead = (vi > v) | ((vi == v) & (blk > i))
            return rank + ahead.astype(jnp.int32)

        rank = lax.fori_loop(0, n_sel, count, jnp.zeros((nsp, n_col), jnp.int32))
        chosen = (rank < n_top) & (v > -jnp.inf)
        blk_f = blk.astype(F32)
        rows = [jnp.sum(jnp.where(chosen & (rank == t), blk_f, 0.0), axis=0, keepdims=True) for t in range(n_top)]
        idx_ref[...] = jnp.concatenate(rows, axis=0).astype(jnp.int32)


def nsa_select_decode(qn, cmp_kv, n_cmp, n_sel, q_pos):
    n_batch, n_heads, hd = qn.shape
    n_cmp_pad = cmp_kv.shape[3]
    nsp = SEL_PAD_DEC
    n_top = min(SEL_TOPN, n_sel)
    ci = jnp.arange(n_cmp_pad)[None, :]
    sj = jnp.arange(nsp)[:, None]
    overlap_t = ((ci * CMP_STRIDE <= sj * SEL_BLOCK + SEL_BLOCK - 1) &
                 (ci * CMP_STRIDE + CMP_LEN - 1 >= sj * SEL_BLOCK) & (ci < n_cmp) & (sj < n_sel)).astype(F32)
    kern = functools.partial(_sel_decode_kernel, n_cmp=n_cmp, n_sel=n_sel, n_top=n_top, q_pos=q_pos)
    kv = lambda kd: pl.BlockSpec((None, 1, NSA_KV_HEADS, n_cmp_pad, hd), lambda b: (b, kd, 0, 0, 0))
    n_col = n_batch * NSA_KV_HEADS
    o_cmp, idx = pl.pallas_call(
        kern,
        grid=(n_batch,),
        in_specs=[pl.BlockSpec((None, 2 * n_heads, hd), lambda b: (b, 0, 0)), kv(0), kv(1),
                  pl.BlockSpec((nsp, n_cmp_pad), lambda b: (0, 0))],
        out_specs=(pl.BlockSpec((None, n_heads, hd), lambda b: (b, 0, 0)),
                   pl.BlockSpec((n_top, n_col), lambda b: (0, 0))),
        out_shape=(jax.ShapeDtypeStruct((n_batch, n_heads, hd), F32),
                   jax.ShapeDtypeStruct((n_top, n_col), jnp.int32)),
        scratch_shapes=[pltpu.VMEM((nsp, n_col), F32), pltpu.VMEM((n_col, n_cmp_pad), F32)],
        compiler_params=_params("arbitrary"),
        name="nsa_select_decode",
    )(jnp.pad(qn, ((0, 0), (0, n_heads), (0, 0))), cmp_kv, cmp_kv, overlap_t)
    return o_cmp, idx.T.reshape(n_batch, NSA_KV_HEADS, n_top)


def _attn_decode_kernel(pt_ref, sel_ref, q_ref, cache_ref, new_ref, kw_ref, vw_ref, wnew_ref, oc_ref, gate_ref,
                        o_ref, kbuf, vbuf, sem, *, layer, n_pages, per_page, n_top):
    b = pl.program_id(0)
    G = NSA_KV_HEADS
    n_past_blocks = n_pages * per_page
    nt = (((1,), (1,)), ((), ()))

    def block_id(g, slot):
        return sel_ref[(b * G + g) * n_top + slot]

    def copies():
        out = []
        for g in range(G):
            for slot in range(n_top):
                j = jnp.minimum(block_id(g, slot), n_past_blocks - 1)
                rows = pl.ds((j % per_page) * SEL_BLOCK, SEL_BLOCK)
                page = pt_ref[b * n_pages + j // per_page]
                for kind, buf in ((2, kbuf), (3, vbuf)):
                    out.append(pltpu.make_async_copy(cache_ref.at[layer, page, rows, kind, g, :],
                                                     buf.at[g, pl.ds(slot * SEL_BLOCK, SEL_BLOCK), :], sem))
        return out

    for cp in copies():
        cp.start()
    for cp in copies():
        cp.wait()

    for g in range(G):
        qb = q_ref[g]
        q = qb.astype(F32)
        n_rows = qb.shape[0]
        s = lax.dot_general(qb, kbuf[g].astype(BF16), nt, preferred_element_type=F32)
        slot_of = lax.broadcasted_iota(jnp.int32, s.shape, 1) // SEL_BLOCK
        for slot in range(n_top):
            s = jnp.where((slot_of == slot) & (block_id(g, slot) >= n_past_blocks), NEG_BIG, s)
        k_new = new_ref[2 * G + g:2 * G + g + 1, :]
        v_new = new_ref[3 * G + g:3 * G + g + 1, :]
        s_new = jnp.sum(q * k_new, axis=-1, keepdims=True)
        m = jnp.maximum(jnp.max(s, axis=-1, keepdims=True), s_new)
        p = jnp.exp(s - m)
        p_new = jnp.exp(s_new - m)
        l = jnp.sum(p, axis=-1, keepdims=True) + p_new
        o_sel = (jnp.dot(p.astype(BF16), vbuf[g].astype(BF16), preferred_element_type=F32) + p_new * v_new) / l

        n_buf = kw_ref.shape[0]
        s = lax.dot_general(qb, kw_ref[:, g, :].astype(BF16), nt, preferred_element_type=F32)
        keep = lax.broadcasted_iota(jnp.int32, s.shape, 1) > n_buf - WINDOW
        s = jnp.where(keep, s, NEG_BIG)
        kw_new = wnew_ref[g:g + 1, :]
        vw_new = wnew_ref[G + g:G + g + 1, :]
        s_new = jnp.sum(q * kw_new, axis=-1, keepdims=True)
        m = jnp.maximum(jnp.max(s, axis=-1, keepdims=True), s_new)
        p = jnp.exp(s - m)
        p_new = jnp.exp(s_new - m)
        l = jnp.sum(p, axis=-1, keepdims=True) + p_new
        o_win = (jnp.dot(p.astype(BF16), vw_ref[:, g, :].astype(BF16), preferred_element_type=F32)
                 + p_new * vw_new) / l
        gates = jnp.broadcast_to(gate_ref[g:g + 1, :], (n_rows, 128))
        lane = lax.broadcasted_iota(jnp.int32, (n_rows, 128), 1)
        head = lax.broadcasted_iota(jnp.int32, (n_rows, 128), 0)
        pick = lambda c: jnp.sum(jnp.where(lane == head * 3 + c, gates, 0.0), axis=-1, keepdims=True)
        o_ref[g] = (pick(0) * oc_ref[g] + pick(1) * o_sel + pick(2) * o_win).astype(o_ref.dtype)


def nsa_attend_decode(qr, o_cmp, sel_idx, gates, cache, layer, page_table, new_rows, state_win, new_win):
    n_batch, n_heads, hd = qr.shape
    page = cache.shape[2]
    n_pages = page_table.shape[1]
    n_top = sel_idx.shape[2]
    per_page = page // SEL_BLOCK
    n_buf = state_win.shape[2]
    G, R = NSA_KV_HEADS, NSA_REP
    rp = 16
    pad_heads = lambda a: jnp.pad(a.reshape(n_batch, G, R, hd), ((0, 0), (0, 0), (0, rp - R), (0, 0)))
    per_bg = lambda: pl.BlockSpec((None, G, rp, hd), lambda b, pt, sel: (b, 0, 0, 0))
    per_b = lambda rows: pl.BlockSpec((None, rows, hd), lambda b, pt, sel: (b, 0, 0))
    win_spec = lambda kv: pl.BlockSpec((None, None, n_buf, None, G, hd), lambda b, pt, sel: (layer, b, 0, kv, 0, 0))
    kern = functools.partial(_attn_decode_kernel, layer=layer, n_pages=n_pages, per_page=per_page, n_top=n_top)
    out = pl.pallas_call(
        kern,
        grid_spec=pltpu.PrefetchScalarGridSpec(
            num_scalar_prefetch=2,
            grid=(n_batch,),
            in_specs=[per_bg(), pl.BlockSpec(memory_space=pl.ANY), per_b(4 * G), win_spec(0), win_spec(1),
                      per_b(2 * G), per_bg(), per_b(G)],
            out_specs=per_bg(),
            scratch_shapes=[pltpu.VMEM((G, n_top * SEL_BLOCK, hd), F32), pltpu.VMEM((G, n_top * SEL_BLOCK, hd), F32),
                            pltpu.SemaphoreType.DMA(())]),
        out_shape=jax.ShapeDtypeStruct((n_batch, G, rp, hd), BF16),
        compiler_params=_params("arbitrary"),
        name="nsa_attend_decode",
    )(page_table.reshape(-1).astype(jnp.int32), sel_idx.reshape(-1).astype(jnp.int32),
      pad_heads(qr), cache, new_rows, state_win, state_win, new_win, pad_heads(o_cmp), gates)
    return out[:, :, :R].reshape(n_batch, n_heads * hd)


def _gla_decode_kernel(q_ref, k_ref, v_ref, r_ref, small_ref, w2_ref, b2_ref, ng_ref, s0_ref, y_ref, sf_ref):
    dk, dv = GLA_DK, GLA_DV
    z = jnp.dot(small_ref[:, 0:GLA_RANK].astype(BF16), w2_ref[...].astype(BF16),
                preferred_element_type=F32) + b2_ref[...]
    g_all = (jnp.minimum(z, 0.0) - jnp.log1p(jnp.exp(-jnp.abs(z)))) / GLA_GATE_NORM
    eye = lax.broadcasted_iota(jnp.int32, (dk, dk), 0) == lax.broadcasted_iota(jnp.int32, (dk, dk), 1)
    column = lambda row: jnp.sum(jnp.where(eye, jnp.broadcast_to(row, (dk, dk)), 0.0), axis=1, keepdims=True)
    for b in range(q_ref.shape[0]):
        for h in range(GLA_HEADS):
            ks = slice(h * dk, (h + 1) * dk)
            vs = slice(h * dv, (h + 1) * dv)
            s_new = (jnp.exp(column(g_all[b:b + 1, ks])) * s0_ref[b, h]
                     + column(k_ref[b:b + 1, ks]) * v_ref[b:b + 1, vs])
            sf_ref[b, h] = s_new
            o = jnp.sum(column(q_ref[b:b + 1, ks] * (dk ** -0.5)) * s_new, axis=0, keepdims=True)
            o = o * lax.rsqrt(jnp.mean(o * o, axis=-1, keepdims=True) + RMS_EPS)
            y_ref[b:b + 1, vs] = (o * ng_ref[:, vs] * jax.nn.silu(r_ref[b:b + 1, vs])).astype(BF16)


def gla_decode(p, s0, w2, b2, norm_g):
    n = p.shape[0]
    row = lambda w, off: pl.BlockSpec((n, w), lambda i: (0, off // w))
    const = lambda shape: pl.BlockSpec(shape, lambda i: (0,) * len(shape))
    return pl.pallas_call(
        _gla_decode_kernel,
        grid=(1,),
        in_specs=[row(256, COL_GLA_Q), row(256, COL_GLA_K), row(512, COL_GLA_V), row(512, COL_GLA_R),
                  row(128, COL_SMALL), const((GLA_RANK, GLA_HEADS * GLA_DK)), const((1, GLA_HEADS * GLA_DK)),
                  const((1, GLA_WIDTH)), const(s0.shape)],
        out_specs=(const((n, GLA_WIDTH)), const(s0.shape)),
        out_shape=(jax.ShapeDtypeStruct((n, GLA_WIDTH), BF16), jax.ShapeDtypeStruct(s0.shape, F32)),
        compiler_params=_params("arbitrary"),
        name="gla_decode",
    )(p, p, p, p, p, w2, b2.reshape(1, -1), norm_g.reshape(1, -1), s0)


def _pool_decode_kernel(u_ref, prev_ref, w_ref, sc_ref, y_ref, *, past_len):
    gd = POOL_GROUP_DIM
    n_prev = prev_ref.shape[1]
    for gi, w in enumerate(POOL_WINDOWS):
        cols = slice(gi * gd, (gi + 1) * gd)
        x = u_ref[:, cols]
        s = x
        for r in range(n_prev - (w - 1), n_prev):
            s = s + prev_ref[:, r, cols]
        pooled = s / float(min(past_len + 1, w)) - x
        y = jnp.dot(pooled.astype(BF16), w_ref[gi].astype(BF16), preferred_element_type=F32)
        y_ref[:, cols] = (y * sc_ref[:, cols]).astype(BF16)


def pool_decode(p, prev, past_len, w_pool, scale):
    n = p.shape[0]
    const = lambda shape: pl.BlockSpec(shape, lambda i: (0,) * len(shape))
    return pl.pallas_call(
        functools.partial(_pool_decode_kernel, past_len=past_len),
        grid=(1,),
        in_specs=[pl.BlockSpec((n, POOL_WIDTH), lambda i: (0, COL_POOL // POOL_WIDTH)), const(prev.shape),
                  const(w_pool.shape), const((1, POOL_WIDTH))],
        out_specs=const((n, POOL_WIDTH)),
        out_shape=jax.ShapeDtypeStruct((n, POOL_WIDTH), BF16),
        compiler_params=_params("arbitrary"),
        name="pool_decode",
    )(p, prev, w_pool, scale.reshape(1, -1))


def decode_mixer(xs2, w_in_packed, w_out_bf16, ln_g, ln_b, gla_w2, gla_b, gla_norm_g, cmp_pos, cmp_w1, cmp_w2,
                 pool_w, pool_scale, cache, layer, page_table, state_win, state_gla, state_pool, past_len,
                 router=None):
    n_dec = xs2.shape[0]
    hd, G = HEAD_DIM, NSA_KV_HEADS
    p = matmul(xs2, w_in_packed)
    pos = jnp.full((n_dec,), past_len, jnp.int32)
    rows, win, qn, qr, _, _, _, _, gates = nsa_prep(p, rope_tables(pos), 1, n_dec)
    cmp_kv = nsa_compress_paged(cache, layer, page_table, cmp_pos, cmp_w1, cmp_w2)
    t_k = past_len + 1
    n_sel = -(-t_k // SEL_BLOCK)
    o_cmp, sel_idx = nsa_select_decode(qn[0].transpose(1, 0, 2), cmp_kv, past_len // CMP_STRIDE - 1, n_sel, past_len)
    y_nsa = nsa_attend_decode(qr[0].transpose(1, 0, 2), o_cmp, sel_idx, gates[0].transpose(1, 0, 2), cache, layer,
                              page_table, rows.reshape(n_dec, 4 * G, hd), state_win, win.reshape(n_dec, 2 * G, hd))
    y_gla, s_gla = gla_decode(p, state_gla, gla_w2, gla_b, gla_norm_g)
    y_pool = pool_decode(p, state_pool, past_len, pool_w, pool_scale)
    pool_rows = jnp.concatenate([state_pool[:, 1:], p[:, None, COL_POOL:COL_POOL + POOL_WIDTH]], axis=1)
    x1 = outproj_ln(xs2, y_gla, y_nsa, y_pool, w_out_bf16, ln_g, ln_b, router)
    nsa_rows = rows.reshape(n_dec, 1, 4, G, hd)
    new_win = jnp.concatenate([state_win[layer, :, 1:], win.reshape(n_dec, 1, 2, G, hd)], axis=1)
    return x1, nsa_rows, new_win, s_gla, pool_rows


def prompt_mixer(x2, w_in_packed, w_out_bf16, ln_g, ln_b, gla_w2, gla_b, gla_norm_g, cmp_pos, cmp_w1, cmp_w2,
                 pool_w, pool_scale, n_batch, t_len, router=None):
    p = matmul(x2, w_in_packed)
    pos = jnp.arange(t_len, dtype=jnp.int32)
    rows, win, qn, qr, ks, vs, kw, vw, gates = nsa_prep(p, rope_tables(pos), n_batch, t_len)
    cmp_kv = nsa_compress_prompt(rows, cmp_pos, cmp_w1, cmp_w2, n_batch, t_len)
    o_cmp, selb = nsa_select(qn, cmp_kv, t_len // CMP_STRIDE - 1, t_len)
    y_nsa = nsa_attend(qr, o_cmp, selb, gates, ks, vs, kw, vw)
    s0 = jnp.zeros((n_batch, GLA_HEADS, GLA_DK, GLA_DV), F32)
    y_gla, s_gla = gla_mix(p, s0, gla_w2, gla_b, gla_norm_g, n_batch, t_len)
    prev = jnp.zeros((n_batch, POOL_MAX - 1, POOL_WIDTH), F32)
    y_pool = pool_mix(p, prev, 0, pool_w, pool_scale, n_batch, t_len)
    x1 = outproj_ln(x2, y_gla, y_nsa, y_pool, w_out_bf16, ln_g, ln_b, router)
    nsa_rows = rows.reshape(n_batch, t_len, 4, NSA_KV_HEADS, HEAD_DIM)
    n_win = min(WINDOW, t_len)
    win_rows = win.reshape(n_batch, t_len, 2, NSA_KV_HEADS, HEAD_DIM)[:, t_len - n_win:]
    pool_rows = p.reshape(n_batch, t_len, PACKED_WIDTH)[:, t_len - (POOL_MAX - 1):, COL_POOL:COL_POOL + POOL_WIDTH]
    return x1, nsa_rows, win_rows, s_gla, pool_rows


def kernel(x_prompt, x_sample, cache_nsa, page_table, state_win, state_gla, state_pool, w_in, gla_gate_w2, gla_gate_b, gla_norm_g, nsa_cmp_pos, nsa_cmp_w1, nsa_cmp_w2, pool_w, pool_scale, w_out, ln1_g, ln1_b, ln2_g, ln2_b, ffn_w_gate, ffn_w_up, ffn_w_down, moe_router, moe_w_gate, moe_w_up, moe_w_down):
    n_prompt, t_len, d = x_prompt.shape
    n_dec = x_sample.shape[0]
    xp, xs = x_prompt.reshape(n_prompt * t_len, d), x_sample.reshape(n_dec, d)
    nsa_p, nsa_s, win_p, win_s, gla_p, gla_s, pool_p, pool_s = [], [], [], [], [], [], [], []
    for l in range(DEPTH):
        i = l // 2
        router = moe_router[i] if l % 2 else None
        w_in_packed, w_out_bf16 = pack_w_in(w_in[l]), w_out[l].astype(BF16)
        lw = (w_in_packed, w_out_bf16, ln1_g[l], ln1_b[l], gla_gate_w2[l], gla_gate_b[l], gla_norm_g[l],
              nsa_cmp_pos[l], nsa_cmp_w1[l], nsa_cmp_w2[l], pool_w[l], pool_scale[l])
        xp, r_p, w_p, g_p, p_p = prompt_mixer(xp, *lw, n_prompt, t_len, router)
        xs, r_s, w_s, g_s, p_s = decode_mixer(xs, *lw, cache_nsa, l, page_table, state_win, state_gla[l],
                                              state_pool[l], PAST_LEN, router)
        if l % 2 == 0:
            wg, wu, wd = ffn_w_gate[i].astype(BF16), ffn_w_up[i].astype(BF16), ffn_w_down[i].astype(BF16)
            xp = ffn_ln(xp, wg, wu, wd, ln2_g[l], ln2_b[l])
            xs = ffn_ln(xs, wg, wu, wd, ln2_g[l], ln2_b[l])
        else:
            (xp, lg_p), (xs, lg_s) = xp, xs
            xp, xs = moe_ln([xp, xs], [lg_p, lg_s], moe_w_gate[i], moe_w_up[i], moe_w_down[i], ln2_g[l], ln2_b[l])
        nsa_p.append(r_p); nsa_s.append(r_s); win_p.append(w_p); win_s.append(w_s)
        gla_p.append(g_p); gla_s.append(g_s); pool_p.append(p_p); pool_s.append(p_s)
    return (xp.reshape(n_prompt, t_len, d), xs.reshape(x_sample.shape), jnp.stack(nsa_p), jnp.stack(nsa_s),
            jnp.stack(win_p), jnp.stack(win_s), jnp.stack(gla_p), jnp.stack(gla_s), jnp.stack(pool_p),
            jnp.stack(pool_s))
```

```python
import functools

import jax
import jax.numpy as jnp
from jax import lax
from jax.experimental import pallas as pl
from jax.experimental.pallas import tpu as pltpu

D_MODEL = 2048
DEPTH = 2
PAST_LEN = 16384
HEAD_DIM = 128
GLA_HEADS = 4
GLA_DK = 64
GLA_DV = 128
GLA_RANK = 16
GLA_GATE_NORM = 16.0
GLA_CHUNK = 64
GLA_WIDTH = GLA_HEADS * GLA_DV
NSA_HEADS = 8
NSA_KV_HEADS = 2
NSA_REP = NSA_HEADS // NSA_KV_HEADS
NSA_WIDTH = NSA_HEADS * HEAD_DIM
CMP_LEN = 32
CMP_STRIDE = 16
CMP_HIDDEN = 128
SEL_BLOCK = 64
SEL_TOPN = 16
WINDOW = 512
POOL_GROUPS = 4
POOL_GROUP_DIM = 128
POOL_WIDTH = POOL_GROUPS * POOL_GROUP_DIM
POOL_WINDOWS = (2, 4, 8, 16)
POOL_MAX = 16
ROPE_THETA = 500000.0
ROPE_DIM = HEAD_DIM // 4
N_EXPERTS = 8
TOP_K = 2
ALPHA = (2 * DEPTH) ** 0.25
LN_EPS = 1e-5
RMS_EPS = 1e-6

PROJ_SIZES = (
    ('gla_q', GLA_HEADS * GLA_DK), ('gla_k', GLA_HEADS * GLA_DK), ('gla_v', GLA_HEADS * GLA_DV),
    ('gla_glr', GLA_RANK), ('gla_r', GLA_HEADS * GLA_DV),
    ('nsa_q', NSA_HEADS * HEAD_DIM),
    ('cmp_k', NSA_KV_HEADS * HEAD_DIM), ('cmp_v', NSA_KV_HEADS * HEAD_DIM),
    ('slc_k', NSA_KV_HEADS * HEAD_DIM), ('slc_v', NSA_KV_HEADS * HEAD_DIM),
    ('win_k', NSA_KV_HEADS * HEAD_DIM), ('win_v', NSA_KV_HEADS * HEAD_DIM),
    ('nsa_gate', 3 * NSA_HEADS),
    ('pool', POOL_WIDTH),
)

GLA_SUB = 16
SEL_PAD = 128

BF16 = jnp.bfloat16
F32 = jnp.float32
NEG_BIG = -1e30
VMEM_LIMIT_BYTES = 56 * 1024 * 1024

COL_NSA_Q = 0
COL_ROWS = 1024
COL_WIN = 2048
COL_POOL = 2560
COL_GLA_V = 3072
COL_GLA_R = 3584
COL_GLA_Q = 4096
COL_GLA_K = 4352
COL_SMALL = 4608
PACKED_WIDTH = 4736
SMALL_GATE_OFF = GLA_RANK


def _params(*sem, vmem_limit_bytes=VMEM_LIMIT_BYTES):
    return pltpu.CompilerParams(dimension_semantics=sem, vmem_limit_bytes=vmem_limit_bytes)


def _proj_offsets():
    out, off = {}, 0
    for name, size in PROJ_SIZES:
        out[name] = (off, size)
        off += size
    return out


def pack_w_in(w):
    offs = _proj_offsets()
    sl = lambda n: w[:, offs[n][0]:offs[n][0] + offs[n][1]]
    pad = jnp.zeros((w.shape[0], 128 - GLA_RANK - 3 * NSA_HEADS), w.dtype)
    cols = [sl('nsa_q'), sl('cmp_k'), sl('cmp_v'), sl('slc_k'), sl('slc_v'), sl('win_k'), sl('win_v'),
            sl('pool'), sl('gla_v'), sl('gla_r'), sl('gla_q'), sl('gla_k'), sl('gla_glr'), sl('nsa_gate'), pad]
    return jnp.concatenate(cols, axis=1).astype(BF16)


def rope_tables(pos):
    half = ROPE_DIM // 2
    inv_freq = ROPE_THETA ** (-jnp.arange(half, dtype=F32) / half)
    ang = pos.astype(F32)[:, None] * inv_freq[None, :]
    cos, sin = jnp.cos(ang), jnp.sin(ang)
    t = pos.shape[0]
    c = jnp.concatenate([cos, cos, jnp.ones((t, HEAD_DIM - ROPE_DIM), F32)], axis=1)
    sa = jnp.concatenate([-sin, jnp.zeros((t, HEAD_DIM - half), F32)], axis=1)
    sb = jnp.concatenate([jnp.zeros((t, half), F32), sin, jnp.zeros((t, HEAD_DIM - ROPE_DIM), F32)], axis=1)
    return c, sa, sb


def _pick_tile(n, pref):
    for t in pref:
        if n % t == 0:
            return t
    return n


def _mm_kernel(x_ref, w_ref, o_ref, xb_ref):
    @pl.when(pl.program_id(1) == 0)
    def _():
        xb_ref[...] = x_ref[...].astype(BF16)

    o_ref[...] = jnp.dot(xb_ref[...], w_ref[...].astype(BF16), preferred_element_type=F32)


def matmul(x, w):
    m, k = x.shape
    n = w.shape[1]
    tm = _pick_tile(m, tuple(t for t in (1024, 512, 256, 128, 64, 32, 16, 8) if t * k <= 2048 * 1024))
    tn = 512 if n >= 512 else n
    return pl.pallas_call(
        _mm_kernel,
        grid=(m // tm, pl.cdiv(n, tn)),
        in_specs=[pl.BlockSpec((tm, k), lambda i, j: (i, 0)),
                  pl.BlockSpec((k, tn), lambda i, j: (0, j))],
        out_specs=pl.BlockSpec((tm, tn), lambda i, j: (i, j)),
        out_shape=jax.ShapeDtypeStruct((m, n), F32),
        scratch_shapes=[pltpu.VMEM((tm, k), BF16)],
        compiler_params=_params("arbitrary", "arbitrary"),
        name="matmul",
    )(x, w)


def _rope(x, c, sa, sb):
    return x * c + pltpu.roll(x, HEAD_DIM - ROPE_DIM // 2, 1) * sa + pltpu.roll(x, ROPE_DIM // 2, 1) * sb


def _nsa_prep_kernel(q_ref, rows_ref, win_ref, small_ref, c_ref, sa_ref, sb_ref,
                     rows_o, win_o, qn_o, qr_o, ks_o, vs_o, kw_o, vw_o, gate_o):
    c, sa, sb = c_ref[...], sa_ref[...], sb_ref[...]
    scale = HEAD_DIM ** -0.5
    hd = HEAD_DIM
    for h in range(NSA_HEADS):
        x = q_ref[:, h * hd:(h + 1) * hd]
        qn_o[0, h] = (x * scale).astype(BF16)
        qr_o[0, h] = (_rope(x, c, sa, sb) * scale).astype(BF16)
    ones = jnp.ones((q_ref.shape[0], hd), BF16)
    rows_o[:, 0:4 * hd] = rows_ref[:, 0:4 * hd]
    for g in range(NSA_KV_HEADS):
        k = _rope(rows_ref[:, (4 + g) * hd:(5 + g) * hd], c, sa, sb)
        rows_o[:, (4 + g) * hd:(5 + g) * hd] = k
        ks_o[0, g] = k.astype(BF16)
        v = rows_ref[:, (6 + g) * hd:(7 + g) * hd]
        rows_o[:, (6 + g) * hd:(7 + g) * hd] = v
        vs_o[0, g, :, 0:hd] = v.astype(BF16)
        vs_o[0, g, :, hd:2 * hd] = ones
        k = _rope(win_ref[:, g * hd:(g + 1) * hd], c, sa, sb)
        win_o[:, g * hd:(g + 1) * hd] = k
        kw_o[0, g] = k.astype(BF16)
        v = win_ref[:, (2 + g) * hd:(3 + g) * hd]
        win_o[:, (2 + g) * hd:(3 + g) * hd] = v
        vw_o[0, g, :, 0:hd] = v.astype(BF16)
        vw_o[0, g, :, hd:2 * hd] = ones
    sig = jax.nn.sigmoid(small_ref[...])
    per_g = 3 * NSA_REP
    for g in range(NSA_KV_HEADS):
        gate_o[0, g] = pltpu.roll(sig, 128 - SMALL_GATE_OFF - g * per_g, 1)


def nsa_prep(p, tables, n_batch, t_len):
    tr = _pick_tile(t_len, (512, 256, 128, 64, 32, 16))
    nt = t_len // tr
    n = n_batch * t_len
    hd = HEAD_DIM
    row = lambda w, cb: pl.BlockSpec((tr, w), lambda b, i: (b * nt + i, cb))
    tab = pl.BlockSpec((tr, hd), lambda b, i: (i, 0))
    head = lambda nh, w: pl.BlockSpec((1, nh, tr, w), lambda b, i: (b, 0, i, 0))
    out_shape = (
        jax.ShapeDtypeStruct((n, 8 * hd), F32),
        jax.ShapeDtypeStruct((n, 4 * hd), F32),
        jax.ShapeDtypeStruct((n_batch, NSA_HEADS, t_len, hd), BF16),
        jax.ShapeDtypeStruct((n_batch, NSA_HEADS, t_len, hd), BF16),
        jax.ShapeDtypeStruct((n_batch, NSA_KV_HEADS, t_len, hd), BF16),
        jax.ShapeDtypeStruct((n_batch, NSA_KV_HEADS, t_len, 2 * hd), BF16),
        jax.ShapeDtypeStruct((n_batch, NSA_KV_HEADS, t_len, hd), BF16),
        jax.ShapeDtypeStruct((n_batch, NSA_KV_HEADS, t_len, 2 * hd), BF16),
        jax.ShapeDtypeStruct((n_batch, NSA_KV_HEADS, t_len, 128), F32),
    )
    return pl.pallas_call(
        _nsa_prep_kernel,
        grid=(n_batch, nt),
        in_specs=[row(8 * hd, COL_NSA_Q // (8 * hd)), row(8 * hd, COL_ROWS // (8 * hd)),
                  row(4 * hd, COL_WIN // (4 * hd)), row(128, COL_SMALL // 128), tab, tab, tab],
        out_specs=(row(8 * hd, 0), row(4 * hd, 0), head(NSA_HEADS, hd), head(NSA_HEADS, hd),
                   head(NSA_KV_HEADS, hd), head(NSA_KV_HEADS, 2 * hd), head(NSA_KV_HEADS, hd),
                   head(NSA_KV_HEADS, 2 * hd), head(NSA_KV_HEADS, 128)),
        out_shape=out_shape,
        compiler_params=_params("arbitrary", "arbitrary"),
        name="nsa_prep",
    )(p, p, p, p, *tables)


def _nsa_cmp_kernel(x_ref, pe_ref, w1_ref, w2_ref, o_ref):
    nh = o_ref.shape[0]
    h_lo = jnp.zeros((nh, CMP_HIDDEN), F32)
    h_hi = jnp.zeros((nh, CMP_HIDDEN), F32)
    for j in range(CMP_STRIDE):
        xj = x_ref[pl.ds(j, nh, stride=CMP_STRIDE), :]
        h_lo += jnp.dot((xj + pe_ref[j:j + 1, :]).astype(BF16), w1_ref[j].astype(BF16), preferred_element_type=F32)
        h_hi += jnp.dot((xj + pe_ref[CMP_STRIDE + j:CMP_STRIDE + j + 1, :]).astype(BF16),
                        w1_ref[CMP_STRIDE + j].astype(BF16), preferred_element_type=F32)
    h = jax.nn.gelu(h_lo + pltpu.roll(h_hi, nh - 1, 0))
    o_ref[...] = jnp.dot(h.astype(BF16), w2_ref[...].astype(BF16), preferred_element_type=F32).astype(BF16)


def nsa_compress_prompt(rows, cmp_pos, cmp_w1, cmp_w2, n_batch, t_len):
    nh = t_len // CMP_STRIDE
    hd = HEAD_DIM
    rows3 = rows.reshape(n_batch, t_len, 8 * hd)
    return pl.pallas_call(
        _nsa_cmp_kernel,
        grid=(n_batch, 2, NSA_KV_HEADS),
        in_specs=[pl.BlockSpec((None, t_len, hd), lambda b, kd, g: (b, 0, kd * NSA_KV_HEADS + g)),
                  pl.BlockSpec((None, CMP_LEN, hd), lambda b, kd, g: (kd, 0, 0)),
                  pl.BlockSpec((None, CMP_LEN, hd, CMP_HIDDEN), lambda b, kd, g: (kd, 0, 0, 0)),
                  pl.BlockSpec((None, CMP_HIDDEN, hd), lambda b, kd, g: (kd, 0, 0))],
        out_specs=pl.BlockSpec((None, None, None, nh, hd), lambda b, kd, g: (b, kd, g, 0, 0)),
        out_shape=jax.ShapeDtypeStruct((n_batch, 2, NSA_KV_HEADS, nh, hd), BF16),
        compiler_params=_params("arbitrary", "arbitrary", "arbitrary"),
        name="nsa_compress",
    )(rows3, cmp_pos, cmp_w1, cmp_w2)


def _nsa_select_kernel(qn_ref, kc_ref, vc_ref, ovt_ref, oc_ref, selb_ref, *, n_cmp, n_top):
    rep, tq, hd = qn_ref.shape[1], qn_ref.shape[2], qn_ref.shape[3]
    n_cmp_pad = kc_ref.shape[0]
    n_sel = ovt_ref.shape[0]
    q0 = pl.program_id(2) * tq
    q = qn_ref[0].reshape(rep * tq, hd)
    s = lax.dot_general(q, kc_ref[...], (((1,), (1,)), ((), ())), preferred_element_type=F32)
    row = lax.broadcasted_iota(jnp.int32, (rep * tq, n_cmp_pad), 0)
    col = lax.broadcasted_iota(jnp.int32, (rep * tq, n_cmp_pad), 1)
    qpos = q0 + (row & (tq - 1))
    mask = (col * CMP_STRIDE + (CMP_LEN - 1) <= qpos) & (col < n_cmp)
    s = jnp.where(mask, s, -jnp.inf)
    m = jnp.max(s, axis=-1, keepdims=True)
    m = jnp.where(m > -jnp.inf, m, 0.0)
    p = jnp.where(mask, jnp.exp(s - m), 0.0)
    p = p / jnp.maximum(jnp.sum(p, axis=-1, keepdims=True), 1e-30)
    oc = jnp.dot(p.astype(BF16), vc_ref[...], preferred_element_type=F32)
    oc_ref[0] = oc.reshape(rep, tq, hd).astype(BF16)
    psum = p[0:tq]
    for r in range(1, rep):
        psum = psum + p[r * tq:(r + 1) * tq]
    imp = lax.dot_general(ovt_ref[...], psum, (((1,), (1,)), ((), ())), preferred_element_type=F32,
                          precision=lax.Precision.HIGHEST)
    blk = lax.broadcasted_iota(jnp.int32, (n_sel, tq), 0)
    cur = (q0 + lax.broadcasted_iota(jnp.int32, (n_sel, tq), 1)) // SEL_BLOCK
    forced = (blk == 0) | (blk == cur) | (blk == cur - 1)
    v = jnp.where(blk <= cur, jnp.where(forced, jnp.inf, imp), -jnp.inf)
    rank = jnp.zeros((n_sel, tq), jnp.int32)
    for i in range(n_sel):
        vi = v[i:i + 1, :]
        ahead = (vi > v) | ((vi == v) & (blk > i))
        rank = rank + ahead.astype(jnp.int32)
    selb_t = jnp.where((rank < n_top) & (v > -jnp.inf), 0.0, NEG_BIG)
    pad = jnp.full((SEL_PAD - n_sel, tq), NEG_BIG, F32)
    selb_ref[0, 0] = jnp.concatenate([selb_t, pad], axis=0).T.astype(BF16)


def nsa_select(qn, cmp_kv, n_cmp, t_k):
    n_batch, _, t_len, hd = qn.shape
    n_cmp_pad = cmp_kv.shape[3]
    n_sel = -(-t_k // SEL_BLOCK)
    tq = _pick_tile(t_len, (256, 128, 64, 32, 16))
    ci = jnp.arange(n_cmp_pad)[None, :]
    sj = jnp.arange(n_sel)[:, None]
    overlap_t = ((ci * CMP_STRIDE <= sj * SEL_BLOCK + SEL_BLOCK - 1) &
                 (ci * CMP_STRIDE + CMP_LEN - 1 >= sj * SEL_BLOCK) & (ci < n_cmp)).astype(F32)
    kern = functools.partial(_nsa_select_kernel, n_cmp=n_cmp, n_top=min(SEL_TOPN, n_sel))
    return pl.pallas_call(
        kern,
        grid=(n_batch, NSA_KV_HEADS, t_len // tq),
        in_specs=[pl.BlockSpec((1, NSA_REP, tq, hd), lambda b, g, i: (b, g, i, 0)),
                  pl.BlockSpec((None, None, None, n_cmp_pad, hd), lambda b, g, i: (b, 0, g, 0, 0)),
                  pl.BlockSpec((None, None, None, n_cmp_pad, hd), lambda b, g, i: (b, 1, g, 0, 0)),
                  pl.BlockSpec((n_sel, n_cmp_pad), lambda b, g, i: (0, 0))],
        out_specs=(pl.BlockSpec((1, NSA_REP, tq, hd), lambda b, g, i: (b, g, i, 0)),
                   pl.BlockSpec((1, 1, tq, SEL_PAD), lambda b, g, i: (b, g, i, 0))),
        out_shape=(jax.ShapeDtypeStruct((n_batch, NSA_HEADS, t_len, hd), BF16),
                   jax.ShapeDtypeStruct((n_batch, NSA_KV_HEADS, t_len, SEL_PAD), BF16)),
        compiler_params=_params("arbitrary", "arbitrary", "arbitrary"),
        name="nsa_select",
    )(qn, cmp_kv, cmp_kv, overlap_t)


ATTN_Q_BLOCK = 256
ATTN_K_TILE = 1024


def _nsa_attn_kernel(qr_ref, oc_ref, selb_ref, gate_ref, ks_ref, vs_ref, kw_ref, vw_ref, e_ref, cb_ref, wb_ref, o_ref,
                     m_scr, acc_scr, *, tk, wk):
    rep, qb, hd = qr_ref.shape[1], qr_ref.shape[2], qr_ref.shape[3]
    nr = rep * qb
    q0 = pl.program_id(2) * qb
    q = qr_ref[0].reshape(nr, hd)
    selb = selb_ref[0, 0]
    nt = (((1,), (1,)), ((), ()))

    def sel_scores(t, extra_bias=None):
        k = ks_ref[0, 0, pl.ds(pl.multiple_of(t * tk, tk), tk), :]
        s = lax.dot_general(q, k, nt, preferred_element_type=F32)
        bias = jnp.dot(selb, e_ref[t], preferred_element_type=F32)
        if extra_bias is not None:
            bias = bias + extra_bias
        return (s.reshape(rep, qb, tk) + bias[None]).reshape(nr, tk)

    def sel_values(t):
        return vs_ref[0, 0, pl.ds(pl.multiple_of(t * tk, tk), tk), :]

    td = q0 // tk
    s = sel_scores(td, cb_ref[0])
    m = jnp.max(s, axis=-1, keepdims=True)
    m_scr[...] = m
    acc_scr[...] = jnp.dot(jnp.exp(s - m).astype(BF16), sel_values(td), preferred_element_type=F32)

    def body(t, carry):
        s = sel_scores(t)
        m_old = m_scr[...]
        m_new = jnp.maximum(m_old, jnp.max(s, axis=-1, keepdims=True))
        p = jnp.exp(s - m_new).astype(BF16)
        acc_scr[...] = jnp.exp(m_old - m_new) * acc_scr[...] + jnp.dot(p, sel_values(t), preferred_element_type=F32)
        m_scr[...] = m_new
        return carry

    kstart = pl.multiple_of(jnp.maximum(q0 - WINDOW, 0), qb)
    kw = kw_ref[0, 0, pl.ds(kstart, wk), :]
    s = lax.dot_general(q, kw, nt, preferred_element_type=F32)
    s = (s.reshape(rep, qb, wk) + wb_ref[0][None]).reshape(nr, wk)
    m = jnp.max(s, axis=-1, keepdims=True)
    accw = jnp.dot(jnp.exp(s - m).astype(BF16), vw_ref[0, 0, pl.ds(kstart, wk), :], preferred_element_type=F32)
    o_win = accw[:, 0:hd] / jnp.maximum(accw[:, hd:hd + 1], 1e-30)

    lax.fori_loop(0, td, body, 0)
    acc = acc_scr[...]
    o_sel = acc[:, 0:hd] / jnp.maximum(acc[:, hd:hd + 1], 1e-30)

    gates = gate_ref[0, 0]
    for r in range(rep):
        rows = slice(r * qb, (r + 1) * qb)
        o = (gates[:, 3 * r:3 * r + 1] * oc_ref[0, r].astype(F32)
             + gates[:, 3 * r + 1:3 * r + 2] * o_sel[rows]
             + gates[:, 3 * r + 2:3 * r + 3] * o_win[rows])
        o_ref[:, r * hd:(r + 1) * hd] = o.astype(BF16)


def nsa_attend(qr, o_cmp, selb, gates, ks, vs, kw, vw):
    n_batch, _, t_len, hd = qr.shape
    n_sel = selb.shape[3]
    qb = min(ATTN_Q_BLOCK, t_len)
    tk = min(ATTN_K_TILE, t_len)
    wk = min(WINDOW + qb, t_len)
    nq = t_len // qb
    n_tiles = t_len // tk
    key_blk = (jnp.arange(n_tiles)[:, None, None] * tk + jnp.arange(tk)[None, None, :]) // SEL_BLOCK
    e = (key_blk == jnp.arange(n_sel)[None, :, None]).astype(BF16)
    n_cv = tk // qb
    qi = jnp.arange(qb)[None, :, None]
    causal = jnp.where(jnp.arange(tk)[None, None, :] <= jnp.arange(n_cv)[:, None, None] * qb + qi, 0.0, NEG_BIG)
    n_wv = min(WINDOW // qb, nq - 1) + 1
    q0v = jnp.arange(n_wv)[:, None, None] * qb
    kpos = jnp.maximum(q0v - WINDOW, 0) + jnp.arange(wk)[None, None, :]
    qpos = q0v + qi
    window = jnp.where((kpos <= qpos) & (kpos > qpos - WINDOW), 0.0, NEG_BIG).astype(F32)
    kern = functools.partial(_nsa_attn_kernel, tk=tk, wk=wk)
    per_q = lambda nh, w: pl.BlockSpec((1, nh, qb, w), lambda b, g, i: (b, g, i, 0))
    full = lambda w: pl.BlockSpec((1, 1, t_len, w), lambda b, g, i: (b, g, 0, 0))
    return pl.pallas_call(
        kern,
        grid=(n_batch, NSA_KV_HEADS, nq),
        in_specs=[per_q(NSA_REP, hd), per_q(NSA_REP, hd), per_q(1, n_sel), per_q(1, 128),
                  full(hd), full(2 * hd), full(hd), full(2 * hd),
                  pl.BlockSpec((n_tiles, n_sel, tk), lambda b, g, i: (0, 0, 0)),
                  pl.BlockSpec((1, qb, tk), lambda b, g, i: (i % n_cv, 0, 0)),
                  pl.BlockSpec((1, qb, wk), lambda b, g, i: (jnp.minimum(i, n_wv - 1), 0, 0))],
        out_specs=pl.BlockSpec((qb, NSA_REP * hd), lambda b, g, i: (b * nq + i, g)),
        out_shape=jax.ShapeDtypeStruct((n_batch * t_len, NSA_HEADS * hd), BF16),
        scratch_shapes=[pltpu.VMEM((NSA_REP * qb, 1), F32), pltpu.VMEM((NSA_REP * qb, 2 * hd), F32)],
        compiler_params=_params("arbitrary", "arbitrary", "arbitrary"),
        name="nsa_attend",
    )(qr, o_cmp, selb, gates, ks, vs, kw, vw, e, causal.astype(F32), window)


def _gla_kernel(q_ref, k_ref, v_ref, r_ref, small_ref, w2_ref, b2_ref, ng_ref, s0_ref, y_ref, sf_ref, s_scr):
    tb = q_ref.shape[0]
    c, sub, dk, dv = GLA_CHUNK, GLA_SUB, GLA_DK, GLA_DV
    n_sub = c // sub
    t = pl.program_id(1)

    @pl.when(t == 0)
    def _():
        s_scr[...] = s0_ref[0]

    z = jnp.dot(small_ref[:, 0:GLA_RANK].astype(BF16), w2_ref[...].astype(BF16),
                preferred_element_type=F32) + b2_ref[...]
    g_all = (jnp.minimum(z, 0.0) - jnp.log1p(jnp.exp(-jnp.abs(z)))) / GLA_GATE_NORM
    ri = lax.broadcasted_iota(jnp.int32, (c, c), 0)
    ci = lax.broadcasted_iota(jnp.int32, (c, c), 1)
    tril = ri >= ci
    cum = tril.astype(F32)
    rsub = lax.broadcasted_iota(jnp.int32, (c, dk), 0) // sub
    eye = lax.broadcasted_iota(jnp.int32, (dk, dk), 0) == lax.broadcasted_iota(jnp.int32, (dk, dk), 1)
    for cc in range(tb // c):
        rows = slice(cc * c, (cc + 1) * c)
        b_all = jnp.dot(cum, g_all[rows], preferred_element_type=F32, precision=lax.Precision.HIGHEST)
        for h in range(GLA_HEADS):
            b = b_all[:, h * dk:(h + 1) * dk]
            qh = q_ref[rows, h * dk:(h + 1) * dk] * (dk ** -0.5)
            kh = k_ref[rows, h * dk:(h + 1) * dk]
            vh = v_ref[rows, h * dv:(h + 1) * dv]
            a_rows = []
            for i in range(n_sub):
                ref = b[sub * i - 1:sub * i, :] if i else jnp.zeros((1, dk), F32)
                rs = slice(sub * i, sub * (i + 1))
                qi = (qh[rs] * jnp.exp(b[rs] - ref)).astype(BF16)
                ki = jnp.where(rsub <= i, kh * jnp.exp(ref - b), 0.0).astype(BF16)
                a_rows.append(lax.dot_general(qi, ki, (((1,), (1,)), ((), ())), preferred_element_type=F32))
            a = jnp.where(tril, jnp.concatenate(a_rows, axis=0), 0.0)
            s_old = s_scr[h]
            o = jnp.dot(a.astype(BF16), vh.astype(BF16), preferred_element_type=F32)
            o += jnp.dot((qh * jnp.exp(b)).astype(BF16), s_old.astype(BF16), preferred_element_type=F32)
            b_last = b[c - 1:c, :]
            ke = (kh * jnp.exp(b_last - b)).astype(BF16)
            upd = lax.dot_general(ke, vh.astype(BF16), (((0,), (0,)), ((), ())), preferred_element_type=F32)
            decay = jnp.exp(jnp.sum(jnp.where(eye, jnp.broadcast_to(b_last, (dk, dk)), 0.0), axis=1, keepdims=True))
            s_scr[h] = decay * s_old + upd
            o = o * lax.rsqrt(jnp.mean(o * o, axis=-1, keepdims=True) + RMS_EPS)
            y = o * ng_ref[:, h * dv:(h + 1) * dv] * jax.nn.silu(r_ref[rows, h * dv:(h + 1) * dv])
            y_ref[rows, h * dv:(h + 1) * dv] = y.astype(BF16)

    @pl.when(t == pl.num_programs(1) - 1)
    def _():
        sf_ref[0] = s_scr[...]


def gla_mix(p, s0, w2, b2, norm_g, n_batch, t_len):
    tb = _pick_tile(t_len, (256, 128, 64))
    nt = t_len // tb
    row = lambda w, off: pl.BlockSpec((tb, w), lambda b, i: (b * nt + i, off // w))
    const = lambda shape: pl.BlockSpec(shape, lambda b, i: (0,) * len(shape))
    return pl.pallas_call(
        _gla_kernel,
        grid=(n_batch, nt),
        in_specs=[row(256, COL_GLA_Q), row(256, COL_GLA_K), row(512, COL_GLA_V), row(512, COL_GLA_R),
                  row(128, COL_SMALL), const((GLA_RANK, GLA_HEADS * GLA_DK)), const((1, GLA_HEADS * GLA_DK)),
                  const((1, GLA_WIDTH)),
                  pl.BlockSpec((1, GLA_HEADS, GLA_DK, GLA_DV), lambda b, i: (b, 0, 0, 0))],
        out_specs=(pl.BlockSpec((tb, GLA_WIDTH), lambda b, i: (b * nt + i, 0)),
                   pl.BlockSpec((1, GLA_HEADS, GLA_DK, GLA_DV), lambda b, i: (b, 0, 0, 0))),
        out_shape=(jax.ShapeDtypeStruct((n_batch * t_len, GLA_WIDTH), BF16),
                   jax.ShapeDtypeStruct((n_batch, GLA_HEADS, GLA_DK, GLA_DV), F32)),
        scratch_shapes=[pltpu.VMEM((GLA_HEADS, GLA_DK, GLA_DV), F32)],
        compiler_params=_params("arbitrary", "arbitrary"),
        name="gla_mix",
    )(p, p, p, p, p, w2, b2.reshape(1, -1), norm_g.reshape(1, -1), s0)


def _pool_kernel(u_ref, prev_ref, cnt_ref, w_ref, sc_ref, y_ref, halo):
    tb = u_ref.shape[0]
    gd = POOL_GROUP_DIM

    @pl.when(pl.program_id(1) == 0)
    def _():
        halo[...] = prev_ref[0]

    ext = jnp.concatenate([halo[...], u_ref[...]], axis=0)
    halo[...] = ext[tb:tb + POOL_MAX]
    for gi, w in enumerate(POOL_WINDOWS):
        x = ext[:, gi * gd:(gi + 1) * gd]
        s = x
        shift = 1
        while shift < w:
            s = s + pltpu.roll(s, shift, 0)
            shift *= 2
        pooled = s[POOL_MAX:] / cnt_ref[:, gi:gi + 1] - x[POOL_MAX:]
        y = jnp.dot(pooled.astype(BF16), w_ref[gi].astype(BF16), preferred_element_type=F32)
        y_ref[:, gi * gd:(gi + 1) * gd] = (y * sc_ref[:, gi * gd:(gi + 1) * gd]).astype(BF16)


def pool_mix(p, prev, pos0, w_pool, scale, n_batch, t_len):
    tb = _pick_tile(t_len, (512, 256, 128, 64, 32, 16))
    nt = t_len // tb
    pos = pos0 + jnp.arange(t_len, dtype=jnp.int32)
    cnt = jnp.stack([jnp.minimum(pos + 1, w).astype(F32) for w in POOL_WINDOWS], axis=1)
    cnt = jnp.pad(cnt, ((0, 0), (0, 128 - POOL_GROUPS)), constant_values=1.0)
    prev16 = jnp.pad(prev.astype(F32), ((0, 0), (1, 0), (0, 0)))
    return pl.pallas_call(
        _pool_kernel,
        grid=(n_batch, nt),
        in_specs=[pl.BlockSpec((tb, POOL_WIDTH), lambda b, i: (b * nt + i, COL_POOL // POOL_WIDTH)),
                  pl.BlockSpec((1, POOL_MAX, POOL_WIDTH), lambda b, i: (b, 0, 0)),
                  pl.BlockSpec((tb, 128), lambda b, i: (i, 0)),
                  pl.BlockSpec((POOL_GROUPS, POOL_GROUP_DIM, POOL_GROUP_DIM), lambda b, i: (0, 0, 0)),
                  pl.BlockSpec((1, POOL_WIDTH), lambda b, i: (0, 0))],
        out_specs=pl.BlockSpec((tb, POOL_WIDTH), lambda b, i: (b * nt + i, 0)),
        out_shape=jax.ShapeDtypeStruct((n_batch * t_len, POOL_WIDTH), BF16),
        scratch_shapes=[pltpu.VMEM((POOL_MAX, POOL_WIDTH), F32)],
        compiler_params=_params("arbitrary", "arbitrary"),
        name="pool_mix",
    )(p, prev16, cnt, w_pool, scale.reshape(1, -1))


def _layer_norm_rows(x, g, b):
    xc = x - jnp.mean(x, axis=-1, keepdims=True)
    var = jnp.mean(xc * xc, axis=-1, keepdims=True)
    return xc * lax.rsqrt(var + LN_EPS) * g + b


def _outproj_kernel(x_ref, yg_ref, yn_ref, yp_ref, w_ref, g_ref, b_ref, *rest):
    h = jnp.dot(yg_ref[...], w_ref[0:GLA_WIDTH, :], preferred_element_type=F32)
    h += jnp.dot(yn_ref[...], w_ref[GLA_WIDTH:GLA_WIDTH + NSA_WIDTH, :], preferred_element_type=F32)
    h += jnp.dot(yp_ref[...], w_ref[GLA_WIDTH + NSA_WIDTH:, :], preferred_element_type=F32)
    x1 = _layer_norm_rows(ALPHA * x_ref[...] + h, g_ref[...], b_ref[...])
    if len(rest) == 1:
        rest[0][...] = x1
    else:
        rh_ref, rl_ref, o_ref, lg_ref = rest
        o_ref[...] = x1
        xh = x1.astype(BF16)
        xl = (x1 - xh.astype(F32)).astype(BF16)
        lg_ref[...] = (jnp.dot(xh, rh_ref[...], preferred_element_type=F32)
                       + jnp.dot(xl, rh_ref[...], preferred_element_type=F32)
                       + jnp.dot(xh, rl_ref[...], preferred_element_type=F32))


def router_split(router):
    r = jnp.pad(router, ((0, 0), (0, 128 - router.shape[1])))
    hi = r.astype(BF16)
    return hi, (r - hi.astype(F32)).astype(BF16)


def outproj_ln(x, y_gla, y_nsa, y_pool, w_out_bf16, g, b, router=None):
    n, d = x.shape
    tm = _pick_tile(n, (512, 256, 128, 64, 32, 16, 8))
    row = lambda w: pl.BlockSpec((tm, w), lambda i: (i, 0))
    const = lambda r, c: pl.BlockSpec((r, c), lambda i: (0, 0))
    in_specs = [row(d), row(GLA_WIDTH), row(NSA_WIDTH), row(POOL_WIDTH), const(d, d), const(1, d), const(1, d)]
    args = [x, y_gla, y_nsa, y_pool, w_out_bf16, g.reshape(1, -1), b.reshape(1, -1)]
    out_specs, out_shape = row(d), jax.ShapeDtypeStruct((n, d), F32)
    if router is not None:
        in_specs += [const(d, 128), const(d, 128)]
        args += list(router_split(router))
        out_specs, out_shape = (out_specs, row(128)), (out_shape, jax.ShapeDtypeStruct((n, 128), F32))
    return pl.pallas_call(
        _outproj_kernel,
        grid=(n // tm,),
        in_specs=in_specs,
        out_specs=out_specs,
        out_shape=out_shape,
        compiler_params=_params("arbitrary"),
        name="outproj_ln",
    )(*args)


MOE_TM = 512
MOE_TM_DOWN = 512
MOE_VMEM_BYTES = 60 * 1024 * 1024
MOE_TF = 1024
MOE_TN = 512
ROUTE_TM = 512
PERMUTE_CHUNK = 1024


def _route_kernel(lg_ref, ii_ref, gf_ref, cnt_ref, carry, *, n_valid):
    tm = lg_ref.shape[0]
    i = pl.program_id(0)

    @pl.when(i == 0)
    def _():
        carry[...] = jnp.zeros_like(carry)

    lane = lax.broadcasted_iota(jnp.int32, (tm, 128), 1)
    valid = (i * tm + lax.broadcasted_iota(jnp.int32, (tm, 128), 0)) < n_valid
    lg = jnp.where(lane < N_EXPERTS, lg_ref[...], -jnp.inf)
    m1 = jnp.max(lg, axis=-1, keepdims=True)
    i1 = jnp.min(jnp.where(lg == m1, lane, 128), axis=-1, keepdims=True)
    lg2 = jnp.where(lane == i1, -jnp.inf, lg)
    m2 = jnp.max(lg2, axis=-1, keepdims=True)
    i2 = jnp.min(jnp.where(lg2 == m2, lane, 128), axis=-1, keepdims=True)
    t = jnp.exp(m2 - m1)
    g1 = 1.0 / (1.0 + t)
    g2 = t / (1.0 + t)
    oh1 = jnp.where((lane == i1) & valid, 1.0, 0.0)
    oh2 = jnp.where((lane == i2) & valid, 1.0, 0.0)
    cnt = oh1 + oh2
    strict = (lax.broadcasted_iota(jnp.int32, (tm, tm), 0) > lax.broadcasted_iota(jnp.int32, (tm, tm), 1))
    before = jnp.dot(strict.astype(BF16), cnt.astype(BF16), preferred_element_type=F32) + carry[...]
    r1 = jnp.sum(before * oh1, axis=-1, keepdims=True).astype(jnp.int32)
    r2 = jnp.sum(before * oh2, axis=-1, keepdims=True).astype(jnp.int32)
    carry[...] += jnp.sum(cnt, axis=0, keepdims=True)
    ii_ref[...] = jnp.where(lane == 0, i1, jnp.where(lane == 1, i2, jnp.where(lane == 2, r1, r2)))
    gf_ref[...] = jnp.where(lane == 0, g1, g2)
    cnt_ref[...] = carry[...]


def moe_route(logits, n_valid):
    npad = logits.shape[0]
    tm = ROUTE_TM
    row = pl.BlockSpec((tm, 128), lambda i: (i, 0))
    info, gates, counts = pl.pallas_call(
        functools.partial(_route_kernel, n_valid=n_valid),
        grid=(npad // tm,),
        in_specs=[row],
        out_specs=(row, row, pl.BlockSpec((1, 128), lambda i: (0, 0))),
        out_shape=(jax.ShapeDtypeStruct((npad, 128), jnp.int32), jax.ShapeDtypeStruct((npad, 128), F32),
                   jax.ShapeDtypeStruct((1, 128), F32)),
        scratch_shapes=[pltpu.VMEM((1, 128), F32)],
        compiler_params=_params("arbitrary"),
        name="moe_route",
    )(logits)
    return info[:, 0:2], info[:, 2:4], gates, counts[0, :N_EXPERTS].astype(jnp.int32)


SLAB = (16, 128)


def _slabify_kernel(x_ref, *rest):
    o_ref = rest[-1]

    def slab_rows(src_ref, n_rows):
        for c in range(SLAB[0]):
            o_ref[0:n_rows, c, :] = src_ref[:, c * SLAB[1]:(c + 1) * SLAB[1]]

    if len(rest) == 1:
        slab_rows(x_ref, x_ref.shape[0])
        return
    t_ref = rest[0]
    last = pl.program_id(0) == pl.num_programs(0) - 1

    @pl.when(jnp.logical_not(last))
    def _():
        slab_rows(x_ref, x_ref.shape[0])

    @pl.when(last)
    def _():
        o_ref[...] = jnp.zeros_like(o_ref)
        slab_rows(t_ref, t_ref.shape[0])


def slabify(x, tail=None):
    n, d = x.shape
    tm = _pick_tile(n, (512, 256, 128, 64, 32, 16, 8))
    nt = n // tm
    in_specs = [pl.BlockSpec((tm, d), lambda i: (jnp.minimum(i, nt - 1), 0))]
    args = [x]
    if tail is not None:
        assert tail.shape[0] <= tm
        in_specs.append(pl.BlockSpec(tail.shape, lambda i: (0, 0)))
        args.append(tail)
    steps = nt + (tail is not None)
    return pl.pallas_call(
        _slabify_kernel,
        grid=(steps,),
        in_specs=in_specs,
        out_specs=pl.BlockSpec((tm,) + SLAB, lambda i: (i, 0, 0)),
        out_shape=jax.ShapeDtypeStruct((steps * tm,) + SLAB, x.dtype),
        compiler_params=_params("arbitrary"),
        name="slabify",
    )(*args)


GATHER_UNROLL = 8


def _gather_rows_kernel(idx_ref, src_ref, o_ref, stage, sem):
    ch = idx_ref.shape[2]

    def row_copy(r):
        dst = stage.at[lax.shift_right_logical(r, 3), :, r & 7, :]
        return pltpu.make_async_copy(src_ref.at[idx_ref[0, 0, r]], dst, sem)

    def issue(t, c):
        for u in range(GATHER_UNROLL):
            row_copy(t * GATHER_UNROLL + u).start(priority=u % 2)
        return c

    lax.fori_loop(0, ch // GATHER_UNROLL, issue, 0)

    def drain(t, c):
        for u in range(GATHER_UNROLL):
            row_copy(t * GATHER_UNROLL + u).wait()
        return c

    lax.fori_loop(0, ch // GATHER_UNROLL, drain, 0)
    rows = jnp.concatenate([stage[:, c].reshape(ch, SLAB[1]) for c in range(SLAB[0])], axis=1)
    o_ref[...] = rows.astype(o_ref.dtype)


def gather_rows_bf16(src, idx):
    n = idx.shape[0]
    ch = PERMUTE_CHUNK
    d = SLAB[0] * SLAB[1]
    return pl.pallas_call(
        _gather_rows_kernel,
        grid=(n // ch,),
        in_specs=[pl.BlockSpec((1, 1, ch), lambda i: (i, 0, 0), memory_space=pltpu.SMEM),
                  pl.BlockSpec(memory_space=pl.ANY)],
        out_specs=pl.BlockSpec((ch, d), lambda i: (i, 0)),
        out_shape=jax.ShapeDtypeStruct((n, d), BF16),
        scratch_shapes=[pltpu.VMEM((ch // 8, SLAB[0], 8, SLAB[1]), src.dtype), pltpu.SemaphoreType.DMA(())],
        compiler_params=_params("arbitrary"),
        name="gather_rows",
    )(idx.reshape(n // ch, 1, ch), src)


def _moe_up_kernel(te_ref, tfirst_ref, tused_ref, x_ref, wg_ref, wu_ref, h_ref, wgb, wub):
    i = pl.program_id(1)

    @pl.when((i == 0) | (tfirst_ref[i] == 1))
    def _():
        wgb[...] = wg_ref[...].astype(BF16)
        wub[...] = wu_ref[...].astype(BF16)

    @pl.when(tused_ref[i] == 1)
    def _():
        xb = x_ref[...]
        gate = jnp.dot(xb, wgb[...], preferred_element_type=F32)
        up = jnp.dot(xb, wub[...], preferred_element_type=F32)
        h_ref[...] = (jax.nn.silu(gate) * up).astype(BF16)

    @pl.when(tused_ref[i] == 0)
    def _():
        h_ref[...] = jnp.zeros_like(h_ref)


def _moe_down_kernel(te_ref, tfirst_ref, tused_ref, h_ref, wd_ref, y_ref, wdb, ybuf, sems):
    j, i = pl.program_id(0), pl.program_id(1)
    n_i = pl.num_programs(1)
    n_c, tm = ybuf.shape[1], ybuf.shape[2]
    step = j * n_i + i
    last = pl.num_programs(0) * n_i - 1

    def copies(s, slot):
        jj, ii = s // n_i, s % n_i
        return [pltpu.make_async_copy(ybuf.at[slot, c], y_ref.at[pl.ds(ii * tm, tm), jj * n_c + c, :], sems.at[slot])
                for c in range(n_c)]

    def for_slot(s, fn):
        for slot in range(2):
            @pl.when(s % 2 == slot)
            def _():
                fn(slot)

    @pl.when((i == 0) | (tfirst_ref[i] == 1))
    def _():
        wdb[...] = wd_ref[...].astype(BF16)

    @pl.when(step >= 2)
    def _():
        for_slot(step, lambda slot: [cp.wait() for cp in copies(step - 2, slot)])

    def fill(slot):
        @pl.when(tused_ref[i] == 1)
        def _():
            y = jnp.dot(h_ref[...], wdb[...], preferred_element_type=F32)
            for c in range(n_c):
                ybuf[slot, c] = y[:, c * SLAB[1]:(c + 1) * SLAB[1]]

        @pl.when(tused_ref[i] == 0)
        def _():
            ybuf[slot] = jnp.zeros(ybuf.shape[1:], F32)

        for cp in copies(step, slot):
            cp.start()

    for_slot(step, fill)

    @pl.when(step == last)
    def _():
        for_slot(step - 1, lambda slot: [cp.wait() for cp in copies(step - 1, slot)])
        for_slot(step, lambda slot: [cp.wait() for cp in copies(step, slot)])


def _tile_meta(tile_e, tile_used, split):
    te = jnp.repeat(tile_e, split)
    first = jnp.concatenate([jnp.ones((1,), jnp.int32), (te[1:] != te[:-1]).astype(jnp.int32)])
    return te, first, jnp.repeat(tile_used, split)


def moe_experts(xs, tile_e, tile_used, wg, wu, wd):
    r, d = xs.shape
    d_ff = wg.shape[2]
    tm, tf, tn = MOE_TM, MOE_TF, MOE_TN
    h = pl.pallas_call(
        _moe_up_kernel,
        grid_spec=pltpu.PrefetchScalarGridSpec(
            num_scalar_prefetch=3,
            grid=(d_ff // tf, r // tm),
            in_specs=[pl.BlockSpec((tm, d), lambda j, i, te, t1, tu: (i, 0)),
                      pl.BlockSpec((None, d, tf), lambda j, i, te, t1, tu: (te[i], 0, j)),
                      pl.BlockSpec((None, d, tf), lambda j, i, te, t1, tu: (te[i], 0, j))],
            out_specs=pl.BlockSpec((tm, tf), lambda j, i, te, t1, tu: (i, j)),
            scratch_shapes=[pltpu.VMEM((d, tf), BF16), pltpu.VMEM((d, tf), BF16)]),
        out_shape=jax.ShapeDtypeStruct((r, d_ff), BF16),
        compiler_params=_params("arbitrary", "arbitrary", vmem_limit_bytes=MOE_VMEM_BYTES),
        name="moe_up",
    )(*_tile_meta(tile_e, tile_used, 1), xs, wg, wu)
    tmd = MOE_TM_DOWN
    return pl.pallas_call(
        _moe_down_kernel,
        grid_spec=pltpu.PrefetchScalarGridSpec(
            num_scalar_prefetch=3,
            grid=(d // tn, r // tmd),
            in_specs=[pl.BlockSpec((tmd, d_ff), lambda j, i, te, t1, tu: (i, 0)),
                      pl.BlockSpec((None, d_ff, tn), lambda j, i, te, t1, tu: (te[i], 0, j))],
            out_specs=pl.BlockSpec(memory_space=pl.ANY),
            scratch_shapes=[pltpu.VMEM((d_ff, tn), BF16), pltpu.VMEM((2, tn // SLAB[1], tmd, SLAB[1]), F32),
                            pltpu.SemaphoreType.DMA((2,))]),
        out_shape=jax.ShapeDtypeStruct((r,) + SLAB, F32),
        compiler_params=_params("arbitrary", "arbitrary", vmem_limit_bytes=MOE_VMEM_BYTES),
        name="moe_down",
    )(*_tile_meta(tile_e, tile_used, tm // tmd), h, wd)


def _moe_combine_kernel(idx_ref, x_ref, gt_ref, g_ref, b_ref, y_ref, o_ref, stage, sem):
    tm = x_ref.shape[0]

    def row_copy(n):
        k, r = lax.shift_right_logical(n, tm.bit_length() - 1), n & (tm - 1)
        return pltpu.make_async_copy(y_ref.at[idx_ref[0, 0, n]],
                                     stage.at[k, lax.shift_right_logical(r, 3), :, r & 7, :], sem)

    def issue(t, c):
        for u in range(GATHER_UNROLL):
            row_copy(t * GATHER_UNROLL + u).start(priority=u % 2)
        return c

    lax.fori_loop(0, TOP_K * tm // GATHER_UNROLL, issue, 0)

    def drain(t, c):
        for u in range(GATHER_UNROLL):
            row_copy(t * GATHER_UNROLL + u).wait()
        return c

    lax.fori_loop(0, TOP_K * tm // GATHER_UNROLL, drain, 0)
    rows = lambda k: jnp.concatenate([stage[k, :, c].reshape(tm, SLAB[1]) for c in range(SLAB[0])], axis=1)
    gt = gt_ref[...]
    y = gt[:, 0:1] * rows(0) + gt[:, 1:2] * rows(1)
    o_ref[...] = _layer_norm_rows(ALPHA * x_ref[...] + y, g_ref[...], b_ref[...])


def moe_combine_ln(x, ys, dest, gates, row0, g, b):
    n, d = x.shape
    tm = _pick_tile(n, (512, 256, 128, 64, 32, 16, 8))
    o0 = row0 // tm
    idx = dest[row0:row0 + n].reshape(n // tm, tm, TOP_K).transpose(0, 2, 1).reshape(n // tm, 1, TOP_K * tm)
    return pl.pallas_call(
        _moe_combine_kernel,
        grid=(n // tm,),
        in_specs=[pl.BlockSpec((1, 1, TOP_K * tm), lambda i: (i, 0, 0), memory_space=pltpu.SMEM),
                  pl.BlockSpec((tm, d), lambda i: (i, 0)),
                  pl.BlockSpec((tm, 128), lambda i: (o0 + i, 0)),
                  pl.BlockSpec((1, d), lambda i: (0, 0)), pl.BlockSpec((1, d), lambda i: (0, 0)),
                  pl.BlockSpec(memory_space=pl.ANY)],
        out_specs=pl.BlockSpec((tm, d), lambda i: (i, 0)),
        out_shape=jax.ShapeDtypeStruct((n, d), F32),
        scratch_shapes=[pltpu.VMEM((TOP_K, tm // 8, SLAB[0], 8, SLAB[1]), F32), pltpu.SemaphoreType.DMA(())],
        compiler_params=_params("arbitrary"),
        name="moe_combine_ln",
    )(idx.astype(jnp.int32), x, gates, g.reshape(1, -1), b.reshape(1, -1), ys)


def moe_ln(x_groups, logit_groups, wg, wu, wd, g, b):
    d = x_groups[0].shape[1]
    n_tok = sum(x.shape[0] for x in x_groups)
    n_tok_pad = -(-n_tok // PERMUTE_CHUNK) * PERMUTE_CHUNK
    n_tok_pad = -(-n_tok_pad // ROUTE_TM) * ROUTE_TM
    logits = jnp.concatenate(logit_groups + [jnp.zeros((n_tok_pad - n_tok, 128), F32)], axis=0)
    experts, ranks, gates, counts = moe_route(logits, n_tok)
    tm = MOE_TM
    n_tiles = -(-(n_tok * TOP_K + N_EXPERTS * (tm - 1)) // tm)
    n_tiles = -(-n_tiles * tm // PERMUTE_CHUNK) * PERMUTE_CHUNK // tm
    padded = (counts + tm - 1) // tm * tm
    pad_end = jnp.cumsum(padded)
    pad_start = pad_end - padded
    valid = (jnp.arange(n_tok_pad) < n_tok)[:, None]
    dest = jnp.where(valid, pad_start[experts] + ranks, 0)
    tok = jnp.broadcast_to(jnp.arange(n_tok_pad, dtype=jnp.int32)[:, None], dest.shape)
    row_tok = jnp.zeros((n_tiles * tm,), jnp.int32).at[jnp.where(valid, dest, n_tiles * tm).reshape(-1)].set(
        tok.reshape(-1), mode='drop')
    tile_start = jnp.arange(n_tiles, dtype=jnp.int32) * tm
    tile_e = jnp.minimum(jnp.sum(tile_start[:, None] >= pad_end[None, :], axis=1), N_EXPERTS - 1).astype(jnp.int32)
    tile_used = (tile_start < pad_end[-1]).astype(jnp.int32)
    assert len(x_groups) == 2
    xs = gather_rows_bf16(slabify(x_groups[0], x_groups[1]), row_tok)
    ys = moe_experts(xs, tile_e, tile_used, wg, wu, wd)
    outs, row0 = [], 0
    for x in x_groups:
        outs.append(moe_combine_ln(x, ys, dest, gates, row0, g, b))
        row0 += x.shape[0]
    return outs


def _ffn_kernel(x_ref, wg_ref, wu_ref, wd_ref, g_ref, b_ref, o_ref, xb_ref, *, d_ff):
    j = pl.program_id(1)
    tf = wg_ref.shape[1]

    @pl.when(j == 0)
    def _():
        xb_ref[...] = x_ref[...].astype(BF16)
        o_ref[...] = jnp.zeros_like(o_ref)

    xb = xb_ref[...]
    gate = jnp.dot(xb, wg_ref[...], preferred_element_type=F32)
    up = jnp.dot(xb, wu_ref[...], preferred_element_type=F32)
    col = j * tf + lax.broadcasted_iota(jnp.int32, (1, tf), 1)
    a = jnp.where(col < d_ff, jax.nn.silu(gate) * up, 0.0).astype(BF16)
    rowi = j * tf + lax.broadcasted_iota(jnp.int32, (tf, 1), 0)
    wd = jnp.where(rowi < d_ff, wd_ref[...], jnp.zeros((), BF16))
    o_ref[...] += jnp.dot(a, wd, preferred_element_type=F32)

    @pl.when(j == pl.num_programs(1) - 1)
    def _():
        o_ref[...] = _layer_norm_rows(ALPHA * x_ref[...] + o_ref[...], g_ref[...], b_ref[...])


def ffn_ln(x, wg, wu, wd, g, b):
    n, d = x.shape
    d_ff = wg.shape[1]
    tm = _pick_tile(n, (512, 256, 128, 64, 32, 16, 8))
    tf = 512
    kern = functools.partial(_ffn_kernel, d_ff=d_ff)
    return pl.pallas_call(
        kern,
        grid=(n // tm, pl.cdiv(d_ff, tf)),
        in_specs=[pl.BlockSpec((tm, d), lambda i, j: (i, 0)),
                  pl.BlockSpec((d, tf), lambda i, j: (0, j)), pl.BlockSpec((d, tf), lambda i, j: (0, j)),
                  pl.BlockSpec((tf, d), lambda i, j: (j, 0)),
                  pl.BlockSpec((1, d), lambda i, j: (0, 0)), pl.BlockSpec((1, d), lambda i, j: (0, 0))],
        out_specs=pl.BlockSpec((tm, d), lambda i, j: (i, 0)),
        out_shape=jax.ShapeDtypeStruct((n, d), F32),
        scratch_shapes=[pltpu.VMEM((tm, d), BF16)],
        compiler_params=_params("arbitrary", "arbitrary"),
        name="ffn_ln",
    )(x, wg, wu, wd, g.reshape(1, -1), b.reshape(1, -1))


CMP_PAGES = 32
SEL_PAD_DEC = 384


def _cmp_paged_kernel(pt_ref, cache_ref, pe_ref, w1_ref, w2_ref, o_ref, buf, hlo, hhi, sems, *, layer, n_pages, page):
    b, ch = pl.program_id(0), pl.program_id(1)
    n_ch = pl.num_programs(1)
    hd = HEAD_DIM
    nh = CMP_PAGES * page // CMP_STRIDE
    step = b * n_ch + ch

    def copies(s, slot):
        first = s * CMP_PAGES
        return [pltpu.make_async_copy(
            cache_ref.at[layer, pt_ref[first + pg], :, c // NSA_KV_HEADS, c % NSA_KV_HEADS, :],
            buf.at[slot, c, pl.ds(pg * page, page), :], sems.at[slot])
            for pg in range(CMP_PAGES) for c in range(2 * NSA_KV_HEADS)]

    def for_slot(s, fn):
        for slot in range(2):
            @pl.when(s % 2 == slot)
            def _():
                fn(slot)

    @pl.when(step == 0)
    def _():
        for cp in copies(0, 0):
            cp.start()

    @pl.when(step + 1 < pl.num_programs(0) * n_ch)
    def _():
        for_slot(step + 1, lambda slot: [cp.start() for cp in copies(step + 1, slot)])

    for_slot(step, lambda slot: [cp.wait() for cp in copies(step, slot)])
    cur = step % 2

    for c in range(2 * NSA_KV_HEADS):
        kd, g = c // NSA_KV_HEADS, c % NSA_KV_HEADS
        xs = [buf[cur, c, pl.ds(j, nh, stride=CMP_STRIDE), :] for j in range(CMP_STRIDE)]
        lo = jnp.concatenate([(xs[j] + pe_ref[kd, j:j + 1, :]).astype(BF16) for j in range(CMP_STRIDE)], axis=1)
        hi = jnp.concatenate([(xs[j] + pe_ref[kd, CMP_STRIDE + j:CMP_STRIDE + j + 1, :]).astype(BF16)
                              for j in range(CMP_STRIDE)], axis=1)
        w_lo = w1_ref[kd, 0:CMP_STRIDE].reshape(CMP_STRIDE * hd, CMP_HIDDEN).astype(BF16)
        w_hi = w1_ref[kd, CMP_STRIDE:CMP_LEN].reshape(CMP_STRIDE * hd, CMP_HIDDEN).astype(BF16)
        rows = pl.ds(pl.multiple_of(ch * nh, nh), nh)
        hlo[c, rows, :] = jnp.dot(lo, w_lo, preferred_element_type=F32)
        hhi[c, rows, :] = jnp.dot(hi, w_hi, preferred_element_type=F32)

    @pl.when(ch == pl.num_programs(1) - 1)
    def _():
        n_all = hlo.shape[1]
        for c in range(2 * NSA_KV_HEADS):
            kd, g = c // NSA_KV_HEADS, c % NSA_KV_HEADS
            h = jax.nn.gelu(hlo[c] + pltpu.roll(hhi[c], n_all - 1, 0))
            o_ref[kd, g] = jnp.dot(h.astype(BF16), w2_ref[kd].astype(BF16), preferred_element_type=F32).astype(BF16)


def nsa_compress_paged(cache, layer, page_table, cmp_pos, cmp_w1, cmp_w2):
    page = cache.shape[2]
    n_batch, n_pages = page_table.shape
    hd = HEAD_DIM
    n_all = n_pages * page // CMP_STRIDE
    kern = functools.partial(_cmp_paged_kernel, layer=layer, n_pages=n_pages, page=page)
    const = lambda shape: pl.BlockSpec(shape, lambda b, c, pt: (0,) * len(shape))
    return pl.pallas_call(
        kern,
        grid_spec=pltpu.PrefetchScalarGridSpec(
            num_scalar_prefetch=1,
            grid=(n_batch, n_pages // CMP_PAGES),
            in_specs=[pl.BlockSpec(memory_space=pl.ANY), const((2, CMP_LEN, hd)),
                      const((2, CMP_LEN, hd, CMP_HIDDEN)), const((2, CMP_HIDDEN, hd))],
            out_specs=pl.BlockSpec((None, 2, NSA_KV_HEADS, n_all, hd), lambda b, c, pt: (b, 0, 0, 0, 0)),
            scratch_shapes=[pltpu.VMEM((2, 2 * NSA_KV_HEADS, CMP_PAGES * page, hd), F32),
                            pltpu.VMEM((2 * NSA_KV_HEADS, n_all, CMP_HIDDEN), F32),
                            pltpu.VMEM((2 * NSA_KV_HEADS, n_all, CMP_HIDDEN), F32),
                            pltpu.SemaphoreType.DMA((2,))]),
        out_shape=jax.ShapeDtypeStruct((n_batch, 2, NSA_KV_HEADS, n_all, hd), BF16),
        compiler_params=_params("arbitrary", "arbitrary"),
        name="nsa_compress_paged",
    )(page_table.reshape(-1).astype(jnp.int32), cache, cmp_pos, cmp_w1, cmp_w2)


def _sel_decode_kernel(qn_ref, kc_ref, vc_ref, ovt_ref, oc_ref, idx_ref, v_scr, psum_scr, *, n_cmp, n_sel, n_top,
                       q_pos):
    b = pl.program_id(0)
    n_cmp_pad = kc_ref.shape[2]
    nsp = ovt_ref.shape[0]
    n_rows = qn_ref.shape[0]
    col = lax.broadcasted_iota(jnp.int32, (n_rows, n_cmp_pad), 1)
    mask = (col * CMP_STRIDE + (CMP_LEN - 1) <= q_pos) & (col < n_cmp)
    psums = []
    for g in range(NSA_KV_HEADS):
        own = slice(g * NSA_REP, (g + 1) * NSA_REP)
        s = lax.dot_general(qn_ref[...], kc_ref[0, g], (((1,), (1,)), ((), ())), preferred_element_type=F32)
        s = jnp.where(mask, s, -jnp.inf)
        m = jnp.max(s, axis=-1, keepdims=True)
        m = jnp.where(m > -jnp.inf, m, 0.0)
        p = jnp.where(mask, jnp.exp(s - m), 0.0)
        p = p / jnp.maximum(jnp.sum(p, axis=-1, keepdims=True), 1e-30)
        oc = jnp.dot(p.astype(BF16), vc_ref[0, g], preferred_element_type=F32)
        oc_ref[own, :] = oc[own]
        psum_scr[pl.ds(b * NSA_KV_HEADS + g, 1), :] = jnp.sum(p[own], axis=0, keepdims=True)

    @pl.when(b == pl.num_programs(0) - 1)
    def _():
        n_col = psum_scr.shape[0]
        imp = lax.dot_general(ovt_ref[...], psum_scr[...], (((1,), (1,)), ((), ())), preferred_element_type=F32,
                              precision=lax.Precision.HIGHEST)
        blk = lax.broadcasted_iota(jnp.int32, (nsp, n_col), 0)
        cur = q_pos // SEL_BLOCK
        forced = (blk == 0) | (blk == cur) | (blk == cur - 1)
        v = jnp.where((blk <= cur) & (blk < n_sel), jnp.where(forced, jnp.inf, imp), -jnp.inf)
        v_scr[...] = v

        def count(i, rank):
            vi = v_scr[pl.ds(i, 1), :]
            ahead = (vi > v) | ((vi == v) & (blk > i))
            return rank + ahead.astype(jnp.int32)

        rank = lax.fori_loop(0, n_sel, count, jnp.zeros((nsp, n_col), jnp.int32))
        chosen = (rank < n_top) & (v > -jnp.inf)
        blk_f = blk.astype(F32)
        rows = [jnp.sum(jnp.where(chosen & (rank == t), blk_f, 0.0), axis=0, keepdims=True) for t in range(n_top)]
        idx_ref[...] = jnp.concatenate(rows, axis=0).astype(jnp.int32)


def nsa_select_decode(qn, cmp_kv, n_cmp, n_sel, q_pos):
    n_batch, n_heads, hd = qn.shape
    n_cmp_pad = cmp_kv.shape[3]
    nsp = SEL_PAD_DEC
    n_top = min(SEL_TOPN, n_sel)
    ci = jnp.arange(n_cmp_pad)[None, :]
    sj = jnp.arange(nsp)[:, None]
    overlap_t = ((ci * CMP_STRIDE <= sj * SEL_BLOCK + SEL_BLOCK - 1) &
                 (ci * CMP_STRIDE + CMP_LEN - 1 >= sj * SEL_BLOCK) & (ci < n_cmp) & (sj < n_sel)).astype(F32)
    kern = functools.partial(_sel_decode_kernel, n_cmp=n_cmp, n_sel=n_sel, n_top=n_top, q_pos=q_pos)
    kv = lambda kd: pl.BlockSpec((None, 1, NSA_KV_HEADS, n_cmp_pad, hd), lambda b: (b, kd, 0, 0, 0))
    n_col = n_batch * NSA_KV_HEADS
    o_cmp, idx = pl.pallas_call(
        kern,
        grid=(n_batch,),
        in_specs=[pl.BlockSpec((None, 2 * n_heads, hd), lambda b: (b, 0, 0)), kv(0), kv(1),
                  pl.BlockSpec((nsp, n_cmp_pad), lambda b: (0, 0))],
        out_specs=(pl.BlockSpec((None, n_heads, hd), lambda b: (b, 0, 0)),
                   pl.BlockSpec((n_top, n_col), lambda b: (0, 0))),
        out_shape=(jax.ShapeDtypeStruct((n_batch, n_heads, hd), F32),
                   jax.ShapeDtypeStruct((n_top, n_col), jnp.int32)),
        scratch_shapes=[pltpu.VMEM((nsp, n_col), F32), pltpu.VMEM((n_col, n_cmp_pad), F32)],
        compiler_params=_params("arbitrary"),
        name="nsa_select_decode",
    )(jnp.pad(qn, ((0, 0), (0, n_heads), (0, 0))), cmp_kv, cmp_kv, overlap_t)
    return o_cmp, idx.T.reshape(n_batch, NSA_KV_HEADS, n_top)


def _attn_decode_kernel(pt_ref, sel_ref, q_ref, cache_ref, new_ref, kw_ref, vw_ref, wnew_ref, oc_ref, gate_ref,
                        o_ref, kbuf, vbuf, sem, *, layer, n_pages, per_page, n_top):
    b = pl.program_id(0)
    G = NSA_KV_HEADS
    n_past_blocks = n_pages * per_page
    nt = (((1,), (1,)), ((), ()))

    def block_id(g, slot):
        return sel_ref[(b * G + g) * n_top + slot]

    def copies():
        out = []
        for g in range(G):
            for slot in range(n_top):
                j = jnp.minimum(block_id(g, slot), n_past_blocks - 1)
                rows = pl.ds((j % per_page) * SEL_BLOCK, SEL_BLOCK)
                page = pt_ref[b * n_pages + j // per_page]
                for kind, buf in ((2, kbuf), (3, vbuf)):
                    out.append(pltpu.make_async_copy(cache_ref.at[layer, page, rows, kind, g, :],
                                                     buf.at[g, pl.ds(slot * SEL_BLOCK, SEL_BLOCK), :], sem))
        return out

    for cp in copies():
        cp.start()
    for cp in copies():
        cp.wait()

    for g in range(G):
        qb = q_ref[g]
        q = qb.astype(F32)
        n_rows = qb.shape[0]
        s = lax.dot_general(qb, kbuf[g].astype(BF16), nt, preferred_element_type=F32)
        slot_of = lax.broadcasted_iota(jnp.int32, s.shape, 1) // SEL_BLOCK
        for slot in range(n_top):
            s = jnp.where((slot_of == slot) & (block_id(g, slot) >= n_past_blocks), NEG_BIG, s)
        k_new = new_ref[2 * G + g:2 * G + g + 1, :]
        v_new = new_ref[3 * G + g:3 * G + g + 1, :]
        s_new = jnp.sum(q * k_new, axis=-1, keepdims=True)
        m = jnp.maximum(jnp.max(s, axis=-1, keepdims=True), s_new)
        p = jnp.exp(s - m)
        p_new = jnp.exp(s_new - m)
        l = jnp.sum(p, axis=-1, keepdims=True) + p_new
        o_sel = (jnp.dot(p.astype(BF16), vbuf[g].astype(BF16), preferred_element_type=F32) + p_new * v_new) / l

        n_buf = kw_ref.shape[0]
        s = lax.dot_general(qb, kw_ref[:, g, :].astype(BF16), nt, preferred_element_type=F32)
        keep = lax.broadcasted_iota(jnp.int32, s.shape, 1) > n_buf - WINDOW
        s = jnp.where(keep, s, NEG_BIG)
        kw_new = wnew_ref[g:g + 1, :]
        vw_new = wnew_ref[G + g:G + g + 1, :]
        s_new = jnp.sum(q * kw_new, axis=-1, keepdims=True)
        m = jnp.maximum(jnp.max(s, axis=-1, keepdims=True), s_new)
        p = jnp.exp(s - m)
        p_new = jnp.exp(s_new - m)
        l = jnp.sum(p, axis=-1, keepdims=True) + p_new
        o_win = (jnp.dot(p.astype(BF16), vw_ref[:, g, :].astype(BF16), preferred_element_type=F32)
                 + p_new * vw_new) / l
        gates = jnp.broadcast_to(gate_ref[g:g + 1, :], (n_rows, 128))
        lane = lax.broadcasted_iota(jnp.int32, (n_rows, 128), 1)
        head = lax.broadcasted_iota(jnp.int32, (n_rows, 128), 0)
        pick = lambda c: jnp.sum(jnp.where(lane == head * 3 + c, gates, 0.0), axis=-1, keepdims=True)
        o_ref[g] = (pick(0) * oc_ref[g] + pick(1) * o_sel + pick(2) * o_win).astype(o_ref.dtype)


def nsa_attend_decode(qr, o_cmp, sel_idx, gates, cache, layer, page_table, new_rows, state_win, new_win):
    n_batch, n_heads, hd = qr.shape
    page = cache.shape[2]
    n_pages = page_table.shape[1]
    n_top = sel_idx.shape[2]
    per_page = page // SEL_BLOCK
    n_buf = state_win.shape[2]
    G, R = NSA_KV_HEADS, NSA_REP
    rp = 16
    pad_heads = lambda a: jnp.pad(a.reshape(n_batch, G, R, hd), ((0, 0), (0, 0), (0, rp - R), (0, 0)))
    per_bg = lambda: pl.BlockSpec((None, G, rp, hd), lambda b, pt, sel: (b, 0, 0, 0))
    per_b = lambda rows: pl.BlockSpec((None, rows, hd), lambda b, pt, sel: (b, 0, 0))
    win_spec = lambda kv: pl.BlockSpec((None, None, n_buf, None, G, hd), lambda b, pt, sel: (layer, b, 0, kv, 0, 0))
    kern = functools.partial(_attn_decode_kernel, layer=layer, n_pages=n_pages, per_page=per_page, n_top=n_top)
    out = pl.pallas_call(
        kern,
        grid_spec=pltpu.PrefetchScalarGridSpec(
            num_scalar_prefetch=2,
            grid=(n_batch,),
            in_specs=[per_bg(), pl.BlockSpec(memory_space=pl.ANY), per_b(4 * G), win_spec(0), win_spec(1),
                      per_b(2 * G), per_bg(), per_b(G)],
            out_specs=per_bg(),
            scratch_shapes=[pltpu.VMEM((G, n_top * SEL_BLOCK, hd), F32), pltpu.VMEM((G, n_top * SEL_BLOCK, hd), F32),
                            pltpu.SemaphoreType.DMA(())]),
        out_shape=jax.ShapeDtypeStruct((n_batch, G, rp, hd), BF16),
        compiler_params=_params("arbitrary"),
        name="nsa_attend_decode",
    )(page_table.reshape(-1).astype(jnp.int32), sel_idx.reshape(-1).astype(jnp.int32),
      pad_heads(qr), cache, new_rows, state_win, state_win, new_win, pad_heads(o_cmp), gates)
    return out[:, :, :R].reshape(n_batch, n_heads * hd)


def _gla_decode_kernel(q_ref, k_ref, v_ref, r_ref, small_ref, w2_ref, b2_ref, ng_ref, s0_ref, y_ref, sf_ref):
    dk, dv = GLA_DK, GLA_DV
    z = jnp.dot(small_ref[:, 0:GLA_RANK].astype(BF16), w2_ref[...].astype(BF16),
                preferred_element_type=F32) + b2_ref[...]
    g_all = (jnp.minimum(z, 0.0) - jnp.log1p(jnp.exp(-jnp.abs(z)))) / GLA_GATE_NORM
    eye = lax.broadcasted_iota(jnp.int32, (dk, dk), 0) == lax.broadcasted_iota(jnp.int32, (dk, dk), 1)
    column = lambda row: jnp.sum(jnp.where(eye, jnp.broadcast_to(row, (dk, dk)), 0.0), axis=1, keepdims=True)
    for b in range(q_ref.shape[0]):
        for h in range(GLA_HEADS):
            ks = slice(h * dk, (h + 1) * dk)
            vs = slice(h * dv, (h + 1) * dv)
            s_new = (jnp.exp(column(g_all[b:b + 1, ks])) * s0_ref[b, h]
                     + column(k_ref[b:b + 1, ks]) * v_ref[b:b + 1, vs])
            sf_ref[b, h] = s_new
            o = jnp.sum(column(q_ref[b:b + 1, ks] * (dk ** -0.5)) * s_new, axis=0, keepdims=True)
            o = o * lax.rsqrt(jnp.mean(o * o, axis=-1, keepdims=True) + RMS_EPS)
            y_ref[b:b + 1, vs] = (o * ng_ref[:, vs] * jax.nn.silu(r_ref[b:b + 1, vs])).astype(BF16)


def gla_decode(p, s0, w2, b2, norm_g):
    n = p.shape[0]
    row = lambda w, off: pl.BlockSpec((n, w), lambda i: (0, off // w))
    const = lambda shape: pl.BlockSpec(shape, lambda i: (0,) * len(shape))
    return pl.pallas_call(
        _gla_decode_kernel,
        grid=(1,),
        in_specs=[row(256, COL_GLA_Q), row(256, COL_GLA_K), row(512, COL_GLA_V), row(512, COL_GLA_R),
                  row(128, COL_SMALL), const((GLA_RANK, GLA_HEADS * GLA_DK)), const((1, GLA_HEADS * GLA_DK)),
                  const((1, GLA_WIDTH)), const(s0.shape)],
        out_specs=(const((n, GLA_WIDTH)), const(s0.shape)),
        out_shape=(jax.ShapeDtypeStruct((n, GLA_WIDTH), BF16), jax.ShapeDtypeStruct(s0.shape, F32)),
        compiler_params=_params("arbitrary"),
        name="gla_decode",
    )(p, p, p, p, p, w2, b2.reshape(1, -1), norm_g.reshape(1, -1), s0)


def _pool_decode_kernel(u_ref, prev_ref, w_ref, sc_ref, y_ref, *, past_len):
    gd = POOL_GROUP_DIM
    n_prev = prev_ref.shape[1]
    for gi, w in enumerate(POOL_WINDOWS):
        cols = slice(gi * gd, (gi + 1) * gd)
        x = u_ref[:, cols]
        s = x
        for r in range(n_prev - (w - 1), n_prev):
            s = s + prev_ref[:, r, cols]
        pooled = s / float(min(past_len + 1, w)) - x
        y = jnp.dot(pooled.astype(BF16), w_ref[gi].astype(BF16), preferred_element_type=F32)
        y_ref[:, cols] = (y * sc_ref[:, cols]).astype(BF16)


def pool_decode(p, prev, past_len, w_pool, scale):
    n = p.shape[0]
    const = lambda shape: pl.BlockSpec(shape, lambda i: (0,) * len(shape))
    return pl.pallas_call(
        functools.partial(_pool_decode_kernel, past_len=past_len),
        grid=(1,),
        in_specs=[pl.BlockSpec((n, POOL_WIDTH), lambda i: (0, COL_POOL // POOL_WIDTH)), const(prev.shape),
                  const(w_pool.shape), const((1, POOL_WIDTH))],
        out_specs=const((n, POOL_WIDTH)),
        out_shape=jax.ShapeDtypeStruct((n, POOL_WIDTH), BF16),
        compiler_params=_params("arbitrary"),
        name="pool_decode",
    )(p, prev, w_pool, scale.reshape(1, -1))


def decode_mixer(xs2, w_in_packed, w_out_bf16, ln_g, ln_b, gla_w2, gla_b, gla_norm_g, cmp_pos, cmp_w1, cmp_w2,
                 pool_w, pool_scale, cache, layer, page_table, state_win, state_gla, state_pool, past_len,
                 router=None):
    n_dec = xs2.shape[0]
    hd, G = HEAD_DIM, NSA_KV_HEADS
    p = matmul(xs2, w_in_packed)
    pos = jnp.full((n_dec,), past_len, jnp.int32)
    rows, win, qn, qr, _, _, _, _, gates = nsa_prep(p, rope_tables(pos), 1, n_dec)
    cmp_kv = nsa_compress_paged(cache, layer, page_table, cmp_pos, cmp_w1, cmp_w2)
    t_k = past_len + 1
    n_sel = -(-t_k // SEL_BLOCK)
    o_cmp, sel_idx = nsa_select_decode(qn[0].transpose(1, 0, 2), cmp_kv, past_len // CMP_STRIDE - 1, n_sel, past_len)
    y_nsa = nsa_attend_decode(qr[0].transpose(1, 0, 2), o_cmp, sel_idx, gates[0].transpose(1, 0, 2), cache, layer,
                              page_table, rows.reshape(n_dec, 4 * G, hd), state_win, win.reshape(n_dec, 2 * G, hd))
    y_gla, s_gla = gla_decode(p, state_gla, gla_w2, gla_b, gla_norm_g)
    y_pool = pool_decode(p, state_pool, past_len, pool_w, pool_scale)
    pool_rows = jnp.concatenate([state_pool[:, 1:], p[:, None, COL_POOL:COL_POOL + POOL_WIDTH]], axis=1)
    x1 = outproj_ln(xs2, y_gla, y_nsa, y_pool, w_out_bf16, ln_g, ln_b, router)
    nsa_rows = rows.reshape(n_dec, 1, 4, G, hd)
    new_win = jnp.concatenate([state_win[layer, :, 1:], win.reshape(n_dec, 1, 2, G, hd)], axis=1)
    return x1, nsa_rows, new_win, s_gla, pool_rows


def prompt_mixer(x2, w_in_packed, w_out_bf16, ln_g, ln_b, gla_w2, gla_b, gla_norm_g, cmp_pos, cmp_w1, cmp_w2,
                 pool_w, pool_scale, n_batch, t_len, router=None):
    p = matmul(x2, w_in_packed)
    pos = jnp.arange(t_len, dtype=jnp.int32)
    rows, win, qn, qr, ks, vs, kw, vw, gates = nsa_prep(p, rope_tables(pos), n_batch, t_len)
    cmp_kv = nsa_compress_prompt(rows, cmp_pos, cmp_w1, cmp_w2, n_batch, t_len)
    o_cmp, selb = nsa_select(qn, cmp_kv, t_len // CMP_STRIDE - 1, t_len)
    y_nsa = nsa_attend(qr, o_cmp, selb, gates, ks, vs, kw, vw)
    s0 = jnp.zeros((n_batch, GLA_HEADS, GLA_DK, GLA_DV), F32)
    y_gla, s_gla = gla_mix(p, s0, gla_w2, gla_b, gla_norm_g, n_batch, t_len)
    prev = jnp.zeros((n_batch, POOL_MAX - 1, POOL_WIDTH), F32)
    y_pool = pool_mix(p, prev, 0, pool_w, pool_scale, n_batch, t_len)
    x1 = outproj_ln(x2, y_gla, y_nsa, y_pool, w_out_bf16, ln_g, ln_b, router)
    nsa_rows = rows.reshape(n_batch, t_len, 4, NSA_KV_HEADS, HEAD_DIM)
    n_win = min(WINDOW, t_len)
    win_rows = win.reshape(n_batch, t_len, 2, NSA_KV_HEADS, HEAD_DIM)[:, t_len - n_win:]
    pool_rows = p.reshape(n_batch, t_len, PACKED_WIDTH)[:, t_len - (POOL_MAX - 1):, COL_POOL:COL_POOL + POOL_WIDTH]
    return x1, nsa_rows, win_rows, s_gla, pool_rows


def kernel(x_prompt, x_sample, cache_nsa, page_table, state_win, state_gla, state_pool, w_in, gla_gate_w2, gla_gate_b, gla_norm_g, nsa_cmp_pos, nsa_cmp_w1, nsa_cmp_w2, pool_w, pool_scale, w_out, ln1_g, ln1_b, ln2_g, ln2_b, ffn_w_gate, ffn_w_up, ffn_w_down, moe_router, moe_w_gate, moe_w_up, moe_w_down):
    n_prompt, t_len, d = x_prompt.shape
    n_dec = x_sample.shape[0]
    xp, xs = x_prompt.reshape(n_prompt * t_len, d), x_sample.reshape(n_dec, d)
    nsa_p, nsa_s, win_p, win_s, gla_p, gla_s, pool_p, pool_s = [], [], [], [], [], [], [], []
    for l in range(DEPTH):
        i = l // 2
        router = moe_router[i] if l % 2 else None
        w_in_packed, w_out_bf16 = pack_w_in(w_in[l]), w_out[l].astype(BF16)
        lw = (w_in_packed, w_out_bf16, ln1_g[l], ln1_b[l], gla_gate_w2[l], gla_gate_b[l], gla_norm_g[l],
              nsa_cmp_pos[l], nsa_cmp_w1[l], nsa_cmp_w2[l], pool_w[l], pool_scale[l])
        xp, r_p, w_p, g_p, p_p = prompt_mixer(xp, *lw, n_prompt, t_len, router)
        xs, r_s, w_s, g_s, p_s = decode_mixer(xs, *lw, cache_nsa, l, page_table, state_win, state_gla[l],
                                              state_pool[l], PAST_LEN, router)
        if l % 2 == 0:
            wg, wu, wd = ffn_w_gate[i].astype(BF16), ffn_w_up[i].astype(BF16), ffn_w_down[i].astype(BF16)
            xp = ffn_ln(xp, wg, wu, wd, ln2_g[l], ln2_b[l])
            xs = ffn_ln(xs, wg, wu, wd, ln2_g[l], ln2_b[l])
        else:
            (xp, lg_p), (xs, lg_s) = xp, xs
            xp, xs = moe_ln([xp, xs], [lg_p, lg_s], moe_w_gate[i], moe_w_up[i], moe_w_down[i], ln2_g[l], ln2_b[l])
        nsa_p.append(r_p); nsa_s.append(r_s); win_p.append(w_p); win_s.append(w_s)
        gla_p.append(g_p); gla_s.append(g_s); pool_p.append(p_p); pool_s.append(p_s)
    return (xp.reshape(n_prompt, t_len, d), xs.reshape(x_sample.shape), jnp.stack(nsa_p), jnp.stack(nsa_s),
            jnp.stack(win_p), jnp.stack(win_s), jnp.stack(gla_p), jnp.stack(gla_s), jnp.stack(pool_p),
            jnp.stack(pool_s))
```

```python
import functools

import jax
import jax.numpy as jnp
import numpy as np
from jax import lax
from jax.experimental import pallas as pl
from jax.experimental.pallas import tpu as pltpu

D_MODEL = 2048
DEPTH = 2
PAST_LEN = 16384
HEAD_DIM = 128
GLA_HEADS = 4
GLA_DK = 64
GLA_DV = 128
GLA_RANK = 16
GLA_GATE_NORM = 16.0
GLA_CHUNK = 64
GLA_WIDTH = GLA_HEADS * GLA_DV
NSA_HEADS = 8
NSA_KV_HEADS = 2
NSA_REP = NSA_HEADS // NSA_KV_HEADS
NSA_WIDTH = NSA_HEADS * HEAD_DIM
CMP_LEN = 32
CMP_STRIDE = 16
CMP_HIDDEN = 128
SEL_BLOCK = 64
SEL_TOPN = 16
WINDOW = 512
POOL_GROUPS = 4
POOL_GROUP_DIM = 128
POOL_WIDTH = POOL_GROUPS * POOL_GROUP_DIM
POOL_WINDOWS = (2, 4, 8, 16)
POOL_MAX = 16
ROPE_THETA = 500000.0
ROPE_DIM = HEAD_DIM // 4
N_EXPERTS = 8
TOP_K = 2
ALPHA = (2 * DEPTH) ** 0.25
LN_EPS = 1e-5
RMS_EPS = 1e-6

PROJ_SIZES = (
    ('gla_q', GLA_HEADS * GLA_DK), ('gla_k', GLA_HEADS * GLA_DK), ('gla_v', GLA_HEADS * GLA_DV),
    ('gla_glr', GLA_RANK), ('gla_r', GLA_HEADS * GLA_DV),
    ('nsa_q', NSA_HEADS * HEAD_DIM),
    ('cmp_k', NSA_KV_HEADS * HEAD_DIM), ('cmp_v', NSA_KV_HEADS * HEAD_DIM),
    ('slc_k', NSA_KV_HEADS * HEAD_DIM), ('slc_v', NSA_KV_HEADS * HEAD_DIM),
    ('win_k', NSA_KV_HEADS * HEAD_DIM), ('win_v', NSA_KV_HEADS * HEAD_DIM),
    ('nsa_gate', 3 * NSA_HEADS),
    ('pool', POOL_WIDTH),
)

GLA_SUB = 16
SEL_PAD = 128

BF16 = jnp.bfloat16
F32 = jnp.float32
NEG_BIG = -1e30
VMEM_LIMIT_BYTES = 56 * 1024 * 1024

COL_NSA_Q = 0
COL_ROWS = 1024
COL_WIN = 2048
COL_POOL = 2560
COL_GLA_V = 3072
COL_GLA_R = 3584
COL_GLA_Q = 4096
COL_GLA_K = 4352
COL_SMALL = 4608
PACKED_WIDTH = 4736
SMALL_GATE_OFF = GLA_RANK


def _params(*sem, vmem_limit_bytes=VMEM_LIMIT_BYTES):
    return pltpu.CompilerParams(dimension_semantics=sem, vmem_limit_bytes=vmem_limit_bytes)


def _proj_offsets():
    out, off = {}, 0
    for name, size in PROJ_SIZES:
        out[name] = (off, size)
        off += size
    return out


def pack_w_in(w):
    offs = _proj_offsets()
    sl = lambda n: w[:, offs[n][0]:offs[n][0] + offs[n][1]]
    pad = jnp.zeros((w.shape[0], 128 - GLA_RANK - 3 * NSA_HEADS), w.dtype)
    cols = [sl('nsa_q'), sl('cmp_k'), sl('cmp_v'), sl('slc_k'), sl('slc_v'), sl('win_k'), sl('win_v'),
            sl('pool'), sl('gla_v'), sl('gla_r'), sl('gla_q'), sl('gla_k'), sl('gla_glr'), sl('nsa_gate'), pad]
    return jnp.concatenate(cols, axis=1).astype(BF16)


def rope_tables(pos):
    half = ROPE_DIM // 2
    inv_freq = ROPE_THETA ** (-jnp.arange(half, dtype=F32) / half)
    ang = pos.astype(F32)[:, None] * inv_freq[None, :]
    cos, sin = jnp.cos(ang), jnp.sin(ang)
    t = pos.shape[0]
    c = jnp.concatenate([cos, cos, jnp.ones((t, HEAD_DIM - ROPE_DIM), F32)], axis=1)
    sa = jnp.concatenate([-sin, jnp.zeros((t, HEAD_DIM - half), F32)], axis=1)
    sb = jnp.concatenate([jnp.zeros((t, half), F32), sin, jnp.zeros((t, HEAD_DIM - ROPE_DIM), F32)], axis=1)
    return c, sa, sb


def _pick_tile(n, pref):
    for t in pref:
        if n % t == 0:
            return t
    return n


def _mm_kernel(x_ref, w_ref, o_ref, xb_ref):
    @pl.when(pl.program_id(1) == 0)
    def _():
        xb_ref[...] = x_ref[...].astype(BF16)

    o_ref[...] = jnp.dot(xb_ref[...], w_ref[...].astype(BF16), preferred_element_type=F32)


def matmul(x, w):
    m, k = x.shape
    n = w.shape[1]
    tm = _pick_tile(m, tuple(t for t in (1024, 512, 256, 128, 64, 32, 16, 8) if t * k <= 2048 * 1024))
    tn = 512 if n >= 512 else n
    return pl.pallas_call(
        _mm_kernel,
        grid=(m // tm, pl.cdiv(n, tn)),
        in_specs=[pl.BlockSpec((tm, k), lambda i, j: (i, 0)),
                  pl.BlockSpec((k, tn), lambda i, j: (0, j))],
        out_specs=pl.BlockSpec((tm, tn), lambda i, j: (i, j)),
        out_shape=jax.ShapeDtypeStruct((m, n), F32),
        scratch_shapes=[pltpu.VMEM((tm, k), BF16)],
        compiler_params=_params("arbitrary", "arbitrary"),
        name="matmul",
    )(x, w)


def _rope(x, c, sa, sb):
    return x * c + pltpu.roll(x, HEAD_DIM - ROPE_DIM // 2, 1) * sa + pltpu.roll(x, ROPE_DIM // 2, 1) * sb


def _nsa_prep_kernel(q_ref, rows_ref, win_ref, small_ref, c_ref, sa_ref, sb_ref,
                     rows_o, win_o, qn_o, qr_o, ks_o, vs_o, kw_o, vw_o, gate_o):
    c, sa, sb = c_ref[...], sa_ref[...], sb_ref[...]
    scale = HEAD_DIM ** -0.5
    hd = HEAD_DIM
    for h in range(NSA_HEADS):
        x = q_ref[:, h * hd:(h + 1) * hd]
        qn_o[0, h] = (x * scale).astype(BF16)
        qr_o[0, h] = (_rope(x, c, sa, sb) * scale).astype(BF16)
    ones = jnp.ones((q_ref.shape[0], hd), BF16)
    rows_o[:, 0:4 * hd] = rows_ref[:, 0:4 * hd]
    for g in range(NSA_KV_HEADS):
        k = _rope(rows_ref[:, (4 + g) * hd:(5 + g) * hd], c, sa, sb)
        rows_o[:, (4 + g) * hd:(5 + g) * hd] = k
        ks_o[0, g] = k.astype(BF16)
        v = rows_ref[:, (6 + g) * hd:(7 + g) * hd]
        rows_o[:, (6 + g) * hd:(7 + g) * hd] = v
        vs_o[0, g, :, 0:hd] = v.astype(BF16)
        vs_o[0, g, :, hd:2 * hd] = ones
        k = _rope(win_ref[:, g * hd:(g + 1) * hd], c, sa, sb)
        win_o[:, g * hd:(g + 1) * hd] = k
        kw_o[0, g] = k.astype(BF16)
        v = win_ref[:, (2 + g) * hd:(3 + g) * hd]
        win_o[:, (2 + g) * hd:(3 + g) * hd] = v
        vw_o[0, g, :, 0:hd] = v.astype(BF16)
        vw_o[0, g, :, hd:2 * hd] = ones
    sig = jax.nn.sigmoid(small_ref[...])
    per_g = 3 * NSA_REP
    for g in range(NSA_KV_HEADS):
        gate_o[0, g] = pltpu.roll(sig, 128 - SMALL_GATE_OFF - g * per_g, 1)


def nsa_prep(p, tables, n_batch, t_len):
    tr = _pick_tile(t_len, (512, 256, 128, 64, 32, 16))
    nt = t_len // tr
    n = n_batch * t_len
    hd = HEAD_DIM
    row = lambda w, cb: pl.BlockSpec((tr, w), lambda b, i: (b * nt + i, cb))
    tab = pl.BlockSpec((tr, hd), lambda b, i: (i, 0))
    head = lambda nh, w: pl.BlockSpec((1, nh, tr, w), lambda b, i: (b, 0, i, 0))
    out_shape = (
        jax.ShapeDtypeStruct((n, 8 * hd), F32),
        jax.ShapeDtypeStruct((n, 4 * hd), F32),
        jax.ShapeDtypeStruct((n_batch, NSA_HEADS, t_len, hd), BF16),
        jax.ShapeDtypeStruct((n_batch, NSA_HEADS, t_len, hd), BF16),
        jax.ShapeDtypeStruct((n_batch, NSA_KV_HEADS, t_len, hd), BF16),
        jax.ShapeDtypeStruct((n_batch, NSA_KV_HEADS, t_len, 2 * hd), BF16),
        jax.ShapeDtypeStruct((n_batch, NSA_KV_HEADS, t_len, hd), BF16),
        jax.ShapeDtypeStruct((n_batch, NSA_KV_HEADS, t_len, 2 * hd), BF16),
        jax.ShapeDtypeStruct((n_batch, NSA_KV_HEADS, t_len, 128), F32),
    )
    return pl.pallas_call(
        _nsa_prep_kernel,
        grid=(n_batch, nt),
        in_specs=[row(8 * hd, COL_NSA_Q // (8 * hd)), row(8 * hd, COL_ROWS // (8 * hd)),
                  row(4 * hd, COL_WIN // (4 * hd)), row(128, COL_SMALL // 128), tab, tab, tab],
        out_specs=(row(8 * hd, 0), row(4 * hd, 0), head(NSA_HEADS, hd), head(NSA_HEADS, hd),
                   head(NSA_KV_HEADS, hd), head(NSA_KV_HEADS, 2 * hd), head(NSA_KV_HEADS, hd),
                   head(NSA_KV_HEADS, 2 * hd), head(NSA_KV_HEADS, 128)),
        out_shape=out_shape,
        compiler_params=_params("arbitrary", "arbitrary"),
        name="nsa_prep",
    )(p, p, p, p, *tables)


def _nsa_cmp_kernel(x_ref, pe_ref, w1_ref, w2_ref, o_ref):
    nh = o_ref.shape[0]
    h_lo = jnp.zeros((nh, CMP_HIDDEN), F32)
    h_hi = jnp.zeros((nh, CMP_HIDDEN), F32)
    for j in range(CMP_STRIDE):
        xj = x_ref[pl.ds(j, nh, stride=CMP_STRIDE), :]
        h_lo += jnp.dot((xj + pe_ref[j:j + 1, :]).astype(BF16), w1_ref[j].astype(BF16), preferred_element_type=F32)
        h_hi += jnp.dot((xj + pe_ref[CMP_STRIDE + j:CMP_STRIDE + j + 1, :]).astype(BF16),
                        w1_ref[CMP_STRIDE + j].astype(BF16), preferred_element_type=F32)
    h = jax.nn.gelu(h_lo + pltpu.roll(h_hi, nh - 1, 0))
    o_ref[...] = jnp.dot(h.astype(BF16), w2_ref[...].astype(BF16), preferred_element_type=F32).astype(BF16)


def nsa_compress_prompt(rows, cmp_pos, cmp_w1, cmp_w2, n_batch, t_len):
    nh = t_len // CMP_STRIDE
    hd = HEAD_DIM
    rows3 = rows.reshape(n_batch, t_len, 8 * hd)
    return pl.pallas_call(
        _nsa_cmp_kernel,
        grid=(n_batch, 2, NSA_KV_HEADS),
        in_specs=[pl.BlockSpec((None, t_len, hd), lambda b, kd, g: (b, 0, kd * NSA_KV_HEADS + g)),
                  pl.BlockSpec((None, CMP_LEN, hd), lambda b, kd, g: (kd, 0, 0)),
                  pl.BlockSpec((None, CMP_LEN, hd, CMP_HIDDEN), lambda b, kd, g: (kd, 0, 0, 0)),
                  pl.BlockSpec((None, CMP_HIDDEN, hd), lambda b, kd, g: (kd, 0, 0))],
        out_specs=pl.BlockSpec((None, None, None, nh, hd), lambda b, kd, g: (b, kd, g, 0, 0)),
        out_shape=jax.ShapeDtypeStruct((n_batch, 2, NSA_KV_HEADS, nh, hd), BF16),
        compiler_params=_params("arbitrary", "arbitrary", "arbitrary"),
        name="nsa_compress",
    )(rows3, cmp_pos, cmp_w1, cmp_w2)


def _nsa_select_kernel(qn_ref, kc_ref, vc_ref, ovt_ref, oc_ref, selb_ref, *, n_cmp, n_top):
    rep, tq, hd = qn_ref.shape[1], qn_ref.shape[2], qn_ref.shape[3]
    n_cmp_pad = kc_ref.shape[0]
    n_sel = ovt_ref.shape[0]
    q0 = pl.program_id(2) * tq
    q = qn_ref[0].reshape(rep * tq, hd)
    s = lax.dot_general(q, kc_ref[...], (((1,), (1,)), ((), ())), preferred_element_type=F32)
    row = lax.broadcasted_iota(jnp.int32, (rep * tq, n_cmp_pad), 0)
    col = lax.broadcasted_iota(jnp.int32, (rep * tq, n_cmp_pad), 1)
    qpos = q0 + (row & (tq - 1))
    mask = (col * CMP_STRIDE + (CMP_LEN - 1) <= qpos) & (col < n_cmp)
    s = jnp.where(mask, s, -jnp.inf)
    m = jnp.max(s, axis=-1, keepdims=True)
    m = jnp.where(m > -jnp.inf, m, 0.0)
    p = jnp.where(mask, jnp.exp(s - m), 0.0)
    p = p / jnp.maximum(jnp.sum(p, axis=-1, keepdims=True), 1e-30)
    oc = jnp.dot(p.astype(BF16), vc_ref[...], preferred_element_type=F32)
    oc_ref[0] = oc.reshape(rep, tq, hd).astype(BF16)
    psum = p[0:tq]
    for r in range(1, rep):
        psum = psum + p[r * tq:(r + 1) * tq]
    imp = lax.dot_general(ovt_ref[...], psum, (((1,), (1,)), ((), ())), preferred_element_type=F32,
                          precision=lax.Precision.HIGHEST)
    blk = lax.broadcasted_iota(jnp.int32, (n_sel, tq), 0)
    cur = (q0 + lax.broadcasted_iota(jnp.int32, (n_sel, tq), 1)) // SEL_BLOCK
    forced = (blk == 0) | (blk == cur) | (blk == cur - 1)
    v = jnp.where(blk <= cur, jnp.where(forced, jnp.inf, imp), -jnp.inf)
    rank = jnp.zeros((n_sel, tq), jnp.int32)
    for i in range(n_sel):
        vi = v[i:i + 1, :]
        ahead = (vi > v) | ((vi == v) & (blk > i))
        rank = rank + ahead.astype(jnp.int32)
    selb_t = jnp.where((rank < n_top) & (v > -jnp.inf), 0.0, NEG_BIG)
    pad = jnp.full((SEL_PAD - n_sel, tq), NEG_BIG, F32)
    selb_ref[0, 0] = jnp.concatenate([selb_t, pad], axis=0).T.astype(BF16)


def nsa_select(qn, cmp_kv, n_cmp, t_k):
    n_batch, _, t_len, hd = qn.shape
    n_cmp_pad = cmp_kv.shape[3]
    n_sel = -(-t_k // SEL_BLOCK)
    tq = _pick_tile(t_len, (256, 128, 64, 32, 16))
    ci = np.arange(n_cmp_pad)[None, :]
    sj = np.arange(n_sel)[:, None]
    overlap_t = ((ci * CMP_STRIDE <= sj * SEL_BLOCK + SEL_BLOCK - 1) &
                 (ci * CMP_STRIDE + CMP_LEN - 1 >= sj * SEL_BLOCK) & (ci < n_cmp)).astype(np.float32)
    kern = functools.partial(_nsa_select_kernel, n_cmp=n_cmp, n_top=min(SEL_TOPN, n_sel))
    return pl.pallas_call(
        kern,
        grid=(n_batch, NSA_KV_HEADS, t_len // tq),
        in_specs=[pl.BlockSpec((1, NSA_REP, tq, hd), lambda b, g, i: (b, g, i, 0)),
                  pl.BlockSpec((None, None, None, n_cmp_pad, hd), lambda b, g, i: (b, 0, g, 0, 0)),
                  pl.BlockSpec((None, None, None, n_cmp_pad, hd), lambda b, g, i: (b, 1, g, 0, 0)),
                  pl.BlockSpec((n_sel, n_cmp_pad), lambda b, g, i: (0, 0))],
        out_specs=(pl.BlockSpec((1, NSA_REP, tq, hd), lambda b, g, i: (b, g, i, 0)),
                   pl.BlockSpec((1, 1, tq, SEL_PAD), lambda b, g, i: (b, g, i, 0))),
        out_shape=(jax.ShapeDtypeStruct((n_batch, NSA_HEADS, t_len, hd), BF16),
                   jax.ShapeDtypeStruct((n_batch, NSA_KV_HEADS, t_len, SEL_PAD), BF16)),
        compiler_params=_params("arbitrary", "arbitrary", "arbitrary"),
        name="nsa_select",
    )(qn, cmp_kv, cmp_kv, overlap_t)


ATTN_Q_BLOCK = 256
ATTN_K_TILE = 1024


def _nsa_attn_kernel(qr_ref, oc_ref, selb_ref, gate_ref, ks_ref, vs_ref, kw_ref, vw_ref, e_ref, cb_ref, wb_ref, o_ref,
                     m_scr, acc_scr, *, tk, wk):
    rep, qb, hd = qr_ref.shape[1], qr_ref.shape[2], qr_ref.shape[3]
    nr = rep * qb
    q0 = pl.program_id(2) * qb
    q = qr_ref[0].reshape(nr, hd)
    selb = selb_ref[0, 0]
    nt = (((1,), (1,)), ((), ()))

    def sel_scores(t, extra_bias=None):
        k = ks_ref[0, 0, pl.ds(pl.multiple_of(t * tk, tk), tk), :]
        s = lax.dot_general(q, k, nt, preferred_element_type=F32)
        bias = jnp.dot(selb, e_ref[t], preferred_element_type=F32)
        if extra_bias is not None:
            bias = bias + extra_bias
        return (s.reshape(rep, qb, tk) + bias[None]).reshape(nr, tk)

    def sel_values(t):
        return vs_ref[0, 0, pl.ds(pl.multiple_of(t * tk, tk), tk), :]

    td = q0 // tk
    s = sel_scores(td, cb_ref[0])
    m = jnp.max(s, axis=-1, keepdims=True)
    m_scr[...] = m
    acc_scr[...] = jnp.dot(jnp.exp(s - m).astype(BF16), sel_values(td), preferred_element_type=F32)

    def body(t, carry):
        s = sel_scores(t)
        m_old = m_scr[...]
        m_new = jnp.maximum(m_old, jnp.max(s, axis=-1, keepdims=True))
        p = jnp.exp(s - m_new).astype(BF16)
        acc_scr[...] = jnp.exp(m_old - m_new) * acc_scr[...] + jnp.dot(p, sel_values(t), preferred_element_type=F32)
        m_scr[...] = m_new
        return carry

    kstart = pl.multiple_of(jnp.maximum(q0 - WINDOW, 0), qb)
    kw = kw_ref[0, 0, pl.ds(kstart, wk), :]
    s = lax.dot_general(q, kw, nt, preferred_element_type=F32)
    s = (s.reshape(rep, qb, wk) + wb_ref[0][None]).reshape(nr, wk)
    m = jnp.max(s, axis=-1, keepdims=True)
    accw = jnp.dot(jnp.exp(s - m).astype(BF16), vw_ref[0, 0, pl.ds(kstart, wk), :], preferred_element_type=F32)
    o_win = accw[:, 0:hd] / jnp.maximum(accw[:, hd:hd + 1], 1e-30)

    lax.fori_loop(0, td, body, 0)
    acc = acc_scr[...]
    o_sel = acc[:, 0:hd] / jnp.maximum(acc[:, hd:hd + 1], 1e-30)

    gates = gate_ref[0, 0]
    for r in range(rep):
        rows = slice(r * qb, (r + 1) * qb)
        o = (gates[:, 3 * r:3 * r + 1] * oc_ref[0, r].astype(F32)
             + gates[:, 3 * r + 1:3 * r + 2] * o_sel[rows]
             + gates[:, 3 * r + 2:3 * r + 3] * o_win[rows])
        o_ref[:, r * hd:(r + 1) * hd] = o.astype(BF16)


def nsa_attend(qr, o_cmp, selb, gates, ks, vs, kw, vw):
    n_batch, _, t_len, hd = qr.shape
    n_sel = selb.shape[3]
    qb = min(ATTN_Q_BLOCK, t_len)
    tk = min(ATTN_K_TILE, t_len)
    wk = min(WINDOW + qb, t_len)
    nq = t_len // qb
    n_tiles = t_len // tk
    key_blk = (np.arange(n_tiles)[:, None, None] * tk + np.arange(tk)[None, None, :]) // SEL_BLOCK
    e = (key_blk == np.arange(n_sel)[None, :, None]).astype(BF16)
    n_cv = tk // qb
    qi = np.arange(qb)[None, :, None]
    causal = np.where(np.arange(tk)[None, None, :] <= np.arange(n_cv)[:, None, None] * qb + qi, 0.0, NEG_BIG)
    n_wv = min(WINDOW // qb, nq - 1) + 1
    q0v = np.arange(n_wv)[:, None, None] * qb
    kpos = np.maximum(q0v - WINDOW, 0) + np.arange(wk)[None, None, :]
    qpos = q0v + qi
    window = np.where((kpos <= qpos) & (kpos > qpos - WINDOW), 0.0, NEG_BIG).astype(np.float32)
    kern = functools.partial(_nsa_attn_kernel, tk=tk, wk=wk)
    per_q = lambda nh, w: pl.BlockSpec((1, nh, qb, w), lambda b, g, i: (b, g, i, 0))
    full = lambda w: pl.BlockSpec((1, 1, t_len, w), lambda b, g, i: (b, g, 0, 0))
    return pl.pallas_call(
        kern,
        grid=(n_batch, NSA_KV_HEADS, nq),
        in_specs=[per_q(NSA_REP, hd), per_q(NSA_REP, hd), per_q(1, n_sel), per_q(1, 128),
                  full(hd), full(2 * hd), full(hd), full(2 * hd),
                  pl.BlockSpec((n_tiles, n_sel, tk), lambda b, g, i: (0, 0, 0)),
                  pl.BlockSpec((1, qb, tk), lambda b, g, i: (i % n_cv, 0, 0)),
                  pl.BlockSpec((1, qb, wk), lambda b, g, i: (jnp.minimum(i, n_wv - 1), 0, 0))],
        out_specs=pl.BlockSpec((qb, NSA_REP * hd), lambda b, g, i: (b * nq + i, g)),
        out_shape=jax.ShapeDtypeStruct((n_batch * t_len, NSA_HEADS * hd), BF16),
        scratch_shapes=[pltpu.VMEM((NSA_REP * qb, 1), F32), pltpu.VMEM((NSA_REP * qb, 2 * hd), F32)],
        compiler_params=_params("arbitrary", "arbitrary", "arbitrary"),
        name="nsa_attend",
    )(qr, o_cmp, selb, gates, ks, vs, kw, vw, e, causal.astype(np.float32), window)


def _gla_kernel(q_ref, k_ref, v_ref, r_ref, small_ref, w2_ref, b2_ref, ng_ref, s0_ref, y_ref, sf_ref, s_scr):
    tb = q_ref.shape[0]
    c, sub, dk, dv = GLA_CHUNK, GLA_SUB, GLA_DK, GLA_DV
    n_sub = c // sub
    t = pl.program_id(1)

    @pl.when(t == 0)
    def _():
        s_scr[...] = s0_ref[0]

    z = jnp.dot(small_ref[:, 0:GLA_RANK].astype(BF16), w2_ref[...].astype(BF16),
                preferred_element_type=F32) + b2_ref[...]
    g_all = (jnp.minimum(z, 0.0) - jnp.log1p(jnp.exp(-jnp.abs(z)))) / GLA_GATE_NORM
    ri = lax.broadcasted_iota(jnp.int32, (c, c), 0)
    ci = lax.broadcasted_iota(jnp.int32, (c, c), 1)
    tril = ri >= ci
    cum = tril.astype(F32)
    rsub = lax.broadcasted_iota(jnp.int32, (c, dk), 0) // sub
    eye = lax.broadcasted_iota(jnp.int32, (dk, dk), 0) == lax.broadcasted_iota(jnp.int32, (dk, dk), 1)
    for cc in range(tb // c):
        rows = slice(cc * c, (cc + 1) * c)
        b_all = jnp.dot(cum, g_all[rows], preferred_element_type=F32, precision=lax.Precision.HIGHEST)
        for h in range(GLA_HEADS):
            b = b_all[:, h * dk:(h + 1) * dk]
            qh = q_ref[rows, h * dk:(h + 1) * dk] * (dk ** -0.5)
            kh = k_ref[rows, h * dk:(h + 1) * dk]
            vh = v_ref[rows, h * dv:(h + 1) * dv]
            a_rows = []
            for i in range(n_sub):
                ref = b[sub * i - 1:sub * i, :] if i else jnp.zeros((1, dk), F32)
                rs = slice(sub * i, sub * (i + 1))
                qi = (qh[rs] * jnp.exp(b[rs] - ref)).astype(BF16)
                ki = jnp.where(rsub <= i, kh * jnp.exp(ref - b), 0.0).astype(BF16)
                a_rows.append(lax.dot_general(qi, ki, (((1,), (1,)), ((), ())), preferred_element_type=F32))
            a = jnp.where(tril, jnp.concatenate(a_rows, axis=0), 0.0)
            s_old = s_scr[h]
            o = jnp.dot(a.astype(BF16), vh.astype(BF16), preferred_element_type=F32)
            o += jnp.dot((qh * jnp.exp(b)).astype(BF16), s_old.astype(BF16), preferred_element_type=F32)
            b_last = b[c - 1:c, :]
            ke = (kh * jnp.exp(b_last - b)).astype(BF16)
            upd = lax.dot_general(ke, vh.astype(BF16), (((0,), (0,)), ((), ())), preferred_element_type=F32)
            decay = jnp.exp(jnp.sum(jnp.where(eye, jnp.broadcast_to(b_last, (dk, dk)), 0.0), axis=1, keepdims=True))
            s_scr[h] = decay * s_old + upd
            o = o * lax.rsqrt(jnp.mean(o * o, axis=-1, keepdims=True) + RMS_EPS)
            y = o * ng_ref[:, h * dv:(h + 1) * dv] * jax.nn.silu(r_ref[rows, h * dv:(h + 1) * dv])
            y_ref[rows, h * dv:(h + 1) * dv] = y.astype(BF16)

    @pl.when(t == pl.num_programs(1) - 1)
    def _():
        sf_ref[0] = s_scr[...]


def gla_mix(p, s0, w2, b2, norm_g, n_batch, t_len):
    tb = _pick_tile(t_len, (256, 128, 64))
    nt = t_len // tb
    row = lambda w, off: pl.BlockSpec((tb, w), lambda b, i: (b * nt + i, off // w))
    const = lambda shape: pl.BlockSpec(shape, lambda b, i: (0,) * len(shape))
    return pl.pallas_call(
        _gla_kernel,
        grid=(n_batch, nt),
        in_specs=[row(256, COL_GLA_Q), row(256, COL_GLA_K), row(512, COL_GLA_V), row(512, COL_GLA_R),
                  row(128, COL_SMALL), const((GLA_RANK, GLA_HEADS * GLA_DK)), const((1, GLA_HEADS * GLA_DK)),
                  const((1, GLA_WIDTH)),
                  pl.BlockSpec((1, GLA_HEADS, GLA_DK, GLA_DV), lambda b, i: (b, 0, 0, 0))],
        out_specs=(pl.BlockSpec((tb, GLA_WIDTH), lambda b, i: (b * nt + i, 0)),
                   pl.BlockSpec((1, GLA_HEADS, GLA_DK, GLA_DV), lambda b, i: (b, 0, 0, 0))),
        out_shape=(jax.ShapeDtypeStruct((n_batch * t_len, GLA_WIDTH), BF16),
                   jax.ShapeDtypeStruct((n_batch, GLA_HEADS, GLA_DK, GLA_DV), F32)),
        scratch_shapes=[pltpu.VMEM((GLA_HEADS, GLA_DK, GLA_DV), F32)],
        compiler_params=_params("arbitrary", "arbitrary"),
        name="gla_mix",
    )(p, p, p, p, p, w2, b2.reshape(1, -1), norm_g.reshape(1, -1), s0)


def _pool_kernel(u_ref, prev_ref, cnt_ref, w_ref, sc_ref, y_ref, halo):
    tb = u_ref.shape[0]
    gd = POOL_GROUP_DIM

    @pl.when(pl.program_id(1) == 0)
    def _():
        halo[...] = prev_ref[0]

    ext = jnp.concatenate([halo[...], u_ref[...]], axis=0)
    halo[...] = ext[tb:tb + POOL_MAX]
    for gi, w in enumerate(POOL_WINDOWS):
        x = ext[:, gi * gd:(gi + 1) * gd]
        s = x
        shift = 1
        while shift < w:
            s = s + pltpu.roll(s, shift, 0)
            shift *= 2
        pooled = s[POOL_MAX:] / cnt_ref[:, gi:gi + 1] - x[POOL_MAX:]
        y = jnp.dot(pooled.astype(BF16), w_ref[gi].astype(BF16), preferred_element_type=F32)
        y_ref[:, gi * gd:(gi + 1) * gd] = (y * sc_ref[:, gi * gd:(gi + 1) * gd]).astype(BF16)


def pool_mix(p, prev, pos0, w_pool, scale, n_batch, t_len):
    tb = _pick_tile(t_len, (512, 256, 128, 64, 32, 16))
    nt = t_len // tb
    pos = pos0 + np.arange(t_len)
    cnt = np.ones((t_len, 128), np.float32)
    for gi, w in enumerate(POOL_WINDOWS):
        cnt[:, gi] = np.minimum(pos + 1, w)
    prev16 = jnp.pad(prev.astype(F32), ((0, 0), (1, 0), (0, 0)))
    return pl.pallas_call(
        _pool_kernel,
        grid=(n_batch, nt),
        in_specs=[pl.BlockSpec((tb, POOL_WIDTH), lambda b, i: (b * nt + i, COL_POOL // POOL_WIDTH)),
                  pl.BlockSpec((1, POOL_MAX, POOL_WIDTH), lambda b, i: (b, 0, 0)),
                  pl.BlockSpec((tb, 128), lambda b, i: (i, 0)),
                  pl.BlockSpec((POOL_GROUPS, POOL_GROUP_DIM, POOL_GROUP_DIM), lambda b, i: (0, 0, 0)),
                  pl.BlockSpec((1, POOL_WIDTH), lambda b, i: (0, 0))],
        out_specs=pl.BlockSpec((tb, POOL_WIDTH), lambda b, i: (b * nt + i, 0)),
        out_shape=jax.ShapeDtypeStruct((n_batch * t_len, POOL_WIDTH), BF16),
        scratch_shapes=[pltpu.VMEM((POOL_MAX, POOL_WIDTH), F32)],
        compiler_params=_params("arbitrary", "arbitrary"),
        name="pool_mix",
    )(p, prev16, cnt, w_pool, scale.reshape(1, -1))


def _layer_norm_rows(x, g, b):
    xc = x - jnp.mean(x, axis=-1, keepdims=True)
    var = jnp.mean(xc * xc, axis=-1, keepdims=True)
    return xc * lax.rsqrt(var + LN_EPS) * g + b


def _outproj_kernel(x_ref, yg_ref, yn_ref, yp_ref, w_ref, g_ref, b_ref, *rest):
    h = jnp.dot(yg_ref[...], w_ref[0:GLA_WIDTH, :], preferred_element_type=F32)
    h += jnp.dot(yn_ref[...], w_ref[GLA_WIDTH:GLA_WIDTH + NSA_WIDTH, :], preferred_element_type=F32)
    h += jnp.dot(yp_ref[...], w_ref[GLA_WIDTH + NSA_WIDTH:, :], preferred_element_type=F32)
    x1 = _layer_norm_rows(ALPHA * x_ref[...] + h, g_ref[...], b_ref[...])
    if len(rest) == 1:
        rest[0][...] = x1
    else:
        rh_ref, rl_ref, o_ref, lg_ref = rest
        o_ref[...] = x1
        xh = x1.astype(BF16)
        xl = (x1 - xh.astype(F32)).astype(BF16)
        lg_ref[...] = (jnp.dot(xh, rh_ref[...], preferred_element_type=F32)
                       + jnp.dot(xl, rh_ref[...], preferred_element_type=F32)
                       + jnp.dot(xh, rl_ref[...], preferred_element_type=F32))


def router_split(router):
    r = jnp.pad(router, ((0, 0), (0, 128 - router.shape[1])))
    hi = r.astype(BF16)
    return hi, (r - hi.astype(F32)).astype(BF16)


def outproj_ln(x, y_gla, y_nsa, y_pool, w_out_bf16, g, b, router=None):
    n, d = x.shape
    tm = _pick_tile(n, (512, 256, 128, 64, 32, 16, 8))
    row = lambda w: pl.BlockSpec((tm, w), lambda i: (i, 0))
    const = lambda r, c: pl.BlockSpec((r, c), lambda i: (0, 0))
    in_specs = [row(d), row(GLA_WIDTH), row(NSA_WIDTH), row(POOL_WIDTH), const(d, d), const(1, d), const(1, d)]
    args = [x, y_gla, y_nsa, y_pool, w_out_bf16, g.reshape(1, -1), b.reshape(1, -1)]
    out_specs, out_shape = row(d), jax.ShapeDtypeStruct((n, d), F32)
    if router is not None:
        in_specs += [const(d, 128), const(d, 128)]
        args += list(router_split(router))
        out_specs, out_shape = (out_specs, row(128)), (out_shape, jax.ShapeDtypeStruct((n, 128), F32))
    return pl.pallas_call(
        _outproj_kernel,
        grid=(n // tm,),
        in_specs=in_specs,
        out_specs=out_specs,
        out_shape=out_shape,
        compiler_params=_params("arbitrary"),
        name="outproj_ln",
    )(*args)


MOE_TM = 512
MOE_TM_DOWN = 512
MOE_VMEM_BYTES = 60 * 1024 * 1024
MOE_TF = 1024
MOE_TN = 512
ROUTE_TM = 512
PERMUTE_CHUNK = 1024


def _route_kernel(lg_ref, ii_ref, gf_ref, cnt_ref, carry, *, n_valid):
    tm = lg_ref.shape[0]
    i = pl.program_id(0)

    @pl.when(i == 0)
    def _():
        carry[...] = jnp.zeros_like(carry)

    lane = lax.broadcasted_iota(jnp.int32, (tm, 128), 1)
    valid = (i * tm + lax.broadcasted_iota(jnp.int32, (tm, 128), 0)) < n_valid
    lg = jnp.where(lane < N_EXPERTS, lg_ref[...], -jnp.inf)
    m1 = jnp.max(lg, axis=-1, keepdims=True)
    i1 = jnp.min(jnp.where(lg == m1, lane, 128), axis=-1, keepdims=True)
    lg2 = jnp.where(lane == i1, -jnp.inf, lg)
    m2 = jnp.max(lg2, axis=-1, keepdims=True)
    i2 = jnp.min(jnp.where(lg2 == m2, lane, 128), axis=-1, keepdims=True)
    t = jnp.exp(m2 - m1)
    g1 = 1.0 / (1.0 + t)
    g2 = t / (1.0 + t)
    oh1 = jnp.where((lane == i1) & valid, 1.0, 0.0)
    oh2 = jnp.where((lane == i2) & valid, 1.0, 0.0)
    cnt = oh1 + oh2
    strict = (lax.broadcasted_iota(jnp.int32, (tm, tm), 0) > lax.broadcasted_iota(jnp.int32, (tm, tm), 1))
    before = jnp.dot(strict.astype(BF16), cnt.astype(BF16), preferred_element_type=F32) + carry[...]
    r1 = jnp.sum(before * oh1, axis=-1, keepdims=True).astype(jnp.int32)
    r2 = jnp.sum(before * oh2, axis=-1, keepdims=True).astype(jnp.int32)
    carry[...] += jnp.sum(cnt, axis=0, keepdims=True)
    ii_ref[...] = jnp.where(lane == 0, i1, jnp.where(lane == 1, i2, jnp.where(lane == 2, r1, r2)))
    gf_ref[...] = jnp.where(lane == 0, g1, g2)
    cnt_ref[...] = carry[...]


def moe_route(logits, n_valid):
    npad = logits.shape[0]
    tm = ROUTE_TM
    row = pl.BlockSpec((tm, 128), lambda i: (i, 0))
    info, gates, counts = pl.pallas_call(
        functools.partial(_route_kernel, n_valid=n_valid),
        grid=(npad // tm,),
        in_specs=[row],
        out_specs=(row, row, pl.BlockSpec((1, 128), lambda i: (0, 0))),
        out_shape=(jax.ShapeDtypeStruct((npad, 128), jnp.int32), jax.ShapeDtypeStruct((npad, 128), F32),
                   jax.ShapeDtypeStruct((1, 128), F32)),
        scratch_shapes=[pltpu.VMEM((1, 128), F32)],
        compiler_params=_params("arbitrary"),
        name="moe_route",
    )(logits)
    return info[:, 0:2], info[:, 2:4], gates, counts[0, :N_EXPERTS].astype(jnp.int32)


SLAB = (16, 128)


def _slabify_kernel(x_ref, *rest):
    o_ref = rest[-1]

    def slab_rows(src_ref, n_rows):
        for c in range(SLAB[0]):
            o_ref[0:n_rows, c, :] = src_ref[:, c * SLAB[1]:(c + 1) * SLAB[1]]

    if len(rest) == 1:
        slab_rows(x_ref, x_ref.shape[0])
        return
    t_ref = rest[0]
    last = pl.program_id(0) == pl.num_programs(0) - 1

    @pl.when(jnp.logical_not(last))
    def _():
        slab_rows(x_ref, x_ref.shape[0])

    @pl.when(last)
    def _():
        o_ref[...] = jnp.zeros_like(o_ref)
        slab_rows(t_ref, t_ref.shape[0])


def slabify(x, tail=None):
    n, d = x.shape
    tm = _pick_tile(n, (512, 256, 128, 64, 32, 16, 8))
    nt = n // tm
    in_specs = [pl.BlockSpec((tm, d), lambda i: (jnp.minimum(i, nt - 1), 0))]
    args = [x]
    if tail is not None:
        assert tail.shape[0] <= tm
        in_specs.append(pl.BlockSpec(tail.shape, lambda i: (0, 0)))
        args.append(tail)
    steps = nt + (tail is not None)
    return pl.pallas_call(
        _slabify_kernel,
        grid=(steps,),
        in_specs=in_specs,
        out_specs=pl.BlockSpec((tm,) + SLAB, lambda i: (i, 0, 0)),
        out_shape=jax.ShapeDtypeStruct((steps * tm,) + SLAB, x.dtype),
        compiler_params=_params("arbitrary"),
        name="slabify",
    )(*args)


GATHER_UNROLL = 8


def _gather_rows_kernel(idx_ref, src_ref, o_ref, stage, sem):
    ch = idx_ref.shape[2]

    def row_copy(r):
        dst = stage.at[lax.shift_right_logical(r, 3), :, r & 7, :]
        return pltpu.make_async_copy(src_ref.at[idx_ref[0, 0, r]], dst, sem)

    def issue(t, c):
        for u in range(GATHER_UNROLL):
            row_copy(t * GATHER_UNROLL + u).start(priority=u % 2)
        return c

    lax.fori_loop(0, ch // GATHER_UNROLL, issue, 0)

    def drain(t, c):
        for u in range(GATHER_UNROLL):
            row_copy(t * GATHER_UNROLL + u).wait()
        return c

    lax.fori_loop(0, ch // GATHER_UNROLL, drain, 0)
    rows = jnp.concatenate([stage[:, c].reshape(ch, SLAB[1]) for c in range(SLAB[0])], axis=1)
    o_ref[...] = rows.astype(o_ref.dtype)


def gather_rows_bf16(src, idx):
    n = idx.shape[0]
    ch = PERMUTE_CHUNK
    d = SLAB[0] * SLAB[1]
    return pl.pallas_call(
        _gather_rows_kernel,
        grid=(n // ch,),
        in_specs=[pl.BlockSpec((1, 1, ch), lambda i: (i, 0, 0), memory_space=pltpu.SMEM),
                  pl.BlockSpec(memory_space=pl.ANY)],
        out_specs=pl.BlockSpec((ch, d), lambda i: (i, 0)),
        out_shape=jax.ShapeDtypeStruct((n, d), BF16),
        scratch_shapes=[pltpu.VMEM((ch // 8, SLAB[0], 8, SLAB[1]), src.dtype), pltpu.SemaphoreType.DMA(())],
        compiler_params=_params("arbitrary"),
        name="gather_rows",
    )(idx.reshape(n // ch, 1, ch), src)


def _moe_up_kernel(te_ref, tfirst_ref, tused_ref, x_ref, wg_ref, wu_ref, h_ref, wgb, wub):
    i = pl.program_id(1)

    @pl.when((i == 0) | (tfirst_ref[i] == 1))
    def _():
        wgb[...] = wg_ref[...].astype(BF16)
        wub[...] = wu_ref[...].astype(BF16)

    @pl.when(tused_ref[i] == 1)
    def _():
        xb = x_ref[...]
        gate = jnp.dot(xb, wgb[...], preferred_element_type=F32)
        up = jnp.dot(xb, wub[...], preferred_element_type=F32)
        h_ref[...] = (jax.nn.silu(gate) * up).astype(BF16)

    @pl.when(tused_ref[i] == 0)
    def _():
        h_ref[...] = jnp.zeros_like(h_ref)


def _moe_down_kernel(te_ref, tfirst_ref, tused_ref, h_ref, wd_ref, y_ref, wdb, ybuf, sems):
    j, i = pl.program_id(0), pl.program_id(1)
    n_i = pl.num_programs(1)
    n_c, tm = ybuf.shape[1], ybuf.shape[2]
    step = j * n_i + i
    last = pl.num_programs(0) * n_i - 1

    def copies(s, slot):
        jj, ii = s // n_i, s % n_i
        return [pltpu.make_async_copy(ybuf.at[slot, c], y_ref.at[pl.ds(ii * tm, tm), jj * n_c + c, :], sems.at[slot])
                for c in range(n_c)]

    def for_slot(s, fn):
        for slot in range(2):
            @pl.when(s % 2 == slot)
            def _():
                fn(slot)

    @pl.when((i == 0) | (tfirst_ref[i] == 1))
    def _():
        wdb[...] = wd_ref[...].astype(BF16)

    @pl.when(step >= 2)
    def _():
        for_slot(step, lambda slot: [cp.wait() for cp in copies(step - 2, slot)])

    def fill(slot):
        @pl.when(tused_ref[i] == 1)
        def _():
            y = jnp.dot(h_ref[...], wdb[...], preferred_element_type=F32)
            for c in range(n_c):
                ybuf[slot, c] = y[:, c * SLAB[1]:(c + 1) * SLAB[1]]

        @pl.when(tused_ref[i] == 0)
        def _():
            ybuf[slot] = jnp.zeros(ybuf.shape[1:], F32)

        for cp in copies(step, slot):
            cp.start()

    for_slot(step, fill)

    @pl.when(step == last)
    def _():
        for_slot(step - 1, lambda slot: [cp.wait() for cp in copies(step - 1, slot)])
        for_slot(step, lambda slot: [cp.wait() for cp in copies(step, slot)])


def _tile_meta(tile_e, tile_used, split):
    te = jnp.repeat(tile_e, split)
    first = jnp.concatenate([jnp.ones((1,), jnp.int32), (te[1:] != te[:-1]).astype(jnp.int32)])
    return te, first, jnp.repeat(tile_used, split)


def moe_experts(xs, tile_e, tile_used, wg, wu, wd):
    r, d = xs.shape
    d_ff = wg.shape[2]
    tm, tf, tn = MOE_TM, MOE_TF, MOE_TN
    h = pl.pallas_call(
        _moe_up_kernel,
        grid_spec=pltpu.PrefetchScalarGridSpec(
            num_scalar_prefetch=3,
            grid=(d_ff // tf, r // tm),
            in_specs=[pl.BlockSpec((tm, d), lambda j, i, te, t1, tu: (i, 0)),
                      pl.BlockSpec((None, d, tf), lambda j, i, te, t1, tu: (te[i], 0, j)),
                      pl.BlockSpec((None, d, tf), lambda j, i, te, t1, tu: (te[i], 0, j))],
            out_specs=pl.BlockSpec((tm, tf), lambda j, i, te, t1, tu: (i, j)),
            scratch_shapes=[pltpu.VMEM((d, tf), BF16), pltpu.VMEM((d, tf), BF16)]),
        out_shape=jax.ShapeDtypeStruct((r, d_ff), BF16),
        compiler_params=_params("arbitrary", "arbitrary", vmem_limit_bytes=MOE_VMEM_BYTES),
        name="moe_up",
    )(*_tile_meta(tile_e, tile_used, 1), xs, wg, wu)
    tmd = MOE_TM_DOWN
    return pl.pallas_call(
        _moe_down_kernel,
        grid_spec=pltpu.PrefetchScalarGridSpec(
            num_scalar_prefetch=3,
            grid=(d // tn, r // tmd),
            in_specs=[pl.BlockSpec((tmd, d_ff), lambda j, i, te, t1, tu: (i, 0)),
                      pl.BlockSpec((None, d_ff, tn), lambda j, i, te, t1, tu: (te[i], 0, j))],
            out_specs=pl.BlockSpec(memory_space=pl.ANY),
            scratch_shapes=[pltpu.VMEM((d_ff, tn), BF16), pltpu.VMEM((2, tn // SLAB[1], tmd, SLAB[1]), F32),
                            pltpu.SemaphoreType.DMA((2,))]),
        out_shape=jax.ShapeDtypeStruct((r,) + SLAB, F32),
        compiler_params=_params("arbitrary", "arbitrary", vmem_limit_bytes=MOE_VMEM_BYTES),
        name="moe_down",
    )(*_tile_meta(tile_e, tile_used, tm // tmd), h, wd)


def _moe_combine_kernel(idx_ref, x_ref, gt_ref, g_ref, b_ref, y_ref, o_ref, stage, sem):
    tm = x_ref.shape[0]

    def row_copy(n):
        k, r = lax.shift_right_logical(n, tm.bit_length() - 1), n & (tm - 1)
        return pltpu.make_async_copy(y_ref.at[idx_ref[0, 0, n]],
                                     stage.at[k, lax.shift_right_logical(r, 3), :, r & 7, :], sem)

    def issue(t, c):
        for u in range(GATHER_UNROLL):
            row_copy(t * GATHER_UNROLL + u).start(priority=u % 2)
        return c

    lax.fori_loop(0, TOP_K * tm // GATHER_UNROLL, issue, 0)

    def drain(t, c):
        for u in range(GATHER_UNROLL):
            row_copy(t * GATHER_UNROLL + u).wait()
        return c

    lax.fori_loop(0, TOP_K * tm // GATHER_UNROLL, drain, 0)
    rows = lambda k: jnp.concatenate([stage[k, :, c].reshape(tm, SLAB[1]) for c in range(SLAB[0])], axis=1)
    gt = gt_ref[...]
    y = gt[:, 0:1] * rows(0) + gt[:, 1:2] * rows(1)
    o_ref[...] = _layer_norm_rows(ALPHA * x_ref[...] + y, g_ref[...], b_ref[...])


def moe_combine_ln(x, ys, dest, gates, row0, g, b):
    n, d = x.shape
    tm = _pick_tile(n, (512, 256, 128, 64, 32, 16, 8))
    o0 = row0 // tm
    idx = dest[row0:row0 + n].reshape(n // tm, tm, TOP_K).transpose(0, 2, 1).reshape(n // tm, 1, TOP_K * tm)
    return pl.pallas_call(
        _moe_combine_kernel,
        grid=(n // tm,),
        in_specs=[pl.BlockSpec((1, 1, TOP_K * tm), lambda i: (i, 0, 0), memory_space=pltpu.SMEM),
                  pl.BlockSpec((tm, d), lambda i: (i, 0)),
                  pl.BlockSpec((tm, 128), lambda i: (o0 + i, 0)),
                  pl.BlockSpec((1, d), lambda i: (0, 0)), pl.BlockSpec((1, d), lambda i: (0, 0)),
                  pl.BlockSpec(memory_space=pl.ANY)],
        out_specs=pl.BlockSpec((tm, d), lambda i: (i, 0)),
        out_shape=jax.ShapeDtypeStruct((n, d), F32),
        scratch_shapes=[pltpu.VMEM((TOP_K, tm // 8, SLAB[0], 8, SLAB[1]), F32), pltpu.SemaphoreType.DMA(())],
        compiler_params=_params("arbitrary"),
        name="moe_combine_ln",
    )(idx.astype(jnp.int32), x, gates, g.reshape(1, -1), b.reshape(1, -1), ys)


def moe_ln(x_groups, logit_groups, wg, wu, wd, g, b):
    d = x_groups[0].shape[1]
    n_tok = sum(x.shape[0] for x in x_groups)
    n_tok_pad = -(-n_tok // PERMUTE_CHUNK) * PERMUTE_CHUNK
    n_tok_pad = -(-n_tok_pad // ROUTE_TM) * ROUTE_TM
    logits = jnp.concatenate(logit_groups + [jnp.zeros((n_tok_pad - n_tok, 128), F32)], axis=0)
    experts, ranks, gates, counts = moe_route(logits, n_tok)
    tm = MOE_TM
    n_tiles = -(-(n_tok * TOP_K + N_EXPERTS * (tm - 1)) // tm)
    n_tiles = -(-n_tiles * tm // PERMUTE_CHUNK) * PERMUTE_CHUNK // tm
    padded = (counts + tm - 1) // tm * tm
    pad_end = jnp.cumsum(padded)
    pad_start = pad_end - padded
    valid = (jnp.arange(n_tok_pad) < n_tok)[:, None]
    dest = jnp.where(valid, pad_start[experts] + ranks, 0)
    tok = jnp.broadcast_to(jnp.arange(n_tok_pad, dtype=jnp.int32)[:, None], dest.shape)
    row_tok = jnp.zeros((n_tiles * tm,), jnp.int32).at[jnp.where(valid, dest, n_tiles * tm).reshape(-1)].set(
        tok.reshape(-1), mode='drop')
    tile_start = jnp.arange(n_tiles, dtype=jnp.int32) * tm
    tile_e = jnp.minimum(jnp.sum(tile_start[:, None] >= pad_end[None, :], axis=1), N_EXPERTS - 1).astype(jnp.int32)
    tile_used = (tile_start < pad_end[-1]).astype(jnp.int32)
    assert len(x_groups) == 2
    xs = gather_rows_bf16(slabify(x_groups[0], x_groups[1]), row_tok)
    ys = moe_experts(xs, tile_e, tile_used, wg, wu, wd)
    outs, row0 = [], 0
    for x in x_groups:
        outs.append(moe_combine_ln(x, ys, dest, gates, row0, g, b))
        row0 += x.shape[0]
    return outs


def _ffn_kernel(x_ref, wg_ref, wu_ref, wd_ref, g_ref, b_ref, o_ref, xb_ref, *, d_ff):
    j = pl.program_id(1)
    tf = wg_ref.shape[1]

    @pl.when(j == 0)
    def _():
        xb_ref[...] = x_ref[...].astype(BF16)
        o_ref[...] = jnp.zeros_like(o_ref)

    xb = xb_ref[...]
    gate = jnp.dot(xb, wg_ref[...], preferred_element_type=F32)
    up = jnp.dot(xb, wu_ref[...], preferred_element_type=F32)
    col = j * tf + lax.broadcasted_iota(jnp.int32, (1, tf), 1)
    a = jnp.where(col < d_ff, jax.nn.silu(gate) * up, 0.0).astype(BF16)
    rowi = j * tf + lax.broadcasted_iota(jnp.int32, (tf, 1), 0)
    wd = jnp.where(rowi < d_ff, wd_ref[...], jnp.zeros((), BF16))
    o_ref[...] += jnp.dot(a, wd, preferred_element_type=F32)

    @pl.when(j == pl.num_programs(1) - 1)
    def _():
        o_ref[...] = _layer_norm_rows(ALPHA * x_ref[...] + o_ref[...], g_ref[...], b_ref[...])


def ffn_ln(x, wg, wu, wd, g, b):
    n, d = x.shape
    d_ff = wg.shape[1]
    tm = _pick_tile(n, (512, 256, 128, 64, 32, 16, 8))
    tf = 512
    kern = functools.partial(_ffn_kernel, d_ff=d_ff)
    return pl.pallas_call(
        kern,
        grid=(n // tm, pl.cdiv(d_ff, tf)),
        in_specs=[pl.BlockSpec((tm, d), lambda i, j: (i, 0)),
                  pl.BlockSpec((d, tf), lambda i, j: (0, j)), pl.BlockSpec((d, tf), lambda i, j: (0, j)),
                  pl.BlockSpec((tf, d), lambda i, j: (j, 0)),
                  pl.BlockSpec((1, d), lambda i, j: (0, 0)), pl.BlockSpec((1, d), lambda i, j: (0, 0))],
        out_specs=pl.BlockSpec((tm, d), lambda i, j: (i, 0)),
        out_shape=jax.ShapeDtypeStruct((n, d), F32),
        scratch_shapes=[pltpu.VMEM((tm, d), BF16)],
        compiler_params=_params("arbitrary", "arbitrary"),
        name="ffn_ln",
    )(x, wg, wu, wd, g.reshape(1, -1), b.reshape(1, -1))


CMP_PAGES = 32
SEL_PAD_DEC = 384


def _cmp_paged_kernel(pt_ref, cache_ref, pe_ref, w1_ref, w2_ref, o_ref, buf, hlo, hhi, sems, *, layer, n_pages, page):
    b, ch = pl.program_id(0), pl.program_id(1)
    n_ch = pl.num_programs(1)
    hd = HEAD_DIM
    nh = CMP_PAGES * page // CMP_STRIDE
    step = b * n_ch + ch

    def copies(s, slot):
        first = s * CMP_PAGES
        return [pltpu.make_async_copy(
            cache_ref.at[layer, pt_ref[first + pg], :, c // NSA_KV_HEADS, c % NSA_KV_HEADS, :],
            buf.at[slot, c, pl.ds(pg * page, page), :], sems.at[slot])
            for pg in range(CMP_PAGES) for c in range(2 * NSA_KV_HEADS)]

    def for_slot(s, fn):
        for slot in range(2):
            @pl.when(s % 2 == slot)
            def _():
                fn(slot)

    @pl.when(step == 0)
    def _():
        for cp in copies(0, 0):
            cp.start()

    @pl.when(step + 1 < pl.num_programs(0) * n_ch)
    def _():
        for_slot(step + 1, lambda slot: [cp.start() for cp in copies(step + 1, slot)])

    for_slot(step, lambda slot: [cp.wait() for cp in copies(step, slot)])
    cur = step % 2

    for c in range(2 * NSA_KV_HEADS):
        kd, g = c // NSA_KV_HEADS, c % NSA_KV_HEADS
        xs = [buf[cur, c, pl.ds(j, nh, stride=CMP_STRIDE), :] for j in range(CMP_STRIDE)]
        lo = jnp.concatenate([(xs[j] + pe_ref[kd, j:j + 1, :]).astype(BF16) for j in range(CMP_STRIDE)], axis=1)
        hi = jnp.concatenate([(xs[j] + pe_ref[kd, CMP_STRIDE + j:CMP_STRIDE + j + 1, :]).astype(BF16)
                              for j in range(CMP_STRIDE)], axis=1)
        w_lo = w1_ref[kd, 0:CMP_STRIDE].reshape(CMP_STRIDE * hd, CMP_HIDDEN).astype(BF16)
        w_hi = w1_ref[kd, CMP_STRIDE:CMP_LEN].reshape(CMP_STRIDE * hd, CMP_HIDDEN).astype(BF16)
        rows = pl.ds(pl.multiple_of(ch * nh, nh), nh)
        hlo[c, rows, :] = jnp.dot(lo, w_lo, preferred_element_type=F32)
        hhi[c, rows, :] = jnp.dot(hi, w_hi, preferred_element_type=F32)

    @pl.when(ch == pl.num_programs(1) - 1)
    def _():
        n_all = hlo.shape[1]
        for c in range(2 * NSA_KV_HEADS):
            kd, g = c // NSA_KV_HEADS, c % NSA_KV_HEADS
            h = jax.nn.gelu(hlo[c] + pltpu.roll(hhi[c], n_all - 1, 0))
            o_ref[kd, g] = jnp.dot(h.astype(BF16), w2_ref[kd].astype(BF16), preferred_element_type=F32).astype(BF16)


def nsa_compress_paged(cache, layer, page_table, cmp_pos, cmp_w1, cmp_w2):
    page = cache.shape[2]
    n_batch, n_pages = page_table.shape
    hd = HEAD_DIM
    n_all = n_pages * page // CMP_STRIDE
    kern = functools.partial(_cmp_paged_kernel, layer=layer, n_pages=n_pages, page=page)
    const = lambda shape: pl.BlockSpec(shape, lambda b, c, pt: (0,) * len(shape))
    return pl.pallas_call(
        kern,
        grid_spec=pltpu.PrefetchScalarGridSpec(
            num_scalar_prefetch=1,
            grid=(n_batch, n_pages // CMP_PAGES),
            in_specs=[pl.BlockSpec(memory_space=pl.ANY), const((2, CMP_LEN, hd)),
                      const((2, CMP_LEN, hd, CMP_HIDDEN)), const((2, CMP_HIDDEN, hd))],
            out_specs=pl.BlockSpec((None, 2, NSA_KV_HEADS, n_all, hd), lambda b, c, pt: (b, 0, 0, 0, 0)),
            scratch_shapes=[pltpu.VMEM((2, 2 * NSA_KV_HEADS, CMP_PAGES * page, hd), F32),
                            pltpu.VMEM((2 * NSA_KV_HEADS, n_all, CMP_HIDDEN), F32),
                            pltpu.VMEM((2 * NSA_KV_HEADS, n_all, CMP_HIDDEN), F32),
                            pltpu.SemaphoreType.DMA((2,))]),
        out_shape=jax.ShapeDtypeStruct((n_batch, 2, NSA_KV_HEADS, n_all, hd), BF16),
        compiler_params=_params("arbitrary", "arbitrary"),
        name="nsa_compress_paged",
    )(page_table.reshape(-1).astype(jnp.int32), cache, cmp_pos, cmp_w1, cmp_w2)


def _sel_decode_kernel(qn_ref, kc_ref, vc_ref, ovt_ref, oc_ref, idx_ref, v_scr, psum_scr, *, n_cmp, n_sel, n_top,
                       q_pos):
    b = pl.program_id(0)
    n_cmp_pad = kc_ref.shape[2]
    nsp = ovt_ref.shape[0]
    n_rows = qn_ref.shape[0]
    col = lax.broadcasted_iota(jnp.int32, (n_rows, n_cmp_pad), 1)
    mask = (col * CMP_STRIDE + (CMP_LEN - 1) <= q_pos) & (col < n_cmp)
    psums = []
    for g in range(NSA_KV_HEADS):
        own = slice(g * NSA_REP, (g + 1) * NSA_REP)
        s = lax.dot_general(qn_ref[...], kc_ref[0, g], (((1,), (1,)), ((), ())), preferred_element_type=F32)
        s = jnp.where(mask, s, -jnp.inf)
        m = jnp.max(s, axis=-1, keepdims=True)
        m = jnp.where(m > -jnp.inf, m, 0.0)
        p = jnp.where(mask, jnp.exp(s - m), 0.0)
        p = p / jnp.maximum(jnp.sum(p, axis=-1, keepdims=True), 1e-30)
        oc = jnp.dot(p.astype(BF16), vc_ref[0, g], preferred_element_type=F32)
        oc_ref[own, :] = oc[own]
        psum_scr[pl.ds(b * NSA_KV_HEADS + g, 1), :] = jnp.sum(p[own], axis=0, keepdims=True)

    @pl.when(b == pl.num_programs(0) - 1)
    def _():
        n_col = psum_scr.shape[0]
        imp = lax.dot_general(ovt_ref[...], psum_scr[...], (((1,), (1,)), ((), ())), preferred_element_type=F32,
                              precision=lax.Precision.HIGHEST)
        blk = lax.broadcasted_iota(jnp.int32, (nsp, n_col), 0)
        cur = q_pos // SEL_BLOCK
        forced = (blk == 0) | (blk == cur) | (blk == cur - 1)
        v = jnp.where((blk <= cur) & (blk < n_sel), jnp.where(forced, jnp.inf, imp), -jnp.inf)
        v_scr[...] = v

        def count(i, rank):
            vi = v_scr[pl.ds(i, 1), :]
            ahead = (vi > v) | ((vi == v) & (blk > i))
            return rank + ahead.astype(jnp.int32)

        rank = lax.fori_loop(0, n_sel, count, jnp.zeros((nsp, n_col), jnp.int32))
        chosen = (rank < n_top) & (v > -jnp.inf)
        blk_f = blk.astype(F32)
        rows = [jnp.sum(jnp.where(chosen & (rank == t), blk_f, 0.0), axis=0, keepdims=True) for t in range(n_top)]
        idx_ref[...] = jnp.concatenate(rows, axis=0).astype(jnp.int32)


def nsa_select_decode(qn, cmp_kv, n_cmp, n_sel, q_pos):
    n_batch, n_heads, hd = qn.shape
    n_cmp_pad = cmp_kv.shape[3]
    nsp = SEL_PAD_DEC
    n_top = min(SEL_TOPN, n_sel)
    ci = np.arange(n_cmp_pad)[None, :]
    sj = np.arange(nsp)[:, None]
    overlap_t = ((ci * CMP_STRIDE <= sj * SEL_BLOCK + SEL_BLOCK - 1) &
                 (ci * CMP_STRIDE + CMP_LEN - 1 >= sj * SEL_BLOCK) & (ci < n_cmp) & (sj < n_sel)).astype(np.float32)
    kern = functools.partial(_sel_decode_kernel, n_cmp=n_cmp, n_sel=n_sel, n_top=n_top, q_pos=q_pos)
    kv = lambda kd: pl.BlockSpec((None, 1, NSA_KV_HEADS, n_cmp_pad, hd), lambda b: (b, kd, 0, 0, 0))
    n_col = n_batch * NSA_KV_HEADS
    o_cmp, idx = pl.pallas_call(
        kern,
        grid=(n_batch,),
        in_specs=[pl.BlockSpec((None, 2 * n_heads, hd), lambda b: (b, 0, 0)), kv(0), kv(1),
                  pl.BlockSpec((nsp, n_cmp_pad), lambda b: (0, 0))],
        out_specs=(pl.BlockSpec((None, n_heads, hd), lambda b: (b, 0, 0)),
                   pl.BlockSpec((n_top, n_col), lambda b: (0, 0))),
        out_shape=(jax.ShapeDtypeStruct((n_batch, n_heads, hd), F32),
                   jax.ShapeDtypeStruct((n_top, n_col), jnp.int32)),
        scratch_shapes=[pltpu.VMEM((nsp, n_col), F32), pltpu.VMEM((n_col, n_cmp_pad), F32)],
        compiler_params=_params("arbitrary"),
        name="nsa_select_decode",
    )(jnp.pad(qn, ((0, 0), (0, n_heads), (0, 0))), cmp_kv, cmp_kv, overlap_t)
    return o_cmp, idx.T.reshape(n_batch, NSA_KV_HEADS, n_top)


def _attn_decode_kernel(pt_ref, sel_ref, q_ref, cache_ref, new_ref, kw_ref, vw_ref, wnew_ref, oc_ref, gate_ref,
                        o_ref, kbuf, vbuf, sem, *, layer, n_pages, per_page, n_top):
    b = pl.program_id(0)
    G = NSA_KV_HEADS
    n_past_blocks = n_pages * per_page
    nt = (((1,), (1,)), ((), ()))

    def block_id(g, slot):
        return sel_ref[(b * G + g) * n_top + slot]

    def copies():
        out = []
        for g in range(G):
            for slot in range(n_top):
                j = jnp.minimum(block_id(g, slot), n_past_blocks - 1)
                rows = pl.ds((j % per_page) * SEL_BLOCK, SEL_BLOCK)
                page = pt_ref[b * n_pages + j // per_page]
                for kind, buf in ((2, kbuf), (3, vbuf)):
                    out.append(pltpu.make_async_copy(cache_ref.at[layer, page, rows, kind, g, :],
                                                     buf.at[g, pl.ds(slot * SEL_BLOCK, SEL_BLOCK), :], sem))
        return out

    for cp in copies():
        cp.start()
    for cp in copies():
        cp.wait()

    for g in range(G):
        qb = q_ref[g]
        q = qb.astype(F32)
        n_rows = qb.shape[0]
        s = lax.dot_general(qb, kbuf[g].astype(BF16), nt, preferred_element_type=F32)
        slot_of = lax.broadcasted_iota(jnp.int32, s.shape, 1) // SEL_BLOCK
        for slot in range(n_top):
            s = jnp.where((slot_of == slot) & (block_id(g, slot) >= n_past_blocks), NEG_BIG, s)
        k_new = new_ref[2 * G + g:2 * G + g + 1, :]
        v_new = new_ref[3 * G + g:3 * G + g + 1, :]
        s_new = jnp.sum(q * k_new, axis=-1, keepdims=True)
        m = jnp.maximum(jnp.max(s, axis=-1, keepdims=True), s_new)
        p = jnp.exp(s - m)
        p_new = jnp.exp(s_new - m)
        l = jnp.sum(p, axis=-1, keepdims=True) + p_new
        o_sel = (jnp.dot(p.astype(BF16), vbuf[g].astype(BF16), preferred_element_type=F32) + p_new * v_new) / l

        n_buf = kw_ref.shape[0]
        s = lax.dot_general(qb, kw_ref[:, g, :].astype(BF16), nt, preferred_element_type=F32)
        keep = lax.broadcasted_iota(jnp.int32, s.shape, 1) > n_buf - WINDOW
        s = jnp.where(keep, s, NEG_BIG)
        kw_new = wnew_ref[g:g + 1, :]
        vw_new = wnew_ref[G + g:G + g + 1, :]
        s_new = jnp.sum(q * kw_new, axis=-1, keepdims=True)
        m = jnp.maximum(jnp.max(s, axis=-1, keepdims=True), s_new)
        p = jnp.exp(s - m)
        p_new = jnp.exp(s_new - m)
        l = jnp.sum(p, axis=-1, keepdims=True) + p_new
        o_win = (jnp.dot(p.astype(BF16), vw_ref[:, g, :].astype(BF16), preferred_element_type=F32)
                 + p_new * vw_new) / l
        gates = jnp.broadcast_to(gate_ref[g:g + 1, :], (n_rows, 128))
        lane = lax.broadcasted_iota(jnp.int32, (n_rows, 128), 1)
        head = lax.broadcasted_iota(jnp.int32, (n_rows, 128), 0)
        pick = lambda c: jnp.sum(jnp.where(lane == head * 3 + c, gates, 0.0), axis=-1, keepdims=True)
        o_ref[g] = (pick(0) * oc_ref[g] + pick(1) * o_sel + pick(2) * o_win).astype(o_ref.dtype)


def nsa_attend_decode(qr, o_cmp, sel_idx, gates, cache, layer, page_table, new_rows, state_win, new_win):
    n_batch, n_heads, hd = qr.shape
    page = cache.shape[2]
    n_pages = page_table.shape[1]
    n_top = sel_idx.shape[2]
    per_page = page // SEL_BLOCK
    n_buf = state_win.shape[2]
    G, R = NSA_KV_HEADS, NSA_REP
    rp = 16
    pad_heads = lambda a: jnp.pad(a.reshape(n_batch, G, R, hd), ((0, 0), (0, 0), (0, rp - R), (0, 0)))
    per_bg = lambda: pl.BlockSpec((None, G, rp, hd), lambda b, pt, sel: (b, 0, 0, 0))
    per_b = lambda rows: pl.BlockSpec((None, rows, hd), lambda b, pt, sel: (b, 0, 0))
    win_spec = lambda kv: pl.BlockSpec((None, None, n_buf, None, G, hd), lambda b, pt, sel: (layer, b, 0, kv, 0, 0))
    kern = functools.partial(_attn_decode_kernel, layer=layer, n_pages=n_pages, per_page=per_page, n_top=n_top)
    out = pl.pallas_call(
        kern,
        grid_spec=pltpu.PrefetchScalarGridSpec(
            num_scalar_prefetch=2,
            grid=(n_batch,),
            in_specs=[per_bg(), pl.BlockSpec(memory_space=pl.ANY), per_b(4 * G), win_spec(0), win_spec(1),
                      per_b(2 * G), per_bg(), per_b(G)],
            out_specs=per_bg(),
            scratch_shapes=[pltpu.VMEM((G, n_top * SEL_BLOCK, hd), F32), pltpu.VMEM((G, n_top * SEL_BLOCK, hd), F32),
                            pltpu.SemaphoreType.DMA(())]),
        out_shape=jax.ShapeDtypeStruct((n_batch, G, rp, hd), BF16),
        compiler_params=_params("arbitrary"),
        name="nsa_attend_decode",
    )(page_table.reshape(-1).astype(jnp.int32), sel_idx.reshape(-1).astype(jnp.int32),
      pad_heads(qr), cache, new_rows, state_win, state_win, new_win, pad_heads(o_cmp), gates)
    return out[:, :, :R].reshape(n_batch, n_heads * hd)


def _gla_decode_kernel(q_ref, k_ref, v_ref, r_ref, small_ref, w2_ref, b2_ref, ng_ref, s0_ref, y_ref, sf_ref):
    dk, dv = GLA_DK, GLA_DV
    z = jnp.dot(small_ref[:, 0:GLA_RANK].astype(BF16), w2_ref[...].astype(BF16),
                preferred_element_type=F32) + b2_ref[...]
    g_all = (jnp.minimum(z, 0.0) - jnp.log1p(jnp.exp(-jnp.abs(z)))) / GLA_GATE_NORM
    eye = lax.broadcasted_iota(jnp.int32, (dk, dk), 0) == lax.broadcasted_iota(jnp.int32, (dk, dk), 1)
    column = lambda row: jnp.sum(jnp.where(eye, jnp.broadcast_to(row, (dk, dk)), 0.0), axis=1, keepdims=True)
    for b in range(q_ref.shape[0]):
        for h in range(GLA_HEADS):
            ks = slice(h * dk, (h + 1) * dk)
            vs = slice(h * dv, (h + 1) * dv)
            s_new = (jnp.exp(column(g_all[b:b + 1, ks])) * s0_ref[b, h]
                     + column(k_ref[b:b + 1, ks]) * v_ref[b:b + 1, vs])
            sf_ref[b, h] = s_new
            o = jnp.sum(column(q_ref[b:b + 1, ks] * (dk ** -0.5)) * s_new, axis=0, keepdims=True)
            o = o * lax.rsqrt(jnp.mean(o * o, axis=-1, keepdims=True) + RMS_EPS)
            y_ref[b:b + 1, vs] = (o * ng_ref[:, vs] * jax.nn.silu(r_ref[b:b + 1, vs])).astype(BF16)


def gla_decode(p, s0, w2, b2, norm_g):
    n = p.shape[0]
    row = lambda w, off: pl.BlockSpec((n, w), lambda i: (0, off // w))
    const = lambda shape: pl.BlockSpec(shape, lambda i: (0,) * len(shape))
    return pl.pallas_call(
        _gla_decode_kernel,
        grid=(1,),
        in_specs=[row(256, COL_GLA_Q), row(256, COL_GLA_K), row(512, COL_GLA_V), row(512, COL_GLA_R),
                  row(128, COL_SMALL), const((GLA_RANK, GLA_HEADS * GLA_DK)), const((1, GLA_HEADS * GLA_DK)),
                  const((1, GLA_WIDTH)), const(s0.shape)],
        out_specs=(const((n, GLA_WIDTH)), const(s0.shape)),
        out_shape=(jax.ShapeDtypeStruct((n, GLA_WIDTH), BF16), jax.ShapeDtypeStruct(s0.shape, F32)),
        compiler_params=_params("arbitrary"),
        name="gla_decode",
    )(p, p, p, p, p, w2, b2.reshape(1, -1), norm_g.reshape(1, -1), s0)


def _pool_decode_kernel(u_ref, prev_ref, w_ref, sc_ref, y_ref, *, past_len):
    gd = POOL_GROUP_DIM
    n_prev = prev_ref.shape[1]
    for gi, w in enumerate(POOL_WINDOWS):
        cols = slice(gi * gd, (gi + 1) * gd)
        x = u_ref[:, cols]
        s = x
        for r in range(n_prev - (w - 1), n_prev):
            s = s + prev_ref[:, r, cols]
        pooled = s / float(min(past_len + 1, w)) - x
        y = jnp.dot(pooled.astype(BF16), w_ref[gi].astype(BF16), preferred_element_type=F32)
        y_ref[:, cols] = (y * sc_ref[:, cols]).astype(BF16)


def pool_decode(p, prev, past_len, w_pool, scale):
    n = p.shape[0]
    const = lambda shape: pl.BlockSpec(shape, lambda i: (0,) * len(shape))
    return pl.pallas_call(
        functools.partial(_pool_decode_kernel, past_len=past_len),
        grid=(1,),
        in_specs=[pl.BlockSpec((n, POOL_WIDTH), lambda i: (0, COL_POOL // POOL_WIDTH)), const(prev.shape),
                  const(w_pool.shape), const((1, POOL_WIDTH))],
        out_specs=const((n, POOL_WIDTH)),
        out_shape=jax.ShapeDtypeStruct((n, POOL_WIDTH), BF16),
        compiler_params=_params("arbitrary"),
        name="pool_decode",
    )(p, prev, w_pool, scale.reshape(1, -1))


def decode_mixer(xs2, w_in_packed, w_out_bf16, ln_g, ln_b, gla_w2, gla_b, gla_norm_g, cmp_pos, cmp_w1, cmp_w2,
                 pool_w, pool_scale, cache, layer, page_table, state_win, state_gla, state_pool, past_len,
                 router=None):
    n_dec = xs2.shape[0]
    hd, G = HEAD_DIM, NSA_KV_HEADS
    p = matmul(xs2, w_in_packed)
    pos = jnp.full((n_dec,), past_len, jnp.int32)
    rows, win, qn, qr, _, _, _, _, gates = nsa_prep(p, rope_tables(pos), 1, n_dec)
    cmp_kv = nsa_compress_paged(cache, layer, page_table, cmp_pos, cmp_w1, cmp_w2)
    t_k = past_len + 1
    n_sel = -(-t_k // SEL_BLOCK)
    o_cmp, sel_idx = nsa_select_decode(qn[0].transpose(1, 0, 2), cmp_kv, past_len // CMP_STRIDE - 1, n_sel, past_len)
    y_nsa = nsa_attend_decode(qr[0].transpose(1, 0, 2), o_cmp, sel_idx, gates[0].transpose(1, 0, 2), cache, layer,
                              page_table, rows.reshape(n_dec, 4 * G, hd), state_win, win.reshape(n_dec, 2 * G, hd))
    y_gla, s_gla = gla_decode(p, state_gla, gla_w2, gla_b, gla_norm_g)
    y_pool = pool_decode(p, state_pool, past_len, pool_w, pool_scale)
    pool_rows = jnp.concatenate([state_pool[:, 1:], p[:, None, COL_POOL:COL_POOL + POOL_WIDTH]], axis=1)
    x1 = outproj_ln(xs2, y_gla, y_nsa, y_pool, w_out_bf16, ln_g, ln_b, router)
    nsa_rows = rows.reshape(n_dec, 1, 4, G, hd)
    new_win = jnp.concatenate([state_win[layer, :, 1:], win.reshape(n_dec, 1, 2, G, hd)], axis=1)
    return x1, nsa_rows, new_win, s_gla, pool_rows


def prompt_mixer(x2, w_in_packed, w_out_bf16, ln_g, ln_b, gla_w2, gla_b, gla_norm_g, cmp_pos, cmp_w1, cmp_w2,
                 pool_w, pool_scale, n_batch, t_len, router=None):
    p = matmul(x2, w_in_packed)
    pos = jnp.arange(t_len, dtype=jnp.int32)
    rows, win, qn, qr, ks, vs, kw, vw, gates = nsa_prep(p, rope_tables(pos), n_batch, t_len)
    cmp_kv = nsa_compress_prompt(rows, cmp_pos, cmp_w1, cmp_w2, n_batch, t_len)
    o_cmp, selb = nsa_select(qn, cmp_kv, t_len // CMP_STRIDE - 1, t_len)
    y_nsa = nsa_attend(qr, o_cmp, selb, gates, ks, vs, kw, vw)
    s0 = jnp.zeros((n_batch, GLA_HEADS, GLA_DK, GLA_DV), F32)
    y_gla, s_gla = gla_mix(p, s0, gla_w2, gla_b, gla_norm_g, n_batch, t_len)
    prev = jnp.zeros((n_batch, POOL_MAX - 1, POOL_WIDTH), F32)
    y_pool = pool_mix(p, prev, 0, pool_w, pool_scale, n_batch, t_len)
    x1 = outproj_ln(x2, y_gla, y_nsa, y_pool, w_out_bf16, ln_g, ln_b, router)
    nsa_rows = rows.reshape(n_batch, t_len, 4, NSA_KV_HEADS, HEAD_DIM)
    n_win = min(WINDOW, t_len)
    win_rows = win.reshape(n_batch, t_len, 2, NSA_KV_HEADS, HEAD_DIM)[:, t_len - n_win:]
    pool_rows = p.reshape(n_batch, t_len, PACKED_WIDTH)[:, t_len - (POOL_MAX - 1):, COL_POOL:COL_POOL + POOL_WIDTH]
    return x1, nsa_rows, win_rows, s_gla, pool_rows


def kernel(x_prompt, x_sample, cache_nsa, page_table, state_win, state_gla, state_pool, w_in, gla_gate_w2, gla_gate_b, gla_norm_g, nsa_cmp_pos, nsa_cmp_w1, nsa_cmp_w2, pool_w, pool_scale, w_out, ln1_g, ln1_b, ln2_g, ln2_b, ffn_w_gate, ffn_w_up, ffn_w_down, moe_router, moe_w_gate, moe_w_up, moe_w_down):
    n_prompt, t_len, d = x_prompt.shape
    n_dec = x_sample.shape[0]
    xp, xs = x_prompt.reshape(n_prompt * t_len, d), x_sample.reshape(n_dec, d)
    nsa_p, nsa_s, win_p, win_s, gla_p, gla_s, pool_p, pool_s = [], [], [], [], [], [], [], []
    for l in range(DEPTH):
        i = l // 2
        router = moe_router[i] if l % 2 else None
        w_in_packed, w_out_bf16 = pack_w_in(w_in[l]), w_out[l].astype(BF16)
        lw = (w_in_packed, w_out_bf16, ln1_g[l], ln1_b[l], gla_gate_w2[l], gla_gate_b[l], gla_norm_g[l],
              nsa_cmp_pos[l], nsa_cmp_w1[l], nsa_cmp_w2[l], pool_w[l], pool_scale[l])
        xp, r_p, w_p, g_p, p_p = prompt_mixer(xp, *lw, n_prompt, t_len, router)
        xs, r_s, w_s, g_s, p_s = decode_mixer(xs, *lw, cache_nsa, l, page_table, state_win, state_gla[l],
                                              state_pool[l], PAST_LEN, router)
        if l % 2 == 0:
            wg, wu, wd = ffn_w_gate[i].astype(BF16), ffn_w_up[i].astype(BF16), ffn_w_down[i].astype(BF16)
            xp = ffn_ln(xp, wg, wu, wd, ln2_g[l], ln2_b[l])
            xs = ffn_ln(xs, wg, wu, wd, ln2_g[l], ln2_b[l])
        else:
            (xp, lg_p), (xs, lg_s) = xp, xs
            xp, xs = moe_ln([xp, xs], [lg_p, lg_s], moe_w_gate[i], moe_w_up[i], moe_w_down[i], ln2_g[l], ln2_b[l])
        nsa_p.append(r_p); nsa_s.append(r_s); win_p.append(w_p); win_s.append(w_s)
        gla_p.append(g_p); gla_s.append(g_s); pool_p.append(p_p); pool_s.append(p_s)
    return (xp.reshape(n_prompt, t_len, d), xs.reshape(x_sample.shape), jnp.stack(nsa_p), jnp.stack(nsa_s),
            jnp.stack(win_p), jnp.stack(win_s), jnp.stack(gla_p), jnp.stack(gla_s), jnp.stack(pool_p),
            jnp.stack(pool_s))
```

```python
import functools

import jax
import jax.numpy as jnp
import numpy as np
from jax import lax
from jax.experimental import pallas as pl
from jax.experimental.pallas import tpu as pltpu

D_MODEL = 2048
DEPTH = 2
PAST_LEN = 16384
HEAD_DIM = 128
GLA_HEADS = 4
GLA_DK = 64
GLA_DV = 128
GLA_RANK = 16
GLA_GATE_NORM = 16.0
GLA_CHUNK = 64
GLA_WIDTH = GLA_HEADS * GLA_DV
NSA_HEADS = 8
NSA_KV_HEADS = 2
NSA_REP = NSA_HEADS // NSA_KV_HEADS
NSA_WIDTH = NSA_HEADS * HEAD_DIM
CMP_LEN = 32
CMP_STRIDE = 16
CMP_HIDDEN = 128
SEL_BLOCK = 64
SEL_TOPN = 16
WINDOW = 512
POOL_GROUPS = 4
POOL_GROUP_DIM = 128
POOL_WIDTH = POOL_GROUPS * POOL_GROUP_DIM
POOL_WINDOWS = (2, 4, 8, 16)
POOL_MAX = 16
ROPE_THETA = 500000.0
ROPE_DIM = HEAD_DIM // 4
N_EXPERTS = 8
TOP_K = 2
ALPHA = (2 * DEPTH) ** 0.25
LN_EPS = 1e-5
RMS_EPS = 1e-6

PROJ_SIZES = (
    ('gla_q', GLA_HEADS * GLA_DK), ('gla_k', GLA_HEADS * GLA_DK), ('gla_v', GLA_HEADS * GLA_DV),
    ('gla_glr', GLA_RANK), ('gla_r', GLA_HEADS * GLA_DV),
    ('nsa_q', NSA_HEADS * HEAD_DIM),
    ('cmp_k', NSA_KV_HEADS * HEAD_DIM), ('cmp_v', NSA_KV_HEADS * HEAD_DIM),
    ('slc_k', NSA_KV_HEADS * HEAD_DIM), ('slc_v', NSA_KV_HEADS * HEAD_DIM),
    ('win_k', NSA_KV_HEADS * HEAD_DIM), ('win_v', NSA_KV_HEADS * HEAD_DIM),
    ('nsa_gate', 3 * NSA_HEADS),
    ('pool', POOL_WIDTH),
)

GLA_SUB = 16
SEL_PAD = 128

BF16 = jnp.bfloat16
F32 = jnp.float32
NEG_BIG = -1e30
VMEM_LIMIT_BYTES = 56 * 1024 * 1024

COL_NSA_Q = 0
COL_ROWS = 1024
COL_WIN = 2048
COL_POOL = 2560
COL_GLA_V = 3072
COL_GLA_R = 3584
COL_GLA_Q = 4096
COL_GLA_K = 4352
COL_SMALL = 4608
PACKED_WIDTH = 4736
SMALL_GATE_OFF = GLA_RANK


def _params(*sem, vmem_limit_bytes=VMEM_LIMIT_BYTES):
    return pltpu.CompilerParams(dimension_semantics=sem, vmem_limit_bytes=vmem_limit_bytes)


def _proj_offsets():
    out, off = {}, 0
    for name, size in PROJ_SIZES:
        out[name] = (off, size)
        off += size
    return out


def pack_w_in(w):
    offs = _proj_offsets()
    sl = lambda n: w[:, offs[n][0]:offs[n][0] + offs[n][1]]
    pad = jnp.zeros((w.shape[0], 128 - GLA_RANK - 3 * NSA_HEADS), w.dtype)
    cols = [sl('nsa_q'), sl('cmp_k'), sl('cmp_v'), sl('slc_k'), sl('slc_v'), sl('win_k'), sl('win_v'),
            sl('pool'), sl('gla_v'), sl('gla_r'), sl('gla_q'), sl('gla_k'), sl('gla_glr'), sl('nsa_gate'), pad]
    return jnp.concatenate(cols, axis=1).astype(BF16)


def rope_tables(pos):
    half = ROPE_DIM // 2
    inv_freq = ROPE_THETA ** (-jnp.arange(half, dtype=F32) / half)
    ang = pos.astype(F32)[:, None] * inv_freq[None, :]
    cos, sin = jnp.cos(ang), jnp.sin(ang)
    t = pos.shape[0]
    c = jnp.concatenate([cos, cos, jnp.ones((t, HEAD_DIM - ROPE_DIM), F32)], axis=1)
    sa = jnp.concatenate([-sin, jnp.zeros((t, HEAD_DIM - half), F32)], axis=1)
    sb = jnp.concatenate([jnp.zeros((t, half), F32), sin, jnp.zeros((t, HEAD_DIM - ROPE_DIM), F32)], axis=1)
    return c, sa, sb


def _pick_tile(n, pref):
    for t in pref:
        if n % t == 0:
            return t
    return n


def _mm_kernel(x_ref, w_ref, o_ref, xb_ref):
    @pl.when(pl.program_id(1) == 0)
    def _():
        xb_ref[...] = x_ref[...].astype(BF16)

    o_ref[...] = jnp.dot(xb_ref[...], w_ref[...].astype(BF16), preferred_element_type=F32)


def matmul(x, w):
    m, k = x.shape
    n = w.shape[1]
    tm = _pick_tile(m, tuple(t for t in (1024, 512, 256, 128, 64, 32, 16, 8) if t * k <= 2048 * 1024))
    tn = 512 if n >= 512 else n
    return pl.pallas_call(
        _mm_kernel,
        grid=(m // tm, pl.cdiv(n, tn)),
        in_specs=[pl.BlockSpec((tm, k), lambda i, j: (i, 0)),
                  pl.BlockSpec((k, tn), lambda i, j: (0, j))],
        out_specs=pl.BlockSpec((tm, tn), lambda i, j: (i, j)),
        out_shape=jax.ShapeDtypeStruct((m, n), F32),
        scratch_shapes=[pltpu.VMEM((tm, k), BF16)],
        compiler_params=_params("arbitrary", "arbitrary"),
        name="matmul",
    )(x, w)


def _rope(x, c, sa, sb):
    return x * c + pltpu.roll(x, HEAD_DIM - ROPE_DIM // 2, 1) * sa + pltpu.roll(x, ROPE_DIM // 2, 1) * sb


def _nsa_prep_kernel(q_ref, rows_ref, win_ref, small_ref, c_ref, sa_ref, sb_ref,
                     rows_o, win_o, qn_o, qr_o, ks_o, vs_o, kw_o, vw_o, gate_o):
    c, sa, sb = c_ref[...], sa_ref[...], sb_ref[...]
    scale = HEAD_DIM ** -0.5
    hd = HEAD_DIM
    for h in range(NSA_HEADS):
        x = q_ref[:, h * hd:(h + 1) * hd]
        qn_o[0, h] = (x * scale).astype(BF16)
        qr_o[0, h] = (_rope(x, c, sa, sb) * scale).astype(BF16)
    ones = jnp.ones((q_ref.shape[0], hd), BF16)
    rows_o[:, 0:4 * hd] = rows_ref[:, 0:4 * hd]
    for g in range(NSA_KV_HEADS):
        k = _rope(rows_ref[:, (4 + g) * hd:(5 + g) * hd], c, sa, sb)
        rows_o[:, (4 + g) * hd:(5 + g) * hd] = k
        ks_o[0, g] = k.astype(BF16)
        v = rows_ref[:, (6 + g) * hd:(7 + g) * hd]
        rows_o[:, (6 + g) * hd:(7 + g) * hd] = v
        vs_o[0, g, :, 0:hd] = v.astype(BF16)
        vs_o[0, g, :, hd:2 * hd] = ones
        k = _rope(win_ref[:, g * hd:(g + 1) * hd], c, sa, sb)
        win_o[:, g * hd:(g + 1) * hd] = k
        kw_o[0, g] = k.astype(BF16)
        v = win_ref[:, (2 + g) * hd:(3 + g) * hd]
        win_o[:, (2 + g) * hd:(3 + g) * hd] = v
        vw_o[0, g, :, 0:hd] = v.astype(BF16)
        vw_o[0, g, :, hd:2 * hd] = ones
    sig = jax.nn.sigmoid(small_ref[...])
    per_g = 3 * NSA_REP
    for g in range(NSA_KV_HEADS):
        gate_o[0, g] = pltpu.roll(sig, 128 - SMALL_GATE_OFF - g * per_g, 1)


def nsa_prep(p, tables, n_batch, t_len):
    tr = _pick_tile(t_len, (512, 256, 128, 64, 32, 16))
    nt = t_len // tr
    n = n_batch * t_len
    hd = HEAD_DIM
    row = lambda w, cb: pl.BlockSpec((tr, w), lambda b, i: (b * nt + i, cb))
    tab = pl.BlockSpec((tr, hd), lambda b, i: (i, 0))
    head = lambda nh, w: pl.BlockSpec((1, nh, tr, w), lambda b, i: (b, 0, i, 0))
    out_shape = (
        jax.ShapeDtypeStruct((n, 8 * hd), F32),
        jax.ShapeDtypeStruct((n, 4 * hd), F32),
        jax.ShapeDtypeStruct((n_batch, NSA_HEADS, t_len, hd), BF16),
        jax.ShapeDtypeStruct((n_batch, NSA_HEADS, t_len, hd), BF16),
        jax.ShapeDtypeStruct((n_batch, NSA_KV_HEADS, t_len, hd), BF16),
        jax.ShapeDtypeStruct((n_batch, NSA_KV_HEADS, t_len, 2 * hd), BF16),
        jax.ShapeDtypeStruct((n_batch, NSA_KV_HEADS, t_len, hd), BF16),
        jax.ShapeDtypeStruct((n_batch, NSA_KV_HEADS, t_len, 2 * hd), BF16),
        jax.ShapeDtypeStruct((n_batch, NSA_KV_HEADS, t_len, 128), F32),
    )
    return pl.pallas_call(
        _nsa_prep_kernel,
        grid=(n_batch, nt),
        in_specs=[row(8 * hd, COL_NSA_Q // (8 * hd)), row(8 * hd, COL_ROWS // (8 * hd)),
                  row(4 * hd, COL_WIN // (4 * hd)), row(128, COL_SMALL // 128), tab, tab, tab],
        out_specs=(row(8 * hd, 0), row(4 * hd, 0), head(NSA_HEADS, hd), head(NSA_HEADS, hd),
                   head(NSA_KV_HEADS, hd), head(NSA_KV_HEADS, 2 * hd), head(NSA_KV_HEADS, hd),
                   head(NSA_KV_HEADS, 2 * hd), head(NSA_KV_HEADS, 128)),
        out_shape=out_shape,
        compiler_params=_params("arbitrary", "arbitrary"),
        name="nsa_prep",
    )(p, p, p, p, *tables)


def _nsa_cmp_kernel(x_ref, pe_ref, w1_ref, w2_ref, o_ref):
    nh = o_ref.shape[0]
    h_lo = jnp.zeros((nh, CMP_HIDDEN), F32)
    h_hi = jnp.zeros((nh, CMP_HIDDEN), F32)
    for j in range(CMP_STRIDE):
        xj = x_ref[pl.ds(j, nh, stride=CMP_STRIDE), :]
        h_lo += jnp.dot((xj + pe_ref[j:j + 1, :]).astype(BF16), w1_ref[j].astype(BF16), preferred_element_type=F32)
        h_hi += jnp.dot((xj + pe_ref[CMP_STRIDE + j:CMP_STRIDE + j + 1, :]).astype(BF16),
                        w1_ref[CMP_STRIDE + j].astype(BF16), preferred_element_type=F32)
    h = jax.nn.gelu(h_lo + pltpu.roll(h_hi, nh - 1, 0))
    o_ref[...] = jnp.dot(h.astype(BF16), w2_ref[...].astype(BF16), preferred_element_type=F32).astype(BF16)


def nsa_compress_prompt(rows, cmp_pos, cmp_w1, cmp_w2, n_batch, t_len):
    nh = t_len // CMP_STRIDE
    hd = HEAD_DIM
    rows3 = rows.reshape(n_batch, t_len, 8 * hd)
    return pl.pallas_call(
        _nsa_cmp_kernel,
        grid=(n_batch, 2, NSA_KV_HEADS),
        in_specs=[pl.BlockSpec((None, t_len, hd), lambda b, kd, g: (b, 0, kd * NSA_KV_HEADS + g)),
                  pl.BlockSpec((None, CMP_LEN, hd), lambda b, kd, g: (kd, 0, 0)),
                  pl.BlockSpec((None, CMP_LEN, hd, CMP_HIDDEN), lambda b, kd, g: (kd, 0, 0, 0)),
                  pl.BlockSpec((None, CMP_HIDDEN, hd), lambda b, kd, g: (kd, 0, 0))],
        out_specs=pl.BlockSpec((None, None, None, nh, hd), lambda b, kd, g: (b, kd, g, 0, 0)),
        out_shape=jax.ShapeDtypeStruct((n_batch, 2, NSA_KV_HEADS, nh, hd), BF16),
        compiler_params=_params("arbitrary", "arbitrary", "arbitrary"),
        name="nsa_compress",
    )(rows3, cmp_pos, cmp_w1, cmp_w2)


def _nsa_select_kernel(qn_ref, kc_ref, vc_ref, ovt_ref, oc_ref, selb_ref, *, n_cmp, n_top):
    rep, tq, hd = qn_ref.shape[1], qn_ref.shape[2], qn_ref.shape[3]
    n_cmp_pad = kc_ref.shape[0]
    n_sel = ovt_ref.shape[0]
    q0 = pl.program_id(2) * tq
    q = qn_ref[0].reshape(rep * tq, hd)
    s = lax.dot_general(q, kc_ref[...], (((1,), (1,)), ((), ())), preferred_element_type=F32)
    row = lax.broadcasted_iota(jnp.int32, (rep * tq, n_cmp_pad), 0)
    col = lax.broadcasted_iota(jnp.int32, (rep * tq, n_cmp_pad), 1)
    qpos = q0 + (row & (tq - 1))
    mask = (col * CMP_STRIDE + (CMP_LEN - 1) <= qpos) & (col < n_cmp)
    s = jnp.where(mask, s, -jnp.inf)
    m = jnp.max(s, axis=-1, keepdims=True)
    m = jnp.where(m > -jnp.inf, m, 0.0)
    p = jnp.where(mask, jnp.exp(s - m), 0.0)
    p = p / jnp.maximum(jnp.sum(p, axis=-1, keepdims=True), 1e-30)
    oc = jnp.dot(p.astype(BF16), vc_ref[...], preferred_element_type=F32)
    oc_ref[0] = oc.reshape(rep, tq, hd).astype(BF16)
    psum = p[0:tq]
    for r in range(1, rep):
        psum = psum + p[r * tq:(r + 1) * tq]
    imp = lax.dot_general(ovt_ref[...], psum, (((1,), (1,)), ((), ())), preferred_element_type=F32,
                          precision=lax.Precision.HIGHEST)
    blk = lax.broadcasted_iota(jnp.int32, (n_sel, tq), 0)
    cur = (q0 + lax.broadcasted_iota(jnp.int32, (n_sel, tq), 1)) // SEL_BLOCK
    forced = (blk == 0) | (blk == cur) | (blk == cur - 1)
    v = jnp.where(blk <= cur, jnp.where(forced, jnp.inf, imp), -jnp.inf)
    rank = jnp.zeros((n_sel, tq), jnp.int32)
    for i in range(n_sel):
        vi = v[i:i + 1, :]
        ahead = (vi > v) | ((vi == v) & (blk > i))
        rank = rank + ahead.astype(jnp.int32)
    selb_t = jnp.where((rank < n_top) & (v > -jnp.inf), 0.0, NEG_BIG)
    pad = jnp.full((SEL_PAD - n_sel, tq), NEG_BIG, F32)
    selb_ref[0, 0] = jnp.concatenate([selb_t, pad], axis=0).T.astype(BF16)


def nsa_select(qn, cmp_kv, n_cmp, t_k):
    n_batch, _, t_len, hd = qn.shape
    n_cmp_pad = cmp_kv.shape[3]
    n_sel = -(-t_k // SEL_BLOCK)
    tq = _pick_tile(t_len, (512, 256, 128, 64, 32, 16))
    ci = np.arange(n_cmp_pad)[None, :]
    sj = np.arange(n_sel)[:, None]
    overlap_t = ((ci * CMP_STRIDE <= sj * SEL_BLOCK + SEL_BLOCK - 1) &
                 (ci * CMP_STRIDE + CMP_LEN - 1 >= sj * SEL_BLOCK) & (ci < n_cmp)).astype(np.float32)
    kern = functools.partial(_nsa_select_kernel, n_cmp=n_cmp, n_top=min(SEL_TOPN, n_sel))
    return pl.pallas_call(
        kern,
        grid=(n_batch, NSA_KV_HEADS, t_len // tq),
        in_specs=[pl.BlockSpec((1, NSA_REP, tq, hd), lambda b, g, i: (b, g, i, 0)),
                  pl.BlockSpec((None, None, None, n_cmp_pad, hd), lambda b, g, i: (b, 0, g, 0, 0)),
                  pl.BlockSpec((None, None, None, n_cmp_pad, hd), lambda b, g, i: (b, 1, g, 0, 0)),
                  pl.BlockSpec((n_sel, n_cmp_pad), lambda b, g, i: (0, 0))],
        out_specs=(pl.BlockSpec((1, NSA_REP, tq, hd), lambda b, g, i: (b, g, i, 0)),
                   pl.BlockSpec((1, 1, tq, SEL_PAD), lambda b, g, i: (b, g, i, 0))),
        out_shape=(jax.ShapeDtypeStruct((n_batch, NSA_HEADS, t_len, hd), BF16),
                   jax.ShapeDtypeStruct((n_batch, NSA_KV_HEADS, t_len, SEL_PAD), BF16)),
        compiler_params=_params("arbitrary", "arbitrary", "arbitrary"),
        name="nsa_select",
    )(qn, cmp_kv, cmp_kv, overlap_t)


ATTN_Q_BLOCK = 256
ATTN_K_TILE = 1024


def _nsa_attn_kernel(qr_ref, oc_ref, selb_ref, gate_ref, ks_ref, vs_ref, kw_ref, vw_ref, e_ref, cb_ref, wb_ref, o_ref,
                     m_scr, acc_scr, *, tk, wk):
    rep, qb, hd = qr_ref.shape[1], qr_ref.shape[2], qr_ref.shape[3]
    nr = rep * qb
    q0 = pl.program_id(2) * qb
    q = qr_ref[0].reshape(nr, hd)
    selb = selb_ref[0, 0]
    nt = (((1,), (1,)), ((), ()))

    def sel_scores(t, extra_bias=None):
        k = ks_ref[0, 0, pl.ds(pl.multiple_of(t * tk, tk), tk), :]
        s = lax.dot_general(q, k, nt, preferred_element_type=F32)
        bias = jnp.dot(selb, e_ref[t], preferred_element_type=F32)
        if extra_bias is not None:
            bias = bias + extra_bias
        return (s.reshape(rep, qb, tk) + bias[None]).reshape(nr, tk)

    def sel_values(t):
        return vs_ref[0, 0, pl.ds(pl.multiple_of(t * tk, tk), tk), :]

    td = q0 // tk
    s = sel_scores(td, cb_ref[0])
    m = jnp.max(s, axis=-1, keepdims=True)
    m_scr[...] = m
    acc_scr[...] = jnp.dot(jnp.exp(s - m).astype(BF16), sel_values(td), preferred_element_type=F32)

    def body(t, carry):
        s = sel_scores(t)
        m_old = m_scr[...]
        m_new = jnp.maximum(m_old, jnp.max(s, axis=-1, keepdims=True))
        p = jnp.exp(s - m_new).astype(BF16)
        acc_scr[...] = jnp.exp(m_old - m_new) * acc_scr[...] + jnp.dot(p, sel_values(t), preferred_element_type=F32)
        m_scr[...] = m_new
        return carry

    kstart = pl.multiple_of(jnp.maximum(q0 - WINDOW, 0), qb)
    kw = kw_ref[0, 0, pl.ds(kstart, wk), :]
    s = lax.dot_general(q, kw, nt, preferred_element_type=F32)
    s = (s.reshape(rep, qb, wk) + wb_ref[0][None]).reshape(nr, wk)
    m = jnp.max(s, axis=-1, keepdims=True)
    accw = jnp.dot(jnp.exp(s - m).astype(BF16), vw_ref[0, 0, pl.ds(kstart, wk), :], preferred_element_type=F32)
    o_win = accw[:, 0:hd] / jnp.maximum(accw[:, hd:hd + 1], 1e-30)

    lax.fori_loop(0, td, body, 0)
    acc = acc_scr[...]
    o_sel = acc[:, 0:hd] / jnp.maximum(acc[:, hd:hd + 1], 1e-30)

    gates = gate_ref[0, 0]
    for r in range(rep):
        rows = slice(r * qb, (r + 1) * qb)
        o = (gates[:, 3 * r:3 * r + 1] * oc_ref[0, r].astype(F32)
             + gates[:, 3 * r + 1:3 * r + 2] * o_sel[rows]
             + gates[:, 3 * r + 2:3 * r + 3] * o_win[rows])
        o_ref[:, r * hd:(r + 1) * hd] = o.astype(BF16)


def nsa_attend(qr, o_cmp, selb, gates, ks, vs, kw, vw):
    n_batch, _, t_len, hd = qr.shape
    n_sel = selb.shape[3]
    qb = min(ATTN_Q_BLOCK, t_len)
    tk = min(ATTN_K_TILE, t_len)
    wk = min(WINDOW + qb, t_len)
    nq = t_len // qb
    n_tiles = t_len // tk
    key_blk = (np.arange(n_tiles)[:, None, None] * tk + np.arange(tk)[None, None, :]) // SEL_BLOCK
    e = (key_blk == np.arange(n_sel)[None, :, None]).astype(BF16)
    n_cv = tk // qb
    qi = np.arange(qb)[None, :, None]
    causal = np.where(np.arange(tk)[None, None, :] <= np.arange(n_cv)[:, None, None] * qb + qi, 0.0, NEG_BIG)
    n_wv = min(WINDOW // qb, nq - 1) + 1
    q0v = np.arange(n_wv)[:, None, None] * qb
    kpos = np.maximum(q0v - WINDOW, 0) + np.arange(wk)[None, None, :]
    qpos = q0v + qi
    window = np.where((kpos <= qpos) & (kpos > qpos - WINDOW), 0.0, NEG_BIG).astype(np.float32)
    kern = functools.partial(_nsa_attn_kernel, tk=tk, wk=wk)
    per_q = lambda nh, w: pl.BlockSpec((1, nh, qb, w), lambda b, g, i: (b, g, i, 0))
    full = lambda w: pl.BlockSpec((1, 1, t_len, w), lambda b, g, i: (b, g, 0, 0))
    return pl.pallas_call(
        kern,
        grid=(n_batch, NSA_KV_HEADS, nq),
        in_specs=[per_q(NSA_REP, hd), per_q(NSA_REP, hd), per_q(1, n_sel), per_q(1, 128),
                  full(hd), full(2 * hd), full(hd), full(2 * hd),
                  pl.BlockSpec((n_tiles, n_sel, tk), lambda b, g, i: (0, 0, 0)),
                  pl.BlockSpec((1, qb, tk), lambda b, g, i: (i % n_cv, 0, 0)),
                  pl.BlockSpec((1, qb, wk), lambda b, g, i: (jnp.minimum(i, n_wv - 1), 0, 0))],
        out_specs=pl.BlockSpec((qb, NSA_REP * hd), lambda b, g, i: (b * nq + i, g)),
        out_shape=jax.ShapeDtypeStruct((n_batch * t_len, NSA_HEADS * hd), BF16),
        scratch_shapes=[pltpu.VMEM((NSA_REP * qb, 1), F32), pltpu.VMEM((NSA_REP * qb, 2 * hd), F32)],
        compiler_params=_params("arbitrary", "arbitrary", "arbitrary"),
        name="nsa_attend",
    )(qr, o_cmp, selb, gates, ks, vs, kw, vw, e, causal.astype(np.float32), window)


def _gla_kernel(q_ref, k_ref, v_ref, r_ref, small_ref, w2_ref, b2_ref, ng_ref, s0_ref, y_ref, sf_ref, s_scr):
    tb = q_ref.shape[0]
    c, sub, dk, dv = GLA_CHUNK, GLA_SUB, GLA_DK, GLA_DV
    n_sub = c // sub
    t = pl.program_id(1)

    @pl.when(t == 0)
    def _():
        s_scr[...] = s0_ref[0]

    z = jnp.dot(small_ref[:, 0:GLA_RANK].astype(BF16), w2_ref[...].astype(BF16),
                preferred_element_type=F32) + b2_ref[...]
    g_all = (jnp.minimum(z, 0.0) - jnp.log1p(jnp.exp(-jnp.abs(z)))) / GLA_GATE_NORM
    ri = lax.broadcasted_iota(jnp.int32, (c, c), 0)
    ci = lax.broadcasted_iota(jnp.int32, (c, c), 1)
    tril = ri >= ci
    cum = tril.astype(F32)
    rsub = lax.broadcasted_iota(jnp.int32, (c, dk), 0) // sub
    eye = lax.broadcasted_iota(jnp.int32, (dk, dk), 0) == lax.broadcasted_iota(jnp.int32, (dk, dk), 1)
    for cc in range(tb // c):
        rows = slice(cc * c, (cc + 1) * c)
        b_all = jnp.dot(cum, g_all[rows], preferred_element_type=F32, precision=lax.Precision.HIGHEST)
        for h in range(GLA_HEADS):
            b = b_all[:, h * dk:(h + 1) * dk]
            qh = q_ref[rows, h * dk:(h + 1) * dk] * (dk ** -0.5)
            kh = k_ref[rows, h * dk:(h + 1) * dk]
            vh = v_ref[rows, h * dv:(h + 1) * dv]
            a_rows = []
            for i in range(n_sub):
                ref = b[sub * i - 1:sub * i, :] if i else jnp.zeros((1, dk), F32)
                rs = slice(sub * i, sub * (i + 1))
                qi = (qh[rs] * jnp.exp(b[rs] - ref)).astype(BF16)
                ki = jnp.where(rsub <= i, kh * jnp.exp(ref - b), 0.0).astype(BF16)
                a_rows.append(lax.dot_general(qi, ki, (((1,), (1,)), ((), ())), preferred_element_type=F32))
            a = jnp.where(tril, jnp.concatenate(a_rows, axis=0), 0.0)
            s_old = s_scr[h]
            o = jnp.dot(a.astype(BF16), vh.astype(BF16), preferred_element_type=F32)
            o += jnp.dot((qh * jnp.exp(b)).astype(BF16), s_old.astype(BF16), preferred_element_type=F32)
            b_last = b[c - 1:c, :]
            ke = (kh * jnp.exp(b_last - b)).astype(BF16)
            upd = lax.dot_general(ke, vh.astype(BF16), (((0,), (0,)), ((), ())), preferred_element_type=F32)
            decay = jnp.exp(jnp.sum(jnp.where(eye, jnp.broadcast_to(b_last, (dk, dk)), 0.0), axis=1, keepdims=True))
            s_scr[h] = decay * s_old + upd
            o = o * lax.rsqrt(jnp.mean(o * o, axis=-1, keepdims=True) + RMS_EPS)
            y = o * ng_ref[:, h * dv:(h + 1) * dv] * jax.nn.silu(r_ref[rows, h * dv:(h + 1) * dv])
            y_ref[rows, h * dv:(h + 1) * dv] = y.astype(BF16)

    @pl.when(t == pl.num_programs(1) - 1)
    def _():
        sf_ref[0] = s_scr[...]


def gla_mix(p, s0, w2, b2, norm_g, n_batch, t_len):
    tb = _pick_tile(t_len, (256, 128, 64))
    nt = t_len // tb
    row = lambda w, off: pl.BlockSpec((tb, w), lambda b, i: (b * nt + i, off // w))
    const = lambda shape: pl.BlockSpec(shape, lambda b, i: (0,) * len(shape))
    return pl.pallas_call(
        _gla_kernel,
        grid=(n_batch, nt),
        in_specs=[row(256, COL_GLA_Q), row(256, COL_GLA_K), row(512, COL_GLA_V), row(512, COL_GLA_R),
                  row(128, COL_SMALL), const((GLA_RANK, GLA_HEADS * GLA_DK)), const((1, GLA_HEADS * GLA_DK)),
                  const((1, GLA_WIDTH)),
                  pl.BlockSpec((1, GLA_HEADS, GLA_DK, GLA_DV), lambda b, i: (b, 0, 0, 0))],
        out_specs=(pl.BlockSpec((tb, GLA_WIDTH), lambda b, i: (b * nt + i, 0)),
                   pl.BlockSpec((1, GLA_HEADS, GLA_DK, GLA_DV), lambda b, i: (b, 0, 0, 0))),
        out_shape=(jax.ShapeDtypeStruct((n_batch * t_len, GLA_WIDTH), BF16),
                   jax.ShapeDtypeStruct((n_batch, GLA_HEADS, GLA_DK, GLA_DV), F32)),
        scratch_shapes=[pltpu.VMEM((GLA_HEADS, GLA_DK, GLA_DV), F32)],
        compiler_params=_params("arbitrary", "arbitrary"),
        name="gla_mix",
    )(p, p, p, p, p, w2, b2.reshape(1, -1), norm_g.reshape(1, -1), s0)


def _pool_kernel(u_ref, prev_ref, cnt_ref, w_ref, sc_ref, y_ref, halo):
    tb = u_ref.shape[0]
    gd = POOL_GROUP_DIM

    @pl.when(pl.program_id(1) == 0)
    def _():
        halo[...] = prev_ref[0]

    ext = jnp.concatenate([halo[...], u_ref[...]], axis=0)
    halo[...] = ext[tb:tb + POOL_MAX]
    for gi, w in enumerate(POOL_WINDOWS):
        x = ext[:, gi * gd:(gi + 1) * gd]
        s = x
        shift = 1
        while shift < w:
            s = s + pltpu.roll(s, shift, 0)
            shift *= 2
        pooled = s[POOL_MAX:] / cnt_ref[:, gi:gi + 1] - x[POOL_MAX:]
        y = jnp.dot(pooled.astype(BF16), w_ref[gi].astype(BF16), preferred_element_type=F32)
        y_ref[:, gi * gd:(gi + 1) * gd] = (y * sc_ref[:, gi * gd:(gi + 1) * gd]).astype(BF16)


def pool_mix(p, prev, pos0, w_pool, scale, n_batch, t_len):
    tb = _pick_tile(t_len, (512, 256, 128, 64, 32, 16))
    nt = t_len // tb
    pos = pos0 + np.arange(t_len)
    cnt = np.ones((t_len, 128), np.float32)
    for gi, w in enumerate(POOL_WINDOWS):
        cnt[:, gi] = np.minimum(pos + 1, w)
    prev16 = jnp.pad(prev.astype(F32), ((0, 0), (1, 0), (0, 0)))
    return pl.pallas_call(
        _pool_kernel,
        grid=(n_batch, nt),
        in_specs=[pl.BlockSpec((tb, POOL_WIDTH), lambda b, i: (b * nt + i, COL_POOL // POOL_WIDTH)),
                  pl.BlockSpec((1, POOL_MAX, POOL_WIDTH), lambda b, i: (b, 0, 0)),
                  pl.BlockSpec((tb, 128), lambda b, i: (i, 0)),
                  pl.BlockSpec((POOL_GROUPS, POOL_GROUP_DIM, POOL_GROUP_DIM), lambda b, i: (0, 0, 0)),
                  pl.BlockSpec((1, POOL_WIDTH), lambda b, i: (0, 0))],
        out_specs=pl.BlockSpec((tb, POOL_WIDTH), lambda b, i: (b * nt + i, 0)),
        out_shape=jax.ShapeDtypeStruct((n_batch * t_len, POOL_WIDTH), BF16),
        scratch_shapes=[pltpu.VMEM((POOL_MAX, POOL_WIDTH), F32)],
        compiler_params=_params("arbitrary", "arbitrary"),
        name="pool_mix",
    )(p, prev16, cnt, w_pool, scale.reshape(1, -1))


def _layer_norm_rows(x, g, b):
    xc = x - jnp.mean(x, axis=-1, keepdims=True)
    var = jnp.mean(xc * xc, axis=-1, keepdims=True)
    return xc * lax.rsqrt(var + LN_EPS) * g + b


def _outproj_kernel(x_ref, yg_ref, yn_ref, yp_ref, w_ref, g_ref, b_ref, *rest):
    h = jnp.dot(yg_ref[...], w_ref[0:GLA_WIDTH, :], preferred_element_type=F32)
    h += jnp.dot(yn_ref[...], w_ref[GLA_WIDTH:GLA_WIDTH + NSA_WIDTH, :], preferred_element_type=F32)
    h += jnp.dot(yp_ref[...], w_ref[GLA_WIDTH + NSA_WIDTH:, :], preferred_element_type=F32)
    x1 = _layer_norm_rows(ALPHA * x_ref[...] + h, g_ref[...], b_ref[...])
    if len(rest) == 1:
        rest[0][...] = x1
    else:
        rh_ref, rl_ref, o_ref, lg_ref = rest
        o_ref[...] = x1
        xh = x1.astype(BF16)
        xl = (x1 - xh.astype(F32)).astype(BF16)
        lg_ref[...] = (jnp.dot(xh, rh_ref[...], preferred_element_type=F32)
                       + jnp.dot(xl, rh_ref[...], preferred_element_type=F32)
                       + jnp.dot(xh, rl_ref[...], preferred_element_type=F32))


def router_split(router):
    r = jnp.pad(router, ((0, 0), (0, 128 - router.shape[1])))
    hi = r.astype(BF16)
    return hi, (r - hi.astype(F32)).astype(BF16)


def outproj_ln(x, y_gla, y_nsa, y_pool, w_out_bf16, g, b, router=None):
    n, d = x.shape
    tm = _pick_tile(n, (512, 256, 128, 64, 32, 16, 8))
    row = lambda w: pl.BlockSpec((tm, w), lambda i: (i, 0))
    const = lambda r, c: pl.BlockSpec((r, c), lambda i: (0, 0))
    in_specs = [row(d), row(GLA_WIDTH), row(NSA_WIDTH), row(POOL_WIDTH), const(d, d), const(1, d), const(1, d)]
    args = [x, y_gla, y_nsa, y_pool, w_out_bf16, g.reshape(1, -1), b.reshape(1, -1)]
    out_specs, out_shape = row(d), jax.ShapeDtypeStruct((n, d), F32)
    if router is not None:
        in_specs += [const(d, 128), const(d, 128)]
        args += list(router_split(router))
        out_specs, out_shape = (out_specs, row(128)), (out_shape, jax.ShapeDtypeStruct((n, 128), F32))
    return pl.pallas_call(
        _outproj_kernel,
        grid=(n // tm,),
        in_specs=in_specs,
        out_specs=out_specs,
        out_shape=out_shape,
        compiler_params=_params("arbitrary"),
        name="outproj_ln",
    )(*args)


MOE_TM = 512
MOE_TM_DOWN = 512
MOE_VMEM_BYTES = 60 * 1024 * 1024
MOE_TF = 1024
MOE_TN = 512
ROUTE_TM = 512
PERMUTE_CHUNK = 1024


def _route_kernel(lg_ref, ii_ref, gf_ref, cnt_ref, carry, *, n_valid):
    tm = lg_ref.shape[0]
    i = pl.program_id(0)

    @pl.when(i == 0)
    def _():
        carry[...] = jnp.zeros_like(carry)

    lane = lax.broadcasted_iota(jnp.int32, (tm, 128), 1)
    valid = (i * tm + lax.broadcasted_iota(jnp.int32, (tm, 128), 0)) < n_valid
    lg = jnp.where(lane < N_EXPERTS, lg_ref[...], -jnp.inf)
    m1 = jnp.max(lg, axis=-1, keepdims=True)
    i1 = jnp.min(jnp.where(lg == m1, lane, 128), axis=-1, keepdims=True)
    lg2 = jnp.where(lane == i1, -jnp.inf, lg)
    m2 = jnp.max(lg2, axis=-1, keepdims=True)
    i2 = jnp.min(jnp.where(lg2 == m2, lane, 128), axis=-1, keepdims=True)
    t = jnp.exp(m2 - m1)
    g1 = 1.0 / (1.0 + t)
    g2 = t / (1.0 + t)
    oh1 = jnp.where((lane == i1) & valid, 1.0, 0.0)
    oh2 = jnp.where((lane == i2) & valid, 1.0, 0.0)
    cnt = oh1 + oh2
    strict = (lax.broadcasted_iota(jnp.int32, (tm, tm), 0) > lax.broadcasted_iota(jnp.int32, (tm, tm), 1))
    before = jnp.dot(strict.astype(BF16), cnt.astype(BF16), preferred_element_type=F32) + carry[...]
    r1 = jnp.sum(before * oh1, axis=-1, keepdims=True).astype(jnp.int32)
    r2 = jnp.sum(before * oh2, axis=-1, keepdims=True).astype(jnp.int32)
    carry[...] += jnp.sum(cnt, axis=0, keepdims=True)
    ii_ref[...] = jnp.where(lane == 0, i1, jnp.where(lane == 1, i2, jnp.where(lane == 2, r1, r2)))
    gf_ref[...] = jnp.where(lane == 0, g1, g2)
    cnt_ref[...] = carry[...]


def moe_route(logits, n_valid):
    npad = logits.shape[0]
    tm = ROUTE_TM
    row = pl.BlockSpec((tm, 128), lambda i: (i, 0))
    info, gates, counts = pl.pallas_call(
        functools.partial(_route_kernel, n_valid=n_valid),
        grid=(npad // tm,),
        in_specs=[row],
        out_specs=(row, row, pl.BlockSpec((1, 128), lambda i: (0, 0))),
        out_shape=(jax.ShapeDtypeStruct((npad, 128), jnp.int32), jax.ShapeDtypeStruct((npad, 128), F32),
                   jax.ShapeDtypeStruct((1, 128), F32)),
        scratch_shapes=[pltpu.VMEM((1, 128), F32)],
        compiler_params=_params("arbitrary"),
        name="moe_route",
    )(logits)
    return info[:, 0:2], info[:, 2:4], gates, counts[0, :N_EXPERTS].astype(jnp.int32)


SLAB = (16, 128)


def _slabify_kernel(x_ref, *rest):
    o_ref = rest[-1]

    def slab_rows(src_ref, n_rows):
        for c in range(SLAB[0]):
            o_ref[0:n_rows, c, :] = src_ref[:, c * SLAB[1]:(c + 1) * SLAB[1]]

    if len(rest) == 1:
        slab_rows(x_ref, x_ref.shape[0])
        return
    t_ref = rest[0]
    last = pl.program_id(0) == pl.num_programs(0) - 1

    @pl.when(jnp.logical_not(last))
    def _():
        slab_rows(x_ref, x_ref.shape[0])

    @pl.when(last)
    def _():
        o_ref[...] = jnp.zeros_like(o_ref)
        slab_rows(t_ref, t_ref.shape[0])


def slabify(x, tail=None):
    n, d = x.shape
    tm = _pick_tile(n, (512, 256, 128, 64, 32, 16, 8))
    nt = n // tm
    in_specs = [pl.BlockSpec((tm, d), lambda i: (jnp.minimum(i, nt - 1), 0))]
    args = [x]
    if tail is not None:
        assert tail.shape[0] <= tm
        in_specs.append(pl.BlockSpec(tail.shape, lambda i: (0, 0)))
        args.append(tail)
    steps = nt + (tail is not None)
    return pl.pallas_call(
        _slabify_kernel,
        grid=(steps,),
        in_specs=in_specs,
        out_specs=pl.BlockSpec((tm,) + SLAB, lambda i: (i, 0, 0)),
        out_shape=jax.ShapeDtypeStruct((steps * tm,) + SLAB, x.dtype),
        compiler_params=_params("arbitrary"),
        name="slabify",
    )(*args)


GATHER_UNROLL = 8


def _gather_rows_kernel(idx_ref, src_ref, o_ref, stage, sem):
    ch = idx_ref.shape[2]

    def row_copy(r):
        dst = stage.at[lax.shift_right_logical(r, 3), :, r & 7, :]
        return pltpu.make_async_copy(src_ref.at[idx_ref[0, 0, r]], dst, sem)

    def issue(t, c):
        for u in range(GATHER_UNROLL):
            row_copy(t * GATHER_UNROLL + u).start(priority=u % 2)
        return c

    lax.fori_loop(0, ch // GATHER_UNROLL, issue, 0)

    def drain(t, c):
        for u in range(GATHER_UNROLL):
            row_copy(t * GATHER_UNROLL + u).wait()
        return c

    lax.fori_loop(0, ch // GATHER_UNROLL, drain, 0)
    rows = jnp.concatenate([stage[:, c].reshape(ch, SLAB[1]) for c in range(SLAB[0])], axis=1)
    o_ref[...] = rows.astype(o_ref.dtype)


def gather_rows_bf16(src, idx):
    n = idx.shape[0]
    ch = PERMUTE_CHUNK
    d = SLAB[0] * SLAB[1]
    return pl.pallas_call(
        _gather_rows_kernel,
        grid=(n // ch,),
        in_specs=[pl.BlockSpec((1, 1, ch), lambda i: (i, 0, 0), memory_space=pltpu.SMEM),
                  pl.BlockSpec(memory_space=pl.ANY)],
        out_specs=pl.BlockSpec((ch, d), lambda i: (i, 0)),
        out_shape=jax.ShapeDtypeStruct((n, d), BF16),
        scratch_shapes=[pltpu.VMEM((ch // 8, SLAB[0], 8, SLAB[1]), src.dtype), pltpu.SemaphoreType.DMA(())],
        compiler_params=_params("arbitrary"),
        name="gather_rows",
    )(idx.reshape(n // ch, 1, ch), src)


def _moe_up_kernel(te_ref, tfirst_ref, tused_ref, x_ref, wg_ref, wu_ref, h_ref, wgb, wub):
    i = pl.program_id(1)

    @pl.when((i == 0) | (tfirst_ref[i] == 1))
    def _():
        wgb[...] = wg_ref[...].astype(BF16)
        wub[...] = wu_ref[...].astype(BF16)

    @pl.when(tused_ref[i] == 1)
    def _():
        xb = x_ref[...]
        gate = jnp.dot(xb, wgb[...], preferred_element_type=F32)
        up = jnp.dot(xb, wub[...], preferred_element_type=F32)
        h_ref[...] = (jax.nn.silu(gate) * up).astype(BF16)

    @pl.when(tused_ref[i] == 0)
    def _():
        h_ref[...] = jnp.zeros_like(h_ref)


def _moe_down_kernel(te_ref, tfirst_ref, tused_ref, h_ref, wd_ref, y_ref, wdb, ybuf, sems):
    j, i = pl.program_id(0), pl.program_id(1)
    n_i = pl.num_programs(1)
    n_c, tm = ybuf.shape[1], ybuf.shape[2]
    step = j * n_i + i
    last = pl.num_programs(0) * n_i - 1

    def copies(s, slot):
        jj, ii = s // n_i, s % n_i
        return [pltpu.make_async_copy(ybuf.at[slot, c], y_ref.at[pl.ds(ii * tm, tm), jj * n_c + c, :], sems.at[slot])
                for c in range(n_c)]

    def for_slot(s, fn):
        for slot in range(2):
            @pl.when(s % 2 == slot)
            def _():
                fn(slot)

    @pl.when((i == 0) | (tfirst_ref[i] == 1))
    def _():
        wdb[...] = wd_ref[...].astype(BF16)

    @pl.when(step >= 2)
    def _():
        for_slot(step, lambda slot: [cp.wait() for cp in copies(step - 2, slot)])

    def fill(slot):
        @pl.when(tused_ref[i] == 1)
        def _():
            y = jnp.dot(h_ref[...], wdb[...], preferred_element_type=F32)
            for c in range(n_c):
                ybuf[slot, c] = y[:, c * SLAB[1]:(c + 1) * SLAB[1]]

        @pl.when(tused_ref[i] == 0)
        def _():
            ybuf[slot] = jnp.zeros(ybuf.shape[1:], F32)

        for cp in copies(step, slot):
            cp.start()

    for_slot(step, fill)

    @pl.when(step == last)
    def _():
        for_slot(step - 1, lambda slot: [cp.wait() for cp in copies(step - 1, slot)])
        for_slot(step, lambda slot: [cp.wait() for cp in copies(step, slot)])


def _tile_meta(tile_e, tile_used, split):
    te = jnp.repeat(tile_e, split)
    first = jnp.concatenate([jnp.ones((1,), jnp.int32), (te[1:] != te[:-1]).astype(jnp.int32)])
    return te, first, jnp.repeat(tile_used, split)


def moe_experts(xs, tile_e, tile_used, wg, wu, wd):
    r, d = xs.shape
    d_ff = wg.shape[2]
    tm, tf, tn = MOE_TM, MOE_TF, MOE_TN
    h = pl.pallas_call(
        _moe_up_kernel,
        grid_spec=pltpu.PrefetchScalarGridSpec(
            num_scalar_prefetch=3,
            grid=(d_ff // tf, r // tm),
            in_specs=[pl.BlockSpec((tm, d), lambda j, i, te, t1, tu: (i, 0)),
                      pl.BlockSpec((None, d, tf), lambda j, i, te, t1, tu: (te[i], 0, j)),
                      pl.BlockSpec((None, d, tf), lambda j, i, te, t1, tu: (te[i], 0, j))],
            out_specs=pl.BlockSpec((tm, tf), lambda j, i, te, t1, tu: (i, j)),
            scratch_shapes=[pltpu.VMEM((d, tf), BF16), pltpu.VMEM((d, tf), BF16)]),
        out_shape=jax.ShapeDtypeStruct((r, d_ff), BF16),
        compiler_params=_params("arbitrary", "arbitrary", vmem_limit_bytes=MOE_VMEM_BYTES),
        name="moe_up",
    )(*_tile_meta(tile_e, tile_used, 1), xs, wg, wu)
    tmd = MOE_TM_DOWN
    return pl.pallas_call(
        _moe_down_kernel,
        grid_spec=pltpu.PrefetchScalarGridSpec(
            num_scalar_prefetch=3,
            grid=(d // tn, r // tmd),
            in_specs=[pl.BlockSpec((tmd, d_ff), lambda j, i, te, t1, tu: (i, 0)),
                      pl.BlockSpec((None, d_ff, tn), lambda j, i, te, t1, tu: (te[i], 0, j))],
            out_specs=pl.BlockSpec(memory_space=pl.ANY),
            scratch_shapes=[pltpu.VMEM((d_ff, tn), BF16), pltpu.VMEM((2, tn // SLAB[1], tmd, SLAB[1]), F32),
                            pltpu.SemaphoreType.DMA((2,))]),
        out_shape=jax.ShapeDtypeStruct((r,) + SLAB, F32),
        compiler_params=_params("arbitrary", "arbitrary", vmem_limit_bytes=MOE_VMEM_BYTES),
        name="moe_down",
    )(*_tile_meta(tile_e, tile_used, tm // tmd), h, wd)


def _moe_combine_kernel(idx_ref, x_ref, gt_ref, g_ref, b_ref, y_ref, o_ref, stage, sem):
    tm = x_ref.shape[0]

    def row_copy(n):
        k, r = lax.shift_right_logical(n, tm.bit_length() - 1), n & (tm - 1)
        return pltpu.make_async_copy(y_ref.at[idx_ref[0, 0, n]],
                                     stage.at[k, lax.shift_right_logical(r, 3), :, r & 7, :], sem)

    def issue(t, c):
        for u in range(GATHER_UNROLL):
            row_copy(t * GATHER_UNROLL + u).start(priority=u % 2)
        return c

    lax.fori_loop(0, TOP_K * tm // GATHER_UNROLL, issue, 0)

    def drain(t, c):
        for u in range(GATHER_UNROLL):
            row_copy(t * GATHER_UNROLL + u).wait()
        return c

    lax.fori_loop(0, TOP_K * tm // GATHER_UNROLL, drain, 0)
    rows = lambda k: jnp.concatenate([stage[k, :, c].reshape(tm, SLAB[1]) for c in range(SLAB[0])], axis=1)
    gt = gt_ref[...]
    y = gt[:, 0:1] * rows(0) + gt[:, 1:2] * rows(1)
    o_ref[...] = _layer_norm_rows(ALPHA * x_ref[...] + y, g_ref[...], b_ref[...])


def moe_combine_ln(x, ys, dest, gates, row0, g, b):
    n, d = x.shape
    tm = _pick_tile(n, (512, 256, 128, 64, 32, 16, 8))
    o0 = row0 // tm
    idx = dest[row0:row0 + n].reshape(n // tm, tm, TOP_K).transpose(0, 2, 1).reshape(n // tm, 1, TOP_K * tm)
    return pl.pallas_call(
        _moe_combine_kernel,
        grid=(n // tm,),
        in_specs=[pl.BlockSpec((1, 1, TOP_K * tm), lambda i: (i, 0, 0), memory_space=pltpu.SMEM),
                  pl.BlockSpec((tm, d), lambda i: (i, 0)),
                  pl.BlockSpec((tm, 128), lambda i: (o0 + i, 0)),
                  pl.BlockSpec((1, d), lambda i: (0, 0)), pl.BlockSpec((1, d), lambda i: (0, 0)),
                  pl.BlockSpec(memory_space=pl.ANY)],
        out_specs=pl.BlockSpec((tm, d), lambda i: (i, 0)),
        out_shape=jax.ShapeDtypeStruct((n, d), F32),
        scratch_shapes=[pltpu.VMEM((TOP_K, tm // 8, SLAB[0], 8, SLAB[1]), F32), pltpu.SemaphoreType.DMA(())],
        compiler_params=_params("arbitrary"),
        name="moe_combine_ln",
    )(idx.astype(jnp.int32), x, gates, g.reshape(1, -1), b.reshape(1, -1), ys)


def moe_ln(x_groups, logit_groups, wg, wu, wd, g, b):
    d = x_groups[0].shape[1]
    n_tok = sum(x.shape[0] for x in x_groups)
    n_tok_pad = -(-n_tok // PERMUTE_CHUNK) * PERMUTE_CHUNK
    n_tok_pad = -(-n_tok_pad // ROUTE_TM) * ROUTE_TM
    logits = jnp.concatenate(logit_groups + [jnp.zeros((n_tok_pad - n_tok, 128), F32)], axis=0)
    experts, ranks, gates, counts = moe_route(logits, n_tok)
    tm = MOE_TM
    n_tiles = -(-(n_tok * TOP_K + N_EXPERTS * (tm - 1)) // tm)
    n_tiles = -(-n_tiles * tm // PERMUTE_CHUNK) * PERMUTE_CHUNK // tm
    padded = (counts + tm - 1) // tm * tm
    pad_end = jnp.cumsum(padded)
    pad_start = pad_end - padded
    valid = (jnp.arange(n_tok_pad) < n_tok)[:, None]
    dest = jnp.where(valid, pad_start[experts] + ranks, 0)
    tok = jnp.broadcast_to(jnp.arange(n_tok_pad, dtype=jnp.int32)[:, None], dest.shape)
    row_tok = jnp.zeros((n_tiles * tm,), jnp.int32).at[jnp.where(valid, dest, n_tiles * tm).reshape(-1)].set(
        tok.reshape(-1), mode='drop')
    tile_start = jnp.arange(n_tiles, dtype=jnp.int32) * tm
    tile_e = jnp.minimum(jnp.sum(tile_start[:, None] >= pad_end[None, :], axis=1), N_EXPERTS - 1).astype(jnp.int32)
    tile_used = (tile_start < pad_end[-1]).astype(jnp.int32)
    assert len(x_groups) == 2
    xs = gather_rows_bf16(slabify(x_groups[0], x_groups[1]), row_tok)
    ys = moe_experts(xs, tile_e, tile_used, wg, wu, wd)
    outs, row0 = [], 0
    for x in x_groups:
        outs.append(moe_combine_ln(x, ys, dest, gates, row0, g, b))
        row0 += x.shape[0]
    return outs


def _ffn_kernel(x_ref, wg_ref, wu_ref, wd_ref, g_ref, b_ref, o_ref, xb_ref, *, d_ff):
    j = pl.program_id(1)
    tf = wg_ref.shape[1]

    @pl.when(j == 0)
    def _():
        xb_ref[...] = x_ref[...].astype(BF16)
        o_ref[...] = jnp.zeros_like(o_ref)

    xb = xb_ref[...]
    gate = jnp.dot(xb, wg_ref[...], preferred_element_type=F32)
    up = jnp.dot(xb, wu_ref[...], preferred_element_type=F32)
    col = j * tf + lax.broadcasted_iota(jnp.int32, (1, tf), 1)
    a = jnp.where(col < d_ff, jax.nn.silu(gate) * up, 0.0).astype(BF16)
    rowi = j * tf + lax.broadcasted_iota(jnp.int32, (tf, 1), 0)
    wd = jnp.where(rowi < d_ff, wd_ref[...], jnp.zeros((), BF16))
    o_ref[...] += jnp.dot(a, wd, preferred_element_type=F32)

    @pl.when(j == pl.num_programs(1) - 1)
    def _():
        o_ref[...] = _layer_norm_rows(ALPHA * x_ref[...] + o_ref[...], g_ref[...], b_ref[...])


def ffn_ln(x, wg, wu, wd, g, b):
    n, d = x.shape
    d_ff = wg.shape[1]
    tm = _pick_tile(n, (512, 256, 128, 64, 32, 16, 8))
    tf = 512
    kern = functools.partial(_ffn_kernel, d_ff=d_ff)
    return pl.pallas_call(
        kern,
        grid=(n // tm, pl.cdiv(d_ff, tf)),
        in_specs=[pl.BlockSpec((tm, d), lambda i, j: (i, 0)),
                  pl.BlockSpec((d, tf), lambda i, j: (0, j)), pl.BlockSpec((d, tf), lambda i, j: (0, j)),
                  pl.BlockSpec((tf, d), lambda i, j: (j, 0)),
                  pl.BlockSpec((1, d), lambda i, j: (0, 0)), pl.BlockSpec((1, d), lambda i, j: (0, 0))],
        out_specs=pl.BlockSpec((tm, d), lambda i, j: (i, 0)),
        out_shape=jax.ShapeDtypeStruct((n, d), F32),
        scratch_shapes=[pltpu.VMEM((tm, d), BF16)],
        compiler_params=_params("arbitrary", "arbitrary"),
        name="ffn_ln",
    )(x, wg, wu, wd, g.reshape(1, -1), b.reshape(1, -1))


CMP_PAGES = 32
SEL_PAD_DEC = 384


def _cmp_paged_kernel(pt_ref, cache_ref, pe_ref, w1_ref, w2_ref, o_ref, buf, hlo, hhi, sems, *, layer, n_pages, page):
    b, ch = pl.program_id(0), pl.program_id(1)
    n_ch = pl.num_programs(1)
    hd = HEAD_DIM
    nh = CMP_PAGES * page // CMP_STRIDE
    step = b * n_ch + ch

    def copies(s, slot):
        first = s * CMP_PAGES
        return [pltpu.make_async_copy(
            cache_ref.at[layer, pt_ref[first + pg], :, c // NSA_KV_HEADS, c % NSA_KV_HEADS, :],
            buf.at[slot, c, pl.ds(pg * page, page), :], sems.at[slot])
            for pg in range(CMP_PAGES) for c in range(2 * NSA_KV_HEADS)]

    def for_slot(s, fn):
        for slot in range(2):
            @pl.when(s % 2 == slot)
            def _():
                fn(slot)

    @pl.when(step == 0)
    def _():
        for cp in copies(0, 0):
            cp.start()

    @pl.when(step + 1 < pl.num_programs(0) * n_ch)
    def _():
        for_slot(step + 1, lambda slot: [cp.start() for cp in copies(step + 1, slot)])

    for_slot(step, lambda slot: [cp.wait() for cp in copies(step, slot)])
    cur = step % 2

    for c in range(2 * NSA_KV_HEADS):
        kd, g = c // NSA_KV_HEADS, c % NSA_KV_HEADS
        xs = [buf[cur, c, pl.ds(j, nh, stride=CMP_STRIDE), :] for j in range(CMP_STRIDE)]
        lo = jnp.concatenate([(xs[j] + pe_ref[kd, j:j + 1, :]).astype(BF16) for j in range(CMP_STRIDE)], axis=1)
        hi = jnp.concatenate([(xs[j] + pe_ref[kd, CMP_STRIDE + j:CMP_STRIDE + j + 1, :]).astype(BF16)
                              for j in range(CMP_STRIDE)], axis=1)
        w_lo = w1_ref[kd, 0:CMP_STRIDE].reshape(CMP_STRIDE * hd, CMP_HIDDEN).astype(BF16)
        w_hi = w1_ref[kd, CMP_STRIDE:CMP_LEN].reshape(CMP_STRIDE * hd, CMP_HIDDEN).astype(BF16)
        rows = pl.ds(pl.multiple_of(ch * nh, nh), nh)
        hlo[c, rows, :] = jnp.dot(lo, w_lo, preferred_element_type=F32)
        hhi[c, rows, :] = jnp.dot(hi, w_hi, preferred_element_type=F32)

    @pl.when(ch == pl.num_programs(1) - 1)
    def _():
        n_all = hlo.shape[1]
        for c in range(2 * NSA_KV_HEADS):
            kd, g = c // NSA_KV_HEADS, c % NSA_KV_HEADS
            h = jax.nn.gelu(hlo[c] + pltpu.roll(hhi[c], n_all - 1, 0))
            o_ref[kd, g] = jnp.dot(h.astype(BF16), w2_ref[kd].astype(BF16), preferred_element_type=F32).astype(BF16)


def nsa_compress_paged(cache, layer, page_table, cmp_pos, cmp_w1, cmp_w2):
    page = cache.shape[2]
    n_batch, n_pages = page_table.shape
    hd = HEAD_DIM
    n_all = n_pages * page // CMP_STRIDE
    kern = functools.partial(_cmp_paged_kernel, layer=layer, n_pages=n_pages, page=page)
    const = lambda shape: pl.BlockSpec(shape, lambda b, c, pt: (0,) * len(shape))
    return pl.pallas_call(
        kern,
        grid_spec=pltpu.PrefetchScalarGridSpec(
            num_scalar_prefetch=1,
            grid=(n_batch, n_pages // CMP_PAGES),
            in_specs=[pl.BlockSpec(memory_space=pl.ANY), const((2, CMP_LEN, hd)),
                      const((2, CMP_LEN, hd, CMP_HIDDEN)), const((2, CMP_HIDDEN, hd))],
            out_specs=pl.BlockSpec((None, 2, NSA_KV_HEADS, n_all, hd), lambda b, c, pt: (b, 0, 0, 0, 0)),
            scratch_shapes=[pltpu.VMEM((2, 2 * NSA_KV_HEADS, CMP_PAGES * page, hd), F32),
                            pltpu.VMEM((2 * NSA_KV_HEADS, n_all, CMP_HIDDEN), F32),
                            pltpu.VMEM((2 * NSA_KV_HEADS, n_all, CMP_HIDDEN), F32),
                            pltpu.SemaphoreType.DMA((2,))]),
        out_shape=jax.ShapeDtypeStruct((n_batch, 2, NSA_KV_HEADS, n_all, hd), BF16),
        compiler_params=_params("arbitrary", "arbitrary"),
        name="nsa_compress_paged",
    )(page_table.reshape(-1).astype(jnp.int32), cache, cmp_pos, cmp_w1, cmp_w2)


def _sel_decode_kernel(qn_ref, kc_ref, vc_ref, ovt_ref, oc_ref, idx_ref, v_scr, psum_scr, *, n_cmp, n_sel, n_top,
                       q_pos):
    b = pl.program_id(0)
    n_cmp_pad = kc_ref.shape[2]
    nsp = ovt_ref.shape[0]
    n_rows = qn_ref.shape[0]
    col = lax.broadcasted_iota(jnp.int32, (n_rows, n_cmp_pad), 1)
    mask = (col * CMP_STRIDE + (CMP_LEN - 1) <= q_pos) & (col < n_cmp)
    psums = []
    for g in range(NSA_KV_HEADS):
        own = slice(g * NSA_REP, (g + 1) * NSA_REP)
        s = lax.dot_general(qn_ref[...], kc_ref[0, g], (((1,), (1,)), ((), ())), preferred_element_type=F32)
        s = jnp.where(mask, s, -jnp.inf)
        m = jnp.max(s, axis=-1, keepdims=True)
        m = jnp.where(m > -jnp.inf, m, 0.0)
        p = jnp.where(mask, jnp.exp(s - m), 0.0)
        p = p / jnp.maximum(jnp.sum(p, axis=-1, keepdims=True), 1e-30)
        oc = jnp.dot(p.astype(BF16), vc_ref[0, g], preferred_element_type=F32)
        oc_ref[own, :] = oc[own]
        psum_scr[pl.ds(b * NSA_KV_HEADS + g, 1), :] = jnp.sum(p[own], axis=0, keepdims=True)

    @pl.when(b == pl.num_programs(0) - 1)
    def _():
        n_col = psum_scr.shape[0]
        imp = lax.dot_general(ovt_ref[...], psum_scr[...], (((1,), (1,)), ((), ())), preferred_element_type=F32,
                              precision=lax.Precision.HIGHEST)
        blk = lax.broadcasted_iota(jnp.int32, (nsp, n_col), 0)
        cur = q_pos // SEL_BLOCK
        forced = (blk == 0) | (blk == cur) | (blk == cur - 1)
        v = jnp.where((blk <= cur) & (blk < n_sel), jnp.where(forced, jnp.inf, imp), -jnp.inf)
        v_scr[...] = v

        def count(i, rank):
            vi = v_scr[pl.ds(i, 1), :]
            ahead = (vi > v) | ((vi == v) & (blk > i))
            return rank + ahead.astype(jnp.int32)

        rank = lax.fori_loop(0, n_sel, count, jnp.zeros((nsp, n_col), jnp.int32))
        chosen = (rank < n_top) & (v > -jnp.inf)
        blk_f = blk.astype(F32)
        rows = [jnp.sum(jnp.where(chosen & (rank == t), blk_f, 0.0), axis=0, keepdims=True) for t in range(n_top)]
        idx_ref[...] = jnp.concatenate(rows, axis=0).astype(jnp.int32)


def nsa_select_decode(qn, cmp_kv, n_cmp, n_sel, q_pos):
    n_batch, n_heads, hd = qn.shape
    n_cmp_pad = cmp_kv.shape[3]
    nsp = SEL_PAD_DEC
    n_top = min(SEL_TOPN, n_sel)
    ci = np.arange(n_cmp_pad)[None, :]
    sj = np.arange(nsp)[:, None]
    overlap_t = ((ci * CMP_STRIDE <= sj * SEL_BLOCK + SEL_BLOCK - 1) &
                 (ci * CMP_STRIDE + CMP_LEN - 1 >= sj * SEL_BLOCK) & (ci < n_cmp) & (sj < n_sel)).astype(np.float32)
    kern = functools.partial(_sel_decode_kernel, n_cmp=n_cmp, n_sel=n_sel, n_top=n_top, q_pos=q_pos)
    kv = lambda kd: pl.BlockSpec((None, 1, NSA_KV_HEADS, n_cmp_pad, hd), lambda b: (b, kd, 0, 0, 0))
    n_col = n_batch * NSA_KV_HEADS
    o_cmp, idx = pl.pallas_call(
        kern,
        grid=(n_batch,),
        in_specs=[pl.BlockSpec((None, 2 * n_heads, hd), lambda b: (b, 0, 0)), kv(0), kv(1),
                  pl.BlockSpec((nsp, n_cmp_pad), lambda b: (0, 0))],
        out_specs=(pl.BlockSpec((None, n_heads, hd), lambda b: (b, 0, 0)),
                   pl.BlockSpec((n_top, n_col), lambda b: (0, 0))),
        out_shape=(jax.ShapeDtypeStruct((n_batch, n_heads, hd), F32),
                   jax.ShapeDtypeStruct((n_top, n_col), jnp.int32)),
        scratch_shapes=[pltpu.VMEM((nsp, n_col), F32), pltpu.VMEM((n_col, n_cmp_pad), F32)],
        compiler_params=_params("arbitrary"),
        name="nsa_select_decode",
    )(jnp.pad(qn, ((0, 0), (0, n_heads), (0, 0))), cmp_kv, cmp_kv, overlap_t)
    return o_cmp, idx.T.reshape(n_batch, NSA_KV_HEADS, n_top)


def _attn_decode_kernel(pt_ref, sel_ref, q_ref, cache_ref, new_ref, kw_ref, vw_ref, wnew_ref, oc_ref, gate_ref,
                        o_ref, kbuf, vbuf, sem, *, layer, n_pages, per_page, n_top):
    b = pl.program_id(0)
    G = NSA_KV_HEADS
    n_past_blocks = n_pages * per_page
    nt = (((1,), (1,)), ((), ()))

    def block_id(g, slot):
        return sel_ref[(b * G + g) * n_top + slot]

    def copies():
        out = []
        for g in range(G):
            for slot in range(n_top):
                j = jnp.minimum(block_id(g, slot), n_past_blocks - 1)
                rows = pl.ds((j % per_page) * SEL_BLOCK, SEL_BLOCK)
                page = pt_ref[b * n_pages + j // per_page]
                for kind, buf in ((2, kbuf), (3, vbuf)):
                    out.append(pltpu.make_async_copy(cache_ref.at[layer, page, rows, kind, g, :],
                                                     buf.at[g, pl.ds(slot * SEL_BLOCK, SEL_BLOCK), :], sem))
        return out

    for cp in copies():
        cp.start()
    for cp in copies():
        cp.wait()

    for g in range(G):
        qb = q_ref[g]
        q = qb.astype(F32)
        n_rows = qb.shape[0]
        s = lax.dot_general(qb, kbuf[g].astype(BF16), nt, preferred_element_type=F32)
        slot_of = lax.broadcasted_iota(jnp.int32, s.shape, 1) // SEL_BLOCK
        for slot in range(n_top):
            s = jnp.where((slot_of == slot) & (block_id(g, slot) >= n_past_blocks), NEG_BIG, s)
        k_new = new_ref[2 * G + g:2 * G + g + 1, :]
        v_new = new_ref[3 * G + g:3 * G + g + 1, :]
        s_new = jnp.sum(q * k_new, axis=-1, keepdims=True)
        m = jnp.maximum(jnp.max(s, axis=-1, keepdims=True), s_new)
        p = jnp.exp(s - m)
        p_new = jnp.exp(s_new - m)
        l = jnp.sum(p, axis=-1, keepdims=True) + p_new
        o_sel = (jnp.dot(p.astype(BF16), vbuf[g].astype(BF16), preferred_element_type=F32) + p_new * v_new) / l

        n_buf = kw_ref.shape[0]
        s = lax.dot_general(qb, kw_ref[:, g, :].astype(BF16), nt, preferred_element_type=F32)
        keep = lax.broadcasted_iota(jnp.int32, s.shape, 1) > n_buf - WINDOW
        s = jnp.where(keep, s, NEG_BIG)
        kw_new = wnew_ref[g:g + 1, :]
        vw_new = wnew_ref[G + g:G + g + 1, :]
        s_new = jnp.sum(q * kw_new, axis=-1, keepdims=True)
        m = jnp.maximum(jnp.max(s, axis=-1, keepdims=True), s_new)
        p = jnp.exp(s - m)
        p_new = jnp.exp(s_new - m)
        l = jnp.sum(p, axis=-1, keepdims=True) + p_new
        o_win = (jnp.dot(p.astype(BF16), vw_ref[:, g, :].astype(BF16), preferred_element_type=F32)
                 + p_new * vw_new) / l
        gates = jnp.broadcast_to(gate_ref[g:g + 1, :], (n_rows, 128))
        lane = lax.broadcasted_iota(jnp.int32, (n_rows, 128), 1)
        head = lax.broadcasted_iota(jnp.int32, (n_rows, 128), 0)
        pick = lambda c: jnp.sum(jnp.where(lane == head * 3 + c, gates, 0.0), axis=-1, keepdims=True)
        o_ref[g] = (pick(0) * oc_ref[g] + pick(1) * o_sel + pick(2) * o_win).astype(o_ref.dtype)


def nsa_attend_decode(qr, o_cmp, sel_idx, gates, cache, layer, page_table, new_rows, state_win, new_win):
    n_batch, n_heads, hd = qr.shape
    page = cache.shape[2]
    n_pages = page_table.shape[1]
    n_top = sel_idx.shape[2]
    per_page = page // SEL_BLOCK
    n_buf = state_win.shape[2]
    G, R = NSA_KV_HEADS, NSA_REP
    rp = 16
    pad_heads = lambda a: jnp.pad(a.reshape(n_batch, G, R, hd), ((0, 0), (0, 0), (0, rp - R), (0, 0)))
    per_bg = lambda: pl.BlockSpec((None, G, rp, hd), lambda b, pt, sel: (b, 0, 0, 0))
    per_b = lambda rows: pl.BlockSpec((None, rows, hd), lambda b, pt, sel: (b, 0, 0))
    win_spec = lambda kv: pl.BlockSpec((None, None, n_buf, None, G, hd), lambda b, pt, sel: (layer, b, 0, kv, 0, 0))
    kern = functools.partial(_attn_decode_kernel, layer=layer, n_pages=n_pages, per_page=per_page, n_top=n_top)
    out = pl.pallas_call(
        kern,
        grid_spec=pltpu.PrefetchScalarGridSpec(
            num_scalar_prefetch=2,
            grid=(n_batch,),
            in_specs=[per_bg(), pl.BlockSpec(memory_space=pl.ANY), per_b(4 * G), win_spec(0), win_spec(1),
                      per_b(2 * G), per_bg(), per_b(G)],
            out_specs=per_bg(),
            scratch_shapes=[pltpu.VMEM((G, n_top * SEL_BLOCK, hd), F32), pltpu.VMEM((G, n_top * SEL_BLOCK, hd), F32),
                            pltpu.SemaphoreType.DMA(())]),
        out_shape=jax.ShapeDtypeStruct((n_batch, G, rp, hd), BF16),
        compiler_params=_params("arbitrary"),
        name="nsa_attend_decode",
    )(page_table.reshape(-1).astype(jnp.int32), sel_idx.reshape(-1).astype(jnp.int32),
      pad_heads(qr), cache, new_rows, state_win, state_win, new_win, pad_heads(o_cmp), gates)
    return out[:, :, :R].reshape(n_batch, n_heads * hd)


def _gla_decode_kernel(q_ref, k_ref, v_ref, r_ref, small_ref, w2_ref, b2_ref, ng_ref, s0_ref, y_ref, sf_ref):
    dk, dv = GLA_DK, GLA_DV
    z = jnp.dot(small_ref[:, 0:GLA_RANK].astype(BF16), w2_ref[...].astype(BF16),
                preferred_element_type=F32) + b2_ref[...]
    g_all = (jnp.minimum(z, 0.0) - jnp.log1p(jnp.exp(-jnp.abs(z)))) / GLA_GATE_NORM
    eye = lax.broadcasted_iota(jnp.int32, (dk, dk), 0) == lax.broadcasted_iota(jnp.int32, (dk, dk), 1)
    column = lambda row: jnp.sum(jnp.where(eye, jnp.broadcast_to(row, (dk, dk)), 0.0), axis=1, keepdims=True)
    for b in range(q_ref.shape[0]):
        for h in range(GLA_HEADS):
            ks = slice(h * dk, (h + 1) * dk)
            vs = slice(h * dv, (h + 1) * dv)
            s_new = (jnp.exp(column(g_all[b:b + 1, ks])) * s0_ref[b, h]
                     + column(k_ref[b:b + 1, ks]) * v_ref[b:b + 1, vs])
            sf_ref[b, h] = s_new
            o = jnp.sum(column(q_ref[b:b + 1, ks] * (dk ** -0.5)) * s_new, axis=0, keepdims=True)
            o = o * lax.rsqrt(jnp.mean(o * o, axis=-1, keepdims=True) + RMS_EPS)
            y_ref[b:b + 1, vs] = (o * ng_ref[:, vs] * jax.nn.silu(r_ref[b:b + 1, vs])).astype(BF16)


def gla_decode(p, s0, w2, b2, norm_g):
    n = p.shape[0]
    row = lambda w, off: pl.BlockSpec((n, w), lambda i: (0, off // w))
    const = lambda shape: pl.BlockSpec(shape, lambda i: (0,) * len(shape))
    return pl.pallas_call(
        _gla_decode_kernel,
        grid=(1,),
        in_specs=[row(256, COL_GLA_Q), row(256, COL_GLA_K), row(512, COL_GLA_V), row(512, COL_GLA_R),
                  row(128, COL_SMALL), const((GLA_RANK, GLA_HEADS * GLA_DK)), const((1, GLA_HEADS * GLA_DK)),
                  const((1, GLA_WIDTH)), const(s0.shape)],
        out_specs=(const((n, GLA_WIDTH)), const(s0.shape)),
        out_shape=(jax.ShapeDtypeStruct((n, GLA_WIDTH), BF16), jax.ShapeDtypeStruct(s0.shape, F32)),
        compiler_params=_params("arbitrary"),
        name="gla_decode",
    )(p, p, p, p, p, w2, b2.reshape(1, -1), norm_g.reshape(1, -1), s0)


def _pool_decode_kernel(u_ref, prev_ref, w_ref, sc_ref, y_ref, *, past_len):
    gd = POOL_GROUP_DIM
    n_prev = prev_ref.shape[1]
    for gi, w in enumerate(POOL_WINDOWS):
        cols = slice(gi * gd, (gi + 1) * gd)
        x = u_ref[:, cols]
        s = x
        for r in range(n_prev - (w - 1), n_prev):
            s = s + prev_ref[:, r, cols]
        pooled = s / float(min(past_len + 1, w)) - x
        y = jnp.dot(pooled.astype(BF16), w_ref[gi].astype(BF16), preferred_element_type=F32)
        y_ref[:, cols] = (y * sc_ref[:, cols]).astype(BF16)


def pool_decode(p, prev, past_len, w_pool, scale):
    n = p.shape[0]
    const = lambda shape: pl.BlockSpec(shape, lambda i: (0,) * len(shape))
    return pl.pallas_call(
        functools.partial(_pool_decode_kernel, past_len=past_len),
        grid=(1,),
        in_specs=[pl.BlockSpec((n, POOL_WIDTH), lambda i: (0, COL_POOL // POOL_WIDTH)), const(prev.shape),
                  const(w_pool.shape), const((1, POOL_WIDTH))],
        out_specs=const((n, POOL_WIDTH)),
        out_shape=jax.ShapeDtypeStruct((n, POOL_WIDTH), BF16),
        compiler_params=_params("arbitrary"),
        name="pool_decode",
    )(p, prev, w_pool, scale.reshape(1, -1))


def decode_mixer(xs2, w_in_packed, w_out_bf16, ln_g, ln_b, gla_w2, gla_b, gla_norm_g, cmp_pos, cmp_w1, cmp_w2,
                 pool_w, pool_scale, cache, layer, page_table, state_win, state_gla, state_pool, past_len,
                 router=None):
    n_dec = xs2.shape[0]
    hd, G = HEAD_DIM, NSA_KV_HEADS
    p = matmul(xs2, w_in_packed)
    pos = jnp.full((n_dec,), past_len, jnp.int32)
    rows, win, qn, qr, _, _, _, _, gates = nsa_prep(p, rope_tables(pos), 1, n_dec)
    cmp_kv = nsa_compress_paged(cache, layer, page_table, cmp_pos, cmp_w1, cmp_w2)
    t_k = past_len + 1
    n_sel = -(-t_k // SEL_BLOCK)
    o_cmp, sel_idx = nsa_select_decode(qn[0].transpose(1, 0, 2), cmp_kv, past_len // CMP_STRIDE - 1, n_sel, past_len)
    y_nsa = nsa_attend_decode(qr[0].transpose(1, 0, 2), o_cmp, sel_idx, gates[0].transpose(1, 0, 2), cache, layer,
                              page_table, rows.reshape(n_dec, 4 * G, hd), state_win, win.reshape(n_dec, 2 * G, hd))
    y_gla, s_gla = gla_decode(p, state_gla, gla_w2, gla_b, gla_norm_g)
    y_pool = pool_decode(p, state_pool, past_len, pool_w, pool_scale)
    pool_rows = jnp.concatenate([state_pool[:, 1:], p[:, None, COL_POOL:COL_POOL + POOL_WIDTH]], axis=1)
    x1 = outproj_ln(xs2, y_gla, y_nsa, y_pool, w_out_bf16, ln_g, ln_b, router)
    nsa_rows = rows.reshape(n_dec, 1, 4, G, hd)
    new_win = jnp.concatenate([state_win[layer, :, 1:], win.reshape(n_dec, 1, 2, G, hd)], axis=1)
    return x1, nsa_rows, new_win, s_gla, pool_rows


def prompt_mixer(x2, w_in_packed, w_out_bf16, ln_g, ln_b, gla_w2, gla_b, gla_norm_g, cmp_pos, cmp_w1, cmp_w2,
                 pool_w, pool_scale, n_batch, t_len, router=None):
    p = matmul(x2, w_in_packed)
    pos = jnp.arange(t_len, dtype=jnp.int32)
    rows, win, qn, qr, ks, vs, kw, vw, gates = nsa_prep(p, rope_tables(pos), n_batch, t_len)
    cmp_kv = nsa_compress_prompt(rows, cmp_pos, cmp_w1, cmp_w2, n_batch, t_len)
    o_cmp, selb = nsa_select(qn, cmp_kv, t_len // CMP_STRIDE - 1, t_len)
    y_nsa = nsa_attend(qr, o_cmp, selb, gates, ks, vs, kw, vw)
    s0 = jnp.zeros((n_batch, GLA_HEADS, GLA_DK, GLA_DV), F32)
    y_gla, s_gla = gla_mix(p, s0, gla_w2, gla_b, gla_norm_g, n_batch, t_len)
    prev = jnp.zeros((n_batch, POOL_MAX - 1, POOL_WIDTH), F32)
    y_pool = pool_mix(p, prev, 0, pool_w, pool_scale, n_batch, t_len)
    x1 = outproj_ln(x2, y_gla, y_nsa, y_pool, w_out_bf16, ln_g, ln_b, router)
    nsa_rows = rows.reshape(n_batch, t_len, 4, NSA_KV_HEADS, HEAD_DIM)
    n_win = min(WINDOW, t_len)
    win_rows = win.reshape(n_batch, t_len, 2, NSA_KV_HEADS, HEAD_DIM)[:, t_len - n_win:]
    pool_rows = p.reshape(n_batch, t_len, PACKED_WIDTH)[:, t_len - (POOL_MAX - 1):, COL_POOL:COL_POOL + POOL_WIDTH]
    return x1, nsa_rows, win_rows, s_gla, pool_rows


def kernel(x_prompt, x_sample, cache_nsa, page_table, state_win, state_gla, state_pool, w_in, gla_gate_w2, gla_gate_b, gla_norm_g, nsa_cmp_pos, nsa_cmp_w1, nsa_cmp_w2, pool_w, pool_scale, w_out, ln1_g, ln1_b, ln2_g, ln2_b, ffn_w_gate, ffn_w_up, ffn_w_down, moe_router, moe_w_gate, moe_w_up, moe_w_down):
    n_prompt, t_len, d = x_prompt.shape
    n_dec = x_sample.shape[0]
    xp, xs = x_prompt.reshape(n_prompt * t_len, d), x_sample.reshape(n_dec, d)
    nsa_p, nsa_s, win_p, win_s, gla_p, gla_s, pool_p, pool_s = [], [], [], [], [], [], [], []
    for l in range(DEPTH):
        i = l // 2
        router = moe_router[i] if l % 2 else None
        w_in_packed, w_out_bf16 = pack_w_in(w_in[l]), w_out[l].astype(BF16)
        lw = (w_in_packed, w_out_bf16, ln1_g[l], ln1_b[l], gla_gate_w2[l], gla_gate_b[l], gla_norm_g[l],
              nsa_cmp_pos[l], nsa_cmp_w1[l], nsa_cmp_w2[l], pool_w[l], pool_scale[l])
        xp, r_p, w_p, g_p, p_p = prompt_mixer(xp, *lw, n_prompt, t_len, router)
        xs, r_s, w_s, g_s, p_s = decode_mixer(xs, *lw, cache_nsa, l, page_table, state_win, state_gla[l],
                                              state_pool[l], PAST_LEN, router)
        if l % 2 == 0:
            wg, wu, wd = ffn_w_gate[i].astype(BF16), ffn_w_up[i].astype(BF16), ffn_w_down[i].astype(BF16)
            xp = ffn_ln(xp, wg, wu, wd, ln2_g[l], ln2_b[l])
            xs = ffn_ln(xs, wg, wu, wd, ln2_g[l], ln2_b[l])
        else:
            (xp, lg_p), (xs, lg_s) = xp, xs
            xp, xs = moe_ln([xp, xs], [lg_p, lg_s], moe_w_gate[i], moe_w_up[i], moe_w_down[i], ln2_g[l], ln2_b[l])
        nsa_p.append(r_p); nsa_s.append(r_s); win_p.append(w_p); win_s.append(w_s)
        gla_p.append(g_p); gla_s.append(g_s); pool_p.append(p_p); pool_s.append(p_s)
    return (xp.reshape(n_prompt, t_len, d), xs.reshape(x_sample.shape), jnp.stack(nsa_p), jnp.stack(nsa_s),
            jnp.stack(win_p), jnp.stack(win_s), jnp.stack(gla_p), jnp.stack(gla_s), jnp.stack(pool_p),
            jnp.stack(pool_s))
```
